```python
import jax
import jax.numpy as jnp
from jax import lax
import numpy as np

D_MODEL = 1024
BATCH = 16
SEQ = 2048
DEPTH = 1

EPS = 1e-6
N_MOD = 6
RET_HEADS = 4
RET_QK_DIM = 256
RET_V_DIM = 512
RET_QK = RET_HEADS * RET_QK_DIM
RET_V = RET_HEADS * RET_V_DIM
RET_CHUNK = 128
ROPE_BASE = 10000.0
SSM_D_INNER = 2 * D_MODEL
SSM_HEAD_DIM = 64
SSM_HEADS = SSM_D_INNER // SSM_HEAD_DIM
SSM_GROUPS = 8
SSM_STATE = 128
SSM_CONV = 4
SSM_CONV_DIM = SSM_D_INNER + 2 * SSM_GROUPS * SSM_STATE
SSM_CHUNK = 128
N_EXPERTS = 32
TOP_K = 4
D_FF = D_MODEL
SWIGLU_LIMIT = 7.0
SWIGLU_ALPHA = 1.702
MOE_BLOCK = 128
IN_WIDTHS = (RET_QK, RET_QK, RET_V, RET_V, SSM_D_INNER, SSM_CONV_DIM, SSM_HEADS, D_MODEL, D_MODEL)
D_IN_PROJ = RET_QK * 2 + RET_V * 2 + SSM_D_INNER + SSM_CONV_DIM + SSM_HEADS + 2 * D_MODEL

kernel_name = 'hybrid_retention_ssd_moe_block'


def _split_points(widths):
    pts, acc = [], 0
    for w in widths[:-1]:
        acc += w
        pts.append(acc)
    return pts


def rmsnorm(x, w):
    xf = x.astype(jnp.float32)
    y = xf * lax.rsqrt(jnp.mean(xf * xf, axis=-1, keepdims=True) + EPS)
    return (y * w.astype(jnp.float32)).astype(x.dtype)


def rope(t, pos):
    half = t.shape[-1] // 2
    inv_freq = ROPE_BASE ** (-jnp.arange(half, dtype=jnp.float32) / half)
    ang = pos[:, None] * inv_freq[None, :]
    cos = jnp.cos(ang)[None, :, None, :]
    sin = jnp.sin(ang)[None, :, None, :]
    t = t.astype(jnp.float32)
    t1, t2 = t[..., :half], t[..., half:]
    return jnp.concatenate([t1 * cos - t2 * sin, t1 * sin + t2 * cos], axis=-1)


def to_chunks(t, chunk):
    b, s = t.shape[:2]
    return jnp.moveaxis(t.reshape((b, s // chunk, chunk) + t.shape[2:]), 1, 0)


def from_chunks(t):
    n, b, c = t.shape[:3]
    return jnp.moveaxis(t, 0, 1).reshape((b, n * c) + t.shape[3:])


def retention_chunkwise(q, k, v):
    b, s, h, dk = q.shape
    dv = v.shape[-1]
    c = RET_CHUNK
    log_gamma = jnp.log(1.0 - 2.0 ** (-5.0 - jnp.arange(h, dtype=jnp.float32)))
    idx = jnp.arange(c, dtype=jnp.float32)
    rel = idx[:, None] - idx[None, :]
    causal = rel >= 0
    decay_in = jnp.where(causal[None], jnp.exp(jnp.where(causal, rel, 0.0)[None] * log_gamma[:, None, None]), 0.0)
    decay_q = jnp.exp((idx + 1.0)[:, None] * log_gamma[None, :])
    decay_k = jnp.exp((c - 1.0 - idx)[:, None] * log_gamma[None, :])
    decay_c = jnp.exp(c * log_gamma)

    def step(state, inp):
        qc, kc, vc = inp
        scores = jnp.einsum('blhd,bshd->bhls', qc, kc) * decay_in[None]
        inner = jnp.einsum('bhls,bshv->blhv', scores, vc)
        cross = jnp.einsum('blhd,bhdv->blhv', qc, state) * decay_q[None, :, :, None]
        state = state * decay_c[None, :, None, None] + jnp.einsum(
            'bshd,bshv->bhdv', kc * decay_k[None, :, :, None], vc)
        return state, inner + cross

    state0 = jnp.zeros((b, h, dk, dv), jnp.float32)
    _, out = lax.scan(step, state0, (to_chunks(q, c), to_chunks(k, c), to_chunks(v, c)))
    return from_chunks(out)


def ssd_chunked(x, dt, a, bm, cm):
    b, s, h, p = x.shape
    g, n = bm.shape[2], bm.shape[3]
    j = h // g
    c = SSM_CHUNK
    xdt = to_chunks((x * dt[..., None]).reshape(b, s, g, j, p), c)
    adt = to_chunks((dt * a).reshape(b, s, g, j), c)
    causal = jnp.tril(jnp.ones((c, c), dtype=bool))[None, :, :, None, None]

    def step(state, inp):
        xc, ac, bc, cc = inp
        acs = jnp.cumsum(ac, axis=1)
        seg = jnp.exp(jnp.where(causal, acs[:, :, None] - acs[:, None, :], -jnp.inf))
        cb = jnp.einsum('blgn,bsgn->blsg', cc, bc)
        y_diag = jnp.einsum('blsgj,bsgjp->blgjp', cb[..., None] * seg, xc)
        y_off = jnp.einsum('blgn,bgjpn->blgjp', cc, state) * jnp.exp(acs)[..., None]
        decay_s = jnp.exp(acs[:, -1:] - acs)
        state = state * jnp.exp(acs[:, -1])[..., None, None] + jnp.einsum(
            'bsgn,bsgj,bsgjp->bgjpn', bc, decay_s, xc)
        return state, y_diag + y_off

    state0 = jnp.zeros((b, g, j, p, n), jnp.float32)
    _, y = lax.scan(step, state0, (xdt, adt, to_chunks(bm, c), to_chunks(cm, c)))
    return from_chunks(y).reshape(b, s, h, p)


def causal_depthwise_conv(u, w, bias):
    k = w.shape[0]
    out = lax.conv_general_dilated(
        u, w[:, None, :].astype(u.dtype), window_strides=(1,), padding=[(k - 1, 0)],
        dimension_numbers=('NWC', 'WIO', 'NWC'), feature_group_count=u.shape[-1])
    return out + bias


def gated_group_rmsnorm(y, z, w):
    b, s, d = y.shape
    yz = y.astype(jnp.float32) * jax.nn.silu(z.astype(jnp.float32))
    grp = yz.reshape(b, s, SSM_GROUPS, d // SSM_GROUPS)
    grp = grp * lax.rsqrt(jnp.mean(grp * grp, axis=-1, keepdims=True) + EPS)
    return (grp.reshape(b, s, d) * w.astype(jnp.float32)).astype(z.dtype)


def hybrid_mixer(h, w_in, conv_w, conv_b, dt_bias, a_log, d_skip, ssm_norm,
                 w_ret_out, w_ssm_out, w_out):
    b, s, _ = h.shape
    proj = h @ w_in
    q, k, v, g, z, xbc, dt, gate_a, gate_b = jnp.split(proj, _split_points(IN_WIDTHS), axis=-1)

    pos = jnp.arange(s, dtype=jnp.float32)
    q = rope(q.reshape(b, s, RET_HEADS, RET_QK_DIM), pos)
    k = rope(k.reshape(b, s, RET_HEADS, RET_QK_DIM), pos) * (RET_QK_DIM ** -0.5)
    v = v.reshape(b, s, RET_HEADS, RET_V_DIM).astype(jnp.float32)
    ret = retention_chunkwise(q, k, v)
    ret = ret * lax.rsqrt(jnp.mean(ret * ret, axis=-1, keepdims=True) + EPS)
    ret = ret.reshape(b, s, RET_V).astype(h.dtype) * jax.nn.silu(g)
    y_a = ret @ w_ret_out

    xbc = jax.nn.silu(causal_depthwise_conv(xbc, conv_w, conv_b))
    xs, bm, cm = jnp.split(xbc, [SSM_D_INNER, SSM_D_INNER + SSM_GROUPS * SSM_STATE], axis=-1)
    dt = jax.nn.softplus(dt.astype(jnp.float32) + dt_bias.astype(jnp.float32))
    a = -jnp.exp(a_log.astype(jnp.float32))
    xs = xs.reshape(b, s, SSM_HEADS, SSM_HEAD_DIM).astype(jnp.float32)
    bm = bm.reshape(b, s, SSM_GROUPS, SSM_STATE).astype(jnp.float32)
    cm = cm.reshape(b, s, SSM_GROUPS, SSM_STATE).astype(jnp.float32)
    y = ssd_chunked(xs, dt, a, bm, cm) + d_skip.astype(jnp.float32)[:, None] * xs
    y = gated_group_rmsnorm(y.reshape(b, s, SSM_D_INNER), z, ssm_norm)
    y_b = y @ w_ssm_out

    merged = jax.nn.sigmoid(gate_a) * y_a + jax.nn.sigmoid(gate_b) * y_b
    return merged @ w_out


def moe_ffn(h, w_router, b_router, w_gate_up, b_gate_up, w_down, b_down):
    b, s, d = h.shape
    t = b * s
    tk = t * TOP_K
    hf = h.reshape(t, d)
    logits = (hf @ w_router).astype(jnp.float32) + b_router.astype(jnp.float32)
    top_vals, top_idx = lax.top_k(logits, TOP_K)
    probs = jax.nn.softmax(top_vals, axis=-1)

    e_flat = top_idx.reshape(-1).astype(jnp.int32)
    w_flat = probs.reshape(-1)
    tok_flat = jnp.arange(tk, dtype=jnp.int32) // TOP_K
    order = jnp.argsort(e_flat)
    sorted_e, sorted_tok, sorted_w = e_flat[order], tok_flat[order], w_flat[order]
    counts = jnp.bincount(e_flat, length=N_EXPERTS).astype(jnp.int32)
    padded = ((counts + MOE_BLOCK - 1) // MOE_BLOCK) * MOE_BLOCK
    start_sorted = jnp.cumsum(counts) - counts
    pad_end = jnp.cumsum(padded)
    start_pad = pad_end - padded
    dest = start_pad[sorted_e] + jnp.arange(tk, dtype=jnp.int32) - start_sorted[sorted_e]
    n_blocks = (tk + MOE_BLOCK - 1) // MOE_BLOCK + N_EXPERTS
    n_rows = n_blocks * MOE_BLOCK
    tok_buf = jnp.full((n_rows,), t, jnp.int32).at[dest].set(sorted_tok)
    w_buf = jnp.zeros((n_rows,), jnp.float32).at[dest].set(sorted_w)
    block_start = jnp.arange(n_blocks, dtype=jnp.int32) * MOE_BLOCK
    block_expert = jnp.minimum(jnp.searchsorted(pad_end, block_start, side='right'),
                               N_EXPERTS - 1).astype(jnp.int32)
    x_pad = jnp.concatenate([hf, jnp.zeros((1, d), hf.dtype)], axis=0)

    def step(acc, inp):
        tok, wt, e = inp
        xb = x_pad[tok]
        gu = xb @ w_gate_up[e] + b_gate_up[e]
        gate = jnp.minimum(gu[:, :D_FF], SWIGLU_LIMIT)
        up = jnp.clip(gu[:, D_FF:], -SWIGLU_LIMIT, SWIGLU_LIMIT)
        act = gate * jax.nn.sigmoid(SWIGLU_ALPHA * gate) * (up + 1.0)
        yb = act @ w_down[e] + b_down[e]
        acc = acc.at[tok].add(yb.astype(jnp.float32) * wt[:, None])
        return acc, None

    acc0 = jnp.zeros((t + 1, d), jnp.float32)
    acc, _ = lax.scan(step, acc0, (tok_buf.reshape(n_blocks, MOE_BLOCK),
                                   w_buf.reshape(n_blocks, MOE_BLOCK), block_expert))
    return acc[:t].reshape(b, s, d).astype(h.dtype)


def setup_inputs(seed: int = 0) -> dict:
    key = jax.random.key(seed)
    ks = jax.random.split(key, 24)
    f32 = jnp.float32
    L = DEPTH

    def nrm(k, shape, scale):
        return jax.random.normal(k, shape, f32) * scale

    dt0 = jnp.exp(jax.random.uniform(ks[9], (L, SSM_HEADS), f32) * (jnp.log(0.1) - jnp.log(0.001)) + jnp.log(0.001))
    return {
        'x': nrm(ks[0], (BATCH, SEQ, D_MODEL), 1.0),
        'c': nrm(ks[1], (BATCH, D_MODEL), 1.0),
        'w_ada': nrm(ks[2], (L, D_MODEL, N_MOD * D_MODEL), 0.5 * D_MODEL ** -0.5),
        'b_ada': nrm(ks[3], (L, N_MOD * D_MODEL), 0.02),
        'norm_mix': 1.0 + nrm(ks[4], (L, D_MODEL), 0.02),
        'norm_ffn': 1.0 + nrm(ks[5], (L, D_MODEL), 0.02),
        'w_in': nrm(ks[6], (L, D_MODEL, D_IN_PROJ), D_MODEL ** -0.5),
        'conv_w': nrm(ks[7], (L, SSM_CONV, SSM_CONV_DIM), SSM_CONV ** -0.5),
        'conv_b': nrm(ks[8], (L, SSM_CONV_DIM), 0.02),
        'dt_bias': dt0 + jnp.log(-jnp.expm1(-dt0)),
        'a_log': jnp.log(jax.random.uniform(ks[10], (L, SSM_HEADS), f32, 1.0, 16.0)),
        'd_skip': 1.0 + nrm(ks[11], (L, SSM_HEADS), 0.02),
        'ssm_norm': 1.0 + nrm(ks[12], (L, SSM_D_INNER), 0.02),
        'w_ret_out': nrm(ks[13], (L, RET_V, D_MODEL), RET_V ** -0.5),
        'w_ssm_out': nrm(ks[14], (L, SSM_D_INNER, D_MODEL), SSM_D_INNER ** -0.5),
        'w_out': nrm(ks[15], (L, D_MODEL, D_MODEL), D_MODEL ** -0.5),
        'w_router': nrm(ks[16], (L, D_MODEL, N_EXPERTS), D_MODEL ** -0.5),
        'b_router': nrm(ks[17], (L, N_EXPERTS), 0.01),
        'w_gate_up': nrm(ks[18], (L, N_EXPERTS, D_MODEL, 2 * D_FF), D_MODEL ** -0.5),
        'b_gate_up': nrm(ks[19], (L, N_EXPERTS, 2 * D_FF), 0.02),
        'w_down': nrm(ks[20], (L, N_EXPERTS, D_FF, D_MODEL), D_FF ** -0.5),
        'b_down': nrm(ks[21], (L, N_EXPERTS, D_MODEL), 0.02),
        'norm_final': 1.0 + nrm(ks[22], (D_MODEL,), 0.02),
    }


def reference(x, c, w_ada, b_ada, norm_mix, norm_ffn, w_in, conv_w, conv_b, dt_bias,
              a_log, d_skip, ssm_norm, w_ret_out, w_ssm_out, w_out, w_router, b_router,
              w_gate_up, b_gate_up, w_down, b_down, norm_final):
    cond = jax.nn.silu(c)
    for l in range(DEPTH):
        mod = (cond @ w_ada[l] + b_ada[l])[:, None, :]
        sh_m, sc_m, g_m, sh_f, sc_f, g_f = jnp.split(mod, N_MOD, axis=-1)
        h = rmsnorm(x, norm_mix[l]) * (1.0 + sc_m) + sh_m
        x = x + g_m * hybrid_mixer(h, w_in[l], conv_w[l], conv_b[l], dt_bias[l], a_log[l],
                                   d_skip[l], ssm_norm[l], w_ret_out[l], w_ssm_out[l], w_out[l])
        h = rmsnorm(x, norm_ffn[l]) * (1.0 + sc_f) + sh_f
        x = x + g_f * moe_ffn(h, w_router[l], b_router[l], w_gate_up[l], b_gate_up[l],
                              w_down[l], b_down[l])
    return rmsnorm(x, norm_final)
```

```python
import functools
import math

import numpy as np
import jax
import jax.numpy as jnp
from jax import lax
from jax.experimental import pallas as pl
from jax.experimental.pallas import tpu as pltpu

F32 = jnp.float32
BF16 = jnp.bfloat16
HIGHEST = lax.Precision.HIGHEST

EPS = 1e-6
N_MOD = 6
RET_HEADS = 4
RET_QK_DIM = 256
RET_V_DIM = 512
ROPE_BASE = 10000.0
SSM_HEAD_DIM = 64
SSM_GROUPS = 8
SSM_STATE = 128
SSM_CONV = 4
N_EXPERTS = 32
TOP_K = 4
SWIGLU_LIMIT = 7.0
SWIGLU_ALPHA = 1.702

LANES = 128
SUBLANES = 8
VMEM_LIMIT = 56 * 1024 * 1024

RET_CHUNK = 128
SSM_CHUNK = 128
FFN_BLOCK = 256


def _params(sem, vmem=VMEM_LIMIT):
    return pltpu.CompilerParams(dimension_semantics=sem, vmem_limit_bytes=vmem)


def _nt_dot(a, b, **kw):
    return lax.dot_general(a, b, (((1,), (1,)), ((), ())), preferred_element_type=F32, **kw)


def _tn_dot(a, b, **kw):
    return lax.dot_general(a, b, (((0,), (0,)), ((), ())), preferred_element_type=F32, **kw)


def _silu(v):
    return v * jax.nn.sigmoid(v)


def _mod_kernel(c_ref, w_ref, b_ref, o_ref):
    cond = _silu(c_ref[...])
    o_ref[...] = jnp.dot(cond, w_ref[...], preferred_element_type=F32, precision=HIGHEST) + b_ref[...]


def _mod_call(c, w_ada, b_ada):
    bsz, d = c.shape
    n = w_ada.shape[1]
    return pl.pallas_call(
        _mod_kernel,
        out_shape=jax.ShapeDtypeStruct((bsz, n), F32),
        grid=(n // d,),
        in_specs=[pl.BlockSpec((bsz, d), lambda j: (0, 0)),
                  pl.BlockSpec((d, d), lambda j: (0, j)),
                  pl.BlockSpec((1, d), lambda j: (0, j))],
        out_specs=pl.BlockSpec((bsz, d), lambda j: (0, j)),
        compiler_params=_params(("arbitrary",)),
        name="mod",
    )(c, w_ada, b_ada.reshape(1, n))


def _inproj_kernel(x_ref, nw_ref, sc_ref, sh_ref, cos_ref, sin_ref, w_ref, wdt_ref,
                   o_ref, dt_ref, h_s, *, n_rope):
    j = pl.program_id(1)

    @pl.when(j == 0)
    def _():
        xf = x_ref[...]
        ms = jnp.mean(xf * xf, axis=-1, keepdims=True)
        y = xf * lax.rsqrt(ms + EPS) * nw_ref[...]
        hm = y * (1.0 + sc_ref[0]) + sh_ref[0]
        hb = hm.astype(BF16)
        h_s[...] = hb
        dt_ref[...] = jnp.dot(hm, wdt_ref[...], preferred_element_type=F32, precision=HIGHEST)
        acc = jnp.dot(hb, w_ref[...], preferred_element_type=F32)
        cos = cos_ref[...]
        sin = sin_ref[...]
        half = RET_QK_DIM // 2
        for c in range(n_rope):
            a = acc[:, c * RET_QK_DIM: c * RET_QK_DIM + half]
            b = acc[:, c * RET_QK_DIM + half: (c + 1) * RET_QK_DIM]
            scale = 1.0 if c < RET_HEADS else RET_QK_DIM ** -0.5
            o_ref[:, c * RET_QK_DIM: c * RET_QK_DIM + half] = ((a * cos - b * sin) * scale).astype(BF16)
            o_ref[:, c * RET_QK_DIM + half: (c + 1) * RET_QK_DIM] = ((a * sin + b * cos) * scale).astype(BF16)

    @pl.when(j != 0)
    def _():
        o_ref[...] = jnp.dot(h_s[...], w_ref[...], preferred_element_type=F32).astype(BF16)


def _inproj_call(x2, norm_w, mod3, cos, sin, w_main, w_dt, seq, tm, tn):
    t, d = x2.shape
    n = w_main.shape[1]
    tiles_per_seq = seq // tm
    assert tn == 2 * RET_HEADS * RET_QK_DIM, "rotary epilogue expects q and k in the first column tile"
    kern = functools.partial(_inproj_kernel, n_rope=2 * RET_HEADS)
    return pl.pallas_call(
        kern,
        out_shape=(jax.ShapeDtypeStruct((t, n), BF16), jax.ShapeDtypeStruct((t, LANES), F32)),
        grid=(t // tm, n // tn),
        in_specs=[
            pl.BlockSpec((tm, d), lambda i, j: (i, 0)),
            pl.BlockSpec((1, d), lambda i, j: (0, 0)),
            pl.BlockSpec((1, 1, d), lambda i, j: ((i // tiles_per_seq) * N_MOD + 1, 0, 0)),
            pl.BlockSpec((1, 1, d), lambda i, j: ((i // tiles_per_seq) * N_MOD + 0, 0, 0)),
            pl.BlockSpec((tm, LANES), lambda i, j: (i % tiles_per_seq, 0)),
            pl.BlockSpec((tm, LANES), lambda i, j: (i % tiles_per_seq, 0)),
            pl.BlockSpec((d, tn), lambda i, j: (0, j)),
            pl.BlockSpec((d, LANES), lambda i, j: (0, 0)),
        ],
        out_specs=(pl.BlockSpec((tm, tn), lambda i, j: (i, j)),
                   pl.BlockSpec((tm, LANES), lambda i, j: (i, 0))),
        scratch_shapes=[pltpu.VMEM((tm, d), BF16)],
        compiler_params=_params(("arbitrary", "arbitrary")),
        name="inproj",
    )(x2, norm_w, mod3, mod3, cos, sin, w_main, w_dt)


def _retention_kernel(q_ref, k_ref, v_ref, g_ref, din_ref, dq_ref, dk_ref, w_ref, o_ref, state,
                      *, decay_c):
    c = pl.program_id(1)

    @pl.when(c == 0)
    def _():
        state[...] = jnp.zeros_like(state)

    acc = None
    for h in range(RET_HEADS):
        qh = q_ref[:, h * RET_QK_DIM:(h + 1) * RET_QK_DIM]
        kh = k_ref[:, h * RET_QK_DIM:(h + 1) * RET_QK_DIM]
        vh = v_ref[:, h * RET_V_DIM:(h + 1) * RET_V_DIM]
        scores = _nt_dot(qh, kh) * din_ref[h]
        inner = jnp.dot(scores.astype(BF16), vh, preferred_element_type=F32)
        st = state[h]
        cross = jnp.dot(qh, st.astype(BF16), preferred_element_type=F32) * dq_ref[h]
        kd = (kh.astype(F32) * dk_ref[h]).astype(BF16)
        state[h] = st * decay_c[h] + _tn_dot(kd, vh)
        ret = inner + cross
        ret = ret * lax.rsqrt(jnp.mean(ret * ret, axis=-1, keepdims=True) + EPS)
        gh = g_ref[:, h * RET_V_DIM:(h + 1) * RET_V_DIM].astype(F32)
        ret = ret * _silu(gh)
        part = jnp.dot(ret.astype(BF16), w_ref[h * RET_V_DIM:(h + 1) * RET_V_DIM, :],
                       preferred_element_type=F32)
        acc = part if acc is None else acc + part
    o_ref[...] = acc


def _retention_tables(chunk):
    lg = np.log(1.0 - 2.0 ** (-5.0 - np.arange(RET_HEADS, dtype=np.float64)))
    idx = np.arange(chunk, dtype=np.float64)
    rel = idx[:, None] - idx[None, :]
    causal = rel >= 0
    din = np.where(causal[None], np.exp(np.where(causal, rel, 0.0)[None] * lg[:, None, None]), 0.0)
    dq = np.exp((idx + 1.0)[None, :, None] * lg[:, None, None])
    dk = np.exp((chunk - 1.0 - idx)[None, :, None] * lg[:, None, None])
    dc = tuple(float(v) for v in np.exp(chunk * lg))
    return (jnp.asarray(din, F32), jnp.asarray(dq, F32), jnp.asarray(dk, F32), dc)


def _retention_call(proj, w_ret, bsz, seq, chunk):
    t = proj.shape[0]
    d = w_ret.shape[1]
    nc = seq // chunk
    qk_w = RET_HEADS * RET_QK_DIM
    v_w = RET_HEADS * RET_V_DIM
    din, dq, dk, dc = _retention_tables(chunk)
    kern = functools.partial(_retention_kernel, decay_c=dc)
    row = lambda b, c: b * nc + c
    return pl.pallas_call(
        kern,
        out_shape=jax.ShapeDtypeStruct((t, d), F32),
        grid=(bsz, nc),
        in_specs=[
            pl.BlockSpec((chunk, qk_w), lambda b, c: (row(b, c), 0)),
            pl.BlockSpec((chunk, qk_w), lambda b, c: (row(b, c), 1)),
            pl.BlockSpec((chunk, v_w), lambda b, c: (row(b, c), 1)),
            pl.BlockSpec((chunk, v_w), lambda b, c: (row(b, c), 2)),
            pl.BlockSpec((RET_HEADS, chunk, chunk), lambda b, c: (0, 0, 0)),
            pl.BlockSpec((RET_HEADS, chunk, 1), lambda b, c: (0, 0, 0)),
            pl.BlockSpec((RET_HEADS, chunk, 1), lambda b, c: (0, 0, 0)),
            pl.BlockSpec((v_w, d), lambda b, c: (0, 0)),
        ],
        out_specs=pl.BlockSpec((chunk, d), lambda b, c: (row(b, c), 0)),
        scratch_shapes=[pltpu.VMEM((RET_HEADS, RET_QK_DIM, RET_V_DIM), F32)],
        compiler_params=_params(("arbitrary", "arbitrary")),
        name="retention",
    )(proj, proj, proj, proj, din, dq, dk, w_ret)


def _ssd_kernel(z_ref, xbc_ref, dt_ref, cw_ref, cb_ref, dtb_ref, alog_ref, dsk_ref, nw_ref,
                tril_ref, w_ref, o_ref, xpad, state, *, chunk, d_inner):
    c = pl.program_id(1)
    heads_per_group = d_inner // SSM_HEAD_DIM // SSM_GROUPS
    gw = heads_per_group * SSM_HEAD_DIM
    pad = SUBLANES

    @pl.when(c == 0)
    def _():
        xpad[0:pad, :] = jnp.zeros((pad, xpad.shape[1]), F32)
        state[...] = jnp.zeros_like(state)

    xpad[pad:pad + chunk, :] = xbc_ref[...].astype(F32)
    conv = cb_ref[...] + cw_ref[SSM_CONV - 1:SSM_CONV, :] * xpad[pad:pad + chunk, :]
    for k in range(SSM_CONV - 1):
        shift = SSM_CONV - 1 - k
        conv = conv + cw_ref[k:k + 1, :] * xpad[pad - shift:pad - shift + chunk, :]
    xpad[0:pad, :] = xpad[chunk:chunk + pad, :]
    act = _silu(conv)

    dt = jax.nn.softplus(dt_ref[...] + dtb_ref[...])
    a = -jnp.exp(alog_ref[...])
    adt = dt * a
    acs = jnp.dot(tril_ref[...], adt, preferred_element_type=F32, precision=HIGHEST)
    acs_t = acs.T
    acs_last = acs[chunk - 1:chunk, :]
    e_acs = jnp.exp(acs)
    e_last = jnp.exp(acs_last)
    decay_s = jnp.exp(acs_last - acs)
    li = lax.broadcasted_iota(jnp.int32, (chunk, chunk), 0)
    si = lax.broadcasted_iota(jnp.int32, (chunk, chunk), 1)
    causal = li >= si

    b_off = d_inner
    c_off = d_inner + SSM_GROUPS * SSM_STATE
    acc = None
    for g in range(SSM_GROUPS):
        bm = act[:, b_off + g * SSM_STATE: b_off + (g + 1) * SSM_STATE].astype(BF16)
        cm = act[:, c_off + g * SSM_STATE: c_off + (g + 1) * SSM_STATE].astype(BF16)
        xs_g = act[:, g * gw:(g + 1) * gw]
        cb = _nt_dot(cm, bm)
        yd, xdt, ea_x, ds_x, el_x = [], [], [], [], []
        for jh in range(heads_per_group):
            h = g * heads_per_group + jh
            seg = jnp.exp(jnp.where(causal, acs[:, h:h + 1] - acs_t[h:h + 1, :], -jnp.inf))
            xdt_h = xs_g[:, jh * SSM_HEAD_DIM:(jh + 1) * SSM_HEAD_DIM] * dt[:, h:h + 1]
            yd.append(jnp.dot((cb * seg).astype(BF16), xdt_h.astype(BF16), preferred_element_type=F32))
            xdt.append(xdt_h)
            ea_x.append(jnp.broadcast_to(e_acs[:, h:h + 1], (chunk, SSM_HEAD_DIM)))
            ds_x.append(jnp.broadcast_to(decay_s[:, h:h + 1], (chunk, SSM_HEAD_DIM)))
            el_x.append(jnp.broadcast_to(e_last[:, h:h + 1], (1, SSM_HEAD_DIM)))
        y_diag = jnp.concatenate(yd, axis=-1)
        xdt_g = jnp.concatenate(xdt, axis=-1)
        st = state[g]
        y_off = jnp.dot(cm, st.astype(BF16), preferred_element_type=F32) * jnp.concatenate(ea_x, axis=-1)
        xdec = (xdt_g * jnp.concatenate(ds_x, axis=-1)).astype(BF16)
        state[g] = st * jnp.concatenate(el_x, axis=-1) + _tn_dot(bm, xdec)
        y = y_diag + y_off + dsk_ref[:, g * gw:(g + 1) * gw] * xs_g
        yz = y * _silu(z_ref[:, g * gw:(g + 1) * gw].astype(F32))
        yn = yz * lax.rsqrt(jnp.mean(yz * yz, axis=-1, keepdims=True) + EPS) * nw_ref[:, g * gw:(g + 1) * gw]
        part = jnp.dot(yn.astype(BF16), w_ref[g * gw:(g + 1) * gw, :], preferred_element_type=F32)
        acc = part if acc is None else acc + part
    o_ref[...] = acc


def _ssd_call(proj, dt_raw, conv_w, conv_b, dt_bias, a_log, d_skip, ssm_norm, w_ssm, bsz, seq, chunk):
    t = proj.shape[0]
    d_inner, d = w_ssm.shape
    conv_dim = conv_w.shape[1]
    n_heads = d_inner // SSM_HEAD_DIM
    nc = seq // chunk
    gw = d_inner // SSM_GROUPS
    pad_h = lambda v: jnp.pad(v.astype(F32), (0, LANES - n_heads)).reshape(1, LANES)
    tril = jnp.asarray(np.tril(np.ones((chunk, chunk), np.float32)))
    kern = functools.partial(_ssd_kernel, chunk=chunk, d_inner=d_inner)
    row = lambda b, c: b * nc + c
    z_blk = (2 * RET_HEADS * RET_QK_DIM + 2 * RET_HEADS * RET_V_DIM) // d_inner
    xbc_blk = (2 * RET_HEADS * RET_QK_DIM + 2 * RET_HEADS * RET_V_DIM + d_inner) // conv_dim
    full = lambda shape: pl.BlockSpec(shape, lambda b, c: (0,) * len(shape))
    return pl.pallas_call(
        kern,
        out_shape=jax.ShapeDtypeStruct((t, d), F32),
        grid=(bsz, nc),
        in_specs=[
            pl.BlockSpec((chunk, d_inner), lambda b, c: (row(b, c), z_blk)),
            pl.BlockSpec((chunk, conv_dim), lambda b, c: (row(b, c), xbc_blk)),
            pl.BlockSpec((chunk, LANES), lambda b, c: (row(b, c), 0)),
            full((SSM_CONV, conv_dim)), full((1, conv_dim)), full((1, LANES)), full((1, LANES)),
            full((1, d_inner)), full((1, d_inner)), full((chunk, chunk)), full((d_inner, d)),
        ],
        out_specs=pl.BlockSpec((chunk, d), lambda b, c: (row(b, c), 0)),
        scratch_shapes=[pltpu.VMEM((chunk + SUBLANES, conv_dim), F32),
                        pltpu.VMEM((SSM_GROUPS, SSM_STATE, gw), F32)],
        compiler_params=_params(("arbitrary", "arbitrary")),
        name="ssd",
    )(proj, proj, dt_raw, conv_w, conv_b.reshape(1, conv_dim), pad_h(dt_bias), pad_h(a_log),
      jnp.repeat(d_skip.astype(F32), SSM_HEAD_DIM).reshape(1, d_inner), ssm_norm.reshape(1, d_inner),
      tril, w_ssm)


def _merge_kernel(ya_ref, yb_ref, ga_ref, gb_ref, x_ref, gm_ref, scf_ref, shf_ref, nw_ref, wo_ref,
                  wr_ref, br_ref, tri_ref,
                  x1_ref, h2_ref, idx_ref, rank_ref, prow_ref, cnt_ref, cnt_s, *, tm):
    i = pl.program_id(0)

    @pl.when(i == 0)
    def _():
        cnt_s[...] = jnp.zeros_like(cnt_s)

    merged = (jax.nn.sigmoid(ga_ref[...].astype(F32)) * ya_ref[...]
              + jax.nn.sigmoid(gb_ref[...].astype(F32)) * yb_ref[...])
    mo = jnp.dot(merged.astype(BF16), wo_ref[...], preferred_element_type=F32)
    x1 = x_ref[...] + gm_ref[0] * mo
    x1_ref[...] = x1
    ms = jnp.mean(x1 * x1, axis=-1, keepdims=True)
    h2 = x1 * lax.rsqrt(ms + EPS) * nw_ref[...] * (1.0 + scf_ref[0]) + shf_ref[0]
    h2_ref[...] = h2

    lg = _nt_dot(wr_ref[...], h2, precision=HIGHEST) + br_ref[...]
    sub = lax.broadcasted_iota(jnp.int32, lg.shape, 0)
    work = lg
    vals, idxs, sels = [], [], []
    for _ in range(TOP_K):
        m = jnp.max(work, axis=0, keepdims=True)
        ik = jnp.min(jnp.where(work == m, sub, N_EXPERTS), axis=0, keepdims=True)
        sel = sub == ik
        vals.append(m)
        idxs.append(ik)
        sels.append(sel)
        work = jnp.where(sel, -jnp.inf, work)
    exps = [jnp.exp(v - vals[0]) for v in vals]
    denom = exps[0]
    for e in exps[1:]:
        denom = denom + e
    probs = [e / denom for e in exps]

    base = cnt_s[:, 0:1]
    ranks = []
    for k in range(TOP_K):
        mk = jnp.where(sels[k], 1.0, 0.0)
        pre = jnp.dot(mk.astype(BF16), tri_ref[...], preferred_element_type=F32)
        ranks.append(jnp.sum(jnp.where(sels[k], pre + base, 0.0), axis=0, keepdims=True))
        base = base + jnp.sum(mk, axis=1, keepdims=True)
    cnt_s[...] = jnp.broadcast_to(base, cnt_s.shape)
    cnt_ref[...] = cnt_s[...].astype(jnp.int32)

    zi = jnp.zeros((SUBLANES - TOP_K, tm), jnp.int32)
    idx_ref[0] = jnp.concatenate(idxs + [zi], axis=0)
    rank_ref[0] = jnp.concatenate([r.astype(jnp.int32) for r in ranks] + [zi], axis=0)
    pt = jnp.concatenate(probs + [jnp.zeros((LANES - TOP_K, tm), F32)], axis=0)
    prow_ref[...] = pt.T


def _merge_call(ya, yb, proj, x2, mod3, norm_w, w_out, w_router_t, b_router, seq, tm):
    t, d = x2.shape
    nt = t // tm
    tiles_per_seq = seq // tm
    ga_blk = proj.shape[1] // d - 2
    tri = jnp.asarray(np.triu(np.ones((tm, tm), np.float32), 1), BF16)
    kern = functools.partial(_merge_kernel, tm=tm)
    modspec = lambda m: pl.BlockSpec((1, 1, d), lambda i: ((i // tiles_per_seq) * N_MOD + m, 0, 0))
    return pl.pallas_call(
        kern,
        out_shape=(jax.ShapeDtypeStruct((t, d), F32), jax.ShapeDtypeStruct((t, d), F32),
                   jax.ShapeDtypeStruct((nt, SUBLANES, tm), jnp.int32),
                   jax.ShapeDtypeStruct((nt, SUBLANES, tm), jnp.int32),
                   jax.ShapeDtypeStruct((t, LANES), F32),
                   jax.ShapeDtypeStruct((N_EXPERTS, LANES), jnp.int32)),
        grid=(nt,),
        in_specs=[
            pl.BlockSpec((tm, d), lambda i: (i, 0)),
            pl.BlockSpec((tm, d), lambda i: (i, 0)),
            pl.BlockSpec((tm, d), lambda i: (i, ga_blk)),
            pl.BlockSpec((tm, d), lambda i: (i, ga_blk + 1)),
            pl.BlockSpec((tm, d), lambda i: (i, 0)),
            modspec(2), modspec(4), modspec(3),
            pl.BlockSpec((1, d), lambda i: (0, 0)),
            pl.BlockSpec((d, d), lambda i: (0, 0)),
            pl.BlockSpec((N_EXPERTS, d), lambda i: (0, 0)),
            pl.BlockSpec((N_EXPERTS, 1), lambda i: (0, 0)),
            pl.BlockSpec((tm, tm), lambda i: (0, 0)),
        ],
        out_specs=(pl.BlockSpec((tm, d), lambda i: (i, 0)),
                   pl.BlockSpec((tm, d), lambda i: (i, 0)),
                   pl.BlockSpec((1, SUBLANES, tm), lambda i: (i, 0, 0)),
                   pl.BlockSpec((1, SUBLANES, tm), lambda i: (i, 0, 0)),
                   pl.BlockSpec((tm, LANES), lambda i: (i, 0)),
                   pl.BlockSpec((N_EXPERTS, LANES), lambda i: (0, 0))),
        scratch_shapes=[pltpu.VMEM((N_EXPERTS, LANES), F32)],
        compiler_params=_params(("arbitrary",)),
        name="merge",
    )(ya, yb, proj, proj, x2, mod3, mod3, mod3, norm_w, w_out, w_router_t,
      b_router.reshape(N_EXPERTS, 1), tri)


def _row_copy(src, src_row, dst, dst_row, sem):
    return pltpu.make_async_copy(
        src.at[pl.ds(pl.multiple_of(src_row * SUBLANES, SUBLANES), SUBLANES), :],
        dst.at[pl.ds(pl.multiple_of(dst_row * SUBLANES, SUBLANES), SUBLANES), :], sem)


def _dispatch_kernel(zs_ref, h2_ref, dest_ref, xs_ref, slab, zbuf, dsm, sem_rows, sem_s, sem_z,
                     *, tm, nt, bm, n_blocks):
    i = pl.program_id(0)
    slot = i % 2
    rows_per_tile = TOP_K * tm

    def wait_rows(s):
        pltpu.make_async_copy(xs_ref.at[pl.ds(0, rows_per_tile * SUBLANES), :],
                              xs_ref.at[pl.ds(0, rows_per_tile * SUBLANES), :], sem_rows.at[s]).wait()

    @pl.when(i == 0)
    def _():
        zbuf[...] = jnp.zeros_like(zbuf)
        for e in range(N_EXPERTS):
            fill = pltpu.make_async_copy(
                zbuf, xs_ref.at[pl.ds(pl.multiple_of(zs_ref[e] * SUBLANES, SUBLANES), bm * SUBLANES), :],
                sem_z)
            fill.start()
            fill.wait()

        def fill_tail(b, carry):
            fill = pltpu.make_async_copy(
                zbuf, xs_ref.at[pl.ds(pl.multiple_of(b * (bm * SUBLANES), SUBLANES), bm * SUBLANES), :], sem_z)
            fill.start()
            fill.wait()
            return carry

        lax.fori_loop(zs_ref[N_EXPERTS], n_blocks + 1, fill_tail, 0)

    @pl.when(i >= 2)
    def _():
        wait_rows(slot)

    cp = pltpu.make_async_copy(dest_ref.at[0], dsm, sem_s)
    cp.start()
    base = pl.multiple_of(slot * (tm * SUBLANES), SUBLANES)
    for s in range(SUBLANES):
        piece = h2_ref[:, s * LANES:(s + 1) * LANES]
        slab[pl.ds(base + s, tm, stride=SUBLANES), :] = piece
    cp.wait()

    def issue(r, carry):
        for k in range(TOP_K):
            _row_copy(slab, slot * tm + r, xs_ref, dsm[k, r], sem_rows.at[slot]).start()
        return carry

    lax.fori_loop(0, tm, issue, 0)

    @pl.when(i == nt - 1)
    def _():
        wait_rows(slot)
        if nt >= 2:
            wait_rows(1 - slot)


def _dispatch_call(h2, dest, zero_start, n_blocks, tm, bm):
    t, d = h2.shape
    nt = t // tm
    n_rows = (n_blocks + 1) * bm
    assert d == SUBLANES * LANES
    kern = functools.partial(_dispatch_kernel, tm=tm, nt=nt, bm=bm, n_blocks=n_blocks)
    gs = pltpu.PrefetchScalarGridSpec(
        num_scalar_prefetch=1,
        grid=(nt,),
        in_specs=[pl.BlockSpec((tm, d), lambda i, zs: (i, 0)),
                  pl.BlockSpec((1, SUBLANES, tm), lambda i, zs: (i, 0, 0))],
        out_specs=pl.BlockSpec(memory_space=pl.ANY),
        scratch_shapes=[pltpu.VMEM((2 * tm * SUBLANES, LANES), F32),
                        pltpu.VMEM((bm * SUBLANES, LANES), F32),
                        pltpu.SMEM((SUBLANES, tm), jnp.int32),
                        pltpu.SemaphoreType.DMA((2,)),
                        pltpu.SemaphoreType.DMA,
                        pltpu.SemaphoreType.DMA],
    )
    return pl.pallas_call(
        kern,
        out_shape=jax.ShapeDtypeStruct((n_rows * SUBLANES, LANES), F32),
        grid_spec=gs,
        compiler_params=pltpu.CompilerParams(dimension_semantics=("arbitrary",),
                                             vmem_limit_bytes=VMEM_LIMIT, has_side_effects=True),
        name="dispatch",
    )(zero_start, h2, dest)


def _ffn_kernel(be_ref, br_ref, bv_ref, x_ref, wgu_ref, bgu_ref, wd_ref, bd_ref, o_ref, *, bm, d_ff):
    i = pl.program_id(0)

    @pl.when(bv_ref[i] == 1)
    def _():
        x = jnp.concatenate([x_ref[pl.ds(s, bm, stride=SUBLANES), :] for s in range(SUBLANES)],
                            axis=-1).astype(BF16)
        gu = jnp.dot(x, wgu_ref[0], preferred_element_type=F32) + bgu_ref[0]
        gate = jnp.minimum(gu[:, :d_ff], SWIGLU_LIMIT)
        up = jnp.clip(gu[:, d_ff:], -SWIGLU_LIMIT, SWIGLU_LIMIT)
        act = gate * jax.nn.sigmoid(SWIGLU_ALPHA * gate) * (up + 1.0)
        y = jnp.dot(act.astype(BF16), wd_ref[0], preferred_element_type=F32) + bd_ref[0]
        for s in range(SUBLANES):
            o_ref[pl.ds(s, bm, stride=SUBLANES), :] = y[:, s * LANES:(s + 1) * LANES]

    @pl.when(bv_ref[i] == 0)
    def _():
        o_ref[...] = jnp.zeros_like(o_ref)


def _ffn_call(blk_e, blk_row, blk_valid, xs, w_gu, b_gu, w_d, b_d, bm):
    n_e, d, f2 = w_gu.shape
    d_ff = f2 // 2
    nb = blk_e.shape[0]
    kern = functools.partial(_ffn_kernel, bm=bm, d_ff=d_ff)
    gs = pltpu.PrefetchScalarGridSpec(
        num_scalar_prefetch=3,
        grid=(nb,),
        in_specs=[pl.BlockSpec((bm * SUBLANES, LANES), lambda i, be, br, bv: (br[i], 0)),
                  pl.BlockSpec((1, d, f2), lambda i, be, br, bv: (be[i], 0, 0)),
                  pl.BlockSpec((1, 1, f2), lambda i, be, br, bv: (be[i], 0, 0)),
                  pl.BlockSpec((1, d_ff, d), lambda i, be, br, bv: (be[i], 0, 0)),
                  pl.BlockSpec((1, 1, d), lambda i, be, br, bv: (be[i], 0, 0))],
        out_specs=pl.BlockSpec((bm * SUBLANES, LANES), lambda i, be, br, bv: (i, 0)),
    )
    return pl.pallas_call(
        kern,
        out_shape=jax.ShapeDtypeStruct((nb * bm * SUBLANES, LANES), F32),
        grid_spec=gs,
        compiler_params=_params(("arbitrary",)),
        name="ffn",
    )(blk_e, blk_row, blk_valid, xs, w_gu, b_gu.reshape(n_e, 1, f2), w_d, b_d.reshape(n_e, 1, d))


def _combine_kernel(dest_ref, prow_ref, x1_ref, gf_ref, nw_ref, ys_ref, o_ref, gbuf, dsm, sem_rows, sem_s,
                    *, tm):
    cp = pltpu.make_async_copy(dest_ref.at[0], dsm, sem_s)
    cp.start()
    cp.wait()

    def issue(r, carry):
        for k in range(TOP_K):
            _row_copy(ys_ref, dsm[k, r], gbuf, k * tm + r, sem_rows).start()
        return carry

    lax.fori_loop(0, tm, issue, 0)
    pltpu.make_async_copy(ys_ref.at[pl.ds(0, TOP_K * tm * SUBLANES), :], gbuf, sem_rows).wait()

    p = prow_ref[...]
    for s in range(SUBLANES):
        moe = None
        for k in range(TOP_K):
            piece = gbuf[pl.ds(k * tm * SUBLANES + s, tm, stride=SUBLANES), :] * p[:, k:k + 1]
            moe = piece if moe is None else moe + piece
        sl = slice(s * LANES, (s + 1) * LANES)
        o_ref[:, sl] = x1_ref[:, sl] + gf_ref[0][:, sl] * moe
    xo = o_ref[...]
    o_ref[...] = xo * lax.rsqrt(jnp.mean(xo * xo, axis=-1, keepdims=True) + EPS) * nw_ref[...]


def _combine_call(dest, prow, x1, mod3, norm_final, ys, seq, tm):
    t, d = x1.shape
    nt = t // tm
    tiles_per_seq = seq // tm
    kern = functools.partial(_combine_kernel, tm=tm)
    return pl.pallas_call(
        kern,
        out_shape=jax.ShapeDtypeStruct((t, d), F32),
        grid=(nt,),
        in_specs=[pl.BlockSpec((1, SUBLANES, tm), lambda i: (i, 0, 0)),
                  pl.BlockSpec((tm, LANES), lambda i: (i, 0)),
                  pl.BlockSpec((tm, d), lambda i: (i, 0)),
                  pl.BlockSpec((1, 1, d), lambda i: ((i // tiles_per_seq) * N_MOD + 5, 0, 0)),
                  pl.BlockSpec((1, d), lambda i: (0, 0)),
                  pl.BlockSpec(memory_space=pl.ANY)],
        out_specs=pl.BlockSpec((tm, d), lambda i: (i, 0)),
        scratch_shapes=[pltpu.VMEM((TOP_K * tm * SUBLANES, LANES), F32),
                        pltpu.SMEM((SUBLANES, tm), jnp.int32),
                        pltpu.SemaphoreType.DMA,
                        pltpu.SemaphoreType.DMA],
        compiler_params=_params(("arbitrary",)),
        name="combine",
    )(dest, prow, x1, mod3, norm_final, ys)


def _plan(seq):
    def fit(pref):
        tm = min(pref, seq)
        assert seq % tm == 0
        return tm
    return dict(tm_in=fit(1024), tm_merge=fit(512), tm_moe=fit(512),
                ret_chunk=fit(RET_CHUNK), ssm_chunk=fit(SSM_CHUNK))


def _layer(x2, mod3, bsz, seq, norm_mix, norm_ffn, w_in, conv_w, conv_b, dt_bias, a_log, d_skip, ssm_norm,
           w_ret_out, w_ssm_out, w_out, w_router, b_router, w_gate_up, b_gate_up, w_down, b_down,
           norm_final):
    t, d = x2.shape
    plan = _plan(seq)
    qk_w = RET_HEADS * RET_QK_DIM
    v_w = RET_HEADS * RET_V_DIM
    d_inner = w_ssm_out.shape[0]
    conv_dim = conv_w.shape[1]
    n_heads = d_inner // SSM_HEAD_DIM
    dt_off = 2 * qk_w + 2 * v_w + d_inner + conv_dim

    w_main = jnp.concatenate([w_in[:, :dt_off], w_in[:, dt_off + n_heads:]], axis=1).astype(BF16)
    w_dt = jnp.pad(w_in[:, dt_off:dt_off + n_heads], ((0, 0), (0, LANES - n_heads)))
    half = RET_QK_DIM // 2
    inv_freq = ROPE_BASE ** (-jnp.arange(half, dtype=F32) / half)
    ang = jnp.arange(seq, dtype=F32)[:, None] * inv_freq[None, :]
    cos, sin = jnp.cos(ang), jnp.sin(ang)

    proj, dt_raw = _inproj_call(x2, norm_mix.reshape(1, d), mod3, cos, sin, w_main, w_dt, seq,
                                plan["tm_in"], 2 * qk_w)
    ya = _retention_call(proj, w_ret_out.astype(BF16), bsz, seq, plan["ret_chunk"])
    yb = _ssd_call(proj, dt_raw, conv_w, conv_b, dt_bias, a_log, d_skip, ssm_norm,
                   w_ssm_out.astype(BF16), bsz, seq, plan["ssm_chunk"])
    tm = plan["tm_merge"]
    x1, h2, idx, rank, prow, cnt = _merge_call(ya, yb, proj, x2, mod3, norm_ffn.reshape(1, d),
                                               w_out.astype(BF16), w_router.T, b_router, seq, tm)

    bm = FFN_BLOCK
    counts = cnt[:, 0]
    padded = ((counts + bm - 1) // bm) * bm
    pad_end = jnp.cumsum(padded)
    start_pad = pad_end - padded
    n_blocks = (t * TOP_K) // bm + N_EXPERTS
    e_sel = idx[:, :TOP_K, :]
    dest = start_pad[e_sel] + rank[:, :TOP_K, :]
    dest = jnp.pad(dest, ((0, 0), (0, SUBLANES - TOP_K), (0, 0))).astype(jnp.int32)
    blk_start = jnp.arange(n_blocks, dtype=jnp.int32) * bm
    n_real = pad_end[-1] // bm
    blk_valid = (jnp.arange(n_blocks) < n_real).astype(jnp.int32)
    blk_row = jnp.minimum(jnp.arange(n_blocks), n_real - 1).astype(jnp.int32)
    blk_e = jnp.minimum(jnp.searchsorted(pad_end, blk_row * bm, side="right"), N_EXPERTS - 1).astype(jnp.int32)

    zero_start = jnp.concatenate([start_pad + counts, n_real[None]]).astype(jnp.int32)
    xs = _dispatch_call(h2, dest, zero_start, n_blocks, plan["tm_moe"], bm)
    ys = _ffn_call(blk_e, blk_row, blk_valid, xs, w_gate_up.astype(BF16), b_gate_up,
                   w_down.astype(BF16), b_down, bm)
    return _combine_call(dest, prow, x1, mod3, norm_final.reshape(1, d), ys, seq, plan["tm_moe"])


def kernel(x, c, w_ada, b_ada, norm_mix, norm_ffn, w_in, conv_w, conv_b, dt_bias, a_log, d_skip, ssm_norm,
           w_ret_out, w_ssm_out, w_out, w_router, b_router, w_gate_up, b_gate_up, w_down, b_down, norm_final):
    bsz, seq, d = x.shape
    depth = w_ada.shape[0]
    assert depth == 1, "the final norm is fused into the single layer's last kernel"
    x2 = x.reshape(bsz * seq, d)
    l = 0
    mod = _mod_call(c, w_ada[l], b_ada[l])
    mod3 = mod.reshape(bsz * N_MOD, 1, d)
    out = _layer(x2, mod3, bsz, seq, norm_mix[l], norm_ffn[l], w_in[l], conv_w[l], conv_b[l], dt_bias[l],
                 a_log[l], d_skip[l], ssm_norm[l], w_ret_out[l], w_ssm_out[l], w_out[l], w_router[l],
                 b_router[l], w_gate_up[l], b_gate_up[l], w_down[l], b_down[l], norm_final)
    return out.reshape(bsz, seq, d)
```

```python
import functools
import math

import numpy as np
import jax
import jax.numpy as jnp
from jax import lax
from jax.experimental import pallas as pl
from jax.experimental.pallas import tpu as pltpu

F32 = jnp.float32
BF16 = jnp.bfloat16
HIGHEST = lax.Precision.HIGHEST

EPS = 1e-6
N_MOD = 6
RET_HEADS = 4
RET_QK_DIM = 256
RET_V_DIM = 512
ROPE_BASE = 10000.0
SSM_HEAD_DIM = 64
SSM_GROUPS = 8
SSM_STATE = 128
SSM_CONV = 4
N_EXPERTS = 32
TOP_K = 4
SWIGLU_LIMIT = 7.0
SWIGLU_ALPHA = 1.702

LANES = 128
SUBLANES = 8
VMEM_LIMIT = 56 * 1024 * 1024

RET_CHUNK = 128
SSM_CHUNK = 128
FFN_BLOCK = 256


def _params(sem, vmem=VMEM_LIMIT):
    return pltpu.CompilerParams(dimension_semantics=sem, vmem_limit_bytes=vmem)


def _nt_dot(a, b, **kw):
    return lax.dot_general(a, b, (((1,), (1,)), ((), ())), preferred_element_type=F32, **kw)


def _tn_dot(a, b, **kw):
    return lax.dot_general(a, b, (((0,), (0,)), ((), ())), preferred_element_type=F32, **kw)


def _silu(v):
    return v * jax.nn.sigmoid(v)


def _mod_kernel(c_ref, w_ref, b_ref, o_ref):
    cond = _silu(c_ref[...])
    o_ref[...] = jnp.dot(cond, w_ref[...], preferred_element_type=F32, precision=HIGHEST) + b_ref[...]


def _mod_call(c, w_ada, b_ada):
    bsz, d = c.shape
    n = w_ada.shape[1]
    return pl.pallas_call(
        _mod_kernel,
        out_shape=jax.ShapeDtypeStruct((bsz, n), F32),
        grid=(n // d,),
        in_specs=[pl.BlockSpec((bsz, d), lambda j: (0, 0)),
                  pl.BlockSpec((d, d), lambda j: (0, j)),
                  pl.BlockSpec((1, d), lambda j: (0, j))],
        out_specs=pl.BlockSpec((bsz, d), lambda j: (0, j)),
        compiler_params=_params(("arbitrary",)),
        name="mod",
    )(c, w_ada, b_ada.reshape(1, n))


def _inproj_kernel(x_ref, nw_ref, sc_ref, sh_ref, cos_ref, sin_ref, w_ref, wdt_ref,
                   o_ref, dt_ref, h_s, *, n_rope):
    j = pl.program_id(1)

    @pl.when(j == 0)
    def _():
        xf = x_ref[...]
        ms = jnp.mean(xf * xf, axis=-1, keepdims=True)
        y = xf * lax.rsqrt(ms + EPS) * nw_ref[...]
        hm = y * (1.0 + sc_ref[0]) + sh_ref[0]
        hb = hm.astype(BF16)
        h_s[...] = hb
        dt_ref[...] = jnp.dot(hm, wdt_ref[...], preferred_element_type=F32, precision=HIGHEST)
        acc = jnp.dot(hb, w_ref[...], preferred_element_type=F32)
        cos = cos_ref[...]
        sin = sin_ref[...]
        half = RET_QK_DIM // 2
        for c in range(n_rope):
            a = acc[:, c * RET_QK_DIM: c * RET_QK_DIM + half]
            b = acc[:, c * RET_QK_DIM + half: (c + 1) * RET_QK_DIM]
            scale = 1.0 if c < RET_HEADS else RET_QK_DIM ** -0.5
            o_ref[:, c * RET_QK_DIM: c * RET_QK_DIM + half] = ((a * cos - b * sin) * scale).astype(BF16)
            o_ref[:, c * RET_QK_DIM + half: (c + 1) * RET_QK_DIM] = ((a * sin + b * cos) * scale).astype(BF16)

    @pl.when(j != 0)
    def _():
        o_ref[...] = jnp.dot(h_s[...], w_ref[...], preferred_element_type=F32).astype(BF16)


def _inproj_call(x2, norm_w, mod3, cos, sin, w_main, w_dt, seq, tm, tn):
    t, d = x2.shape
    n = w_main.shape[1]
    tiles_per_seq = seq // tm
    assert tn == 2 * RET_HEADS * RET_QK_DIM, "rotary epilogue expects q and k in the first column tile"
    kern = functools.partial(_inproj_kernel, n_rope=2 * RET_HEADS)
    return pl.pallas_call(
        kern,
        out_shape=(jax.ShapeDtypeStruct((t, n), BF16), jax.ShapeDtypeStruct((t, LANES), F32)),
        grid=(t // tm, n // tn),
        in_specs=[
            pl.BlockSpec((tm, d), lambda i, j: (i, 0)),
            pl.BlockSpec((1, d), lambda i, j: (0, 0)),
            pl.BlockSpec((1, 1, d), lambda i, j: ((i // tiles_per_seq) * N_MOD + 1, 0, 0)),
            pl.BlockSpec((1, 1, d), lambda i, j: ((i // tiles_per_seq) * N_MOD + 0, 0, 0)),
            pl.BlockSpec((tm, LANES), lambda i, j: (i % tiles_per_seq, 0)),
            pl.BlockSpec((tm, LANES), lambda i, j: (i % tiles_per_seq, 0)),
            pl.BlockSpec((d, tn), lambda i, j: (0, j)),
            pl.BlockSpec((d, LANES), lambda i, j: (0, 0)),
        ],
        out_specs=(pl.BlockSpec((tm, tn), lambda i, j: (i, j)),
                   pl.BlockSpec((tm, LANES), lambda i, j: (i, 0))),
        scratch_shapes=[pltpu.VMEM((tm, d), BF16)],
        compiler_params=_params(("arbitrary", "arbitrary")),
        name="inproj",
    )(x2, norm_w, mod3, mod3, cos, sin, w_main, w_dt)


def _retention_kernel(q_ref, k_ref, v_ref, g_ref, din_ref, dq_ref, dk_ref, w_ref, o_ref, state,
                      *, decay_c):
    c = pl.program_id(1)

    @pl.when(c == 0)
    def _():
        state[...] = jnp.zeros_like(state)

    acc = None
    for h in range(RET_HEADS):
        qh = q_ref[:, h * RET_QK_DIM:(h + 1) * RET_QK_DIM]
        kh = k_ref[:, h * RET_QK_DIM:(h + 1) * RET_QK_DIM]
        vh = v_ref[:, h * RET_V_DIM:(h + 1) * RET_V_DIM]
        scores = _nt_dot(qh, kh) * din_ref[h]
        inner = jnp.dot(scores.astype(BF16), vh, preferred_element_type=F32)
        st = state[h]
        cross = jnp.dot(qh, st.astype(BF16), preferred_element_type=F32) * dq_ref[h]
        kd = (kh.astype(F32) * dk_ref[h]).astype(BF16)
        state[h] = st * decay_c[h] + _tn_dot(kd, vh)
        ret = inner + cross
        ret = ret * lax.rsqrt(jnp.mean(ret * ret, axis=-1, keepdims=True) + EPS)
        gh = g_ref[:, h * RET_V_DIM:(h + 1) * RET_V_DIM].astype(F32)
        ret = ret * _silu(gh)
        part = jnp.dot(ret.astype(BF16), w_ref[h * RET_V_DIM:(h + 1) * RET_V_DIM, :],
                       preferred_element_type=F32)
        acc = part if acc is None else acc + part
    o_ref[...] = acc


def _retention_tables(chunk):
    lg = np.log(1.0 - 2.0 ** (-5.0 - np.arange(RET_HEADS, dtype=np.float64)))
    idx = np.arange(chunk, dtype=np.float64)
    rel = idx[:, None] - idx[None, :]
    causal = rel >= 0
    din = np.where(causal[None], np.exp(np.where(causal, rel, 0.0)[None] * lg[:, None, None]), 0.0)
    dq = np.exp((idx + 1.0)[None, :, None] * lg[:, None, None])
    dk = np.exp((chunk - 1.0 - idx)[None, :, None] * lg[:, None, None])
    dc = tuple(float(v) for v in np.exp(chunk * lg))
    return (jnp.asarray(din, F32), jnp.asarray(dq, F32), jnp.asarray(dk, F32), dc)


def _retention_call(proj, w_ret, bsz, seq, chunk):
    t = proj.shape[0]
    d = w_ret.shape[1]
    nc = seq // chunk
    qk_w = RET_HEADS * RET_QK_DIM
    v_w = RET_HEADS * RET_V_DIM
    din, dq, dk, dc = _retention_tables(chunk)
    kern = functools.partial(_retention_kernel, decay_c=dc)
    row = lambda b, c: b * nc + c
    return pl.pallas_call(
        kern,
        out_shape=jax.ShapeDtypeStruct((t, d), F32),
        grid=(bsz, nc),
        in_specs=[
            pl.BlockSpec((chunk, qk_w), lambda b, c: (row(b, c), 0)),
            pl.BlockSpec((chunk, qk_w), lambda b, c: (row(b, c), 1)),
            pl.BlockSpec((chunk, v_w), lambda b, c: (row(b, c), 1)),
            pl.BlockSpec((chunk, v_w), lambda b, c: (row(b, c), 2)),
            pl.BlockSpec((RET_HEADS, chunk, chunk), lambda b, c: (0, 0, 0)),
            pl.BlockSpec((RET_HEADS, chunk, 1), lambda b, c: (0, 0, 0)),
            pl.BlockSpec((RET_HEADS, chunk, 1), lambda b, c: (0, 0, 0)),
            pl.BlockSpec((v_w, d), lambda b, c: (0, 0)),
        ],
        out_specs=pl.BlockSpec((chunk, d), lambda b, c: (row(b, c), 0)),
        scratch_shapes=[pltpu.VMEM((RET_HEADS, RET_QK_DIM, RET_V_DIM), F32)],
        compiler_params=_params(("arbitrary", "arbitrary")),
        name="retention",
    )(proj, proj, proj, proj, din, dq, dk, w_ret)


def _ssd_kernel(z_ref, xbc_ref, dt_ref, cw_ref, cb_ref, dtb_ref, alog_ref, dsk_ref, nw_ref,
                tril_ref, w_ref, o_ref, xpad, state, *, chunk, d_inner):
    c = pl.program_id(1)
    heads_per_group = d_inner // SSM_HEAD_DIM // SSM_GROUPS
    gw = heads_per_group * SSM_HEAD_DIM
    pad = SUBLANES

    @pl.when(c == 0)
    def _():
        xpad[0:pad, :] = jnp.zeros((pad, xpad.shape[1]), F32)
        state[...] = jnp.zeros_like(state)

    xpad[pad:pad + chunk, :] = xbc_ref[...].astype(F32)
    conv = cb_ref[...] + cw_ref[SSM_CONV - 1:SSM_CONV, :] * xpad[pad:pad + chunk, :]
    for k in range(SSM_CONV - 1):
        shift = SSM_CONV - 1 - k
        conv = conv + cw_ref[k:k + 1, :] * xpad[pad - shift:pad - shift + chunk, :]
    xpad[0:pad, :] = xpad[chunk:chunk + pad, :]
    act = _silu(conv)

    dt = jax.nn.softplus(dt_ref[...] + dtb_ref[...])
    a = -jnp.exp(alog_ref[...])
    adt = dt * a
    acs = jnp.dot(tril_ref[...], adt, preferred_element_type=F32, precision=HIGHEST)
    acs_t = acs.T
    acs_last = acs[chunk - 1:chunk, :]
    e_acs = jnp.exp(acs)
    e_last = jnp.exp(acs_last)
    decay_s = jnp.exp(acs_last - acs)
    li = lax.broadcasted_iota(jnp.int32, (chunk, chunk), 0)
    si = lax.broadcasted_iota(jnp.int32, (chunk, chunk), 1)
    causal = li >= si

    b_off = d_inner
    c_off = d_inner + SSM_GROUPS * SSM_STATE
    acc = None
    for g in range(SSM_GROUPS):
        bm = act[:, b_off + g * SSM_STATE: b_off + (g + 1) * SSM_STATE].astype(BF16)
        cm = act[:, c_off + g * SSM_STATE: c_off + (g + 1) * SSM_STATE].astype(BF16)
        xs_g = act[:, g * gw:(g + 1) * gw]
        cb = _nt_dot(cm, bm)
        yd, xdt, ea_x, ds_x, el_x = [], [], [], [], []
        for jh in range(heads_per_group):
            h = g * heads_per_group + jh
            seg = jnp.exp(jnp.where(causal, acs[:, h:h + 1] - acs_t[h:h + 1, :], -jnp.inf))
            xdt_h = xs_g[:, jh * SSM_HEAD_DIM:(jh + 1) * SSM_HEAD_DIM] * dt[:, h:h + 1]
            yd.append(jnp.dot((cb * seg).astype(BF16), xdt_h.astype(BF16), preferred_element_type=F32))
            xdt.append(xdt_h)
            ea_x.append(jnp.broadcast_to(e_acs[:, h:h + 1], (chunk, SSM_HEAD_DIM)))
            ds_x.append(jnp.broadcast_to(decay_s[:, h:h + 1], (chunk, SSM_HEAD_DIM)))
            el_x.append(jnp.broadcast_to(e_last[:, h:h + 1], (1, SSM_HEAD_DIM)))
        y_diag = jnp.concatenate(yd, axis=-1)
        xdt_g = jnp.concatenate(xdt, axis=-1)
        st = state[g]
        y_off = jnp.dot(cm, st.astype(BF16), preferred_element_type=F32) * jnp.concatenate(ea_x, axis=-1)
        xdec = (xdt_g * jnp.concatenate(ds_x, axis=-1)).astype(BF16)
        state[g] = st * jnp.concatenate(el_x, axis=-1) + _tn_dot(bm, xdec)
        y = y_diag + y_off + dsk_ref[:, g * gw:(g + 1) * gw] * xs_g
        yz = y * _silu(z_ref[:, g * gw:(g + 1) * gw].astype(F32))
        yn = yz * lax.rsqrt(jnp.mean(yz * yz, axis=-1, keepdims=True) + EPS) * nw_ref[:, g * gw:(g + 1) * gw]
        part = jnp.dot(yn.astype(BF16), w_ref[g * gw:(g + 1) * gw, :], preferred_element_type=F32)
        acc = part if acc is None else acc + part
    o_ref[...] = acc


def _ssd_call(proj, dt_raw, conv_w, conv_b, dt_bias, a_log, d_skip, ssm_norm, w_ssm, bsz, seq, chunk):
    t = proj.shape[0]
    d_inner, d = w_ssm.shape
    conv_dim = conv_w.shape[1]
    n_heads = d_inner // SSM_HEAD_DIM
    nc = seq // chunk
    gw = d_inner // SSM_GROUPS
    pad_h = lambda v: jnp.pad(v.astype(F32), (0, LANES - n_heads)).reshape(1, LANES)
    tril = jnp.asarray(np.tril(np.ones((chunk, chunk), np.float32)))
    kern = functools.partial(_ssd_kernel, chunk=chunk, d_inner=d_inner)
    row = lambda b, c: b * nc + c
    z_blk = (2 * RET_HEADS * RET_QK_DIM + 2 * RET_HEADS * RET_V_DIM) // d_inner
    xbc_blk = (2 * RET_HEADS * RET_QK_DIM + 2 * RET_HEADS * RET_V_DIM + d_inner) // conv_dim
    full = lambda shape: pl.BlockSpec(shape, lambda b, c: (0,) * len(shape))
    return pl.pallas_call(
        kern,
        out_shape=jax.ShapeDtypeStruct((t, d), F32),
        grid=(bsz, nc),
        in_specs=[
            pl.BlockSpec((chunk, d_inner), lambda b, c: (row(b, c), z_blk)),
            pl.BlockSpec((chunk, conv_dim), lambda b, c: (row(b, c), xbc_blk)),
            pl.BlockSpec((chunk, LANES), lambda b, c: (row(b, c), 0)),
            full((SSM_CONV, conv_dim)), full((1, conv_dim)), full((1, LANES)), full((1, LANES)),
            full((1, d_inner)), full((1, d_inner)), full((chunk, chunk)), full((d_inner, d)),
        ],
        out_specs=pl.BlockSpec((chunk, d), lambda b, c: (row(b, c), 0)),
        scratch_shapes=[pltpu.VMEM((chunk + SUBLANES, conv_dim), F32),
                        pltpu.VMEM((SSM_GROUPS, SSM_STATE, gw), F32)],
        compiler_params=_params(("arbitrary", "arbitrary")),
        name="ssd",
    )(proj, proj, dt_raw, conv_w, conv_b.reshape(1, conv_dim), pad_h(dt_bias), pad_h(a_log),
      jnp.repeat(d_skip.astype(F32), SSM_HEAD_DIM).reshape(1, d_inner), ssm_norm.reshape(1, d_inner),
      tril, w_ssm)


def _merge_kernel(ya_ref, yb_ref, ga_ref, gb_ref, x_ref, gm_ref, scf_ref, shf_ref, nw_ref, wo_ref,
                  wr_ref, br_ref, tri_ref,
                  x1_ref, h2_ref, idx_ref, rank_ref, prow_ref, cnt_ref, cnt_s, *, tm):
    i = pl.program_id(0)

    @pl.when(i == 0)
    def _():
        cnt_s[...] = jnp.zeros_like(cnt_s)

    merged = (jax.nn.sigmoid(ga_ref[...].astype(F32)) * ya_ref[...]
              + jax.nn.sigmoid(gb_ref[...].astype(F32)) * yb_ref[...])
    mo = jnp.dot(merged.astype(BF16), wo_ref[...], preferred_element_type=F32)
    x1 = x_ref[...] + gm_ref[0] * mo
    x1_ref[...] = x1
    ms = jnp.mean(x1 * x1, axis=-1, keepdims=True)
    h2 = x1 * lax.rsqrt(ms + EPS) * nw_ref[...] * (1.0 + scf_ref[0]) + shf_ref[0]
    h2_ref[...] = h2

    lg = _nt_dot(wr_ref[...], h2, precision=HIGHEST) + br_ref[...]
    sub = lax.broadcasted_iota(jnp.int32, lg.shape, 0)
    work = lg
    vals, idxs, sels = [], [], []
    for _ in range(TOP_K):
        m = jnp.max(work, axis=0, keepdims=True)
        ik = jnp.min(jnp.where(work == m, sub, N_EXPERTS), axis=0, keepdims=True)
        sel = sub == ik
        vals.append(m)
        idxs.append(ik)
        sels.append(sel)
        work = jnp.where(sel, -jnp.inf, work)
    exps = [jnp.exp(v - vals[0]) for v in vals]
    denom = exps[0]
    for e in exps[1:]:
        denom = denom + e
    probs = [e / denom for e in exps]

    base = cnt_s[:, 0:1]
    ranks = []
    for k in range(TOP_K):
        mk = jnp.where(sels[k], 1.0, 0.0)
        pre = jnp.dot(mk.astype(BF16), tri_ref[...], preferred_element_type=F32)
        ranks.append(jnp.sum(jnp.where(sels[k], pre + base, 0.0), axis=0, keepdims=True))
        base = base + jnp.sum(mk, axis=1, keepdims=True)
    cnt_s[...] = jnp.broadcast_to(base, cnt_s.shape)
    cnt_ref[...] = cnt_s[...].astype(jnp.int32)

    zi = jnp.zeros((SUBLANES - TOP_K, tm), jnp.int32)
    idx_ref[0] = jnp.concatenate(idxs + [zi], axis=0)
    rank_ref[0] = jnp.concatenate([r.astype(jnp.int32) for r in ranks] + [zi], axis=0)
    pt = jnp.concatenate(probs + [jnp.zeros((LANES - TOP_K, tm), F32)], axis=0)
    prow_ref[...] = pt.T


def _merge_call(ya, yb, proj, x2, mod3, norm_w, w_out, w_router_t, b_router, seq, tm):
    t, d = x2.shape
    nt = t // tm
    tiles_per_seq = seq // tm
    ga_blk = proj.shape[1] // d - 2
    tri = jnp.asarray(np.triu(np.ones((tm, tm), np.float32), 1), BF16)
    kern = functools.partial(_merge_kernel, tm=tm)
    modspec = lambda m: pl.BlockSpec((1, 1, d), lambda i: ((i // tiles_per_seq) * N_MOD + m, 0, 0))
    return pl.pallas_call(
        kern,
        out_shape=(jax.ShapeDtypeStruct((t, d), F32), jax.ShapeDtypeStruct((t, d), F32),
                   jax.ShapeDtypeStruct((nt, SUBLANES, tm), jnp.int32),
                   jax.ShapeDtypeStruct((nt, SUBLANES, tm), jnp.int32),
                   jax.ShapeDtypeStruct((t, LANES), F32),
                   jax.ShapeDtypeStruct((N_EXPERTS, LANES), jnp.int32)),
        grid=(nt,),
        in_specs=[
            pl.BlockSpec((tm, d), lambda i: (i, 0)),
            pl.BlockSpec((tm, d), lambda i: (i, 0)),
            pl.BlockSpec((tm, d), lambda i: (i, ga_blk)),
            pl.BlockSpec((tm, d), lambda i: (i, ga_blk + 1)),
            pl.BlockSpec((tm, d), lambda i: (i, 0)),
            modspec(2), modspec(4), modspec(3),
            pl.BlockSpec((1, d), lambda i: (0, 0)),
            pl.BlockSpec((d, d), lambda i: (0, 0)),
            pl.BlockSpec((N_EXPERTS, d), lambda i: (0, 0)),
            pl.BlockSpec((N_EXPERTS, 1), lambda i: (0, 0)),
            pl.BlockSpec((tm, tm), lambda i: (0, 0)),
        ],
        out_specs=(pl.BlockSpec((tm, d), lambda i: (i, 0)),
                   pl.BlockSpec((tm, d), lambda i: (i, 0)),
                   pl.BlockSpec((1, SUBLANES, tm), lambda i: (i, 0, 0)),
                   pl.BlockSpec((1, SUBLANES, tm), lambda i: (i, 0, 0)),
                   pl.BlockSpec((tm, LANES), lambda i: (i, 0)),
                   pl.BlockSpec((N_EXPERTS, LANES), lambda i: (0, 0))),
        scratch_shapes=[pltpu.VMEM((N_EXPERTS, LANES), F32)],
        compiler_params=_params(("arbitrary",)),
        name="merge",
    )(ya, yb, proj, proj, x2, mod3, mod3, mod3, norm_w, w_out, w_router_t,
      b_router.reshape(N_EXPERTS, 1), tri)


def _row_copy(src, src_row, dst, dst_row, sem):
    return pltpu.make_async_copy(
        src.at[pl.ds(pl.multiple_of(src_row * SUBLANES, SUBLANES), SUBLANES), :],
        dst.at[pl.ds(pl.multiple_of(dst_row * SUBLANES, SUBLANES), SUBLANES), :], sem)


def _dispatch_kernel(zs_ref, h2_ref, dest_ref, xs_ref, slab, zbuf, dsm, sem_rows, sem_s, sem_z,
                     *, tm, nt, bm, n_blocks):
    i = pl.program_id(0)
    slot = i % 2
    rows_per_tile = TOP_K * tm

    def wait_rows(s):
        pltpu.make_async_copy(xs_ref.at[pl.ds(0, rows_per_tile * SUBLANES), :],
                              xs_ref.at[pl.ds(0, rows_per_tile * SUBLANES), :], sem_rows.at[s]).wait()

    @pl.when(i == 0)
    def _():
        zbuf[...] = jnp.zeros_like(zbuf)
        for e in range(N_EXPERTS):
            fill = pltpu.make_async_copy(
                zbuf, xs_ref.at[pl.ds(pl.multiple_of(zs_ref[e] * SUBLANES, SUBLANES), bm * SUBLANES), :],
                sem_z)
            fill.start()
            fill.wait()

        def fill_tail(b, carry):
            fill = pltpu.make_async_copy(
                zbuf, xs_ref.at[pl.ds(pl.multiple_of(b * (bm * SUBLANES), SUBLANES), bm * SUBLANES), :], sem_z)
            fill.start()
            fill.wait()
            return carry

        lax.fori_loop(zs_ref[N_EXPERTS], n_blocks + 1, fill_tail, 0)

    @pl.when(i >= 2)
    def _():
        wait_rows(slot)

    cp = pltpu.make_async_copy(dest_ref.at[0], dsm, sem_s)
    cp.start()
    base = pl.multiple_of(slot * (tm * SUBLANES), SUBLANES)
    for s in range(SUBLANES):
        piece = h2_ref[:, s * LANES:(s + 1) * LANES]
        slab[pl.ds(base + s, tm, stride=SUBLANES), :] = piece
    cp.wait()

    def issue(r, carry):
        for k in range(TOP_K):
            _row_copy(slab, slot * tm + r, xs_ref, dsm[k, r], sem_rows.at[slot]).start()
        return carry

    lax.fori_loop(0, tm, issue, 0)

    @pl.when(i == nt - 1)
    def _():
        wait_rows(slot)
        if nt >= 2:
            wait_rows(1 - slot)


def _dispatch_call(h2, dest, zero_start, n_blocks, tm, bm):
    t, d = h2.shape
    nt = t // tm
    n_rows = (n_blocks + 1) * bm
    assert d == SUBLANES * LANES
    kern = functools.partial(_dispatch_kernel, tm=tm, nt=nt, bm=bm, n_blocks=n_blocks)
    gs = pltpu.PrefetchScalarGridSpec(
        num_scalar_prefetch=1,
        grid=(nt,),
        in_specs=[pl.BlockSpec((tm, d), lambda i, zs: (i, 0)),
                  pl.BlockSpec((1, SUBLANES, tm), lambda i, zs: (i, 0, 0))],
        out_specs=pl.BlockSpec(memory_space=pl.ANY),
        scratch_shapes=[pltpu.VMEM((2 * tm * SUBLANES, LANES), F32),
                        pltpu.VMEM((bm * SUBLANES, LANES), F32),
                        pltpu.SMEM((SUBLANES, tm), jnp.int32),
                        pltpu.SemaphoreType.DMA((2,)),
                        pltpu.SemaphoreType.DMA,
                        pltpu.SemaphoreType.DMA],
    )
    return pl.pallas_call(
        kern,
        out_shape=jax.ShapeDtypeStruct((n_rows * SUBLANES, LANES), F32),
        grid_spec=gs,
        compiler_params=pltpu.CompilerParams(dimension_semantics=("arbitrary",),
                                             vmem_limit_bytes=VMEM_LIMIT, has_side_effects=True),
        name="dispatch",
    )(zero_start, h2, dest)


def _ffn_kernel(be_ref, br_ref, bv_ref, x_ref, wgu_ref, bgu_ref, wd_ref, bd_ref, o_ref, *, bm, d_ff):
    i = pl.program_id(0)

    @pl.when(bv_ref[i] == 1)
    def _():
        x = jnp.concatenate([x_ref[pl.ds(s, bm, stride=SUBLANES), :] for s in range(SUBLANES)],
                            axis=-1).astype(BF16)
        gu = jnp.dot(x, wgu_ref[0], preferred_element_type=F32) + bgu_ref[0]
        gate = jnp.minimum(gu[:, :d_ff], SWIGLU_LIMIT)
        up = jnp.clip(gu[:, d_ff:], -SWIGLU_LIMIT, SWIGLU_LIMIT)
        act = gate * jax.nn.sigmoid(SWIGLU_ALPHA * gate) * (up + 1.0)
        y = jnp.dot(act.astype(BF16), wd_ref[0], preferred_element_type=F32) + bd_ref[0]
        for s in range(SUBLANES):
            o_ref[pl.ds(s, bm, stride=SUBLANES), :] = y[:, s * LANES:(s + 1) * LANES]

    @pl.when(bv_ref[i] == 0)
    def _():
        o_ref[...] = jnp.zeros_like(o_ref)


def _ffn_call(blk_e, blk_row, blk_valid, xs, w_gu, b_gu, w_d, b_d, bm):
    n_e, d, f2 = w_gu.shape
    d_ff = f2 // 2
    nb = blk_e.shape[0]
    kern = functools.partial(_ffn_kernel, bm=bm, d_ff=d_ff)
    gs = pltpu.PrefetchScalarGridSpec(
        num_scalar_prefetch=3,
        grid=(nb,),
        in_specs=[pl.BlockSpec((bm * SUBLANES, LANES), lambda i, be, br, bv: (br[i], 0)),
                  pl.BlockSpec((1, d, f2), lambda i, be, br, bv: (be[i], 0, 0)),
                  pl.BlockSpec((1, 1, f2), lambda i, be, br, bv: (be[i], 0, 0)),
                  pl.BlockSpec((1, d_ff, d), lambda i, be, br, bv: (be[i], 0, 0)),
                  pl.BlockSpec((1, 1, d), lambda i, be, br, bv: (be[i], 0, 0))],
        out_specs=pl.BlockSpec((bm * SUBLANES, LANES), lambda i, be, br, bv: (i, 0)),
    )
    return pl.pallas_call(
        kern,
        out_shape=jax.ShapeDtypeStruct((nb * bm * SUBLANES, LANES), F32),
        grid_spec=gs,
        compiler_params=_params(("arbitrary",)),
        name="ffn",
    )(blk_e, blk_row, blk_valid, xs, w_gu, b_gu.reshape(n_e, 1, f2), w_d, b_d.reshape(n_e, 1, d))


def _combine_kernel(dest_ref, prow_ref, x1_ref, gf_ref, nw_ref, ys_ref, o_ref, gbuf, dsm, sem_rows, sem_s,
                    *, tm):
    cp = pltpu.make_async_copy(dest_ref.at[0], dsm, sem_s)
    cp.start()
    cp.wait()

    def issue(r, carry):
        for k in range(TOP_K):
            _row_copy(ys_ref, dsm[k, r], gbuf, k * tm + r, sem_rows).start()
        return carry

    lax.fori_loop(0, tm, issue, 0)
    pltpu.make_async_copy(ys_ref.at[pl.ds(0, TOP_K * tm * SUBLANES), :], gbuf, sem_rows).wait()

    p = prow_ref[...]
    for s in range(SUBLANES):
        moe = None
        for k in range(TOP_K):
            piece = gbuf[pl.ds(k * tm * SUBLANES + s, tm, stride=SUBLANES), :] * p[:, k:k + 1]
            moe = piece if moe is None else moe + piece
        sl = slice(s * LANES, (s + 1) * LANES)
        o_ref[:, sl] = x1_ref[:, sl] + gf_ref[0][:, sl] * moe
    xo = o_ref[...]
    o_ref[...] = xo * lax.rsqrt(jnp.mean(xo * xo, axis=-1, keepdims=True) + EPS) * nw_ref[...]


def _combine_call(dest, prow, x1, mod3, norm_final, ys, seq, tm):
    t, d = x1.shape
    nt = t // tm
    tiles_per_seq = seq // tm
    kern = functools.partial(_combine_kernel, tm=tm)
    return pl.pallas_call(
        kern,
        out_shape=jax.ShapeDtypeStruct((t, d), F32),
        grid=(nt,),
        in_specs=[pl.BlockSpec((1, SUBLANES, tm), lambda i: (i, 0, 0)),
                  pl.BlockSpec((tm, LANES), lambda i: (i, 0)),
                  pl.BlockSpec((tm, d), lambda i: (i, 0)),
                  pl.BlockSpec((1, 1, d), lambda i: ((i // tiles_per_seq) * N_MOD + 5, 0, 0)),
                  pl.BlockSpec((1, d), lambda i: (0, 0)),
                  pl.BlockSpec(memory_space=pl.ANY)],
        out_specs=pl.BlockSpec((tm, d), lambda i: (i, 0)),
        scratch_shapes=[pltpu.VMEM((TOP_K * tm * SUBLANES, LANES), F32),
                        pltpu.SMEM((SUBLANES, tm), jnp.int32),
                        pltpu.SemaphoreType.DMA,
                        pltpu.SemaphoreType.DMA],
        compiler_params=_params(("arbitrary",)),
        name="combine",
    )(dest, prow, x1, mod3, norm_final, ys)


def _plan(seq):
    def fit(pref):
        tm = min(pref, seq)
        assert seq % tm == 0
        return tm
    return dict(tm_in=fit(1024), tm_merge=fit(512), tm_moe=fit(512),
                ret_chunk=fit(RET_CHUNK), ssm_chunk=fit(SSM_CHUNK))


def _layer(x2, mod3, bsz, seq, norm_mix, norm_ffn, w_in, conv_w, conv_b, dt_bias, a_log, d_skip, ssm_norm,
           w_ret_out, w_ssm_out, w_out, w_router, b_router, w_gate_up, b_gate_up, w_down, b_down,
           norm_final):
    t, d = x2.shape
    plan = _plan(seq)
    qk_w = RET_HEADS * RET_QK_DIM
    v_w = RET_HEADS * RET_V_DIM
    d_inner = w_ssm_out.shape[0]
    conv_dim = conv_w.shape[1]
    n_heads = d_inner // SSM_HEAD_DIM
    dt_off = 2 * qk_w + 2 * v_w + d_inner + conv_dim

    w_main = jnp.concatenate([w_in[:, :dt_off], w_in[:, dt_off + n_heads:]], axis=1).astype(BF16)
    w_dt = jnp.pad(w_in[:, dt_off:dt_off + n_heads], ((0, 0), (0, LANES - n_heads)))
    half = RET_QK_DIM // 2
    inv_freq = ROPE_BASE ** (-jnp.arange(half, dtype=F32) / half)
    ang = jnp.arange(seq, dtype=F32)[:, None] * inv_freq[None, :]
    cos, sin = jnp.cos(ang), jnp.sin(ang)

    proj, dt_raw = _inproj_call(x2, norm_mix.reshape(1, d), mod3, cos, sin, w_main, w_dt, seq,
                                plan["tm_in"], 2 * qk_w)
    ya = _retention_call(proj, w_ret_out.astype(BF16), bsz, seq, plan["ret_chunk"])
    yb = _ssd_call(proj, dt_raw, conv_w, conv_b, dt_bias, a_log, d_skip, ssm_norm,
                   w_ssm_out.astype(BF16), bsz, seq, plan["ssm_chunk"])
    tm = plan["tm_merge"]
    x1, h2, idx, rank, prow, cnt = _merge_call(ya, yb, proj, x2, mod3, norm_ffn.reshape(1, d),
                                               w_out.astype(BF16), w_router.T, b_router, seq, tm)

    bm = FFN_BLOCK
    counts = cnt[:, 0]
    padded = ((counts + bm - 1) // bm) * bm
    pad_end = jnp.cumsum(padded)
    start_pad = pad_end - padded
    n_blocks = (t * TOP_K) // bm + N_EXPERTS
    e_ids = jnp.arange(N_EXPERTS, dtype=jnp.int32)[:, None, None, None]
    dest = rank + jnp.sum(jnp.where(idx[None] == e_ids, start_pad[:, None, None, None], 0), axis=0)
    dest = dest.astype(jnp.int32)
    n_real = pad_end[-1] // bm
    blk_valid = (jnp.arange(n_blocks) < n_real).astype(jnp.int32)
    blk_row = jnp.minimum(jnp.arange(n_blocks), n_real - 1).astype(jnp.int32)
    blk_e = jnp.minimum(jnp.sum(pad_end[None, :] <= (blk_row * bm)[:, None], axis=1),
                        N_EXPERTS - 1).astype(jnp.int32)

    zero_start = jnp.concatenate([start_pad + counts, n_real[None]]).astype(jnp.int32)
    xs = _dispatch_call(h2, dest, zero_start, n_blocks, plan["tm_moe"], bm)
    ys = _ffn_call(blk_e, blk_row, blk_valid, xs, w_gate_up.astype(BF16), b_gate_up,
                   w_down.astype(BF16), b_down, bm)
    return _combine_call(dest, prow, x1, mod3, norm_final.reshape(1, d), ys, seq, plan["tm_moe"])


def kernel(x, c, w_ada, b_ada, norm_mix, norm_ffn, w_in, conv_w, conv_b, dt_bias, a_log, d_skip, ssm_norm,
           w_ret_out, w_ssm_out, w_out, w_router, b_router, w_gate_up, b_gate_up, w_down, b_down, norm_final):
    bsz, seq, d = x.shape
    depth = w_ada.shape[0]
    assert depth == 1, "the final norm is fused into the single layer's last kernel"
    x2 = x.reshape(bsz * seq, d)
    l = 0
    mod = _mod_call(c, w_ada[l], b_ada[l])
    mod3 = mod.reshape(bsz * N_MOD, 1, d)
    out = _layer(x2, mod3, bsz, seq, norm_mix[l], norm_ffn[l], w_in[l], conv_w[l], conv_b[l], dt_bias[l],
                 a_log[l], d_skip[l], ssm_norm[l], w_ret_out[l], w_ssm_out[l], w_out[l], w_router[l],
                 b_router[l], w_gate_up[l], b_gate_up[l], w_down[l], b_down[l], norm_final)
    return out.reshape(bsz, seq, d)
```

```python
import functools
import math

import numpy as np
import jax
import jax.numpy as jnp
from jax import lax
from jax.experimental import pallas as pl
from jax.experimental.pallas import tpu as pltpu

F32 = jnp.float32
BF16 = jnp.bfloat16
HIGHEST = lax.Precision.HIGHEST

EPS = 1e-6
N_MOD = 6
RET_HEADS = 4
RET_QK_DIM = 256
RET_V_DIM = 512
ROPE_BASE = 10000.0
SSM_HEAD_DIM = 64
SSM_GROUPS = 8
SSM_STATE = 128
SSM_CONV = 4
N_EXPERTS = 32
TOP_K = 4
SWIGLU_LIMIT = 7.0
SWIGLU_ALPHA = 1.702

LANES = 128
SUBLANES = 8
VMEM_LIMIT = 56 * 1024 * 1024

RET_CHUNK = 128
SSM_CHUNK = 128
FFN_BLOCK = 256


def _params(sem, vmem=VMEM_LIMIT):
    return pltpu.CompilerParams(dimension_semantics=sem, vmem_limit_bytes=vmem)


def _nt_dot(a, b, **kw):
    return lax.dot_general(a, b, (((1,), (1,)), ((), ())), preferred_element_type=F32, **kw)


def _tn_dot(a, b, **kw):
    return lax.dot_general(a, b, (((0,), (0,)), ((), ())), preferred_element_type=F32, **kw)


def _silu(v):
    return v * jax.nn.sigmoid(v)


def _mod_kernel(c_ref, w_ref, b_ref, o_ref):
    cond = _silu(c_ref[...])
    o_ref[...] = jnp.dot(cond, w_ref[...], preferred_element_type=F32, precision=HIGHEST) + b_ref[...]


def _mod_call(c, w_ada, b_ada):
    bsz, d = c.shape
    n = w_ada.shape[1]
    return pl.pallas_call(
        _mod_kernel,
        out_shape=jax.ShapeDtypeStruct((bsz, n), F32),
        grid=(n // d,),
        in_specs=[pl.BlockSpec((bsz, d), lambda j: (0, 0)),
                  pl.BlockSpec((d, d), lambda j: (0, j)),
                  pl.BlockSpec((1, d), lambda j: (0, j))],
        out_specs=pl.BlockSpec((bsz, d), lambda j: (0, j)),
        compiler_params=_params(("arbitrary",)),
        name="mod",
    )(c, w_ada, b_ada.reshape(1, n))


def _inproj_kernel(x_ref, nw_ref, sc_ref, sh_ref, cos_ref, sin_ref, w_ref, wdt_ref, cw_ref, cb_ref,
                   o_ref, dt_ref, h_s, work, carry, *, n_rope, conv_j0, conv_nj, tiles_per_seq, tm):
    i = pl.program_id(0)
    j = pl.program_id(1)
    n_dt = dt_ref.shape[1]

    @pl.when(j == 0)
    def _():
        xf = x_ref[...]
        ms = jnp.mean(xf * xf, axis=-1, keepdims=True)
        y = xf * lax.rsqrt(ms + EPS) * nw_ref[...]
        hm = y * (1.0 + sc_ref[0]) + sh_ref[0]
        hb = hm.astype(BF16)
        h_s[...] = hb
        h_lo = (hm - hb.astype(F32)).astype(BF16)
        d_hi = jnp.dot(hb, wdt_ref[...], preferred_element_type=F32)
        d_lo = jnp.dot(h_lo, wdt_ref[:, :n_dt], preferred_element_type=F32)
        dt_ref[...] = d_hi[:, :n_dt] + d_hi[:, n_dt:] + d_lo
        acc = jnp.dot(hb, w_ref[...], preferred_element_type=F32)
        cos = cos_ref[...]
        sin = sin_ref[...]
        half = RET_QK_DIM // 2
        for c in range(n_rope):
            a = acc[:, c * RET_QK_DIM: c * RET_QK_DIM + half]
            b = acc[:, c * RET_QK_DIM + half: (c + 1) * RET_QK_DIM]
            scale = 1.0 if c < RET_HEADS else RET_QK_DIM ** -0.5
            o_ref[:, c * RET_QK_DIM: c * RET_QK_DIM + half] = ((a * cos - b * sin) * scale).astype(BF16)
            o_ref[:, c * RET_QK_DIM + half: (c + 1) * RET_QK_DIM] = ((a * sin + b * cos) * scale).astype(BF16)

    is_conv = (j >= conv_j0) & (j < conv_j0 + conv_nj)

    @pl.when(is_conv)
    def _():
        cj = j - conv_j0
        pad = SUBLANES

        @pl.when(i % tiles_per_seq == 0)
        def _():
            carry[cj] = jnp.zeros(carry.shape[1:], F32)

        work[0:pad, :] = carry[cj]
        work[pad:pad + tm, :] = jnp.dot(h_s[...], w_ref[...], preferred_element_type=F32)
        conv = cb_ref[...] + cw_ref[SSM_CONV - 1:SSM_CONV, :] * work[pad:pad + tm, :]
        for k in range(SSM_CONV - 1):
            shift = SSM_CONV - 1 - k
            conv = conv + cw_ref[k:k + 1, :] * work[pad - shift:pad - shift + tm, :]
        carry[cj] = work[tm:tm + pad, :]
        o_ref[...] = _silu(conv).astype(BF16)

    @pl.when((j != 0) & jnp.logical_not(is_conv))
    def _():
        o_ref[...] = jnp.dot(h_s[...], w_ref[...], preferred_element_type=F32).astype(BF16)


def _inproj_call(x2, norm_w, mod3, cos, sin, w_main, w_dt, conv_w, conv_b, conv_off, seq, tm, tn):
    t, d = x2.shape
    n = w_main.shape[1]
    conv_dim = conv_w.shape[1]
    tiles_per_seq = seq // tm
    assert tn == 2 * RET_HEADS * RET_QK_DIM, "rotary epilogue expects q and k in the first column tile"
    assert conv_off % tn == 0 and conv_dim % tn == 0
    conv_j0, conv_nj = conv_off // tn, conv_dim // tn
    kern = functools.partial(_inproj_kernel, n_rope=2 * RET_HEADS, conv_j0=conv_j0, conv_nj=conv_nj,
                             tiles_per_seq=tiles_per_seq, tm=tm)
    conv_idx = lambda i, j: (0, jnp.clip(j - conv_j0, 0, conv_nj - 1))
    return pl.pallas_call(
        kern,
        out_shape=(jax.ShapeDtypeStruct((t, n), BF16), jax.ShapeDtypeStruct((t, LANES), F32)),
        grid=(t // tm, n // tn),
        in_specs=[
            pl.BlockSpec((tm, d), lambda i, j: (i, 0)),
            pl.BlockSpec((1, d), lambda i, j: (0, 0)),
            pl.BlockSpec((1, 1, d), lambda i, j: ((i // tiles_per_seq) * N_MOD + 1, 0, 0)),
            pl.BlockSpec((1, 1, d), lambda i, j: ((i // tiles_per_seq) * N_MOD + 0, 0, 0)),
            pl.BlockSpec((tm, LANES), lambda i, j: (i % tiles_per_seq, 0)),
            pl.BlockSpec((tm, LANES), lambda i, j: (i % tiles_per_seq, 0)),
            pl.BlockSpec((d, tn), lambda i, j: (0, j)),
            pl.BlockSpec((d, 2 * LANES), lambda i, j: (0, 0)),
            pl.BlockSpec((SSM_CONV, tn), conv_idx),
            pl.BlockSpec((1, tn), conv_idx),
        ],
        out_specs=(pl.BlockSpec((tm, tn), lambda i, j: (i, j)),
                   pl.BlockSpec((tm, LANES), lambda i, j: (i, 0))),
        scratch_shapes=[pltpu.VMEM((tm, d), BF16),
                        pltpu.VMEM((tm + SUBLANES, tn), F32),
                        pltpu.VMEM((conv_nj, SUBLANES, tn), F32)],
        compiler_params=_params(("arbitrary", "arbitrary")),
        name="inproj",
    )(x2, norm_w, mod3, mod3, cos, sin, w_main, w_dt, conv_w, conv_b.reshape(1, conv_dim))


def _retention_kernel(q_ref, k_ref, v_ref, g_ref, din_ref, dq_ref, dk_ref, w_ref, o_ref, state,
                      *, decay_c):
    c = pl.program_id(1)

    @pl.when(c == 0)
    def _():
        state[...] = jnp.zeros_like(state)

    acc = None
    for h in range(RET_HEADS):
        qh = q_ref[:, h * RET_QK_DIM:(h + 1) * RET_QK_DIM]
        kh = k_ref[:, h * RET_QK_DIM:(h + 1) * RET_QK_DIM]
        vh = v_ref[:, h * RET_V_DIM:(h + 1) * RET_V_DIM]
        scores = _nt_dot(qh, kh) * din_ref[h]
        inner = jnp.dot(scores.astype(BF16), vh, preferred_element_type=F32)
        st = state[h]
        cross = jnp.dot(qh, st.astype(BF16), preferred_element_type=F32) * dq_ref[h]
        kd = (kh.astype(F32) * dk_ref[h]).astype(BF16)
        state[h] = st * decay_c[h] + _tn_dot(kd, vh)
        ret = inner + cross
        ret = ret * lax.rsqrt(jnp.mean(ret * ret, axis=-1, keepdims=True) + EPS)
        gh = g_ref[:, h * RET_V_DIM:(h + 1) * RET_V_DIM].astype(F32)
        ret = ret * _silu(gh)
        part = jnp.dot(ret.astype(BF16), w_ref[h * RET_V_DIM:(h + 1) * RET_V_DIM, :],
                       preferred_element_type=F32)
        acc = part if acc is None else acc + part
    o_ref[...] = acc


def _retention_tables(chunk):
    lg = np.log(1.0 - 2.0 ** (-5.0 - np.arange(RET_HEADS, dtype=np.float64)))
    idx = np.arange(chunk, dtype=np.float64)
    rel = idx[:, None] - idx[None, :]
    causal = rel >= 0
    din = np.where(causal[None], np.exp(np.where(causal, rel, 0.0)[None] * lg[:, None, None]), 0.0)
    dq = np.exp((idx + 1.0)[None, :, None] * lg[:, None, None])
    dk = np.exp((chunk - 1.0 - idx)[None, :, None] * lg[:, None, None])
    dc = tuple(float(v) for v in np.exp(chunk * lg))
    return (jnp.asarray(din, F32), jnp.asarray(dq, F32), jnp.asarray(dk, F32), dc)


def _retention_call(proj, w_ret, bsz, seq, chunk):
    t = proj.shape[0]
    d = w_ret.shape[1]
    nc = seq // chunk
    qk_w = RET_HEADS * RET_QK_DIM
    v_w = RET_HEADS * RET_V_DIM
    din, dq, dk, dc = _retention_tables(chunk)
    kern = functools.partial(_retention_kernel, decay_c=dc)
    row = lambda b, c: b * nc + c
    return pl.pallas_call(
        kern,
        out_shape=jax.ShapeDtypeStruct((t, d), F32),
        grid=(bsz, nc),
        in_specs=[
            pl.BlockSpec((chunk, qk_w), lambda b, c: (row(b, c), 0)),
            pl.BlockSpec((chunk, qk_w), lambda b, c: (row(b, c), 1)),
            pl.BlockSpec((chunk, v_w), lambda b, c: (row(b, c), 1)),
            pl.BlockSpec((chunk, v_w), lambda b, c: (row(b, c), 2)),
            pl.BlockSpec((RET_HEADS, chunk, chunk), lambda b, c: (0, 0, 0)),
            pl.BlockSpec((RET_HEADS, chunk, 1), lambda b, c: (0, 0, 0)),
            pl.BlockSpec((RET_HEADS, chunk, 1), lambda b, c: (0, 0, 0)),
            pl.BlockSpec((v_w, d), lambda b, c: (0, 0)),
        ],
        out_specs=pl.BlockSpec((chunk, d), lambda b, c: (row(b, c), 0)),
        scratch_shapes=[pltpu.VMEM((RET_HEADS, RET_QK_DIM, RET_V_DIM), F32)],
        compiler_params=_params(("arbitrary", "arbitrary")),
        name="retention",
    )(proj, proj, proj, proj, din, dq, dk, w_ret)


def _ssd_kernel(z_ref, xbc_ref, dt_ref, dtb_ref, alog_ref, dsk_ref, nw_ref,
                tril_ref, w_ref, o_ref, state, *, chunk, d_inner):
    c = pl.program_id(1)
    heads_per_group = d_inner // SSM_HEAD_DIM // SSM_GROUPS
    gw = heads_per_group * SSM_HEAD_DIM

    @pl.when(c == 0)
    def _():
        state[...] = jnp.zeros_like(state)

    dt =jax.nn.softplus(dt_ref[...] + dtb_ref[...])
    a = -jnp.exp(alog_ref[...])
    adt = dt * a
    acs = jnp.dot(tril_ref[...], adt, preferred_element_type=F32, precision=HIGHEST)
    acs_t = acs.T
    acs_last = acs[chunk - 1:chunk, :]
    e_acs = jnp.exp(acs)
    e_last = jnp.exp(acs_last)
    decay_s = jnp.exp(acs_last - acs)
    li = lax.broadcasted_iota(jnp.int32, (chunk, chunk), 0)
    si = lax.broadcasted_iota(jnp.int32, (chunk, chunk), 1)
    causal = li >= si

    b_off = d_inner
    c_off = d_inner + SSM_GROUPS * SSM_STATE
    acc = None
    for g in range(SSM_GROUPS):
        bm = xbc_ref[:, b_off + g * SSM_STATE: b_off + (g + 1) * SSM_STATE]
        cm = xbc_ref[:, c_off + g * SSM_STATE: c_off + (g + 1) * SSM_STATE]
        xs_g = xbc_ref[:, g * gw:(g + 1) * gw].astype(F32)
        cb = _nt_dot(cm, bm)
        yd, xdt, ea_x, ds_x, el_x = [], [], [], [], []
        for jh in range(heads_per_group):
            h = g * heads_per_group + jh
            seg = jnp.exp(jnp.where(causal, acs[:, h:h + 1] - acs_t[h:h + 1, :], -jnp.inf))
            xdt_h = xs_g[:, jh * SSM_HEAD_DIM:(jh + 1) * SSM_HEAD_DIM] * dt[:, h:h + 1]
            yd.append(jnp.dot((cb * seg).astype(BF16), xdt_h.astype(BF16), preferred_element_type=F32))
            xdt.append(xdt_h)
            ea_x.append(jnp.broadcast_to(e_acs[:, h:h + 1], (chunk, SSM_HEAD_DIM)))
            ds_x.append(jnp.broadcast_to(decay_s[:, h:h + 1], (chunk, SSM_HEAD_DIM)))
            el_x.append(jnp.broadcast_to(e_last[:, h:h + 1], (1, SSM_HEAD_DIM)))
        y_diag = jnp.concatenate(yd, axis=-1)
        xdt_g = jnp.concatenate(xdt, axis=-1)
        st = state[g]
        y_off = jnp.dot(cm, st.astype(BF16), preferred_element_type=F32) * jnp.concatenate(ea_x, axis=-1)
        xdec = (xdt_g * jnp.concatenate(ds_x, axis=-1)).astype(BF16)
        state[g] = st * jnp.concatenate(el_x, axis=-1) + _tn_dot(bm, xdec)
        y = y_diag + y_off + dsk_ref[:, g * gw:(g + 1) * gw] * xs_g
        yz = y * _silu(z_ref[:, g * gw:(g + 1) * gw].astype(F32))
        yn = yz * lax.rsqrt(jnp.mean(yz * yz, axis=-1, keepdims=True) + EPS) * nw_ref[:, g * gw:(g + 1) * gw]
        part = jnp.dot(yn.astype(BF16), w_ref[g * gw:(g + 1) * gw, :], preferred_element_type=F32)
        acc = part if acc is None else acc + part
    o_ref[...] = acc


def _ssd_call(proj, dt_raw, dt_bias, a_log, d_skip, ssm_norm, w_ssm, bsz, seq, chunk):
    t = proj.shape[0]
    d_inner, d = w_ssm.shape
    conv_dim = d_inner + 2 * SSM_GROUPS * SSM_STATE
    n_heads = d_inner // SSM_HEAD_DIM
    nc = seq // chunk
    gw = d_inner // SSM_GROUPS
    pad_h = lambda v: jnp.pad(v.astype(F32), (0, LANES - n_heads)).reshape(1, LANES)
    tril = jnp.asarray(np.tril(np.ones((chunk, chunk), np.float32)))
    kern = functools.partial(_ssd_kernel, chunk=chunk, d_inner=d_inner)
    row = lambda b, c: b * nc + c
    z_blk = (2 * RET_HEADS * RET_QK_DIM + 2 * RET_HEADS * RET_V_DIM) // d_inner
    xbc_blk = (2 * RET_HEADS * RET_QK_DIM + 2 * RET_HEADS * RET_V_DIM + d_inner) // conv_dim
    full = lambda shape: pl.BlockSpec(shape, lambda b, c: (0,) * len(shape))
    return pl.pallas_call(
        kern,
        out_shape=jax.ShapeDtypeStruct((t, d), F32),
        grid=(bsz, nc),
        in_specs=[
            pl.BlockSpec((chunk, d_inner), lambda b, c: (row(b, c), z_blk)),
            pl.BlockSpec((chunk, conv_dim), lambda b, c: (row(b, c), xbc_blk)),
            pl.BlockSpec((chunk, LANES), lambda b, c: (row(b, c), 0)),
            full((1, LANES)), full((1, LANES)),
            full((1, d_inner)), full((1, d_inner)), full((chunk, chunk)), full((d_inner, d)),
        ],
        out_specs=pl.BlockSpec((chunk, d), lambda b, c: (row(b, c), 0)),
        scratch_shapes=[pltpu.VMEM((SSM_GROUPS, SSM_STATE, gw), F32)],
        compiler_params=_params(("arbitrary", "arbitrary")),
        name="ssd",
    )(proj, proj, dt_raw, pad_h(dt_bias), pad_h(a_log),
      jnp.repeat(d_skip.astype(F32), SSM_HEAD_DIM).reshape(1, d_inner), ssm_norm.reshape(1, d_inner),
      tril, w_ssm)


def _merge_kernel(ya_ref, yb_ref, ga_ref, gb_ref, x_ref, gm_ref, scf_ref, shf_ref, nw_ref, wo_ref,
                  wr_ref, br_ref, tri_ref,
                  x1_ref, h2_ref, idx_ref, rank_ref, prow_ref, cnt_ref, cnt_s, *, tm):
    i = pl.program_id(0)

    @pl.when(i == 0)
    def _():
        cnt_s[...] = jnp.zeros_like(cnt_s)

    merged = (jax.nn.sigmoid(ga_ref[...].astype(F32)) * ya_ref[...]
              + jax.nn.sigmoid(gb_ref[...].astype(F32)) * yb_ref[...])
    mo = jnp.dot(merged.astype(BF16), wo_ref[...], preferred_element_type=F32)
    x1 = x_ref[...] + gm_ref[0] * mo
    x1_ref[...] = x1
    ms = jnp.mean(x1 * x1, axis=-1, keepdims=True)
    h2 = x1 * lax.rsqrt(ms + EPS) * nw_ref[...] * (1.0 + scf_ref[0]) + shf_ref[0]
    h2_ref[...] = h2

    lg = _nt_dot(wr_ref[...], h2, precision=HIGHEST) + br_ref[...]
    sub = lax.broadcasted_iota(jnp.int32, lg.shape, 0)
    work = lg
    vals, idxs, sels = [], [], []
    for _ in range(TOP_K):
        m = jnp.max(work, axis=0, keepdims=True)
        ik = jnp.min(jnp.where(work == m, sub, N_EXPERTS), axis=0, keepdims=True)
        sel = sub == ik
        vals.append(m)
        idxs.append(ik)
        sels.append(sel)
        work = jnp.where(sel, -jnp.inf, work)
    exps = [jnp.exp(v - vals[0]) for v in vals]
    denom = exps[0]
    for e in exps[1:]:
        denom = denom + e
    probs = [e / denom for e in exps]

    base = cnt_s[:, 0:1]
    ranks = []
    for k in range(TOP_K):
        mk = jnp.where(sels[k], 1.0, 0.0)
        pre = jnp.dot(mk.astype(BF16), tri_ref[...], preferred_element_type=F32)
        ranks.append(jnp.sum(jnp.where(sels[k], pre + base, 0.0), axis=0, keepdims=True))
        base = base + jnp.sum(mk, axis=1, keepdims=True)
    cnt_s[...] = jnp.broadcast_to(base, cnt_s.shape)
    cnt_ref[...] = cnt_s[...].astype(jnp.int32)

    zi = jnp.zeros((SUBLANES - TOP_K, tm), jnp.int32)
    idx_ref[0] = jnp.concatenate(idxs + [zi], axis=0)
    rank_ref[0] = jnp.concatenate([r.astype(jnp.int32) for r in ranks] + [zi], axis=0)
    pt = jnp.concatenate(probs + [jnp.zeros((LANES - TOP_K, tm), F32)], axis=0)
    prow_ref[...] = pt.T


def _merge_call(ya, yb, proj, x2, mod3, norm_w, w_out, w_router_t, b_router, seq, tm):
    t, d = x2.shape
    nt = t // tm
    tiles_per_seq = seq // tm
    ga_blk = proj.shape[1] // d - 2
    tri = jnp.asarray(np.triu(np.ones((tm, tm), np.float32), 1), BF16)
    kern = functools.partial(_merge_kernel, tm=tm)
    modspec = lambda m: pl.BlockSpec((1, 1, d), lambda i: ((i // tiles_per_seq) * N_MOD + m, 0, 0))
    return pl.pallas_call(
        kern,
        out_shape=(jax.ShapeDtypeStruct((t, d), F32), jax.ShapeDtypeStruct((t, d), F32),
                   jax.ShapeDtypeStruct((nt, SUBLANES, tm), jnp.int32),
                   jax.ShapeDtypeStruct((nt, SUBLANES, tm), jnp.int32),
                   jax.ShapeDtypeStruct((t, LANES), F32),
                   jax.ShapeDtypeStruct((N_EXPERTS, LANES), jnp.int32)),
        grid=(nt,),
        in_specs=[
            pl.BlockSpec((tm, d), lambda i: (i, 0)),
            pl.BlockSpec((tm, d), lambda i: (i, 0)),
            pl.BlockSpec((tm, d), lambda i: (i, ga_blk)),
            pl.BlockSpec((tm, d), lambda i: (i, ga_blk + 1)),
            pl.BlockSpec((tm, d), lambda i: (i, 0)),
            modspec(2), modspec(4), modspec(3),
            pl.BlockSpec((1, d), lambda i: (0, 0)),
            pl.BlockSpec((d, d), lambda i: (0, 0)),
            pl.BlockSpec((N_EXPERTS, d), lambda i: (0, 0)),
            pl.BlockSpec((N_EXPERTS, 1), lambda i: (0, 0)),
            pl.BlockSpec((tm, tm), lambda i: (0, 0)),
        ],
        out_specs=(pl.BlockSpec((tm, d), lambda i: (i, 0)),
                   pl.BlockSpec((tm, d), lambda i: (i, 0)),
                   pl.BlockSpec((1, SUBLANES, tm), lambda i: (i, 0, 0)),
                   pl.BlockSpec((1, SUBLANES, tm), lambda i: (i, 0, 0)),
                   pl.BlockSpec((tm, LANES), lambda i: (i, 0)),
                   pl.BlockSpec((N_EXPERTS, LANES), lambda i: (0, 0))),
        scratch_shapes=[pltpu.VMEM((N_EXPERTS, LANES), F32)],
        compiler_params=_params(("arbitrary",)),
        name="merge",
    )(ya, yb, proj, proj, x2, mod3, mod3, mod3, norm_w, w_out, w_router_t,
      b_router.reshape(N_EXPERTS, 1), tri)


def _row_copy(src, src_row, dst, dst_row, sem):
    return pltpu.make_async_copy(
        src.at[pl.ds(pl.multiple_of(src_row * SUBLANES, SUBLANES), SUBLANES), :],
        dst.at[pl.ds(pl.multiple_of(dst_row * SUBLANES, SUBLANES), SUBLANES), :], sem)


def _dispatch_kernel(zs_ref, h2_ref, dest_ref, xs_ref, slab, zbuf, dsm, sem_rows, sem_s, sem_z,
                     *, tm, nt, bm, n_blocks):
    i = pl.program_id(0)
    slot = i % 2
    rows_per_tile = TOP_K * tm

    def wait_rows(s):
        pltpu.make_async_copy(xs_ref.at[pl.ds(0, rows_per_tile * SUBLANES), :],
                              xs_ref.at[pl.ds(0, rows_per_tile * SUBLANES), :], sem_rows.at[s]).wait()

    @pl.when(i == 0)
    def _():
        zbuf[...] = jnp.zeros_like(zbuf)
        for e in range(N_EXPERTS):
            fill = pltpu.make_async_copy(
                zbuf, xs_ref.at[pl.ds(pl.multiple_of(zs_ref[e] * SUBLANES, SUBLANES), bm * SUBLANES), :],
                sem_z)
            fill.start()
            fill.wait()

        def fill_tail(b, carry):
            fill = pltpu.make_async_copy(
                zbuf, xs_ref.at[pl.ds(pl.multiple_of(b * (bm * SUBLANES), SUBLANES), bm * SUBLANES), :], sem_z)
            fill.start()
            fill.wait()
            return carry

        lax.fori_loop(zs_ref[N_EXPERTS], n_blocks + 1, fill_tail, 0)

    @pl.when(i >= 2)
    def _():
        wait_rows(slot)

    cp = pltpu.make_async_copy(dest_ref.at[0], dsm, sem_s)
    cp.start()
    base = pl.multiple_of(slot * (tm * SUBLANES), SUBLANES)
    for s in range(SUBLANES):
        piece = h2_ref[:, s * LANES:(s + 1) * LANES]
        slab[pl.ds(base + s, tm, stride=SUBLANES), :] = piece
    cp.wait()

    def issue(r, carry):
        for k in range(TOP_K):
            _row_copy(slab, slot * tm + r, xs_ref, dsm[k, r], sem_rows.at[slot]).start(priority=k % 2)
        return carry

    lax.fori_loop(0, tm, issue, 0)

    @pl.when(i == nt - 1)
    def _():
        wait_rows(slot)
        if nt >= 2:
            wait_rows(1 - slot)


def _dispatch_call(h2, dest, zero_start, n_blocks, tm, bm):
    t, d = h2.shape
    nt = t // tm
    n_rows = (n_blocks + 1) * bm
    assert d == SUBLANES * LANES
    kern = functools.partial(_dispatch_kernel, tm=tm, nt=nt, bm=bm, n_blocks=n_blocks)
    gs = pltpu.PrefetchScalarGridSpec(
        num_scalar_prefetch=1,
        grid=(nt,),
        in_specs=[pl.BlockSpec((tm, d), lambda i, zs: (i, 0)),
                  pl.BlockSpec((1, SUBLANES, tm), lambda i, zs: (i, 0, 0))],
        out_specs=pl.BlockSpec(memory_space=pl.ANY),
        scratch_shapes=[pltpu.VMEM((2 * tm * SUBLANES, LANES), F32),
                        pltpu.VMEM((bm * SUBLANES, LANES), F32),
                        pltpu.SMEM((SUBLANES, tm), jnp.int32),
                        pltpu.SemaphoreType.DMA((2,)),
                        pltpu.SemaphoreType.DMA,
                        pltpu.SemaphoreType.DMA],
    )
    return pl.pallas_call(
        kern,
        out_shape=jax.ShapeDtypeStruct((n_rows * SUBLANES, LANES), F32),
        grid_spec=gs,
        compiler_params=pltpu.CompilerParams(dimension_semantics=("arbitrary",),
                                             vmem_limit_bytes=VMEM_LIMIT, has_side_effects=True),
        name="dispatch",
    )(zero_start, h2, dest)


def _ffn_kernel(be_ref, br_ref, bv_ref, x_ref, wgu_ref, bgu_ref, wd_ref, bd_ref, o_ref, *, bm, d_ff):
    i = pl.program_id(0)

    @pl.when(bv_ref[i] == 1)
    def _():
        x = jnp.concatenate([x_ref[pl.ds(s, bm, stride=SUBLANES), :] for s in range(SUBLANES)],
                            axis=-1).astype(BF16)
        gu = jnp.dot(x, wgu_ref[0], preferred_element_type=F32) + bgu_ref[0]
        gate = jnp.minimum(gu[:, :d_ff], SWIGLU_LIMIT)
        up = jnp.clip(gu[:, d_ff:], -SWIGLU_LIMIT, SWIGLU_LIMIT)
        act = gate * jax.nn.sigmoid(SWIGLU_ALPHA * gate) * (up + 1.0)
        y = jnp.dot(act.astype(BF16), wd_ref[0], preferred_element_type=F32) + bd_ref[0]
        for s in range(SUBLANES):
            o_ref[pl.ds(s, bm, stride=SUBLANES), :] = y[:, s * LANES:(s + 1) * LANES]

    @pl.when(bv_ref[i] == 0)
    def _():
        o_ref[...] = jnp.zeros_like(o_ref)


def _ffn_call(blk_e, blk_row, blk_valid, xs, w_gu, b_gu, w_d, b_d, bm):
    n_e, d, f2 = w_gu.shape
    d_ff = f2 // 2
    nb = blk_e.shape[0]
    kern = functools.partial(_ffn_kernel, bm=bm, d_ff=d_ff)
    gs = pltpu.PrefetchScalarGridSpec(
        num_scalar_prefetch=3,
        grid=(nb,),
        in_specs=[pl.BlockSpec((bm * SUBLANES, LANES), lambda i, be, br, bv: (br[i], 0)),
                  pl.BlockSpec((1, d, f2), lambda i, be, br, bv: (be[i], 0, 0)),
                  pl.BlockSpec((1, 1, f2), lambda i, be, br, bv: (be[i], 0, 0)),
                  pl.BlockSpec((1, d_ff, d), lambda i, be, br, bv: (be[i], 0, 0)),
                  pl.BlockSpec((1, 1, d), lambda i, be, br, bv: (be[i], 0, 0))],
        out_specs=pl.BlockSpec((bm * SUBLANES, LANES), lambda i, be, br, bv: (i, 0)),
    )
    return pl.pallas_call(
        kern,
        out_shape=jax.ShapeDtypeStruct((nb * bm * SUBLANES, LANES), F32),
        grid_spec=gs,
        compiler_params=_params(("arbitrary",)),
        name="ffn",
    )(blk_e, blk_row, blk_valid, xs, w_gu, b_gu.reshape(n_e, 1, f2), w_d, b_d.reshape(n_e, 1, d))


def _combine_kernel(dest_ref, dest_next_ref, prow_ref, x1_ref, gf_ref, nw_ref, ys_ref, o_ref,
                    gbuf, dsm, sem_rows, sem_s, *, tm, nt):
    i = pl.program_id(0)
    slot = i % 2
    rows_per_tile = TOP_K * tm

    def gather(dref, s):
        cp = pltpu.make_async_copy(dref.at[0], dsm, sem_s)
        cp.start()
        cp.wait()

        def issue(r, carry):
            for k in range(TOP_K):
                _row_copy(ys_ref, dsm[k, r], gbuf, s * rows_per_tile + k * tm + r,
                          sem_rows.at[s]).start(priority=k % 2)
            return carry

        lax.fori_loop(0, tm, issue, 0)

    @pl.when(i == 0)
    def _():
        gather(dest_ref, 0)

    @pl.when(i + 1 < nt)
    def _():
        gather(dest_next_ref, 1 - slot)

    pltpu.make_async_copy(ys_ref.at[pl.ds(0, rows_per_tile * SUBLANES), :],
                          gbuf.at[pl.ds(0, rows_per_tile * SUBLANES), :], sem_rows.at[slot]).wait()

    p = prow_ref[...]
    base = pl.multiple_of(slot * (rows_per_tile * SUBLANES), SUBLANES)
    for s in range(SUBLANES):
        moe = None
        for k in range(TOP_K):
            piece = gbuf[pl.ds(base + k * tm * SUBLANES + s, tm, stride=SUBLANES), :] * p[:, k:k + 1]
            moe = piece if moe is None else moe + piece
        sl = slice(s * LANES, (s + 1) * LANES)
        o_ref[:, sl] = x1_ref[:, sl] + gf_ref[0][:, sl] * moe
    xo = o_ref[...]
    o_ref[...] = xo * lax.rsqrt(jnp.mean(xo * xo, axis=-1, keepdims=True) + EPS) * nw_ref[...]


def _combine_call(dest, prow, x1, mod3, norm_final, ys, seq, tm):
    t, d = x1.shape
    nt = t // tm
    tiles_per_seq = seq // tm
    kern = functools.partial(_combine_kernel, tm=tm, nt=nt)
    return pl.pallas_call(
        kern,
        out_shape=jax.ShapeDtypeStruct((t, d), F32),
        grid=(nt,),
        in_specs=[pl.BlockSpec((1, SUBLANES, tm), lambda i: (i, 0, 0)),
                  pl.BlockSpec((1, SUBLANES, tm), lambda i: (jnp.minimum(i + 1, nt - 1), 0, 0)),
                  pl.BlockSpec((tm, LANES), lambda i: (i, 0)),
                  pl.BlockSpec((tm, d), lambda i: (i, 0)),
                  pl.BlockSpec((1, 1, d), lambda i: ((i // tiles_per_seq) * N_MOD + 5, 0, 0)),
                  pl.BlockSpec((1, d), lambda i: (0, 0)),
                  pl.BlockSpec(memory_space=pl.ANY)],
        out_specs=pl.BlockSpec((tm, d), lambda i: (i, 0)),
        scratch_shapes=[pltpu.VMEM((2 * TOP_K * tm * SUBLANES, LANES), F32),
                        pltpu.SMEM((SUBLANES, tm), jnp.int32),
                        pltpu.SemaphoreType.DMA((2,)),
                        pltpu.SemaphoreType.DMA],
        compiler_params=_params(("arbitrary",)),
        name="combine",
    )(dest, dest, prow, x1, mod3, norm_final, ys)


def _plan(seq):
    def fit(pref):
        tm = min(pref, seq)
        assert seq % tm == 0
        return tm
    return dict(tm_in=fit(1024), tm_merge=fit(512), tm_moe=fit(512),
                ret_chunk=fit(RET_CHUNK), ssm_chunk=fit(SSM_CHUNK))


def _layer(x2, mod3, bsz, seq, norm_mix, norm_ffn, w_in, conv_w, conv_b, dt_bias, a_log, d_skip, ssm_norm,
           w_ret_out, w_ssm_out, w_out, w_router, b_router, w_gate_up, b_gate_up, w_down, b_down,
           norm_final):
    t, d = x2.shape
    plan = _plan(seq)
    qk_w = RET_HEADS * RET_QK_DIM
    v_w = RET_HEADS * RET_V_DIM
    d_inner = w_ssm_out.shape[0]
    conv_dim = conv_w.shape[1]
    n_heads = d_inner // SSM_HEAD_DIM
    dt_off = 2 * qk_w + 2 * v_w + d_inner + conv_dim

    w_main = jnp.concatenate([w_in[:, :dt_off], w_in[:, dt_off + n_heads:]], axis=1).astype(BF16)
    w_dt = jnp.pad(w_in[:, dt_off:dt_off + n_heads], ((0, 0), (0, LANES - n_heads)))
    w_dt_hi = w_dt.astype(BF16)
    w_dt = jnp.concatenate([w_dt_hi, (w_dt - w_dt_hi.astype(F32)).astype(BF16)], axis=1)
    half = RET_QK_DIM // 2
    inv_freq = ROPE_BASE ** (-jnp.arange(half, dtype=F32) / half)
    ang = jnp.arange(seq, dtype=F32)[:, None] * inv_freq[None, :]
    cos, sin = jnp.cos(ang), jnp.sin(ang)

    proj, dt_raw = _inproj_call(x2, norm_mix.reshape(1, d), mod3, cos, sin, w_main, w_dt, conv_w, conv_b,
                                2 * qk_w + 2 * v_w + d_inner, seq, plan["tm_in"], 2 * qk_w)
    ya = _retention_call(proj, w_ret_out.astype(BF16), bsz, seq, plan["ret_chunk"])
    yb = _ssd_call(proj, dt_raw, dt_bias, a_log, d_skip, ssm_norm,
                   w_ssm_out.astype(BF16), bsz, seq, plan["ssm_chunk"])
    tm = plan["tm_merge"]
    x1, h2, idx, rank, prow, cnt = _merge_call(ya, yb, proj, x2, mod3, norm_ffn.reshape(1, d),
                                               w_out.astype(BF16), w_router.T, b_router, seq, tm)

    bm = FFN_BLOCK
    counts = cnt[:, 0]
    padded = ((counts + bm - 1) // bm) * bm
    pad_end = jnp.cumsum(padded)
    start_pad = pad_end - padded
    n_blocks = (t * TOP_K) // bm + N_EXPERTS
    e_ids = jnp.arange(N_EXPERTS, dtype=jnp.int32)[:, None, None, None]
    dest = rank + jnp.sum(jnp.where(idx[None] == e_ids, start_pad[:, None, None, None], 0), axis=0)
    dest = dest.astype(jnp.int32)
    n_real = pad_end[-1] // bm
    blk_valid = (jnp.arange(n_blocks) < n_real).astype(jnp.int32)
    blk_row = jnp.minimum(jnp.arange(n_blocks), n_real - 1).astype(jnp.int32)
    blk_e = jnp.minimum(jnp.sum(pad_end[None, :] <= (blk_row * bm)[:, None], axis=1),
                        N_EXPERTS - 1).astype(jnp.int32)

    zero_start = jnp.concatenate([start_pad + counts, n_real[None]]).astype(jnp.int32)
    xs = _dispatch_call(h2, dest, zero_start, n_blocks, plan["tm_moe"], bm)
    ys = _ffn_call(blk_e, blk_row, blk_valid, xs, w_gate_up.astype(BF16), b_gate_up,
                   w_down.astype(BF16), b_down, bm)
    return _combine_call(dest, prow, x1, mod3, norm_final.reshape(1, d), ys, seq, plan["tm_moe"])


def kernel(x, c, w_ada, b_ada, norm_mix, norm_ffn, w_in, conv_w, conv_b, dt_bias, a_log, d_skip, ssm_norm,
           w_ret_out, w_ssm_out, w_out, w_router, b_router, w_gate_up, b_gate_up, w_down, b_down, norm_final):
    bsz, seq, d = x.shape
    depth = w_ada.shape[0]
    assert depth == 1, "the final norm is fused into the single layer's last kernel"
    x2 = x.reshape(bsz * seq, d)
    l = 0
    mod = _mod_call(c, w_ada[l], b_ada[l])
    mod3 = mod.reshape(bsz * N_MOD, 1, d)
    out = _layer(x2, mod3, bsz, seq, norm_mix[l], norm_ffn[l], w_in[l], conv_w[l], conv_b[l], dt_bias[l],
                 a_log[l], d_skip[l], ssm_norm[l], w_ret_out[l], w_ssm_out[l], w_out[l], w_router[l],
                 b_router[l], w_gate_up[l], b_gate_up[l], w_down[l], b_down[l], norm_final)
    return out.reshape(bsz, seq, d)
```

```python
import functools
import math

import numpy as np
import jax
import jax.numpy as jnp
from jax import lax
from jax.experimental import pallas as pl
from jax.experimental.pallas import tpu as pltpu

F32 = jnp.float32
BF16 = jnp.bfloat16
HIGHEST = lax.Precision.HIGHEST

EPS = 1e-6
N_MOD = 6
RET_HEADS = 4
RET_QK_DIM = 256
RET_V_DIM = 512
ROPE_BASE = 10000.0
SSM_HEAD_DIM = 64
SSM_GROUPS = 8
SSM_STATE = 128
SSM_CONV = 4
N_EXPERTS = 32
TOP_K = 4
SWIGLU_LIMIT = 7.0
SWIGLU_ALPHA = 1.702

LANES = 128
SUBLANES = 8
VMEM_LIMIT = 56 * 1024 * 1024

RET_CHUNK = 128
SSM_CHUNK = 128
FFN_BLOCK = 256


def _params(sem, vmem=VMEM_LIMIT):
    return pltpu.CompilerParams(dimension_semantics=sem, vmem_limit_bytes=vmem)


def _nt_dot(a, b, **kw):
    return lax.dot_general(a, b, (((1,), (1,)), ((), ())), preferred_element_type=F32, **kw)


def _tn_dot(a, b, **kw):
    return lax.dot_general(a, b, (((0,), (0,)), ((), ())), preferred_element_type=F32, **kw)


def _silu(v):
    return v * jax.nn.sigmoid(v)


def _mod_kernel(c_ref, w_ref, b_ref, o_ref):
    cond = _silu(c_ref[...])
    o_ref[...] = jnp.dot(cond, w_ref[...], preferred_element_type=F32, precision=HIGHEST) + b_ref[...]


def _mod_call(c, w_ada, b_ada):
    bsz, d = c.shape
    n = w_ada.shape[1]
    return pl.pallas_call(
        _mod_kernel,
        out_shape=jax.ShapeDtypeStruct((bsz, n), F32),
        grid=(n // d,),
        in_specs=[pl.BlockSpec((bsz, d), lambda j: (0, 0)),
                  pl.BlockSpec((d, d), lambda j: (0, j)),
                  pl.BlockSpec((1, d), lambda j: (0, j))],
        out_specs=pl.BlockSpec((bsz, d), lambda j: (0, j)),
        compiler_params=_params(("arbitrary",)),
        name="mod",
    )(c, w_ada, b_ada.reshape(1, n))


def _inproj_kernel(x_ref, nw_ref, sc_ref, sh_ref, cos_ref, sin_ref, w_ref, wdt_ref, cw_ref, cb_ref,
                   o_ref, dt_ref, h_s, work, carry, *, conv_j0, conv_nj, tiles_per_seq, tm, tn, sub):
    i = pl.program_id(0)
    j = pl.program_id(1)
    n_dt = dt_ref.shape[1]

    @pl.when(j == 0)
    def _():
        xf = x_ref[...]
        ms = jnp.mean(xf * xf, axis=-1, keepdims=True)
        y = xf * lax.rsqrt(ms + EPS) * nw_ref[...]
        hm = y * (1.0 + sc_ref[0]) + sh_ref[0]
        hb = hm.astype(BF16)
        h_s[...] = hb
        h_lo = (hm - hb.astype(F32)).astype(BF16)
        d_hi = jnp.dot(hb, wdt_ref[...], preferred_element_type=F32)
        d_lo = jnp.dot(h_lo, wdt_ref[:, :n_dt], preferred_element_type=F32)
        dt_ref[...] = d_hi[:, :n_dt] + d_hi[:, n_dt:] + d_lo
        cos = cos_ref[...]
        sin = sin_ref[...]
        half = RET_QK_DIM // 2
        for p in range(tn // sub):
            acc = jnp.dot(hb, w_ref[:, p * sub:(p + 1) * sub], preferred_element_type=F32)
            for cc in range(sub // RET_QK_DIM):
                c = p * (sub // RET_QK_DIM) + cc
                a = acc[:, cc * RET_QK_DIM: cc * RET_QK_DIM + half]
                b = acc[:, cc * RET_QK_DIM + half: (cc + 1) * RET_QK_DIM]
                scale = 1.0 if c < RET_HEADS else RET_QK_DIM ** -0.5
                o_ref[:, c * RET_QK_DIM: c * RET_QK_DIM + half] = ((a * cos - b * sin) * scale).astype(BF16)
                o_ref[:, c * RET_QK_DIM + half: (c + 1) * RET_QK_DIM] = ((a * sin + b * cos) * scale).astype(BF16)

    is_conv = (j >= conv_j0) & (j < conv_j0 + conv_nj)

    @pl.when(is_conv)
    def _():
        cj = j - conv_j0
        pad = SUBLANES

        @pl.when(i % tiles_per_seq == 0)
        def _():
            carry[cj] = jnp.zeros(carry.shape[1:], F32)

        for p in range(tn // sub):
            acc = jnp.dot(h_s[...], w_ref[:, p * sub:(p + 1) * sub], preferred_element_type=F32)
            for cc in range(sub // LANES):
                c = p * (sub // LANES) + cc
                cols = slice(c * LANES, (c + 1) * LANES)
                work[c, 0:pad, :] = carry[cj, c]
                work[c, pad:pad + tm, :] = acc[:, cc * LANES:(cc + 1) * LANES]
                conv = cb_ref[:, cols] + cw_ref[SSM_CONV - 1:SSM_CONV, cols] * work[c, pad:pad + tm, :]
                for k in range(SSM_CONV - 1):
                    shift = SSM_CONV - 1 - k
                    conv = conv + cw_ref[k:k + 1, cols] * work[c, pad - shift:pad - shift + tm, :]
                carry[cj, c] = work[c, tm:tm + pad, :]
                o_ref[:, cols] = _silu(conv).astype(BF16)

    @pl.when((j != 0) & jnp.logical_not(is_conv))
    def _():
        for p in range(tn // sub):
            o_ref[:, p * sub:(p + 1) * sub] = jnp.dot(
                h_s[...], w_ref[:, p * sub:(p + 1) * sub], preferred_element_type=F32).astype(BF16)


def _inproj_call(x2, norm_w, mod3, cos, sin, w_main, w_dt, conv_w, conv_b, conv_off, seq, tm, tn):
    t, d = x2.shape
    n = w_main.shape[1]
    conv_dim = conv_w.shape[1]
    tiles_per_seq = seq // tm
    assert tn == 2 * RET_HEADS * RET_QK_DIM, "rotary epilogue expects q and k in the first column tile"
    assert conv_off % tn == 0 and conv_dim % tn == 0
    conv_j0, conv_nj = conv_off // tn, conv_dim // tn
    sub = 512
    kern = functools.partial(_inproj_kernel, conv_j0=conv_j0, conv_nj=conv_nj,
                             tiles_per_seq=tiles_per_seq, tm=tm, tn=tn, sub=sub)
    conv_idx = lambda i, j: (0, jnp.clip(j - conv_j0, 0, conv_nj - 1))
    return pl.pallas_call(
        kern,
        out_shape=(jax.ShapeDtypeStruct((t, n), BF16), jax.ShapeDtypeStruct((t, LANES), F32)),
        grid=(t // tm, n // tn),
        in_specs=[
            pl.BlockSpec((tm, d), lambda i, j: (i, 0)),
            pl.BlockSpec((1, d), lambda i, j: (0, 0)),
            pl.BlockSpec((1, 1, d), lambda i, j: ((i // tiles_per_seq) * N_MOD + 1, 0, 0)),
            pl.BlockSpec((1, 1, d), lambda i, j: ((i // tiles_per_seq) * N_MOD + 0, 0, 0)),
            pl.BlockSpec((tm, LANES), lambda i, j: (i % tiles_per_seq, 0)),
            pl.BlockSpec((tm, LANES), lambda i, j: (i % tiles_per_seq, 0)),
            pl.BlockSpec((d, tn), lambda i, j: (0, j)),
            pl.BlockSpec((d, 2 * LANES), lambda i, j: (0, 0)),
            pl.BlockSpec((SSM_CONV, tn), conv_idx),
            pl.BlockSpec((1, tn), conv_idx),
        ],
        out_specs=(pl.BlockSpec((tm, tn), lambda i, j: (i, j)),
                   pl.BlockSpec((tm, LANES), lambda i, j: (i, 0))),
        scratch_shapes=[pltpu.VMEM((tm, d), BF16),
                        pltpu.VMEM((tn // LANES, tm + SUBLANES, LANES), F32),
                        pltpu.VMEM((conv_nj, tn // LANES, SUBLANES, LANES), F32)],
        compiler_params=_params(("arbitrary", "arbitrary")),
        name="inproj",
    )(x2, norm_w, mod3, mod3, cos, sin, w_main, w_dt, conv_w, conv_b.reshape(1, conv_dim))


def _retention_kernel(q_ref, k_ref, v_ref, g_ref, din_ref, dq_ref, dk_ref, w_ref, o_ref, state,
                      *, decay_c):
    c = pl.program_id(1)

    @pl.when(c == 0)
    def _():
        state[...] = jnp.zeros_like(state)

    acc = None
    for h in range(RET_HEADS):
        qh = q_ref[:, h * RET_QK_DIM:(h + 1) * RET_QK_DIM]
        kh = k_ref[:, h * RET_QK_DIM:(h + 1) * RET_QK_DIM]
        vh = v_ref[:, h * RET_V_DIM:(h + 1) * RET_V_DIM]
        scores = _nt_dot(qh, kh) * din_ref[h]
        inner = jnp.dot(scores.astype(BF16), vh, preferred_element_type=F32)
        st = state[h]
        cross = jnp.dot(qh, st.astype(BF16), preferred_element_type=F32) * dq_ref[h]
        kd = (kh.astype(F32) * dk_ref[h]).astype(BF16)
        state[h] = st * decay_c[h] + _tn_dot(kd, vh)
        ret = inner + cross
        ret = ret * lax.rsqrt(jnp.mean(ret * ret, axis=-1, keepdims=True) + EPS)
        gh = g_ref[:, h * RET_V_DIM:(h + 1) * RET_V_DIM].astype(F32)
        ret = ret * _silu(gh)
        part = jnp.dot(ret.astype(BF16), w_ref[h * RET_V_DIM:(h + 1) * RET_V_DIM, :],
                       preferred_element_type=F32)
        acc = part if acc is None else acc + part
    o_ref[...] = acc


def _retention_tables(chunk):
    lg = np.log(1.0 - 2.0 ** (-5.0 - np.arange(RET_HEADS, dtype=np.float64)))
    idx = np.arange(chunk, dtype=np.float64)
    rel = idx[:, None] - idx[None, :]
    causal = rel >= 0
    din = np.where(causal[None], np.exp(np.where(causal, rel, 0.0)[None] * lg[:, None, None]), 0.0)
    dq = np.exp((idx + 1.0)[None, :, None] * lg[:, None, None])
    dk = np.exp((chunk - 1.0 - idx)[None, :, None] * lg[:, None, None])
    dc = tuple(float(v) for v in np.exp(chunk * lg))
    return (jnp.asarray(din, F32), jnp.asarray(dq, F32), jnp.asarray(dk, F32), dc)


def _retention_call(proj, w_ret, bsz, seq, chunk):
    t = proj.shape[0]
    d = w_ret.shape[1]
    nc = seq // chunk
    qk_w = RET_HEADS * RET_QK_DIM
    v_w = RET_HEADS * RET_V_DIM
    din, dq, dk, dc = _retention_tables(chunk)
    kern = functools.partial(_retention_kernel, decay_c=dc)
    row = lambda b, c: b * nc + c
    return pl.pallas_call(
        kern,
        out_shape=jax.ShapeDtypeStruct((t, d), F32),
        grid=(bsz, nc),
        in_specs=[
            pl.BlockSpec((chunk, qk_w), lambda b, c: (row(b, c), 0)),
            pl.BlockSpec((chunk, qk_w), lambda b, c: (row(b, c), 1)),
            pl.BlockSpec((chunk, v_w), lambda b, c: (row(b, c), 1)),
            pl.BlockSpec((chunk, v_w), lambda b, c: (row(b, c), 2)),
            pl.BlockSpec((RET_HEADS, chunk, chunk), lambda b, c: (0, 0, 0)),
            pl.BlockSpec((RET_HEADS, chunk, 1), lambda b, c: (0, 0, 0)),
            pl.BlockSpec((RET_HEADS, chunk, 1), lambda b, c: (0, 0, 0)),
            pl.BlockSpec((v_w, d), lambda b, c: (0, 0)),
        ],
        out_specs=pl.BlockSpec((chunk, d), lambda b, c: (row(b, c), 0)),
        scratch_shapes=[pltpu.VMEM((RET_HEADS, RET_QK_DIM, RET_V_DIM), F32)],
        compiler_params=_params(("arbitrary", "arbitrary")),
        name="retention",
    )(proj, proj, proj, proj, din, dq, dk, w_ret)


def _ssd_kernel(z_ref, xbc_ref, dt_ref, dtb_ref, alog_ref, dsk_ref, nw_ref,
                tril_ref, exp_ref, w_ref, o_ref, state, yn_s, *, chunk, d_inner):
    c = pl.program_id(1)
    heads_per_group = d_inner // SSM_HEAD_DIM // SSM_GROUPS
    gw = heads_per_group * SSM_HEAD_DIM
    assert SSM_HEAD_DIM * 2 == LANES and gw == 2 * LANES

    @pl.when(c == 0)
    def _():
        state[...] = jnp.zeros_like(state)

    dt = jax.nn.softplus(dt_ref[...] + dtb_ref[...])
    a = -jnp.exp(alog_ref[...])
    adt = dt * a
    acs = jnp.dot(tril_ref[...], adt, preferred_element_type=F32, precision=HIGHEST)
    acs_t = acs.T
    dt_x = jnp.dot(dt.astype(BF16), exp_ref[...], preferred_element_type=F32)
    li = lax.broadcasted_iota(jnp.int32, (chunk, chunk), 0)
    si = lax.broadcasted_iota(jnp.int32, (chunk, chunk), 1)
    causal = li >= si
    low_half = si < SSM_HEAD_DIM
    lane_g = lax.broadcasted_iota(jnp.int32, (chunk, gw), 1)

    b_off = d_inner
    c_off = d_inner + SSM_GROUPS * SSM_STATE
    for g in range(SSM_GROUPS):
        bm = xbc_ref[:, b_off + g * SSM_STATE: b_off + (g + 1) * SSM_STATE]
        cm = xbc_ref[:, c_off + g * SSM_STATE: c_off + (g + 1) * SSM_STATE]
        xs_g = xbc_ref[:, g * gw:(g + 1) * gw].astype(F32)
        xdt_g = xs_g * dt_x[:, g * gw:(g + 1) * gw]
        cb = _nt_dot(cm, bm)
        cols, ms, xm = [], [], []
        for jh in range(heads_per_group):
            h = g * heads_per_group + jh
            col = jnp.broadcast_to(acs[:, h:h + 1], (chunk, chunk))
            seg = jnp.exp(jnp.where(causal, col - acs_t[h:h + 1, :], -jnp.inf))
            cols.append(col)
            ms.append((cb * seg).astype(BF16))
            in_head = (lane_g >= jh * SSM_HEAD_DIM) & (lane_g < (jh + 1) * SSM_HEAD_DIM)
            xm.append(jnp.where(in_head, xdt_g, 0.0).astype(BF16))
        y_diag = jnp.dot(jnp.concatenate(ms, axis=-1), jnp.concatenate(xm, axis=0),
                         preferred_element_type=F32)
        a_x = jnp.concatenate([jnp.where(low_half, cols[0], cols[1]),
                               jnp.where(low_half, cols[2], cols[3])], axis=-1)
        e_acs_x = jnp.exp(a_x)
        a_last_x = a_x[chunk - 1:chunk, :]
        st = state[g]
        y_off = jnp.dot(cm, st.astype(BF16), preferred_element_type=F32) * e_acs_x
        xdec = (xdt_g * jnp.exp(a_last_x - a_x)).astype(BF16)
        state[g] = st * e_acs_x[chunk - 1:chunk, :] + _tn_dot(bm, xdec)
        y = y_diag + y_off + dsk_ref[:, g * gw:(g + 1) * gw] * xs_g
        yz = y * _silu(z_ref[:, g * gw:(g + 1) * gw].astype(F32))
        yn = yz * lax.rsqrt(jnp.mean(yz * yz, axis=-1, keepdims=True) + EPS) * nw_ref[:, g * gw:(g + 1) * gw]
        yn_s[:, g * gw:(g + 1) * gw] = yn.astype(BF16)
    o_ref[...] = jnp.dot(yn_s[...], w_ref[...], preferred_element_type=F32)


def _ssd_call(proj, dt_raw, dt_bias, a_log, d_skip, ssm_norm, w_ssm, bsz, seq, chunk):
    t = proj.shape[0]
    d_inner, d = w_ssm.shape
    conv_dim = d_inner + 2 * SSM_GROUPS * SSM_STATE
    n_heads = d_inner // SSM_HEAD_DIM
    nc = seq // chunk
    gw = d_inner // SSM_GROUPS
    pad_h = lambda v: jnp.pad(v.astype(F32), (0, LANES - n_heads)).reshape(1, LANES)
    assert chunk == LANES, "the per-head decay tiles are built lane-for-lane against the chunk"
    tril = jnp.asarray(np.tril(np.ones((chunk, chunk), np.float32)))
    expand = np.zeros((LANES, d_inner), np.float32)
    for h in range(n_heads):
        expand[h, h * SSM_HEAD_DIM:(h + 1) * SSM_HEAD_DIM] = 1.0
    expand = jnp.asarray(expand, BF16)
    kern = functools.partial(_ssd_kernel, chunk=chunk, d_inner=d_inner)
    row = lambda b, c: b * nc + c
    z_blk = (2 * RET_HEADS * RET_QK_DIM + 2 * RET_HEADS * RET_V_DIM) // d_inner
    xbc_blk = (2 * RET_HEADS * RET_QK_DIM + 2 * RET_HEADS * RET_V_DIM + d_inner) // conv_dim
    full = lambda shape: pl.BlockSpec(shape, lambda b, c: (0,) * len(shape))
    return pl.pallas_call(
        kern,
        out_shape=jax.ShapeDtypeStruct((t, d), F32),
        grid=(bsz, nc),
        in_specs=[
            pl.BlockSpec((chunk, d_inner), lambda b, c: (row(b, c), z_blk)),
            pl.BlockSpec((chunk, conv_dim), lambda b, c: (row(b, c), xbc_blk)),
            pl.BlockSpec((chunk, LANES), lambda b, c: (row(b, c), 0)),
            full((1, LANES)), full((1, LANES)),
            full((1, d_inner)), full((1, d_inner)), full((chunk, chunk)), full((LANES, d_inner)),
            full((d_inner, d)),
        ],
        out_specs=pl.BlockSpec((chunk, d), lambda b, c: (row(b, c), 0)),
        scratch_shapes=[pltpu.VMEM((SSM_GROUPS, SSM_STATE, gw), F32),
                        pltpu.VMEM((chunk, d_inner), BF16)],
        compiler_params=_params(("arbitrary", "arbitrary")),
        name="ssd",
    )(proj, proj, dt_raw, pad_h(dt_bias), pad_h(a_log),
      jnp.repeat(d_skip.astype(F32), SSM_HEAD_DIM).reshape(1, d_inner), ssm_norm.reshape(1, d_inner),
      tril, expand, w_ssm)


def _merge_kernel(ya_ref, yb_ref, ga_ref, gb_ref, x_ref, gm_ref, scf_ref, shf_ref, nw_ref, wo_ref,
                  wr_ref, br_ref, tri_ref,
                  x1_ref, h2_ref, idx_ref, rank_ref, prow_ref, cnt_ref, cnt_s, *, tm):
    i = pl.program_id(0)

    @pl.when(i == 0)
    def _():
        cnt_s[...] = jnp.zeros_like(cnt_s)

    merged = (jax.nn.sigmoid(ga_ref[...].astype(F32)) * ya_ref[...]
              + jax.nn.sigmoid(gb_ref[...].astype(F32)) * yb_ref[...])
    mo = jnp.dot(merged.astype(BF16), wo_ref[...], preferred_element_type=F32)
    x1 = x_ref[...] + gm_ref[0] * mo
    x1_ref[...] = x1
    ms = jnp.mean(x1 * x1, axis=-1, keepdims=True)
    h2 = x1 * lax.rsqrt(ms + EPS) * nw_ref[...] * (1.0 + scf_ref[0]) + shf_ref[0]
    h2_ref[...] = h2

    lg = _nt_dot(wr_ref[...], h2, precision=HIGHEST) + br_ref[...]
    sub = lax.broadcasted_iota(jnp.int32, lg.shape, 0)
    work = lg
    vals, idxs, sels = [], [], []
    for _ in range(TOP_K):
        m = jnp.max(work, axis=0, keepdims=True)
        ik = jnp.min(jnp.where(work == m, sub, N_EXPERTS), axis=0, keepdims=True)
        sel = sub == ik
        vals.append(m)
        idxs.append(ik)
        sels.append(sel)
        work = jnp.where(sel, -jnp.inf, work)
    exps = [jnp.exp(v - vals[0]) for v in vals]
    denom = exps[0]
    for e in exps[1:]:
        denom = denom + e
    probs = [e / denom for e in exps]

    base = cnt_s[:, 0:1]
    ranks = []
    for k in range(TOP_K):
        mk = jnp.where(sels[k], 1.0, 0.0)
        pre = jnp.dot(mk.astype(BF16), tri_ref[...], preferred_element_type=F32)
        ranks.append(jnp.sum(jnp.where(sels[k], pre + base, 0.0), axis=0, keepdims=True))
        base = base + jnp.sum(mk, axis=1, keepdims=True)
    cnt_s[...] = jnp.broadcast_to(base, cnt_s.shape)
    cnt_ref[...] = cnt_s[...].astype(jnp.int32)

    zi = jnp.zeros((SUBLANES - TOP_K, tm), jnp.int32)
    idx_ref[0] = jnp.concatenate(idxs + [zi], axis=0)
    rank_ref[0] = jnp.concatenate([r.astype(jnp.int32) for r in ranks] + [zi], axis=0)
    pt = jnp.concatenate(probs + [jnp.zeros((LANES - TOP_K, tm), F32)], axis=0)
    prow_ref[...] = pt.T


def _merge_call(ya, yb, proj, x2, mod3, norm_w, w_out, w_router_t, b_router, seq, tm):
    t, d = x2.shape
    nt = t // tm
    tiles_per_seq = seq // tm
    ga_blk = proj.shape[1] // d - 2
    tri = jnp.asarray(np.triu(np.ones((tm, tm), np.float32), 1), BF16)
    kern = functools.partial(_merge_kernel, tm=tm)
    modspec = lambda m: pl.BlockSpec((1, 1, d), lambda i: ((i // tiles_per_seq) * N_MOD + m, 0, 0))
    return pl.pallas_call(
        kern,
        out_shape=(jax.ShapeDtypeStruct((t, d), F32), jax.ShapeDtypeStruct((t, d), F32),
                   jax.ShapeDtypeStruct((nt, SUBLANES, tm), jnp.int32),
                   jax.ShapeDtypeStruct((nt, SUBLANES, tm), jnp.int32),
                   jax.ShapeDtypeStruct((t, LANES), F32),
                   jax.ShapeDtypeStruct((N_EXPERTS, LANES), jnp.int32)),
        grid=(nt,),
        in_specs=[
            pl.BlockSpec((tm, d), lambda i: (i, 0)),
            pl.BlockSpec((tm, d), lambda i: (i, 0)),
            pl.BlockSpec((tm, d), lambda i: (i, ga_blk)),
            pl.BlockSpec((tm, d), lambda i: (i, ga_blk + 1)),
            pl.BlockSpec((tm, d), lambda i: (i, 0)),
            modspec(2), modspec(4), modspec(3),
            pl.BlockSpec((1, d), lambda i: (0, 0)),
            pl.BlockSpec((d, d), lambda i: (0, 0)),
            pl.BlockSpec((N_EXPERTS, d), lambda i: (0, 0)),
            pl.BlockSpec((N_EXPERTS, 1), lambda i: (0, 0)),
            pl.BlockSpec((tm, tm), lambda i: (0, 0)),
        ],
        out_specs=(pl.BlockSpec((tm, d), lambda i: (i, 0)),
                   pl.BlockSpec((tm, d), lambda i: (i, 0)),
                   pl.BlockSpec((1, SUBLANES, tm), lambda i: (i, 0, 0)),
                   pl.BlockSpec((1, SUBLANES, tm), lambda i: (i, 0, 0)),
                   pl.BlockSpec((tm, LANES), lambda i: (i, 0)),
                   pl.BlockSpec((N_EXPERTS, LANES), lambda i: (0, 0))),
        scratch_shapes=[pltpu.VMEM((N_EXPERTS, LANES), F32)],
        compiler_params=_params(("arbitrary",)),
        name="merge",
    )(ya, yb, proj, proj, x2, mod3, mod3, mod3, norm_w, w_out, w_router_t,
      b_router.reshape(N_EXPERTS, 1), tri)


def _row_copy(src, src_row, dst, dst_row, sem):
    return pltpu.make_async_copy(
        src.at[pl.ds(pl.multiple_of(src_row * SUBLANES, SUBLANES), SUBLANES), :],
        dst.at[pl.ds(pl.multiple_of(dst_row * SUBLANES, SUBLANES), SUBLANES), :], sem)


def _dispatch_kernel(zs_ref, h2_ref, dest_ref, xs_ref, slab, zbuf, dsm, sem_rows, sem_s, sem_z,
                     *, tm, nt, bm, n_blocks):
    i = pl.program_id(0)
    slot = i % 2
    rows_per_tile = TOP_K * tm

    def wait_rows(s):
        pltpu.make_async_copy(xs_ref.at[pl.ds(0, rows_per_tile * SUBLANES), :],
                              xs_ref.at[pl.ds(0, rows_per_tile * SUBLANES), :], sem_rows.at[s]).wait()

    @pl.when(i == 0)
    def _():
        zbuf[...] = jnp.zeros_like(zbuf)
        for e in range(N_EXPERTS):
            fill = pltpu.make_async_copy(
                zbuf, xs_ref.at[pl.ds(pl.multiple_of(zs_ref[e] * SUBLANES, SUBLANES), bm * SUBLANES), :],
                sem_z)
            fill.start()
            fill.wait()

        def fill_tail(b, carry):
            fill = pltpu.make_async_copy(
                zbuf, xs_ref.at[pl.ds(pl.multiple_of(b * (bm * SUBLANES), SUBLANES), bm * SUBLANES), :], sem_z)
            fill.start()
            fill.wait()
            return carry

        lax.fori_loop(zs_ref[N_EXPERTS], n_blocks + 1, fill_tail, 0)

    @pl.when(i >= 2)
    def _():
        wait_rows(slot)

    cp = pltpu.make_async_copy(dest_ref.at[0], dsm, sem_s)
    cp.start()
    base = pl.multiple_of(slot * (tm * SUBLANES), SUBLANES)
    for s in range(SUBLANES):
        piece = h2_ref[:, s * LANES:(s + 1) * LANES]
        slab[pl.ds(base + s, tm, stride=SUBLANES), :] = piece
    cp.wait()

    def issue(r, carry):
        for k in range(TOP_K):
            _row_copy(slab, slot * tm + r, xs_ref, dsm[k, r], sem_rows.at[slot]).start(priority=k % 2)
        return carry

    lax.fori_loop(0, tm, issue, 0)

    @pl.when(i == nt - 1)
    def _():
        wait_rows(slot)
        if nt >= 2:
            wait_rows(1 - slot)


def _dispatch_call(h2, dest, zero_start, n_blocks, tm, bm):
    t, d = h2.shape
    nt = t // tm
    n_rows = (n_blocks + 1) * bm
    assert d == SUBLANES * LANES
    kern = functools.partial(_dispatch_kernel, tm=tm, nt=nt, bm=bm, n_blocks=n_blocks)
    gs = pltpu.PrefetchScalarGridSpec(
        num_scalar_prefetch=1,
        grid=(nt,),
        in_specs=[pl.BlockSpec((tm, d), lambda i, zs: (i, 0)),
                  pl.BlockSpec((1, SUBLANES, tm), lambda i, zs: (i, 0, 0))],
        out_specs=pl.BlockSpec(memory_space=pl.ANY),
        scratch_shapes=[pltpu.VMEM((2 * tm * SUBLANES, LANES), F32),
                        pltpu.VMEM((bm * SUBLANES, LANES), F32),
                        pltpu.SMEM((SUBLANES, tm), jnp.int32),
                        pltpu.SemaphoreType.DMA((2,)),
                        pltpu.SemaphoreType.DMA,
                        pltpu.SemaphoreType.DMA],
    )
    return pl.pallas_call(
        kern,
        out_shape=jax.ShapeDtypeStruct((n_rows * SUBLANES, LANES), F32),
        grid_spec=gs,
        compiler_params=pltpu.CompilerParams(dimension_semantics=("arbitrary",),
                                             vmem_limit_bytes=VMEM_LIMIT, has_side_effects=True),
        name="dispatch",
    )(zero_start, h2, dest)


def _ffn_kernel(be_ref, br_ref, bv_ref, x_ref, wgu_ref, bgu_ref, wd_ref, bd_ref, o_ref, *, bm, d_ff):
    i = pl.program_id(0)

    @pl.when(bv_ref[i] == 1)
    def _():
        x = jnp.concatenate([x_ref[pl.ds(s, bm, stride=SUBLANES), :] for s in range(SUBLANES)],
                            axis=-1).astype(BF16)
        gu = jnp.dot(x, wgu_ref[0], preferred_element_type=F32) + bgu_ref[0]
        gate = jnp.minimum(gu[:, :d_ff], SWIGLU_LIMIT)
        up = jnp.clip(gu[:, d_ff:], -SWIGLU_LIMIT, SWIGLU_LIMIT)
        act = gate * jax.nn.sigmoid(SWIGLU_ALPHA * gate) * (up + 1.0)
        y = jnp.dot(act.astype(BF16), wd_ref[0], preferred_element_type=F32) + bd_ref[0]
        for s in range(SUBLANES):
            o_ref[pl.ds(s, bm, stride=SUBLANES), :] = y[:, s * LANES:(s + 1) * LANES]

    @pl.when(bv_ref[i] == 0)
    def _():
        o_ref[...] = jnp.zeros_like(o_ref)


def _ffn_call(blk_e, blk_row, blk_valid, xs, w_gu, b_gu, w_d, b_d, bm):
    n_e, d, f2 = w_gu.shape
    d_ff = f2 // 2
    nb = blk_e.shape[0]
    kern = functools.partial(_ffn_kernel, bm=bm, d_ff=d_ff)
    gs = pltpu.PrefetchScalarGridSpec(
        num_scalar_prefetch=3,
        grid=(nb,),
        in_specs=[pl.BlockSpec((bm * SUBLANES, LANES), lambda i, be, br, bv: (br[i], 0)),
                  pl.BlockSpec((1, d, f2), lambda i, be, br, bv: (be[i], 0, 0)),
                  pl.BlockSpec((1, 1, f2), lambda i, be, br, bv: (be[i], 0, 0)),
                  pl.BlockSpec((1, d_ff, d), lambda i, be, br, bv: (be[i], 0, 0)),
                  pl.BlockSpec((1, 1, d), lambda i, be, br, bv: (be[i], 0, 0))],
        out_specs=pl.BlockSpec((bm * SUBLANES, LANES), lambda i, be, br, bv: (i, 0)),
    )
    return pl.pallas_call(
        kern,
        out_shape=jax.ShapeDtypeStruct((nb * bm * SUBLANES, LANES), F32),
        grid_spec=gs,
        compiler_params=_params(("arbitrary",)),
        name="ffn",
    )(blk_e, blk_row, blk_valid, xs, w_gu, b_gu.reshape(n_e, 1, f2), w_d, b_d.reshape(n_e, 1, d))


def _combine_kernel(dest_ref, dest_next_ref, prow_ref, x1_ref, gf_ref, nw_ref, ys_ref, o_ref,
                    gbuf, dsm, sem_rows, sem_s, *, tm, nt):
    i = pl.program_id(0)
    slot = i % 2
    rows_per_tile = TOP_K * tm

    def gather(dref, s):
        cp = pltpu.make_async_copy(dref.at[0], dsm, sem_s)
        cp.start()
        cp.wait()

        def issue(r, carry):
            for k in range(TOP_K):
                _row_copy(ys_ref, dsm[k, r], gbuf, s * rows_per_tile + k * tm + r,
                          sem_rows.at[s]).start(priority=k % 2)
            return carry

        lax.fori_loop(0, tm, issue, 0)

    @pl.when(i == 0)
    def _():
        gather(dest_ref, 0)

    @pl.when(i + 1 < nt)
    def _():
        gather(dest_next_ref, 1 - slot)

    pltpu.make_async_copy(ys_ref.at[pl.ds(0, rows_per_tile * SUBLANES), :],
                          gbuf.at[pl.ds(0, rows_per_tile * SUBLANES), :], sem_rows.at[slot]).wait()

    p = prow_ref[...]
    base = pl.multiple_of(slot * (rows_per_tile * SUBLANES), SUBLANES)
    for s in range(SUBLANES):
        moe = None
        for k in range(TOP_K):
            piece = gbuf[pl.ds(base + k * tm * SUBLANES + s, tm, stride=SUBLANES), :] * p[:, k:k + 1]
            moe = piece if moe is None else moe + piece
        sl = slice(s * LANES, (s + 1) * LANES)
        o_ref[:, sl] = x1_ref[:, sl] + gf_ref[0][:, sl] * moe
    xo = o_ref[...]
    o_ref[...] = xo * lax.rsqrt(jnp.mean(xo * xo, axis=-1, keepdims=True) + EPS) * nw_ref[...]


def _combine_call(dest, prow, x1, mod3, norm_final, ys, seq, tm):
    t, d = x1.shape
    nt = t // tm
    tiles_per_seq = seq // tm
    kern = functools.partial(_combine_kernel, tm=tm, nt=nt)
    return pl.pallas_call(
        kern,
        out_shape=jax.ShapeDtypeStruct((t, d), F32),
        grid=(nt,),
        in_specs=[pl.BlockSpec((1, SUBLANES, tm), lambda i: (i, 0, 0)),
                  pl.BlockSpec((1, SUBLANES, tm), lambda i: (jnp.minimum(i + 1, nt - 1), 0, 0)),
                  pl.BlockSpec((tm, LANES), lambda i: (i, 0)),
                  pl.BlockSpec((tm, d), lambda i: (i, 0)),
                  pl.BlockSpec((1, 1, d), lambda i: ((i // tiles_per_seq) * N_MOD + 5, 0, 0)),
                  pl.BlockSpec((1, d), lambda i: (0, 0)),
                  pl.BlockSpec(memory_space=pl.ANY)],
        out_specs=pl.BlockSpec((tm, d), lambda i: (i, 0)),
        scratch_shapes=[pltpu.VMEM((2 * TOP_K * tm * SUBLANES, LANES), F32),
                        pltpu.SMEM((SUBLANES, tm), jnp.int32),
                        pltpu.SemaphoreType.DMA((2,)),
                        pltpu.SemaphoreType.DMA],
        compiler_params=_params(("arbitrary",)),
        name="combine",
    )(dest, dest, prow, x1, mod3, norm_final, ys)


def _plan(seq):
    def fit(pref):
        tm = min(pref, seq)
        assert seq % tm == 0
        return tm
    return dict(tm_in=fit(1024), tm_merge=fit(512), tm_moe=fit(512),
                ret_chunk=fit(RET_CHUNK), ssm_chunk=fit(SSM_CHUNK))


def _layer(x2, mod3, bsz, seq, norm_mix, norm_ffn, w_in, conv_w, conv_b, dt_bias, a_log, d_skip, ssm_norm,
           w_ret_out, w_ssm_out, w_out, w_router, b_router, w_gate_up, b_gate_up, w_down, b_down,
           norm_final):
    t, d = x2.shape
    plan = _plan(seq)
    qk_w = RET_HEADS * RET_QK_DIM
    v_w = RET_HEADS * RET_V_DIM
    d_inner = w_ssm_out.shape[0]
    conv_dim = conv_w.shape[1]
    n_heads = d_inner // SSM_HEAD_DIM
    dt_off = 2 * qk_w + 2 * v_w + d_inner + conv_dim

    w_main = jnp.concatenate([w_in[:, :dt_off], w_in[:, dt_off + n_heads:]], axis=1).astype(BF16)
    w_dt = jnp.pad(w_in[:, dt_off:dt_off + n_heads], ((0, 0), (0, LANES - n_heads)))
    w_dt_hi = w_dt.astype(BF16)
    w_dt = jnp.concatenate([w_dt_hi, (w_dt - w_dt_hi.astype(F32)).astype(BF16)], axis=1)
    half = RET_QK_DIM // 2
    inv_freq = ROPE_BASE ** (-jnp.arange(half, dtype=F32) / half)
    ang = jnp.arange(seq, dtype=F32)[:, None] * inv_freq[None, :]
    cos, sin = jnp.cos(ang), jnp.sin(ang)

    proj, dt_raw = _inproj_call(x2, norm_mix.reshape(1, d), mod3, cos, sin, w_main, w_dt, conv_w, conv_b,
                                2 * qk_w + 2 * v_w + d_inner, seq, plan["tm_in"], 2 * qk_w)
    ya = _retention_call(proj, w_ret_out.astype(BF16), bsz, seq, plan["ret_chunk"])
    yb = _ssd_call(proj, dt_raw, dt_bias, a_log, d_skip, ssm_norm,
                   w_ssm_out.astype(BF16), bsz, seq, plan["ssm_chunk"])
    tm = plan["tm_merge"]
    x1, h2, idx, rank, prow, cnt = _merge_call(ya, yb, proj, x2, mod3, norm_ffn.reshape(1, d),
                                               w_out.astype(BF16), w_router.T, b_router, seq, tm)

    bm = FFN_BLOCK
    counts = cnt[:, 0]
    padded = ((counts + bm - 1) // bm) * bm
    pad_end = jnp.cumsum(padded)
    start_pad = pad_end - padded
    n_blocks = (t * TOP_K) // bm + N_EXPERTS
    e_ids = jnp.arange(N_EXPERTS, dtype=jnp.int32)[:, None, None, None]
    dest = rank + jnp.sum(jnp.where(idx[None] == e_ids, start_pad[:, None, None, None], 0), axis=0)
    dest = dest.astype(jnp.int32)
    n_real = pad_end[-1] // bm
    blk_valid = (jnp.arange(n_blocks) < n_real).astype(jnp.int32)
    blk_row = jnp.minimum(jnp.arange(n_blocks), n_real - 1).astype(jnp.int32)
    blk_e = jnp.minimum(jnp.sum(pad_end[None, :] <= (blk_row * bm)[:, None], axis=1),
                        N_EXPERTS - 1).astype(jnp.int32)

    zero_start = jnp.concatenate([start_pad + counts, n_real[None]]).astype(jnp.int32)
    xs = _dispatch_call(h2, dest, zero_start, n_blocks, plan["tm_moe"], bm)
    ys = _ffn_call(blk_e, blk_row, blk_valid, xs, w_gate_up.astype(BF16), b_gate_up,
                   w_down.astype(BF16), b_down, bm)
    return _combine_call(dest, prow, x1, mod3, norm_final.reshape(1, d), ys, seq, plan["tm_moe"])


def kernel(x, c, w_ada, b_ada, norm_mix, norm_ffn, w_in, conv_w, conv_b, dt_bias, a_log, d_skip, ssm_norm,
           w_ret_out, w_ssm_out, w_out, w_router, b_router, w_gate_up, b_gate_up, w_down, b_down, norm_final):
    bsz, seq, d = x.shape
    depth = w_ada.shape[0]
    assert depth == 1, "the final norm is fused into the single layer's last kernel"
    x2 = x.reshape(bsz * seq, d)
    l = 0
    mod = _mod_call(c, w_ada[l], b_ada[l])
    mod3 = mod.reshape(bsz * N_MOD, 1, d)
    out = _layer(x2, mod3, bsz, seq, norm_mix[l], norm_ffn[l], w_in[l], conv_w[l], conv_b[l], dt_bias[l],
                 a_log[l], d_skip[l], ssm_norm[l], w_ret_out[l], w_ssm_out[l], w_out[l], w_router[l],
                 b_router[l], w_gate_up[l], b_gate_up[l], w_down[l], b_down[l], norm_final)
    return out.reshape(bsz, seq, d)
```

```python
import functools
import math

import numpy as np
import jax
import jax.numpy as jnp
from jax import lax
from jax.experimental import pallas as pl
from jax.experimental.pallas import tpu as pltpu

F32 = jnp.float32
BF16 = jnp.bfloat16
HIGHEST = lax.Precision.HIGHEST

EPS = 1e-6
N_MOD = 6
RET_HEADS = 4
RET_QK_DIM = 256
RET_V_DIM = 512
ROPE_BASE = 10000.0
SSM_HEAD_DIM = 64
SSM_GROUPS = 8
SSM_STATE = 128
SSM_CONV = 4
N_EXPERTS = 32
TOP_K = 4
SWIGLU_LIMIT = 7.0
SWIGLU_ALPHA = 1.702

LANES = 128
SUBLANES = 8
VMEM_LIMIT = 56 * 1024 * 1024

RET_CHUNK = 256
SSM_CHUNK = 128
FFN_BLOCK = 512
ISSUE_UNROLL = 8


def _params(sem, vmem=VMEM_LIMIT):
    return pltpu.CompilerParams(dimension_semantics=sem, vmem_limit_bytes=vmem)


def _nt_dot(a, b, **kw):
    return lax.dot_general(a, b, (((1,), (1,)), ((), ())), preferred_element_type=F32, **kw)


def _tn_dot(a, b, **kw):
    return lax.dot_general(a, b, (((0,), (0,)), ((), ())), preferred_element_type=F32, **kw)


def _silu(v):
    return v * jax.nn.sigmoid(v)


def _mod_kernel(c_ref, w_ref, b_ref, o_ref):
    cond = _silu(c_ref[...])
    o_ref[...] = jnp.dot(cond, w_ref[...], preferred_element_type=F32, precision=HIGHEST) + b_ref[...]


def _mod_call(c, w_ada, b_ada):
    bsz, d = c.shape
    n = w_ada.shape[1]
    return pl.pallas_call(
        _mod_kernel,
        out_shape=jax.ShapeDtypeStruct((bsz, n), F32),
        grid=(n // d,),
        in_specs=[pl.BlockSpec((bsz, d), lambda j: (0, 0)),
                  pl.BlockSpec((d, d), lambda j: (0, j)),
                  pl.BlockSpec((1, d), lambda j: (0, j))],
        out_specs=pl.BlockSpec((bsz, d), lambda j: (0, j)),
        compiler_params=_params(("arbitrary",)),
        name="mod",
    )(c, w_ada, b_ada.reshape(1, n))


def _inproj_kernel(x_ref, nw_ref, sc_ref, sh_ref, cos_ref, sin_ref, w_ref, wdt_ref, cw_ref, cb_ref,
                   o_ref, dt_ref, h_s, work, carry, *, conv_j0, conv_nj, tiles_per_seq, tm, tn, sub):
    i = pl.program_id(0)
    j = pl.program_id(1)
    n_dt = dt_ref.shape[1]

    @pl.when(j == 0)
    def _():
        xf = x_ref[...]
        ms = jnp.mean(xf * xf, axis=-1, keepdims=True)
        y = xf * lax.rsqrt(ms + EPS) * nw_ref[...]
        hm = y * (1.0 + sc_ref[0]) + sh_ref[0]
        hb = hm.astype(BF16)
        h_s[...] = hb
        h_lo = (hm - hb.astype(F32)).astype(BF16)
        d_hi = jnp.dot(hb, wdt_ref[...], preferred_element_type=F32)
        d_lo = jnp.dot(h_lo, wdt_ref[:, :n_dt], preferred_element_type=F32)
        dt_ref[...] = d_hi[:, :n_dt] + d_hi[:, n_dt:] + d_lo
        cos = cos_ref[...]
        sin = sin_ref[...]
        half = RET_QK_DIM // 2
        for p in range(tn // sub):
            acc = jnp.dot(hb, w_ref[:, p * sub:(p + 1) * sub], preferred_element_type=F32)
            for cc in range(sub // RET_QK_DIM):
                c = p * (sub // RET_QK_DIM) + cc
                a = acc[:, cc * RET_QK_DIM: cc * RET_QK_DIM + half]
                b = acc[:, cc * RET_QK_DIM + half: (cc + 1) * RET_QK_DIM]
                scale = 1.0 if c < RET_HEADS else RET_QK_DIM ** -0.5
                o_ref[:, c * RET_QK_DIM: c * RET_QK_DIM + half] = ((a * cos - b * sin) * scale).astype(BF16)
                o_ref[:, c * RET_QK_DIM + half: (c + 1) * RET_QK_DIM] = ((a * sin + b * cos) * scale).astype(BF16)

    is_conv = (j >= conv_j0) & (j < conv_j0 + conv_nj)

    @pl.when(is_conv)
    def _():
        cj = j - conv_j0
        pad = SUBLANES

        @pl.when(i % tiles_per_seq == 0)
        def _():
            carry[cj] = jnp.zeros(carry.shape[1:], F32)

        for p in range(tn // sub):
            acc = jnp.dot(h_s[...], w_ref[:, p * sub:(p + 1) * sub], preferred_element_type=F32)
            for cc in range(sub // LANES):
                c = p * (sub // LANES) + cc
                cols = slice(c * LANES, (c + 1) * LANES)
                work[c, 0:pad, :] = carry[cj, c]
                work[c, pad:pad + tm, :] = acc[:, cc * LANES:(cc + 1) * LANES]
                conv = cb_ref[:, cols] + cw_ref[SSM_CONV - 1:SSM_CONV, cols] * work[c, pad:pad + tm, :]
                for k in range(SSM_CONV - 1):
                    shift = SSM_CONV - 1 - k
                    conv = conv + cw_ref[k:k + 1, cols] * work[c, pad - shift:pad - shift + tm, :]
                carry[cj, c] = work[c, tm:tm + pad, :]
                o_ref[:, cols] = _silu(conv).astype(BF16)

    @pl.when((j != 0) & jnp.logical_not(is_conv))
    def _():
        for p in range(tn // sub):
            o_ref[:, p * sub:(p + 1) * sub] = jnp.dot(
                h_s[...], w_ref[:, p * sub:(p + 1) * sub], preferred_element_type=F32).astype(BF16)


def _inproj_call(x2, norm_w, mod3, cos, sin, w_main, w_dt, conv_w, conv_b, conv_off, seq, tm, tn):
    t, d = x2.shape
    n = w_main.shape[1]
    conv_dim = conv_w.shape[1]
    tiles_per_seq = seq // tm
    assert tn == 2 * RET_HEADS * RET_QK_DIM, "rotary epilogue expects q and k in the first column tile"
    assert conv_off % tn == 0 and conv_dim % tn == 0
    conv_j0, conv_nj = conv_off // tn, conv_dim // tn
    sub = 512
    kern = functools.partial(_inproj_kernel, conv_j0=conv_j0, conv_nj=conv_nj,
                             tiles_per_seq=tiles_per_seq, tm=tm, tn=tn, sub=sub)
    conv_idx = lambda i, j: (0, jnp.clip(j - conv_j0, 0, conv_nj - 1))
    return pl.pallas_call(
        kern,
        out_shape=(jax.ShapeDtypeStruct((t, n), BF16), jax.ShapeDtypeStruct((t, LANES), F32)),
        grid=(t // tm, n // tn),
        in_specs=[
            pl.BlockSpec((tm, d), lambda i, j: (i, 0)),
            pl.BlockSpec((1, d), lambda i, j: (0, 0)),
            pl.BlockSpec((1, 1, d), lambda i, j: ((i // tiles_per_seq) * N_MOD + 1, 0, 0)),
            pl.BlockSpec((1, 1, d), lambda i, j: ((i // tiles_per_seq) * N_MOD + 0, 0, 0)),
            pl.BlockSpec((tm, LANES), lambda i, j: (i % tiles_per_seq, 0)),
            pl.BlockSpec((tm, LANES), lambda i, j: (i % tiles_per_seq, 0)),
            pl.BlockSpec((d, tn), lambda i, j: (0, j)),
            pl.BlockSpec((d, 2 * LANES), lambda i, j: (0, 0)),
            pl.BlockSpec((SSM_CONV, tn), conv_idx),
            pl.BlockSpec((1, tn), conv_idx),
        ],
        out_specs=(pl.BlockSpec((tm, tn), lambda i, j: (i, j)),
                   pl.BlockSpec((tm, LANES), lambda i, j: (i, 0))),
        scratch_shapes=[pltpu.VMEM((tm, d), BF16),
                        pltpu.VMEM((tn // LANES, tm + SUBLANES, LANES), F32),
                        pltpu.VMEM((conv_nj, tn // LANES, SUBLANES, LANES), F32)],
        compiler_params=_params(("arbitrary", "arbitrary")),
        name="inproj",
    )(x2, norm_w, mod3, mod3, cos, sin, w_main, w_dt, conv_w, conv_b.reshape(1, conv_dim))


def _retention_kernel(q_ref, k_ref, v_ref, g_ref, din_ref, dq_ref, dk_ref, w_ref, o_ref, state,
                      *, decay_c):
    c = pl.program_id(1)

    @pl.when(c == 0)
    def _():
        state[...] = jnp.zeros_like(state)

    acc = None
    for h in range(RET_HEADS):
        qh = q_ref[:, h * RET_QK_DIM:(h + 1) * RET_QK_DIM]
        kh = k_ref[:, h * RET_QK_DIM:(h + 1) * RET_QK_DIM]
        vh = v_ref[:, h * RET_V_DIM:(h + 1) * RET_V_DIM]
        scores = _nt_dot(qh, kh) * din_ref[h]
        inner = jnp.dot(scores.astype(BF16), vh, preferred_element_type=F32)
        st = state[h]
        cross = jnp.dot(qh, st.astype(BF16), preferred_element_type=F32) * dq_ref[h]
        kd = (kh.astype(F32) * dk_ref[h]).astype(BF16)
        state[h] = st * decay_c[h] + _tn_dot(kd, vh)
        ret = inner + cross
        ret = ret * lax.rsqrt(jnp.mean(ret * ret, axis=-1, keepdims=True) + EPS)
        gh = g_ref[:, h * RET_V_DIM:(h + 1) * RET_V_DIM].astype(F32)
        ret = ret * _silu(gh)
        part = jnp.dot(ret.astype(BF16), w_ref[h * RET_V_DIM:(h + 1) * RET_V_DIM, :],
                       preferred_element_type=F32)
        acc = part if acc is None else acc + part
    o_ref[...] = acc


def _retention_tables(chunk):
    lg = np.log(1.0 - 2.0 ** (-5.0 - np.arange(RET_HEADS, dtype=np.float64)))
    idx = np.arange(chunk, dtype=np.float64)
    rel = idx[:, None] - idx[None, :]
    causal = rel >= 0
    din = np.where(causal[None], np.exp(np.where(causal, rel, 0.0)[None] * lg[:, None, None]), 0.0)
    dq = np.exp((idx + 1.0)[None, :, None] * lg[:, None, None])
    dk = np.exp((chunk - 1.0 - idx)[None, :, None] * lg[:, None, None])
    dc = tuple(float(v) for v in np.exp(chunk * lg))
    return (jnp.asarray(din, F32), jnp.asarray(dq, F32), jnp.asarray(dk, F32), dc)


def _retention_call(proj, w_ret, bsz, seq, chunk):
    t = proj.shape[0]
    d = w_ret.shape[1]
    nc = seq // chunk
    qk_w = RET_HEADS * RET_QK_DIM
    v_w = RET_HEADS * RET_V_DIM
    din, dq, dk, dc = _retention_tables(chunk)
    kern = functools.partial(_retention_kernel, decay_c=dc)
    row = lambda b, c: b * nc + c
    return pl.pallas_call(
        kern,
        out_shape=jax.ShapeDtypeStruct((t, d), F32),
        grid=(bsz, nc),
        in_specs=[
            pl.BlockSpec((chunk, qk_w), lambda b, c: (row(b, c), 0)),
            pl.BlockSpec((chunk, qk_w), lambda b, c: (row(b, c), 1)),
            pl.BlockSpec((chunk, v_w), lambda b, c: (row(b, c), 1)),
            pl.BlockSpec((chunk, v_w), lambda b, c: (row(b, c), 2)),
            pl.BlockSpec((RET_HEADS, chunk, chunk), lambda b, c: (0, 0, 0)),
            pl.BlockSpec((RET_HEADS, chunk, 1), lambda b, c: (0, 0, 0)),
            pl.BlockSpec((RET_HEADS, chunk, 1), lambda b, c: (0, 0, 0)),
            pl.BlockSpec((v_w, d), lambda b, c: (0, 0)),
        ],
        out_specs=pl.BlockSpec((chunk, d), lambda b, c: (row(b, c), 0)),
        scratch_shapes=[pltpu.VMEM((RET_HEADS, RET_QK_DIM, RET_V_DIM), F32)],
        compiler_params=_params(("arbitrary", "arbitrary")),
        name="retention",
    )(proj, proj, proj, proj, din, dq, dk, w_ret)


def _ssd_kernel(z_ref, xbc_ref, dt_ref, dtb_ref, alog_ref, dsk_ref, nw_ref,
                tril_ref, exp_ref, w_ref, o_ref, state, yn_s, *, chunk, d_inner):
    c = pl.program_id(1)
    heads_per_group = d_inner // SSM_HEAD_DIM // SSM_GROUPS
    gw = heads_per_group * SSM_HEAD_DIM
    assert SSM_HEAD_DIM * 2 == LANES and gw == 2 * LANES

    @pl.when(c == 0)
    def _():
        state[...] = jnp.zeros_like(state)

    dt = jax.nn.softplus(dt_ref[...] + dtb_ref[...])
    a = -jnp.exp(alog_ref[...])
    adt = dt * a
    acs = jnp.dot(tril_ref[...], adt, preferred_element_type=F32, precision=HIGHEST)
    acs_t = acs.T
    dt_x = jnp.dot(dt.astype(BF16), exp_ref[...], preferred_element_type=F32)
    li = lax.broadcasted_iota(jnp.int32, (chunk, chunk), 0)
    si = lax.broadcasted_iota(jnp.int32, (chunk, chunk), 1)
    causal = li >= si
    low_half = si < SSM_HEAD_DIM
    lane_g = lax.broadcasted_iota(jnp.int32, (chunk, gw), 1)

    b_off = d_inner
    c_off = d_inner + SSM_GROUPS * SSM_STATE
    for g in range(SSM_GROUPS):
        bm = xbc_ref[:, b_off + g * SSM_STATE: b_off + (g + 1) * SSM_STATE]
        cm = xbc_ref[:, c_off + g * SSM_STATE: c_off + (g + 1) * SSM_STATE]
        xs_g = xbc_ref[:, g * gw:(g + 1) * gw].astype(F32)
        xdt_g = xs_g * dt_x[:, g * gw:(g + 1) * gw]
        cb = _nt_dot(cm, bm)
        cols, ms, xm = [], [], []
        for jh in range(heads_per_group):
            h = g * heads_per_group + jh
            col = jnp.broadcast_to(acs[:, h:h + 1], (chunk, chunk))
            seg = jnp.exp(jnp.where(causal, col - acs_t[h:h + 1, :], -jnp.inf))
            cols.append(col)
            ms.append((cb * seg).astype(BF16))
            in_head = (lane_g >= jh * SSM_HEAD_DIM) & (lane_g < (jh + 1) * SSM_HEAD_DIM)
            xm.append(jnp.where(in_head, xdt_g, 0.0).astype(BF16))
        y_diag = jnp.dot(jnp.concatenate(ms, axis=-1), jnp.concatenate(xm, axis=0),
                         preferred_element_type=F32)
        a_x = jnp.concatenate([jnp.where(low_half, cols[0], cols[1]),
                               jnp.where(low_half, cols[2], cols[3])], axis=-1)
        e_acs_x = jnp.exp(a_x)
        a_last_x = a_x[chunk - 1:chunk, :]
        st = state[g]
        y_off = jnp.dot(cm, st.astype(BF16), preferred_element_type=F32) * e_acs_x
        xdec = (xdt_g * jnp.exp(a_last_x - a_x)).astype(BF16)
        state[g] = st * e_acs_x[chunk - 1:chunk, :] + _tn_dot(bm, xdec)
        y = y_diag + y_off + dsk_ref[:, g * gw:(g + 1) * gw] * xs_g
        yz = y * _silu(z_ref[:, g * gw:(g + 1) * gw].astype(F32))
        yn = yz * lax.rsqrt(jnp.mean(yz * yz, axis=-1, keepdims=True) + EPS) * nw_ref[:, g * gw:(g + 1) * gw]
        yn_s[:, g * gw:(g + 1) * gw] = yn.astype(BF16)
    o_ref[...] = jnp.dot(yn_s[...], w_ref[...], preferred_element_type=F32)


def _ssd_call(proj, dt_raw, dt_bias, a_log, d_skip, ssm_norm, w_ssm, bsz, seq, chunk):
    t = proj.shape[0]
    d_inner, d = w_ssm.shape
    conv_dim = d_inner + 2 * SSM_GROUPS * SSM_STATE
    n_heads = d_inner // SSM_HEAD_DIM
    nc = seq // chunk
    gw = d_inner // SSM_GROUPS
    pad_h = lambda v: jnp.pad(v.astype(F32), (0, LANES - n_heads)).reshape(1, LANES)
    assert chunk == LANES, "the per-head decay tiles are built lane-for-lane against the chunk"
    tril = jnp.asarray(np.tril(np.ones((chunk, chunk), np.float32)))
    expand = np.zeros((LANES, d_inner), np.float32)
    for h in range(n_heads):
        expand[h, h * SSM_HEAD_DIM:(h + 1) * SSM_HEAD_DIM] = 1.0
    expand = jnp.asarray(expand, BF16)
    kern = functools.partial(_ssd_kernel, chunk=chunk, d_inner=d_inner)
    row = lambda b, c: b * nc + c
    z_blk = (2 * RET_HEADS * RET_QK_DIM + 2 * RET_HEADS * RET_V_DIM) // d_inner
    xbc_blk = (2 * RET_HEADS * RET_QK_DIM + 2 * RET_HEADS * RET_V_DIM + d_inner) // conv_dim
    full = lambda shape: pl.BlockSpec(shape, lambda b, c: (0,) * len(shape))
    return pl.pallas_call(
        kern,
        out_shape=jax.ShapeDtypeStruct((t, d), F32),
        grid=(bsz, nc),
        in_specs=[
            pl.BlockSpec((chunk, d_inner), lambda b, c: (row(b, c), z_blk)),
            pl.BlockSpec((chunk, conv_dim), lambda b, c: (row(b, c), xbc_blk)),
            pl.BlockSpec((chunk, LANES), lambda b, c: (row(b, c), 0)),
            full((1, LANES)), full((1, LANES)),
            full((1, d_inner)), full((1, d_inner)), full((chunk, chunk)), full((LANES, d_inner)),
            full((d_inner, d)),
        ],
        out_specs=pl.BlockSpec((chunk, d), lambda b, c: (row(b, c), 0)),
        scratch_shapes=[pltpu.VMEM((SSM_GROUPS, SSM_STATE, gw), F32),
                        pltpu.VMEM((chunk, d_inner), BF16)],
        compiler_params=_params(("arbitrary", "arbitrary")),
        name="ssd",
    )(proj, proj, dt_raw, pad_h(dt_bias), pad_h(a_log),
      jnp.repeat(d_skip.astype(F32), SSM_HEAD_DIM).reshape(1, d_inner), ssm_norm.reshape(1, d_inner),
      tril, expand, w_ssm)


def _merge_kernel(ya_ref, yb_ref, ga_ref, gb_ref, x_ref, gm_ref, scf_ref, shf_ref, nw_ref, wo_ref,
                  wr_ref, br_ref, tri_ref,
                  x1_ref, h2_ref, idx_ref, rank_ref, prow_ref, cnt_ref, cnt_s, *, tm):
    i = pl.program_id(0)

    @pl.when(i == 0)
    def _():
        cnt_s[...] = jnp.zeros_like(cnt_s)

    merged = (jax.nn.sigmoid(ga_ref[...].astype(F32)) * ya_ref[...]
              + jax.nn.sigmoid(gb_ref[...].astype(F32)) * yb_ref[...])
    mo = jnp.dot(merged.astype(BF16), wo_ref[...], preferred_element_type=F32)
    x1 = x_ref[...] + gm_ref[0] * mo
    x1_ref[...] = x1
    ms = jnp.mean(x1 * x1, axis=-1, keepdims=True)
    h2 = x1 * lax.rsqrt(ms + EPS) * nw_ref[...] * (1.0 + scf_ref[0]) + shf_ref[0]
    h2_ref[...] = h2

    lg = _nt_dot(wr_ref[...], h2, precision=HIGHEST) + br_ref[...]
    sub = lax.broadcasted_iota(jnp.int32, lg.shape, 0)
    work = lg
    vals, idxs, sels = [], [], []
    for _ in range(TOP_K):
        m = jnp.max(work, axis=0, keepdims=True)
        ik = jnp.min(jnp.where(work == m, sub, N_EXPERTS), axis=0, keepdims=True)
        sel = sub == ik
        vals.append(m)
        idxs.append(ik)
        sels.append(sel)
        work = jnp.where(sel, -jnp.inf, work)
    exps = [jnp.exp(v - vals[0]) for v in vals]
    denom = exps[0]
    for e in exps[1:]:
        denom = denom + e
    probs = [e / denom for e in exps]

    base = cnt_s[:, 0:1]
    ranks = []
    for k in range(TOP_K):
        mk = jnp.where(sels[k], 1.0, 0.0)
        pre = jnp.dot(mk.astype(BF16), tri_ref[...], preferred_element_type=F32)
        ranks.append(jnp.sum(jnp.where(sels[k], pre + base, 0.0), axis=0, keepdims=True))
        base = base + jnp.sum(mk, axis=1, keepdims=True)
    cnt_s[...] = jnp.broadcast_to(base, cnt_s.shape)
    cnt_ref[...] = cnt_s[...].astype(jnp.int32)

    zi = jnp.zeros((SUBLANES - TOP_K, tm), jnp.int32)
    idx_ref[0] = jnp.concatenate(idxs + [zi], axis=0)
    rank_ref[0] = jnp.concatenate([r.astype(jnp.int32) for r in ranks] + [zi], axis=0)
    pt = jnp.concatenate(probs + [jnp.zeros((LANES - TOP_K, tm), F32)], axis=0)
    prow_ref[...] = pt.T


def _merge_call(ya, yb, proj, x2, mod3, norm_w, w_out, w_router_t, b_router, seq, tm):
    t, d = x2.shape
    nt = t // tm
    tiles_per_seq = seq // tm
    ga_blk = proj.shape[1] // d - 2
    tri = jnp.asarray(np.triu(np.ones((tm, tm), np.float32), 1), BF16)
    kern = functools.partial(_merge_kernel, tm=tm)
    modspec = lambda m: pl.BlockSpec((1, 1, d), lambda i: ((i // tiles_per_seq) * N_MOD + m, 0, 0))
    return pl.pallas_call(
        kern,
        out_shape=(jax.ShapeDtypeStruct((t, d), F32), jax.ShapeDtypeStruct((t, d), F32),
                   jax.ShapeDtypeStruct((nt, SUBLANES, tm), jnp.int32),
                   jax.ShapeDtypeStruct((nt, SUBLANES, tm), jnp.int32),
                   jax.ShapeDtypeStruct((t, LANES), F32),
                   jax.ShapeDtypeStruct((N_EXPERTS, LANES), jnp.int32)),
        grid=(nt,),
        in_specs=[
            pl.BlockSpec((tm, d), lambda i: (i, 0)),
            pl.BlockSpec((tm, d), lambda i: (i, 0)),
            pl.BlockSpec((tm, d), lambda i: (i, ga_blk)),
            pl.BlockSpec((tm, d), lambda i: (i, ga_blk + 1)),
            pl.BlockSpec((tm, d), lambda i: (i, 0)),
            modspec(2), modspec(4), modspec(3),
            pl.BlockSpec((1, d), lambda i: (0, 0)),
            pl.BlockSpec((d, d), lambda i: (0, 0)),
            pl.BlockSpec((N_EXPERTS, d), lambda i: (0, 0)),
            pl.BlockSpec((N_EXPERTS, 1), lambda i: (0, 0)),
            pl.BlockSpec((tm, tm), lambda i: (0, 0)),
        ],
        out_specs=(pl.BlockSpec((tm, d), lambda i: (i, 0)),
                   pl.BlockSpec((tm, d), lambda i: (i, 0)),
                   pl.BlockSpec((1, SUBLANES, tm), lambda i: (i, 0, 0)),
                   pl.BlockSpec((1, SUBLANES, tm), lambda i: (i, 0, 0)),
                   pl.BlockSpec((tm, LANES), lambda i: (i, 0)),
                   pl.BlockSpec((N_EXPERTS, LANES), lambda i: (0, 0))),
        scratch_shapes=[pltpu.VMEM((N_EXPERTS, LANES), F32)],
        compiler_params=_params(("arbitrary",)),
        name="merge",
    )(ya, yb, proj, proj, x2, mod3, mod3, mod3, norm_w, w_out, w_router_t,
      b_router.reshape(N_EXPERTS, 1), tri)


def _row_copy(src, src_row, dst, dst_row, sem):
    return pltpu.make_async_copy(
        src.at[pl.ds(pl.multiple_of(src_row * SUBLANES, SUBLANES), SUBLANES), :],
        dst.at[pl.ds(pl.multiple_of(dst_row * SUBLANES, SUBLANES), SUBLANES), :], sem)


def _dispatch_kernel(zs_ref, h2_ref, dest_ref, xs_ref, slab, zbuf, dsm, sem_rows, sem_s, sem_z,
                     *, tm, nt, bm, n_blocks):
    i = pl.program_id(0)
    slot = i % 2
    rows_per_tile = TOP_K * tm

    def wait_rows(s):
        pltpu.make_async_copy(xs_ref.at[pl.ds(0, rows_per_tile * SUBLANES), :],
                              xs_ref.at[pl.ds(0, rows_per_tile * SUBLANES), :], sem_rows.at[s]).wait()

    @pl.when(i == 0)
    def _():
        zbuf[...] = jnp.zeros_like(zbuf)
        for e in range(N_EXPERTS):
            fill = pltpu.make_async_copy(
                zbuf, xs_ref.at[pl.ds(pl.multiple_of(zs_ref[e] * SUBLANES, SUBLANES), bm * SUBLANES), :],
                sem_z)
            fill.start()
            fill.wait()

        def fill_tail(b, carry):
            fill = pltpu.make_async_copy(
                zbuf, xs_ref.at[pl.ds(pl.multiple_of(b * (bm * SUBLANES), SUBLANES), bm * SUBLANES), :], sem_z)
            fill.start()
            fill.wait()
            return carry

        lax.fori_loop(zs_ref[N_EXPERTS], n_blocks + 1, fill_tail, 0)

    @pl.when(i >= 2)
    def _():
        wait_rows(slot)

    cp = pltpu.make_async_copy(dest_ref.at[0], dsm, sem_s)
    cp.start()
    base = pl.multiple_of(slot * (tm * SUBLANES), SUBLANES)
    for s in range(SUBLANES):
        piece = h2_ref[:, s * LANES:(s + 1) * LANES]
        slab[pl.ds(base + s, tm, stride=SUBLANES), :] = piece
    cp.wait()

    def issue(rb, carry):
        for rr in range(ISSUE_UNROLL):
            r = rb * ISSUE_UNROLL + rr
            for k in range(TOP_K):
                _row_copy(slab, slot * tm + r, xs_ref, dsm[k, r], sem_rows.at[slot]).start(priority=k % 2)
        return carry

    lax.fori_loop(0, tm // ISSUE_UNROLL, issue, 0)

    @pl.when(i == nt - 1)
    def _():
        wait_rows(slot)
        if nt >= 2:
            wait_rows(1 - slot)


def _dispatch_call(h2, dest, zero_start, n_blocks, tm, bm):
    t, d = h2.shape
    nt = t // tm
    n_rows = (n_blocks + 1) * bm
    assert d == SUBLANES * LANES
    kern = functools.partial(_dispatch_kernel, tm=tm, nt=nt, bm=bm, n_blocks=n_blocks)
    gs = pltpu.PrefetchScalarGridSpec(
        num_scalar_prefetch=1,
        grid=(nt,),
        in_specs=[pl.BlockSpec((tm, d), lambda i, zs: (i, 0)),
                  pl.BlockSpec((1, SUBLANES, tm), lambda i, zs: (i, 0, 0))],
        out_specs=pl.BlockSpec(memory_space=pl.ANY),
        scratch_shapes=[pltpu.VMEM((2 * tm * SUBLANES, LANES), F32),
                        pltpu.VMEM((bm * SUBLANES, LANES), F32),
                        pltpu.SMEM((SUBLANES, tm), jnp.int32),
                        pltpu.SemaphoreType.DMA((2,)),
                        pltpu.SemaphoreType.DMA,
                        pltpu.SemaphoreType.DMA],
    )
    return pl.pallas_call(
        kern,
        out_shape=jax.ShapeDtypeStruct((n_rows * SUBLANES, LANES), F32),
        grid_spec=gs,
        compiler_params=pltpu.CompilerParams(dimension_semantics=("arbitrary",),
                                             vmem_limit_bytes=VMEM_LIMIT, has_side_effects=True),
        name="dispatch",
    )(zero_start, h2, dest)


def _ffn_kernel(be_ref, br_ref, bv_ref, x_ref, wgu_ref, bgu_ref, wd_ref, bd_ref, o_ref, wgu_s, wd_s,
                *, bm, d_ff):
    i = pl.program_id(0)

    @pl.when((i == 0) | (be_ref[i] != be_ref[jnp.maximum(i - 1, 0)]))
    def _():
        wgu_s[...] = wgu_ref[0].astype(BF16)
        wd_s[...] = wd_ref[0].astype(BF16)

    @pl.when(bv_ref[i] == 1)
    def _():
        x = jnp.concatenate([x_ref[pl.ds(s, bm, stride=SUBLANES), :] for s in range(SUBLANES)],
                            axis=-1).astype(BF16)
        gu = jnp.dot(x, wgu_s[...], preferred_element_type=F32) + bgu_ref[0]
        gate = jnp.minimum(gu[:, :d_ff], SWIGLU_LIMIT)
        up = jnp.clip(gu[:, d_ff:], -SWIGLU_LIMIT, SWIGLU_LIMIT)
        act = gate * jax.nn.sigmoid(SWIGLU_ALPHA * gate) * (up + 1.0)
        y = jnp.dot(act.astype(BF16), wd_s[...], preferred_element_type=F32) + bd_ref[0]
        for s in range(SUBLANES):
            o_ref[pl.ds(s, bm, stride=SUBLANES), :] = y[:, s * LANES:(s + 1) * LANES]

    @pl.when(bv_ref[i] == 0)
    def _():
        o_ref[...] = jnp.zeros_like(o_ref)


def _ffn_call(blk_e, blk_row, blk_valid, xs, w_gu, b_gu, w_d, b_d, bm):
    n_e, d, f2 = w_gu.shape
    d_ff = f2 // 2
    nb = blk_e.shape[0]
    kern = functools.partial(_ffn_kernel, bm=bm, d_ff=d_ff)
    gs = pltpu.PrefetchScalarGridSpec(
        num_scalar_prefetch=3,
        grid=(nb,),
        in_specs=[pl.BlockSpec((bm * SUBLANES, LANES), lambda i, be, br, bv: (br[i], 0)),
                  pl.BlockSpec((1, d, f2), lambda i, be, br, bv: (be[i], 0, 0)),
                  pl.BlockSpec((1, 1, f2), lambda i, be, br, bv: (be[i], 0, 0)),
                  pl.BlockSpec((1, d_ff, d), lambda i, be, br, bv: (be[i], 0, 0)),
                  pl.BlockSpec((1, 1, d), lambda i, be, br, bv: (be[i], 0, 0))],
        out_specs=pl.BlockSpec((bm * SUBLANES, LANES), lambda i, be, br, bv: (i, 0)),
        scratch_shapes=[pltpu.VMEM((d, f2), BF16), pltpu.VMEM((d_ff, d), BF16)],
    )
    return pl.pallas_call(
        kern,
        out_shape=jax.ShapeDtypeStruct((nb * bm * SUBLANES, LANES), F32),
        grid_spec=gs,
        compiler_params=_params(("arbitrary",)),
        name="ffn",
    )(blk_e, blk_row, blk_valid, xs, w_gu, b_gu.reshape(n_e, 1, f2), w_d, b_d.reshape(n_e, 1, d))


def _combine_kernel(dest_ref, dest_next_ref, prow_ref, x1_ref, gf_ref, nw_ref, ys_ref, o_ref,
                    gbuf, dsm, sem_rows, sem_s, *, tm, nt):
    i = pl.program_id(0)
    slot = i % 2
    rows_per_tile = TOP_K * tm

    def gather(dref, s):
        cp = pltpu.make_async_copy(dref.at[0], dsm, sem_s)
        cp.start()
        cp.wait()

        def issue(rb, carry):
            for rr in range(ISSUE_UNROLL):
                r = rb * ISSUE_UNROLL + rr
                for k in range(TOP_K):
                    _row_copy(ys_ref, dsm[k, r], gbuf, s * rows_per_tile + k * tm + r,
                              sem_rows.at[s]).start(priority=k % 2)
            return carry

        lax.fori_loop(0, tm // ISSUE_UNROLL, issue, 0)

    @pl.when(i == 0)
    def _():
        gather(dest_ref, 0)

    @pl.when(i + 1 < nt)
    def _():
        gather(dest_next_ref, 1 - slot)

    pltpu.make_async_copy(ys_ref.at[pl.ds(0, rows_per_tile * SUBLANES), :],
                          gbuf.at[pl.ds(0, rows_per_tile * SUBLANES), :], sem_rows.at[slot]).wait()

    p = prow_ref[...]
    base = pl.multiple_of(slot * (rows_per_tile * SUBLANES), SUBLANES)
    for s in range(SUBLANES):
        moe = None
        for k in range(TOP_K):
            piece = gbuf[pl.ds(base + k * tm * SUBLANES + s, tm, stride=SUBLANES), :] * p[:, k:k + 1]
            moe = piece if moe is None else moe + piece
        sl = slice(s * LANES, (s + 1) * LANES)
        o_ref[:, sl] = x1_ref[:, sl] + gf_ref[0][:, sl] * moe
    xo = o_ref[...]
    o_ref[...] = xo * lax.rsqrt(jnp.mean(xo * xo, axis=-1, keepdims=True) + EPS) * nw_ref[...]


def _combine_call(dest, prow, x1, mod3, norm_final, ys, seq, tm):
    t, d = x1.shape
    nt = t // tm
    tiles_per_seq = seq // tm
    kern = functools.partial(_combine_kernel, tm=tm, nt=nt)
    return pl.pallas_call(
        kern,
        out_shape=jax.ShapeDtypeStruct((t, d), F32),
        grid=(nt,),
        in_specs=[pl.BlockSpec((1, SUBLANES, tm), lambda i: (i, 0, 0)),
                  pl.BlockSpec((1, SUBLANES, tm), lambda i: (jnp.minimum(i + 1, nt - 1), 0, 0)),
                  pl.BlockSpec((tm, LANES), lambda i: (i, 0)),
                  pl.BlockSpec((tm, d), lambda i: (i, 0)),
                  pl.BlockSpec((1, 1, d), lambda i: ((i // tiles_per_seq) * N_MOD + 5, 0, 0)),
                  pl.BlockSpec((1, d), lambda i: (0, 0)),
                  pl.BlockSpec(memory_space=pl.ANY)],
        out_specs=pl.BlockSpec((tm, d), lambda i: (i, 0)),
        scratch_shapes=[pltpu.VMEM((2 * TOP_K * tm * SUBLANES, LANES), F32),
                        pltpu.SMEM((SUBLANES, tm), jnp.int32),
                        pltpu.SemaphoreType.DMA((2,)),
                        pltpu.SemaphoreType.DMA],
        compiler_params=_params(("arbitrary",)),
        name="combine",
    )(dest, dest, prow, x1, mod3, norm_final, ys)


def _plan(seq):
    def fit(pref):
        tm = min(pref, seq)
        assert seq % tm == 0
        return tm
    return dict(tm_in=fit(1024), tm_merge=fit(512), tm_moe=fit(512),
                ret_chunk=fit(RET_CHUNK), ssm_chunk=fit(SSM_CHUNK))


def _layer(x2, mod3, bsz, seq, norm_mix, norm_ffn, w_in, conv_w, conv_b, dt_bias, a_log, d_skip, ssm_norm,
           w_ret_out, w_ssm_out, w_out, w_router, b_router, w_gate_up, b_gate_up, w_down, b_down,
           norm_final):
    t, d = x2.shape
    plan = _plan(seq)
    qk_w = RET_HEADS * RET_QK_DIM
    v_w = RET_HEADS * RET_V_DIM
    d_inner = w_ssm_out.shape[0]
    conv_dim = conv_w.shape[1]
    n_heads = d_inner // SSM_HEAD_DIM
    dt_off = 2 * qk_w + 2 * v_w + d_inner + conv_dim

    w_main = jnp.concatenate([w_in[:, :dt_off], w_in[:, dt_off + n_heads:]], axis=1).astype(BF16)
    w_dt = jnp.pad(w_in[:, dt_off:dt_off + n_heads], ((0, 0), (0, LANES - n_heads)))
    w_dt_hi = w_dt.astype(BF16)
    w_dt = jnp.concatenate([w_dt_hi, (w_dt - w_dt_hi.astype(F32)).astype(BF16)], axis=1)
    half = RET_QK_DIM // 2
    inv_freq = ROPE_BASE ** (-jnp.arange(half, dtype=F32) / half)
    ang = jnp.arange(seq, dtype=F32)[:, None] * inv_freq[None, :]
    cos, sin = jnp.cos(ang), jnp.sin(ang)

    proj, dt_raw = _inproj_call(x2, norm_mix.reshape(1, d), mod3, cos, sin, w_main, w_dt, conv_w, conv_b,
                                2 * qk_w + 2 * v_w + d_inner, seq, plan["tm_in"], 2 * qk_w)
    ya = _retention_call(proj, w_ret_out.astype(BF16), bsz, seq, plan["ret_chunk"])
    yb = _ssd_call(proj, dt_raw, dt_bias, a_log, d_skip, ssm_norm,
                   w_ssm_out.astype(BF16), bsz, seq, plan["ssm_chunk"])
    tm = plan["tm_merge"]
    x1, h2, idx, rank, prow, cnt = _merge_call(ya, yb, proj, x2, mod3, norm_ffn.reshape(1, d),
                                               w_out.astype(BF16), w_router.T, b_router, seq, tm)

    bm = FFN_BLOCK
    counts = cnt[:, 0]
    padded = ((counts + bm - 1) // bm) * bm
    pad_end = jnp.cumsum(padded)
    start_pad = pad_end - padded
    n_blocks = (t * TOP_K) // bm + N_EXPERTS
    e_ids = jnp.arange(N_EXPERTS, dtype=jnp.int32)[:, None, None, None]
    dest = rank + jnp.sum(jnp.where(idx[None] == e_ids, start_pad[:, None, None, None], 0), axis=0)
    dest = dest.astype(jnp.int32)
    n_real = pad_end[-1] // bm
    blk_valid = (jnp.arange(n_blocks) < n_real).astype(jnp.int32)
    blk_row = jnp.minimum(jnp.arange(n_blocks), n_real - 1).astype(jnp.int32)
    blk_e = jnp.minimum(jnp.sum(pad_end[None, :] <= (blk_row * bm)[:, None], axis=1),
                        N_EXPERTS - 1).astype(jnp.int32)

    zero_start = jnp.concatenate([start_pad + counts, n_real[None]]).astype(jnp.int32)
    xs = _dispatch_call(h2, dest, zero_start, n_blocks, plan["tm_moe"], bm)
    ys = _ffn_call(blk_e, blk_row, blk_valid, xs, w_gate_up, b_gate_up, w_down, b_down, bm)
    return _combine_call(dest, prow, x1, mod3, norm_final.reshape(1, d), ys, seq, plan["tm_moe"])


def kernel(x, c, w_ada, b_ada, norm_mix, norm_ffn, w_in, conv_w, conv_b, dt_bias, a_log, d_skip, ssm_norm,
           w_ret_out, w_ssm_out, w_out, w_router, b_router, w_gate_up, b_gate_up, w_down, b_down, norm_final):
    bsz, seq, d = x.shape
    depth = w_ada.shape[0]
    assert depth == 1, "the final norm is fused into the single layer's last kernel"
    x2 = x.reshape(bsz * seq, d)
    l = 0
    mod = _mod_call(c, w_ada[l], b_ada[l])
    mod3 = mod.reshape(bsz * N_MOD, 1, d)
    out = _layer(x2, mod3, bsz, seq, norm_mix[l], norm_ffn[l], w_in[l], conv_w[l], conv_b[l], dt_bias[l],
                 a_log[l], d_skip[l], ssm_norm[l], w_ret_out[l], w_ssm_out[l], w_out[l], w_router[l],
                 b_router[l], w_gate_up[l], b_gate_up[l], w_down[l], b_down[l], norm_final)
    return out.reshape(bsz, seq, d)
```

```python
import functools
import math

import numpy as np
import jax
import jax.numpy as jnp
from jax import lax
from jax.experimental import pallas as pl
from jax.experimental.pallas import tpu as pltpu
from jax.experimental.pallas import tpu_sc as plsc

F32 = jnp.float32
BF16 = jnp.bfloat16
HIGHEST = lax.Precision.HIGHEST

EPS = 1e-6
N_MOD = 6
RET_HEADS = 4
RET_QK_DIM = 256
RET_V_DIM = 512
ROPE_BASE = 10000.0
SSM_HEAD_DIM = 64
SSM_GROUPS = 8
SSM_STATE = 128
SSM_CONV = 4
N_EXPERTS = 32
TOP_K = 4
SWIGLU_LIMIT = 7.0
SWIGLU_ALPHA = 1.702

LANES = 128
SUBLANES = 8
VMEM_LIMIT = 56 * 1024 * 1024

RET_CHUNK = 256
SSM_CHUNK = 128
FFN_BLOCK = 512
SC_CORES = 2
SC_SUBCORES = 16
SC_GROUP = 32


def _params(sem, vmem=VMEM_LIMIT):
    return pltpu.CompilerParams(dimension_semantics=sem, vmem_limit_bytes=vmem)


def _nt_dot(a, b, **kw):
    return lax.dot_general(a, b, (((1,), (1,)), ((), ())), preferred_element_type=F32, **kw)


def _tn_dot(a, b, **kw):
    return lax.dot_general(a, b, (((0,), (0,)), ((), ())), preferred_element_type=F32, **kw)


def _silu(v):
    return v * jax.nn.sigmoid(v)


def _mod_kernel(c_ref, w_ref, b_ref, o_ref):
    cond = _silu(c_ref[...])
    o_ref[...] = jnp.dot(cond, w_ref[...], preferred_element_type=F32, precision=HIGHEST) + b_ref[...]


def _mod_call(c, w_ada, b_ada):
    bsz, d = c.shape
    n = w_ada.shape[1]
    return pl.pallas_call(
        _mod_kernel,
        out_shape=jax.ShapeDtypeStruct((bsz, n), F32),
        grid=(n // d,),
        in_specs=[pl.BlockSpec((bsz, d), lambda j: (0, 0)),
                  pl.BlockSpec((d, d), lambda j: (0, j)),
                  pl.BlockSpec((1, d), lambda j: (0, j))],
        out_specs=pl.BlockSpec((bsz, d), lambda j: (0, j)),
        compiler_params=_params(("arbitrary",)),
        name="mod",
    )(c, w_ada, b_ada.reshape(1, n))


def _inproj_kernel(x_ref, nw_ref, sc_ref, sh_ref, cos_ref, sin_ref, w_ref, wdt_ref, cw_ref, cb_ref,
                   o_ref, dt_ref, h_s, work, carry, *, conv_j0, conv_nj, tiles_per_seq, tm, tn, sub):
    i = pl.program_id(0)
    j = pl.program_id(1)
    n_dt = dt_ref.shape[1]

    @pl.when(j == 0)
    def _():
        xf = x_ref[...]
        ms = jnp.mean(xf * xf, axis=-1, keepdims=True)
        y = xf * lax.rsqrt(ms + EPS) * nw_ref[...]
        hm = y * (1.0 + sc_ref[0]) + sh_ref[0]
        hb = hm.astype(BF16)
        h_s[...] = hb
        h_lo = (hm - hb.astype(F32)).astype(BF16)
        d_hi = jnp.dot(hb, wdt_ref[...], preferred_element_type=F32)
        d_lo = jnp.dot(h_lo, wdt_ref[:, :n_dt], preferred_element_type=F32)
        dt_ref[...] = d_hi[:, :n_dt] + d_hi[:, n_dt:] + d_lo
        cos = cos_ref[...]
        sin = sin_ref[...]
        half = RET_QK_DIM // 2
        for p in range(tn // sub):
            acc = jnp.dot(hb, w_ref[:, p * sub:(p + 1) * sub], preferred_element_type=F32)
            for cc in range(sub // RET_QK_DIM):
                c = p * (sub // RET_QK_DIM) + cc
                a = acc[:, cc * RET_QK_DIM: cc * RET_QK_DIM + half]
                b = acc[:, cc * RET_QK_DIM + half: (cc + 1) * RET_QK_DIM]
                scale = 1.0 if c < RET_HEADS else RET_QK_DIM ** -0.5
                o_ref[:, c * RET_QK_DIM: c * RET_QK_DIM + half] = ((a * cos - b * sin) * scale).astype(BF16)
                o_ref[:, c * RET_QK_DIM + half: (c + 1) * RET_QK_DIM] = ((a * sin + b * cos) * scale).astype(BF16)

    is_conv = (j >= conv_j0) & (j < conv_j0 + conv_nj)

    @pl.when(is_conv)
    def _():
        cj = j - conv_j0
        pad = SUBLANES

        @pl.when(i % tiles_per_seq == 0)
        def _():
            carry[cj] = jnp.zeros(carry.shape[1:], F32)

        for p in range(tn // sub):
            acc = jnp.dot(h_s[...], w_ref[:, p * sub:(p + 1) * sub], preferred_element_type=F32)
            for cc in range(sub // LANES):
                c = p * (sub // LANES) + cc
                cols = slice(c * LANES, (c + 1) * LANES)
                work[c, 0:pad, :] = carry[cj, c]
                work[c, pad:pad + tm, :] = acc[:, cc * LANES:(cc + 1) * LANES]
                conv = cb_ref[:, cols] + cw_ref[SSM_CONV - 1:SSM_CONV, cols] * work[c, pad:pad + tm, :]
                for k in range(SSM_CONV - 1):
                    shift = SSM_CONV - 1 - k
                    conv = conv + cw_ref[k:k + 1, cols] * work[c, pad - shift:pad - shift + tm, :]
                carry[cj, c] = work[c, tm:tm + pad, :]
                o_ref[:, cols] = _silu(conv).astype(BF16)

    @pl.when((j != 0) & jnp.logical_not(is_conv))
    def _():
        for p in range(tn // sub):
            o_ref[:, p * sub:(p + 1) * sub] = jnp.dot(
                h_s[...], w_ref[:, p * sub:(p + 1) * sub], preferred_element_type=F32).astype(BF16)


def _inproj_call(x2, norm_w, mod3, cos, sin, w_main, w_dt, conv_w, conv_b, conv_off, seq, tm, tn):
    t, d = x2.shape
    n = w_main.shape[1]
    conv_dim = conv_w.shape[1]
    tiles_per_seq = seq // tm
    assert tn == 2 * RET_HEADS * RET_QK_DIM, "rotary epilogue expects q and k in the first column tile"
    assert conv_off % tn == 0 and conv_dim % tn == 0
    conv_j0, conv_nj = conv_off // tn, conv_dim // tn
    sub = 512
    kern = functools.partial(_inproj_kernel, conv_j0=conv_j0, conv_nj=conv_nj,
                             tiles_per_seq=tiles_per_seq, tm=tm, tn=tn, sub=sub)
    conv_idx = lambda i, j: (0, jnp.clip(j - conv_j0, 0, conv_nj - 1))
    return pl.pallas_call(
        kern,
        out_shape=(jax.ShapeDtypeStruct((t, n), BF16), jax.ShapeDtypeStruct((t, LANES), F32)),
        grid=(t // tm, n // tn),
        in_specs=[
            pl.BlockSpec((tm, d), lambda i, j: (i, 0)),
            pl.BlockSpec((1, d), lambda i, j: (0, 0)),
            pl.BlockSpec((1, 1, d), lambda i, j: ((i // tiles_per_seq) * N_MOD + 1, 0, 0)),
            pl.BlockSpec((1, 1, d), lambda i, j: ((i // tiles_per_seq) * N_MOD + 0, 0, 0)),
            pl.BlockSpec((tm, LANES), lambda i, j: (i % tiles_per_seq, 0)),
            pl.BlockSpec((tm, LANES), lambda i, j: (i % tiles_per_seq, 0)),
            pl.BlockSpec((d, tn), lambda i, j: (0, j)),
            pl.BlockSpec((d, 2 * LANES), lambda i, j: (0, 0)),
            pl.BlockSpec((SSM_CONV, tn), conv_idx),
            pl.BlockSpec((1, tn), conv_idx),
        ],
        out_specs=(pl.BlockSpec((tm, tn), lambda i, j: (i, j)),
                   pl.BlockSpec((tm, LANES), lambda i, j: (i, 0))),
        scratch_shapes=[pltpu.VMEM((tm, d), BF16),
                        pltpu.VMEM((tn // LANES, tm + SUBLANES, LANES), F32),
                        pltpu.VMEM((conv_nj, tn // LANES, SUBLANES, LANES), F32)],
        compiler_params=_params(("arbitrary", "arbitrary")),
        name="inproj",
    )(x2, norm_w, mod3, mod3, cos, sin, w_main, w_dt, conv_w, conv_b.reshape(1, conv_dim))


def _retention_kernel(q_ref, k_ref, v_ref, g_ref, din_ref, dq_ref, dk_ref, w_ref, o_ref, state,
                      *, decay_c):
    c = pl.program_id(1)

    @pl.when(c == 0)
    def _():
        state[...] = jnp.zeros_like(state)

    acc = None
    for h in range(RET_HEADS):
        qh = q_ref[:, h * RET_QK_DIM:(h + 1) * RET_QK_DIM]
        kh = k_ref[:, h * RET_QK_DIM:(h + 1) * RET_QK_DIM]
        vh = v_ref[:, h * RET_V_DIM:(h + 1) * RET_V_DIM]
        scores = _nt_dot(qh, kh) * din_ref[h]
        inner = jnp.dot(scores.astype(BF16), vh, preferred_element_type=F32)
        st = state[h]
        cross = jnp.dot(qh, st.astype(BF16), preferred_element_type=F32) * dq_ref[h]
        kd = (kh.astype(F32) * dk_ref[h]).astype(BF16)
        state[h] = st * decay_c[h] + _tn_dot(kd, vh)
        ret = inner + cross
        ret = ret * lax.rsqrt(jnp.mean(ret * ret, axis=-1, keepdims=True) + EPS)
        gh = g_ref[:, h * RET_V_DIM:(h + 1) * RET_V_DIM].astype(F32)
        ret = ret * _silu(gh)
        part = jnp.dot(ret.astype(BF16), w_ref[h * RET_V_DIM:(h + 1) * RET_V_DIM, :],
                       preferred_element_type=F32)
        acc = part if acc is None else acc + part
    o_ref[...] = acc


def _retention_tables(chunk):
    lg = np.log(1.0 - 2.0 ** (-5.0 - np.arange(RET_HEADS, dtype=np.float64)))
    idx = np.arange(chunk, dtype=np.float64)
    rel = idx[:, None] - idx[None, :]
    causal = rel >= 0
    din = np.where(causal[None], np.exp(np.where(causal, rel, 0.0)[None] * lg[:, None, None]), 0.0)
    dq = np.exp((idx + 1.0)[None, :, None] * lg[:, None, None])
    dk = np.exp((chunk - 1.0 - idx)[None, :, None] * lg[:, None, None])
    dc = tuple(float(v) for v in np.exp(chunk * lg))
    return (jnp.asarray(din, F32), jnp.asarray(dq, F32), jnp.asarray(dk, F32), dc)


def _retention_call(proj, w_ret, bsz, seq, chunk):
    t = proj.shape[0]
    d = w_ret.shape[1]
    nc = seq // chunk
    qk_w = RET_HEADS * RET_QK_DIM
    v_w = RET_HEADS * RET_V_DIM
    din, dq, dk, dc = _retention_tables(chunk)
    kern = functools.partial(_retention_kernel, decay_c=dc)
    row = lambda b, c: b * nc + c
    return pl.pallas_call(
        kern,
        out_shape=jax.ShapeDtypeStruct((t, d), F32),
        grid=(bsz, nc),
        in_specs=[
            pl.BlockSpec((chunk, qk_w), lambda b, c: (row(b, c), 0)),
            pl.BlockSpec((chunk, qk_w), lambda b, c: (row(b, c), 1)),
            pl.BlockSpec((chunk, v_w), lambda b, c: (row(b, c), 1)),
            pl.BlockSpec((chunk, v_w), lambda b, c: (row(b, c), 2)),
            pl.BlockSpec((RET_HEADS, chunk, chunk), lambda b, c: (0, 0, 0)),
            pl.BlockSpec((RET_HEADS, chunk, 1), lambda b, c: (0, 0, 0)),
            pl.BlockSpec((RET_HEADS, chunk, 1), lambda b, c: (0, 0, 0)),
            pl.BlockSpec((v_w, d), lambda b, c: (0, 0)),
        ],
        out_specs=pl.BlockSpec((chunk, d), lambda b, c: (row(b, c), 0)),
        scratch_shapes=[pltpu.VMEM((RET_HEADS, RET_QK_DIM, RET_V_DIM), F32)],
        compiler_params=_params(("arbitrary", "arbitrary")),
        name="retention",
    )(proj, proj, proj, proj, din, dq, dk, w_ret)


def _ssd_kernel(z_ref, xbc_ref, dt_ref, dtb_ref, alog_ref, dsk_ref, nw_ref,
                tril_ref, exp_ref, w_ref, o_ref, state, yn_s, *, chunk, d_inner):
    c = pl.program_id(1)
    heads_per_group = d_inner // SSM_HEAD_DIM // SSM_GROUPS
    gw = heads_per_group * SSM_HEAD_DIM
    assert SSM_HEAD_DIM * 2 == LANES and gw == 2 * LANES

    @pl.when(c == 0)
    def _():
        state[...] = jnp.zeros_like(state)

    dt = jax.nn.softplus(dt_ref[...] + dtb_ref[...])
    a = -jnp.exp(alog_ref[...])
    adt = dt * a
    acs = jnp.dot(tril_ref[...], adt, preferred_element_type=F32, precision=HIGHEST)
    acs_t = acs.T
    dt_x = jnp.dot(dt.astype(BF16), exp_ref[...], preferred_element_type=F32)
    li = lax.broadcasted_iota(jnp.int32, (chunk, chunk), 0)
    si = lax.broadcasted_iota(jnp.int32, (chunk, chunk), 1)
    causal = li >= si
    low_half = si < SSM_HEAD_DIM
    lane_g = lax.broadcasted_iota(jnp.int32, (chunk, gw), 1)

    b_off = d_inner
    c_off = d_inner + SSM_GROUPS * SSM_STATE
    for g in range(SSM_GROUPS):
        bm = xbc_ref[:, b_off + g * SSM_STATE: b_off + (g + 1) * SSM_STATE]
        cm = xbc_ref[:, c_off + g * SSM_STATE: c_off + (g + 1) * SSM_STATE]
        xs_g = xbc_ref[:, g * gw:(g + 1) * gw].astype(F32)
        xdt_g = xs_g * dt_x[:, g * gw:(g + 1) * gw]
        cb = _nt_dot(cm, bm)
        cols, ms, xm = [], [], []
        for jh in range(heads_per_group):
            h = g * heads_per_group + jh
            col = jnp.broadcast_to(acs[:, h:h + 1], (chunk, chunk))
            seg = jnp.exp(jnp.where(causal, col - acs_t[h:h + 1, :], -jnp.inf))
            cols.append(col)
            ms.append((cb * seg).astype(BF16))
            in_head = (lane_g >= jh * SSM_HEAD_DIM) & (lane_g < (jh + 1) * SSM_HEAD_DIM)
            xm.append(jnp.where(in_head, xdt_g, 0.0).astype(BF16))
        y_diag = jnp.dot(jnp.concatenate(ms, axis=-1), jnp.concatenate(xm, axis=0),
                         preferred_element_type=F32)
        a_x = jnp.concatenate([jnp.where(low_half, cols[0], cols[1]),
                               jnp.where(low_half, cols[2], cols[3])], axis=-1)
        e_acs_x = jnp.exp(a_x)
        a_last_x = a_x[chunk - 1:chunk, :]
        st = state[g]
        y_off = jnp.dot(cm, st.astype(BF16), preferred_element_type=F32) * e_acs_x
        xdec = (xdt_g * jnp.exp(a_last_x - a_x)).astype(BF16)
        state[g] = st * e_acs_x[chunk - 1:chunk, :] + _tn_dot(bm, xdec)
        y = y_diag + y_off + dsk_ref[:, g * gw:(g + 1) * gw] * xs_g
        yz = y * _silu(z_ref[:, g * gw:(g + 1) * gw].astype(F32))
        yn = yz * lax.rsqrt(jnp.mean(yz * yz, axis=-1, keepdims=True) + EPS) * nw_ref[:, g * gw:(g + 1) * gw]
        yn_s[:, g * gw:(g + 1) * gw] = yn.astype(BF16)
    o_ref[...] = jnp.dot(yn_s[...], w_ref[...], preferred_element_type=F32)


def _ssd_call(proj, dt_raw, dt_bias, a_log, d_skip, ssm_norm, w_ssm, bsz, seq, chunk):
    t = proj.shape[0]
    d_inner, d = w_ssm.shape
    conv_dim = d_inner + 2 * SSM_GROUPS * SSM_STATE
    n_heads = d_inner // SSM_HEAD_DIM
    nc = seq // chunk
    gw = d_inner // SSM_GROUPS
    pad_h = lambda v: jnp.pad(v.astype(F32), (0, LANES - n_heads)).reshape(1, LANES)
    assert chunk == LANES, "the per-head decay tiles are built lane-for-lane against the chunk"
    tril = jnp.asarray(np.tril(np.ones((chunk, chunk), np.float32)))
    expand = np.zeros((LANES, d_inner), np.float32)
    for h in range(n_heads):
        expand[h, h * SSM_HEAD_DIM:(h + 1) * SSM_HEAD_DIM] = 1.0
    expand = jnp.asarray(expand, BF16)
    kern = functools.partial(_ssd_kernel, chunk=chunk, d_inner=d_inner)
    row = lambda b, c: b * nc + c
    z_blk = (2 * RET_HEADS * RET_QK_DIM + 2 * RET_HEADS * RET_V_DIM) // d_inner
    xbc_blk = (2 * RET_HEADS * RET_QK_DIM + 2 * RET_HEADS * RET_V_DIM + d_inner) // conv_dim
    full = lambda shape: pl.BlockSpec(shape, lambda b, c: (0,) * len(shape))
    return pl.pallas_call(
        kern,
        out_shape=jax.ShapeDtypeStruct((t, d), F32),
        grid=(bsz, nc),
        in_specs=[
            pl.BlockSpec((chunk, d_inner), lambda b, c: (row(b, c), z_blk)),
            pl.BlockSpec((chunk, conv_dim), lambda b, c: (row(b, c), xbc_blk)),
            pl.BlockSpec((chunk, LANES), lambda b, c: (row(b, c), 0)),
            full((1, LANES)), full((1, LANES)),
            full((1, d_inner)), full((1, d_inner)), full((chunk, chunk)), full((LANES, d_inner)),
            full((d_inner, d)),
        ],
        out_specs=pl.BlockSpec((chunk, d), lambda b, c: (row(b, c), 0)),
        scratch_shapes=[pltpu.VMEM((SSM_GROUPS, SSM_STATE, gw), F32),
                        pltpu.VMEM((chunk, d_inner), BF16)],
        compiler_params=_params(("arbitrary", "arbitrary")),
        name="ssd",
    )(proj, proj, dt_raw, pad_h(dt_bias), pad_h(a_log),
      jnp.repeat(d_skip.astype(F32), SSM_HEAD_DIM).reshape(1, d_inner), ssm_norm.reshape(1, d_inner),
      tril, expand, w_ssm)


def _merge_kernel(ya_ref, yb_ref, ga_ref, gb_ref, x_ref, gm_ref, scf_ref, shf_ref, nw_ref, wo_ref,
                  wr_ref, br_ref, tri_ref,
                  x1_ref, h2_ref, idx_ref, rank_ref, prow_ref, cnt_ref, cnt_s, *, tm):
    i = pl.program_id(0)

    @pl.when(i == 0)
    def _():
        cnt_s[...] = jnp.zeros_like(cnt_s)

    merged = (jax.nn.sigmoid(ga_ref[...].astype(F32)) * ya_ref[...]
              + jax.nn.sigmoid(gb_ref[...].astype(F32)) * yb_ref[...])
    mo = jnp.dot(merged.astype(BF16), wo_ref[...], preferred_element_type=F32)
    x1 = x_ref[...] + gm_ref[0] * mo
    x1_ref[...] = x1
    ms = jnp.mean(x1 * x1, axis=-1, keepdims=True)
    h2 = x1 * lax.rsqrt(ms + EPS) * nw_ref[...] * (1.0 + scf_ref[0]) + shf_ref[0]
    for s in range(SUBLANES):
        h2_ref[pl.ds(s, tm, stride=SUBLANES), :] = h2[:, s * LANES:(s + 1) * LANES]

    lg = _nt_dot(wr_ref[...], h2, precision=HIGHEST) + br_ref[...]
    sub = lax.broadcasted_iota(jnp.int32, lg.shape, 0)
    work = lg
    vals, idxs, sels = [], [], []
    for _ in range(TOP_K):
        m = jnp.max(work, axis=0, keepdims=True)
        ik = jnp.min(jnp.where(work == m, sub, N_EXPERTS), axis=0, keepdims=True)
        sel = sub == ik
        vals.append(m)
        idxs.append(ik)
        sels.append(sel)
        work = jnp.where(sel, -jnp.inf, work)
    exps = [jnp.exp(v - vals[0]) for v in vals]
    denom = exps[0]
    for e in exps[1:]:
        denom = denom + e
    probs = [e / denom for e in exps]

    base = cnt_s[:, 0:1]
    ranks = []
    for k in range(TOP_K):
        mk = jnp.where(sels[k], 1.0, 0.0)
        pre = jnp.dot(mk.astype(BF16), tri_ref[...], preferred_element_type=F32)
        ranks.append(jnp.sum(jnp.where(sels[k], pre + base, 0.0), axis=0, keepdims=True))
        base = base + jnp.sum(mk, axis=1, keepdims=True)
    cnt_s[...] = jnp.broadcast_to(base, cnt_s.shape)
    cnt_ref[...] = cnt_s[...].astype(jnp.int32)

    zi = jnp.zeros((SUBLANES - TOP_K, tm), jnp.int32)
    idx_ref[0] = jnp.concatenate(idxs + [zi], axis=0)
    rank_ref[0] = jnp.concatenate([r.astype(jnp.int32) for r in ranks] + [zi], axis=0)
    pt = jnp.concatenate(probs + [jnp.zeros((LANES - TOP_K, tm), F32)], axis=0)
    prow_ref[...] = pt.T


def _merge_call(ya, yb, proj, x2, mod3, norm_w, w_out, w_router_t, b_router, seq, tm):
    t, d = x2.shape
    nt = t // tm
    tiles_per_seq = seq // tm
    ga_blk = proj.shape[1] // d - 2
    tri = jnp.asarray(np.triu(np.ones((tm, tm), np.float32), 1), BF16)
    kern = functools.partial(_merge_kernel, tm=tm)
    modspec = lambda m: pl.BlockSpec((1, 1, d), lambda i: ((i // tiles_per_seq) * N_MOD + m, 0, 0))
    return pl.pallas_call(
        kern,
        out_shape=(jax.ShapeDtypeStruct((t, d), F32), jax.ShapeDtypeStruct((t * SUBLANES, LANES), F32),
                   jax.ShapeDtypeStruct((nt, SUBLANES, tm), jnp.int32),
                   jax.ShapeDtypeStruct((nt, SUBLANES, tm), jnp.int32),
                   jax.ShapeDtypeStruct((t, LANES), F32),
                   jax.ShapeDtypeStruct((N_EXPERTS, LANES), jnp.int32)),
        grid=(nt,),
        in_specs=[
            pl.BlockSpec((tm, d), lambda i: (i, 0)),
            pl.BlockSpec((tm, d), lambda i: (i, 0)),
            pl.BlockSpec((tm, d), lambda i: (i, ga_blk)),
            pl.BlockSpec((tm, d), lambda i: (i, ga_blk + 1)),
            pl.BlockSpec((tm, d), lambda i: (i, 0)),
            modspec(2), modspec(4), modspec(3),
            pl.BlockSpec((1, d), lambda i: (0, 0)),
            pl.BlockSpec((d, d), lambda i: (0, 0)),
            pl.BlockSpec((N_EXPERTS, d), lambda i: (0, 0)),
            pl.BlockSpec((N_EXPERTS, 1), lambda i: (0, 0)),
            pl.BlockSpec((tm, tm), lambda i: (0, 0)),
        ],
        out_specs=(pl.BlockSpec((tm, d), lambda i: (i, 0)),
                   pl.BlockSpec((tm * SUBLANES, LANES), lambda i: (i, 0)),
                   pl.BlockSpec((1, SUBLANES, tm), lambda i: (i, 0, 0)),
                   pl.BlockSpec((1, SUBLANES, tm), lambda i: (i, 0, 0)),
                   pl.BlockSpec((tm, LANES), lambda i: (i, 0)),
                   pl.BlockSpec((N_EXPERTS, LANES), lambda i: (0, 0))),
        scratch_shapes=[pltpu.VMEM((N_EXPERTS, LANES), F32)],
        compiler_params=_params(("arbitrary",)),
        name="merge",
    )(ya, yb, proj, proj, x2, mod3, mod3, mod3, norm_w, w_out, w_router_t,
      b_router.reshape(N_EXPERTS, 1), tri)


def _sc_mesh():
    return plsc.VectorSubcoreMesh(core_axis_name="c", subcore_axis_name="s")


def _sc_worker():
    return lax.axis_index("s") * SC_CORES + lax.axis_index("c")


def _sc_scatter_rows(rows, dest, n_out):
    t = rows.shape[0]
    n_k = dest.shape[0]
    g = SC_GROUP
    n_w = SC_CORES * SC_SUBCORES
    assert t % (n_w * g) == 0
    cpw = t // (n_w * g)
    dest_w = dest.reshape(n_k, n_w, cpw, g).transpose(1, 0, 2, 3)

    @functools.partial(
        pl.kernel, mesh=_sc_mesh(),
        out_type=jax.ShapeDtypeStruct((n_out,) + rows.shape[1:], rows.dtype),
        scratch_types=[pltpu.VMEM((n_k, cpw, g), jnp.int32),
                       pltpu.VMEM((g,) + rows.shape[1:], rows.dtype),
                       pltpu.SemaphoreType.DMA],
    )
    def scatter(rows_hbm, dest_hbm, out_hbm, idx_v, rows_v, sem):
        wid = _sc_worker()
        pltpu.sync_copy(dest_hbm.at[wid], idx_v)

        @pl.loop(0, cpw)
        def _(cc):
            r0 = pl.multiple_of((wid * cpw + cc) * g, g)
            pltpu.sync_copy(rows_hbm.at[pl.ds(r0, g)], rows_v)
            copies = [pltpu.async_copy(rows_v, out_hbm.at[idx_v.at[k, cc]], sem) for k in range(n_k)]
            for cp in copies:
                cp.wait()

    return scatter(rows, dest_w)


def _sc_gather_rows(table, idx):
    m = idx.shape[0]
    g = SC_GROUP
    n_w = SC_CORES * SC_SUBCORES
    assert m % (n_w * g) == 0
    per_w = m // n_w

    @functools.partial(
        pl.kernel, mesh=_sc_mesh(),
        out_type=jax.ShapeDtypeStruct((m,) + table.shape[1:], table.dtype),
        scratch_types=[pltpu.VMEM((per_w,), jnp.int32),
                       pltpu.VMEM((g,) + table.shape[1:], table.dtype),
                       pltpu.SemaphoreType.DMA],
    )
    def gather(table_hbm, idx_hbm, out_hbm, idx_v, rows_v, sem):
        base = _sc_worker() * per_w
        pltpu.sync_copy(idx_hbm.at[pl.ds(base, per_w)], idx_v)

        @pl.loop(0, per_w // g)
        def _(cc):
            off = pl.multiple_of(cc * g, g)
            pltpu.async_copy(table_hbm.at[idx_v.at[pl.ds(off, g)]], rows_v, sem).wait()
            pltpu.sync_copy(rows_v, out_hbm.at[pl.ds(base + off, g)])

    return gather(table, idx)


def _ffn_kernel(be_ref, br_ref, bv_ref, x_ref, wgu_ref, bgu_ref, wd_ref, bd_ref, o_ref, wgu_s, wd_s,
                *, bm, d_ff):
    i = pl.program_id(0)

    @pl.when((i == 0) | (be_ref[i] != be_ref[jnp.maximum(i - 1, 0)]))
    def _():
        wgu_s[...] = wgu_ref[0].astype(BF16)
        wd_s[...] = wd_ref[0].astype(BF16)

    @pl.when(bv_ref[i] == 1)
    def _():
        x = jnp.concatenate([x_ref[pl.ds(s, bm, stride=SUBLANES), :] for s in range(SUBLANES)],
                            axis=-1).astype(BF16)
        gu = jnp.dot(x, wgu_s[...], preferred_element_type=F32) + bgu_ref[0]
        gate = jnp.minimum(gu[:, :d_ff], SWIGLU_LIMIT)
        up = jnp.clip(gu[:, d_ff:], -SWIGLU_LIMIT, SWIGLU_LIMIT)
        act = gate * jax.nn.sigmoid(SWIGLU_ALPHA * gate) * (up + 1.0)
        y = jnp.dot(act.astype(BF16), wd_s[...], preferred_element_type=F32) + bd_ref[0]
        for s in range(SUBLANES):
            o_ref[pl.ds(s, bm, stride=SUBLANES), :] = y[:, s * LANES:(s + 1) * LANES]

    @pl.when(bv_ref[i] == 0)
    def _():
        o_ref[...] = jnp.zeros_like(o_ref)


def _ffn_call(blk_e, blk_row, blk_valid, xs, w_gu, b_gu, w_d, b_d, bm):
    n_e, d, f2 = w_gu.shape
    d_ff = f2 // 2
    nb = blk_e.shape[0]
    kern = functools.partial(_ffn_kernel, bm=bm, d_ff=d_ff)
    gs = pltpu.PrefetchScalarGridSpec(
        num_scalar_prefetch=3,
        grid=(nb,),
        in_specs=[pl.BlockSpec((bm * SUBLANES, LANES), lambda i, be, br, bv: (br[i], 0)),
                  pl.BlockSpec((1, d, f2), lambda i, be, br, bv: (be[i], 0, 0)),
                  pl.BlockSpec((1, 1, f2), lambda i, be, br, bv: (be[i], 0, 0)),
                  pl.BlockSpec((1, d_ff, d), lambda i, be, br, bv: (be[i], 0, 0)),
                  pl.BlockSpec((1, 1, d), lambda i, be, br, bv: (be[i], 0, 0))],
        out_specs=pl.BlockSpec((bm * SUBLANES, LANES), lambda i, be, br, bv: (i, 0)),
        scratch_shapes=[pltpu.VMEM((d, f2), BF16), pltpu.VMEM((d_ff, d), BF16)],
    )
    return pl.pallas_call(
        kern,
        out_shape=jax.ShapeDtypeStruct((nb * bm * SUBLANES, LANES), F32),
        grid_spec=gs,
        compiler_params=_params(("arbitrary",)),
        name="ffn",
    )(blk_e, blk_row, blk_valid, xs, w_gu, b_gu.reshape(n_e, 1, f2), w_d, b_d.reshape(n_e, 1, d))


def _combine_kernel(y0_ref, y1_ref, y2_ref, y3_ref, prow_ref, x1_ref, gf_ref, nw_ref, o_ref, *, tm):
    p = prow_ref[...]
    ys = (y0_ref, y1_ref, y2_ref, y3_ref)
    for s in range(SUBLANES):
        moe = None
        for k in range(TOP_K):
            piece = ys[k][pl.ds(s, tm, stride=SUBLANES), :] * p[:, k:k + 1]
            moe = piece if moe is None else moe + piece
        sl = slice(s * LANES, (s + 1) * LANES)
        o_ref[:, sl] = x1_ref[:, sl] + gf_ref[0][:, sl] * moe
    xo = o_ref[...]
    o_ref[...] = xo * lax.rsqrt(jnp.mean(xo * xo, axis=-1, keepdims=True) + EPS) * nw_ref[...]


def _combine_call(ytok, prow, x1, mod3, norm_final, seq, tm):
    t, d = x1.shape
    nt = t // tm
    tiles_per_seq = seq // tm
    kern = functools.partial(_combine_kernel, tm=tm)
    yspec = lambda k: pl.BlockSpec((tm * SUBLANES, LANES), lambda i: (k * nt + i, 0))
    return pl.pallas_call(
        kern,
        out_shape=jax.ShapeDtypeStruct((t, d), F32),
        grid=(nt,),
        in_specs=[yspec(0), yspec(1), yspec(2), yspec(3),
                  pl.BlockSpec((tm, LANES), lambda i: (i, 0)),
                  pl.BlockSpec((tm, d), lambda i: (i, 0)),
                  pl.BlockSpec((1, 1, d), lambda i: ((i // tiles_per_seq) * N_MOD + 5, 0, 0)),
                  pl.BlockSpec((1, d), lambda i: (0, 0))],
        out_specs=pl.BlockSpec((tm, d), lambda i: (i, 0)),
        compiler_params=_params(("arbitrary",)),
        name="combine",
    )(ytok, ytok, ytok, ytok, prow, x1, mod3, norm_final)


def _plan(seq):
    def fit(pref):
        tm = min(pref, seq)
        assert seq % tm == 0
        return tm
    return dict(tm_in=fit(1024), tm_merge=fit(512), tm_moe=fit(512),
                ret_chunk=fit(RET_CHUNK), ssm_chunk=fit(SSM_CHUNK))


def _layer(x2, mod3, bsz, seq, norm_mix, norm_ffn, w_in, conv_w, conv_b, dt_bias, a_log, d_skip, ssm_norm,
           w_ret_out, w_ssm_out, w_out, w_router, b_router, w_gate_up, b_gate_up, w_down, b_down,
           norm_final):
    t, d = x2.shape
    plan = _plan(seq)
    qk_w = RET_HEADS * RET_QK_DIM
    v_w = RET_HEADS * RET_V_DIM
    d_inner = w_ssm_out.shape[0]
    conv_dim = conv_w.shape[1]
    n_heads = d_inner // SSM_HEAD_DIM
    dt_off = 2 * qk_w + 2 * v_w + d_inner + conv_dim

    w_main = jnp.concatenate([w_in[:, :dt_off], w_in[:, dt_off + n_heads:]], axis=1).astype(BF16)
    w_dt = jnp.pad(w_in[:, dt_off:dt_off + n_heads], ((0, 0), (0, LANES - n_heads)))
    w_dt_hi = w_dt.astype(BF16)
    w_dt = jnp.concatenate([w_dt_hi, (w_dt - w_dt_hi.astype(F32)).astype(BF16)], axis=1)
    half = RET_QK_DIM // 2
    inv_freq = ROPE_BASE ** (-jnp.arange(half, dtype=F32) / half)
    ang = jnp.arange(seq, dtype=F32)[:, None] * inv_freq[None, :]
    cos, sin = jnp.cos(ang), jnp.sin(ang)

    proj, dt_raw = _inproj_call(x2, norm_mix.reshape(1, d), mod3, cos, sin, w_main, w_dt, conv_w, conv_b,
                                2 * qk_w + 2 * v_w + d_inner, seq, plan["tm_in"], 2 * qk_w)
    ya = _retention_call(proj, w_ret_out.astype(BF16), bsz, seq, plan["ret_chunk"])
    yb = _ssd_call(proj, dt_raw, dt_bias, a_log, d_skip, ssm_norm,
                   w_ssm_out.astype(BF16), bsz, seq, plan["ssm_chunk"])
    tm = plan["tm_merge"]
    x1, h2, idx, rank, prow, cnt = _merge_call(ya, yb, proj, x2, mod3, norm_ffn.reshape(1, d),
                                               w_out.astype(BF16), w_router.T, b_router, seq, tm)

    bm = FFN_BLOCK
    counts = cnt[:, 0]
    padded = ((counts + bm - 1) // bm) * bm
    pad_end = jnp.cumsum(padded)
    start_pad = pad_end - padded
    n_blocks = (t * TOP_K) // bm + N_EXPERTS
    e_ids = jnp.arange(N_EXPERTS, dtype=jnp.int32)[:, None, None, None]
    dest = rank + jnp.sum(jnp.where(idx[None] == e_ids, start_pad[:, None, None, None], 0), axis=0)
    dest = dest[:, :TOP_K, :].transpose(1, 0, 2).reshape(TOP_K, t).astype(jnp.int32)
    n_real = pad_end[-1] // bm
    blk_valid = (jnp.arange(n_blocks) < n_real).astype(jnp.int32)
    blk_row = jnp.minimum(jnp.arange(n_blocks), n_real - 1).astype(jnp.int32)
    blk_e = jnp.minimum(jnp.sum(pad_end[None, :] <= (blk_row * bm)[:, None], axis=1),
                        N_EXPERTS - 1).astype(jnp.int32)

    slab = (SUBLANES, LANES)
    xs = _sc_scatter_rows(h2.reshape((t,) + slab), dest, n_blocks * bm)
    ys = _ffn_call(blk_e, blk_row, blk_valid, xs.reshape(n_blocks * bm * SUBLANES, LANES),
                   w_gate_up, b_gate_up, w_down, b_down, bm)
    ytok = _sc_gather_rows(ys.reshape((n_blocks * bm,) + slab), dest.reshape(TOP_K * t))
    return _combine_call(ytok.reshape(TOP_K * t * SUBLANES, LANES), prow, x1, mod3,
                         norm_final.reshape(1, d), seq, plan["tm_moe"])


def kernel(x, c, w_ada, b_ada, norm_mix, norm_ffn, w_in, conv_w, conv_b, dt_bias, a_log, d_skip, ssm_norm,
           w_ret_out, w_ssm_out, w_out, w_router, b_router, w_gate_up, b_gate_up, w_down, b_down, norm_final):
    bsz, seq, d = x.shape
    depth = w_ada.shape[0]
    assert depth == 1, "the final norm is fused into the single layer's last kernel"
    x2 = x.reshape(bsz * seq, d)
    l = 0
    mod = _mod_call(c, w_ada[l], b_ada[l])
    mod3 = mod.reshape(bsz * N_MOD, 1, d)
    out = _layer(x2, mod3, bsz, seq, norm_mix[l], norm_ffn[l], w_in[l], conv_w[l], conv_b[l], dt_bias[l],
                 a_log[l], d_skip[l], ssm_norm[l], w_ret_out[l], w_ssm_out[l], w_out[l], w_router[l],
                 b_router[l], w_gate_up[l], b_gate_up[l], w_down[l], b_down[l], norm_final)
    return out.reshape(bsz, seq, d)
```

```python
import functools
import math

import numpy as np
import jax
import jax.numpy as jnp
from jax import lax
from jax.experimental import pallas as pl
from jax.experimental.pallas import tpu as pltpu
from jax.experimental.pallas import tpu_sc as plsc

F32 = jnp.float32
BF16 = jnp.bfloat16
HIGHEST = lax.Precision.HIGHEST

EPS = 1e-6
N_MOD = 6
RET_HEADS = 4
RET_QK_DIM = 256
RET_V_DIM = 512
ROPE_BASE = 10000.0
SSM_HEAD_DIM = 64
SSM_GROUPS = 8
SSM_STATE = 128
SSM_CONV = 4
N_EXPERTS = 32
TOP_K = 4
SWIGLU_LIMIT = 7.0
SWIGLU_ALPHA = 1.702

LANES = 128
SUBLANES = 8
VMEM_LIMIT = 56 * 1024 * 1024

RET_CHUNK = 256
SSM_CHUNK = 128
FFN_BLOCK = 512
SC_CORES = 2
SC_SUBCORES = 16
SC_GROUP = 32
N_GROUPS = 2


def _params(sem, vmem=VMEM_LIMIT):
    return pltpu.CompilerParams(dimension_semantics=sem, vmem_limit_bytes=vmem)


def _nt_dot(a, b, **kw):
    return lax.dot_general(a, b, (((1,), (1,)), ((), ())), preferred_element_type=F32, **kw)


def _tn_dot(a, b, **kw):
    return lax.dot_general(a, b, (((0,), (0,)), ((), ())), preferred_element_type=F32, **kw)


def _silu(v):
    return v * jax.nn.sigmoid(v)


def _mod_kernel(c_ref, w_ref, b_ref, o_ref):
    cond = _silu(c_ref[...])
    o_ref[...] = jnp.dot(cond, w_ref[...], preferred_element_type=F32, precision=HIGHEST) + b_ref[...]


def _mod_call(c, w_ada, b_ada):
    bsz, d = c.shape
    n = w_ada.shape[1]
    return pl.pallas_call(
        _mod_kernel,
        out_shape=jax.ShapeDtypeStruct((bsz, n), F32),
        grid=(n // d,),
        in_specs=[pl.BlockSpec((bsz, d), lambda j: (0, 0)),
                  pl.BlockSpec((d, d), lambda j: (0, j)),
                  pl.BlockSpec((1, d), lambda j: (0, j))],
        out_specs=pl.BlockSpec((bsz, d), lambda j: (0, j)),
        compiler_params=_params(("arbitrary",)),
        name="mod",
    )(c, w_ada, b_ada.reshape(1, n))


def _inproj_kernel(x_ref, nw_ref, sc_ref, sh_ref, cos_ref, sin_ref, w_ref, wdt_ref, cw_ref, cb_ref,
                   o_ref, dt_ref, h_s, work, carry, *, conv_j0, conv_nj, tiles_per_seq, tm, tn, sub):
    i = pl.program_id(0)
    j = pl.program_id(1)
    n_dt = dt_ref.shape[1]

    @pl.when(j == 0)
    def _():
        xf = x_ref[...]
        ms = jnp.mean(xf * xf, axis=-1, keepdims=True)
        y = xf * lax.rsqrt(ms + EPS) * nw_ref[...]
        hm = y * (1.0 + sc_ref[0]) + sh_ref[0]
        hb = hm.astype(BF16)
        h_s[...] = hb
        h_lo = (hm - hb.astype(F32)).astype(BF16)
        d_hi = jnp.dot(hb, wdt_ref[...], preferred_element_type=F32)
        d_lo = jnp.dot(h_lo, wdt_ref[:, :n_dt], preferred_element_type=F32)
        dt_ref[...] = d_hi[:, :n_dt] + d_hi[:, n_dt:] + d_lo
        cos = cos_ref[...]
        sin = sin_ref[...]
        half = RET_QK_DIM // 2
        for p in range(tn // sub):
            acc = jnp.dot(hb, w_ref[:, p * sub:(p + 1) * sub], preferred_element_type=F32)
            for cc in range(sub // RET_QK_DIM):
                c = p * (sub // RET_QK_DIM) + cc
                a = acc[:, cc * RET_QK_DIM: cc * RET_QK_DIM + half]
                b = acc[:, cc * RET_QK_DIM + half: (cc + 1) * RET_QK_DIM]
                scale = 1.0 if c < RET_HEADS else RET_QK_DIM ** -0.5
                o_ref[:, c * RET_QK_DIM: c * RET_QK_DIM + half] = ((a * cos - b * sin) * scale).astype(BF16)
                o_ref[:, c * RET_QK_DIM + half: (c + 1) * RET_QK_DIM] = ((a * sin + b * cos) * scale).astype(BF16)

    is_conv = (j >= conv_j0) & (j < conv_j0 + conv_nj)

    @pl.when(is_conv)
    def _():
        cj = j - conv_j0
        pad = SUBLANES

        @pl.when(i % tiles_per_seq == 0)
        def _():
            carry[cj] = jnp.zeros(carry.shape[1:], F32)

        for p in range(tn // sub):
            acc = jnp.dot(h_s[...], w_ref[:, p * sub:(p + 1) * sub], preferred_element_type=F32)
            for cc in range(sub // LANES):
                c = p * (sub // LANES) + cc
                cols = slice(c * LANES, (c + 1) * LANES)
                work[c, 0:pad, :] = carry[cj, c]
                work[c, pad:pad + tm, :] = acc[:, cc * LANES:(cc + 1) * LANES]
                conv = cb_ref[:, cols] + cw_ref[SSM_CONV - 1:SSM_CONV, cols] * work[c, pad:pad + tm, :]
                for k in range(SSM_CONV - 1):
                    shift = SSM_CONV - 1 - k
                    conv = conv + cw_ref[k:k + 1, cols] * work[c, pad - shift:pad - shift + tm, :]
                carry[cj, c] = work[c, tm:tm + pad, :]
                o_ref[:, cols] = _silu(conv).astype(BF16)

    @pl.when((j != 0) & jnp.logical_not(is_conv))
    def _():
        for p in range(tn // sub):
            o_ref[:, p * sub:(p + 1) * sub] = jnp.dot(
                h_s[...], w_ref[:, p * sub:(p + 1) * sub], preferred_element_type=F32).astype(BF16)


def _inproj_call(x2, norm_w, mod3, cos, sin, w_main, w_dt, conv_w, conv_b, conv_off, seq, tm, tn):
    t, d = x2.shape
    n = w_main.shape[1]
    conv_dim = conv_w.shape[1]
    tiles_per_seq = seq // tm
    assert tn == 2 * RET_HEADS * RET_QK_DIM, "rotary epilogue expects q and k in the first column tile"
    assert conv_off % tn == 0 and conv_dim % tn == 0
    conv_j0, conv_nj = conv_off // tn, conv_dim // tn
    sub = 512
    kern = functools.partial(_inproj_kernel, conv_j0=conv_j0, conv_nj=conv_nj,
                             tiles_per_seq=tiles_per_seq, tm=tm, tn=tn, sub=sub)
    conv_idx = lambda i, j: (0, jnp.clip(j - conv_j0, 0, conv_nj - 1))
    return pl.pallas_call(
        kern,
        out_shape=(jax.ShapeDtypeStruct((t, n), BF16), jax.ShapeDtypeStruct((t, LANES), F32)),
        grid=(t // tm, n // tn),
        in_specs=[
            pl.BlockSpec((tm, d), lambda i, j: (i, 0)),
            pl.BlockSpec((1, d), lambda i, j: (0, 0)),
            pl.BlockSpec((1, 1, d), lambda i, j: ((i // tiles_per_seq) * N_MOD + 1, 0, 0)),
            pl.BlockSpec((1, 1, d), lambda i, j: ((i // tiles_per_seq) * N_MOD + 0, 0, 0)),
            pl.BlockSpec((tm, LANES), lambda i, j: (i % tiles_per_seq, 0)),
            pl.BlockSpec((tm, LANES), lambda i, j: (i % tiles_per_seq, 0)),
            pl.BlockSpec((d, tn), lambda i, j: (0, j)),
            pl.BlockSpec((d, 2 * LANES), lambda i, j: (0, 0)),
            pl.BlockSpec((SSM_CONV, tn), conv_idx),
            pl.BlockSpec((1, tn), conv_idx),
        ],
        out_specs=(pl.BlockSpec((tm, tn), lambda i, j: (i, j)),
                   pl.BlockSpec((tm, LANES), lambda i, j: (i, 0))),
        scratch_shapes=[pltpu.VMEM((tm, d), BF16),
                        pltpu.VMEM((tn // LANES, tm + SUBLANES, LANES), F32),
                        pltpu.VMEM((conv_nj, tn // LANES, SUBLANES, LANES), F32)],
        compiler_params=_params(("arbitrary", "arbitrary")),
        name="inproj",
    )(x2, norm_w, mod3, mod3, cos, sin, w_main, w_dt, conv_w, conv_b.reshape(1, conv_dim))


def _retention_kernel(q_ref, k_ref, v_ref, g_ref, din_ref, dq_ref, dk_ref, w_ref, o_ref, state,
                      *, decay_c):
    c = pl.program_id(1)

    @pl.when(c == 0)
    def _():
        state[...] = jnp.zeros_like(state)

    acc = None
    for h in range(RET_HEADS):
        qh = q_ref[:, h * RET_QK_DIM:(h + 1) * RET_QK_DIM]
        kh = k_ref[:, h * RET_QK_DIM:(h + 1) * RET_QK_DIM]
        vh = v_ref[:, h * RET_V_DIM:(h + 1) * RET_V_DIM]
        scores = _nt_dot(qh, kh) * din_ref[h]
        inner = jnp.dot(scores.astype(BF16), vh, preferred_element_type=F32)
        st = state[h]
        cross = jnp.dot(qh, st.astype(BF16), preferred_element_type=F32) * dq_ref[h]
        kd = (kh.astype(F32) * dk_ref[h]).astype(BF16)
        state[h] = st * decay_c[h] + _tn_dot(kd, vh)
        ret = inner + cross
        ret = ret * lax.rsqrt(jnp.mean(ret * ret, axis=-1, keepdims=True) + EPS)
        gh = g_ref[:, h * RET_V_DIM:(h + 1) * RET_V_DIM].astype(F32)
        ret = ret * _silu(gh)
        part = jnp.dot(ret.astype(BF16), w_ref[h * RET_V_DIM:(h + 1) * RET_V_DIM, :],
                       preferred_element_type=F32)
        acc = part if acc is None else acc + part
    o_ref[...] = acc


def _retention_tables(chunk):
    lg = np.log(1.0 - 2.0 ** (-5.0 - np.arange(RET_HEADS, dtype=np.float64)))
    idx = np.arange(chunk, dtype=np.float64)
    rel = idx[:, None] - idx[None, :]
    causal = rel >= 0
    din = np.where(causal[None], np.exp(np.where(causal, rel, 0.0)[None] * lg[:, None, None]), 0.0)
    dq = np.exp((idx + 1.0)[None, :, None] * lg[:, None, None])
    dk = np.exp((chunk - 1.0 - idx)[None, :, None] * lg[:, None, None])
    dc = tuple(float(v) for v in np.exp(chunk * lg))
    return (jnp.asarray(din, F32), jnp.asarray(dq, F32), jnp.asarray(dk, F32), dc)


def _retention_call(proj, w_ret, bsz, seq, chunk):
    t = proj.shape[0]
    d = w_ret.shape[1]
    nc = seq // chunk
    qk_w = RET_HEADS * RET_QK_DIM
    v_w = RET_HEADS * RET_V_DIM
    din, dq, dk, dc = _retention_tables(chunk)
    kern = functools.partial(_retention_kernel, decay_c=dc)
    row = lambda b, c: b * nc + c
    return pl.pallas_call(
        kern,
        out_shape=jax.ShapeDtypeStruct((t, d), F32),
        grid=(bsz, nc),
        in_specs=[
            pl.BlockSpec((chunk, qk_w), lambda b, c: (row(b, c), 0)),
            pl.BlockSpec((chunk, qk_w), lambda b, c: (row(b, c), 1)),
            pl.BlockSpec((chunk, v_w), lambda b, c: (row(b, c), 1)),
            pl.BlockSpec((chunk, v_w), lambda b, c: (row(b, c), 2)),
            pl.BlockSpec((RET_HEADS, chunk, chunk), lambda b, c: (0, 0, 0)),
            pl.BlockSpec((RET_HEADS, chunk, 1), lambda b, c: (0, 0, 0)),
            pl.BlockSpec((RET_HEADS, chunk, 1), lambda b, c: (0, 0, 0)),
            pl.BlockSpec((v_w, d), lambda b, c: (0, 0)),
        ],
        out_specs=pl.BlockSpec((chunk, d), lambda b, c: (row(b, c), 0)),
        scratch_shapes=[pltpu.VMEM((RET_HEADS, RET_QK_DIM, RET_V_DIM), F32)],
        compiler_params=_params(("arbitrary", "arbitrary")),
        name="retention",
    )(proj, proj, proj, proj, din, dq, dk, w_ret)


def _ssd_kernel(z_ref, xbc_ref, dt_ref, dtb_ref, alog_ref, dsk_ref, nw_ref,
                tril_ref, exp_ref, w_ref, o_ref, state, yn_s, *, chunk, d_inner):
    c = pl.program_id(1)
    heads_per_group = d_inner // SSM_HEAD_DIM // SSM_GROUPS
    gw = heads_per_group * SSM_HEAD_DIM
    assert SSM_HEAD_DIM * 2 == LANES and gw == 2 * LANES

    @pl.when(c == 0)
    def _():
        state[...] = jnp.zeros_like(state)

    dt = jax.nn.softplus(dt_ref[...] + dtb_ref[...])
    a = -jnp.exp(alog_ref[...])
    adt = dt * a
    acs = jnp.dot(tril_ref[...], adt, preferred_element_type=F32, precision=HIGHEST)
    acs_t = acs.T
    dt_x = jnp.dot(dt.astype(BF16), exp_ref[...], preferred_element_type=F32)
    li = lax.broadcasted_iota(jnp.int32, (chunk, chunk), 0)
    si = lax.broadcasted_iota(jnp.int32, (chunk, chunk), 1)
    causal = li >= si
    low_half = si < SSM_HEAD_DIM
    lane_g = lax.broadcasted_iota(jnp.int32, (chunk, gw), 1)

    b_off = d_inner
    c_off = d_inner + SSM_GROUPS * SSM_STATE
    for g in range(SSM_GROUPS):
        bm = xbc_ref[:, b_off + g * SSM_STATE: b_off + (g + 1) * SSM_STATE]
        cm = xbc_ref[:, c_off + g * SSM_STATE: c_off + (g + 1) * SSM_STATE]
        xs_g = xbc_ref[:, g * gw:(g + 1) * gw].astype(F32)
        xdt_g = xs_g * dt_x[:, g * gw:(g + 1) * gw]
        cb = _nt_dot(cm, bm)
        cols, ms, xm = [], [], []
        for jh in range(heads_per_group):
            h = g * heads_per_group + jh
            col = jnp.broadcast_to(acs[:, h:h + 1], (chunk, chunk))
            seg = jnp.exp(jnp.where(causal, col - acs_t[h:h + 1, :], -jnp.inf))
            cols.append(col)
            ms.append((cb * seg).astype(BF16))
            in_head = (lane_g >= jh * SSM_HEAD_DIM) & (lane_g < (jh + 1) * SSM_HEAD_DIM)
            xm.append(jnp.where(in_head, xdt_g, 0.0).astype(BF16))
        y_diag = jnp.dot(jnp.concatenate(ms, axis=-1), jnp.concatenate(xm, axis=0),
                         preferred_element_type=F32)
        a_x = jnp.concatenate([jnp.where(low_half, cols[0], cols[1]),
                               jnp.where(low_half, cols[2], cols[3])], axis=-1)
        e_acs_x = jnp.exp(a_x)
        a_last_x = a_x[chunk - 1:chunk, :]
        st = state[g]
        y_off = jnp.dot(cm, st.astype(BF16), preferred_element_type=F32) * e_acs_x
        xdec = (xdt_g * jnp.exp(a_last_x - a_x)).astype(BF16)
        state[g] = st * e_acs_x[chunk - 1:chunk, :] + _tn_dot(bm, xdec)
        y = y_diag + y_off + dsk_ref[:, g * gw:(g + 1) * gw] * xs_g
        yz = y * _silu(z_ref[:, g * gw:(g + 1) * gw].astype(F32))
        yn = yz * lax.rsqrt(jnp.mean(yz * yz, axis=-1, keepdims=True) + EPS) * nw_ref[:, g * gw:(g + 1) * gw]
        yn_s[:, g * gw:(g + 1) * gw] = yn.astype(BF16)
    o_ref[...] = jnp.dot(yn_s[...], w_ref[...], preferred_element_type=F32)


def _ssd_call(proj, dt_raw, dt_bias, a_log, d_skip, ssm_norm, w_ssm, bsz, seq, chunk):
    t = proj.shape[0]
    d_inner, d = w_ssm.shape
    conv_dim = d_inner + 2 * SSM_GROUPS * SSM_STATE
    n_heads = d_inner // SSM_HEAD_DIM
    nc = seq // chunk
    gw = d_inner // SSM_GROUPS
    pad_h = lambda v: jnp.pad(v.astype(F32), (0, LANES - n_heads)).reshape(1, LANES)
    assert chunk == LANES, "the per-head decay tiles are built lane-for-lane against the chunk"
    tril = jnp.asarray(np.tril(np.ones((chunk, chunk), np.float32)))
    expand = np.zeros((LANES, d_inner), np.float32)
    for h in range(n_heads):
        expand[h, h * SSM_HEAD_DIM:(h + 1) * SSM_HEAD_DIM] = 1.0
    expand = jnp.asarray(expand, BF16)
    kern = functools.partial(_ssd_kernel, chunk=chunk, d_inner=d_inner)
    row = lambda b, c: b * nc + c
    z_blk = (2 * RET_HEADS * RET_QK_DIM + 2 * RET_HEADS * RET_V_DIM) // d_inner
    xbc_blk = (2 * RET_HEADS * RET_QK_DIM + 2 * RET_HEADS * RET_V_DIM + d_inner) // conv_dim
    full = lambda shape: pl.BlockSpec(shape, lambda b, c: (0,) * len(shape))
    return pl.pallas_call(
        kern,
        out_shape=jax.ShapeDtypeStruct((t, d), F32),
        grid=(bsz, nc),
        in_specs=[
            pl.BlockSpec((chunk, d_inner), lambda b, c: (row(b, c), z_blk)),
            pl.BlockSpec((chunk, conv_dim), lambda b, c: (row(b, c), xbc_blk)),
            pl.BlockSpec((chunk, LANES), lambda b, c: (row(b, c), 0)),
            full((1, LANES)), full((1, LANES)),
            full((1, d_inner)), full((1, d_inner)), full((chunk, chunk)), full((LANES, d_inner)),
            full((d_inner, d)),
        ],
        out_specs=pl.BlockSpec((chunk, d), lambda b, c: (row(b, c), 0)),
        scratch_shapes=[pltpu.VMEM((SSM_GROUPS, SSM_STATE, gw), F32),
                        pltpu.VMEM((chunk, d_inner), BF16)],
        compiler_params=_params(("arbitrary", "arbitrary")),
        name="ssd",
    )(proj, proj, dt_raw, pad_h(dt_bias), pad_h(a_log),
      jnp.repeat(d_skip.astype(F32), SSM_HEAD_DIM).reshape(1, d_inner), ssm_norm.reshape(1, d_inner),
      tril, expand, w_ssm)


def _merge_kernel(ya_ref, yb_ref, ga_ref, gb_ref, x_ref, gm_ref, scf_ref, shf_ref, nw_ref, wo_ref,
                  wr_ref, br_ref, tri_ref,
                  x1_ref, h2_ref, idx_ref, rank_ref, prow_ref, cnt_ref, cnt_s, *, tm):
    i = pl.program_id(0)

    @pl.when(i == 0)
    def _():
        cnt_s[...] = jnp.zeros_like(cnt_s)

    merged = (jax.nn.sigmoid(ga_ref[...].astype(F32)) * ya_ref[...]
              + jax.nn.sigmoid(gb_ref[...].astype(F32)) * yb_ref[...])
    mo = jnp.dot(merged.astype(BF16), wo_ref[...], preferred_element_type=F32)
    x1 = x_ref[...] + gm_ref[0] * mo
    x1_ref[...] = x1
    ms = jnp.mean(x1 * x1, axis=-1, keepdims=True)
    h2 = x1 * lax.rsqrt(ms + EPS) * nw_ref[...] * (1.0 + scf_ref[0]) + shf_ref[0]
    for s in range(SUBLANES):
        h2_ref[pl.ds(s, tm, stride=SUBLANES), :] = h2[:, s * LANES:(s + 1) * LANES]

    lg = _nt_dot(wr_ref[...], h2, precision=HIGHEST) + br_ref[...]
    sub = lax.broadcasted_iota(jnp.int32, lg.shape, 0)
    work = lg
    vals, idxs, sels = [], [], []
    for _ in range(TOP_K):
        m = jnp.max(work, axis=0, keepdims=True)
        ik = jnp.min(jnp.where(work == m, sub, N_EXPERTS), axis=0, keepdims=True)
        sel = sub == ik
        vals.append(m)
        idxs.append(ik)
        sels.append(sel)
        work = jnp.where(sel, -jnp.inf, work)
    exps = [jnp.exp(v - vals[0]) for v in vals]
    denom = exps[0]
    for e in exps[1:]:
        denom = denom + e
    probs = [e / denom for e in exps]

    base = cnt_s[:, 0:1]
    ranks = []
    for k in range(TOP_K):
        mk = jnp.where(sels[k], 1.0, 0.0)
        pre = jnp.dot(mk.astype(BF16), tri_ref[...], preferred_element_type=F32)
        ranks.append(jnp.sum(jnp.where(sels[k], pre + base, 0.0), axis=0, keepdims=True))
        base = base + jnp.sum(mk, axis=1, keepdims=True)
    cnt_s[...] = jnp.broadcast_to(base, cnt_s.shape)
    cnt_ref[...] = cnt_s[...].astype(jnp.int32)

    zi = jnp.zeros((SUBLANES - TOP_K, tm), jnp.int32)
    idx_ref[0] = jnp.concatenate(idxs + [zi], axis=0)
    rank_ref[0] = jnp.concatenate([r.astype(jnp.int32) for r in ranks] + [zi], axis=0)
    pt = jnp.concatenate(probs + [jnp.zeros((LANES - TOP_K, tm), F32)], axis=0)
    prow_ref[...] = pt.T


def _merge_call(ya, yb, proj, x2, mod3, norm_w, w_out, w_router_t, b_router, seq, tm):
    t, d = x2.shape
    nt = t // tm
    tiles_per_seq = seq // tm
    ga_blk = proj.shape[1] // d - 2
    tri = jnp.asarray(np.triu(np.ones((tm, tm), np.float32), 1), BF16)
    kern = functools.partial(_merge_kernel, tm=tm)
    modspec = lambda m: pl.BlockSpec((1, 1, d), lambda i: ((i // tiles_per_seq) * N_MOD + m, 0, 0))
    return pl.pallas_call(
        kern,
        out_shape=(jax.ShapeDtypeStruct((t, d), F32), jax.ShapeDtypeStruct((t * SUBLANES, LANES), F32),
                   jax.ShapeDtypeStruct((nt, SUBLANES, tm), jnp.int32),
                   jax.ShapeDtypeStruct((nt, SUBLANES, tm), jnp.int32),
                   jax.ShapeDtypeStruct((t, LANES), F32),
                   jax.ShapeDtypeStruct((N_EXPERTS, LANES), jnp.int32)),
        grid=(nt,),
        in_specs=[
            pl.BlockSpec((tm, d), lambda i: (i, 0)),
            pl.BlockSpec((tm, d), lambda i: (i, 0)),
            pl.BlockSpec((tm, d), lambda i: (i, ga_blk)),
            pl.BlockSpec((tm, d), lambda i: (i, ga_blk + 1)),
            pl.BlockSpec((tm, d), lambda i: (i, 0)),
            modspec(2), modspec(4), modspec(3),
            pl.BlockSpec((1, d), lambda i: (0, 0)),
            pl.BlockSpec((d, d), lambda i: (0, 0)),
            pl.BlockSpec((N_EXPERTS, d), lambda i: (0, 0)),
            pl.BlockSpec((N_EXPERTS, 1), lambda i: (0, 0)),
            pl.BlockSpec((tm, tm), lambda i: (0, 0)),
        ],
        out_specs=(pl.BlockSpec((tm, d), lambda i: (i, 0)),
                   pl.BlockSpec((tm * SUBLANES, LANES), lambda i: (i, 0)),
                   pl.BlockSpec((1, SUBLANES, tm), lambda i: (i, 0, 0)),
                   pl.BlockSpec((1, SUBLANES, tm), lambda i: (i, 0, 0)),
                   pl.BlockSpec((tm, LANES), lambda i: (i, 0)),
                   pl.BlockSpec((N_EXPERTS, LANES), lambda i: (0, 0))),
        scratch_shapes=[pltpu.VMEM((N_EXPERTS, LANES), F32)],
        compiler_params=_params(("arbitrary",)),
        name="merge",
    )(ya, yb, proj, proj, x2, mod3, mod3, mod3, norm_w, w_out, w_router_t,
      b_router.reshape(N_EXPERTS, 1), tri)


def _sc_mesh():
    return plsc.VectorSubcoreMesh(core_axis_name="c", subcore_axis_name="s")


def _sc_worker():
    return lax.axis_index("s") * SC_CORES + lax.axis_index("c")


def _sc_scatter_rows(rows, dest, n_out):
    t = rows.shape[0]
    n_k = dest.shape[0]
    g = SC_GROUP
    n_w = SC_CORES * SC_SUBCORES
    assert t % (n_w * g) == 0
    cpw = t // (n_w * g)
    dest_w = dest.reshape(n_k, n_w, cpw, g).transpose(1, 0, 2, 3)

    @functools.partial(
        pl.kernel, mesh=_sc_mesh(),
        out_type=jax.ShapeDtypeStruct((n_out,) + rows.shape[1:], rows.dtype),
        scratch_types=[pltpu.VMEM((n_k, cpw, g), jnp.int32),
                       pltpu.VMEM((g,) + rows.shape[1:], rows.dtype),
                       pltpu.SemaphoreType.DMA],
    )
    def scatter(rows_hbm, dest_hbm, out_hbm, idx_v, rows_v, sem):
        wid = _sc_worker()
        pltpu.sync_copy(dest_hbm.at[wid], idx_v)

        @pl.loop(0, cpw)
        def _(cc):
            r0 = pl.multiple_of((wid * cpw + cc) * g, g)
            pltpu.sync_copy(rows_hbm.at[pl.ds(r0, g)], rows_v)
            copies = [pltpu.async_copy(rows_v, out_hbm.at[idx_v.at[k, cc]], sem) for k in range(n_k)]
            for cp in copies:
                cp.wait()

    return scatter(rows, dest_w)


def _sc_gather_rows(table, idx):
    m = idx.shape[0]
    g = SC_GROUP
    n_w = SC_CORES * SC_SUBCORES
    assert m % (n_w * g) == 0
    per_w = m // n_w

    @functools.partial(
        pl.kernel, mesh=_sc_mesh(),
        out_type=jax.ShapeDtypeStruct((m,) + table.shape[1:], table.dtype),
        scratch_types=[pltpu.VMEM((per_w,), jnp.int32),
                       pltpu.VMEM((g,) + table.shape[1:], table.dtype),
                       pltpu.SemaphoreType.DMA],
    )
    def gather(table_hbm, idx_hbm, out_hbm, idx_v, rows_v, sem):
        base = _sc_worker() * per_w
        pltpu.sync_copy(idx_hbm.at[pl.ds(base, per_w)], idx_v)

        @pl.loop(0, per_w // g)
        def _(cc):
            off = pl.multiple_of(cc * g, g)
            pltpu.async_copy(table_hbm.at[idx_v.at[pl.ds(off, g)]], rows_v, sem).wait()
            pltpu.sync_copy(rows_v, out_hbm.at[pl.ds(base + off, g)])

    return gather(table, idx)


def _ffn_kernel(be_ref, br_ref, bv_ref, x_ref, wgu_ref, bgu_ref, wd_ref, bd_ref, o_ref, wgu_s, wd_s,
                *, bm, d_ff):
    i = pl.program_id(0)

    @pl.when((i == 0) | (be_ref[i] != be_ref[jnp.maximum(i - 1, 0)]))
    def _():
        wgu_s[...] = wgu_ref[0].astype(BF16)
        wd_s[...] = wd_ref[0].astype(BF16)

    @pl.when(bv_ref[i] == 1)
    def _():
        x = jnp.concatenate([x_ref[pl.ds(s, bm, stride=SUBLANES), :] for s in range(SUBLANES)],
                            axis=-1).astype(BF16)
        gu = jnp.dot(x, wgu_s[...], preferred_element_type=F32) + bgu_ref[0]
        gate = jnp.minimum(gu[:, :d_ff], SWIGLU_LIMIT)
        up = jnp.clip(gu[:, d_ff:], -SWIGLU_LIMIT, SWIGLU_LIMIT)
        act = gate * jax.nn.sigmoid(SWIGLU_ALPHA * gate) * (up + 1.0)
        y = jnp.dot(act.astype(BF16), wd_s[...], preferred_element_type=F32) + bd_ref[0]
        for s in range(SUBLANES):
            o_ref[pl.ds(s, bm, stride=SUBLANES), :] = y[:, s * LANES:(s + 1) * LANES]

    @pl.when(bv_ref[i] == 0)
    def _():
        o_ref[...] = jnp.zeros_like(o_ref)


def _ffn_call(blk_e, blk_row, blk_valid, xs, w_gu, b_gu, w_d, b_d, bm):
    n_e, d, f2 = w_gu.shape
    d_ff = f2 // 2
    nb = blk_e.shape[0]
    kern = functools.partial(_ffn_kernel, bm=bm, d_ff=d_ff)
    gs = pltpu.PrefetchScalarGridSpec(
        num_scalar_prefetch=3,
        grid=(nb,),
        in_specs=[pl.BlockSpec((bm * SUBLANES, LANES), lambda i, be, br, bv: (br[i], 0)),
                  pl.BlockSpec((1, d, f2), lambda i, be, br, bv: (be[i], 0, 0)),
                  pl.BlockSpec((1, 1, f2), lambda i, be, br, bv: (be[i], 0, 0)),
                  pl.BlockSpec((1, d_ff, d), lambda i, be, br, bv: (be[i], 0, 0)),
                  pl.BlockSpec((1, 1, d), lambda i, be, br, bv: (be[i], 0, 0))],
        out_specs=pl.BlockSpec((bm * SUBLANES, LANES), lambda i, be, br, bv: (i, 0)),
        scratch_shapes=[pltpu.VMEM((d, f2), BF16), pltpu.VMEM((d_ff, d), BF16)],
    )
    return pl.pallas_call(
        kern,
        out_shape=jax.ShapeDtypeStruct((nb * bm * SUBLANES, LANES), F32),
        grid_spec=gs,
        compiler_params=_params(("arbitrary",)),
        name="ffn",
    )(blk_e, blk_row, blk_valid, xs, w_gu, b_gu.reshape(n_e, 1, f2), w_d, b_d.reshape(n_e, 1, d))


def _combine_kernel(y0_ref, y1_ref, y2_ref, y3_ref, prow_ref, x1_ref, gf_ref, nw_ref, o_ref, *, tm):
    p = prow_ref[...]
    ys = (y0_ref, y1_ref, y2_ref, y3_ref)
    for s in range(SUBLANES):
        moe = None
        for k in range(TOP_K):
            piece = ys[k][pl.ds(s, tm, stride=SUBLANES), :] * p[:, k:k + 1]
            moe = piece if moe is None else moe + piece
        sl = slice(s * LANES, (s + 1) * LANES)
        o_ref[:, sl] = x1_ref[:, sl] + gf_ref[0][:, sl] * moe
    xo = o_ref[...]
    o_ref[...] = xo * lax.rsqrt(jnp.mean(xo * xo, axis=-1, keepdims=True) + EPS) * nw_ref[...]


def _combine_call(ytok, prow, x1, mod3, norm_final, seq, tm):
    t, d = x1.shape
    nt = t // tm
    tiles_per_seq = seq // tm
    kern = functools.partial(_combine_kernel, tm=tm)
    yspec = lambda k: pl.BlockSpec((tm * SUBLANES, LANES), lambda i: (k * nt + i, 0))
    return pl.pallas_call(
        kern,
        out_shape=jax.ShapeDtypeStruct((t, d), F32),
        grid=(nt,),
        in_specs=[yspec(0), yspec(1), yspec(2), yspec(3),
                  pl.BlockSpec((tm, LANES), lambda i: (i, 0)),
                  pl.BlockSpec((tm, d), lambda i: (i, 0)),
                  pl.BlockSpec((1, 1, d), lambda i: ((i // tiles_per_seq) * N_MOD + 5, 0, 0)),
                  pl.BlockSpec((1, d), lambda i: (0, 0))],
        out_specs=pl.BlockSpec((tm, d), lambda i: (i, 0)),
        compiler_params=_params(("arbitrary",)),
        name="combine",
    )(ytok, ytok, ytok, ytok, prow, x1, mod3, norm_final)


def _plan(seq):
    def fit(pref):
        tm = min(pref, seq)
        assert seq % tm == 0
        return tm
    return dict(tm_in=fit(1024), tm_merge=fit(512), tm_moe=fit(512),
                ret_chunk=fit(RET_CHUNK), ssm_chunk=fit(SSM_CHUNK))


def _layer(x2, mod3, bsz, seq, norm_mix, norm_ffn, w_in, conv_w, conv_b, dt_bias, a_log, d_skip, ssm_norm,
           w_ret_out, w_ssm_out, w_out, w_router, b_router, w_gate_up, b_gate_up, w_down, b_down,
           norm_final):
    t, d = x2.shape
    plan = _plan(seq)
    qk_w = RET_HEADS * RET_QK_DIM
    v_w = RET_HEADS * RET_V_DIM
    d_inner = w_ssm_out.shape[0]
    conv_dim = conv_w.shape[1]
    n_heads = d_inner // SSM_HEAD_DIM
    dt_off = 2 * qk_w + 2 * v_w + d_inner + conv_dim

    w_main = jnp.concatenate([w_in[:, :dt_off], w_in[:, dt_off + n_heads:]], axis=1).astype(BF16)
    w_dt = jnp.pad(w_in[:, dt_off:dt_off + n_heads], ((0, 0), (0, LANES - n_heads)))
    w_dt_hi = w_dt.astype(BF16)
    w_dt = jnp.concatenate([w_dt_hi, (w_dt - w_dt_hi.astype(F32)).astype(BF16)], axis=1)
    half = RET_QK_DIM // 2
    inv_freq = ROPE_BASE ** (-jnp.arange(half, dtype=F32) / half)
    ang = jnp.arange(seq, dtype=F32)[:, None] * inv_freq[None, :]
    cos, sin = jnp.cos(ang), jnp.sin(ang)

    w_ret_b, w_ssm_b, w_out_b, w_router_t = (w_ret_out.astype(BF16), w_ssm_out.astype(BF16),
                                             w_out.astype(BF16), w_router.T)
    bm = FFN_BLOCK
    slab = (SUBLANES, LANES)

    def mixer(xg, modg, bg):
        tg = xg.shape[0]
        proj, dt_raw = _inproj_call(xg, norm_mix.reshape(1, d), modg, cos, sin, w_main, w_dt, conv_w, conv_b,
                                    2 * qk_w + 2 * v_w + d_inner, seq, plan["tm_in"], 2 * qk_w)
        ya = _retention_call(proj, w_ret_b, bg, seq, plan["ret_chunk"])
        yb = _ssd_call(proj, dt_raw, dt_bias, a_log, d_skip, ssm_norm, w_ssm_b, bg, seq, plan["ssm_chunk"])
        x1, h2, idx, rank, prow, cnt = _merge_call(ya, yb, proj, xg, modg, norm_ffn.reshape(1, d),
                                                   w_out_b, w_router_t, b_router, seq, plan["tm_merge"])
        counts = cnt[:, 0]
        padded = ((counts + bm - 1) // bm) * bm
        pad_end = jnp.cumsum(padded)
        start_pad = pad_end - padded
        n_blocks = (tg * TOP_K) // bm + N_EXPERTS
        e_ids = jnp.arange(N_EXPERTS, dtype=jnp.int32)[:, None, None, None]
        dest = rank + jnp.sum(jnp.where(idx[None] == e_ids, start_pad[:, None, None, None], 0), axis=0)
        dest = dest[:, :TOP_K, :].transpose(1, 0, 2).reshape(TOP_K, tg).astype(jnp.int32)
        n_real = pad_end[-1] // bm
        blk_valid = (jnp.arange(n_blocks) < n_real).astype(jnp.int32)
        blk_row = jnp.minimum(jnp.arange(n_blocks), n_real - 1).astype(jnp.int32)
        blk_e = jnp.minimum(jnp.sum(pad_end[None, :] <= (blk_row * bm)[:, None], axis=1),
                            N_EXPERTS - 1).astype(jnp.int32)
        xs = _sc_scatter_rows(h2.reshape((tg,) + slab), dest, n_blocks * bm)
        return dict(x1=x1, prow=prow, dest=dest, blocks=(blk_e, blk_row, blk_valid), xs=xs, mod=modg)

    def experts(m):
        n_rows = m["xs"].shape[0]
        ys = _ffn_call(*m["blocks"], m["xs"].reshape(n_rows * SUBLANES, LANES),
                       w_gate_up, b_gate_up, w_down, b_down, bm)
        return _sc_gather_rows(ys.reshape((n_rows,) + slab), m["dest"].reshape(-1))

    n_groups = N_GROUPS if bsz % N_GROUPS == 0 else 1
    bg = bsz // n_groups
    tg = bg * seq
    groups = [mixer(x2[g * tg:(g + 1) * tg], mod3[g * bg * N_MOD:(g + 1) * bg * N_MOD], bg)
              for g in range(n_groups)]
    ytoks = [experts(m) for m in groups]
    outs = [_combine_call(y.reshape(TOP_K * tg * SUBLANES, LANES), m["prow"], m["x1"], m["mod"],
                          norm_final.reshape(1, d), seq, plan["tm_moe"])
            for y, m in zip(ytoks, groups)]
    return jnp.concatenate(outs, axis=0) if n_groups > 1 else outs[0]


def kernel(x, c, w_ada, b_ada, norm_mix, norm_ffn, w_in, conv_w, conv_b, dt_bias, a_log, d_skip, ssm_norm,
           w_ret_out, w_ssm_out, w_out, w_router, b_router, w_gate_up, b_gate_up, w_down, b_down, norm_final):
    bsz, seq, d = x.shape
    depth = w_ada.shape[0]
    assert depth == 1, "the final norm is fused into the single layer's last kernel"
    x2 = x.reshape(bsz * seq, d)
    l = 0
    mod = _mod_call(c, w_ada[l], b_ada[l])
    mod3 = mod.reshape(bsz * N_MOD, 1, d)
    out = _layer(x2, mod3, bsz, seq, norm_mix[l], norm_ffn[l], w_in[l], conv_w[l], conv_b[l], dt_bias[l],
                 a_log[l], d_skip[l], ssm_norm[l], w_ret_out[l], w_ssm_out[l], w_out[l], w_router[l],
                 b_router[l], w_gate_up[l], b_gate_up[l], w_down[l], b_down[l], norm_final)
    return out.reshape(bsz, seq, d)
```

```python
import functools
import math

import numpy as np
import jax
import jax.numpy as jnp
from jax import lax
from jax.experimental import pallas as pl
from jax.experimental.pallas import tpu as pltpu
from jax.experimental.pallas import tpu_sc as plsc

F32 = jnp.float32
BF16 = jnp.bfloat16
HIGHEST = lax.Precision.HIGHEST

EPS = 1e-6
N_MOD = 6
RET_HEADS = 4
RET_QK_DIM = 256
RET_V_DIM = 512
ROPE_BASE = 10000.0
SSM_HEAD_DIM = 64
SSM_GROUPS = 8
SSM_STATE = 128
SSM_CONV = 4
N_EXPERTS = 32
TOP_K = 4
SWIGLU_LIMIT = 7.0
SWIGLU_ALPHA = 1.702

LANES = 128
SUBLANES = 8
VMEM_LIMIT = 56 * 1024 * 1024

RET_CHUNK = 256
SSM_CHUNK = 128
FFN_BLOCK = 512
SC_CORES = 2
SC_SUBCORES = 16
SC_GROUP = 32
N_GROUPS = 2


def _params(sem, vmem=VMEM_LIMIT):
    return pltpu.CompilerParams(dimension_semantics=sem, vmem_limit_bytes=vmem)


def _nt_dot(a, b, **kw):
    return lax.dot_general(a, b, (((1,), (1,)), ((), ())), preferred_element_type=F32, **kw)


def _tn_dot(a, b, **kw):
    return lax.dot_general(a, b, (((0,), (0,)), ((), ())), preferred_element_type=F32, **kw)


def _silu(v):
    return v * jax.nn.sigmoid(v)


def _mod_kernel(c_ref, w_ref, b_ref, o_ref):
    cond = _silu(c_ref[...])
    o_ref[...] = jnp.dot(cond, w_ref[...], preferred_element_type=F32, precision=HIGHEST) + b_ref[...]


def _mod_call(c, w_ada, b_ada):
    bsz, d = c.shape
    n = w_ada.shape[1]
    return pl.pallas_call(
        _mod_kernel,
        out_shape=jax.ShapeDtypeStruct((bsz, n), F32),
        grid=(n // d,),
        in_specs=[pl.BlockSpec((bsz, d), lambda j: (0, 0)),
                  pl.BlockSpec((d, d), lambda j: (0, j)),
                  pl.BlockSpec((1, d), lambda j: (0, j))],
        out_specs=pl.BlockSpec((bsz, d), lambda j: (0, j)),
        compiler_params=_params(("arbitrary",)),
        name="mod",
    )(c, w_ada, b_ada.reshape(1, n))


def _inproj_kernel(x_ref, nw_ref, sc_ref, sh_ref, cos_ref, sin_ref, w_ref, wdt_ref, cw_ref, cb_ref,
                   o_ref, dt_ref, h_s, work, carry, *, conv_j0, conv_nj, tiles_per_seq, tm, tn, sub):
    i = pl.program_id(0)
    j = pl.program_id(1)
    n_dt = dt_ref.shape[1]

    @pl.when(j == 0)
    def _():
        xf = x_ref[...]
        ms = jnp.mean(xf * xf, axis=-1, keepdims=True)
        y = xf * lax.rsqrt(ms + EPS) * nw_ref[...]
        hm = y * (1.0 + sc_ref[0]) + sh_ref[0]
        hb = hm.astype(BF16)
        h_s[...] = hb
        h_lo = (hm - hb.astype(F32)).astype(BF16)
        d_hi = jnp.dot(hb, wdt_ref[...], preferred_element_type=F32)
        d_lo = jnp.dot(h_lo, wdt_ref[:, :n_dt], preferred_element_type=F32)
        dt_ref[...] = d_hi[:, :n_dt] + d_hi[:, n_dt:] + d_lo
        cos = cos_ref[...]
        sin = sin_ref[...]
        half = RET_QK_DIM // 2
        for p in range(tn // sub):
            acc = jnp.dot(hb, w_ref[:, p * sub:(p + 1) * sub], preferred_element_type=F32)
            for cc in range(sub // RET_QK_DIM):
                c = p * (sub // RET_QK_DIM) + cc
                a = acc[:, cc * RET_QK_DIM: cc * RET_QK_DIM + half]
                b = acc[:, cc * RET_QK_DIM + half: (cc + 1) * RET_QK_DIM]
                scale = 1.0 if c < RET_HEADS else RET_QK_DIM ** -0.5
                o_ref[:, c * RET_QK_DIM: c * RET_QK_DIM + half] = ((a * cos - b * sin) * scale).astype(BF16)
                o_ref[:, c * RET_QK_DIM + half: (c + 1) * RET_QK_DIM] = ((a * sin + b * cos) * scale).astype(BF16)

    is_conv = (j >= conv_j0) & (j < conv_j0 + conv_nj)

    @pl.when(is_conv)
    def _():
        cj = j - conv_j0
        pad = SUBLANES

        @pl.when(i % tiles_per_seq == 0)
        def _():
            carry[cj] = jnp.zeros(carry.shape[1:], F32)

        for p in range(tn // sub):
            acc = jnp.dot(h_s[...], w_ref[:, p * sub:(p + 1) * sub], preferred_element_type=F32)
            for cc in range(sub // LANES):
                c = p * (sub // LANES) + cc
                cols = slice(c * LANES, (c + 1) * LANES)
                work[c, 0:pad, :] = carry[cj, c]
                work[c, pad:pad + tm, :] = acc[:, cc * LANES:(cc + 1) * LANES]
                conv = cb_ref[:, cols] + cw_ref[SSM_CONV - 1:SSM_CONV, cols] * work[c, pad:pad + tm, :]
                for k in range(SSM_CONV - 1):
                    shift = SSM_CONV - 1 - k
                    conv = conv + cw_ref[k:k + 1, cols] * work[c, pad - shift:pad - shift + tm, :]
                carry[cj, c] = work[c, tm:tm + pad, :]
                o_ref[:, cols] = _silu(conv).astype(BF16)

    @pl.when((j != 0) & jnp.logical_not(is_conv))
    def _():
        for p in range(tn // sub):
            o_ref[:, p * sub:(p + 1) * sub] = jnp.dot(
                h_s[...], w_ref[:, p * sub:(p + 1) * sub], preferred_element_type=F32).astype(BF16)


def _inproj_call(x2, row0, t, norm_w, mod3, cos, sin, w_main, w_dt, conv_w, conv_b, conv_off, seq, tm, tn):
    d = x2.shape[1]
    n = w_main.shape[1]
    conv_dim = conv_w.shape[1]
    tiles_per_seq = seq // tm
    off = row0 // tm
    assert tn == 2 * RET_HEADS * RET_QK_DIM, "rotary epilogue expects q and k in the first column tile"
    assert conv_off % tn == 0 and conv_dim % tn == 0
    conv_j0, conv_nj = conv_off // tn, conv_dim // tn
    sub = 512
    kern = functools.partial(_inproj_kernel, conv_j0=conv_j0, conv_nj=conv_nj,
                             tiles_per_seq=tiles_per_seq, tm=tm, tn=tn, sub=sub)
    conv_idx = lambda i, j: (0, jnp.clip(j - conv_j0, 0, conv_nj - 1))
    return pl.pallas_call(
        kern,
        out_shape=(jax.ShapeDtypeStruct((t, n), BF16), jax.ShapeDtypeStruct((t, LANES), F32)),
        grid=(t // tm, n // tn),
        in_specs=[
            pl.BlockSpec((tm, d), lambda i, j: (i + off, 0)),
            pl.BlockSpec((1, d), lambda i, j: (0, 0)),
            pl.BlockSpec((1, 1, d), lambda i, j: (((i + off) // tiles_per_seq) * N_MOD + 1, 0, 0)),
            pl.BlockSpec((1, 1, d), lambda i, j: (((i + off) // tiles_per_seq) * N_MOD + 0, 0, 0)),
            pl.BlockSpec((tm, LANES), lambda i, j: (i % tiles_per_seq, 0)),
            pl.BlockSpec((tm, LANES), lambda i, j: (i % tiles_per_seq, 0)),
            pl.BlockSpec((d, tn), lambda i, j: (0, j)),
            pl.BlockSpec((d, 2 * LANES), lambda i, j: (0, 0)),
            pl.BlockSpec((SSM_CONV, tn), conv_idx),
            pl.BlockSpec((1, tn), conv_idx),
        ],
        out_specs=(pl.BlockSpec((tm, tn), lambda i, j: (i, j)),
                   pl.BlockSpec((tm, LANES), lambda i, j: (i, 0))),
        scratch_shapes=[pltpu.VMEM((tm, d), BF16),
                        pltpu.VMEM((tn // LANES, tm + SUBLANES, LANES), F32),
                        pltpu.VMEM((conv_nj, tn // LANES, SUBLANES, LANES), F32)],
        compiler_params=_params(("arbitrary", "arbitrary")),
        name="inproj",
    )(x2, norm_w, mod3, mod3, cos, sin, w_main, w_dt, conv_w, conv_b.reshape(1, conv_dim))


def _retention_kernel(q_ref, k_ref, v_ref, g_ref, din_ref, dq_ref, dk_ref, w_ref, o_ref, state,
                      *, decay_c):
    c = pl.program_id(1)

    @pl.when(c == 0)
    def _():
        state[...] = jnp.zeros_like(state)

    acc = None
    for h in range(RET_HEADS):
        qh = q_ref[:, h * RET_QK_DIM:(h + 1) * RET_QK_DIM]
        kh = k_ref[:, h * RET_QK_DIM:(h + 1) * RET_QK_DIM]
        vh = v_ref[:, h * RET_V_DIM:(h + 1) * RET_V_DIM]
        scores = _nt_dot(qh, kh) * din_ref[h]
        inner = jnp.dot(scores.astype(BF16), vh, preferred_element_type=F32)
        st = state[h]
        cross = jnp.dot(qh, st.astype(BF16), preferred_element_type=F32) * dq_ref[h]
        kd = (kh.astype(F32) * dk_ref[h]).astype(BF16)
        state[h] = st * decay_c[h] + _tn_dot(kd, vh)
        ret = inner + cross
        ret = ret * lax.rsqrt(jnp.mean(ret * ret, axis=-1, keepdims=True) + EPS)
        gh = g_ref[:, h * RET_V_DIM:(h + 1) * RET_V_DIM].astype(F32)
        ret = ret * _silu(gh)
        part = jnp.dot(ret.astype(BF16), w_ref[h * RET_V_DIM:(h + 1) * RET_V_DIM, :],
                       preferred_element_type=F32)
        acc = part if acc is None else acc + part
    o_ref[...] = acc


def _retention_tables(chunk):
    lg = np.log(1.0 - 2.0 ** (-5.0 - np.arange(RET_HEADS, dtype=np.float64)))
    idx = np.arange(chunk, dtype=np.float64)
    rel = idx[:, None] - idx[None, :]
    causal = rel >= 0
    din = np.where(causal[None], np.exp(np.where(causal, rel, 0.0)[None] * lg[:, None, None]), 0.0)
    dq = np.exp((idx + 1.0)[None, :, None] * lg[:, None, None])
    dk = np.exp((chunk - 1.0 - idx)[None, :, None] * lg[:, None, None])
    dc = tuple(float(v) for v in np.exp(chunk * lg))
    return (jnp.asarray(din, F32), jnp.asarray(dq, F32), jnp.asarray(dk, F32), dc)


def _retention_call(proj, w_ret, bsz, seq, chunk):
    t = proj.shape[0]
    d = w_ret.shape[1]
    nc = seq // chunk
    qk_w = RET_HEADS * RET_QK_DIM
    v_w = RET_HEADS * RET_V_DIM
    din, dq, dk, dc = _retention_tables(chunk)
    kern = functools.partial(_retention_kernel, decay_c=dc)
    row = lambda b, c: b * nc + c
    return pl.pallas_call(
        kern,
        out_shape=jax.ShapeDtypeStruct((t, d), F32),
        grid=(bsz, nc),
        in_specs=[
            pl.BlockSpec((chunk, qk_w), lambda b, c: (row(b, c), 0)),
            pl.BlockSpec((chunk, qk_w), lambda b, c: (row(b, c), 1)),
            pl.BlockSpec((chunk, v_w), lambda b, c: (row(b, c), 1)),
            pl.BlockSpec((chunk, v_w), lambda b, c: (row(b, c), 2)),
            pl.BlockSpec((RET_HEADS, chunk, chunk), lambda b, c: (0, 0, 0)),
            pl.BlockSpec((RET_HEADS, chunk, 1), lambda b, c: (0, 0, 0)),
            pl.BlockSpec((RET_HEADS, chunk, 1), lambda b, c: (0, 0, 0)),
            pl.BlockSpec((v_w, d), lambda b, c: (0, 0)),
        ],
        out_specs=pl.BlockSpec((chunk, d), lambda b, c: (row(b, c), 0)),
        scratch_shapes=[pltpu.VMEM((RET_HEADS, RET_QK_DIM, RET_V_DIM), F32)],
        compiler_params=_params(("arbitrary", "arbitrary")),
        name="retention",
    )(proj, proj, proj, proj, din, dq, dk, w_ret)


def _ssd_kernel(z_ref, xbc_ref, dt_ref, dtb_ref, alog_ref, dsk_ref, nw_ref,
                tril_ref, exp_ref, w_ref, o_ref, state, yn_s, *, chunk, d_inner):
    c = pl.program_id(1)
    heads_per_group = d_inner // SSM_HEAD_DIM // SSM_GROUPS
    gw = heads_per_group * SSM_HEAD_DIM
    assert SSM_HEAD_DIM * 2 == LANES and gw == 2 * LANES

    @pl.when(c == 0)
    def _():
        state[...] = jnp.zeros_like(state)

    dt = jax.nn.softplus(dt_ref[...] + dtb_ref[...])
    a = -jnp.exp(alog_ref[...])
    adt = dt * a
    acs = jnp.dot(tril_ref[...], adt, preferred_element_type=F32, precision=HIGHEST)
    acs_t = acs.T
    dt_x = jnp.dot(dt.astype(BF16), exp_ref[...], preferred_element_type=F32)
    li = lax.broadcasted_iota(jnp.int32, (chunk, chunk), 0)
    si = lax.broadcasted_iota(jnp.int32, (chunk, chunk), 1)
    causal = li >= si
    low_half = si < SSM_HEAD_DIM
    lane_g = lax.broadcasted_iota(jnp.int32, (chunk, gw), 1)

    b_off = d_inner
    c_off = d_inner + SSM_GROUPS * SSM_STATE
    for g in range(SSM_GROUPS):
        bm = xbc_ref[:, b_off + g * SSM_STATE: b_off + (g + 1) * SSM_STATE]
        cm = xbc_ref[:, c_off + g * SSM_STATE: c_off + (g + 1) * SSM_STATE]
        xs_g = xbc_ref[:, g * gw:(g + 1) * gw].astype(F32)
        xdt_g = xs_g * dt_x[:, g * gw:(g + 1) * gw]
        cb = _nt_dot(cm, bm)
        cols, ms, xm = [], [], []
        for jh in range(heads_per_group):
            h = g * heads_per_group + jh
            col = jnp.broadcast_to(acs[:, h:h + 1], (chunk, chunk))
            seg = jnp.exp(jnp.where(causal, col - acs_t[h:h + 1, :], -jnp.inf))
            cols.append(col)
            ms.append((cb * seg).astype(BF16))
            in_head = (lane_g >= jh * SSM_HEAD_DIM) & (lane_g < (jh + 1) * SSM_HEAD_DIM)
            xm.append(jnp.where(in_head, xdt_g, 0.0).astype(BF16))
        y_diag = jnp.dot(jnp.concatenate(ms, axis=-1), jnp.concatenate(xm, axis=0),
                         preferred_element_type=F32)
        a_x = jnp.concatenate([jnp.where(low_half, cols[0], cols[1]),
                               jnp.where(low_half, cols[2], cols[3])], axis=-1)
        e_acs_x = jnp.exp(a_x)
        a_last_x = a_x[chunk - 1:chunk, :]
        st = state[g]
        y_off = jnp.dot(cm, st.astype(BF16), preferred_element_type=F32) * e_acs_x
        xdec = (xdt_g * jnp.exp(a_last_x - a_x)).astype(BF16)
        state[g] = st * e_acs_x[chunk - 1:chunk, :] + _tn_dot(bm, xdec)
        y = y_diag + y_off + dsk_ref[:, g * gw:(g + 1) * gw] * xs_g
        yz = y * _silu(z_ref[:, g * gw:(g + 1) * gw].astype(F32))
        yn = yz * lax.rsqrt(jnp.mean(yz * yz, axis=-1, keepdims=True) + EPS) * nw_ref[:, g * gw:(g + 1) * gw]
        yn_s[:, g * gw:(g + 1) * gw] = yn.astype(BF16)
    o_ref[...] = jnp.dot(yn_s[...], w_ref[...], preferred_element_type=F32)


def _ssd_call(proj, dt_raw, dt_bias, a_log, d_skip, ssm_norm, w_ssm, bsz, seq, chunk):
    t = proj.shape[0]
    d_inner, d = w_ssm.shape
    conv_dim = d_inner + 2 * SSM_GROUPS * SSM_STATE
    n_heads = d_inner // SSM_HEAD_DIM
    nc = seq // chunk
    gw = d_inner // SSM_GROUPS
    pad_h = lambda v: jnp.pad(v.astype(F32), (0, LANES - n_heads)).reshape(1, LANES)
    assert chunk == LANES, "the per-head decay tiles are built lane-for-lane against the chunk"
    tril = jnp.asarray(np.tril(np.ones((chunk, chunk), np.float32)))
    expand = np.zeros((LANES, d_inner), np.float32)
    for h in range(n_heads):
        expand[h, h * SSM_HEAD_DIM:(h + 1) * SSM_HEAD_DIM] = 1.0
    expand = jnp.asarray(expand, BF16)
    kern = functools.partial(_ssd_kernel, chunk=chunk, d_inner=d_inner)
    row = lambda b, c: b * nc + c
    z_blk = (2 * RET_HEADS * RET_QK_DIM + 2 * RET_HEADS * RET_V_DIM) // d_inner
    xbc_blk = (2 * RET_HEADS * RET_QK_DIM + 2 * RET_HEADS * RET_V_DIM + d_inner) // conv_dim
    full = lambda shape: pl.BlockSpec(shape, lambda b, c: (0,) * len(shape))
    return pl.pallas_call(
        kern,
        out_shape=jax.ShapeDtypeStruct((t, d), F32),
        grid=(bsz, nc),
        in_specs=[
            pl.BlockSpec((chunk, d_inner), lambda b, c: (row(b, c), z_blk)),
            pl.BlockSpec((chunk, conv_dim), lambda b, c: (row(b, c), xbc_blk)),
            pl.BlockSpec((chunk, LANES), lambda b, c: (row(b, c), 0)),
            full((1, LANES)), full((1, LANES)),
            full((1, d_inner)), full((1, d_inner)), full((chunk, chunk)), full((LANES, d_inner)),
            full((d_inner, d)),
        ],
        out_specs=pl.BlockSpec((chunk, d), lambda b, c: (row(b, c), 0)),
        scratch_shapes=[pltpu.VMEM((SSM_GROUPS, SSM_STATE, gw), F32),
                        pltpu.VMEM((chunk, d_inner), BF16)],
        compiler_params=_params(("arbitrary", "arbitrary")),
        name="ssd",
    )(proj, proj, dt_raw, pad_h(dt_bias), pad_h(a_log),
      jnp.repeat(d_skip.astype(F32), SSM_HEAD_DIM).reshape(1, d_inner), ssm_norm.reshape(1, d_inner),
      tril, expand, w_ssm)


def _merge_kernel(ya_ref, yb_ref, ga_ref, gb_ref, x_ref, gm_ref, scf_ref, shf_ref, nw_ref, wo_ref,
                  wr_ref, br_ref, tri_ref,
                  x1_ref, h2_ref, idx_ref, rank_ref, prow_ref, cnt_ref, cnt_s, *, tm):
    i = pl.program_id(0)

    @pl.when(i == 0)
    def _():
        cnt_s[...] = jnp.zeros_like(cnt_s)

    merged = (jax.nn.sigmoid(ga_ref[...].astype(F32)) * ya_ref[...]
              + jax.nn.sigmoid(gb_ref[...].astype(F32)) * yb_ref[...])
    mo = jnp.dot(merged.astype(BF16), wo_ref[...], preferred_element_type=F32)
    x1 = x_ref[...] + gm_ref[0] * mo
    x1_ref[...] = x1
    ms = jnp.mean(x1 * x1, axis=-1, keepdims=True)
    h2 = x1 * lax.rsqrt(ms + EPS) * nw_ref[...] * (1.0 + scf_ref[0]) + shf_ref[0]
    for s in range(SUBLANES):
        h2_ref[pl.ds(s, tm, stride=SUBLANES), :] = h2[:, s * LANES:(s + 1) * LANES]

    lg = _nt_dot(wr_ref[...], h2, precision=HIGHEST) + br_ref[...]
    sub = lax.broadcasted_iota(jnp.int32, lg.shape, 0)
    work = lg
    vals, idxs, sels = [], [], []
    for _ in range(TOP_K):
        m = jnp.max(work, axis=0, keepdims=True)
        ik = jnp.min(jnp.where(work == m, sub, N_EXPERTS), axis=0, keepdims=True)
        sel = sub == ik
        vals.append(m)
        idxs.append(ik)
        sels.append(sel)
        work = jnp.where(sel, -jnp.inf, work)
    exps = [jnp.exp(v - vals[0]) for v in vals]
    denom = exps[0]
    for e in exps[1:]:
        denom = denom + e
    probs = [e / denom for e in exps]

    base = cnt_s[:, 0:1]
    ranks = []
    for k in range(TOP_K):
        mk = jnp.where(sels[k], 1.0, 0.0)
        pre = jnp.dot(mk.astype(BF16), tri_ref[...], preferred_element_type=F32)
        ranks.append(jnp.sum(jnp.where(sels[k], pre + base, 0.0), axis=0, keepdims=True))
        base = base + jnp.sum(mk, axis=1, keepdims=True)
    cnt_s[...] = jnp.broadcast_to(base, cnt_s.shape)
    cnt_ref[...] = cnt_s[...].astype(jnp.int32)

    zi = jnp.zeros((SUBLANES - TOP_K, tm), jnp.int32)
    idx_ref[0] = jnp.concatenate(idxs + [zi], axis=0)
    rank_ref[0] = jnp.concatenate([r.astype(jnp.int32) for r in ranks] + [zi], axis=0)
    pt = jnp.concatenate(probs + [jnp.zeros((LANES - TOP_K, tm), F32)], axis=0)
    prow_ref[...] = pt.T


def _merge_call(ya, yb, proj, x2, row0, mod3, norm_w, w_out, w_router_t, b_router, seq, tm):
    t, d = ya.shape
    nt = t // tm
    tiles_per_seq = seq // tm
    off = row0 // tm
    ga_blk = proj.shape[1] // d - 2
    tri = jnp.asarray(np.triu(np.ones((tm, tm), np.float32), 1), BF16)
    kern = functools.partial(_merge_kernel, tm=tm)
    modspec = lambda m: pl.BlockSpec((1, 1, d), lambda i: (((i + off) // tiles_per_seq) * N_MOD + m, 0, 0))
    return pl.pallas_call(
        kern,
        out_shape=(jax.ShapeDtypeStruct((t, d), F32), jax.ShapeDtypeStruct((t * SUBLANES, LANES), F32),
                   jax.ShapeDtypeStruct((nt, SUBLANES, tm), jnp.int32),
                   jax.ShapeDtypeStruct((nt, SUBLANES, tm), jnp.int32),
                   jax.ShapeDtypeStruct((t, LANES), F32),
                   jax.ShapeDtypeStruct((N_EXPERTS, LANES), jnp.int32)),
        grid=(nt,),
        in_specs=[
            pl.BlockSpec((tm, d), lambda i: (i, 0)),
            pl.BlockSpec((tm, d), lambda i: (i, 0)),
            pl.BlockSpec((tm, d), lambda i: (i, ga_blk)),
            pl.BlockSpec((tm, d), lambda i: (i, ga_blk + 1)),
            pl.BlockSpec((tm, d), lambda i: (i + off, 0)),
            modspec(2), modspec(4), modspec(3),
            pl.BlockSpec((1, d), lambda i: (0, 0)),
            pl.BlockSpec((d, d), lambda i: (0, 0)),
            pl.BlockSpec((N_EXPERTS, d), lambda i: (0, 0)),
            pl.BlockSpec((N_EXPERTS, 1), lambda i: (0, 0)),
            pl.BlockSpec((tm, tm), lambda i: (0, 0)),
        ],
        out_specs=(pl.BlockSpec((tm, d), lambda i: (i, 0)),
                   pl.BlockSpec((tm * SUBLANES, LANES), lambda i: (i, 0)),
                   pl.BlockSpec((1, SUBLANES, tm), lambda i: (i, 0, 0)),
                   pl.BlockSpec((1, SUBLANES, tm), lambda i: (i, 0, 0)),
                   pl.BlockSpec((tm, LANES), lambda i: (i, 0)),
                   pl.BlockSpec((N_EXPERTS, LANES), lambda i: (0, 0))),
        scratch_shapes=[pltpu.VMEM((N_EXPERTS, LANES), F32)],
        compiler_params=_params(("arbitrary",)),
        name="merge",
    )(ya, yb, proj, proj, x2, mod3, mod3, mod3, norm_w, w_out, w_router_t,
      b_router.reshape(N_EXPERTS, 1), tri)


def _sc_mesh():
    return plsc.VectorSubcoreMesh(core_axis_name="c", subcore_axis_name="s")


def _sc_worker():
    return lax.axis_index("s") * SC_CORES + lax.axis_index("c")


def _sc_scatter_rows(rows, dest, n_out):
    t = rows.shape[0]
    n_k = dest.shape[0]
    g = SC_GROUP
    n_w = SC_CORES * SC_SUBCORES
    assert t % (n_w * g) == 0
    cpw = t // (n_w * g)
    dest_w = dest.reshape(n_k, n_w, cpw, g).transpose(1, 0, 2, 3)

    @functools.partial(
        pl.kernel, mesh=_sc_mesh(),
        out_type=jax.ShapeDtypeStruct((n_out,) + rows.shape[1:], rows.dtype),
        scratch_types=[pltpu.VMEM((n_k, cpw, g), jnp.int32),
                       pltpu.VMEM((g,) + rows.shape[1:], rows.dtype),
                       pltpu.SemaphoreType.DMA],
    )
    def scatter(rows_hbm, dest_hbm, out_hbm, idx_v, rows_v, sem):
        wid = _sc_worker()
        pltpu.sync_copy(dest_hbm.at[wid], idx_v)

        @pl.loop(0, cpw)
        def _(cc):
            r0 = pl.multiple_of((wid * cpw + cc) * g, g)
            pltpu.sync_copy(rows_hbm.at[pl.ds(r0, g)], rows_v)
            copies = [pltpu.async_copy(rows_v, out_hbm.at[idx_v.at[k, cc]], sem) for k in range(n_k)]
            for cp in copies:
                cp.wait()

    return scatter(rows, dest_w)


def _sc_gather_rows(table, idx):
    m = idx.shape[0]
    g = SC_GROUP
    n_w = SC_CORES * SC_SUBCORES
    assert m % (n_w * g) == 0
    per_w = m // n_w

    @functools.partial(
        pl.kernel, mesh=_sc_mesh(),
        out_type=jax.ShapeDtypeStruct((m,) + table.shape[1:], table.dtype),
        scratch_types=[pltpu.VMEM((per_w,), jnp.int32),
                       pltpu.VMEM((g,) + table.shape[1:], table.dtype),
                       pltpu.SemaphoreType.DMA],
    )
    def gather(table_hbm, idx_hbm, out_hbm, idx_v, rows_v, sem):
        base = _sc_worker() * per_w
        pltpu.sync_copy(idx_hbm.at[pl.ds(base, per_w)], idx_v)

        @pl.loop(0, per_w // g)
        def _(cc):
            off = pl.multiple_of(cc * g, g)
            pltpu.async_copy(table_hbm.at[idx_v.at[pl.ds(off, g)]], rows_v, sem).wait()
            pltpu.sync_copy(rows_v, out_hbm.at[pl.ds(base + off, g)])

    return gather(table, idx)


def _ffn_kernel(be_ref, br_ref, bv_ref, x_ref, wgu_ref, bgu_ref, wd_ref, bd_ref, o_ref, wgu_s, wd_s,
                *, bm, d_ff):
    i = pl.program_id(0)

    @pl.when((i == 0) | (be_ref[i] != be_ref[jnp.maximum(i - 1, 0)]))
    def _():
        wgu_s[...] = wgu_ref[0].astype(BF16)
        wd_s[...] = wd_ref[0].astype(BF16)

    @pl.when(bv_ref[i] == 1)
    def _():
        x = jnp.concatenate([x_ref[pl.ds(s, bm, stride=SUBLANES), :] for s in range(SUBLANES)],
                            axis=-1).astype(BF16)
        gu = jnp.dot(x, wgu_s[...], preferred_element_type=F32) + bgu_ref[0]
        gate = jnp.minimum(gu[:, :d_ff], SWIGLU_LIMIT)
        up = jnp.clip(gu[:, d_ff:], -SWIGLU_LIMIT, SWIGLU_LIMIT)
        act = gate * jax.nn.sigmoid(SWIGLU_ALPHA * gate) * (up + 1.0)
        y = jnp.dot(act.astype(BF16), wd_s[...], preferred_element_type=F32) + bd_ref[0]
        for s in range(SUBLANES):
            o_ref[pl.ds(s, bm, stride=SUBLANES), :] = y[:, s * LANES:(s + 1) * LANES]

    @pl.when(bv_ref[i] == 0)
    def _():
        o_ref[...] = jnp.zeros_like(o_ref)


def _ffn_call(blk_e, blk_row, blk_valid, xs, w_gu, b_gu, w_d, b_d, bm):
    n_e, d, f2 = w_gu.shape
    d_ff = f2 // 2
    nb = blk_e.shape[0]
    kern = functools.partial(_ffn_kernel, bm=bm, d_ff=d_ff)
    gs = pltpu.PrefetchScalarGridSpec(
        num_scalar_prefetch=3,
        grid=(nb,),
        in_specs=[pl.BlockSpec((bm * SUBLANES, LANES), lambda i, be, br, bv: (br[i], 0)),
                  pl.BlockSpec((1, d, f2), lambda i, be, br, bv: (be[i], 0, 0)),
                  pl.BlockSpec((1, 1, f2), lambda i, be, br, bv: (be[i], 0, 0)),
                  pl.BlockSpec((1, d_ff, d), lambda i, be, br, bv: (be[i], 0, 0)),
                  pl.BlockSpec((1, 1, d), lambda i, be, br, bv: (be[i], 0, 0))],
        out_specs=pl.BlockSpec((bm * SUBLANES, LANES), lambda i, be, br, bv: (i, 0)),
        scratch_shapes=[pltpu.VMEM((d, f2), BF16), pltpu.VMEM((d_ff, d), BF16)],
    )
    return pl.pallas_call(
        kern,
        out_shape=jax.ShapeDtypeStruct((nb * bm * SUBLANES, LANES), F32),
        grid_spec=gs,
        compiler_params=_params(("arbitrary",)),
        name="ffn",
    )(blk_e, blk_row, blk_valid, xs, w_gu, b_gu.reshape(n_e, 1, f2), w_d, b_d.reshape(n_e, 1, d))


def _combine_kernel(y0_ref, y1_ref, y2_ref, y3_ref, prow_ref, x1_ref, gf_ref, nw_ref, *rest, tm):
    o_ref = rest[-1]
    p = prow_ref[...]
    ys = (y0_ref, y1_ref, y2_ref, y3_ref)
    for s in range(SUBLANES):
        moe = None
        for k in range(TOP_K):
            piece = ys[k][pl.ds(s, tm, stride=SUBLANES), :] * p[:, k:k + 1]
            moe = piece if moe is None else moe + piece
        sl = slice(s * LANES, (s + 1) * LANES)
        o_ref[:, sl] = x1_ref[:, sl] + gf_ref[0][:, sl] * moe
    xo = o_ref[...]
    o_ref[...] = xo * lax.rsqrt(jnp.mean(xo * xo, axis=-1, keepdims=True) + EPS) * nw_ref[...]


def _combine_call(ytok, prow, x1, mod3, norm_final, seq, tm, row0, t_total, out_prev):
    t, d = x1.shape
    nt = t // tm
    tiles_per_seq = seq // tm
    off = row0 // tm
    kern = functools.partial(_combine_kernel, tm=tm)
    yspec = lambda k: pl.BlockSpec((tm * SUBLANES, LANES), lambda i: (k * nt + i, 0))
    in_specs = [yspec(0), yspec(1), yspec(2), yspec(3),
                pl.BlockSpec((tm, LANES), lambda i: (i, 0)),
                pl.BlockSpec((tm, d), lambda i: (i, 0)),
                pl.BlockSpec((1, 1, d), lambda i: (((i + off) // tiles_per_seq) * N_MOD + 5, 0, 0)),
                pl.BlockSpec((1, d), lambda i: (0, 0))]
    args = [ytok, ytok, ytok, ytok, prow, x1, mod3, norm_final]
    aliases = {}
    if out_prev is not None:
        in_specs.append(pl.BlockSpec(memory_space=pl.ANY))
        aliases = {len(args): 0}
        args.append(out_prev)
    return pl.pallas_call(
        kern,
        out_shape=jax.ShapeDtypeStruct((t_total, d), F32),
        grid=(nt,),
        in_specs=in_specs,
        out_specs=pl.BlockSpec((tm, d), lambda i: (i + off, 0)),
        input_output_aliases=aliases,
        compiler_params=_params(("arbitrary",)),
        name="combine",
    )(*args)


def _plan(seq):
    def fit(pref):
        tm = min(pref, seq)
        assert seq % tm == 0
        return tm
    return dict(tm_in=fit(1024), tm_merge=fit(512), tm_moe=fit(512),
                ret_chunk=fit(RET_CHUNK), ssm_chunk=fit(SSM_CHUNK))


def _layer(x2, mod3, bsz, seq, norm_mix, norm_ffn, w_in, conv_w, conv_b, dt_bias, a_log, d_skip, ssm_norm,
           w_ret_out, w_ssm_out, w_out, w_router, b_router, w_gate_up, b_gate_up, w_down, b_down,
           norm_final):
    t, d = x2.shape
    plan = _plan(seq)
    qk_w = RET_HEADS * RET_QK_DIM
    v_w = RET_HEADS * RET_V_DIM
    d_inner = w_ssm_out.shape[0]
    conv_dim = conv_w.shape[1]
    n_heads = d_inner // SSM_HEAD_DIM
    dt_off = 2 * qk_w + 2 * v_w + d_inner + conv_dim

    w_main = jnp.concatenate([w_in[:, :dt_off], w_in[:, dt_off + n_heads:]], axis=1).astype(BF16)
    w_dt = jnp.pad(w_in[:, dt_off:dt_off + n_heads], ((0, 0), (0, LANES - n_heads)))
    w_dt_hi = w_dt.astype(BF16)
    w_dt = jnp.concatenate([w_dt_hi, (w_dt - w_dt_hi.astype(F32)).astype(BF16)], axis=1)
    half = RET_QK_DIM // 2
    inv_freq = ROPE_BASE ** (-jnp.arange(half, dtype=F32) / half)
    ang = jnp.arange(seq, dtype=F32)[:, None] * inv_freq[None, :]
    cos, sin = jnp.cos(ang), jnp.sin(ang)

    w_ret_b, w_ssm_b, w_out_b, w_router_t = (w_ret_out.astype(BF16), w_ssm_out.astype(BF16),
                                             w_out.astype(BF16), w_router.T)
    bm = FFN_BLOCK
    slab = (SUBLANES, LANES)

    def mixer(row0, tg, bg):
        proj, dt_raw = _inproj_call(x2, row0, tg, norm_mix.reshape(1, d), mod3, cos, sin, w_main, w_dt,
                                    conv_w, conv_b, 2 * qk_w + 2 * v_w + d_inner, seq, plan["tm_in"], 2 * qk_w)
        ya = _retention_call(proj, w_ret_b, bg, seq, plan["ret_chunk"])
        yb = _ssd_call(proj, dt_raw, dt_bias, a_log, d_skip, ssm_norm, w_ssm_b, bg, seq, plan["ssm_chunk"])
        x1, h2, idx, rank, prow, cnt = _merge_call(ya, yb, proj, x2, row0, mod3, norm_ffn.reshape(1, d),
                                                   w_out_b, w_router_t, b_router, seq, plan["tm_merge"])
        counts = cnt[:, 0]
        padded = ((counts + bm - 1) // bm) * bm
        pad_end = jnp.cumsum(padded)
        start_pad = pad_end - padded
        n_blocks = (tg * TOP_K) // bm + N_EXPERTS
        e_ids = jnp.arange(N_EXPERTS, dtype=jnp.int32)[:, None, None, None]
        dest = rank + jnp.sum(jnp.where(idx[None] == e_ids, start_pad[:, None, None, None], 0), axis=0)
        dest = dest[:, :TOP_K, :].transpose(1, 0, 2).reshape(TOP_K, tg).astype(jnp.int32)
        n_real = pad_end[-1] // bm
        blk_valid = (jnp.arange(n_blocks) < n_real).astype(jnp.int32)
        blk_row = jnp.minimum(jnp.arange(n_blocks), n_real - 1).astype(jnp.int32)
        blk_e = jnp.minimum(jnp.sum(pad_end[None, :] <= (blk_row * bm)[:, None], axis=1),
                            N_EXPERTS - 1).astype(jnp.int32)
        xs = _sc_scatter_rows(h2.reshape((tg,) + slab), dest, n_blocks * bm)
        return dict(x1=x1, prow=prow, dest=dest, blocks=(blk_e, blk_row, blk_valid), xs=xs, row0=row0)

    def experts(m):
        n_rows = m["xs"].shape[0]
        ys = _ffn_call(*m["blocks"], m["xs"].reshape(n_rows * SUBLANES, LANES),
                       w_gate_up, b_gate_up, w_down, b_down, bm)
        return _sc_gather_rows(ys.reshape((n_rows,) + slab), m["dest"].reshape(-1))

    n_groups = N_GROUPS if bsz % N_GROUPS == 0 else 1
    bg = bsz // n_groups
    tg = bg * seq
    groups = [mixer(g * tg, tg, bg) for g in range(n_groups)]
    ytoks = [experts(m) for m in groups]
    out = None
    for y, m in zip(ytoks, groups):
        out = _combine_call(y.reshape(TOP_K * tg * SUBLANES, LANES), m["prow"], m["x1"], mod3,
                            norm_final.reshape(1, d), seq, plan["tm_moe"], m["row0"], t, out)
    return out


def kernel(x, c, w_ada, b_ada, norm_mix, norm_ffn, w_in, conv_w, conv_b, dt_bias, a_log, d_skip, ssm_norm,
           w_ret_out, w_ssm_out, w_out, w_router, b_router, w_gate_up, b_gate_up, w_down, b_down, norm_final):
    bsz, seq, d = x.shape
    depth = w_ada.shape[0]
    assert depth == 1, "the final norm is fused into the single layer's last kernel"
    x2 = x.reshape(bsz * seq, d)
    l = 0
    mod = _mod_call(c, w_ada[l], b_ada[l])
    mod3 = mod.reshape(bsz * N_MOD, 1, d)
    out = _layer(x2, mod3, bsz, seq, norm_mix[l], norm_ffn[l], w_in[l], conv_w[l], conv_b[l], dt_bias[l],
                 a_log[l], d_skip[l], ssm_norm[l], w_ret_out[l], w_ssm_out[l], w_out[l], w_router[l],
                 b_router[l], w_gate_up[l], b_gate_up[l], w_down[l], b_down[l], norm_final)
    return out.reshape(bsz, seq, d)
```

```python
import functools
import math

import numpy as np
import jax
import jax.numpy as jnp
from jax import lax
from jax.experimental import pallas as pl
from jax.experimental.pallas import tpu as pltpu
from jax.experimental.pallas import tpu_sc as plsc

F32 = jnp.float32
BF16 = jnp.bfloat16
HIGHEST = lax.Precision.HIGHEST

EPS = 1e-6
N_MOD = 6
RET_HEADS = 4
RET_QK_DIM = 256
RET_V_DIM = 512
ROPE_BASE = 10000.0
SSM_HEAD_DIM = 64
SSM_GROUPS = 8
SSM_STATE = 128
SSM_CONV = 4
N_EXPERTS = 32
TOP_K = 4
SWIGLU_LIMIT = 7.0
SWIGLU_ALPHA = 1.702

LANES = 128
SUBLANES = 8
VMEM_LIMIT = 56 * 1024 * 1024

RET_CHUNK = 256
SSM_CHUNK = 128
FFN_BLOCK = 512
FFN_PIECE = 256
SC_CORES = 2
SC_SUBCORES = 16
SC_GROUP = 32
N_GROUPS = 2


def _params(sem, vmem=VMEM_LIMIT):
    return pltpu.CompilerParams(dimension_semantics=sem, vmem_limit_bytes=vmem)


def _nt_dot(a, b, **kw):
    return lax.dot_general(a, b, (((1,), (1,)), ((), ())), preferred_element_type=F32, **kw)


def _tn_dot(a, b, **kw):
    return lax.dot_general(a, b, (((0,), (0,)), ((), ())), preferred_element_type=F32, **kw)


def _silu(v):
    return v * jax.nn.sigmoid(v)


def _mod_kernel(c_ref, w_ref, b_ref, o_ref):
    cond = _silu(c_ref[...])
    o_ref[...] = jnp.dot(cond, w_ref[...], preferred_element_type=F32, precision=HIGHEST) + b_ref[...]


def _mod_call(c, w_ada, b_ada):
    bsz, d = c.shape
    n = w_ada.shape[1]
    return pl.pallas_call(
        _mod_kernel,
        out_shape=jax.ShapeDtypeStruct((bsz, n), F32),
        grid=(n // d,),
        in_specs=[pl.BlockSpec((bsz, d), lambda j: (0, 0)),
                  pl.BlockSpec((d, d), lambda j: (0, j)),
                  pl.BlockSpec((1, d), lambda j: (0, j))],
        out_specs=pl.BlockSpec((bsz, d), lambda j: (0, j)),
        compiler_params=_params(("arbitrary",)),
        name="mod",
    )(c, w_ada, b_ada.reshape(1, n))


def _inproj_kernel(x_ref, nw_ref, sc_ref, sh_ref, cos_ref, sin_ref, w_ref, wdt_ref, cw_ref, cb_ref,
                   o_ref, dt_ref, h_s, work, carry, *, conv_j0, conv_nj, tiles_per_seq, tm, tn, sub):
    i = pl.program_id(0)
    j = pl.program_id(1)
    n_dt = dt_ref.shape[1]

    @pl.when(j == 0)
    def _():
        xf = x_ref[...]
        ms = jnp.mean(xf * xf, axis=-1, keepdims=True)
        y = xf * lax.rsqrt(ms + EPS) * nw_ref[...]
        hm = y * (1.0 + sc_ref[0]) + sh_ref[0]
        hb = hm.astype(BF16)
        h_s[...] = hb
        h_lo = (hm - hb.astype(F32)).astype(BF16)
        d_hi = jnp.dot(hb, wdt_ref[...], preferred_element_type=F32)
        d_lo = jnp.dot(h_lo, wdt_ref[:, :n_dt], preferred_element_type=F32)
        dt_ref[...] = d_hi[:, :n_dt] + d_hi[:, n_dt:] + d_lo
        cos = cos_ref[...]
        sin = sin_ref[...]
        half = RET_QK_DIM // 2
        for p in range(tn // sub):
            acc = jnp.dot(hb, w_ref[:, p * sub:(p + 1) * sub], preferred_element_type=F32)
            for cc in range(sub // RET_QK_DIM):
                c = p * (sub // RET_QK_DIM) + cc
                a = acc[:, cc * RET_QK_DIM: cc * RET_QK_DIM + half]
                b = acc[:, cc * RET_QK_DIM + half: (cc + 1) * RET_QK_DIM]
                scale = 1.0 if c < RET_HEADS else RET_QK_DIM ** -0.5
                o_ref[:, c * RET_QK_DIM: c * RET_QK_DIM + half] = ((a * cos - b * sin) * scale).astype(BF16)
                o_ref[:, c * RET_QK_DIM + half: (c + 1) * RET_QK_DIM] = ((a * sin + b * cos) * scale).astype(BF16)

    is_conv = (j >= conv_j0) & (j < conv_j0 + conv_nj)

    @pl.when(is_conv)
    def _():
        cj = j - conv_j0
        pad = SUBLANES

        @pl.when(i % tiles_per_seq == 0)
        def _():
            carry[cj] = jnp.zeros(carry.shape[1:], F32)

        for p in range(tn // sub):
            acc = jnp.dot(h_s[...], w_ref[:, p * sub:(p + 1) * sub], preferred_element_type=F32)
            for cc in range(sub // LANES):
                c = p * (sub // LANES) + cc
                cols = slice(c * LANES, (c + 1) * LANES)
                work[c, 0:pad, :] = carry[cj, c]
                work[c, pad:pad + tm, :] = acc[:, cc * LANES:(cc + 1) * LANES]
                conv = cb_ref[:, cols] + cw_ref[SSM_CONV - 1:SSM_CONV, cols] * work[c, pad:pad + tm, :]
                for k in range(SSM_CONV - 1):
                    shift = SSM_CONV - 1 - k
                    conv = conv + cw_ref[k:k + 1, cols] * work[c, pad - shift:pad - shift + tm, :]
                carry[cj, c] = work[c, tm:tm + pad, :]
                o_ref[:, cols] = _silu(conv).astype(BF16)

    @pl.when((j != 0) & jnp.logical_not(is_conv))
    def _():
        for p in range(tn // sub):
            o_ref[:, p * sub:(p + 1) * sub] = jnp.dot(
                h_s[...], w_ref[:, p * sub:(p + 1) * sub], preferred_element_type=F32).astype(BF16)


def _inproj_call(x2, row0, t, norm_w, mod3, cos, sin, w_main, w_dt, conv_w, conv_b, conv_off, seq, tm, tn):
    d = x2.shape[1]
    n = w_main.shape[1]
    conv_dim = conv_w.shape[1]
    tiles_per_seq = seq // tm
    off = row0 // tm
    assert tn == 2 * RET_HEADS * RET_QK_DIM, "rotary epilogue expects q and k in the first column tile"
    assert conv_off % tn == 0 and conv_dim % tn == 0
    conv_j0, conv_nj = conv_off // tn, conv_dim // tn
    sub = 512
    kern = functools.partial(_inproj_kernel, conv_j0=conv_j0, conv_nj=conv_nj,
                             tiles_per_seq=tiles_per_seq, tm=tm, tn=tn, sub=sub)
    conv_idx = lambda i, j: (0, jnp.clip(j - conv_j0, 0, conv_nj - 1))
    return pl.pallas_call(
        kern,
        out_shape=(jax.ShapeDtypeStruct((t, n), BF16), jax.ShapeDtypeStruct((t, LANES), F32)),
        grid=(t // tm, n // tn),
        in_specs=[
            pl.BlockSpec((tm, d), lambda i, j: (i + off, 0)),
            pl.BlockSpec((1, d), lambda i, j: (0, 0)),
            pl.BlockSpec((1, 1, d), lambda i, j: (((i + off) // tiles_per_seq) * N_MOD + 1, 0, 0)),
            pl.BlockSpec((1, 1, d), lambda i, j: (((i + off) // tiles_per_seq) * N_MOD + 0, 0, 0)),
            pl.BlockSpec((tm, LANES), lambda i, j: (i % tiles_per_seq, 0)),
            pl.BlockSpec((tm, LANES), lambda i, j: (i % tiles_per_seq, 0)),
            pl.BlockSpec((d, tn), lambda i, j: (0, j)),
            pl.BlockSpec((d, 2 * LANES), lambda i, j: (0, 0)),
            pl.BlockSpec((SSM_CONV, tn), conv_idx),
            pl.BlockSpec((1, tn), conv_idx),
        ],
        out_specs=(pl.BlockSpec((tm, tn), lambda i, j: (i, j)),
                   pl.BlockSpec((tm, LANES), lambda i, j: (i, 0))),
        scratch_shapes=[pltpu.VMEM((tm, d), BF16),
                        pltpu.VMEM((tn // LANES, tm + SUBLANES, LANES), F32),
                        pltpu.VMEM((conv_nj, tn // LANES, SUBLANES, LANES), F32)],
        compiler_params=_params(("arbitrary", "arbitrary")),
        name="inproj",
    )(x2, norm_w, mod3, mod3, cos, sin, w_main, w_dt, conv_w, conv_b.reshape(1, conv_dim))


def _retention_kernel(q_ref, k_ref, v_ref, g_ref, din_ref, dq_ref, dk_ref, w_ref, o_ref, state,
                      *, decay_c):
    c = pl.program_id(1)

    @pl.when(c == 0)
    def _():
        state[...] = jnp.zeros_like(state)

    acc = None
    for h in range(RET_HEADS):
        qh = q_ref[:, h * RET_QK_DIM:(h + 1) * RET_QK_DIM]
        kh = k_ref[:, h * RET_QK_DIM:(h + 1) * RET_QK_DIM]
        vh = v_ref[:, h * RET_V_DIM:(h + 1) * RET_V_DIM]
        scores = _nt_dot(qh, kh) * din_ref[h]
        inner = jnp.dot(scores.astype(BF16), vh, preferred_element_type=F32)
        st = state[h]
        cross = jnp.dot(qh, st.astype(BF16), preferred_element_type=F32) * dq_ref[h]
        kd = (kh.astype(F32) * dk_ref[h]).astype(BF16)
        state[h] = st * decay_c[h] + _tn_dot(kd, vh)
        ret = inner + cross
        ret = ret * lax.rsqrt(jnp.mean(ret * ret, axis=-1, keepdims=True) + EPS)
        gh = g_ref[:, h * RET_V_DIM:(h + 1) * RET_V_DIM].astype(F32)
        ret = ret * _silu(gh)
        part = jnp.dot(ret.astype(BF16), w_ref[h * RET_V_DIM:(h + 1) * RET_V_DIM, :],
                       preferred_element_type=F32)
        acc = part if acc is None else acc + part
    o_ref[...] = acc


def _retention_tables(chunk):
    lg = np.log(1.0 - 2.0 ** (-5.0 - np.arange(RET_HEADS, dtype=np.float64)))
    idx = np.arange(chunk, dtype=np.float64)
    rel = idx[:, None] - idx[None, :]
    causal = rel >= 0
    din = np.where(causal[None], np.exp(np.where(causal, rel, 0.0)[None] * lg[:, None, None]), 0.0)
    dq = np.exp((idx + 1.0)[None, :, None] * lg[:, None, None])
    dk = np.exp((chunk - 1.0 - idx)[None, :, None] * lg[:, None, None])
    dc = tuple(float(v) for v in np.exp(chunk * lg))
    return (jnp.asarray(din, F32), jnp.asarray(dq, F32), jnp.asarray(dk, F32), dc)


def _retention_call(proj, w_ret, bsz, seq, chunk):
    t = proj.shape[0]
    d = w_ret.shape[1]
    nc = seq // chunk
    qk_w = RET_HEADS * RET_QK_DIM
    v_w = RET_HEADS * RET_V_DIM
    din, dq, dk, dc = _retention_tables(chunk)
    kern = functools.partial(_retention_kernel, decay_c=dc)
    row = lambda b, c: b * nc + c
    return pl.pallas_call(
        kern,
        out_shape=jax.ShapeDtypeStruct((t, d), F32),
        grid=(bsz, nc),
        in_specs=[
            pl.BlockSpec((chunk, qk_w), lambda b, c: (row(b, c), 0)),
            pl.BlockSpec((chunk, qk_w), lambda b, c: (row(b, c), 1)),
            pl.BlockSpec((chunk, v_w), lambda b, c: (row(b, c), 1)),
            pl.BlockSpec((chunk, v_w), lambda b, c: (row(b, c), 2)),
            pl.BlockSpec((RET_HEADS, chunk, chunk), lambda b, c: (0, 0, 0)),
            pl.BlockSpec((RET_HEADS, chunk, 1), lambda b, c: (0, 0, 0)),
            pl.BlockSpec((RET_HEADS, chunk, 1), lambda b, c: (0, 0, 0)),
            pl.BlockSpec((v_w, d), lambda b, c: (0, 0)),
        ],
        out_specs=pl.BlockSpec((chunk, d), lambda b, c: (row(b, c), 0)),
        scratch_shapes=[pltpu.VMEM((RET_HEADS, RET_QK_DIM, RET_V_DIM), F32)],
        compiler_params=_params(("arbitrary", "arbitrary")),
        name="retention",
    )(proj, proj, proj, proj, din, dq, dk, w_ret)


def _ssd_kernel(z_ref, xbc_ref, dt_ref, dtb_ref, alog_ref, dsk_ref, nw_ref,
                tril_ref, exp_ref, w_ref, o_ref, state, yn_s, *, chunk, d_inner):
    c = pl.program_id(1)
    heads_per_group = d_inner // SSM_HEAD_DIM // SSM_GROUPS
    gw = heads_per_group * SSM_HEAD_DIM
    assert SSM_HEAD_DIM * 2 == LANES and gw == 2 * LANES

    @pl.when(c == 0)
    def _():
        state[...] = jnp.zeros_like(state)

    dt = jax.nn.softplus(dt_ref[...] + dtb_ref[...])
    a = -jnp.exp(alog_ref[...])
    adt = dt * a
    acs = jnp.dot(tril_ref[...], adt, preferred_element_type=F32, precision=HIGHEST)
    acs_t = acs.T
    dt_x = jnp.dot(dt.astype(BF16), exp_ref[...], preferred_element_type=F32)
    li = lax.broadcasted_iota(jnp.int32, (chunk, chunk), 0)
    si = lax.broadcasted_iota(jnp.int32, (chunk, chunk), 1)
    causal = li >= si
    low_half = si < SSM_HEAD_DIM
    lane_g = lax.broadcasted_iota(jnp.int32, (chunk, gw), 1)

    b_off = d_inner
    c_off = d_inner + SSM_GROUPS * SSM_STATE
    for g in range(SSM_GROUPS):
        bm = xbc_ref[:, b_off + g * SSM_STATE: b_off + (g + 1) * SSM_STATE]
        cm = xbc_ref[:, c_off + g * SSM_STATE: c_off + (g + 1) * SSM_STATE]
        xs_g = xbc_ref[:, g * gw:(g + 1) * gw].astype(F32)
        xdt_g = xs_g * dt_x[:, g * gw:(g + 1) * gw]
        cb = _nt_dot(cm, bm)
        cols, ms, xm = [], [], []
        for jh in range(heads_per_group):
            h = g * heads_per_group + jh
            col = jnp.broadcast_to(acs[:, h:h + 1], (chunk, chunk))
            seg = jnp.exp(jnp.where(causal, col - acs_t[h:h + 1, :], -jnp.inf))
            cols.append(col)
            ms.append((cb * seg).astype(BF16))
            in_head = (lane_g >= jh * SSM_HEAD_DIM) & (lane_g < (jh + 1) * SSM_HEAD_DIM)
            xm.append(jnp.where(in_head, xdt_g, 0.0).astype(BF16))
        y_diag = jnp.dot(jnp.concatenate(ms, axis=-1), jnp.concatenate(xm, axis=0),
                         preferred_element_type=F32)
        a_x = jnp.concatenate([jnp.where(low_half, cols[0], cols[1]),
                               jnp.where(low_half, cols[2], cols[3])], axis=-1)
        e_acs_x = jnp.exp(a_x)
        a_last_x = a_x[chunk - 1:chunk, :]
        st = state[g]
        y_off = jnp.dot(cm, st.astype(BF16), preferred_element_type=F32) * e_acs_x
        xdec = (xdt_g * jnp.exp(a_last_x - a_x)).astype(BF16)
        state[g] = st * e_acs_x[chunk - 1:chunk, :] + _tn_dot(bm, xdec)
        y = y_diag + y_off + dsk_ref[:, g * gw:(g + 1) * gw] * xs_g
        yz = y * _silu(z_ref[:, g * gw:(g + 1) * gw].astype(F32))
        yn = yz * lax.rsqrt(jnp.mean(yz * yz, axis=-1, keepdims=True) + EPS) * nw_ref[:, g * gw:(g + 1) * gw]
        yn_s[:, g * gw:(g + 1) * gw] = yn.astype(BF16)
    o_ref[...] = jnp.dot(yn_s[...], w_ref[...], preferred_element_type=F32)


def _ssd_call(proj, dt_raw, dt_bias, a_log, d_skip, ssm_norm, w_ssm, bsz, seq, chunk):
    t = proj.shape[0]
    d_inner, d = w_ssm.shape
    conv_dim = d_inner + 2 * SSM_GROUPS * SSM_STATE
    n_heads = d_inner // SSM_HEAD_DIM
    nc = seq // chunk
    gw = d_inner // SSM_GROUPS
    pad_h = lambda v: jnp.pad(v.astype(F32), (0, LANES - n_heads)).reshape(1, LANES)
    assert chunk == LANES, "the per-head decay tiles are built lane-for-lane against the chunk"
    tril = jnp.asarray(np.tril(np.ones((chunk, chunk), np.float32)))
    expand = np.zeros((LANES, d_inner), np.float32)
    for h in range(n_heads):
        expand[h, h * SSM_HEAD_DIM:(h + 1) * SSM_HEAD_DIM] = 1.0
    expand = jnp.asarray(expand, BF16)
    kern = functools.partial(_ssd_kernel, chunk=chunk, d_inner=d_inner)
    row = lambda b, c: b * nc + c
    z_blk = (2 * RET_HEADS * RET_QK_DIM + 2 * RET_HEADS * RET_V_DIM) // d_inner
    xbc_blk = (2 * RET_HEADS * RET_QK_DIM + 2 * RET_HEADS * RET_V_DIM + d_inner) // conv_dim
    full = lambda shape: pl.BlockSpec(shape, lambda b, c: (0,) * len(shape))
    return pl.pallas_call(
        kern,
        out_shape=jax.ShapeDtypeStruct((t, d), F32),
        grid=(bsz, nc),
        in_specs=[
            pl.BlockSpec((chunk, d_inner), lambda b, c: (row(b, c), z_blk)),
            pl.BlockSpec((chunk, conv_dim), lambda b, c: (row(b, c), xbc_blk)),
            pl.BlockSpec((chunk, LANES), lambda b, c: (row(b, c), 0)),
            full((1, LANES)), full((1, LANES)),
            full((1, d_inner)), full((1, d_inner)), full((chunk, chunk)), full((LANES, d_inner)),
            full((d_inner, d)),
        ],
        out_specs=pl.BlockSpec((chunk, d), lambda b, c: (row(b, c), 0)),
        scratch_shapes=[pltpu.VMEM((SSM_GROUPS, SSM_STATE, gw), F32),
                        pltpu.VMEM((chunk, d_inner), BF16)],
        compiler_params=_params(("arbitrary", "arbitrary")),
        name="ssd",
    )(proj, proj, dt_raw, pad_h(dt_bias), pad_h(a_log),
      jnp.repeat(d_skip.astype(F32), SSM_HEAD_DIM).reshape(1, d_inner), ssm_norm.reshape(1, d_inner),
      tril, expand, w_ssm)


def _merge_kernel(ya_ref, yb_ref, ga_ref, gb_ref, x_ref, gm_ref, scf_ref, shf_ref, nw_ref, wo_ref,
                  wr_ref, br_ref, tri_ref,
                  x1_ref, h2_ref, idx_ref, rank_ref, prow_ref, cnt_ref, cnt_s, *, tm):
    i = pl.program_id(0)

    @pl.when(i == 0)
    def _():
        cnt_s[...] = jnp.zeros_like(cnt_s)

    merged = (jax.nn.sigmoid(ga_ref[...].astype(F32)) * ya_ref[...]
              + jax.nn.sigmoid(gb_ref[...].astype(F32)) * yb_ref[...])
    mo = jnp.dot(merged.astype(BF16), wo_ref[...], preferred_element_type=F32)
    x1 = x_ref[...] + gm_ref[0] * mo
    x1_ref[...] = x1
    ms = jnp.mean(x1 * x1, axis=-1, keepdims=True)
    h2 = x1 * lax.rsqrt(ms + EPS) * nw_ref[...] * (1.0 + scf_ref[0]) + shf_ref[0]
    for s in range(SUBLANES):
        h2_ref[pl.ds(s, tm, stride=SUBLANES), :] = h2[:, s * LANES:(s + 1) * LANES]

    lg = _nt_dot(wr_ref[...], h2, precision=HIGHEST) + br_ref[...]
    sub = lax.broadcasted_iota(jnp.int32, lg.shape, 0)
    work = lg
    vals, idxs, sels = [], [], []
    for _ in range(TOP_K):
        m = jnp.max(work, axis=0, keepdims=True)
        ik = jnp.min(jnp.where(work == m, sub, N_EXPERTS), axis=0, keepdims=True)
        sel = sub == ik
        vals.append(m)
        idxs.append(ik)
        sels.append(sel)
        work = jnp.where(sel, -jnp.inf, work)
    exps = [jnp.exp(v - vals[0]) for v in vals]
    denom = exps[0]
    for e in exps[1:]:
        denom = denom + e
    probs = [e / denom for e in exps]

    base = cnt_s[:, 0:1]
    ranks = []
    for k in range(TOP_K):
        mk = jnp.where(sels[k], 1.0, 0.0)
        pre = jnp.dot(mk.astype(BF16), tri_ref[...], preferred_element_type=F32)
        ranks.append(jnp.sum(jnp.where(sels[k], pre + base, 0.0), axis=0, keepdims=True))
        base = base + jnp.sum(mk, axis=1, keepdims=True)
    cnt_s[...] = jnp.broadcast_to(base, cnt_s.shape)
    cnt_ref[...] = cnt_s[...].astype(jnp.int32)

    zi = jnp.zeros((SUBLANES - TOP_K, tm), jnp.int32)
    idx_ref[0] = jnp.concatenate(idxs + [zi], axis=0)
    rank_ref[0] = jnp.concatenate([r.astype(jnp.int32) for r in ranks] + [zi], axis=0)
    pt = jnp.concatenate(probs + [jnp.zeros((LANES - TOP_K, tm), F32)], axis=0)
    prow_ref[...] = pt.T


def _merge_call(ya, yb, proj, x2, row0, mod3, norm_w, w_out, w_router_t, b_router, seq, tm):
    t, d = ya.shape
    nt = t // tm
    tiles_per_seq = seq // tm
    off = row0 // tm
    ga_blk = proj.shape[1] // d - 2
    tri = jnp.asarray(np.triu(np.ones((tm, tm), np.float32), 1), BF16)
    kern = functools.partial(_merge_kernel, tm=tm)
    modspec = lambda m: pl.BlockSpec((1, 1, d), lambda i: (((i + off) // tiles_per_seq) * N_MOD + m, 0, 0))
    return pl.pallas_call(
        kern,
        out_shape=(jax.ShapeDtypeStruct((t, d), F32), jax.ShapeDtypeStruct((t * SUBLANES, LANES), F32),
                   jax.ShapeDtypeStruct((nt, SUBLANES, tm), jnp.int32),
                   jax.ShapeDtypeStruct((nt, SUBLANES, tm), jnp.int32),
                   jax.ShapeDtypeStruct((t, LANES), F32),
                   jax.ShapeDtypeStruct((N_EXPERTS, LANES), jnp.int32)),
        grid=(nt,),
        in_specs=[
            pl.BlockSpec((tm, d), lambda i: (i, 0)),
            pl.BlockSpec((tm, d), lambda i: (i, 0)),
            pl.BlockSpec((tm, d), lambda i: (i, ga_blk)),
            pl.BlockSpec((tm, d), lambda i: (i, ga_blk + 1)),
            pl.BlockSpec((tm, d), lambda i: (i + off, 0)),
            modspec(2), modspec(4), modspec(3),
            pl.BlockSpec((1, d), lambda i: (0, 0)),
            pl.BlockSpec((d, d), lambda i: (0, 0)),
            pl.BlockSpec((N_EXPERTS, d), lambda i: (0, 0)),
            pl.BlockSpec((N_EXPERTS, 1), lambda i: (0, 0)),
            pl.BlockSpec((tm, tm), lambda i: (0, 0)),
        ],
        out_specs=(pl.BlockSpec((tm, d), lambda i: (i, 0)),
                   pl.BlockSpec((tm * SUBLANES, LANES), lambda i: (i, 0)),
                   pl.BlockSpec((1, SUBLANES, tm), lambda i: (i, 0, 0)),
                   pl.BlockSpec((1, SUBLANES, tm), lambda i: (i, 0, 0)),
                   pl.BlockSpec((tm, LANES), lambda i: (i, 0)),
                   pl.BlockSpec((N_EXPERTS, LANES), lambda i: (0, 0))),
        scratch_shapes=[pltpu.VMEM((N_EXPERTS, LANES), F32)],
        compiler_params=_params(("arbitrary",)),
        name="merge",
    )(ya, yb, proj, proj, x2, mod3, mod3, mod3, norm_w, w_out, w_router_t,
      b_router.reshape(N_EXPERTS, 1), tri)


def _sc_mesh():
    return plsc.VectorSubcoreMesh(core_axis_name="c", subcore_axis_name="s")


def _sc_worker():
    return lax.axis_index("s") * SC_CORES + lax.axis_index("c")


def _sc_scatter_rows(rows, dest, n_out):
    t = rows.shape[0]
    n_k = dest.shape[0]
    g = SC_GROUP
    n_w = SC_CORES * SC_SUBCORES
    assert t % (n_w * g) == 0
    cpw = t // (n_w * g)
    dest_w = dest.reshape(n_k, n_w, cpw, g).transpose(1, 0, 2, 3)

    @functools.partial(
        pl.kernel, mesh=_sc_mesh(),
        out_type=jax.ShapeDtypeStruct((n_out,) + rows.shape[1:], rows.dtype),
        scratch_types=[pltpu.VMEM((n_k, cpw, g), jnp.int32),
                       pltpu.VMEM((g,) + rows.shape[1:], rows.dtype),
                       pltpu.SemaphoreType.DMA],
    )
    def scatter(rows_hbm, dest_hbm, out_hbm, idx_v, rows_v, sem):
        wid = _sc_worker()
        pltpu.sync_copy(dest_hbm.at[wid], idx_v)

        @pl.loop(0, cpw)
        def _(cc):
            r0 = pl.multiple_of((wid * cpw + cc) * g, g)
            pltpu.sync_copy(rows_hbm.at[pl.ds(r0, g)], rows_v)
            copies = [pltpu.async_copy(rows_v, out_hbm.at[idx_v.at[k, cc]], sem) for k in range(n_k)]
            for cp in copies:
                cp.wait()

    return scatter(rows, dest_w)


def _sc_gather_rows(table, idx):
    m = idx.shape[0]
    g = SC_GROUP
    n_w = SC_CORES * SC_SUBCORES
    assert m % (n_w * g) == 0
    per_w = m // n_w

    @functools.partial(
        pl.kernel, mesh=_sc_mesh(),
        out_type=jax.ShapeDtypeStruct((m,) + table.shape[1:], table.dtype),
        scratch_types=[pltpu.VMEM((per_w,), jnp.int32),
                       pltpu.VMEM((g,) + table.shape[1:], table.dtype),
                       pltpu.SemaphoreType.DMA],
    )
    def gather(table_hbm, idx_hbm, out_hbm, idx_v, rows_v, sem):
        base = _sc_worker() * per_w
        pltpu.sync_copy(idx_hbm.at[pl.ds(base, per_w)], idx_v)

        @pl.loop(0, per_w // g)
        def _(cc):
            off = pl.multiple_of(cc * g, g)
            pltpu.async_copy(table_hbm.at[idx_v.at[pl.ds(off, g)]], rows_v, sem).wait()
            pltpu.sync_copy(rows_v, out_hbm.at[pl.ds(base + off, g)])

    return gather(table, idx)


def _ffn_kernel(be_ref, br_ref, bv_ref, x_ref, wgu_ref, bgu_ref, wd_ref, bd_ref, o_ref, wgu_s, wd_s,
                *, bm, d_ff):
    i = pl.program_id(0)

    @pl.when((i == 0) | (be_ref[i] != be_ref[jnp.maximum(i - 1, 0)]))
    def _():
        wgu_s[...] = wgu_ref[0].astype(BF16)
        wd_s[...] = wd_ref[0].astype(BF16)

    @pl.when(bv_ref[i] == 1)
    def _():
        x = jnp.concatenate([x_ref[pl.ds(s, bm, stride=SUBLANES), :] for s in range(SUBLANES)],
                            axis=-1).astype(BF16)
        y = None
        for p in range(d_ff // FFN_PIECE):
            lo, hi = p * FFN_PIECE, (p + 1) * FFN_PIECE
            gate = jnp.dot(x, wgu_s[:, lo:hi], preferred_element_type=F32) + bgu_ref[0][:, lo:hi]
            up = jnp.dot(x, wgu_s[:, d_ff + lo:d_ff + hi], preferred_element_type=F32) \
                + bgu_ref[0][:, d_ff + lo:d_ff + hi]
            gate = jnp.minimum(gate, SWIGLU_LIMIT)
            up = jnp.clip(up, -SWIGLU_LIMIT, SWIGLU_LIMIT)
            act = gate * jax.nn.sigmoid(SWIGLU_ALPHA * gate) * (up + 1.0)
            part = jnp.dot(act.astype(BF16), wd_s[lo:hi, :], preferred_element_type=F32)
            y = part if y is None else y + part
        y = y + bd_ref[0]
        for s in range(SUBLANES):
            o_ref[pl.ds(s, bm, stride=SUBLANES), :] = y[:, s * LANES:(s + 1) * LANES]

    @pl.when(bv_ref[i] == 0)
    def _():
        o_ref[...] = jnp.zeros_like(o_ref)


def _ffn_call(blk_e, blk_row, blk_valid, xs, w_gu, b_gu, w_d, b_d, bm):
    n_e, d, f2 = w_gu.shape
    d_ff = f2 // 2
    nb = blk_e.shape[0]
    kern = functools.partial(_ffn_kernel, bm=bm, d_ff=d_ff)
    gs = pltpu.PrefetchScalarGridSpec(
        num_scalar_prefetch=3,
        grid=(nb,),
        in_specs=[pl.BlockSpec((bm * SUBLANES, LANES), lambda i, be, br, bv: (br[i], 0)),
                  pl.BlockSpec((1, d, f2), lambda i, be, br, bv: (be[i], 0, 0)),
                  pl.BlockSpec((1, 1, f2), lambda i, be, br, bv: (be[i], 0, 0)),
                  pl.BlockSpec((1, d_ff, d), lambda i, be, br, bv: (be[i], 0, 0)),
                  pl.BlockSpec((1, 1, d), lambda i, be, br, bv: (be[i], 0, 0))],
        out_specs=pl.BlockSpec((bm * SUBLANES, LANES), lambda i, be, br, bv: (i, 0)),
        scratch_shapes=[pltpu.VMEM((d, f2), BF16), pltpu.VMEM((d_ff, d), BF16)],
    )
    return pl.pallas_call(
        kern,
        out_shape=jax.ShapeDtypeStruct((nb * bm * SUBLANES, LANES), F32),
        grid_spec=gs,
        compiler_params=_params(("arbitrary",)),
        name="ffn",
    )(blk_e, blk_row, blk_valid, xs, w_gu, b_gu.reshape(n_e, 1, f2), w_d, b_d.reshape(n_e, 1, d))


def _combine_kernel(y0_ref, y1_ref, y2_ref, y3_ref, prow_ref, x1_ref, gf_ref, nw_ref, *rest, tm):
    o_ref = rest[-1]
    p = prow_ref[...]
    ys = (y0_ref, y1_ref, y2_ref, y3_ref)
    for s in range(SUBLANES):
        moe = None
        for k in range(TOP_K):
            piece = ys[k][pl.ds(s, tm, stride=SUBLANES), :] * p[:, k:k + 1]
            moe = piece if moe is None else moe + piece
        sl = slice(s * LANES, (s + 1) * LANES)
        o_ref[:, sl] = x1_ref[:, sl] + gf_ref[0][:, sl] * moe
    xo = o_ref[...]
    o_ref[...] = xo * lax.rsqrt(jnp.mean(xo * xo, axis=-1, keepdims=True) + EPS) * nw_ref[...]


def _combine_call(ytok, prow, x1, mod3, norm_final, seq, tm, row0, t_total, out_prev):
    t, d = x1.shape
    nt = t // tm
    tiles_per_seq = seq // tm
    off = row0 // tm
    kern = functools.partial(_combine_kernel, tm=tm)
    yspec = lambda k: pl.BlockSpec((tm * SUBLANES, LANES), lambda i: (k * nt + i, 0))
    in_specs = [yspec(0), yspec(1), yspec(2), yspec(3),
                pl.BlockSpec((tm, LANES), lambda i: (i, 0)),
                pl.BlockSpec((tm, d), lambda i: (i, 0)),
                pl.BlockSpec((1, 1, d), lambda i: (((i + off) // tiles_per_seq) * N_MOD + 5, 0, 0)),
                pl.BlockSpec((1, d), lambda i: (0, 0))]
    args = [ytok, ytok, ytok, ytok, prow, x1, mod3, norm_final]
    aliases = {}
    if out_prev is not None:
        in_specs.append(pl.BlockSpec(memory_space=pl.ANY))
        aliases = {len(args): 0}
        args.append(out_prev)
    return pl.pallas_call(
        kern,
        out_shape=jax.ShapeDtypeStruct((t_total, d), F32),
        grid=(nt,),
        in_specs=in_specs,
        out_specs=pl.BlockSpec((tm, d), lambda i: (i + off, 0)),
        input_output_aliases=aliases,
        compiler_params=_params(("arbitrary",)),
        name="combine",
    )(*args)


def _plan(seq):
    def fit(pref):
        tm = min(pref, seq)
        assert seq % tm == 0
        return tm
    return dict(tm_in=fit(1024), tm_merge=fit(512), tm_moe=fit(512),
                ret_chunk=fit(RET_CHUNK), ssm_chunk=fit(SSM_CHUNK))


def _layer(x2, mod3, bsz, seq, norm_mix, norm_ffn, w_in, conv_w, conv_b, dt_bias, a_log, d_skip, ssm_norm,
           w_ret_out, w_ssm_out, w_out, w_router, b_router, w_gate_up, b_gate_up, w_down, b_down,
           norm_final):
    t, d = x2.shape
    plan = _plan(seq)
    qk_w = RET_HEADS * RET_QK_DIM
    v_w = RET_HEADS * RET_V_DIM
    d_inner = w_ssm_out.shape[0]
    conv_dim = conv_w.shape[1]
    n_heads = d_inner // SSM_HEAD_DIM
    dt_off = 2 * qk_w + 2 * v_w + d_inner + conv_dim

    w_main = jnp.concatenate([w_in[:, :dt_off], w_in[:, dt_off + n_heads:]], axis=1).astype(BF16)
    w_dt = jnp.pad(w_in[:, dt_off:dt_off + n_heads], ((0, 0), (0, LANES - n_heads)))
    w_dt_hi = w_dt.astype(BF16)
    w_dt = jnp.concatenate([w_dt_hi, (w_dt - w_dt_hi.astype(F32)).astype(BF16)], axis=1)
    half = RET_QK_DIM // 2
    inv_freq = ROPE_BASE ** (-jnp.arange(half, dtype=F32) / half)
    ang = jnp.arange(seq, dtype=F32)[:, None] * inv_freq[None, :]
    cos, sin = jnp.cos(ang), jnp.sin(ang)

    w_ret_b, w_ssm_b, w_out_b, w_router_t = (w_ret_out.astype(BF16), w_ssm_out.astype(BF16),
                                             w_out.astype(BF16), w_router.T)
    bm = FFN_BLOCK
    slab = (SUBLANES, LANES)

    def mixer(row0, tg, bg):
        proj, dt_raw = _inproj_call(x2, row0, tg, norm_mix.reshape(1, d), mod3, cos, sin, w_main, w_dt,
                                    conv_w, conv_b, 2 * qk_w + 2 * v_w + d_inner, seq, plan["tm_in"], 2 * qk_w)
        ya = _retention_call(proj, w_ret_b, bg, seq, plan["ret_chunk"])
        yb = _ssd_call(proj, dt_raw, dt_bias, a_log, d_skip, ssm_norm, w_ssm_b, bg, seq, plan["ssm_chunk"])
        x1, h2, idx, rank, prow, cnt = _merge_call(ya, yb, proj, x2, row0, mod3, norm_ffn.reshape(1, d),
                                                   w_out_b, w_router_t, b_router, seq, plan["tm_merge"])
        counts = cnt[:, 0]
        padded = ((counts + bm - 1) // bm) * bm
        pad_end = jnp.cumsum(padded)
        start_pad = pad_end - padded
        n_blocks = (tg * TOP_K) // bm + N_EXPERTS
        e_ids = jnp.arange(N_EXPERTS, dtype=jnp.int32)[:, None, None, None]
        dest = rank + jnp.sum(jnp.where(idx[None] == e_ids, start_pad[:, None, None, None], 0), axis=0)
        dest = dest[:, :TOP_K, :].transpose(1, 0, 2).reshape(TOP_K, tg).astype(jnp.int32)
        n_real = pad_end[-1] // bm
        blk_valid = (jnp.arange(n_blocks) < n_real).astype(jnp.int32)
        blk_row = jnp.minimum(jnp.arange(n_blocks), n_real - 1).astype(jnp.int32)
        blk_e = jnp.minimum(jnp.sum(pad_end[None, :] <= (blk_row * bm)[:, None], axis=1),
                            N_EXPERTS - 1).astype(jnp.int32)
        xs = _sc_scatter_rows(h2.reshape((tg,) + slab), dest, n_blocks * bm)
        return dict(x1=x1, prow=prow, dest=dest, blocks=(blk_e, blk_row, blk_valid), xs=xs, row0=row0)

    def experts(m):
        n_rows = m["xs"].shape[0]
        ys = _ffn_call(*m["blocks"], m["xs"].reshape(n_rows * SUBLANES, LANES),
                       w_gate_up, b_gate_up, w_down, b_down, bm)
        return _sc_gather_rows(ys.reshape((n_rows,) + slab), m["dest"].reshape(-1))

    n_groups = N_GROUPS if bsz % N_GROUPS == 0 else 1
    bg = bsz // n_groups
    tg = bg * seq
    groups = [mixer(g * tg, tg, bg) for g in range(n_groups)]
    ytoks = [experts(m) for m in groups]
    out = None
    for y, m in zip(ytoks, groups):
        out = _combine_call(y.reshape(TOP_K * tg * SUBLANES, LANES), m["prow"], m["x1"], mod3,
                            norm_final.reshape(1, d), seq, plan["tm_moe"], m["row0"], t, out)
    return out


def kernel(x, c, w_ada, b_ada, norm_mix, norm_ffn, w_in, conv_w, conv_b, dt_bias, a_log, d_skip, ssm_norm,
           w_ret_out, w_ssm_out, w_out, w_router, b_router, w_gate_up, b_gate_up, w_down, b_down, norm_final):
    bsz, seq, d = x.shape
    depth = w_ada.shape[0]
    assert depth == 1, "the final norm is fused into the single layer's last kernel"
    x2 = x.reshape(bsz * seq, d)
    l = 0
    mod = _mod_call(c, w_ada[l], b_ada[l])
    mod3 = mod.reshape(bsz * N_MOD, 1, d)
    out = _layer(x2, mod3, bsz, seq, norm_mix[l], norm_ffn[l], w_in[l], conv_w[l], conv_b[l], dt_bias[l],
                 a_log[l], d_skip[l], ssm_norm[l], w_ret_out[l], w_ssm_out[l], w_out[l], w_router[l],
                 b_router[l], w_gate_up[l], b_gate_up[l], w_down[l], b_down[l], norm_final)
    return out.reshape(bsz, seq, d)
```

```python
import functools
import math

import numpy as np
import jax
import jax.numpy as jnp
from jax import lax
from jax.experimental import pallas as pl
from jax.experimental.pallas import tpu as pltpu
from jax.experimental.pallas import tpu_sc as plsc

F32 = jnp.float32
BF16 = jnp.bfloat16
HIGHEST = lax.Precision.HIGHEST

EPS = 1e-6
N_MOD = 6
RET_HEADS = 4
RET_QK_DIM = 256
RET_V_DIM = 512
ROPE_BASE = 10000.0
SSM_HEAD_DIM = 64
SSM_GROUPS = 8
SSM_STATE = 128
SSM_CONV = 4
N_EXPERTS = 32
TOP_K = 4
SWIGLU_LIMIT = 7.0
SWIGLU_ALPHA = 1.702

LANES = 128
SUBLANES = 8
VMEM_LIMIT = 56 * 1024 * 1024

RET_CHUNK = 256
SSM_CHUNK = 128
FFN_BLOCK = 512
SC_CORES = 2
SC_SUBCORES = 16
SC_GROUP = 64
N_GROUPS = 2


def _params(sem, vmem=VMEM_LIMIT):
    return pltpu.CompilerParams(dimension_semantics=sem, vmem_limit_bytes=vmem)


def _nt_dot(a, b, **kw):
    return lax.dot_general(a, b, (((1,), (1,)), ((), ())), preferred_element_type=F32, **kw)


def _tn_dot(a, b, **kw):
    return lax.dot_general(a, b, (((0,), (0,)), ((), ())), preferred_element_type=F32, **kw)


def _silu(v):
    return v * jax.nn.sigmoid(v)


SLAB_ROWS = 4
HIGH_HALF = 0xFFFF0000


def _store_slabs(ref, vals, n):
    for s in range(SLAB_ROWS):
        lo = vals[:, s * LANES:(s + 1) * LANES].astype(BF16).astype(F32)
        hi = vals[:, (s + SLAB_ROWS) * LANES:(s + SLAB_ROWS + 1) * LANES].astype(BF16).astype(F32)
        word = (pltpu.bitcast(lo, jnp.uint32) >> 16) | (pltpu.bitcast(hi, jnp.uint32) & jnp.uint32(HIGH_HALF))
        ref[pl.ds(s, n, stride=SLAB_ROWS), :] = word


def _load_slabs(ref, n, base=0):
    lo, hi = [], []
    for s in range(SLAB_ROWS):
        word = ref[pl.ds(base + s, n, stride=SLAB_ROWS), :]
        lo.append(pltpu.bitcast(word << 16, F32))
        hi.append(pltpu.bitcast(word & jnp.uint32(HIGH_HALF), F32))
    return lo + hi


def _mod_kernel(c_ref, w_ref, b_ref, o_ref):
    cond = _silu(c_ref[...])
    o_ref[...] = jnp.dot(cond, w_ref[...], preferred_element_type=F32, precision=HIGHEST) + b_ref[...]


def _mod_call(c, w_ada, b_ada):
    bsz, d = c.shape
    n = w_ada.shape[1]
    return pl.pallas_call(
        _mod_kernel,
        out_shape=jax.ShapeDtypeStruct((bsz, n), F32),
        grid=(n // d,),
        in_specs=[pl.BlockSpec((bsz, d), lambda j: (0, 0)),
                  pl.BlockSpec((d, d), lambda j: (0, j)),
                  pl.BlockSpec((1, d), lambda j: (0, j))],
        out_specs=pl.BlockSpec((bsz, d), lambda j: (0, j)),
        compiler_params=_params(("arbitrary",)),
        name="mod",
    )(c, w_ada, b_ada.reshape(1, n))


def _inproj_kernel(x_ref, nw_ref, sc_ref, sh_ref, cos_ref, sin_ref, w_ref, wdt_ref, cw_ref, cb_ref,
                   o_ref, dt_ref, h_s, work, carry, *, conv_j0, conv_nj, tiles_per_seq, tm, tn, sub):
    i = pl.program_id(0)
    j = pl.program_id(1)
    n_dt = dt_ref.shape[1]

    @pl.when(j == 0)
    def _():
        xf = x_ref[...]
        ms = jnp.mean(xf * xf, axis=-1, keepdims=True)
        y = xf * lax.rsqrt(ms + EPS) * nw_ref[...]
        hm = y * (1.0 + sc_ref[0]) + sh_ref[0]
        hb = hm.astype(BF16)
        h_s[...] = hb
        h_lo = (hm - hb.astype(F32)).astype(BF16)
        d_hi = jnp.dot(hb, wdt_ref[...], preferred_element_type=F32)
        d_lo = jnp.dot(h_lo, wdt_ref[:, :n_dt], preferred_element_type=F32)
        dt_ref[...] = d_hi[:, :n_dt] + d_hi[:, n_dt:] + d_lo
        cos = cos_ref[...]
        sin = sin_ref[...]
        half = RET_QK_DIM // 2
        for p in range(tn // sub):
            acc = jnp.dot(hb, w_ref[:, p * sub:(p + 1) * sub], preferred_element_type=F32)
            for cc in range(sub // RET_QK_DIM):
                c = p * (sub // RET_QK_DIM) + cc
                a = acc[:, cc * RET_QK_DIM: cc * RET_QK_DIM + half]
                b = acc[:, cc * RET_QK_DIM + half: (cc + 1) * RET_QK_DIM]
                scale = 1.0 if c < RET_HEADS else RET_QK_DIM ** -0.5
                o_ref[:, c * RET_QK_DIM: c * RET_QK_DIM + half] = ((a * cos - b * sin) * scale).astype(BF16)
                o_ref[:, c * RET_QK_DIM + half: (c + 1) * RET_QK_DIM] = ((a * sin + b * cos) * scale).astype(BF16)

    is_conv = (j >= conv_j0) & (j < conv_j0 + conv_nj)

    @pl.when(is_conv)
    def _():
        cj = j - conv_j0
        pad = SUBLANES

        @pl.when(i % tiles_per_seq == 0)
        def _():
            carry[cj] = jnp.zeros(carry.shape[1:], F32)

        for p in range(tn // sub):
            acc = jnp.dot(h_s[...], w_ref[:, p * sub:(p + 1) * sub], preferred_element_type=F32)
            for cc in range(sub // LANES):
                c = p * (sub // LANES) + cc
                cols = slice(c * LANES, (c + 1) * LANES)
                work[c, 0:pad, :] = carry[cj, c]
                work[c, pad:pad + tm, :] = acc[:, cc * LANES:(cc + 1) * LANES]
                conv = cb_ref[:, cols] + cw_ref[SSM_CONV - 1:SSM_CONV, cols] * work[c, pad:pad + tm, :]
                for k in range(SSM_CONV - 1):
                    shift = SSM_CONV - 1 - k
                    conv = conv + cw_ref[k:k + 1, cols] * work[c, pad - shift:pad - shift + tm, :]
                carry[cj, c] = work[c, tm:tm + pad, :]
                o_ref[:, cols] = _silu(conv).astype(BF16)

    @pl.when((j != 0) & jnp.logical_not(is_conv))
    def _():
        for p in range(tn // sub):
            o_ref[:, p * sub:(p + 1) * sub] = jnp.dot(
                h_s[...], w_ref[:, p * sub:(p + 1) * sub], preferred_element_type=F32).astype(BF16)


def _inproj_call(x2, row0, t, norm_w, mod3, cos, sin, w_main, w_dt, conv_w, conv_b, conv_off, seq, tm, tn):
    d = x2.shape[1]
    n = w_main.shape[1]
    conv_dim = conv_w.shape[1]
    tiles_per_seq = seq // tm
    off = row0 // tm
    assert tn == 2 * RET_HEADS * RET_QK_DIM, "rotary epilogue expects q and k in the first column tile"
    assert conv_off % tn == 0 and conv_dim % tn == 0
    conv_j0, conv_nj = conv_off // tn, conv_dim // tn
    sub = 512
    kern = functools.partial(_inproj_kernel, conv_j0=conv_j0, conv_nj=conv_nj,
                             tiles_per_seq=tiles_per_seq, tm=tm, tn=tn, sub=sub)
    conv_idx = lambda i, j: (0, jnp.clip(j - conv_j0, 0, conv_nj - 1))
    return pl.pallas_call(
        kern,
        out_shape=(jax.ShapeDtypeStruct((t, n), BF16), jax.ShapeDtypeStruct((t, LANES), F32)),
        grid=(t // tm, n // tn),
        in_specs=[
            pl.BlockSpec((tm, d), lambda i, j: (i + off, 0)),
            pl.BlockSpec((1, d), lambda i, j: (0, 0)),
            pl.BlockSpec((1, 1, d), lambda i, j: (((i + off) // tiles_per_seq) * N_MOD + 1, 0, 0)),
            pl.BlockSpec((1, 1, d), lambda i, j: (((i + off) // tiles_per_seq) * N_MOD + 0, 0, 0)),
            pl.BlockSpec((tm, LANES), lambda i, j: (i % tiles_per_seq, 0)),
            pl.BlockSpec((tm, LANES), lambda i, j: (i % tiles_per_seq, 0)),
            pl.BlockSpec((d, tn), lambda i, j: (0, j)),
            pl.BlockSpec((d, 2 * LANES), lambda i, j: (0, 0)),
            pl.BlockSpec((SSM_CONV, tn), conv_idx),
            pl.BlockSpec((1, tn), conv_idx),
        ],
        out_specs=(pl.BlockSpec((tm, tn), lambda i, j: (i, j)),
                   pl.BlockSpec((tm, LANES), lambda i, j: (i, 0))),
        scratch_shapes=[pltpu.VMEM((tm, d), BF16),
                        pltpu.VMEM((tn // LANES, tm + SUBLANES, LANES), F32),
                        pltpu.VMEM((conv_nj, tn // LANES, SUBLANES, LANES), F32)],
        compiler_params=_params(("arbitrary", "arbitrary")),
        name="inproj",
    )(x2, norm_w, mod3, mod3, cos, sin, w_main, w_dt, conv_w, conv_b.reshape(1, conv_dim))


def _retention_kernel(q_ref, k_ref, v_ref, g_ref, din_ref, dq_ref, dk_ref, w_ref, o_ref, state,
                      *, decay_c):
    c = pl.program_id(1)

    @pl.when(c == 0)
    def _():
        state[...] = jnp.zeros_like(state)

    acc = None
    for h in range(RET_HEADS):
        qh = q_ref[:, h * RET_QK_DIM:(h + 1) * RET_QK_DIM]
        kh = k_ref[:, h * RET_QK_DIM:(h + 1) * RET_QK_DIM]
        vh = v_ref[:, h * RET_V_DIM:(h + 1) * RET_V_DIM]
        scores = _nt_dot(qh, kh) * din_ref[h]
        inner = jnp.dot(scores.astype(BF16), vh, preferred_element_type=F32)
        st = state[h]
        cross = jnp.dot(qh, st.astype(BF16), preferred_element_type=F32) * dq_ref[h]
        kd = (kh.astype(F32) * dk_ref[h]).astype(BF16)
        state[h] = st * decay_c[h] + _tn_dot(kd, vh)
        ret = inner + cross
        ret = ret * lax.rsqrt(jnp.mean(ret * ret, axis=-1, keepdims=True) + EPS)
        gh = g_ref[:, h * RET_V_DIM:(h + 1) * RET_V_DIM].astype(F32)
        ret = ret * _silu(gh)
        part = jnp.dot(ret.astype(BF16), w_ref[h * RET_V_DIM:(h + 1) * RET_V_DIM, :],
                       preferred_element_type=F32)
        acc = part if acc is None else acc + part
    o_ref[...] = acc


def _retention_tables(chunk):
    lg = np.log(1.0 - 2.0 ** (-5.0 - np.arange(RET_HEADS, dtype=np.float64)))
    idx = np.arange(chunk, dtype=np.float64)
    rel = idx[:, None] - idx[None, :]
    causal = rel >= 0
    din = np.where(causal[None], np.exp(np.where(causal, rel, 0.0)[None] * lg[:, None, None]), 0.0)
    dq = np.exp((idx + 1.0)[None, :, None] * lg[:, None, None])
    dk = np.exp((chunk - 1.0 - idx)[None, :, None] * lg[:, None, None])
    dc = tuple(float(v) for v in np.exp(chunk * lg))
    return (jnp.asarray(din, F32), jnp.asarray(dq, F32), jnp.asarray(dk, F32), dc)


def _retention_call(proj, w_ret, bsz, seq, chunk):
    t = proj.shape[0]
    d = w_ret.shape[1]
    nc = seq // chunk
    qk_w = RET_HEADS * RET_QK_DIM
    v_w = RET_HEADS * RET_V_DIM
    din, dq, dk, dc = _retention_tables(chunk)
    kern = functools.partial(_retention_kernel, decay_c=dc)
    row = lambda b, c: b * nc + c
    return pl.pallas_call(
        kern,
        out_shape=jax.ShapeDtypeStruct((t, d), F32),
        grid=(bsz, nc),
        in_specs=[
            pl.BlockSpec((chunk, qk_w), lambda b, c: (row(b, c), 0)),
            pl.BlockSpec((chunk, qk_w), lambda b, c: (row(b, c), 1)),
            pl.BlockSpec((chunk, v_w), lambda b, c: (row(b, c), 1)),
            pl.BlockSpec((chunk, v_w), lambda b, c: (row(b, c), 2)),
            pl.BlockSpec((RET_HEADS, chunk, chunk), lambda b, c: (0, 0, 0)),
            pl.BlockSpec((RET_HEADS, chunk, 1), lambda b, c: (0, 0, 0)),
            pl.BlockSpec((RET_HEADS, chunk, 1), lambda b, c: (0, 0, 0)),
            pl.BlockSpec((v_w, d), lambda b, c: (0, 0)),
        ],
        out_specs=pl.BlockSpec((chunk, d), lambda b, c: (row(b, c), 0)),
        scratch_shapes=[pltpu.VMEM((RET_HEADS, RET_QK_DIM, RET_V_DIM), F32)],
        compiler_params=_params(("arbitrary", "arbitrary")),
        name="retention",
    )(proj, proj, proj, proj, din, dq, dk, w_ret)


def _ssd_kernel(z_ref, xbc_ref, dt_ref, dtb_ref, alog_ref, dsk_ref, nw_ref,
                tril_ref, exp_ref, w_ref, o_ref, state, yn_s, *, chunk, d_inner):
    c = pl.program_id(1)
    heads_per_group = d_inner // SSM_HEAD_DIM // SSM_GROUPS
    gw = heads_per_group * SSM_HEAD_DIM
    assert SSM_HEAD_DIM * 2 == LANES and gw == 2 * LANES

    @pl.when(c == 0)
    def _():
        state[...] = jnp.zeros_like(state)

    dt = jax.nn.softplus(dt_ref[...] + dtb_ref[...])
    a = -jnp.exp(alog_ref[...])
    adt = dt * a
    acs = jnp.dot(tril_ref[...], adt, preferred_element_type=F32, precision=HIGHEST)
    acs_t = acs.T
    dt_x = jnp.dot(dt.astype(BF16), exp_ref[...], preferred_element_type=F32)
    li = lax.broadcasted_iota(jnp.int32, (chunk, chunk), 0)
    si = lax.broadcasted_iota(jnp.int32, (chunk, chunk), 1)
    causal = li >= si
    low_half = si < SSM_HEAD_DIM
    lane_g = lax.broadcasted_iota(jnp.int32, (chunk, gw), 1)

    b_off = d_inner
    c_off = d_inner + SSM_GROUPS * SSM_STATE
    for g in range(SSM_GROUPS):
        bm = xbc_ref[:, b_off + g * SSM_STATE: b_off + (g + 1) * SSM_STATE]
        cm = xbc_ref[:, c_off + g * SSM_STATE: c_off + (g + 1) * SSM_STATE]
        xs_g = xbc_ref[:, g * gw:(g + 1) * gw].astype(F32)
        xdt_g = xs_g * dt_x[:, g * gw:(g + 1) * gw]
        cb = _nt_dot(cm, bm)
        cols, ms, xm = [], [], []
        for jh in range(heads_per_group):
            h = g * heads_per_group + jh
            col = jnp.broadcast_to(acs[:, h:h + 1], (chunk, chunk))
            seg = jnp.exp(jnp.where(causal, col - acs_t[h:h + 1, :], -jnp.inf))
            cols.append(col)
            ms.append((cb * seg).astype(BF16))
            in_head = (lane_g >= jh * SSM_HEAD_DIM) & (lane_g < (jh + 1) * SSM_HEAD_DIM)
            xm.append(jnp.where(in_head, xdt_g, 0.0).astype(BF16))
        y_diag = jnp.dot(jnp.concatenate(ms, axis=-1), jnp.concatenate(xm, axis=0),
                         preferred_element_type=F32)
        a_x = jnp.concatenate([jnp.where(low_half, cols[0], cols[1]),
                               jnp.where(low_half, cols[2], cols[3])], axis=-1)
        e_acs_x = jnp.exp(a_x)
        a_last_x = a_x[chunk - 1:chunk, :]
        st = state[g]
        y_off = jnp.dot(cm, st.astype(BF16), preferred_element_type=F32) * e_acs_x
        xdec = (xdt_g * jnp.exp(a_last_x - a_x)).astype(BF16)
        state[g] = st * e_acs_x[chunk - 1:chunk, :] + _tn_dot(bm, xdec)
        y = y_diag + y_off + dsk_ref[:, g * gw:(g + 1) * gw] * xs_g
        yz = y * _silu(z_ref[:, g * gw:(g + 1) * gw].astype(F32))
        yn = yz * lax.rsqrt(jnp.mean(yz * yz, axis=-1, keepdims=True) + EPS) * nw_ref[:, g * gw:(g + 1) * gw]
        yn_s[:, g * gw:(g + 1) * gw] = yn.astype(BF16)
    o_ref[...] = jnp.dot(yn_s[...], w_ref[...], preferred_element_type=F32)


def _ssd_call(proj, dt_raw, dt_bias, a_log, d_skip, ssm_norm, w_ssm, bsz, seq, chunk):
    t = proj.shape[0]
    d_inner, d = w_ssm.shape
    conv_dim = d_inner + 2 * SSM_GROUPS * SSM_STATE
    n_heads = d_inner // SSM_HEAD_DIM
    nc = seq // chunk
    gw = d_inner // SSM_GROUPS
    pad_h = lambda v: jnp.pad(v.astype(F32), (0, LANES - n_heads)).reshape(1, LANES)
    assert chunk == LANES, "the per-head decay tiles are built lane-for-lane against the chunk"
    tril = jnp.asarray(np.tril(np.ones((chunk, chunk), np.float32)))
    expand = np.zeros((LANES, d_inner), np.float32)
    for h in range(n_heads):
        expand[h, h * SSM_HEAD_DIM:(h + 1) * SSM_HEAD_DIM] = 1.0
    expand = jnp.asarray(expand, BF16)
    kern = functools.partial(_ssd_kernel, chunk=chunk, d_inner=d_inner)
    row = lambda b, c: b * nc + c
    z_blk = (2 * RET_HEADS * RET_QK_DIM + 2 * RET_HEADS * RET_V_DIM) // d_inner
    xbc_blk = (2 * RET_HEADS * RET_QK_DIM + 2 * RET_HEADS * RET_V_DIM + d_inner) // conv_dim
    full = lambda shape: pl.BlockSpec(shape, lambda b, c: (0,) * len(shape))
    return pl.pallas_call(
        kern,
        out_shape=jax.ShapeDtypeStruct((t, d), F32),
        grid=(bsz, nc),
        in_specs=[
            pl.BlockSpec((chunk, d_inner), lambda b, c: (row(b, c), z_blk)),
            pl.BlockSpec((chunk, conv_dim), lambda b, c: (row(b, c), xbc_blk)),
            pl.BlockSpec((chunk, LANES), lambda b, c: (row(b, c), 0)),
            full((1, LANES)), full((1, LANES)),
            full((1, d_inner)), full((1, d_inner)), full((chunk, chunk)), full((LANES, d_inner)),
            full((d_inner, d)),
        ],
        out_specs=pl.BlockSpec((chunk, d), lambda b, c: (row(b, c), 0)),
        scratch_shapes=[pltpu.VMEM((SSM_GROUPS, SSM_STATE, gw), F32),
                        pltpu.VMEM((chunk, d_inner), BF16)],
        compiler_params=_params(("arbitrary", "arbitrary")),
        name="ssd",
    )(proj, proj, dt_raw, pad_h(dt_bias), pad_h(a_log),
      jnp.repeat(d_skip.astype(F32), SSM_HEAD_DIM).reshape(1, d_inner), ssm_norm.reshape(1, d_inner),
      tril, expand, w_ssm)


def _merge_kernel(ya_ref, yb_ref, ga_ref, gb_ref, x_ref, gm_ref, scf_ref, shf_ref, nw_ref, wo_ref,
                  wr_ref, br_ref, tri_ref,
                  x1_ref, h2_ref, idx_ref, rank_ref, prow_ref, cnt_ref, cnt_s, *, tm):
    i = pl.program_id(0)

    @pl.when(i == 0)
    def _():
        cnt_s[...] = jnp.zeros_like(cnt_s)

    merged = (jax.nn.sigmoid(ga_ref[...].astype(F32)) * ya_ref[...]
              + jax.nn.sigmoid(gb_ref[...].astype(F32)) * yb_ref[...])
    mo = jnp.dot(merged.astype(BF16), wo_ref[...], preferred_element_type=F32)
    x1 = x_ref[...] + gm_ref[0] * mo
    x1_ref[...] = x1
    ms = jnp.mean(x1 * x1, axis=-1, keepdims=True)
    h2 = x1 * lax.rsqrt(ms + EPS) * nw_ref[...] * (1.0 + scf_ref[0]) + shf_ref[0]
    _store_slabs(h2_ref, h2, tm)

    lg = _nt_dot(wr_ref[...], h2, precision=HIGHEST) + br_ref[...]
    sub = lax.broadcasted_iota(jnp.int32, lg.shape, 0)
    work = lg
    vals, idxs, sels = [], [], []
    for _ in range(TOP_K):
        m = jnp.max(work, axis=0, keepdims=True)
        ik = jnp.min(jnp.where(work == m, sub, N_EXPERTS), axis=0, keepdims=True)
        sel = sub == ik
        vals.append(m)
        idxs.append(ik)
        sels.append(sel)
        work = jnp.where(sel, -jnp.inf, work)
    exps = [jnp.exp(v - vals[0]) for v in vals]
    denom = exps[0]
    for e in exps[1:]:
        denom = denom + e
    probs = [e / denom for e in exps]

    base = cnt_s[:, 0:1]
    ranks = []
    for k in range(TOP_K):
        mk = jnp.where(sels[k], 1.0, 0.0)
        pre = jnp.dot(mk.astype(BF16), tri_ref[...], preferred_element_type=F32)
        ranks.append(jnp.sum(jnp.where(sels[k], pre + base, 0.0), axis=0, keepdims=True))
        base = base + jnp.sum(mk, axis=1, keepdims=True)
    cnt_s[...] = jnp.broadcast_to(base, cnt_s.shape)
    cnt_ref[...] = cnt_s[...].astype(jnp.int32)

    zi = jnp.zeros((SUBLANES - TOP_K, tm), jnp.int32)
    idx_ref[0] = jnp.concatenate(idxs + [zi], axis=0)
    rank_ref[0] = jnp.concatenate([r.astype(jnp.int32) for r in ranks] + [zi], axis=0)
    pt = jnp.concatenate(probs + [jnp.zeros((LANES - TOP_K, tm), F32)], axis=0)
    prow_ref[...] = pt.T


def _merge_call(ya, yb, proj, x2, row0, mod3, norm_w, w_out, w_router_t, b_router, seq, tm):
    t, d = ya.shape
    nt = t // tm
    tiles_per_seq = seq // tm
    off = row0 // tm
    ga_blk = proj.shape[1] // d - 2
    tri = jnp.asarray(np.triu(np.ones((tm, tm), np.float32), 1), BF16)
    kern = functools.partial(_merge_kernel, tm=tm)
    modspec = lambda m: pl.BlockSpec((1, 1, d), lambda i: (((i + off) // tiles_per_seq) * N_MOD + m, 0, 0))
    return pl.pallas_call(
        kern,
        out_shape=(jax.ShapeDtypeStruct((t, d), F32), jax.ShapeDtypeStruct((t * SLAB_ROWS, LANES), jnp.uint32),
                   jax.ShapeDtypeStruct((nt, SUBLANES, tm), jnp.int32),
                   jax.ShapeDtypeStruct((nt, SUBLANES, tm), jnp.int32),
                   jax.ShapeDtypeStruct((t, LANES), F32),
                   jax.ShapeDtypeStruct((N_EXPERTS, LANES), jnp.int32)),
        grid=(nt,),
        in_specs=[
            pl.BlockSpec((tm, d), lambda i: (i, 0)),
            pl.BlockSpec((tm, d), lambda i: (i, 0)),
            pl.BlockSpec((tm, d), lambda i: (i, ga_blk)),
            pl.BlockSpec((tm, d), lambda i: (i, ga_blk + 1)),
            pl.BlockSpec((tm, d), lambda i: (i + off, 0)),
            modspec(2), modspec(4), modspec(3),
            pl.BlockSpec((1, d), lambda i: (0, 0)),
            pl.BlockSpec((d, d), lambda i: (0, 0)),
            pl.BlockSpec((N_EXPERTS, d), lambda i: (0, 0)),
            pl.BlockSpec((N_EXPERTS, 1), lambda i: (0, 0)),
            pl.BlockSpec((tm, tm), lambda i: (0, 0)),
        ],
        out_specs=(pl.BlockSpec((tm, d), lambda i: (i, 0)),
                   pl.BlockSpec((tm * SLAB_ROWS, LANES), lambda i: (i, 0)),
                   pl.BlockSpec((1, SUBLANES, tm), lambda i: (i, 0, 0)),
                   pl.BlockSpec((1, SUBLANES, tm), lambda i: (i, 0, 0)),
                   pl.BlockSpec((tm, LANES), lambda i: (i, 0)),
                   pl.BlockSpec((N_EXPERTS, LANES), lambda i: (0, 0))),
        scratch_shapes=[pltpu.VMEM((N_EXPERTS, LANES), F32)],
        compiler_params=_params(("arbitrary",)),
        name="merge",
    )(ya, yb, proj, proj, x2, mod3, mod3, mod3, norm_w, w_out, w_router_t,
      b_router.reshape(N_EXPERTS, 1), tri)


def _sc_mesh():
    return plsc.VectorSubcoreMesh(core_axis_name="c", subcore_axis_name="s")


def _sc_worker():
    return lax.axis_index("s") * SC_CORES + lax.axis_index("c")


def _sc_scatter_rows(rows, dest, n_out):
    t = rows.shape[0]
    n_k = dest.shape[0]
    g = SC_GROUP
    n_w = SC_CORES * SC_SUBCORES
    assert t % (n_w * g) == 0
    cpw = t // (n_w * g)
    dest_w = dest.reshape(n_k, n_w, cpw, g).transpose(1, 0, 2, 3)

    @functools.partial(
        pl.kernel, mesh=_sc_mesh(),
        out_type=jax.ShapeDtypeStruct((n_out,) + rows.shape[1:], rows.dtype),
        scratch_types=[pltpu.VMEM((n_k, cpw, g), jnp.int32),
                       pltpu.VMEM((g,) + rows.shape[1:], rows.dtype),
                       pltpu.SemaphoreType.DMA],
    )
    def scatter(rows_hbm, dest_hbm, out_hbm, idx_v, rows_v, sem):
        wid = _sc_worker()
        pltpu.sync_copy(dest_hbm.at[wid], idx_v)

        @pl.loop(0, cpw)
        def _(cc):
            r0 = pl.multiple_of((wid * cpw + cc) * g, g)
            pltpu.sync_copy(rows_hbm.at[pl.ds(r0, g)], rows_v)
            copies = [pltpu.async_copy(rows_v, out_hbm.at[idx_v.at[k, cc]], sem) for k in range(n_k)]
            for cp in copies:
                cp.wait()

    return scatter(rows, dest_w)


def _sc_gather_rows(table, idx):
    m = idx.shape[0]
    g = SC_GROUP
    n_w = SC_CORES * SC_SUBCORES
    assert m % (n_w * g) == 0
    per_w = m // n_w

    @functools.partial(
        pl.kernel, mesh=_sc_mesh(),
        out_type=jax.ShapeDtypeStruct((m,) + table.shape[1:], table.dtype),
        scratch_types=[pltpu.VMEM((per_w,), jnp.int32),
                       pltpu.VMEM((g,) + table.shape[1:], table.dtype),
                       pltpu.SemaphoreType.DMA],
    )
    def gather(table_hbm, idx_hbm, out_hbm, idx_v, rows_v, sem):
        base = _sc_worker() * per_w
        pltpu.sync_copy(idx_hbm.at[pl.ds(base, per_w)], idx_v)

        @pl.loop(0, per_w // g)
        def _(cc):
            off = pl.multiple_of(cc * g, g)
            pltpu.async_copy(table_hbm.at[idx_v.at[pl.ds(off, g)]], rows_v, sem).wait()
            pltpu.sync_copy(rows_v, out_hbm.at[pl.ds(base + off, g)])

    return gather(table, idx)


def _ffn_kernel(be_ref, br_ref, bv_ref, x_ref, wgu_ref, bgu_ref, wd_ref, bd_ref, o_ref, wgu_s, wd_s,
                *, bm, d_ff):
    i = pl.program_id(0)

    @pl.when((i == 0) | (be_ref[i] != be_ref[jnp.maximum(i - 1, 0)]))
    def _():
        wgu_s[...] = wgu_ref[0].astype(BF16)
        wd_s[...] = wd_ref[0].astype(BF16)

    @pl.when(bv_ref[i] == 1)
    def _():
        x = jnp.concatenate(_load_slabs(x_ref, bm), axis=-1).astype(BF16)
        gu = jnp.dot(x, wgu_s[...], preferred_element_type=F32) + bgu_ref[0]
        gate = jnp.minimum(gu[:, :d_ff], SWIGLU_LIMIT)
        up = jnp.clip(gu[:, d_ff:], -SWIGLU_LIMIT, SWIGLU_LIMIT)
        act = gate * jax.nn.sigmoid(SWIGLU_ALPHA * gate) * (up + 1.0)
        y = jnp.dot(act.astype(BF16), wd_s[...], preferred_element_type=F32) + bd_ref[0]
        _store_slabs(o_ref, y, bm)

    @pl.when(bv_ref[i] == 0)
    def _():
        o_ref[...] = jnp.zeros_like(o_ref)


def _ffn_call(blk_e, blk_row, blk_valid, xs, w_gu, b_gu, w_d, b_d, bm):
    n_e, d, f2 = w_gu.shape
    d_ff = f2 // 2
    nb = blk_e.shape[0]
    kern = functools.partial(_ffn_kernel, bm=bm, d_ff=d_ff)
    gs = pltpu.PrefetchScalarGridSpec(
        num_scalar_prefetch=3,
        grid=(nb,),
        in_specs=[pl.BlockSpec((bm * SLAB_ROWS, LANES), lambda i, be, br, bv: (br[i], 0)),
                  pl.BlockSpec((1, d, f2), lambda i, be, br, bv: (be[i], 0, 0)),
                  pl.BlockSpec((1, 1, f2), lambda i, be, br, bv: (be[i], 0, 0)),
                  pl.BlockSpec((1, d_ff, d), lambda i, be, br, bv: (be[i], 0, 0)),
                  pl.BlockSpec((1, 1, d), lambda i, be, br, bv: (be[i], 0, 0))],
        out_specs=pl.BlockSpec((bm * SLAB_ROWS, LANES), lambda i, be, br, bv: (i, 0)),
        scratch_shapes=[pltpu.VMEM((d, f2), BF16), pltpu.VMEM((d_ff, d), BF16)],
    )
    return pl.pallas_call(
        kern,
        out_shape=jax.ShapeDtypeStruct((nb * bm * SLAB_ROWS, LANES), jnp.uint32),
        grid_spec=gs,
        compiler_params=_params(("arbitrary",)),
        name="ffn",
    )(blk_e, blk_row, blk_valid, xs, w_gu, b_gu.reshape(n_e, 1, f2), w_d, b_d.reshape(n_e, 1, d))


def _combine_kernel(y0_ref, y1_ref, y2_ref, y3_ref, prow_ref, x1_ref, gf_ref, nw_ref, *rest, tm):
    o_ref = rest[-1]
    p = prow_ref[...]
    pieces = [_load_slabs(y_ref, tm) for y_ref in (y0_ref, y1_ref, y2_ref, y3_ref)]
    for s in range(len(pieces[0])):
        moe = None
        for k in range(TOP_K):
            piece = pieces[k][s] * p[:, k:k + 1]
            moe = piece if moe is None else moe + piece
        sl = slice(s * LANES, (s + 1) * LANES)
        o_ref[:, sl] = x1_ref[:, sl] + gf_ref[0][:, sl] * moe
    xo = o_ref[...]
    o_ref[...] = xo * lax.rsqrt(jnp.mean(xo * xo, axis=-1, keepdims=True) + EPS) * nw_ref[...]


def _combine_call(ytok, prow, x1, mod3, norm_final, seq, tm, row0, t_total, out_prev):
    t, d = x1.shape
    nt = t // tm
    tiles_per_seq = seq // tm
    off = row0 // tm
    kern = functools.partial(_combine_kernel, tm=tm)
    yspec = lambda k: pl.BlockSpec((tm * SLAB_ROWS, LANES), lambda i: (k * nt + i, 0))
    in_specs = [yspec(0), yspec(1), yspec(2), yspec(3),
                pl.BlockSpec((tm, LANES), lambda i: (i, 0)),
                pl.BlockSpec((tm, d), lambda i: (i, 0)),
                pl.BlockSpec((1, 1, d), lambda i: (((i + off) // tiles_per_seq) * N_MOD + 5, 0, 0)),
                pl.BlockSpec((1, d), lambda i: (0, 0))]
    args = [ytok, ytok, ytok, ytok, prow, x1, mod3, norm_final]
    aliases = {}
    if out_prev is not None:
        in_specs.append(pl.BlockSpec(memory_space=pl.ANY))
        aliases = {len(args): 0}
        args.append(out_prev)
    return pl.pallas_call(
        kern,
        out_shape=jax.ShapeDtypeStruct((t_total, d), F32),
        grid=(nt,),
        in_specs=in_specs,
        out_specs=pl.BlockSpec((tm, d), lambda i: (i + off, 0)),
        input_output_aliases=aliases,
        compiler_params=_params(("arbitrary",)),
        name="combine",
    )(*args)


def _plan(seq):
    def fit(pref):
        tm = min(pref, seq)
        assert seq % tm == 0
        return tm
    return dict(tm_in=fit(1024), tm_merge=fit(512), tm_moe=fit(512),
                ret_chunk=fit(RET_CHUNK), ssm_chunk=fit(SSM_CHUNK))


def _layer(x2, mod3, bsz, seq, norm_mix, norm_ffn, w_in, conv_w, conv_b, dt_bias, a_log, d_skip, ssm_norm,
           w_ret_out, w_ssm_out, w_out, w_router, b_router, w_gate_up, b_gate_up, w_down, b_down,
           norm_final):
    t, d = x2.shape
    plan = _plan(seq)
    qk_w = RET_HEADS * RET_QK_DIM
    v_w = RET_HEADS * RET_V_DIM
    d_inner = w_ssm_out.shape[0]
    conv_dim = conv_w.shape[1]
    n_heads = d_inner // SSM_HEAD_DIM
    dt_off = 2 * qk_w + 2 * v_w + d_inner + conv_dim

    w_main = jnp.concatenate([w_in[:, :dt_off], w_in[:, dt_off + n_heads:]], axis=1).astype(BF16)
    w_dt = jnp.pad(w_in[:, dt_off:dt_off + n_heads], ((0, 0), (0, LANES - n_heads)))
    w_dt_hi = w_dt.astype(BF16)
    w_dt = jnp.concatenate([w_dt_hi, (w_dt - w_dt_hi.astype(F32)).astype(BF16)], axis=1)
    half = RET_QK_DIM // 2
    inv_freq = ROPE_BASE ** (-jnp.arange(half, dtype=F32) / half)
    ang = jnp.arange(seq, dtype=F32)[:, None] * inv_freq[None, :]
    cos, sin = jnp.cos(ang), jnp.sin(ang)

    w_ret_b, w_ssm_b, w_out_b, w_router_t = (w_ret_out.astype(BF16), w_ssm_out.astype(BF16),
                                             w_out.astype(BF16), w_router.T)
    bm = FFN_BLOCK
    slab = (SLAB_ROWS, LANES)

    def mixer(row0, tg, bg):
        proj, dt_raw = _inproj_call(x2, row0, tg, norm_mix.reshape(1, d), mod3, cos, sin, w_main, w_dt,
                                    conv_w, conv_b, 2 * qk_w + 2 * v_w + d_inner, seq, plan["tm_in"], 2 * qk_w)
        ya = _retention_call(proj, w_ret_b, bg, seq, plan["ret_chunk"])
        yb = _ssd_call(proj, dt_raw, dt_bias, a_log, d_skip, ssm_norm, w_ssm_b, bg, seq, plan["ssm_chunk"])
        x1, h2, idx, rank, prow, cnt = _merge_call(ya, yb, proj, x2, row0, mod3, norm_ffn.reshape(1, d),
                                                   w_out_b, w_router_t, b_router, seq, plan["tm_merge"])
        counts = cnt[:, 0]
        padded = ((counts + bm - 1) // bm) * bm
        pad_end = jnp.cumsum(padded)
        start_pad = pad_end - padded
        n_blocks = (tg * TOP_K) // bm + N_EXPERTS
        e_ids = jnp.arange(N_EXPERTS, dtype=jnp.int32)[:, None, None, None]
        dest = rank + jnp.sum(jnp.where(idx[None] == e_ids, start_pad[:, None, None, None], 0), axis=0)
        dest = dest[:, :TOP_K, :].transpose(1, 0, 2).reshape(TOP_K, tg).astype(jnp.int32)
        n_real = pad_end[-1] // bm
        blk_valid = (jnp.arange(n_blocks) < n_real).astype(jnp.int32)
        blk_row = jnp.minimum(jnp.arange(n_blocks), n_real - 1).astype(jnp.int32)
        blk_e = jnp.minimum(jnp.sum(pad_end[None, :] <= (blk_row * bm)[:, None], axis=1),
                            N_EXPERTS - 1).astype(jnp.int32)
        xs = _sc_scatter_rows(h2.reshape((tg,) + slab), dest, n_blocks * bm)
        return dict(x1=x1, prow=prow, dest=dest, blocks=(blk_e, blk_row, blk_valid), xs=xs, row0=row0)

    def experts(m):
        n_rows = m["xs"].shape[0]
        ys = _ffn_call(*m["blocks"], m["xs"].reshape(n_rows * SLAB_ROWS, LANES),
                       w_gate_up, b_gate_up, w_down, b_down, bm)
        return _sc_gather_rows(ys.reshape((n_rows,) + slab), m["dest"].reshape(-1))

    n_groups = N_GROUPS if bsz % N_GROUPS == 0 else 1
    bg = bsz // n_groups
    tg = bg * seq
    groups = [mixer(g * tg, tg, bg) for g in range(n_groups)]
    ytoks = [experts(m) for m in groups]
    out = None
    for y, m in zip(ytoks, groups):
        out = _combine_call(y.reshape(TOP_K * tg * SLAB_ROWS, LANES), m["prow"], m["x1"], mod3,
                            norm_final.reshape(1, d), seq, plan["tm_moe"], m["row0"], t, out)
    return out


def kernel(x, c, w_ada, b_ada, norm_mix, norm_ffn, w_in, conv_w, conv_b, dt_bias, a_log, d_skip, ssm_norm,
           w_ret_out, w_ssm_out, w_out, w_router, b_router, w_gate_up, b_gate_up, w_down, b_down, norm_final):
    bsz, seq, d = x.shape
    depth = w_ada.shape[0]
    assert depth == 1, "the final norm is fused into the single layer's last kernel"
    x2 = x.reshape(bsz * seq, d)
    l = 0
    mod = _mod_call(c, w_ada[l], b_ada[l])
    mod3 = mod.reshape(bsz * N_MOD, 1, d)
    out = _layer(x2, mod3, bsz, seq, norm_mix[l], norm_ffn[l], w_in[l], conv_w[l], conv_b[l], dt_bias[l],
                 a_log[l], d_skip[l], ssm_norm[l], w_ret_out[l], w_ssm_out[l], w_out[l], w_router[l],
                 b_router[l], w_gate_up[l], b_gate_up[l], w_down[l], b_down[l], norm_final)
    return out.reshape(bsz, seq, d)
```

```python
import functools
import math

import numpy as np
import jax
import jax.numpy as jnp
from jax import lax
from jax.experimental import pallas as pl
from jax.experimental.pallas import tpu as pltpu
from jax.experimental.pallas import tpu_sc as plsc

F32 = jnp.float32
BF16 = jnp.bfloat16
HIGHEST = lax.Precision.HIGHEST

EPS = 1e-6
N_MOD = 6
RET_HEADS = 4
RET_QK_DIM = 256
RET_V_DIM = 512
ROPE_BASE = 10000.0
SSM_HEAD_DIM = 64
SSM_GROUPS = 8
SSM_STATE = 128
SSM_CONV = 4
N_EXPERTS = 32
TOP_K = 4
SWIGLU_LIMIT = 7.0
SWIGLU_ALPHA = 1.702

LANES = 128
SUBLANES = 8
VMEM_LIMIT = 56 * 1024 * 1024

RET_CHUNK = 256
SSM_CHUNK = 128
FFN_BLOCK = 512
SC_CORES = 2
SC_SUBCORES = 16
SC_GROUP = 64
N_GROUPS = 2


def _params(sem, vmem=VMEM_LIMIT):
    return pltpu.CompilerParams(dimension_semantics=sem, vmem_limit_bytes=vmem)


def _nt_dot(a, b, **kw):
    return lax.dot_general(a, b, (((1,), (1,)), ((), ())), preferred_element_type=F32, **kw)


def _tn_dot(a, b, **kw):
    return lax.dot_general(a, b, (((0,), (0,)), ((), ())), preferred_element_type=F32, **kw)


def _silu(v):
    return v * jax.nn.sigmoid(v)


SLAB_ROWS = 4
HIGH_HALF = 0xFFFF0000


def _store_slabs(ref, vals, n):
    for s in range(SLAB_ROWS):
        lo = vals[:, s * LANES:(s + 1) * LANES].astype(BF16).astype(F32)
        hi = vals[:, (s + SLAB_ROWS) * LANES:(s + SLAB_ROWS + 1) * LANES].astype(BF16).astype(F32)
        word = (pltpu.bitcast(lo, jnp.uint32) >> 16) | (pltpu.bitcast(hi, jnp.uint32) & jnp.uint32(HIGH_HALF))
        ref[pl.ds(s, n, stride=SLAB_ROWS), :] = word


def _load_slabs(ref, n, base=0):
    lo, hi = [], []
    for s in range(SLAB_ROWS):
        word = ref[pl.ds(base + s, n, stride=SLAB_ROWS), :]
        lo.append(pltpu.bitcast(word << 16, F32))
        hi.append(pltpu.bitcast(word & jnp.uint32(HIGH_HALF), F32))
    return lo + hi


def _mod_kernel(c_ref, w_ref, b_ref, o_ref):
    cond = _silu(c_ref[...])
    o_ref[...] = jnp.dot(cond, w_ref[...], preferred_element_type=F32, precision=HIGHEST) + b_ref[...]


def _mod_call(c, w_ada, b_ada):
    bsz, d = c.shape
    n = w_ada.shape[1]
    return pl.pallas_call(
        _mod_kernel,
        out_shape=jax.ShapeDtypeStruct((bsz, n), F32),
        grid=(n // d,),
        in_specs=[pl.BlockSpec((bsz, d), lambda j: (0, 0)),
                  pl.BlockSpec((d, d), lambda j: (0, j)),
                  pl.BlockSpec((1, d), lambda j: (0, j))],
        out_specs=pl.BlockSpec((bsz, d), lambda j: (0, j)),
        compiler_params=_params(("arbitrary",)),
        name="mod",
    )(c, w_ada, b_ada.reshape(1, n))


def _inproj_kernel(x_ref, nw_ref, sc_ref, sh_ref, cos_ref, sin_ref, w_ref, wdt_ref, cw_ref, cb_ref,
                   o_ref, dt_ref, h_s, work, carry, *, conv_j0, conv_nj, silu_j, sigm_j, tiles_per_seq,
                   tm, tn, sub):
    i = pl.program_id(0)
    j = pl.program_id(1)
    n_dt = dt_ref.shape[1]

    @pl.when(j == 0)
    def _():
        xf = x_ref[...]
        ms = jnp.mean(xf * xf, axis=-1, keepdims=True)
        y = xf * lax.rsqrt(ms + EPS) * nw_ref[...]
        hm = y * (1.0 + sc_ref[0]) + sh_ref[0]
        hb = hm.astype(BF16)
        h_s[...] = hb
        h_lo = (hm - hb.astype(F32)).astype(BF16)
        d_hi = jnp.dot(hb, wdt_ref[...], preferred_element_type=F32)
        d_lo = jnp.dot(h_lo, wdt_ref[:, :n_dt], preferred_element_type=F32)
        dt_ref[...] = d_hi[:, :n_dt] + d_hi[:, n_dt:] + d_lo
        cos = cos_ref[...]
        sin = sin_ref[...]
        half = RET_QK_DIM // 2
        for p in range(tn // sub):
            acc = jnp.dot(hb, w_ref[:, p * sub:(p + 1) * sub], preferred_element_type=F32)
            for cc in range(sub // RET_QK_DIM):
                c = p * (sub // RET_QK_DIM) + cc
                a = acc[:, cc * RET_QK_DIM: cc * RET_QK_DIM + half]
                b = acc[:, cc * RET_QK_DIM + half: (cc + 1) * RET_QK_DIM]
                scale = 1.0 if c < RET_HEADS else RET_QK_DIM ** -0.5
                o_ref[:, c * RET_QK_DIM: c * RET_QK_DIM + half] = ((a * cos - b * sin) * scale).astype(BF16)
                o_ref[:, c * RET_QK_DIM + half: (c + 1) * RET_QK_DIM] = ((a * sin + b * cos) * scale).astype(BF16)

    is_conv = (j >= conv_j0) & (j < conv_j0 + conv_nj)

    @pl.when(is_conv)
    def _():
        cj = j - conv_j0
        pad = SUBLANES

        @pl.when(i % tiles_per_seq == 0)
        def _():
            carry[cj] = jnp.zeros(carry.shape[1:], F32)

        for p in range(tn // sub):
            acc = jnp.dot(h_s[...], w_ref[:, p * sub:(p + 1) * sub], preferred_element_type=F32)
            for cc in range(sub // LANES):
                c = p * (sub // LANES) + cc
                cols = slice(c * LANES, (c + 1) * LANES)
                work[c, 0:pad, :] = carry[cj, c]
                work[c, pad:pad + tm, :] = acc[:, cc * LANES:(cc + 1) * LANES]
                conv = cb_ref[:, cols] + cw_ref[SSM_CONV - 1:SSM_CONV, cols] * work[c, pad:pad + tm, :]
                for k in range(SSM_CONV - 1):
                    shift = SSM_CONV - 1 - k
                    conv = conv + cw_ref[k:k + 1, cols] * work[c, pad - shift:pad - shift + tm, :]
                carry[cj, c] = work[c, tm:tm + pad, :]
                o_ref[:, cols] = _silu(conv).astype(BF16)

    def plain(act):
        for p in range(tn // sub):
            acc = jnp.dot(h_s[...], w_ref[:, p * sub:(p + 1) * sub], preferred_element_type=F32)
            o_ref[:, p * sub:(p + 1) * sub] = act(acc).astype(BF16)

    is_silu = (j >= silu_j[0]) & (j < silu_j[1])
    is_sigm = (j >= sigm_j[0]) & (j < sigm_j[1])
    pl.when(is_silu)(lambda: plain(_silu))
    pl.when(is_sigm)(lambda: plain(jax.nn.sigmoid))
    pl.when((j != 0) & jnp.logical_not(is_conv | is_silu | is_sigm))(lambda: plain(lambda v: v))


def _inproj_call(x2, row0, t, norm_w, mod3, cos, sin, w_main, w_dt, conv_w, conv_b, conv_off, seq, tm, tn):
    d = x2.shape[1]
    n = w_main.shape[1]
    conv_dim = conv_w.shape[1]
    tiles_per_seq = seq // tm
    off = row0 // tm
    assert tn == 2 * RET_HEADS * RET_QK_DIM, "rotary epilogue expects q and k in the first column tile"
    assert conv_off % tn == 0 and conv_dim % tn == 0
    conv_j0, conv_nj = conv_off // tn, conv_dim // tn
    sub = 512
    g_off = 2 * RET_HEADS * RET_QK_DIM + RET_HEADS * RET_V_DIM
    assert g_off % tn == 0 and (conv_off - g_off) % tn == 0 and (n - conv_off - conv_dim) % tn == 0
    silu_j = (g_off // tn, conv_off // tn)
    sigm_j = ((conv_off + conv_dim) // tn, n // tn)
    kern = functools.partial(_inproj_kernel, conv_j0=conv_j0, conv_nj=conv_nj, silu_j=silu_j, sigm_j=sigm_j,
                             tiles_per_seq=tiles_per_seq, tm=tm, tn=tn, sub=sub)
    conv_idx = lambda i, j: (0, jnp.clip(j - conv_j0, 0, conv_nj - 1))
    return pl.pallas_call(
        kern,
        out_shape=(jax.ShapeDtypeStruct((t, n), BF16), jax.ShapeDtypeStruct((t, LANES), F32)),
        grid=(t // tm, n // tn),
        in_specs=[
            pl.BlockSpec((tm, d), lambda i, j: (i + off, 0)),
            pl.BlockSpec((1, d), lambda i, j: (0, 0)),
            pl.BlockSpec((1, 1, d), lambda i, j: (((i + off) // tiles_per_seq) * N_MOD + 1, 0, 0)),
            pl.BlockSpec((1, 1, d), lambda i, j: (((i + off) // tiles_per_seq) * N_MOD + 0, 0, 0)),
            pl.BlockSpec((tm, LANES), lambda i, j: (i % tiles_per_seq, 0)),
            pl.BlockSpec((tm, LANES), lambda i, j: (i % tiles_per_seq, 0)),
            pl.BlockSpec((d, tn), lambda i, j: (0, j)),
            pl.BlockSpec((d, 2 * LANES), lambda i, j: (0, 0)),
            pl.BlockSpec((SSM_CONV, tn), conv_idx),
            pl.BlockSpec((1, tn), conv_idx),
        ],
        out_specs=(pl.BlockSpec((tm, tn), lambda i, j: (i, j)),
                   pl.BlockSpec((tm, LANES), lambda i, j: (i, 0))),
        scratch_shapes=[pltpu.VMEM((tm, d), BF16),
                        pltpu.VMEM((tn // LANES, tm + SUBLANES, LANES), F32),
                        pltpu.VMEM((conv_nj, tn // LANES, SUBLANES, LANES), F32)],
        compiler_params=_params(("arbitrary", "arbitrary")),
        name="inproj",
    )(x2, norm_w, mod3, mod3, cos, sin, w_main, w_dt, conv_w, conv_b.reshape(1, conv_dim))


def _retention_kernel(q_ref, k_ref, v_ref, g_ref, din_ref, dq_ref, dk_ref, w_ref, o_ref, state,
                      *, decay_c):
    c = pl.program_id(1)

    @pl.when(c == 0)
    def _():
        state[...] = jnp.zeros_like(state)

    acc = None
    for h in range(RET_HEADS):
        qh = q_ref[:, h * RET_QK_DIM:(h + 1) * RET_QK_DIM]
        kh = k_ref[:, h * RET_QK_DIM:(h + 1) * RET_QK_DIM]
        vh = v_ref[:, h * RET_V_DIM:(h + 1) * RET_V_DIM]
        scores = _nt_dot(qh, kh) * din_ref[h]
        inner = jnp.dot(scores.astype(BF16), vh, preferred_element_type=F32)
        st = state[h]
        cross = jnp.dot(qh, st.astype(BF16), preferred_element_type=F32) * dq_ref[h]
        kd = (kh.astype(F32) * dk_ref[h]).astype(BF16)
        state[h] = st * decay_c[h] + _tn_dot(kd, vh)
        ret = inner + cross
        ret = ret * lax.rsqrt(jnp.mean(ret * ret, axis=-1, keepdims=True) + EPS)
        ret = ret * g_ref[:, h * RET_V_DIM:(h + 1) * RET_V_DIM].astype(F32)
        part = jnp.dot(ret.astype(BF16), w_ref[h * RET_V_DIM:(h + 1) * RET_V_DIM, :],
                       preferred_element_type=F32)
        acc = part if acc is None else acc + part
    o_ref[...] = acc


def _retention_tables(chunk):
    lg = np.log(1.0 - 2.0 ** (-5.0 - np.arange(RET_HEADS, dtype=np.float64)))
    idx = np.arange(chunk, dtype=np.float64)
    rel = idx[:, None] - idx[None, :]
    causal = rel >= 0
    din = np.where(causal[None], np.exp(np.where(causal, rel, 0.0)[None] * lg[:, None, None]), 0.0)
    dq = np.exp((idx + 1.0)[None, :, None] * lg[:, None, None])
    dk = np.exp((chunk - 1.0 - idx)[None, :, None] * lg[:, None, None])
    dc = tuple(float(v) for v in np.exp(chunk * lg))
    return (jnp.asarray(din, F32), jnp.asarray(dq, F32), jnp.asarray(dk, F32), dc)


def _retention_call(proj, w_ret, bsz, seq, chunk):
    t = proj.shape[0]
    d = w_ret.shape[1]
    nc = seq // chunk
    qk_w = RET_HEADS * RET_QK_DIM
    v_w = RET_HEADS * RET_V_DIM
    din, dq, dk, dc = _retention_tables(chunk)
    kern = functools.partial(_retention_kernel, decay_c=dc)
    row = lambda b, c: b * nc + c
    return pl.pallas_call(
        kern,
        out_shape=jax.ShapeDtypeStruct((t, d), F32),
        grid=(bsz, nc),
        in_specs=[
            pl.BlockSpec((chunk, qk_w), lambda b, c: (row(b, c), 0)),
            pl.BlockSpec((chunk, qk_w), lambda b, c: (row(b, c), 1)),
            pl.BlockSpec((chunk, v_w), lambda b, c: (row(b, c), 1)),
            pl.BlockSpec((chunk, v_w), lambda b, c: (row(b, c), 2)),
            pl.BlockSpec((RET_HEADS, chunk, chunk), lambda b, c: (0, 0, 0)),
            pl.BlockSpec((RET_HEADS, chunk, 1), lambda b, c: (0, 0, 0)),
            pl.BlockSpec((RET_HEADS, chunk, 1), lambda b, c: (0, 0, 0)),
            pl.BlockSpec((v_w, d), lambda b, c: (0, 0)),
        ],
        out_specs=pl.BlockSpec((chunk, d), lambda b, c: (row(b, c), 0)),
        scratch_shapes=[pltpu.VMEM((RET_HEADS, RET_QK_DIM, RET_V_DIM), F32)],
        compiler_params=_params(("arbitrary", "arbitrary")),
        name="retention",
    )(proj, proj, proj, proj, din, dq, dk, w_ret)


def _ssd_kernel(z_ref, xbc_ref, dt_ref, dtb_ref, alog_ref, dsk_ref, nw_ref,
                tril_ref, exp_ref, w_ref, o_ref, state, yn_s, *, chunk, d_inner):
    c = pl.program_id(1)
    heads_per_group = d_inner // SSM_HEAD_DIM // SSM_GROUPS
    gw = heads_per_group * SSM_HEAD_DIM
    assert SSM_HEAD_DIM * 2 == LANES and gw == 2 * LANES

    @pl.when(c == 0)
    def _():
        state[...] = jnp.zeros_like(state)

    dt = jax.nn.softplus(dt_ref[...] + dtb_ref[...])
    a = -jnp.exp(alog_ref[...])
    adt = dt * a
    acs = jnp.dot(tril_ref[...], adt, preferred_element_type=F32, precision=HIGHEST)
    acs_t = acs.T
    dt_x = jnp.dot(dt.astype(BF16), exp_ref[...], preferred_element_type=F32)
    li = lax.broadcasted_iota(jnp.int32, (chunk, chunk), 0)
    si = lax.broadcasted_iota(jnp.int32, (chunk, chunk), 1)
    causal = li >= si
    low_half = si < SSM_HEAD_DIM
    lane_g = lax.broadcasted_iota(jnp.int32, (chunk, gw), 1)

    b_off = d_inner
    c_off = d_inner + SSM_GROUPS * SSM_STATE
    for g in range(SSM_GROUPS):
        bm = xbc_ref[:, b_off + g * SSM_STATE: b_off + (g + 1) * SSM_STATE]
        cm = xbc_ref[:, c_off + g * SSM_STATE: c_off + (g + 1) * SSM_STATE]
        xs_g = xbc_ref[:, g * gw:(g + 1) * gw].astype(F32)
        xdt_g = xs_g * dt_x[:, g * gw:(g + 1) * gw]
        cb = _nt_dot(cm, bm)
        cols, ms, xm = [], [], []
        for jh in range(heads_per_group):
            h = g * heads_per_group + jh
            col = jnp.broadcast_to(acs[:, h:h + 1], (chunk, chunk))
            seg = jnp.exp(jnp.where(causal, col - acs_t[h:h + 1, :], -jnp.inf))
            cols.append(col)
            ms.append((cb * seg).astype(BF16))
            in_head = (lane_g >= jh * SSM_HEAD_DIM) & (lane_g < (jh + 1) * SSM_HEAD_DIM)
            xm.append(jnp.where(in_head, xdt_g, 0.0).astype(BF16))
        y_diag = jnp.dot(jnp.concatenate(ms, axis=-1), jnp.concatenate(xm, axis=0),
                         preferred_element_type=F32)
        a_x = jnp.concatenate([jnp.where(low_half, cols[0], cols[1]),
                               jnp.where(low_half, cols[2], cols[3])], axis=-1)
        e_acs_x = jnp.exp(a_x)
        a_last_x = a_x[chunk - 1:chunk, :]
        st = state[g]
        y_off = jnp.dot(cm, st.astype(BF16), preferred_element_type=F32) * e_acs_x
        xdec = (xdt_g * jnp.exp(a_last_x - a_x)).astype(BF16)
        state[g] = st * e_acs_x[chunk - 1:chunk, :] + _tn_dot(bm, xdec)
        y = y_diag + y_off + dsk_ref[:, g * gw:(g + 1) * gw] * xs_g
        yz = y * z_ref[:, g * gw:(g + 1) * gw].astype(F32)
        yn = yz * lax.rsqrt(jnp.mean(yz * yz, axis=-1, keepdims=True) + EPS) * nw_ref[:, g * gw:(g + 1) * gw]
        yn_s[:, g * gw:(g + 1) * gw] = yn.astype(BF16)
    o_ref[...] = jnp.dot(yn_s[...], w_ref[...], preferred_element_type=F32)


def _ssd_call(proj, dt_raw, dt_bias, a_log, d_skip, ssm_norm, w_ssm, bsz, seq, chunk):
    t = proj.shape[0]
    d_inner, d = w_ssm.shape
    conv_dim = d_inner + 2 * SSM_GROUPS * SSM_STATE
    n_heads = d_inner // SSM_HEAD_DIM
    nc = seq // chunk
    gw = d_inner // SSM_GROUPS
    pad_h = lambda v: jnp.pad(v.astype(F32), (0, LANES - n_heads)).reshape(1, LANES)
    assert chunk == LANES, "the per-head decay tiles are built lane-for-lane against the chunk"
    tril = jnp.asarray(np.tril(np.ones((chunk, chunk), np.float32)))
    expand = np.zeros((LANES, d_inner), np.float32)
    for h in range(n_heads):
        expand[h, h * SSM_HEAD_DIM:(h + 1) * SSM_HEAD_DIM] = 1.0
    expand = jnp.asarray(expand, BF16)
    kern = functools.partial(_ssd_kernel, chunk=chunk, d_inner=d_inner)
    row = lambda b, c: b * nc + c
    z_blk = (2 * RET_HEADS * RET_QK_DIM + 2 * RET_HEADS * RET_V_DIM) // d_inner
    xbc_blk = (2 * RET_HEADS * RET_QK_DIM + 2 * RET_HEADS * RET_V_DIM + d_inner) // conv_dim
    full = lambda shape: pl.BlockSpec(shape, lambda b, c: (0,) * len(shape))
    return pl.pallas_call(
        kern,
        out_shape=jax.ShapeDtypeStruct((t, d), F32),
        grid=(bsz, nc),
        in_specs=[
            pl.BlockSpec((chunk, d_inner), lambda b, c: (row(b, c), z_blk)),
            pl.BlockSpec((chunk, conv_dim), lambda b, c: (row(b, c), xbc_blk)),
            pl.BlockSpec((chunk, LANES), lambda b, c: (row(b, c), 0)),
            full((1, LANES)), full((1, LANES)),
            full((1, d_inner)), full((1, d_inner)), full((chunk, chunk)), full((LANES, d_inner)),
            full((d_inner, d)),
        ],
        out_specs=pl.BlockSpec((chunk, d), lambda b, c: (row(b, c), 0)),
        scratch_shapes=[pltpu.VMEM((SSM_GROUPS, SSM_STATE, gw), F32),
                        pltpu.VMEM((chunk, d_inner), BF16)],
        compiler_params=_params(("arbitrary", "arbitrary")),
        name="ssd",
    )(proj, proj, dt_raw, pad_h(dt_bias), pad_h(a_log),
      jnp.repeat(d_skip.astype(F32), SSM_HEAD_DIM).reshape(1, d_inner), ssm_norm.reshape(1, d_inner),
      tril, expand, w_ssm)


def _merge_kernel(ya_ref, yb_ref, ga_ref, gb_ref, x_ref, gm_ref, scf_ref, shf_ref, nw_ref, wo_ref,
                  wr_ref, br_ref, tri_ref,
                  x1_ref, h2_ref, idx_ref, rank_ref, prow_ref, cnt_ref, cnt_s, *, tm):
    i = pl.program_id(0)

    @pl.when(i == 0)
    def _():
        cnt_s[...] = jnp.zeros_like(cnt_s)

    merged = ga_ref[...].astype(F32) * ya_ref[...] + gb_ref[...].astype(F32) * yb_ref[...]
    mo = jnp.dot(merged.astype(BF16), wo_ref[...], preferred_element_type=F32)
    x1 = x_ref[...] + gm_ref[0] * mo
    x1_ref[...] = x1
    ms = jnp.mean(x1 * x1, axis=-1, keepdims=True)
    h2 = x1 * lax.rsqrt(ms + EPS) * nw_ref[...] * (1.0 + scf_ref[0]) + shf_ref[0]
    _store_slabs(h2_ref, h2, tm)

    h_hi = h2.astype(BF16)
    h_lo = (h2 - h_hi.astype(F32)).astype(BF16)
    lg2 = _nt_dot(wr_ref[...], h_hi)
    lg = lg2[:N_EXPERTS] + lg2[N_EXPERTS:] + _nt_dot(wr_ref[:N_EXPERTS, :], h_lo) + br_ref[...]
    sub = lax.broadcasted_iota(jnp.int32, lg.shape, 0)
    work = lg
    vals, idxs, sels = [], [], []
    for _ in range(TOP_K):
        m = jnp.max(work, axis=0, keepdims=True)
        ik = jnp.min(jnp.where(work == m, sub, N_EXPERTS), axis=0, keepdims=True)
        sel = sub == ik
        vals.append(m)
        idxs.append(ik)
        sels.append(sel)
        work = jnp.where(sel, -jnp.inf, work)
    exps = [jnp.exp(v - vals[0]) for v in vals]
    denom = exps[0]
    for e in exps[1:]:
        denom = denom + e
    probs = [e / denom for e in exps]

    base = cnt_s[:, 0:1]
    ranks = []
    for k in range(TOP_K):
        mk = jnp.where(sels[k], 1.0, 0.0)
        pre = jnp.dot(mk.astype(BF16), tri_ref[...], preferred_element_type=F32)
        ranks.append(jnp.sum(jnp.where(sels[k], pre + base, 0.0), axis=0, keepdims=True))
        base = base + jnp.sum(mk, axis=1, keepdims=True)
    cnt_s[...] = jnp.broadcast_to(base, cnt_s.shape)
    cnt_ref[...] = cnt_s[...].astype(jnp.int32)

    zi = jnp.zeros((SUBLANES - TOP_K, tm), jnp.int32)
    idx_ref[0] = jnp.concatenate(idxs + [zi], axis=0)
    rank_ref[0] = jnp.concatenate([r.astype(jnp.int32) for r in ranks] + [zi], axis=0)
    pt = jnp.concatenate(probs + [jnp.zeros((LANES - TOP_K, tm), F32)], axis=0)
    prow_ref[...] = pt.T


def _merge_call(ya, yb, proj, x2, row0, mod3, norm_w, w_out, w_router_t, b_router, seq, tm):
    t, d = ya.shape
    nt = t // tm
    tiles_per_seq = seq // tm
    off = row0 // tm
    ga_blk = proj.shape[1] // d - 2
    tri = jnp.asarray(np.triu(np.ones((tm, tm), np.float32), 1), BF16)
    kern = functools.partial(_merge_kernel, tm=tm)
    modspec = lambda m: pl.BlockSpec((1, 1, d), lambda i: (((i + off) // tiles_per_seq) * N_MOD + m, 0, 0))
    return pl.pallas_call(
        kern,
        out_shape=(jax.ShapeDtypeStruct((t, d), F32), jax.ShapeDtypeStruct((t * SLAB_ROWS, LANES), jnp.uint32),
                   jax.ShapeDtypeStruct((nt, SUBLANES, tm), jnp.int32),
                   jax.ShapeDtypeStruct((nt, SUBLANES, tm), jnp.int32),
                   jax.ShapeDtypeStruct((t, LANES), F32),
                   jax.ShapeDtypeStruct((N_EXPERTS, LANES), jnp.int32)),
        grid=(nt,),
        in_specs=[
            pl.BlockSpec((tm, d), lambda i: (i, 0)),
            pl.BlockSpec((tm, d), lambda i: (i, 0)),
            pl.BlockSpec((tm, d), lambda i: (i, ga_blk)),
            pl.BlockSpec((tm, d), lambda i: (i, ga_blk + 1)),
            pl.BlockSpec((tm, d), lambda i: (i + off, 0)),
            modspec(2), modspec(4), modspec(3),
            pl.BlockSpec((1, d), lambda i: (0, 0)),
            pl.BlockSpec((d, d), lambda i: (0, 0)),
            pl.BlockSpec((2 * N_EXPERTS, d), lambda i: (0, 0)),
            pl.BlockSpec((N_EXPERTS, 1), lambda i: (0, 0)),
            pl.BlockSpec((tm, tm), lambda i: (0, 0)),
        ],
        out_specs=(pl.BlockSpec((tm, d), lambda i: (i, 0)),
                   pl.BlockSpec((tm * SLAB_ROWS, LANES), lambda i: (i, 0)),
                   pl.BlockSpec((1, SUBLANES, tm), lambda i: (i, 0, 0)),
                   pl.BlockSpec((1, SUBLANES, tm), lambda i: (i, 0, 0)),
                   pl.BlockSpec((tm, LANES), lambda i: (i, 0)),
                   pl.BlockSpec((N_EXPERTS, LANES), lambda i: (0, 0))),
        scratch_shapes=[pltpu.VMEM((N_EXPERTS, LANES), F32)],
        compiler_params=_params(("arbitrary",)),
        name="merge",
    )(ya, yb, proj, proj, x2, mod3, mod3, mod3, norm_w, w_out, w_router_t,
      b_router.reshape(N_EXPERTS, 1), tri)


def _sc_mesh():
    return plsc.VectorSubcoreMesh(core_axis_name="c", subcore_axis_name="s")


def _sc_worker():
    return lax.axis_index("s") * SC_CORES + lax.axis_index("c")


def _sc_scatter_rows(rows, dest, n_out):
    t = rows.shape[0]
    n_k = dest.shape[0]
    g = SC_GROUP
    n_w = SC_CORES * SC_SUBCORES
    assert t % (n_w * g) == 0
    cpw = t // (n_w * g)
    dest_w = dest.reshape(n_k, n_w, cpw, g).transpose(1, 0, 2, 3)

    @functools.partial(
        pl.kernel, mesh=_sc_mesh(),
        out_type=jax.ShapeDtypeStruct((n_out,) + rows.shape[1:], rows.dtype),
        scratch_types=[pltpu.VMEM((n_k, cpw, g), jnp.int32),
                       pltpu.VMEM((g,) + rows.shape[1:], rows.dtype),
                       pltpu.SemaphoreType.DMA],
    )
    def scatter(rows_hbm, dest_hbm, out_hbm, idx_v, rows_v, sem):
        wid = _sc_worker()
        pltpu.sync_copy(dest_hbm.at[wid], idx_v)

        @pl.loop(0, cpw)
        def _(cc):
            r0 = pl.multiple_of((wid * cpw + cc) * g, g)
            pltpu.sync_copy(rows_hbm.at[pl.ds(r0, g)], rows_v)
            copies = [pltpu.async_copy(rows_v, out_hbm.at[idx_v.at[k, cc]], sem) for k in range(n_k)]
            for cp in copies:
                cp.wait()

    return scatter(rows, dest_w)


def _sc_gather_rows(table, idx):
    m = idx.shape[0]
    g = SC_GROUP
    n_w = SC_CORES * SC_SUBCORES
    assert m % (n_w * g) == 0
    per_w = m // n_w

    @functools.partial(
        pl.kernel, mesh=_sc_mesh(),
        out_type=jax.ShapeDtypeStruct((m,) + table.shape[1:], table.dtype),
        scratch_types=[pltpu.VMEM((per_w,), jnp.int32),
                       pltpu.VMEM((g,) + table.shape[1:], table.dtype),
                       pltpu.SemaphoreType.DMA],
    )
    def gather(table_hbm, idx_hbm, out_hbm, idx_v, rows_v, sem):
        base = _sc_worker() * per_w
        pltpu.sync_copy(idx_hbm.at[pl.ds(base, per_w)], idx_v)

        @pl.loop(0, per_w // g)
        def _(cc):
            off = pl.multiple_of(cc * g, g)
            pltpu.async_copy(table_hbm.at[idx_v.at[pl.ds(off, g)]], rows_v, sem).wait()
            pltpu.sync_copy(rows_v, out_hbm.at[pl.ds(base + off, g)])

    return gather(table, idx)


def _ffn_kernel(be_ref, br_ref, bv_ref, x_ref, wgu_ref, bgu_ref, wd_ref, bd_ref, o_ref, wgu_s, wd_s,
                *, bm, d_ff):
    i = pl.program_id(0)

    @pl.when((i == 0) | (be_ref[i] != be_ref[jnp.maximum(i - 1, 0)]))
    def _():
        wgu_s[...] = wgu_ref[0].astype(BF16)
        wd_s[...] = wd_ref[0].astype(BF16)

    @pl.when(bv_ref[i] == 1)
    def _():
        x = jnp.concatenate(_load_slabs(x_ref, bm), axis=-1).astype(BF16)
        gu = jnp.dot(x, wgu_s[...], preferred_element_type=F32) + bgu_ref[0]
        gate = jnp.minimum(gu[:, :d_ff], SWIGLU_LIMIT)
        up = jnp.clip(gu[:, d_ff:], -SWIGLU_LIMIT, SWIGLU_LIMIT)
        act = gate * jax.nn.sigmoid(SWIGLU_ALPHA * gate) * (up + 1.0)
        y = jnp.dot(act.astype(BF16), wd_s[...], preferred_element_type=F32) + bd_ref[0]
        _store_slabs(o_ref, y, bm)

    @pl.when(bv_ref[i] == 0)
    def _():
        o_ref[...] = jnp.zeros_like(o_ref)


def _ffn_call(blk_e, blk_row, blk_valid, xs, w_gu, b_gu, w_d, b_d, bm):
    n_e, d, f2 = w_gu.shape
    d_ff = f2 // 2
    nb = blk_e.shape[0]
    kern = functools.partial(_ffn_kernel, bm=bm, d_ff=d_ff)
    gs = pltpu.PrefetchScalarGridSpec(
        num_scalar_prefetch=3,
        grid=(nb,),
        in_specs=[pl.BlockSpec((bm * SLAB_ROWS, LANES), lambda i, be, br, bv: (br[i], 0)),
                  pl.BlockSpec((1, d, f2), lambda i, be, br, bv: (be[i], 0, 0)),
                  pl.BlockSpec((1, 1, f2), lambda i, be, br, bv: (be[i], 0, 0)),
                  pl.BlockSpec((1, d_ff, d), lambda i, be, br, bv: (be[i], 0, 0)),
                  pl.BlockSpec((1, 1, d), lambda i, be, br, bv: (be[i], 0, 0))],
        out_specs=pl.BlockSpec((bm * SLAB_ROWS, LANES), lambda i, be, br, bv: (i, 0)),
        scratch_shapes=[pltpu.VMEM((d, f2), BF16), pltpu.VMEM((d_ff, d), BF16)],
    )
    return pl.pallas_call(
        kern,
        out_shape=jax.ShapeDtypeStruct((nb * bm * SLAB_ROWS, LANES), jnp.uint32),
        grid_spec=gs,
        compiler_params=_params(("arbitrary",)),
        name="ffn",
    )(blk_e, blk_row, blk_valid, xs, w_gu, b_gu.reshape(n_e, 1, f2), w_d, b_d.reshape(n_e, 1, d))


def _combine_kernel(y0_ref, y1_ref, y2_ref, y3_ref, prow_ref, x1_ref, gf_ref, nw_ref, *rest, tm):
    o_ref = rest[-1]
    p = prow_ref[...]
    pieces = [_load_slabs(y_ref, tm) for y_ref in (y0_ref, y1_ref, y2_ref, y3_ref)]
    for s in range(len(pieces[0])):
        moe = None
        for k in range(TOP_K):
            piece = pieces[k][s] * p[:, k:k + 1]
            moe = piece if moe is None else moe + piece
        sl = slice(s * LANES, (s + 1) * LANES)
        o_ref[:, sl] = x1_ref[:, sl] + gf_ref[0][:, sl] * moe
    xo = o_ref[...]
    o_ref[...] = xo * lax.rsqrt(jnp.mean(xo * xo, axis=-1, keepdims=True) + EPS) * nw_ref[...]


def _combine_call(ytok, prow, x1, mod3, norm_final, seq, tm, row0, t_total, out_prev):
    t, d = x1.shape
    nt = t // tm
    tiles_per_seq = seq // tm
    off = row0 // tm
    kern = functools.partial(_combine_kernel, tm=tm)
    yspec = lambda k: pl.BlockSpec((tm * SLAB_ROWS, LANES), lambda i: (k * nt + i, 0))
    in_specs = [yspec(0), yspec(1), yspec(2), yspec(3),
                pl.BlockSpec((tm, LANES), lambda i: (i, 0)),
                pl.BlockSpec((tm, d), lambda i: (i, 0)),
                pl.BlockSpec((1, 1, d), lambda i: (((i + off) // tiles_per_seq) * N_MOD + 5, 0, 0)),
                pl.BlockSpec((1, d), lambda i: (0, 0))]
    args = [ytok, ytok, ytok, ytok, prow, x1, mod3, norm_final]
    aliases = {}
    if out_prev is not None:
        in_specs.append(pl.BlockSpec(memory_space=pl.ANY))
        aliases = {len(args): 0}
        args.append(out_prev)
    return pl.pallas_call(
        kern,
        out_shape=jax.ShapeDtypeStruct((t_total, d), F32),
        grid=(nt,),
        in_specs=in_specs,
        out_specs=pl.BlockSpec((tm, d), lambda i: (i + off, 0)),
        input_output_aliases=aliases,
        compiler_params=_params(("arbitrary",)),
        name="combine",
    )(*args)


def _plan(seq):
    def fit(pref):
        tm = min(pref, seq)
        assert seq % tm == 0
        return tm
    return dict(tm_in=fit(1024), tm_merge=fit(512), tm_moe=fit(512),
                ret_chunk=fit(RET_CHUNK), ssm_chunk=fit(SSM_CHUNK))


def _layer(x2, mod3, bsz, seq, norm_mix, norm_ffn, w_in, conv_w, conv_b, dt_bias, a_log, d_skip, ssm_norm,
           w_ret_out, w_ssm_out, w_out, w_router, b_router, w_gate_up, b_gate_up, w_down, b_down,
           norm_final):
    t, d = x2.shape
    plan = _plan(seq)
    qk_w = RET_HEADS * RET_QK_DIM
    v_w = RET_HEADS * RET_V_DIM
    d_inner = w_ssm_out.shape[0]
    conv_dim = conv_w.shape[1]
    n_heads = d_inner // SSM_HEAD_DIM
    dt_off = 2 * qk_w + 2 * v_w + d_inner + conv_dim

    w_main = jnp.concatenate([w_in[:, :dt_off], w_in[:, dt_off + n_heads:]], axis=1).astype(BF16)
    w_dt = jnp.pad(w_in[:, dt_off:dt_off + n_heads], ((0, 0), (0, LANES - n_heads)))
    w_dt_hi = w_dt.astype(BF16)
    w_dt = jnp.concatenate([w_dt_hi, (w_dt - w_dt_hi.astype(F32)).astype(BF16)], axis=1)
    half = RET_QK_DIM // 2
    inv_freq = ROPE_BASE ** (-jnp.arange(half, dtype=F32) / half)
    ang = jnp.arange(seq, dtype=F32)[:, None] * inv_freq[None, :]
    cos, sin = jnp.cos(ang), jnp.sin(ang)

    w_ret_b, w_ssm_b, w_out_b = w_ret_out.astype(BF16), w_ssm_out.astype(BF16), w_out.astype(BF16)
    w_r_hi = w_router.T.astype(BF16)
    w_router_t = jnp.concatenate([w_r_hi, (w_router.T - w_r_hi.astype(F32)).astype(BF16)], axis=0)
    bm = FFN_BLOCK
    slab = (SLAB_ROWS, LANES)

    def mixer(row0, tg, bg):
        proj, dt_raw = _inproj_call(x2, row0, tg, norm_mix.reshape(1, d), mod3, cos, sin, w_main, w_dt,
                                    conv_w, conv_b, 2 * qk_w + 2 * v_w + d_inner, seq, plan["tm_in"], 2 * qk_w)
        ya = _retention_call(proj, w_ret_b, bg, seq, plan["ret_chunk"])
        yb = _ssd_call(proj, dt_raw, dt_bias, a_log, d_skip, ssm_norm, w_ssm_b, bg, seq, plan["ssm_chunk"])
        x1, h2, idx, rank, prow, cnt = _merge_call(ya, yb, proj, x2, row0, mod3, norm_ffn.reshape(1, d),
                                                   w_out_b, w_router_t, b_router, seq, plan["tm_merge"])
        counts = cnt[:, 0]
        padded = ((counts + bm - 1) // bm) * bm
        pad_end = jnp.cumsum(padded)
        start_pad = pad_end - padded
        n_blocks = (tg * TOP_K) // bm + N_EXPERTS
        e_ids = jnp.arange(N_EXPERTS, dtype=jnp.int32)[:, None, None, None]
        dest = rank + jnp.sum(jnp.where(idx[None] == e_ids, start_pad[:, None, None, None], 0), axis=0)
        dest = dest[:, :TOP_K, :].transpose(1, 0, 2).reshape(TOP_K, tg).astype(jnp.int32)
        n_real = pad_end[-1] // bm
        blk_valid = (jnp.arange(n_blocks) < n_real).astype(jnp.int32)
        blk_row = jnp.minimum(jnp.arange(n_blocks), n_real - 1).astype(jnp.int32)
        blk_e = jnp.minimum(jnp.sum(pad_end[None, :] <= (blk_row * bm)[:, None], axis=1),
                            N_EXPERTS - 1).astype(jnp.int32)
        xs = _sc_scatter_rows(h2.reshape((tg,) + slab), dest, n_blocks * bm)
        return dict(x1=x1, prow=prow, dest=dest, blocks=(blk_e, blk_row, blk_valid), xs=xs, row0=row0)

    def experts(m):
        n_rows = m["xs"].shape[0]
        ys = _ffn_call(*m["blocks"], m["xs"].reshape(n_rows * SLAB_ROWS, LANES),
                       w_gate_up, b_gate_up, w_down, b_down, bm)
        return _sc_gather_rows(ys.reshape((n_rows,) + slab), m["dest"].reshape(-1))

    n_groups = N_GROUPS if bsz % N_GROUPS == 0 else 1
    bg = bsz // n_groups
    tg = bg * seq
    groups = [mixer(g * tg, tg, bg) for g in range(n_groups)]
    ytoks = [experts(m) for m in groups]
    out = None
    for y, m in zip(ytoks, groups):
        out = _combine_call(y.reshape(TOP_K * tg * SLAB_ROWS, LANES), m["prow"], m["x1"], mod3,
                            norm_final.reshape(1, d), seq, plan["tm_moe"], m["row0"], t, out)
    return out


def kernel(x, c, w_ada, b_ada, norm_mix, norm_ffn, w_in, conv_w, conv_b, dt_bias, a_log, d_skip, ssm_norm,
           w_ret_out, w_ssm_out, w_out, w_router, b_router, w_gate_up, b_gate_up, w_down, b_down, norm_final):
    bsz, seq, d = x.shape
    depth = w_ada.shape[0]
    assert depth == 1, "the final norm is fused into the single layer's last kernel"
    x2 = x.reshape(bsz * seq, d)
    l = 0
    mod = _mod_call(c, w_ada[l], b_ada[l])
    mod3 = mod.reshape(bsz * N_MOD, 1, d)
    out = _layer(x2, mod3, bsz, seq, norm_mix[l], norm_ffn[l], w_in[l], conv_w[l], conv_b[l], dt_bias[l],
                 a_log[l], d_skip[l], ssm_norm[l], w_ret_out[l], w_ssm_out[l], w_out[l], w_router[l],
                 b_router[l], w_gate_up[l], b_gate_up[l], w_down[l], b_down[l], norm_final)
    return out.reshape(bsz, seq, d)
```

```python
import functools
import math

import numpy as np
import jax
import jax.numpy as jnp
from jax import lax
from jax.experimental import pallas as pl
from jax.experimental.pallas import tpu as pltpu
from jax.experimental.pallas import tpu_sc as plsc

F32 = jnp.float32
BF16 = jnp.bfloat16
HIGHEST = lax.Precision.HIGHEST

EPS = 1e-6
N_MOD = 6
RET_HEADS = 4
RET_QK_DIM = 256
RET_V_DIM = 512
ROPE_BASE = 10000.0
SSM_HEAD_DIM = 64
SSM_GROUPS = 8
SSM_STATE = 128
SSM_CONV = 4
N_EXPERTS = 32
TOP_K = 4
SWIGLU_LIMIT = 7.0
SWIGLU_ALPHA = 1.702

LANES = 128
SUBLANES = 8
VMEM_LIMIT = 56 * 1024 * 1024

RET_CHUNK = 256
SSM_CHUNK = 128
FFN_BLOCK = 512
SC_CORES = 2
SC_SUBCORES = 16
SC_GROUP = 64
N_GROUPS = 2


def _params(sem, vmem=VMEM_LIMIT):
    return pltpu.CompilerParams(dimension_semantics=sem, vmem_limit_bytes=vmem)


def _nt_dot(a, b, **kw):
    return lax.dot_general(a, b, (((1,), (1,)), ((), ())), preferred_element_type=F32, **kw)


def _tn_dot(a, b, **kw):
    return lax.dot_general(a, b, (((0,), (0,)), ((), ())), preferred_element_type=F32, **kw)


def _silu(v):
    return v * jax.nn.sigmoid(v)


SLAB_ROWS = 4
HIGH_HALF = 0xFFFF0000


def _store_slabs(ref, vals, n):
    for s in range(SLAB_ROWS):
        lo = vals[:, s * LANES:(s + 1) * LANES].astype(BF16).astype(F32)
        hi = vals[:, (s + SLAB_ROWS) * LANES:(s + SLAB_ROWS + 1) * LANES].astype(BF16).astype(F32)
        word = (pltpu.bitcast(lo, jnp.uint32) >> 16) | (pltpu.bitcast(hi, jnp.uint32) & jnp.uint32(HIGH_HALF))
        ref[pl.ds(s, n, stride=SLAB_ROWS), :] = word


def _load_slabs(ref, n, base=0):
    lo, hi = [], []
    for s in range(SLAB_ROWS):
        word = ref[pl.ds(base + s, n, stride=SLAB_ROWS), :]
        lo.append(pltpu.bitcast(word << 16, F32))
        hi.append(pltpu.bitcast(word & jnp.uint32(HIGH_HALF), F32))
    return lo + hi


def _mod_kernel(c_ref, w_ref, b_ref, o_ref):
    cond = _silu(c_ref[...])
    o_ref[...] = jnp.dot(cond, w_ref[...], preferred_element_type=F32, precision=HIGHEST) + b_ref[...]


def _mod_call(c, w_ada, b_ada):
    bsz, d = c.shape
    n = w_ada.shape[1]
    return pl.pallas_call(
        _mod_kernel,
        out_shape=jax.ShapeDtypeStruct((bsz, n), F32),
        grid=(n // d,),
        in_specs=[pl.BlockSpec((bsz, d), lambda j: (0, 0)),
                  pl.BlockSpec((d, d), lambda j: (0, j)),
                  pl.BlockSpec((1, d), lambda j: (0, j))],
        out_specs=pl.BlockSpec((bsz, d), lambda j: (0, j)),
        compiler_params=_params(("arbitrary",)),
        name="mod",
    )(c, w_ada, b_ada.reshape(1, n))


def _inproj_kernel(x_ref, nw_ref, sc_ref, sh_ref, cos_ref, sin_ref, w_ref, wdt_ref, cw_ref, cb_ref,
                   o_ref, dt_ref, h_s, work, carry, *, conv_j0, conv_nj, silu_j, sigm_j, tiles_per_seq,
                   tm, tn, sub):
    i = pl.program_id(0)
    j = pl.program_id(1)
    n_dt = dt_ref.shape[1]
    rows = min(tm, 256)

    @pl.when(j == 0)
    def _():
        xf = x_ref[...]
        ms = jnp.mean(xf * xf, axis=-1, keepdims=True)
        y = xf * lax.rsqrt(ms + EPS) * nw_ref[...]
        hm = y * (1.0 + sc_ref[0]) + sh_ref[0]
        hb = hm.astype(BF16)
        h_s[...] = hb
        h_lo = (hm - hb.astype(F32)).astype(BF16)
        d_hi = jnp.dot(hb, wdt_ref[...], preferred_element_type=F32)
        d_lo = jnp.dot(h_lo, wdt_ref[:, :n_dt], preferred_element_type=F32)
        dt_ref[...] = d_hi[:, :n_dt] + d_hi[:, n_dt:] + d_lo
        half = RET_QK_DIM // 2
        for p in range(tn // sub):
            for r in range(tm // rows):
                rs = slice(r * rows, (r + 1) * rows)
                acc = jnp.dot(h_s[rs, :], w_ref[:, p * sub:(p + 1) * sub], preferred_element_type=F32)
                cos = cos_ref[rs, :]
                sin = sin_ref[rs, :]
                for cc in range(sub // RET_QK_DIM):
                    c = p * (sub // RET_QK_DIM) + cc
                    a = acc[:, cc * RET_QK_DIM: cc * RET_QK_DIM + half]
                    b = acc[:, cc * RET_QK_DIM + half: (cc + 1) * RET_QK_DIM]
                    scale = 1.0 if c < RET_HEADS else RET_QK_DIM ** -0.5
                    o_ref[rs, c * RET_QK_DIM: c * RET_QK_DIM + half] = ((a * cos - b * sin) * scale).astype(BF16)
                    o_ref[rs, c * RET_QK_DIM + half: (c + 1) * RET_QK_DIM] = (
                        (a * sin + b * cos) * scale).astype(BF16)

    is_conv = (j >= conv_j0) & (j < conv_j0 + conv_nj)

    @pl.when(is_conv)
    def _():
        cj = j - conv_j0
        pad = SUBLANES

        @pl.when(i % tiles_per_seq == 0)
        def _():
            carry[cj] = jnp.zeros(carry.shape[1:], F32)

        for p in range(tn // sub):
            for r in range(tm // rows):
                r0 = r * rows
                acc = jnp.dot(h_s[r0:r0 + rows, :], w_ref[:, p * sub:(p + 1) * sub], preferred_element_type=F32)
                for cc in range(sub // LANES):
                    c = p * (sub // LANES) + cc
                    cols = slice(c * LANES, (c + 1) * LANES)
                    if r == 0:
                        work[c, 0:pad, :] = carry[cj, c]
                    lo = pad + r0
                    work[c, lo:lo + rows, :] = acc[:, cc * LANES:(cc + 1) * LANES]
                    conv = cb_ref[:, cols] + cw_ref[SSM_CONV - 1:SSM_CONV, cols] * work[c, lo:lo + rows, :]
                    for k in range(SSM_CONV - 1):
                        shift = SSM_CONV - 1 - k
                        conv = conv + cw_ref[k:k + 1, cols] * work[c, lo - shift:lo - shift + rows, :]
                    if r0 + rows == tm:
                        carry[cj, c] = work[c, tm:tm + pad, :]
                    o_ref[r0:r0 + rows, cols] = _silu(conv).astype(BF16)

    def plain(act):
        for p in range(tn // sub):
            for r in range(tm // rows):
                acc = jnp.dot(h_s[r * rows:(r + 1) * rows, :], w_ref[:, p * sub:(p + 1) * sub],
                              preferred_element_type=F32)
                o_ref[r * rows:(r + 1) * rows, p * sub:(p + 1) * sub] = act(acc).astype(BF16)

    is_silu = (j >= silu_j[0]) & (j < silu_j[1])
    is_sigm = (j >= sigm_j[0]) & (j < sigm_j[1])
    pl.when(is_silu)(lambda: plain(_silu))
    pl.when(is_sigm)(lambda: plain(jax.nn.sigmoid))
    pl.when((j != 0) & jnp.logical_not(is_conv | is_silu | is_sigm))(lambda: plain(lambda v: v))


def _inproj_call(x2, row0, t, norm_w, mod3, cos, sin, w_main, w_dt, conv_w, conv_b, conv_off, seq, tm, tn):
    d = x2.shape[1]
    n = w_main.shape[1]
    conv_dim = conv_w.shape[1]
    tiles_per_seq = seq // tm
    off = row0 // tm
    assert tn == 2 * RET_HEADS * RET_QK_DIM, "rotary epilogue expects q and k in the first column tile"
    assert conv_off % tn == 0 and conv_dim % tn == 0
    conv_j0, conv_nj = conv_off // tn, conv_dim // tn
    sub = 512
    g_off = 2 * RET_HEADS * RET_QK_DIM + RET_HEADS * RET_V_DIM
    assert g_off % tn == 0 and (conv_off - g_off) % tn == 0 and (n - conv_off - conv_dim) % tn == 0
    silu_j = (g_off // tn, conv_off // tn)
    sigm_j = ((conv_off + conv_dim) // tn, n // tn)
    kern = functools.partial(_inproj_kernel, conv_j0=conv_j0, conv_nj=conv_nj, silu_j=silu_j, sigm_j=sigm_j,
                             tiles_per_seq=tiles_per_seq, tm=tm, tn=tn, sub=sub)
    conv_idx = lambda i, j: (0, jnp.clip(j - conv_j0, 0, conv_nj - 1))
    return pl.pallas_call(
        kern,
        out_shape=(jax.ShapeDtypeStruct((t, n), BF16), jax.ShapeDtypeStruct((t, LANES), F32)),
        grid=(t // tm, n // tn),
        in_specs=[
            pl.BlockSpec((tm, d), lambda i, j: (i + off, 0)),
            pl.BlockSpec((1, d), lambda i, j: (0, 0)),
            pl.BlockSpec((1, 1, d), lambda i, j: (((i + off) // tiles_per_seq) * N_MOD + 1, 0, 0)),
            pl.BlockSpec((1, 1, d), lambda i, j: (((i + off) // tiles_per_seq) * N_MOD + 0, 0, 0)),
            pl.BlockSpec((tm, LANES), lambda i, j: (i % tiles_per_seq, 0)),
            pl.BlockSpec((tm, LANES), lambda i, j: (i % tiles_per_seq, 0)),
            pl.BlockSpec((d, tn), lambda i, j: (0, j)),
            pl.BlockSpec((d, 2 * LANES), lambda i, j: (0, 0)),
            pl.BlockSpec((SSM_CONV, tn), conv_idx),
            pl.BlockSpec((1, tn), conv_idx),
        ],
        out_specs=(pl.BlockSpec((tm, tn), lambda i, j: (i, j)),
                   pl.BlockSpec((tm, LANES), lambda i, j: (i, 0))),
        scratch_shapes=[pltpu.VMEM((tm, d), BF16),
                        pltpu.VMEM((tn // LANES, tm + SUBLANES, LANES), F32),
                        pltpu.VMEM((conv_nj, tn // LANES, SUBLANES, LANES), F32)],
        compiler_params=_params(("arbitrary", "arbitrary")),
        name="inproj",
    )(x2, norm_w, mod3, mod3, cos, sin, w_main, w_dt, conv_w, conv_b.reshape(1, conv_dim))


def _retention_kernel(q_ref, k_ref, v_ref, g_ref, din_ref, dq_ref, dk_ref, w_ref, o_ref, state,
                      *, decay_c):
    c = pl.program_id(1)

    @pl.when(c == 0)
    def _():
        state[...] = jnp.zeros_like(state)

    acc = None
    for h in range(RET_HEADS):
        qh = q_ref[:, h * RET_QK_DIM:(h + 1) * RET_QK_DIM]
        kh = k_ref[:, h * RET_QK_DIM:(h + 1) * RET_QK_DIM]
        vh = v_ref[:, h * RET_V_DIM:(h + 1) * RET_V_DIM]
        scores = _nt_dot(qh, kh) * din_ref[h]
        inner = jnp.dot(scores.astype(BF16), vh, preferred_element_type=F32)
        st = state[h]
        cross = jnp.dot(qh, st.astype(BF16), preferred_element_type=F32) * dq_ref[h]
        kd = (kh.astype(F32) * dk_ref[h]).astype(BF16)
        state[h] = st * decay_c[h] + _tn_dot(kd, vh)
        ret = inner + cross
        ret = ret * lax.rsqrt(jnp.mean(ret * ret, axis=-1, keepdims=True) + EPS)
        ret = ret * g_ref[:, h * RET_V_DIM:(h + 1) * RET_V_DIM].astype(F32)
        part = jnp.dot(ret.astype(BF16), w_ref[h * RET_V_DIM:(h + 1) * RET_V_DIM, :],
                       preferred_element_type=F32)
        acc = part if acc is None else acc + part
    o_ref[...] = acc


def _retention_tables(chunk):
    lg = np.log(1.0 - 2.0 ** (-5.0 - np.arange(RET_HEADS, dtype=np.float64)))
    idx = np.arange(chunk, dtype=np.float64)
    rel = idx[:, None] - idx[None, :]
    causal = rel >= 0
    din = np.where(causal[None], np.exp(np.where(causal, rel, 0.0)[None] * lg[:, None, None]), 0.0)
    dq = np.exp((idx + 1.0)[None, :, None] * lg[:, None, None])
    dk = np.exp((chunk - 1.0 - idx)[None, :, None] * lg[:, None, None])
    dc = tuple(float(v) for v in np.exp(chunk * lg))
    return (jnp.asarray(din, F32), jnp.asarray(dq, F32), jnp.asarray(dk, F32), dc)


def _retention_call(proj, w_ret, bsz, seq, chunk):
    t = proj.shape[0]
    d = w_ret.shape[1]
    nc = seq // chunk
    qk_w = RET_HEADS * RET_QK_DIM
    v_w = RET_HEADS * RET_V_DIM
    din, dq, dk, dc = _retention_tables(chunk)
    kern = functools.partial(_retention_kernel, decay_c=dc)
    row = lambda b, c: b * nc + c
    return pl.pallas_call(
        kern,
        out_shape=jax.ShapeDtypeStruct((t, d), F32),
        grid=(bsz, nc),
        in_specs=[
            pl.BlockSpec((chunk, qk_w), lambda b, c: (row(b, c), 0)),
            pl.BlockSpec((chunk, qk_w), lambda b, c: (row(b, c), 1)),
            pl.BlockSpec((chunk, v_w), lambda b, c: (row(b, c), 1)),
            pl.BlockSpec((chunk, v_w), lambda b, c: (row(b, c), 2)),
            pl.BlockSpec((RET_HEADS, chunk, chunk), lambda b, c: (0, 0, 0)),
            pl.BlockSpec((RET_HEADS, chunk, 1), lambda b, c: (0, 0, 0)),
            pl.BlockSpec((RET_HEADS, chunk, 1), lambda b, c: (0, 0, 0)),
            pl.BlockSpec((v_w, d), lambda b, c: (0, 0)),
        ],
        out_specs=pl.BlockSpec((chunk, d), lambda b, c: (row(b, c), 0)),
        scratch_shapes=[pltpu.VMEM((RET_HEADS, RET_QK_DIM, RET_V_DIM), F32)],
        compiler_params=_params(("arbitrary", "arbitrary")),
        name="retention",
    )(proj, proj, proj, proj, din, dq, dk, w_ret)


def _ssd_kernel(z_ref, xbc_ref, dt_ref, dtb_ref, alog_ref, dsk_ref, nw_ref,
                tril_ref, exp_ref, w_ref, o_ref, state, yn_s, *, chunk, d_inner):
    c = pl.program_id(1)
    heads_per_group = d_inner // SSM_HEAD_DIM // SSM_GROUPS
    gw = heads_per_group * SSM_HEAD_DIM
    assert SSM_HEAD_DIM * 2 == LANES and gw == 2 * LANES

    @pl.when(c == 0)
    def _():
        state[...] = jnp.zeros_like(state)

    dt = jax.nn.softplus(dt_ref[...] + dtb_ref[...])
    a = -jnp.exp(alog_ref[...])
    adt = dt * a
    acs = jnp.dot(tril_ref[...], adt, preferred_element_type=F32, precision=HIGHEST)
    acs_t = acs.T
    dt_x = jnp.dot(dt.astype(BF16), exp_ref[...], preferred_element_type=F32)
    li = lax.broadcasted_iota(jnp.int32, (chunk, chunk), 0)
    si = lax.broadcasted_iota(jnp.int32, (chunk, chunk), 1)
    causal = li >= si
    low_half = si < SSM_HEAD_DIM
    lane_g = lax.broadcasted_iota(jnp.int32, (chunk, gw), 1)

    b_off = d_inner
    c_off = d_inner + SSM_GROUPS * SSM_STATE
    for g in range(SSM_GROUPS):
        bm = xbc_ref[:, b_off + g * SSM_STATE: b_off + (g + 1) * SSM_STATE]
        cm = xbc_ref[:, c_off + g * SSM_STATE: c_off + (g + 1) * SSM_STATE]
        xs_g = xbc_ref[:, g * gw:(g + 1) * gw].astype(F32)
        xdt_g = xs_g * dt_x[:, g * gw:(g + 1) * gw]
        cb = _nt_dot(cm, bm)
        cols, ms, xm = [], [], []
        for jh in range(heads_per_group):
            h = g * heads_per_group + jh
            col = jnp.broadcast_to(acs[:, h:h + 1], (chunk, chunk))
            seg = jnp.exp(jnp.where(causal, col - acs_t[h:h + 1, :], -jnp.inf))
            cols.append(col)
            ms.append((cb * seg).astype(BF16))
            in_head = (lane_g >= jh * SSM_HEAD_DIM) & (lane_g < (jh + 1) * SSM_HEAD_DIM)
            xm.append(jnp.where(in_head, xdt_g, 0.0).astype(BF16))
        y_diag = jnp.dot(jnp.concatenate(ms, axis=-1), jnp.concatenate(xm, axis=0),
                         preferred_element_type=F32)
        a_x = jnp.concatenate([jnp.where(low_half, cols[0], cols[1]),
                               jnp.where(low_half, cols[2], cols[3])], axis=-1)
        e_acs_x = jnp.exp(a_x)
        a_last_x = a_x[chunk - 1:chunk, :]
        st = state[g]
        y_off = jnp.dot(cm, st.astype(BF16), preferred_element_type=F32) * e_acs_x
        xdec = (xdt_g * jnp.exp(a_last_x - a_x)).astype(BF16)
        state[g] = st * e_acs_x[chunk - 1:chunk, :] + _tn_dot(bm, xdec)
        y = y_diag + y_off + dsk_ref[:, g * gw:(g + 1) * gw] * xs_g
        yz = y * z_ref[:, g * gw:(g + 1) * gw].astype(F32)
        yn = yz * lax.rsqrt(jnp.mean(yz * yz, axis=-1, keepdims=True) + EPS) * nw_ref[:, g * gw:(g + 1) * gw]
        yn_s[:, g * gw:(g + 1) * gw] = yn.astype(BF16)
    o_ref[...] = jnp.dot(yn_s[...], w_ref[...], preferred_element_type=F32)


def _ssd_call(proj, dt_raw, dt_bias, a_log, d_skip, ssm_norm, w_ssm, bsz, seq, chunk):
    t = proj.shape[0]
    d_inner, d = w_ssm.shape
    conv_dim = d_inner + 2 * SSM_GROUPS * SSM_STATE
    n_heads = d_inner // SSM_HEAD_DIM
    nc = seq // chunk
    gw = d_inner // SSM_GROUPS
    pad_h = lambda v: jnp.pad(v.astype(F32), (0, LANES - n_heads)).reshape(1, LANES)
    assert chunk == LANES, "the per-head decay tiles are built lane-for-lane against the chunk"
    tril = jnp.asarray(np.tril(np.ones((chunk, chunk), np.float32)))
    expand = np.zeros((LANES, d_inner), np.float32)
    for h in range(n_heads):
        expand[h, h * SSM_HEAD_DIM:(h + 1) * SSM_HEAD_DIM] = 1.0
    expand = jnp.asarray(expand, BF16)
    kern = functools.partial(_ssd_kernel, chunk=chunk, d_inner=d_inner)
    row = lambda b, c: b * nc + c
    z_blk = (2 * RET_HEADS * RET_QK_DIM + 2 * RET_HEADS * RET_V_DIM) // d_inner
    xbc_blk = (2 * RET_HEADS * RET_QK_DIM + 2 * RET_HEADS * RET_V_DIM + d_inner) // conv_dim
    full = lambda shape: pl.BlockSpec(shape, lambda b, c: (0,) * len(shape))
    return pl.pallas_call(
        kern,
        out_shape=jax.ShapeDtypeStruct((t, d), F32),
        grid=(bsz, nc),
        in_specs=[
            pl.BlockSpec((chunk, d_inner), lambda b, c: (row(b, c), z_blk)),
            pl.BlockSpec((chunk, conv_dim), lambda b, c: (row(b, c), xbc_blk)),
            pl.BlockSpec((chunk, LANES), lambda b, c: (row(b, c), 0)),
            full((1, LANES)), full((1, LANES)),
            full((1, d_inner)), full((1, d_inner)), full((chunk, chunk)), full((LANES, d_inner)),
            full((d_inner, d)),
        ],
        out_specs=pl.BlockSpec((chunk, d), lambda b, c: (row(b, c), 0)),
        scratch_shapes=[pltpu.VMEM((SSM_GROUPS, SSM_STATE, gw), F32),
                        pltpu.VMEM((chunk, d_inner), BF16)],
        compiler_params=_params(("arbitrary", "arbitrary")),
        name="ssd",
    )(proj, proj, dt_raw, pad_h(dt_bias), pad_h(a_log),
      jnp.repeat(d_skip.astype(F32), SSM_HEAD_DIM).reshape(1, d_inner), ssm_norm.reshape(1, d_inner),
      tril, expand, w_ssm)


def _merge_kernel(ya_ref, yb_ref, ga_ref, gb_ref, x_ref, gm_ref, scf_ref, shf_ref, nw_ref, wo_ref,
                  wr_ref, br_ref, tri_ref,
                  x1_ref, h2_ref, idx_ref, rank_ref, prow_ref, cnt_ref, cnt_s, *, tm):
    i = pl.program_id(0)

    @pl.when(i == 0)
    def _():
        cnt_s[...] = jnp.zeros_like(cnt_s)

    merged = ga_ref[...].astype(F32) * ya_ref[...] + gb_ref[...].astype(F32) * yb_ref[...]
    mo = jnp.dot(merged.astype(BF16), wo_ref[...], preferred_element_type=F32)
    x1 = x_ref[...] + gm_ref[0] * mo
    x1_ref[...] = x1
    ms = jnp.mean(x1 * x1, axis=-1, keepdims=True)
    h2 = x1 * lax.rsqrt(ms + EPS) * nw_ref[...] * (1.0 + scf_ref[0]) + shf_ref[0]
    _store_slabs(h2_ref, h2, tm)

    h_hi = h2.astype(BF16)
    h_lo = (h2 - h_hi.astype(F32)).astype(BF16)
    lg2 = _nt_dot(wr_ref[...], h_hi)
    lg = lg2[:N_EXPERTS] + lg2[N_EXPERTS:] + _nt_dot(wr_ref[:N_EXPERTS, :], h_lo) + br_ref[...]
    sub = lax.broadcasted_iota(jnp.int32, lg.shape, 0)
    work = lg
    vals, idxs, sels = [], [], []
    for _ in range(TOP_K):
        m = jnp.max(work, axis=0, keepdims=True)
        ik = jnp.min(jnp.where(work == m, sub, N_EXPERTS), axis=0, keepdims=True)
        sel = sub == ik
        vals.append(m)
        idxs.append(ik)
        sels.append(sel)
        work = jnp.where(sel, -jnp.inf, work)
    exps = [jnp.exp(v - vals[0]) for v in vals]
    denom = exps[0]
    for e in exps[1:]:
        denom = denom + e
    probs = [e / denom for e in exps]

    base = cnt_s[:, 0:1]
    ranks = []
    for k in range(TOP_K):
        mk = jnp.where(sels[k], 1.0, 0.0)
        pre = jnp.dot(mk.astype(BF16), tri_ref[...], preferred_element_type=F32)
        ranks.append(jnp.sum(jnp.where(sels[k], pre + base, 0.0), axis=0, keepdims=True))
        base = base + jnp.sum(mk, axis=1, keepdims=True)
    cnt_s[...] = jnp.broadcast_to(base, cnt_s.shape)
    cnt_ref[...] = cnt_s[...].astype(jnp.int32)

    zi = jnp.zeros((SUBLANES - TOP_K, tm), jnp.int32)
    idx_ref[0] = jnp.concatenate(idxs + [zi], axis=0)
    rank_ref[0] = jnp.concatenate([r.astype(jnp.int32) for r in ranks] + [zi], axis=0)
    pt = jnp.concatenate(probs + [jnp.zeros((LANES - TOP_K, tm), F32)], axis=0)
    prow_ref[...] = pt.T


def _merge_call(ya, yb, proj, x2, row0, mod3, norm_w, w_out, w_router_t, b_router, seq, tm):
    t, d = ya.shape
    nt = t // tm
    tiles_per_seq = seq // tm
    off = row0 // tm
    ga_blk = proj.shape[1] // d - 2
    tri = jnp.asarray(np.triu(np.ones((tm, tm), np.float32), 1), BF16)
    kern = functools.partial(_merge_kernel, tm=tm)
    modspec = lambda m: pl.BlockSpec((1, 1, d), lambda i: (((i + off) // tiles_per_seq) * N_MOD + m, 0, 0))
    return pl.pallas_call(
        kern,
        out_shape=(jax.ShapeDtypeStruct((t, d), F32), jax.ShapeDtypeStruct((t * SLAB_ROWS, LANES), jnp.uint32),
                   jax.ShapeDtypeStruct((nt, SUBLANES, tm), jnp.int32),
                   jax.ShapeDtypeStruct((nt, SUBLANES, tm), jnp.int32),
                   jax.ShapeDtypeStruct((t, LANES), F32),
                   jax.ShapeDtypeStruct((N_EXPERTS, LANES), jnp.int32)),
        grid=(nt,),
        in_specs=[
            pl.BlockSpec((tm, d), lambda i: (i, 0)),
            pl.BlockSpec((tm, d), lambda i: (i, 0)),
            pl.BlockSpec((tm, d), lambda i: (i, ga_blk)),
            pl.BlockSpec((tm, d), lambda i: (i, ga_blk + 1)),
            pl.BlockSpec((tm, d), lambda i: (i + off, 0)),
            modspec(2), modspec(4), modspec(3),
            pl.BlockSpec((1, d), lambda i: (0, 0)),
            pl.BlockSpec((d, d), lambda i: (0, 0)),
            pl.BlockSpec((2 * N_EXPERTS, d), lambda i: (0, 0)),
            pl.BlockSpec((N_EXPERTS, 1), lambda i: (0, 0)),
            pl.BlockSpec((tm, tm), lambda i: (0, 0)),
        ],
        out_specs=(pl.BlockSpec((tm, d), lambda i: (i, 0)),
                   pl.BlockSpec((tm * SLAB_ROWS, LANES), lambda i: (i, 0)),
                   pl.BlockSpec((1, SUBLANES, tm), lambda i: (i, 0, 0)),
                   pl.BlockSpec((1, SUBLANES, tm), lambda i: (i, 0, 0)),
                   pl.BlockSpec((tm, LANES), lambda i: (i, 0)),
                   pl.BlockSpec((N_EXPERTS, LANES), lambda i: (0, 0))),
        scratch_shapes=[pltpu.VMEM((N_EXPERTS, LANES), F32)],
        compiler_params=_params(("arbitrary",)),
        name="merge",
    )(ya, yb, proj, proj, x2, mod3, mod3, mod3, norm_w, w_out, w_router_t,
      b_router.reshape(N_EXPERTS, 1), tri)


def _sc_mesh():
    return plsc.VectorSubcoreMesh(core_axis_name="c", subcore_axis_name="s")


def _sc_worker():
    return lax.axis_index("s") * SC_CORES + lax.axis_index("c")


def _sc_scatter_rows(rows, dest, n_out):
    t = rows.shape[0]
    n_k = dest.shape[0]
    g = SC_GROUP
    n_w = SC_CORES * SC_SUBCORES
    assert t % (n_w * g) == 0
    cpw = t // (n_w * g)
    dest_w = dest.reshape(n_k, n_w, cpw, g).transpose(1, 0, 2, 3)

    @functools.partial(
        pl.kernel, mesh=_sc_mesh(),
        out_type=jax.ShapeDtypeStruct((n_out,) + rows.shape[1:], rows.dtype),
        scratch_types=[pltpu.VMEM((n_k, cpw, g), jnp.int32),
                       pltpu.VMEM((g,) + rows.shape[1:], rows.dtype),
                       pltpu.SemaphoreType.DMA],
    )
    def scatter(rows_hbm, dest_hbm, out_hbm, idx_v, rows_v, sem):
        wid = _sc_worker()
        pltpu.sync_copy(dest_hbm.at[wid], idx_v)

        @pl.loop(0, cpw)
        def _(cc):
            r0 = pl.multiple_of((wid * cpw + cc) * g, g)
            pltpu.sync_copy(rows_hbm.at[pl.ds(r0, g)], rows_v)
            copies = [pltpu.async_copy(rows_v, out_hbm.at[idx_v.at[k, cc]], sem) for k in range(n_k)]
            for cp in copies:
                cp.wait()

    return scatter(rows, dest_w)


def _sc_gather_rows(table, idx):
    m = idx.shape[0]
    g = SC_GROUP
    n_w = SC_CORES * SC_SUBCORES
    assert m % (n_w * g) == 0
    per_w = m // n_w

    @functools.partial(
        pl.kernel, mesh=_sc_mesh(),
        out_type=jax.ShapeDtypeStruct((m,) + table.shape[1:], table.dtype),
        scratch_types=[pltpu.VMEM((per_w,), jnp.int32),
                       pltpu.VMEM((g,) + table.shape[1:], table.dtype),
                       pltpu.SemaphoreType.DMA],
    )
    def gather(table_hbm, idx_hbm, out_hbm, idx_v, rows_v, sem):
        base = _sc_worker() * per_w
        pltpu.sync_copy(idx_hbm.at[pl.ds(base, per_w)], idx_v)

        @pl.loop(0, per_w // g)
        def _(cc):
            off = pl.multiple_of(cc * g, g)
            pltpu.async_copy(table_hbm.at[idx_v.at[pl.ds(off, g)]], rows_v, sem).wait()
            pltpu.sync_copy(rows_v, out_hbm.at[pl.ds(base + off, g)])

    return gather(table, idx)


def _ffn_kernel(be_ref, br_ref, bv_ref, x_ref, wgu_ref, bgu_ref, wd_ref, bd_ref, o_ref, wgu_s, wd_s,
                *, bm, d_ff):
    i = pl.program_id(0)

    @pl.when((i == 0) | (be_ref[i] != be_ref[jnp.maximum(i - 1, 0)]))
    def _():
        wgu_s[...] = wgu_ref[0].astype(BF16)
        wd_s[...] = wd_ref[0].astype(BF16)

    @pl.when(bv_ref[i] == 1)
    def _():
        x = jnp.concatenate(_load_slabs(x_ref, bm), axis=-1).astype(BF16)
        gu = jnp.dot(x, wgu_s[...], preferred_element_type=F32) + bgu_ref[0]
        gate = jnp.minimum(gu[:, :d_ff], SWIGLU_LIMIT)
        up = jnp.clip(gu[:, d_ff:], -SWIGLU_LIMIT, SWIGLU_LIMIT)
        act = gate * jax.nn.sigmoid(SWIGLU_ALPHA * gate) * (up + 1.0)
        y = jnp.dot(act.astype(BF16), wd_s[...], preferred_element_type=F32) + bd_ref[0]
        _store_slabs(o_ref, y, bm)

    @pl.when(bv_ref[i] == 0)
    def _():
        o_ref[...] = jnp.zeros_like(o_ref)


def _ffn_call(blk_e, blk_row, blk_valid, xs, w_gu, b_gu, w_d, b_d, bm):
    n_e, d, f2 = w_gu.shape
    d_ff = f2 // 2
    nb = blk_e.shape[0]
    kern = functools.partial(_ffn_kernel, bm=bm, d_ff=d_ff)
    gs = pltpu.PrefetchScalarGridSpec(
        num_scalar_prefetch=3,
        grid=(nb,),
        in_specs=[pl.BlockSpec((bm * SLAB_ROWS, LANES), lambda i, be, br, bv: (br[i], 0)),
                  pl.BlockSpec((1, d, f2), lambda i, be, br, bv: (be[i], 0, 0)),
                  pl.BlockSpec((1, 1, f2), lambda i, be, br, bv: (be[i], 0, 0)),
                  pl.BlockSpec((1, d_ff, d), lambda i, be, br, bv: (be[i], 0, 0)),
                  pl.BlockSpec((1, 1, d), lambda i, be, br, bv: (be[i], 0, 0))],
        out_specs=pl.BlockSpec((bm * SLAB_ROWS, LANES), lambda i, be, br, bv: (i, 0)),
        scratch_shapes=[pltpu.VMEM((d, f2), BF16), pltpu.VMEM((d_ff, d), BF16)],
    )
    return pl.pallas_call(
        kern,
        out_shape=jax.ShapeDtypeStruct((nb * bm * SLAB_ROWS, LANES), jnp.uint32),
        grid_spec=gs,
        compiler_params=_params(("arbitrary",)),
        name="ffn",
    )(blk_e, blk_row, blk_valid, xs, w_gu, b_gu.reshape(n_e, 1, f2), w_d, b_d.reshape(n_e, 1, d))


def _combine_kernel(y0_ref, y1_ref, y2_ref, y3_ref, prow_ref, x1_ref, gf_ref, nw_ref, *rest, tm):
    o_ref = rest[-1]
    p = prow_ref[...]
    pieces = [_load_slabs(y_ref, tm) for y_ref in (y0_ref, y1_ref, y2_ref, y3_ref)]
    for s in range(len(pieces[0])):
        moe = None
        for k in range(TOP_K):
            piece = pieces[k][s] * p[:, k:k + 1]
            moe = piece if moe is None else moe + piece
        sl = slice(s * LANES, (s + 1) * LANES)
        o_ref[:, sl] = x1_ref[:, sl] + gf_ref[0][:, sl] * moe
    xo = o_ref[...]
    o_ref[...] = xo * lax.rsqrt(jnp.mean(xo * xo, axis=-1, keepdims=True) + EPS) * nw_ref[...]


def _combine_call(ytok, prow, x1, mod3, norm_final, seq, tm, row0, t_total, out_prev):
    t, d = x1.shape
    nt = t // tm
    tiles_per_seq = seq // tm
    off = row0 // tm
    kern = functools.partial(_combine_kernel, tm=tm)
    yspec = lambda k: pl.BlockSpec((tm * SLAB_ROWS, LANES), lambda i: (k * nt + i, 0))
    in_specs = [yspec(0), yspec(1), yspec(2), yspec(3),
                pl.BlockSpec((tm, LANES), lambda i: (i, 0)),
                pl.BlockSpec((tm, d), lambda i: (i, 0)),
                pl.BlockSpec((1, 1, d), lambda i: (((i + off) // tiles_per_seq) * N_MOD + 5, 0, 0)),
                pl.BlockSpec((1, d), lambda i: (0, 0))]
    args = [ytok, ytok, ytok, ytok, prow, x1, mod3, norm_final]
    aliases = {}
    if out_prev is not None:
        in_specs.append(pl.BlockSpec(memory_space=pl.ANY))
        aliases = {len(args): 0}
        args.append(out_prev)
    return pl.pallas_call(
        kern,
        out_shape=jax.ShapeDtypeStruct((t_total, d), F32),
        grid=(nt,),
        in_specs=in_specs,
        out_specs=pl.BlockSpec((tm, d), lambda i: (i + off, 0)),
        input_output_aliases=aliases,
        compiler_params=_params(("arbitrary",)),
        name="combine",
    )(*args)


def _plan(seq):
    def fit(pref):
        tm = min(pref, seq)
        assert seq % tm == 0
        return tm
    return dict(tm_in=fit(1024), tm_merge=fit(512), tm_moe=fit(512),
                ret_chunk=fit(RET_CHUNK), ssm_chunk=fit(SSM_CHUNK))


def _layer(x2, mod3, bsz, seq, norm_mix, norm_ffn, w_in, conv_w, conv_b, dt_bias, a_log, d_skip, ssm_norm,
           w_ret_out, w_ssm_out, w_out, w_router, b_router, w_gate_up, b_gate_up, w_down, b_down,
           norm_final):
    t, d = x2.shape
    plan = _plan(seq)
    qk_w = RET_HEADS * RET_QK_DIM
    v_w = RET_HEADS * RET_V_DIM
    d_inner = w_ssm_out.shape[0]
    conv_dim = conv_w.shape[1]
    n_heads = d_inner // SSM_HEAD_DIM
    dt_off = 2 * qk_w + 2 * v_w + d_inner + conv_dim

    w_main = jnp.concatenate([w_in[:, :dt_off], w_in[:, dt_off + n_heads:]], axis=1).astype(BF16)
    w_dt = jnp.pad(w_in[:, dt_off:dt_off + n_heads], ((0, 0), (0, LANES - n_heads)))
    w_dt_hi = w_dt.astype(BF16)
    w_dt = jnp.concatenate([w_dt_hi, (w_dt - w_dt_hi.astype(F32)).astype(BF16)], axis=1)
    half = RET_QK_DIM // 2
    inv_freq = ROPE_BASE ** (-jnp.arange(half, dtype=F32) / half)
    ang = jnp.arange(seq, dtype=F32)[:, None] * inv_freq[None, :]
    cos, sin = jnp.cos(ang), jnp.sin(ang)

    w_ret_b, w_ssm_b, w_out_b = w_ret_out.astype(BF16), w_ssm_out.astype(BF16), w_out.astype(BF16)
    w_r_hi = w_router.T.astype(BF16)
    w_router_t = jnp.concatenate([w_r_hi, (w_router.T - w_r_hi.astype(F32)).astype(BF16)], axis=0)
    bm = FFN_BLOCK
    slab = (SLAB_ROWS, LANES)

    def mixer(row0, tg, bg):
        proj, dt_raw = _inproj_call(x2, row0, tg, norm_mix.reshape(1, d), mod3, cos, sin, w_main, w_dt,
                                    conv_w, conv_b, 2 * qk_w + 2 * v_w + d_inner, seq, plan["tm_in"], 2 * qk_w)
        ya = _retention_call(proj, w_ret_b, bg, seq, plan["ret_chunk"])
        yb = _ssd_call(proj, dt_raw, dt_bias, a_log, d_skip, ssm_norm, w_ssm_b, bg, seq, plan["ssm_chunk"])
        x1, h2, idx, rank, prow, cnt = _merge_call(ya, yb, proj, x2, row0, mod3, norm_ffn.reshape(1, d),
                                                   w_out_b, w_router_t, b_router, seq, plan["tm_merge"])
        counts = cnt[:, 0]
        padded = ((counts + bm - 1) // bm) * bm
        pad_end = jnp.cumsum(padded)
        start_pad = pad_end - padded
        n_blocks = (tg * TOP_K) // bm + N_EXPERTS
        e_ids = jnp.arange(N_EXPERTS, dtype=jnp.int32)[:, None, None, None]
        dest = rank + jnp.sum(jnp.where(idx[None] == e_ids, start_pad[:, None, None, None], 0), axis=0)
        dest = dest[:, :TOP_K, :].transpose(1, 0, 2).reshape(TOP_K, tg).astype(jnp.int32)
        n_real = pad_end[-1] // bm
        blk_valid = (jnp.arange(n_blocks) < n_real).astype(jnp.int32)
        blk_row = jnp.minimum(jnp.arange(n_blocks), n_real - 1).astype(jnp.int32)
        blk_e = jnp.minimum(jnp.sum(pad_end[None, :] <= (blk_row * bm)[:, None], axis=1),
                            N_EXPERTS - 1).astype(jnp.int32)
        xs = _sc_scatter_rows(h2.reshape((tg,) + slab), dest, n_blocks * bm)
        return dict(x1=x1, prow=prow, dest=dest, blocks=(blk_e, blk_row, blk_valid), xs=xs, row0=row0)

    def experts(m):
        n_rows = m["xs"].shape[0]
        ys = _ffn_call(*m["blocks"], m["xs"].reshape(n_rows * SLAB_ROWS, LANES),
                       w_gate_up, b_gate_up, w_down, b_down, bm)
        return _sc_gather_rows(ys.reshape((n_rows,) + slab), m["dest"].reshape(-1))

    n_groups = N_GROUPS if bsz % N_GROUPS == 0 else 1
    bg = bsz // n_groups
    tg = bg * seq
    groups = [mixer(g * tg, tg, bg) for g in range(n_groups)]
    ytoks = [experts(m) for m in groups]
    out = None
    for y, m in zip(ytoks, groups):
        out = _combine_call(y.reshape(TOP_K * tg * SLAB_ROWS, LANES), m["prow"], m["x1"], mod3,
                            norm_final.reshape(1, d), seq, plan["tm_moe"], m["row0"], t, out)
    return out


def kernel(x, c, w_ada, b_ada, norm_mix, norm_ffn, w_in, conv_w, conv_b, dt_bias, a_log, d_skip, ssm_norm,
           w_ret_out, w_ssm_out, w_out, w_router, b_router, w_gate_up, b_gate_up, w_down, b_down, norm_final):
    bsz, seq, d = x.shape
    depth = w_ada.shape[0]
    assert depth == 1, "the final norm is fused into the single layer's last kernel"
    x2 = x.reshape(bsz * seq, d)
    l = 0
    mod = _mod_call(c, w_ada[l], b_ada[l])
    mod3 = mod.reshape(bsz * N_MOD, 1, d)
    out = _layer(x2, mod3, bsz, seq, norm_mix[l], norm_ffn[l], w_in[l], conv_w[l], conv_b[l], dt_bias[l],
                 a_log[l], d_skip[l], ssm_norm[l], w_ret_out[l], w_ssm_out[l], w_out[l], w_router[l],
                 b_router[l], w_gate_up[l], b_gate_up[l], w_down[l], b_down[l], norm_final)
    return out.reshape(bsz, seq, d)
```

```python
import functools
import math

import numpy as np
import jax
import jax.numpy as jnp
from jax import lax
from jax.experimental import pallas as pl
from jax.experimental.pallas import tpu as pltpu
from jax.experimental.pallas import tpu_sc as plsc

F32 = jnp.float32
BF16 = jnp.bfloat16
HIGHEST = lax.Precision.HIGHEST

EPS = 1e-6
N_MOD = 6
RET_HEADS = 4
RET_QK_DIM = 256
RET_V_DIM = 512
ROPE_BASE = 10000.0
SSM_HEAD_DIM = 64
SSM_GROUPS = 8
SSM_STATE = 128
SSM_CONV = 4
N_EXPERTS = 32
TOP_K = 4
SWIGLU_LIMIT = 7.0
SWIGLU_ALPHA = 1.702

LANES = 128
SUBLANES = 8
VMEM_LIMIT = 56 * 1024 * 1024

RET_CHUNK = 256
SSM_CHUNK = 128
FFN_BLOCK = 512
SC_CORES = 2
SC_SUBCORES = 16
SC_GROUP = 64
N_GROUPS = 2


def _params(sem, vmem=VMEM_LIMIT):
    return pltpu.CompilerParams(dimension_semantics=sem, vmem_limit_bytes=vmem)


def _nt_dot(a, b, **kw):
    return lax.dot_general(a, b, (((1,), (1,)), ((), ())), preferred_element_type=F32, **kw)


def _tn_dot(a, b, **kw):
    return lax.dot_general(a, b, (((0,), (0,)), ((), ())), preferred_element_type=F32, **kw)


def _silu(v):
    return v * jax.nn.sigmoid(v)


SLAB_ROWS = 4
HIGH_HALF = 0xFFFF0000


def _store_slabs(ref, vals, n):
    for s in range(SLAB_ROWS):
        lo = vals[:, s * LANES:(s + 1) * LANES].astype(BF16).astype(F32)
        hi = vals[:, (s + SLAB_ROWS) * LANES:(s + SLAB_ROWS + 1) * LANES].astype(BF16).astype(F32)
        word = (pltpu.bitcast(lo, jnp.uint32) >> 16) | (pltpu.bitcast(hi, jnp.uint32) & jnp.uint32(HIGH_HALF))
        ref[pl.ds(s, n, stride=SLAB_ROWS), :] = word


def _load_slabs(ref, n, base=0):
    lo, hi = [], []
    for s in range(SLAB_ROWS):
        word = ref[pl.ds(base + s, n, stride=SLAB_ROWS), :]
        lo.append(pltpu.bitcast(word << 16, F32))
        hi.append(pltpu.bitcast(word & jnp.uint32(HIGH_HALF), F32))
    return lo + hi


def _mod_kernel(c_ref, w_ref, b_ref, o_ref):
    cond = _silu(c_ref[...])
    o_ref[...] = jnp.dot(cond, w_ref[...], preferred_element_type=F32, precision=HIGHEST) + b_ref[...]


def _mod_call(c, w_ada, b_ada):
    bsz, d = c.shape
    n = w_ada.shape[1]
    return pl.pallas_call(
        _mod_kernel,
        out_shape=jax.ShapeDtypeStruct((bsz, n), F32),
        grid=(n // d,),
        in_specs=[pl.BlockSpec((bsz, d), lambda j: (0, 0)),
                  pl.BlockSpec((d, d), lambda j: (0, j)),
                  pl.BlockSpec((1, d), lambda j: (0, j))],
        out_specs=pl.BlockSpec((bsz, d), lambda j: (0, j)),
        compiler_params=_params(("arbitrary",)),
        name="mod",
    )(c, w_ada, b_ada.reshape(1, n))


def _inproj_kernel(x_ref, nw_ref, sc_ref, sh_ref, cos_ref, sin_ref, w_ref, wdt_ref, cw_ref, cb_ref,
                   o_ref, dt_ref, h_s, work, carry, *, conv_j0, conv_nj, silu_j, sigm_j, tiles_per_seq,
                   tm, tn, sub):
    i = pl.program_id(0)
    j = pl.program_id(1)
    n_dt = dt_ref.shape[1]
    rows = min(tm, 256)

    @pl.when(j == 0)
    def _():
        xf = x_ref[...]
        ms = jnp.mean(xf * xf, axis=-1, keepdims=True)
        y = xf * lax.rsqrt(ms + EPS) * nw_ref[...]
        hm = y * (1.0 + sc_ref[0]) + sh_ref[0]
        hb = hm.astype(BF16)
        h_s[...] = hb
        h_lo = (hm - hb.astype(F32)).astype(BF16)
        d_hi = jnp.dot(hb, wdt_ref[...], preferred_element_type=F32)
        d_lo = jnp.dot(h_lo, wdt_ref[:, :n_dt], preferred_element_type=F32)
        dt_ref[...] = d_hi[:, :n_dt] + d_hi[:, n_dt:] + d_lo
        half = RET_QK_DIM // 2
        for p in range(tn // sub):
            for r in range(tm // rows):
                rs = slice(r * rows, (r + 1) * rows)
                acc = jnp.dot(h_s[rs, :], w_ref[:, p * sub:(p + 1) * sub], preferred_element_type=F32)
                cos = cos_ref[rs, :]
                sin = sin_ref[rs, :]
                for cc in range(sub // RET_QK_DIM):
                    c = p * (sub // RET_QK_DIM) + cc
                    a = acc[:, cc * RET_QK_DIM: cc * RET_QK_DIM + half]
                    b = acc[:, cc * RET_QK_DIM + half: (cc + 1) * RET_QK_DIM]
                    scale = 1.0 if c < RET_HEADS else RET_QK_DIM ** -0.5
                    o_ref[rs, c * RET_QK_DIM: c * RET_QK_DIM + half] = ((a * cos - b * sin) * scale).astype(BF16)
                    o_ref[rs, c * RET_QK_DIM + half: (c + 1) * RET_QK_DIM] = (
                        (a * sin + b * cos) * scale).astype(BF16)

    is_conv = (j >= conv_j0) & (j < conv_j0 + conv_nj)

    @pl.when(is_conv)
    def _():
        cj = j - conv_j0
        pad = SUBLANES

        @pl.when(i % tiles_per_seq == 0)
        def _():
            carry[cj] = jnp.zeros(carry.shape[1:], F32)

        for p in range(tn // sub):
            for r in range(tm // rows):
                r0 = r * rows
                acc = jnp.dot(h_s[r0:r0 + rows, :], w_ref[:, p * sub:(p + 1) * sub], preferred_element_type=F32)
                for cc in range(sub // LANES):
                    c = p * (sub // LANES) + cc
                    cols = slice(c * LANES, (c + 1) * LANES)
                    if r == 0:
                        work[c, 0:pad, :] = carry[cj, c]
                    lo = pad + r0
                    work[c, lo:lo + rows, :] = acc[:, cc * LANES:(cc + 1) * LANES]
                    conv = cb_ref[:, cols] + cw_ref[SSM_CONV - 1:SSM_CONV, cols] * work[c, lo:lo + rows, :]
                    for k in range(SSM_CONV - 1):
                        shift = SSM_CONV - 1 - k
                        conv = conv + cw_ref[k:k + 1, cols] * work[c, lo - shift:lo - shift + rows, :]
                    if r0 + rows == tm:
                        carry[cj, c] = work[c, tm:tm + pad, :]
                    o_ref[r0:r0 + rows, cols] = _silu(conv).astype(BF16)

    def plain(act):
        for p in range(tn // sub):
            for r in range(tm // rows):
                acc = jnp.dot(h_s[r * rows:(r + 1) * rows, :], w_ref[:, p * sub:(p + 1) * sub],
                              preferred_element_type=F32)
                o_ref[r * rows:(r + 1) * rows, p * sub:(p + 1) * sub] = act(acc).astype(BF16)

    is_silu = (j >= silu_j[0]) & (j < silu_j[1])
    is_sigm = (j >= sigm_j[0]) & (j < sigm_j[1])
    pl.when(is_silu)(lambda: plain(_silu))
    pl.when(is_sigm)(lambda: plain(jax.nn.sigmoid))
    pl.when((j != 0) & jnp.logical_not(is_conv | is_silu | is_sigm))(lambda: plain(lambda v: v))


def _inproj_call(x2, row0, t, norm_w, mod3, cos, sin, w_main, w_dt, conv_w, conv_b, conv_off, seq, tm, tn):
    d = x2.shape[1]
    n = w_main.shape[1]
    conv_dim = conv_w.shape[1]
    tiles_per_seq = seq // tm
    off = row0 // tm
    assert tn == 2 * RET_HEADS * RET_QK_DIM, "rotary epilogue expects q and k in the first column tile"
    assert conv_off % tn == 0 and conv_dim % tn == 0
    conv_j0, conv_nj = conv_off // tn, conv_dim // tn
    sub = 512
    g_off = 2 * RET_HEADS * RET_QK_DIM + RET_HEADS * RET_V_DIM
    assert g_off % tn == 0 and (conv_off - g_off) % tn == 0 and (n - conv_off - conv_dim) % tn == 0
    silu_j = (g_off // tn, conv_off // tn)
    sigm_j = ((conv_off + conv_dim) // tn, n // tn)
    kern = functools.partial(_inproj_kernel, conv_j0=conv_j0, conv_nj=conv_nj, silu_j=silu_j, sigm_j=sigm_j,
                             tiles_per_seq=tiles_per_seq, tm=tm, tn=tn, sub=sub)
    conv_idx = lambda i, j: (0, jnp.clip(j - conv_j0, 0, conv_nj - 1))
    return pl.pallas_call(
        kern,
        out_shape=(jax.ShapeDtypeStruct((t, n), BF16), jax.ShapeDtypeStruct((t, LANES), F32)),
        grid=(t // tm, n // tn),
        in_specs=[
            pl.BlockSpec((tm, d), lambda i, j: (i + off, 0)),
            pl.BlockSpec((1, d), lambda i, j: (0, 0)),
            pl.BlockSpec((1, 1, d), lambda i, j: (((i + off) // tiles_per_seq) * N_MOD + 1, 0, 0)),
            pl.BlockSpec((1, 1, d), lambda i, j: (((i + off) // tiles_per_seq) * N_MOD + 0, 0, 0)),
            pl.BlockSpec((tm, LANES), lambda i, j: (i % tiles_per_seq, 0)),
            pl.BlockSpec((tm, LANES), lambda i, j: (i % tiles_per_seq, 0)),
            pl.BlockSpec((d, tn), lambda i, j: (0, j)),
            pl.BlockSpec((d, 2 * LANES), lambda i, j: (0, 0)),
            pl.BlockSpec((SSM_CONV, tn), conv_idx),
            pl.BlockSpec((1, tn), conv_idx),
        ],
        out_specs=(pl.BlockSpec((tm, tn), lambda i, j: (i, j)),
                   pl.BlockSpec((tm, LANES), lambda i, j: (i, 0))),
        scratch_shapes=[pltpu.VMEM((tm, d), BF16),
                        pltpu.VMEM((tn // LANES, tm + SUBLANES, LANES), F32),
                        pltpu.VMEM((conv_nj, tn // LANES, SUBLANES, LANES), F32)],
        compiler_params=_params(("arbitrary", "arbitrary")),
        name="inproj",
    )(x2, norm_w, mod3, mod3, cos, sin, w_main, w_dt, conv_w, conv_b.reshape(1, conv_dim))


def _retention_kernel(q_ref, k_ref, v_ref, g_ref, din_ref, dq_ref, dk_ref, w_ref, o_ref, state,
                      *, decay_c):
    c = pl.program_id(1)

    @pl.when(c == 0)
    def _():
        state[...] = jnp.zeros_like(state)

    acc = None
    for h in range(RET_HEADS):
        qh = q_ref[:, h * RET_QK_DIM:(h + 1) * RET_QK_DIM]
        kh = k_ref[:, h * RET_QK_DIM:(h + 1) * RET_QK_DIM]
        vh = v_ref[:, h * RET_V_DIM:(h + 1) * RET_V_DIM]
        scores = _nt_dot(qh, kh) * din_ref[h]
        inner = jnp.dot(scores.astype(BF16), vh, preferred_element_type=F32)
        st = state[h]
        cross = jnp.dot(qh, st.astype(BF16), preferred_element_type=F32) * dq_ref[h]
        kd = (kh.astype(F32) * dk_ref[h]).astype(BF16)
        state[h] = st * decay_c[h] + _tn_dot(kd, vh)
        ret = inner + cross
        ret = ret * lax.rsqrt(jnp.mean(ret * ret, axis=-1, keepdims=True) + EPS)
        ret = ret * g_ref[:, h * RET_V_DIM:(h + 1) * RET_V_DIM].astype(F32)
        part = jnp.dot(ret.astype(BF16), w_ref[h * RET_V_DIM:(h + 1) * RET_V_DIM, :],
                       preferred_element_type=F32)
        acc = part if acc is None else acc + part
    o_ref[...] = acc.astype(o_ref.dtype)


def _retention_tables(chunk):
    lg = np.log(1.0 - 2.0 ** (-5.0 - np.arange(RET_HEADS, dtype=np.float64)))
    idx = np.arange(chunk, dtype=np.float64)
    rel = idx[:, None] - idx[None, :]
    causal = rel >= 0
    din = np.where(causal[None], np.exp(np.where(causal, rel, 0.0)[None] * lg[:, None, None]), 0.0)
    dq = np.exp((idx + 1.0)[None, :, None] * lg[:, None, None])
    dk = np.exp((chunk - 1.0 - idx)[None, :, None] * lg[:, None, None])
    dc = tuple(float(v) for v in np.exp(chunk * lg))
    return (jnp.asarray(din, F32), jnp.asarray(dq, F32), jnp.asarray(dk, F32), dc)


def _retention_call(proj, w_ret, bsz, seq, chunk):
    t = proj.shape[0]
    d = w_ret.shape[1]
    nc = seq // chunk
    qk_w = RET_HEADS * RET_QK_DIM
    v_w = RET_HEADS * RET_V_DIM
    din, dq, dk, dc = _retention_tables(chunk)
    kern = functools.partial(_retention_kernel, decay_c=dc)
    row = lambda b, c: b * nc + c
    return pl.pallas_call(
        kern,
        out_shape=jax.ShapeDtypeStruct((t, d), BF16),
        grid=(bsz, nc),
        in_specs=[
            pl.BlockSpec((chunk, qk_w), lambda b, c: (row(b, c), 0)),
            pl.BlockSpec((chunk, qk_w), lambda b, c: (row(b, c), 1)),
            pl.BlockSpec((chunk, v_w), lambda b, c: (row(b, c), 1)),
            pl.BlockSpec((chunk, v_w), lambda b, c: (row(b, c), 2)),
            pl.BlockSpec((RET_HEADS, chunk, chunk), lambda b, c: (0, 0, 0)),
            pl.BlockSpec((RET_HEADS, chunk, 1), lambda b, c: (0, 0, 0)),
            pl.BlockSpec((RET_HEADS, chunk, 1), lambda b, c: (0, 0, 0)),
            pl.BlockSpec((v_w, d), lambda b, c: (0, 0)),
        ],
        out_specs=pl.BlockSpec((chunk, d), lambda b, c: (row(b, c), 0)),
        scratch_shapes=[pltpu.VMEM((RET_HEADS, RET_QK_DIM, RET_V_DIM), F32)],
        compiler_params=_params(("arbitrary", "arbitrary")),
        name="retention",
    )(proj, proj, proj, proj, din, dq, dk, w_ret)


def _ssd_kernel(z_ref, xbc_ref, dt_ref, dtb_ref, alog_ref, dsk_ref, nw_ref,
                tril_ref, exp_ref, w_ref, o_ref, state, yn_s, *, chunk, d_inner):
    c = pl.program_id(1)
    heads_per_group = d_inner // SSM_HEAD_DIM // SSM_GROUPS
    gw = heads_per_group * SSM_HEAD_DIM
    assert SSM_HEAD_DIM * 2 == LANES and gw == 2 * LANES

    @pl.when(c == 0)
    def _():
        state[...] = jnp.zeros_like(state)

    dt = jax.nn.softplus(dt_ref[...] + dtb_ref[...])
    a = -jnp.exp(alog_ref[...])
    adt = dt * a
    acs = jnp.dot(tril_ref[...], adt, preferred_element_type=F32, precision=HIGHEST)
    acs_t = acs.T
    dt_x = jnp.dot(dt.astype(BF16), exp_ref[...], preferred_element_type=F32)
    li = lax.broadcasted_iota(jnp.int32, (chunk, chunk), 0)
    si = lax.broadcasted_iota(jnp.int32, (chunk, chunk), 1)
    causal = li >= si
    low_half = si < SSM_HEAD_DIM
    lane_g = lax.broadcasted_iota(jnp.int32, (chunk, gw), 1)

    b_off = d_inner
    c_off = d_inner + SSM_GROUPS * SSM_STATE
    for g in range(SSM_GROUPS):
        bm = xbc_ref[:, b_off + g * SSM_STATE: b_off + (g + 1) * SSM_STATE]
        cm = xbc_ref[:, c_off + g * SSM_STATE: c_off + (g + 1) * SSM_STATE]
        xs_g = xbc_ref[:, g * gw:(g + 1) * gw].astype(F32)
        xdt_g = xs_g * dt_x[:, g * gw:(g + 1) * gw]
        cb = _nt_dot(cm, bm)
        cols, ms, xm = [], [], []
        for jh in range(heads_per_group):
            h = g * heads_per_group + jh
            col = jnp.broadcast_to(acs[:, h:h + 1], (chunk, chunk))
            seg = jnp.exp(jnp.where(causal, col - acs_t[h:h + 1, :], -jnp.inf))
            cols.append(col)
            ms.append((cb * seg).astype(BF16))
            in_head = (lane_g >= jh * SSM_HEAD_DIM) & (lane_g < (jh + 1) * SSM_HEAD_DIM)
            xm.append(jnp.where(in_head, xdt_g, 0.0).astype(BF16))
        y_diag = jnp.dot(jnp.concatenate(ms, axis=-1), jnp.concatenate(xm, axis=0),
                         preferred_element_type=F32)
        a_x = jnp.concatenate([jnp.where(low_half, cols[0], cols[1]),
                               jnp.where(low_half, cols[2], cols[3])], axis=-1)
        e_acs_x = jnp.exp(a_x)
        a_last_x = a_x[chunk - 1:chunk, :]
        st = state[g]
        y_off = jnp.dot(cm, st.astype(BF16), preferred_element_type=F32) * e_acs_x
        xdec = (xdt_g * jnp.exp(a_last_x - a_x)).astype(BF16)
        state[g] = st * e_acs_x[chunk - 1:chunk, :] + _tn_dot(bm, xdec)
        y = y_diag + y_off + dsk_ref[:, g * gw:(g + 1) * gw] * xs_g
        yz = y * z_ref[:, g * gw:(g + 1) * gw].astype(F32)
        yn = yz * lax.rsqrt(jnp.mean(yz * yz, axis=-1, keepdims=True) + EPS) * nw_ref[:, g * gw:(g + 1) * gw]
        yn_s[:, g * gw:(g + 1) * gw] = yn.astype(BF16)
    o_ref[...] = jnp.dot(yn_s[...], w_ref[...], preferred_element_type=F32).astype(o_ref.dtype)


def _ssd_call(proj, dt_raw, dt_bias, a_log, d_skip, ssm_norm, w_ssm, bsz, seq, chunk):
    t = proj.shape[0]
    d_inner, d = w_ssm.shape
    conv_dim = d_inner + 2 * SSM_GROUPS * SSM_STATE
    n_heads = d_inner // SSM_HEAD_DIM
    nc = seq // chunk
    gw = d_inner // SSM_GROUPS
    pad_h = lambda v: jnp.pad(v.astype(F32), (0, LANES - n_heads)).reshape(1, LANES)
    assert chunk == LANES, "the per-head decay tiles are built lane-for-lane against the chunk"
    tril = jnp.asarray(np.tril(np.ones((chunk, chunk), np.float32)))
    expand = np.zeros((LANES, d_inner), np.float32)
    for h in range(n_heads):
        expand[h, h * SSM_HEAD_DIM:(h + 1) * SSM_HEAD_DIM] = 1.0
    expand = jnp.asarray(expand, BF16)
    kern = functools.partial(_ssd_kernel, chunk=chunk, d_inner=d_inner)
    row = lambda b, c: b * nc + c
    z_blk = (2 * RET_HEADS * RET_QK_DIM + 2 * RET_HEADS * RET_V_DIM) // d_inner
    xbc_blk = (2 * RET_HEADS * RET_QK_DIM + 2 * RET_HEADS * RET_V_DIM + d_inner) // conv_dim
    full = lambda shape: pl.BlockSpec(shape, lambda b, c: (0,) * len(shape))
    return pl.pallas_call(
        kern,
        out_shape=jax.ShapeDtypeStruct((t, d), BF16),
        grid=(bsz, nc),
        in_specs=[
            pl.BlockSpec((chunk, d_inner), lambda b, c: (row(b, c), z_blk)),
            pl.BlockSpec((chunk, conv_dim), lambda b, c: (row(b, c), xbc_blk)),
            pl.BlockSpec((chunk, LANES), lambda b, c: (row(b, c), 0)),
            full((1, LANES)), full((1, LANES)),
            full((1, d_inner)), full((1, d_inner)), full((chunk, chunk)), full((LANES, d_inner)),
            full((d_inner, d)),
        ],
        out_specs=pl.BlockSpec((chunk, d), lambda b, c: (row(b, c), 0)),
        scratch_shapes=[pltpu.VMEM((SSM_GROUPS, SSM_STATE, gw), F32),
                        pltpu.VMEM((chunk, d_inner), BF16)],
        compiler_params=_params(("arbitrary", "arbitrary")),
        name="ssd",
    )(proj, proj, dt_raw, pad_h(dt_bias), pad_h(a_log),
      jnp.repeat(d_skip.astype(F32), SSM_HEAD_DIM).reshape(1, d_inner), ssm_norm.reshape(1, d_inner),
      tril, expand, w_ssm)


def _merge_kernel(ya_ref, yb_ref, ga_ref, gb_ref, x_ref, gm_ref, scf_ref, shf_ref, nw_ref, wo_ref,
                  wr_ref, br_ref, tri_ref,
                  x1_ref, h2_ref, idx_ref, rank_ref, prow_ref, cnt_ref, cnt_s, *, tm):
    i = pl.program_id(0)

    @pl.when(i == 0)
    def _():
        cnt_s[...] = jnp.zeros_like(cnt_s)

    merged = (ga_ref[...].astype(F32) * ya_ref[...].astype(F32)
              + gb_ref[...].astype(F32) * yb_ref[...].astype(F32))
    mo = jnp.dot(merged.astype(BF16), wo_ref[...], preferred_element_type=F32)
    x1 = x_ref[...] + gm_ref[0] * mo
    x1_ref[...] = x1
    ms = jnp.mean(x1 * x1, axis=-1, keepdims=True)
    h2 = x1 * lax.rsqrt(ms + EPS) * nw_ref[...] * (1.0 + scf_ref[0]) + shf_ref[0]
    _store_slabs(h2_ref, h2, tm)

    h_hi = h2.astype(BF16)
    h_lo = (h2 - h_hi.astype(F32)).astype(BF16)
    lg2 = _nt_dot(wr_ref[...], h_hi)
    lg = lg2[:N_EXPERTS] + lg2[N_EXPERTS:] + _nt_dot(wr_ref[:N_EXPERTS, :], h_lo) + br_ref[...]
    sub = lax.broadcasted_iota(jnp.int32, lg.shape, 0)
    work = lg
    vals, idxs, sels = [], [], []
    for _ in range(TOP_K):
        m = jnp.max(work, axis=0, keepdims=True)
        ik = jnp.min(jnp.where(work == m, sub, N_EXPERTS), axis=0, keepdims=True)
        sel = sub == ik
        vals.append(m)
        idxs.append(ik)
        sels.append(sel)
        work = jnp.where(sel, -jnp.inf, work)
    exps = [jnp.exp(v - vals[0]) for v in vals]
    denom = exps[0]
    for e in exps[1:]:
        denom = denom + e
    probs = [e / denom for e in exps]

    base = cnt_s[:, 0:1]
    ranks = []
    for k in range(TOP_K):
        mk = jnp.where(sels[k], 1.0, 0.0)
        pre = jnp.dot(mk.astype(BF16), tri_ref[...], preferred_element_type=F32)
        ranks.append(jnp.sum(jnp.where(sels[k], pre + base, 0.0), axis=0, keepdims=True))
        base = base + jnp.sum(mk, axis=1, keepdims=True)
    cnt_s[...] = jnp.broadcast_to(base, cnt_s.shape)
    cnt_ref[...] = cnt_s[...].astype(jnp.int32)

    zi = jnp.zeros((SUBLANES - TOP_K, tm), jnp.int32)
    idx_ref[0] = jnp.concatenate(idxs + [zi], axis=0)
    rank_ref[0] = jnp.concatenate([r.astype(jnp.int32) for r in ranks] + [zi], axis=0)
    pt = jnp.concatenate(probs + [jnp.zeros((LANES - TOP_K, tm), F32)], axis=0)
    prow_ref[...] = pt.T


def _merge_call(ya, yb, proj, x2, row0, mod3, norm_w, w_out, w_router_t, b_router, seq, tm):
    t, d = ya.shape
    nt = t // tm
    tiles_per_seq = seq // tm
    off = row0 // tm
    ga_blk = proj.shape[1] // d - 2
    tri = jnp.asarray(np.triu(np.ones((tm, tm), np.float32), 1), BF16)
    kern = functools.partial(_merge_kernel, tm=tm)
    modspec = lambda m: pl.BlockSpec((1, 1, d), lambda i: (((i + off) // tiles_per_seq) * N_MOD + m, 0, 0))
    return pl.pallas_call(
        kern,
        out_shape=(jax.ShapeDtypeStruct((t, d), F32), jax.ShapeDtypeStruct((t * SLAB_ROWS, LANES), jnp.uint32),
                   jax.ShapeDtypeStruct((nt, SUBLANES, tm), jnp.int32),
                   jax.ShapeDtypeStruct((nt, SUBLANES, tm), jnp.int32),
                   jax.ShapeDtypeStruct((t, LANES), F32),
                   jax.ShapeDtypeStruct((N_EXPERTS, LANES), jnp.int32)),
        grid=(nt,),
        in_specs=[
            pl.BlockSpec((tm, d), lambda i: (i, 0)),
            pl.BlockSpec((tm, d), lambda i: (i, 0)),
            pl.BlockSpec((tm, d), lambda i: (i, ga_blk)),
            pl.BlockSpec((tm, d), lambda i: (i, ga_blk + 1)),
            pl.BlockSpec((tm, d), lambda i: (i + off, 0)),
            modspec(2), modspec(4), modspec(3),
            pl.BlockSpec((1, d), lambda i: (0, 0)),
            pl.BlockSpec((d, d), lambda i: (0, 0)),
            pl.BlockSpec((2 * N_EXPERTS, d), lambda i: (0, 0)),
            pl.BlockSpec((N_EXPERTS, 1), lambda i: (0, 0)),
            pl.BlockSpec((tm, tm), lambda i: (0, 0)),
        ],
        out_specs=(pl.BlockSpec((tm, d), lambda i: (i, 0)),
                   pl.BlockSpec((tm * SLAB_ROWS, LANES), lambda i: (i, 0)),
                   pl.BlockSpec((1, SUBLANES, tm), lambda i: (i, 0, 0)),
                   pl.BlockSpec((1, SUBLANES, tm), lambda i: (i, 0, 0)),
                   pl.BlockSpec((tm, LANES), lambda i: (i, 0)),
                   pl.BlockSpec((N_EXPERTS, LANES), lambda i: (0, 0))),
        scratch_shapes=[pltpu.VMEM((N_EXPERTS, LANES), F32)],
        compiler_params=_params(("arbitrary",)),
        name="merge",
    )(ya, yb, proj, proj, x2, mod3, mod3, mod3, norm_w, w_out, w_router_t,
      b_router.reshape(N_EXPERTS, 1), tri)


def _sc_mesh():
    return plsc.VectorSubcoreMesh(core_axis_name="c", subcore_axis_name="s")


def _sc_worker():
    return lax.axis_index("s") * SC_CORES + lax.axis_index("c")


def _sc_scatter_rows(rows, dest, n_out):
    t = rows.shape[0]
    n_k = dest.shape[0]
    g = SC_GROUP
    n_w = SC_CORES * SC_SUBCORES
    assert t % (n_w * g) == 0
    cpw = t // (n_w * g)
    dest_w = dest.reshape(n_k, n_w, cpw, g).transpose(1, 0, 2, 3)

    @functools.partial(
        pl.kernel, mesh=_sc_mesh(),
        out_type=jax.ShapeDtypeStruct((n_out,) + rows.shape[1:], rows.dtype),
        scratch_types=[pltpu.VMEM((n_k, cpw, g), jnp.int32),
                       pltpu.VMEM((g,) + rows.shape[1:], rows.dtype),
                       pltpu.SemaphoreType.DMA],
    )
    def scatter(rows_hbm, dest_hbm, out_hbm, idx_v, rows_v, sem):
        wid = _sc_worker()
        pltpu.sync_copy(dest_hbm.at[wid], idx_v)

        @pl.loop(0, cpw)
        def _(cc):
            r0 = pl.multiple_of((wid * cpw + cc) * g, g)
            pltpu.sync_copy(rows_hbm.at[pl.ds(r0, g)], rows_v)
            copies = [pltpu.async_copy(rows_v, out_hbm.at[idx_v.at[k, cc]], sem) for k in range(n_k)]
            for cp in copies:
                cp.wait()

    return scatter(rows, dest_w)


def _sc_gather_rows(table, idx):
    m = idx.shape[0]
    g = SC_GROUP
    n_w = SC_CORES * SC_SUBCORES
    assert m % (n_w * g) == 0
    per_w = m // n_w

    @functools.partial(
        pl.kernel, mesh=_sc_mesh(),
        out_type=jax.ShapeDtypeStruct((m,) + table.shape[1:], table.dtype),
        scratch_types=[pltpu.VMEM((per_w,), jnp.int32),
                       pltpu.VMEM((g,) + table.shape[1:], table.dtype),
                       pltpu.SemaphoreType.DMA],
    )
    def gather(table_hbm, idx_hbm, out_hbm, idx_v, rows_v, sem):
        base = _sc_worker() * per_w
        pltpu.sync_copy(idx_hbm.at[pl.ds(base, per_w)], idx_v)

        @pl.loop(0, per_w // g)
        def _(cc):
            off = pl.multiple_of(cc * g, g)
            pltpu.async_copy(table_hbm.at[idx_v.at[pl.ds(off, g)]], rows_v, sem).wait()
            pltpu.sync_copy(rows_v, out_hbm.at[pl.ds(base + off, g)])

    return gather(table, idx)


def _ffn_kernel(be_ref, br_ref, bv_ref, x_ref, wgu_ref, bgu_ref, wd_ref, bd_ref, o_ref, wgu_s, wd_s,
                *, bm, d_ff):
    i = pl.program_id(0)

    @pl.when((i == 0) | (be_ref[i] != be_ref[jnp.maximum(i - 1, 0)]))
    def _():
        wgu_s[...] = wgu_ref[0].astype(BF16)
        wd_s[...] = wd_ref[0].astype(BF16)

    @pl.when(bv_ref[i] == 1)
    def _():
        x = jnp.concatenate(_load_slabs(x_ref, bm), axis=-1).astype(BF16)
        gu = jnp.dot(x, wgu_s[...], preferred_element_type=F32) + bgu_ref[0]
        gate = jnp.minimum(gu[:, :d_ff], SWIGLU_LIMIT)
        up = jnp.clip(gu[:, d_ff:], -SWIGLU_LIMIT, SWIGLU_LIMIT)
        act = gate * jax.nn.sigmoid(SWIGLU_ALPHA * gate) * (up + 1.0)
        y = jnp.dot(act.astype(BF16), wd_s[...], preferred_element_type=F32) + bd_ref[0]
        _store_slabs(o_ref, y, bm)

    @pl.when(bv_ref[i] == 0)
    def _():
        o_ref[...] = jnp.zeros_like(o_ref)


def _ffn_call(blk_e, blk_row, blk_valid, xs, w_gu, b_gu, w_d, b_d, bm):
    n_e, d, f2 = w_gu.shape
    d_ff = f2 // 2
    nb = blk_e.shape[0]
    kern = functools.partial(_ffn_kernel, bm=bm, d_ff=d_ff)
    gs = pltpu.PrefetchScalarGridSpec(
        num_scalar_prefetch=3,
        grid=(nb,),
        in_specs=[pl.BlockSpec((bm * SLAB_ROWS, LANES), lambda i, be, br, bv: (br[i], 0)),
                  pl.BlockSpec((1, d, f2), lambda i, be, br, bv: (be[i], 0, 0)),
                  pl.BlockSpec((1, 1, f2), lambda i, be, br, bv: (be[i], 0, 0)),
                  pl.BlockSpec((1, d_ff, d), lambda i, be, br, bv: (be[i], 0, 0)),
                  pl.BlockSpec((1, 1, d), lambda i, be, br, bv: (be[i], 0, 0))],
        out_specs=pl.BlockSpec((bm * SLAB_ROWS, LANES), lambda i, be, br, bv: (i, 0)),
        scratch_shapes=[pltpu.VMEM((d, f2), BF16), pltpu.VMEM((d_ff, d), BF16)],
    )
    return pl.pallas_call(
        kern,
        out_shape=jax.ShapeDtypeStruct((nb * bm * SLAB_ROWS, LANES), jnp.uint32),
        grid_spec=gs,
        compiler_params=_params(("arbitrary",)),
        name="ffn",
    )(blk_e, blk_row, blk_valid, xs, w_gu, b_gu.reshape(n_e, 1, f2), w_d, b_d.reshape(n_e, 1, d))


def _combine_kernel(y0_ref, y1_ref, y2_ref, y3_ref, prow_ref, x1_ref, gf_ref, nw_ref, *rest, tm):
    o_ref = rest[-1]
    p = prow_ref[...]
    pieces = [_load_slabs(y_ref, tm) for y_ref in (y0_ref, y1_ref, y2_ref, y3_ref)]
    for s in range(len(pieces[0])):
        moe = None
        for k in range(TOP_K):
            piece = pieces[k][s] * p[:, k:k + 1]
            moe = piece if moe is None else moe + piece
        sl = slice(s * LANES, (s + 1) * LANES)
        o_ref[:, sl] = x1_ref[:, sl] + gf_ref[0][:, sl] * moe
    xo = o_ref[...]
    o_ref[...] = xo * lax.rsqrt(jnp.mean(xo * xo, axis=-1, keepdims=True) + EPS) * nw_ref[...]


def _combine_call(ytok, prow, x1, mod3, norm_final, seq, tm, row0, t_total, out_prev):
    t, d = x1.shape
    nt = t // tm
    tiles_per_seq = seq // tm
    off = row0 // tm
    kern = functools.partial(_combine_kernel, tm=tm)
    yspec = lambda k: pl.BlockSpec((tm * SLAB_ROWS, LANES), lambda i: (k * nt + i, 0))
    in_specs = [yspec(0), yspec(1), yspec(2), yspec(3),
                pl.BlockSpec((tm, LANES), lambda i: (i, 0)),
                pl.BlockSpec((tm, d), lambda i: (i, 0)),
                pl.BlockSpec((1, 1, d), lambda i: (((i + off) // tiles_per_seq) * N_MOD + 5, 0, 0)),
                pl.BlockSpec((1, d), lambda i: (0, 0))]
    args = [ytok, ytok, ytok, ytok, prow, x1, mod3, norm_final]
    aliases = {}
    if out_prev is not None:
        in_specs.append(pl.BlockSpec(memory_space=pl.ANY))
        aliases = {len(args): 0}
        args.append(out_prev)
    return pl.pallas_call(
        kern,
        out_shape=jax.ShapeDtypeStruct((t_total, d), F32),
        grid=(nt,),
        in_specs=in_specs,
        out_specs=pl.BlockSpec((tm, d), lambda i: (i + off, 0)),
        input_output_aliases=aliases,
        compiler_params=_params(("arbitrary",)),
        name="combine",
    )(*args)


def _plan(seq):
    def fit(pref):
        tm = min(pref, seq)
        assert seq % tm == 0
        return tm
    return dict(tm_in=fit(1024), tm_merge=fit(512), tm_moe=fit(512),
                ret_chunk=fit(RET_CHUNK), ssm_chunk=fit(SSM_CHUNK))


def _layer(x2, mod3, bsz, seq, norm_mix, norm_ffn, w_in, conv_w, conv_b, dt_bias, a_log, d_skip, ssm_norm,
           w_ret_out, w_ssm_out, w_out, w_router, b_router, w_gate_up, b_gate_up, w_down, b_down,
           norm_final):
    t, d = x2.shape
    plan = _plan(seq)
    qk_w = RET_HEADS * RET_QK_DIM
    v_w = RET_HEADS * RET_V_DIM
    d_inner = w_ssm_out.shape[0]
    conv_dim = conv_w.shape[1]
    n_heads = d_inner // SSM_HEAD_DIM
    dt_off = 2 * qk_w + 2 * v_w + d_inner + conv_dim

    w_main = jnp.concatenate([w_in[:, :dt_off], w_in[:, dt_off + n_heads:]], axis=1).astype(BF16)
    w_dt = jnp.pad(w_in[:, dt_off:dt_off + n_heads], ((0, 0), (0, LANES - n_heads)))
    w_dt_hi = w_dt.astype(BF16)
    w_dt = jnp.concatenate([w_dt_hi, (w_dt - w_dt_hi.astype(F32)).astype(BF16)], axis=1)
    half = RET_QK_DIM // 2
    inv_freq = ROPE_BASE ** (-jnp.arange(half, dtype=F32) / half)
    ang = jnp.arange(seq, dtype=F32)[:, None] * inv_freq[None, :]
    cos, sin = jnp.cos(ang), jnp.sin(ang)

    w_ret_b, w_ssm_b, w_out_b = w_ret_out.astype(BF16), w_ssm_out.astype(BF16), w_out.astype(BF16)
    w_r_hi = w_router.T.astype(BF16)
    w_router_t = jnp.concatenate([w_r_hi, (w_router.T - w_r_hi.astype(F32)).astype(BF16)], axis=0)
    bm = FFN_BLOCK
    slab = (SLAB_ROWS, LANES)

    def mixer(row0, tg, bg):
        proj, dt_raw = _inproj_call(x2, row0, tg, norm_mix.reshape(1, d), mod3, cos, sin, w_main, w_dt,
                                    conv_w, conv_b, 2 * qk_w + 2 * v_w + d_inner, seq, plan["tm_in"], 2 * qk_w)
        ya = _retention_call(proj, w_ret_b, bg, seq, plan["ret_chunk"])
        yb = _ssd_call(proj, dt_raw, dt_bias, a_log, d_skip, ssm_norm, w_ssm_b, bg, seq, plan["ssm_chunk"])
        x1, h2, idx, rank, prow, cnt = _merge_call(ya, yb, proj, x2, row0, mod3, norm_ffn.reshape(1, d),
                                                   w_out_b, w_router_t, b_router, seq, plan["tm_merge"])
        counts = cnt[:, 0]
        padded = ((counts + bm - 1) // bm) * bm
        pad_end = jnp.cumsum(padded)
        start_pad = pad_end - padded
        n_blocks = (tg * TOP_K) // bm + N_EXPERTS
        e_ids = jnp.arange(N_EXPERTS, dtype=jnp.int32)[:, None, None, None]
        dest = rank + jnp.sum(jnp.where(idx[None] == e_ids, start_pad[:, None, None, None], 0), axis=0)
        dest = dest[:, :TOP_K, :].transpose(1, 0, 2).reshape(TOP_K, tg).astype(jnp.int32)
        n_real = pad_end[-1] // bm
        blk_valid = (jnp.arange(n_blocks) < n_real).astype(jnp.int32)
        blk_row = jnp.minimum(jnp.arange(n_blocks), n_real - 1).astype(jnp.int32)
        blk_e = jnp.minimum(jnp.sum(pad_end[None, :] <= (blk_row * bm)[:, None], axis=1),
                            N_EXPERTS - 1).astype(jnp.int32)
        xs = _sc_scatter_rows(h2.reshape((tg,) + slab), dest, n_blocks * bm)
        return dict(x1=x1, prow=prow, dest=dest, blocks=(blk_e, blk_row, blk_valid), xs=xs, row0=row0)

    def experts(m):
        n_rows = m["xs"].shape[0]
        ys = _ffn_call(*m["blocks"], m["xs"].reshape(n_rows * SLAB_ROWS, LANES),
                       w_gate_up, b_gate_up, w_down, b_down, bm)
        return _sc_gather_rows(ys.reshape((n_rows,) + slab), m["dest"].reshape(-1))

    n_groups = N_GROUPS if bsz % N_GROUPS == 0 else 1
    bg = bsz // n_groups
    tg = bg * seq
    groups = [mixer(g * tg, tg, bg) for g in range(n_groups)]
    ytoks = [experts(m) for m in groups]
    out = None
    for y, m in zip(ytoks, groups):
        out = _combine_call(y.reshape(TOP_K * tg * SLAB_ROWS, LANES), m["prow"], m["x1"], mod3,
                            norm_final.reshape(1, d), seq, plan["tm_moe"], m["row0"], t, out)
    return out


def kernel(x, c, w_ada, b_ada, norm_mix, norm_ffn, w_in, conv_w, conv_b, dt_bias, a_log, d_skip, ssm_norm,
           w_ret_out, w_ssm_out, w_out, w_router, b_router, w_gate_up, b_gate_up, w_down, b_down, norm_final):
    bsz, seq, d = x.shape
    depth = w_ada.shape[0]
    assert depth == 1, "the final norm is fused into the single layer's last kernel"
    x2 = x.reshape(bsz * seq, d)
    l = 0
    mod = _mod_call(c, w_ada[l], b_ada[l])
    mod3 = mod.reshape(bsz * N_MOD, 1, d)
    out = _layer(x2, mod3, bsz, seq, norm_mix[l], norm_ffn[l], w_in[l], conv_w[l], conv_b[l], dt_bias[l],
                 a_log[l], d_skip[l], ssm_norm[l], w_ret_out[l], w_ssm_out[l], w_out[l], w_router[l],
                 b_router[l], w_gate_up[l], b_gate_up[l], w_down[l], b_down[l], norm_final)
    return out.reshape(bsz, seq, d)
```

```python
import functools
import math

import numpy as np
import jax
import jax.numpy as jnp
from jax import lax
from jax.experimental import pallas as pl
from jax.experimental.pallas import tpu as pltpu
from jax.experimental.pallas import tpu_sc as plsc

F32 = jnp.float32
BF16 = jnp.bfloat16
HIGHEST = lax.Precision.HIGHEST

EPS = 1e-6
N_MOD = 6
RET_HEADS = 4
RET_QK_DIM = 256
RET_V_DIM = 512
ROPE_BASE = 10000.0
SSM_HEAD_DIM = 64
SSM_GROUPS = 8
SSM_STATE = 128
SSM_CONV = 4
N_EXPERTS = 32
TOP_K = 4
SWIGLU_LIMIT = 7.0
SWIGLU_ALPHA = 1.702

LANES = 128
SUBLANES = 8
VMEM_LIMIT = 56 * 1024 * 1024

RET_CHUNK = 256
SSM_CHUNK = 128
FFN_BLOCK = 512
FFN_ROWS = 256
SC_CORES = 2
SC_SUBCORES = 16
SC_GROUP = 64
N_GROUPS = 2


def _params(sem, vmem=VMEM_LIMIT):
    return pltpu.CompilerParams(dimension_semantics=sem, vmem_limit_bytes=vmem)


def _nt_dot(a, b, **kw):
    return lax.dot_general(a, b, (((1,), (1,)), ((), ())), preferred_element_type=F32, **kw)


def _tn_dot(a, b, **kw):
    return lax.dot_general(a, b, (((0,), (0,)), ((), ())), preferred_element_type=F32, **kw)


def _silu(v):
    return v * jax.nn.sigmoid(v)


SLAB_ROWS = 4
HIGH_HALF = 0xFFFF0000


def _store_slabs(ref, vals, n):
    for s in range(SLAB_ROWS):
        lo = vals[:, s * LANES:(s + 1) * LANES].astype(BF16).astype(F32)
        hi = vals[:, (s + SLAB_ROWS) * LANES:(s + SLAB_ROWS + 1) * LANES].astype(BF16).astype(F32)
        word = (pltpu.bitcast(lo, jnp.uint32) >> 16) | (pltpu.bitcast(hi, jnp.uint32) & jnp.uint32(HIGH_HALF))
        ref[pl.ds(s, n, stride=SLAB_ROWS), :] = word


def _load_slabs(ref, n, base=0):
    lo, hi = [], []
    for s in range(SLAB_ROWS):
        word = ref[pl.ds(base + s, n, stride=SLAB_ROWS), :]
        lo.append(pltpu.bitcast(word << 16, F32))
        hi.append(pltpu.bitcast(word & jnp.uint32(HIGH_HALF), F32))
    return lo + hi


def _mod_kernel(c_ref, w_ref, b_ref, o_ref):
    cond = _silu(c_ref[...])
    o_ref[...] = jnp.dot(cond, w_ref[...], preferred_element_type=F32, precision=HIGHEST) + b_ref[...]


def _mod_call(c, w_ada, b_ada):
    bsz, d = c.shape
    n = w_ada.shape[1]
    return pl.pallas_call(
        _mod_kernel,
        out_shape=jax.ShapeDtypeStruct((bsz, n), F32),
        grid=(n // d,),
        in_specs=[pl.BlockSpec((bsz, d), lambda j: (0, 0)),
                  pl.BlockSpec((d, d), lambda j: (0, j)),
                  pl.BlockSpec((1, d), lambda j: (0, j))],
        out_specs=pl.BlockSpec((bsz, d), lambda j: (0, j)),
        compiler_params=_params(("arbitrary",)),
        name="mod",
    )(c, w_ada, b_ada.reshape(1, n))


def _inproj_kernel(x_ref, nw_ref, sc_ref, sh_ref, cos_ref, sin_ref, w_ref, wdt_ref, cw_ref, cb_ref,
                   o_ref, dt_ref, h_s, work, carry, *, conv_j0, conv_nj, silu_j, sigm_j, tiles_per_seq,
                   tm, tn, sub):
    i = pl.program_id(0)
    j = pl.program_id(1)
    n_dt = dt_ref.shape[1]
    rows = min(tm, 256)

    @pl.when(j == 0)
    def _():
        xf = x_ref[...]
        ms = jnp.mean(xf * xf, axis=-1, keepdims=True)
        y = xf * lax.rsqrt(ms + EPS) * nw_ref[...]
        hm = y * (1.0 + sc_ref[0]) + sh_ref[0]
        hb = hm.astype(BF16)
        h_s[...] = hb
        h_lo = (hm - hb.astype(F32)).astype(BF16)
        d_hi = jnp.dot(hb, wdt_ref[...], preferred_element_type=F32)
        d_lo = jnp.dot(h_lo, wdt_ref[:, :n_dt], preferred_element_type=F32)
        dt_ref[...] = d_hi[:, :n_dt] + d_hi[:, n_dt:] + d_lo
        half = RET_QK_DIM // 2
        for p in range(tn // sub):
            for r in range(tm // rows):
                rs = slice(r * rows, (r + 1) * rows)
                acc = jnp.dot(h_s[rs, :], w_ref[:, p * sub:(p + 1) * sub], preferred_element_type=F32)
                cos = cos_ref[rs, :]
                sin = sin_ref[rs, :]
                for cc in range(sub // RET_QK_DIM):
                    c = p * (sub // RET_QK_DIM) + cc
                    a = acc[:, cc * RET_QK_DIM: cc * RET_QK_DIM + half]
                    b = acc[:, cc * RET_QK_DIM + half: (cc + 1) * RET_QK_DIM]
                    scale = 1.0 if c < RET_HEADS else RET_QK_DIM ** -0.5
                    o_ref[rs, c * RET_QK_DIM: c * RET_QK_DIM + half] = ((a * cos - b * sin) * scale).astype(BF16)
                    o_ref[rs, c * RET_QK_DIM + half: (c + 1) * RET_QK_DIM] = (
                        (a * sin + b * cos) * scale).astype(BF16)

    is_conv = (j >= conv_j0) & (j < conv_j0 + conv_nj)

    @pl.when(is_conv)
    def _():
        cj = j - conv_j0
        pad = SUBLANES

        @pl.when(i % tiles_per_seq == 0)
        def _():
            carry[cj] = jnp.zeros(carry.shape[1:], F32)

        for p in range(tn // sub):
            for r in range(tm // rows):
                r0 = r * rows
                acc = jnp.dot(h_s[r0:r0 + rows, :], w_ref[:, p * sub:(p + 1) * sub], preferred_element_type=F32)
                for cc in range(sub // LANES):
                    c = p * (sub // LANES) + cc
                    cols = slice(c * LANES, (c + 1) * LANES)
                    if r == 0:
                        work[c, 0:pad, :] = carry[cj, c]
                    lo = pad + r0
                    work[c, lo:lo + rows, :] = acc[:, cc * LANES:(cc + 1) * LANES]
                    conv = cb_ref[:, cols] + cw_ref[SSM_CONV - 1:SSM_CONV, cols] * work[c, lo:lo + rows, :]
                    for k in range(SSM_CONV - 1):
                        shift = SSM_CONV - 1 - k
                        conv = conv + cw_ref[k:k + 1, cols] * work[c, lo - shift:lo - shift + rows, :]
                    if r0 + rows == tm:
                        carry[cj, c] = work[c, tm:tm + pad, :]
                    o_ref[r0:r0 + rows, cols] = _silu(conv).astype(BF16)

    def plain(act):
        for p in range(tn // sub):
            for r in range(tm // rows):
                acc = jnp.dot(h_s[r * rows:(r + 1) * rows, :], w_ref[:, p * sub:(p + 1) * sub],
                              preferred_element_type=F32)
                o_ref[r * rows:(r + 1) * rows, p * sub:(p + 1) * sub] = act(acc).astype(BF16)

    is_silu = (j >= silu_j[0]) & (j < silu_j[1])
    is_sigm = (j >= sigm_j[0]) & (j < sigm_j[1])
    pl.when(is_silu)(lambda: plain(_silu))
    pl.when(is_sigm)(lambda: plain(jax.nn.sigmoid))
    pl.when((j != 0) & jnp.logical_not(is_conv | is_silu | is_sigm))(lambda: plain(lambda v: v))


def _inproj_call(x2, row0, t, norm_w, mod3, cos, sin, w_main, w_dt, conv_w, conv_b, conv_off, seq, tm, tn):
    d = x2.shape[1]
    n = w_main.shape[1]
    conv_dim = conv_w.shape[1]
    tiles_per_seq = seq // tm
    off = row0 // tm
    assert tn == 2 * RET_HEADS * RET_QK_DIM, "rotary epilogue expects q and k in the first column tile"
    assert conv_off % tn == 0 and conv_dim % tn == 0
    conv_j0, conv_nj = conv_off // tn, conv_dim // tn
    sub = 512
    g_off = 2 * RET_HEADS * RET_QK_DIM + RET_HEADS * RET_V_DIM
    assert g_off % tn == 0 and (conv_off - g_off) % tn == 0 and (n - conv_off - conv_dim) % tn == 0
    silu_j = (g_off // tn, conv_off // tn)
    sigm_j = ((conv_off + conv_dim) // tn, n // tn)
    kern = functools.partial(_inproj_kernel, conv_j0=conv_j0, conv_nj=conv_nj, silu_j=silu_j, sigm_j=sigm_j,
                             tiles_per_seq=tiles_per_seq, tm=tm, tn=tn, sub=sub)
    conv_idx = lambda i, j: (0, jnp.clip(j - conv_j0, 0, conv_nj - 1))
    return pl.pallas_call(
        kern,
        out_shape=(jax.ShapeDtypeStruct((t, n), BF16), jax.ShapeDtypeStruct((t, LANES), F32)),
        grid=(t // tm, n // tn),
        in_specs=[
            pl.BlockSpec((tm, d), lambda i, j: (i + off, 0)),
            pl.BlockSpec((1, d), lambda i, j: (0, 0)),
            pl.BlockSpec((1, 1, d), lambda i, j: (((i + off) // tiles_per_seq) * N_MOD + 1, 0, 0)),
            pl.BlockSpec((1, 1, d), lambda i, j: (((i + off) // tiles_per_seq) * N_MOD + 0, 0, 0)),
            pl.BlockSpec((tm, LANES), lambda i, j: (i % tiles_per_seq, 0)),
            pl.BlockSpec((tm, LANES), lambda i, j: (i % tiles_per_seq, 0)),
            pl.BlockSpec((d, tn), lambda i, j: (0, j)),
            pl.BlockSpec((d, 2 * LANES), lambda i, j: (0, 0)),
            pl.BlockSpec((SSM_CONV, tn), conv_idx),
            pl.BlockSpec((1, tn), conv_idx),
        ],
        out_specs=(pl.BlockSpec((tm, tn), lambda i, j: (i, j)),
                   pl.BlockSpec((tm, LANES), lambda i, j: (i, 0))),
        scratch_shapes=[pltpu.VMEM((tm, d), BF16),
                        pltpu.VMEM((tn // LANES, tm + SUBLANES, LANES), F32),
                        pltpu.VMEM((conv_nj, tn // LANES, SUBLANES, LANES), F32)],
        compiler_params=_params(("arbitrary", "arbitrary")),
        name="inproj",
    )(x2, norm_w, mod3, mod3, cos, sin, w_main, w_dt, conv_w, conv_b.reshape(1, conv_dim))


def _retention_kernel(q_ref, k_ref, v_ref, g_ref, din_ref, dq_ref, dk_ref, w_ref, o_ref, state,
                      *, decay_c):
    c = pl.program_id(1)

    @pl.when(c == 0)
    def _():
        state[...] = jnp.zeros_like(state)

    acc = None
    for h in range(RET_HEADS):
        qh = q_ref[:, h * RET_QK_DIM:(h + 1) * RET_QK_DIM]
        kh = k_ref[:, h * RET_QK_DIM:(h + 1) * RET_QK_DIM]
        vh = v_ref[:, h * RET_V_DIM:(h + 1) * RET_V_DIM]
        scores = _nt_dot(qh, kh) * din_ref[h]
        inner = jnp.dot(scores.astype(BF16), vh, preferred_element_type=F32)
        st = state[h]
        cross = jnp.dot(qh, st.astype(BF16), preferred_element_type=F32) * dq_ref[h]
        kd = (kh.astype(F32) * dk_ref[h]).astype(BF16)
        state[h] = st * decay_c[h] + _tn_dot(kd, vh)
        ret = inner + cross
        ret = ret * lax.rsqrt(jnp.mean(ret * ret, axis=-1, keepdims=True) + EPS)
        ret = ret * g_ref[:, h * RET_V_DIM:(h + 1) * RET_V_DIM].astype(F32)
        part = jnp.dot(ret.astype(BF16), w_ref[h * RET_V_DIM:(h + 1) * RET_V_DIM, :],
                       preferred_element_type=F32)
        acc = part if acc is None else acc + part
    o_ref[...] = acc.astype(o_ref.dtype)


def _retention_tables(chunk):
    lg = np.log(1.0 - 2.0 ** (-5.0 - np.arange(RET_HEADS, dtype=np.float64)))
    idx = np.arange(chunk, dtype=np.float64)
    rel = idx[:, None] - idx[None, :]
    causal = rel >= 0
    din = np.where(causal[None], np.exp(np.where(causal, rel, 0.0)[None] * lg[:, None, None]), 0.0)
    dq = np.exp((idx + 1.0)[None, :, None] * lg[:, None, None])
    dk = np.exp((chunk - 1.0 - idx)[None, :, None] * lg[:, None, None])
    dc = tuple(float(v) for v in np.exp(chunk * lg))
    return (jnp.asarray(din, F32), jnp.asarray(dq, F32), jnp.asarray(dk, F32), dc)


def _retention_call(proj, w_ret, bsz, seq, chunk):
    t = proj.shape[0]
    d = w_ret.shape[1]
    nc = seq // chunk
    qk_w = RET_HEADS * RET_QK_DIM
    v_w = RET_HEADS * RET_V_DIM
    din, dq, dk, dc = _retention_tables(chunk)
    kern = functools.partial(_retention_kernel, decay_c=dc)
    row = lambda b, c: b * nc + c
    return pl.pallas_call(
        kern,
        out_shape=jax.ShapeDtypeStruct((t, d), BF16),
        grid=(bsz, nc),
        in_specs=[
            pl.BlockSpec((chunk, qk_w), lambda b, c: (row(b, c), 0)),
            pl.BlockSpec((chunk, qk_w), lambda b, c: (row(b, c), 1)),
            pl.BlockSpec((chunk, v_w), lambda b, c: (row(b, c), 1)),
            pl.BlockSpec((chunk, v_w), lambda b, c: (row(b, c), 2)),
            pl.BlockSpec((RET_HEADS, chunk, chunk), lambda b, c: (0, 0, 0)),
            pl.BlockSpec((RET_HEADS, chunk, 1), lambda b, c: (0, 0, 0)),
            pl.BlockSpec((RET_HEADS, chunk, 1), lambda b, c: (0, 0, 0)),
            pl.BlockSpec((v_w, d), lambda b, c: (0, 0)),
        ],
        out_specs=pl.BlockSpec((chunk, d), lambda b, c: (row(b, c), 0)),
        scratch_shapes=[pltpu.VMEM((RET_HEADS, RET_QK_DIM, RET_V_DIM), F32)],
        compiler_params=_params(("arbitrary", "arbitrary")),
        name="retention",
    )(proj, proj, proj, proj, din, dq, dk, w_ret)


def _ssd_kernel(z_ref, xbc_ref, dt_ref, dtb_ref, alog_ref, dsk_ref, nw_ref,
                tril_ref, exp_ref, w_ref, o_ref, state, yn_s, *, chunk, d_inner):
    c = pl.program_id(1)
    heads_per_group = d_inner // SSM_HEAD_DIM // SSM_GROUPS
    gw = heads_per_group * SSM_HEAD_DIM
    assert SSM_HEAD_DIM * 2 == LANES and gw == 2 * LANES

    @pl.when(c == 0)
    def _():
        state[...] = jnp.zeros_like(state)

    dt = jax.nn.softplus(dt_ref[...] + dtb_ref[...])
    a = -jnp.exp(alog_ref[...])
    adt = dt * a
    acs = jnp.dot(tril_ref[...], adt, preferred_element_type=F32, precision=HIGHEST)
    acs_t = acs.T
    dt_x = jnp.dot(dt.astype(BF16), exp_ref[...], preferred_element_type=F32)
    li = lax.broadcasted_iota(jnp.int32, (chunk, chunk), 0)
    si = lax.broadcasted_iota(jnp.int32, (chunk, chunk), 1)
    causal = li >= si
    low_half = si < SSM_HEAD_DIM
    lane_g = lax.broadcasted_iota(jnp.int32, (chunk, gw), 1)

    b_off = d_inner
    c_off = d_inner + SSM_GROUPS * SSM_STATE
    for g in range(SSM_GROUPS):
        bm = xbc_ref[:, b_off + g * SSM_STATE: b_off + (g + 1) * SSM_STATE]
        cm = xbc_ref[:, c_off + g * SSM_STATE: c_off + (g + 1) * SSM_STATE]
        xs_g = xbc_ref[:, g * gw:(g + 1) * gw].astype(F32)
        xdt_g = xs_g * dt_x[:, g * gw:(g + 1) * gw]
        cb = _nt_dot(cm, bm)
        cols, ms, xm = [], [], []
        for jh in range(heads_per_group):
            h = g * heads_per_group + jh
            col = jnp.broadcast_to(acs[:, h:h + 1], (chunk, chunk))
            seg = jnp.exp(jnp.where(causal, col - acs_t[h:h + 1, :], -jnp.inf))
            cols.append(col)
            ms.append((cb * seg).astype(BF16))
            in_head = (lane_g >= jh * SSM_HEAD_DIM) & (lane_g < (jh + 1) * SSM_HEAD_DIM)
            xm.append(jnp.where(in_head, xdt_g, 0.0).astype(BF16))
        y_diag = jnp.dot(jnp.concatenate(ms, axis=-1), jnp.concatenate(xm, axis=0),
                         preferred_element_type=F32)
        a_x = jnp.concatenate([jnp.where(low_half, cols[0], cols[1]),
                               jnp.where(low_half, cols[2], cols[3])], axis=-1)
        e_acs_x = jnp.exp(a_x)
        a_last_x = a_x[chunk - 1:chunk, :]
        st = state[g]
        y_off = jnp.dot(cm, st.astype(BF16), preferred_element_type=F32) * e_acs_x
        xdec = (xdt_g * jnp.exp(a_last_x - a_x)).astype(BF16)
        state[g] = st * e_acs_x[chunk - 1:chunk, :] + _tn_dot(bm, xdec)
        y = y_diag + y_off + dsk_ref[:, g * gw:(g + 1) * gw] * xs_g
        yz = y * z_ref[:, g * gw:(g + 1) * gw].astype(F32)
        yn = yz * lax.rsqrt(jnp.mean(yz * yz, axis=-1, keepdims=True) + EPS) * nw_ref[:, g * gw:(g + 1) * gw]
        yn_s[:, g * gw:(g + 1) * gw] = yn.astype(BF16)
    o_ref[...] = jnp.dot(yn_s[...], w_ref[...], preferred_element_type=F32).astype(o_ref.dtype)


def _ssd_call(proj, dt_raw, dt_bias, a_log, d_skip, ssm_norm, w_ssm, bsz, seq, chunk):
    t = proj.shape[0]
    d_inner, d = w_ssm.shape
    conv_dim = d_inner + 2 * SSM_GROUPS * SSM_STATE
    n_heads = d_inner // SSM_HEAD_DIM
    nc = seq // chunk
    gw = d_inner // SSM_GROUPS
    pad_h = lambda v: jnp.pad(v.astype(F32), (0, LANES - n_heads)).reshape(1, LANES)
    assert chunk == LANES, "the per-head decay tiles are built lane-for-lane against the chunk"
    tril = jnp.asarray(np.tril(np.ones((chunk, chunk), np.float32)))
    expand = np.zeros((LANES, d_inner), np.float32)
    for h in range(n_heads):
        expand[h, h * SSM_HEAD_DIM:(h + 1) * SSM_HEAD_DIM] = 1.0
    expand = jnp.asarray(expand, BF16)
    kern = functools.partial(_ssd_kernel, chunk=chunk, d_inner=d_inner)
    row = lambda b, c: b * nc + c
    z_blk = (2 * RET_HEADS * RET_QK_DIM + 2 * RET_HEADS * RET_V_DIM) // d_inner
    xbc_blk = (2 * RET_HEADS * RET_QK_DIM + 2 * RET_HEADS * RET_V_DIM + d_inner) // conv_dim
    full = lambda shape: pl.BlockSpec(shape, lambda b, c: (0,) * len(shape))
    return pl.pallas_call(
        kern,
        out_shape=jax.ShapeDtypeStruct((t, d), BF16),
        grid=(bsz, nc),
        in_specs=[
            pl.BlockSpec((chunk, d_inner), lambda b, c: (row(b, c), z_blk)),
            pl.BlockSpec((chunk, conv_dim), lambda b, c: (row(b, c), xbc_blk)),
            pl.BlockSpec((chunk, LANES), lambda b, c: (row(b, c), 0)),
            full((1, LANES)), full((1, LANES)),
            full((1, d_inner)), full((1, d_inner)), full((chunk, chunk)), full((LANES, d_inner)),
            full((d_inner, d)),
        ],
        out_specs=pl.BlockSpec((chunk, d), lambda b, c: (row(b, c), 0)),
        scratch_shapes=[pltpu.VMEM((SSM_GROUPS, SSM_STATE, gw), F32),
                        pltpu.VMEM((chunk, d_inner), BF16)],
        compiler_params=_params(("arbitrary", "arbitrary")),
        name="ssd",
    )(proj, proj, dt_raw, pad_h(dt_bias), pad_h(a_log),
      jnp.repeat(d_skip.astype(F32), SSM_HEAD_DIM).reshape(1, d_inner), ssm_norm.reshape(1, d_inner),
      tril, expand, w_ssm)


def _merge_kernel(ya_ref, yb_ref, ga_ref, gb_ref, x_ref, gm_ref, scf_ref, shf_ref, nw_ref, wo_ref,
                  wr_ref, br_ref, tri_ref,
                  x1_ref, h2_ref, idx_ref, rank_ref, prow_ref, cnt_ref, cnt_s, *, tm):
    i = pl.program_id(0)

    @pl.when(i == 0)
    def _():
        cnt_s[...] = jnp.zeros_like(cnt_s)

    merged = (ga_ref[...].astype(F32) * ya_ref[...].astype(F32)
              + gb_ref[...].astype(F32) * yb_ref[...].astype(F32))
    mo = jnp.dot(merged.astype(BF16), wo_ref[...], preferred_element_type=F32)
    x1 = x_ref[...] + gm_ref[0] * mo
    x1_ref[...] = x1
    ms = jnp.mean(x1 * x1, axis=-1, keepdims=True)
    h2 = x1 * lax.rsqrt(ms + EPS) * nw_ref[...] * (1.0 + scf_ref[0]) + shf_ref[0]
    _store_slabs(h2_ref, h2, tm)

    h_hi = h2.astype(BF16)
    h_lo = (h2 - h_hi.astype(F32)).astype(BF16)
    lg2 = _nt_dot(wr_ref[...], h_hi)
    lg = lg2[:N_EXPERTS] + lg2[N_EXPERTS:] + _nt_dot(wr_ref[:N_EXPERTS, :], h_lo) + br_ref[...]
    sub = lax.broadcasted_iota(jnp.int32, lg.shape, 0)
    work = lg
    vals, idxs, sels = [], [], []
    for _ in range(TOP_K):
        m = jnp.max(work, axis=0, keepdims=True)
        ik = jnp.min(jnp.where(work == m, sub, N_EXPERTS), axis=0, keepdims=True)
        sel = sub == ik
        vals.append(m)
        idxs.append(ik)
        sels.append(sel)
        work = jnp.where(sel, -jnp.inf, work)
    exps = [jnp.exp(v - vals[0]) for v in vals]
    denom = exps[0]
    for e in exps[1:]:
        denom = denom + e
    probs = [e / denom for e in exps]

    base = cnt_s[:, 0:1]
    ranks = []
    for k in range(TOP_K):
        mk = jnp.where(sels[k], 1.0, 0.0)
        pre = jnp.dot(mk.astype(BF16), tri_ref[...], preferred_element_type=F32)
        ranks.append(jnp.sum(jnp.where(sels[k], pre + base, 0.0), axis=0, keepdims=True))
        base = base + jnp.sum(mk, axis=1, keepdims=True)
    cnt_s[...] = jnp.broadcast_to(base, cnt_s.shape)
    cnt_ref[...] = cnt_s[...].astype(jnp.int32)

    zi = jnp.zeros((SUBLANES - TOP_K, tm), jnp.int32)
    idx_ref[0] = jnp.concatenate(idxs + [zi], axis=0)
    rank_ref[0] = jnp.concatenate([r.astype(jnp.int32) for r in ranks] + [zi], axis=0)
    pt = jnp.concatenate(probs + [jnp.zeros((LANES - TOP_K, tm), F32)], axis=0)
    prow_ref[...] = pt.T


def _merge_call(ya, yb, proj, x2, row0, mod3, norm_w, w_out, w_router_t, b_router, seq, tm):
    t, d = ya.shape
    nt = t // tm
    tiles_per_seq = seq // tm
    off = row0 // tm
    ga_blk = proj.shape[1] // d - 2
    tri = jnp.asarray(np.triu(np.ones((tm, tm), np.float32), 1), BF16)
    kern = functools.partial(_merge_kernel, tm=tm)
    modspec = lambda m: pl.BlockSpec((1, 1, d), lambda i: (((i + off) // tiles_per_seq) * N_MOD + m, 0, 0))
    return pl.pallas_call(
        kern,
        out_shape=(jax.ShapeDtypeStruct((t, d), F32), jax.ShapeDtypeStruct((t * SLAB_ROWS, LANES), jnp.uint32),
                   jax.ShapeDtypeStruct((nt, SUBLANES, tm), jnp.int32),
                   jax.ShapeDtypeStruct((nt, SUBLANES, tm), jnp.int32),
                   jax.ShapeDtypeStruct((t, LANES), F32),
                   jax.ShapeDtypeStruct((N_EXPERTS, LANES), jnp.int32)),
        grid=(nt,),
        in_specs=[
            pl.BlockSpec((tm, d), lambda i: (i, 0)),
            pl.BlockSpec((tm, d), lambda i: (i, 0)),
            pl.BlockSpec((tm, d), lambda i: (i, ga_blk)),
            pl.BlockSpec((tm, d), lambda i: (i, ga_blk + 1)),
            pl.BlockSpec((tm, d), lambda i: (i + off, 0)),
            modspec(2), modspec(4), modspec(3),
            pl.BlockSpec((1, d), lambda i: (0, 0)),
            pl.BlockSpec((d, d), lambda i: (0, 0)),
            pl.BlockSpec((2 * N_EXPERTS, d), lambda i: (0, 0)),
            pl.BlockSpec((N_EXPERTS, 1), lambda i: (0, 0)),
            pl.BlockSpec((tm, tm), lambda i: (0, 0)),
        ],
        out_specs=(pl.BlockSpec((tm, d), lambda i: (i, 0)),
                   pl.BlockSpec((tm * SLAB_ROWS, LANES), lambda i: (i, 0)),
                   pl.BlockSpec((1, SUBLANES, tm), lambda i: (i, 0, 0)),
                   pl.BlockSpec((1, SUBLANES, tm), lambda i: (i, 0, 0)),
                   pl.BlockSpec((tm, LANES), lambda i: (i, 0)),
                   pl.BlockSpec((N_EXPERTS, LANES), lambda i: (0, 0))),
        scratch_shapes=[pltpu.VMEM((N_EXPERTS, LANES), F32)],
        compiler_params=_params(("arbitrary",)),
        name="merge",
    )(ya, yb, proj, proj, x2, mod3, mod3, mod3, norm_w, w_out, w_router_t,
      b_router.reshape(N_EXPERTS, 1), tri)


def _sc_mesh():
    return plsc.VectorSubcoreMesh(core_axis_name="c", subcore_axis_name="s")


def _sc_worker():
    return lax.axis_index("s") * SC_CORES + lax.axis_index("c")


def _sc_scatter_rows(rows, dest, n_out):
    t = rows.shape[0]
    n_k = dest.shape[0]
    g = SC_GROUP
    n_w = SC_CORES * SC_SUBCORES
    assert t % (n_w * g) == 0
    cpw = t // (n_w * g)
    dest_w = dest.reshape(n_k, n_w, cpw, g).transpose(1, 0, 2, 3)

    @functools.partial(
        pl.kernel, mesh=_sc_mesh(),
        out_type=jax.ShapeDtypeStruct((n_out,) + rows.shape[1:], rows.dtype),
        scratch_types=[pltpu.VMEM((n_k, cpw, g), jnp.int32),
                       pltpu.VMEM((g,) + rows.shape[1:], rows.dtype),
                       pltpu.SemaphoreType.DMA],
    )
    def scatter(rows_hbm, dest_hbm, out_hbm, idx_v, rows_v, sem):
        wid = _sc_worker()
        pltpu.sync_copy(dest_hbm.at[wid], idx_v)

        @pl.loop(0, cpw)
        def _(cc):
            r0 = pl.multiple_of((wid * cpw + cc) * g, g)
            pltpu.sync_copy(rows_hbm.at[pl.ds(r0, g)], rows_v)
            copies = [pltpu.async_copy(rows_v, out_hbm.at[idx_v.at[k, cc]], sem) for k in range(n_k)]
            for cp in copies:
                cp.wait()

    return scatter(rows, dest_w)


def _sc_gather_rows(table, idx):
    m = idx.shape[0]
    g = SC_GROUP
    n_w = SC_CORES * SC_SUBCORES
    assert m % (n_w * g) == 0
    per_w = m // n_w

    @functools.partial(
        pl.kernel, mesh=_sc_mesh(),
        out_type=jax.ShapeDtypeStruct((m,) + table.shape[1:], table.dtype),
        scratch_types=[pltpu.VMEM((per_w,), jnp.int32),
                       pltpu.VMEM((g,) + table.shape[1:], table.dtype),
                       pltpu.SemaphoreType.DMA],
    )
    def gather(table_hbm, idx_hbm, out_hbm, idx_v, rows_v, sem):
        base = _sc_worker() * per_w
        pltpu.sync_copy(idx_hbm.at[pl.ds(base, per_w)], idx_v)

        @pl.loop(0, per_w // g)
        def _(cc):
            off = pl.multiple_of(cc * g, g)
            pltpu.async_copy(table_hbm.at[idx_v.at[pl.ds(off, g)]], rows_v, sem).wait()
            pltpu.sync_copy(rows_v, out_hbm.at[pl.ds(base + off, g)])

    return gather(table, idx)


def _ffn_kernel(be_ref, br_ref, bv_ref, x_ref, wgu_ref, bgu_ref, wd_ref, bd_ref, o_ref, wgu_s, wd_s,
                *, bm, d_ff):
    i = pl.program_id(0)

    @pl.when((i == 0) | (be_ref[i] != be_ref[jnp.maximum(i - 1, 0)]))
    def _():
        wgu_s[...] = wgu_ref[0].astype(BF16)
        wd_s[...] = wd_ref[0].astype(BF16)

    @pl.when(bv_ref[i] == 1)
    def _():
        n_piece = bm // FFN_ROWS
        for r in range(n_piece):
            base = r * FFN_ROWS * SLAB_ROWS
            x = jnp.concatenate(_load_slabs(x_ref, FFN_ROWS, base), axis=-1).astype(BF16)
            gu = jnp.dot(x, wgu_s[...], preferred_element_type=F32) + bgu_ref[0]
            gate = jnp.minimum(gu[:, :d_ff], SWIGLU_LIMIT)
            up = jnp.clip(gu[:, d_ff:], -SWIGLU_LIMIT, SWIGLU_LIMIT)
            act = gate * jax.nn.sigmoid(SWIGLU_ALPHA * gate) * (up + 1.0)
            y = jnp.dot(act.astype(BF16), wd_s[...], preferred_element_type=F32) + bd_ref[0]
            _store_slabs(o_ref.at[pl.ds(base, FFN_ROWS * SLAB_ROWS), :], y, FFN_ROWS)

    @pl.when(bv_ref[i] == 0)
    def _():
        o_ref[...] = jnp.zeros_like(o_ref)


def _ffn_call(blk_e, blk_row, blk_valid, xs, w_gu, b_gu, w_d, b_d, bm):
    n_e, d, f2 = w_gu.shape
    d_ff = f2 // 2
    nb = blk_e.shape[0]
    kern = functools.partial(_ffn_kernel, bm=bm, d_ff=d_ff)
    gs = pltpu.PrefetchScalarGridSpec(
        num_scalar_prefetch=3,
        grid=(nb,),
        in_specs=[pl.BlockSpec((bm * SLAB_ROWS, LANES), lambda i, be, br, bv: (br[i], 0)),
                  pl.BlockSpec((1, d, f2), lambda i, be, br, bv: (be[i], 0, 0)),
                  pl.BlockSpec((1, 1, f2), lambda i, be, br, bv: (be[i], 0, 0)),
                  pl.BlockSpec((1, d_ff, d), lambda i, be, br, bv: (be[i], 0, 0)),
                  pl.BlockSpec((1, 1, d), lambda i, be, br, bv: (be[i], 0, 0))],
        out_specs=pl.BlockSpec((bm * SLAB_ROWS, LANES), lambda i, be, br, bv: (i, 0)),
        scratch_shapes=[pltpu.VMEM((d, f2), BF16), pltpu.VMEM((d_ff, d), BF16)],
    )
    return pl.pallas_call(
        kern,
        out_shape=jax.ShapeDtypeStruct((nb * bm * SLAB_ROWS, LANES), jnp.uint32),
        grid_spec=gs,
        compiler_params=_params(("arbitrary",)),
        name="ffn",
    )(blk_e, blk_row, blk_valid, xs, w_gu, b_gu.reshape(n_e, 1, f2), w_d, b_d.reshape(n_e, 1, d))


def _combine_kernel(y0_ref, y1_ref, y2_ref, y3_ref, prow_ref, x1_ref, gf_ref, nw_ref, *rest, tm):
    o_ref = rest[-1]
    p = prow_ref[...]
    pieces = [_load_slabs(y_ref, tm) for y_ref in (y0_ref, y1_ref, y2_ref, y3_ref)]
    for s in range(len(pieces[0])):
        moe = None
        for k in range(TOP_K):
            piece = pieces[k][s] * p[:, k:k + 1]
            moe = piece if moe is None else moe + piece
        sl = slice(s * LANES, (s + 1) * LANES)
        o_ref[:, sl] = x1_ref[:, sl] + gf_ref[0][:, sl] * moe
    xo = o_ref[...]
    o_ref[...] = xo * lax.rsqrt(jnp.mean(xo * xo, axis=-1, keepdims=True) + EPS) * nw_ref[...]


def _combine_call(ytok, prow, x1, mod3, norm_final, seq, tm, row0, t_total, out_prev):
    t, d = x1.shape
    nt = t // tm
    tiles_per_seq = seq // tm
    off = row0 // tm
    kern = functools.partial(_combine_kernel, tm=tm)
    yspec = lambda k: pl.BlockSpec((tm * SLAB_ROWS, LANES), lambda i: (k * nt + i, 0))
    in_specs = [yspec(0), yspec(1), yspec(2), yspec(3),
                pl.BlockSpec((tm, LANES), lambda i: (i, 0)),
                pl.BlockSpec((tm, d), lambda i: (i, 0)),
                pl.BlockSpec((1, 1, d), lambda i: (((i + off) // tiles_per_seq) * N_MOD + 5, 0, 0)),
                pl.BlockSpec((1, d), lambda i: (0, 0))]
    args = [ytok, ytok, ytok, ytok, prow, x1, mod3, norm_final]
    aliases = {}
    if out_prev is not None:
        in_specs.append(pl.BlockSpec(memory_space=pl.ANY))
        aliases = {len(args): 0}
        args.append(out_prev)
    return pl.pallas_call(
        kern,
        out_shape=jax.ShapeDtypeStruct((t_total, d), F32),
        grid=(nt,),
        in_specs=in_specs,
        out_specs=pl.BlockSpec((tm, d), lambda i: (i + off, 0)),
        input_output_aliases=aliases,
        compiler_params=_params(("arbitrary",)),
        name="combine",
    )(*args)


def _plan(seq):
    def fit(pref):
        tm = min(pref, seq)
        assert seq % tm == 0
        return tm
    return dict(tm_in=fit(1024), tm_merge=fit(512), tm_moe=fit(512),
                ret_chunk=fit(RET_CHUNK), ssm_chunk=fit(SSM_CHUNK))


def _layer(x2, mod3, bsz, seq, norm_mix, norm_ffn, w_in, conv_w, conv_b, dt_bias, a_log, d_skip, ssm_norm,
           w_ret_out, w_ssm_out, w_out, w_router, b_router, w_gate_up, b_gate_up, w_down, b_down,
           norm_final):
    t, d = x2.shape
    plan = _plan(seq)
    qk_w = RET_HEADS * RET_QK_DIM
    v_w = RET_HEADS * RET_V_DIM
    d_inner = w_ssm_out.shape[0]
    conv_dim = conv_w.shape[1]
    n_heads = d_inner // SSM_HEAD_DIM
    dt_off = 2 * qk_w + 2 * v_w + d_inner + conv_dim

    w_main = jnp.concatenate([w_in[:, :dt_off], w_in[:, dt_off + n_heads:]], axis=1).astype(BF16)
    w_dt = jnp.pad(w_in[:, dt_off:dt_off + n_heads], ((0, 0), (0, LANES - n_heads)))
    w_dt_hi = w_dt.astype(BF16)
    w_dt = jnp.concatenate([w_dt_hi, (w_dt - w_dt_hi.astype(F32)).astype(BF16)], axis=1)
    half = RET_QK_DIM // 2
    inv_freq = ROPE_BASE ** (-jnp.arange(half, dtype=F32) / half)
    ang = jnp.arange(seq, dtype=F32)[:, None] * inv_freq[None, :]
    cos, sin = jnp.cos(ang), jnp.sin(ang)

    w_ret_b, w_ssm_b, w_out_b = w_ret_out.astype(BF16), w_ssm_out.astype(BF16), w_out.astype(BF16)
    w_r_hi = w_router.T.astype(BF16)
    w_router_t = jnp.concatenate([w_r_hi, (w_router.T - w_r_hi.astype(F32)).astype(BF16)], axis=0)
    bm = FFN_BLOCK
    slab = (SLAB_ROWS, LANES)

    def mixer(row0, tg, bg):
        proj, dt_raw = _inproj_call(x2, row0, tg, norm_mix.reshape(1, d), mod3, cos, sin, w_main, w_dt,
                                    conv_w, conv_b, 2 * qk_w + 2 * v_w + d_inner, seq, plan["tm_in"], 2 * qk_w)
        ya = _retention_call(proj, w_ret_b, bg, seq, plan["ret_chunk"])
        yb = _ssd_call(proj, dt_raw, dt_bias, a_log, d_skip, ssm_norm, w_ssm_b, bg, seq, plan["ssm_chunk"])
        x1, h2, idx, rank, prow, cnt = _merge_call(ya, yb, proj, x2, row0, mod3, norm_ffn.reshape(1, d),
                                                   w_out_b, w_router_t, b_router, seq, plan["tm_merge"])
        counts = cnt[:, 0]
        padded = ((counts + bm - 1) // bm) * bm
        pad_end = jnp.cumsum(padded)
        start_pad = pad_end - padded
        n_blocks = (tg * TOP_K) // bm + N_EXPERTS
        e_ids = jnp.arange(N_EXPERTS, dtype=jnp.int32)[:, None, None, None]
        dest = rank + jnp.sum(jnp.where(idx[None] == e_ids, start_pad[:, None, None, None], 0), axis=0)
        dest = dest[:, :TOP_K, :].transpose(1, 0, 2).reshape(TOP_K, tg).astype(jnp.int32)
        n_real = pad_end[-1] // bm
        blk_valid = (jnp.arange(n_blocks) < n_real).astype(jnp.int32)
        blk_row = jnp.minimum(jnp.arange(n_blocks), n_real - 1).astype(jnp.int32)
        blk_e = jnp.minimum(jnp.sum(pad_end[None, :] <= (blk_row * bm)[:, None], axis=1),
                            N_EXPERTS - 1).astype(jnp.int32)
        xs = _sc_scatter_rows(h2.reshape((tg,) + slab), dest, n_blocks * bm)
        return dict(x1=x1, prow=prow, dest=dest, blocks=(blk_e, blk_row, blk_valid), xs=xs, row0=row0)

    def experts(m):
        n_rows = m["xs"].shape[0]
        ys = _ffn_call(*m["blocks"], m["xs"].reshape(n_rows * SLAB_ROWS, LANES),
                       w_gate_up, b_gate_up, w_down, b_down, bm)
        return _sc_gather_rows(ys.reshape((n_rows,) + slab), m["dest"].reshape(-1))

    n_groups = N_GROUPS if bsz % N_GROUPS == 0 else 1
    bg = bsz // n_groups
    tg = bg * seq
    groups = [mixer(g * tg, tg, bg) for g in range(n_groups)]
    ytoks = [experts(m) for m in groups]
    out = None
    for y, m in zip(ytoks, groups):
        out = _combine_call(y.reshape(TOP_K * tg * SLAB_ROWS, LANES), m["prow"], m["x1"], mod3,
                            norm_final.reshape(1, d), seq, plan["tm_moe"], m["row0"], t, out)
    return out


def kernel(x, c, w_ada, b_ada, norm_mix, norm_ffn, w_in, conv_w, conv_b, dt_bias, a_log, d_skip, ssm_norm,
           w_ret_out, w_ssm_out, w_out, w_router, b_router, w_gate_up, b_gate_up, w_down, b_down, norm_final):
    bsz, seq, d = x.shape
    depth = w_ada.shape[0]
    assert depth == 1, "the final norm is fused into the single layer's last kernel"
    x2 = x.reshape(bsz * seq, d)
    l = 0
    mod = _mod_call(c, w_ada[l], b_ada[l])
    mod3 = mod.reshape(bsz * N_MOD, 1, d)
    out = _layer(x2, mod3, bsz, seq, norm_mix[l], norm_ffn[l], w_in[l], conv_w[l], conv_b[l], dt_bias[l],
                 a_log[l], d_skip[l], ssm_norm[l], w_ret_out[l], w_ssm_out[l], w_out[l], w_router[l],
                 b_router[l], w_gate_up[l], b_gate_up[l], w_down[l], b_down[l], norm_final)
    return out.reshape(bsz, seq, d)
```

```python
import functools
import math

import numpy as np
import jax
import jax.numpy as jnp
from jax import lax
from jax.experimental import pallas as pl
from jax.experimental.pallas import tpu as pltpu
from jax.experimental.pallas import tpu_sc as plsc

F32 = jnp.float32
BF16 = jnp.bfloat16
HIGHEST = lax.Precision.HIGHEST

EPS = 1e-6
N_MOD = 6
RET_HEADS = 4
RET_QK_DIM = 256
RET_V_DIM = 512
ROPE_BASE = 10000.0
SSM_HEAD_DIM = 64
SSM_GROUPS = 8
SSM_STATE = 128
SSM_CONV = 4
N_EXPERTS = 32
TOP_K = 4
SWIGLU_LIMIT = 7.0
SWIGLU_ALPHA = 1.702

LANES = 128
SUBLANES = 8
VMEM_LIMIT = 56 * 1024 * 1024

RET_CHUNK = 256
SSM_CHUNK = 128
FFN_BLOCK = 512
SC_CORES = 2
SC_SUBCORES = 16
SC_GROUP = 64
N_GROUPS = 2


def _params(sem, vmem=VMEM_LIMIT):
    return pltpu.CompilerParams(dimension_semantics=sem, vmem_limit_bytes=vmem)


def _nt_dot(a, b, **kw):
    return lax.dot_general(a, b, (((1,), (1,)), ((), ())), preferred_element_type=F32, **kw)


def _tn_dot(a, b, **kw):
    return lax.dot_general(a, b, (((0,), (0,)), ((), ())), preferred_element_type=F32, **kw)


def _silu(v):
    return v * jax.nn.sigmoid(v)


SLAB_ROWS = 4
HIGH_HALF = 0xFFFF0000


def _store_slabs(ref, vals, n):
    for s in range(SLAB_ROWS):
        lo = vals[:, s * LANES:(s + 1) * LANES].astype(BF16).astype(F32)
        hi = vals[:, (s + SLAB_ROWS) * LANES:(s + SLAB_ROWS + 1) * LANES].astype(BF16).astype(F32)
        word = (pltpu.bitcast(lo, jnp.uint32) >> 16) | (pltpu.bitcast(hi, jnp.uint32) & jnp.uint32(HIGH_HALF))
        ref[pl.ds(s, n, stride=SLAB_ROWS), :] = word


def _load_slabs(ref, n, base=0):
    lo, hi = [], []
    for s in range(SLAB_ROWS):
        word = ref[pl.ds(base + s, n, stride=SLAB_ROWS), :]
        lo.append(pltpu.bitcast(word << 16, F32))
        hi.append(pltpu.bitcast(word & jnp.uint32(HIGH_HALF), F32))
    return lo + hi


def _mod_kernel(c_ref, w_ref, b_ref, o_ref):
    cond = _silu(c_ref[...])
    o_ref[...] = jnp.dot(cond, w_ref[...], preferred_element_type=F32, precision=HIGHEST) + b_ref[...]


def _mod_call(c, w_ada, b_ada):
    bsz, d = c.shape
    n = w_ada.shape[1]
    return pl.pallas_call(
        _mod_kernel,
        out_shape=jax.ShapeDtypeStruct((bsz, n), F32),
        grid=(n // d,),
        in_specs=[pl.BlockSpec((bsz, d), lambda j: (0, 0)),
                  pl.BlockSpec((d, d), lambda j: (0, j)),
                  pl.BlockSpec((1, d), lambda j: (0, j))],
        out_specs=pl.BlockSpec((bsz, d), lambda j: (0, j)),
        compiler_params=_params(("arbitrary",)),
        name="mod",
    )(c, w_ada, b_ada.reshape(1, n))


def _inproj_kernel(x_ref, nw_ref, sc_ref, sh_ref, cos_ref, sin_ref, w_ref, wdt_ref, cw_ref, cb_ref,
                   o_ref, dt_ref, h_s, work, carry, *, conv_j0, conv_nj, silu_j, sigm_j, tiles_per_seq,
                   tm, tn, sub):
    i = pl.program_id(0)
    j = pl.program_id(1)
    n_dt = dt_ref.shape[1]
    rows = min(tm, 256)

    @pl.when(j == 0)
    def _():
        xf = x_ref[...]
        ms = jnp.mean(xf * xf, axis=-1, keepdims=True)
        y = xf * lax.rsqrt(ms + EPS) * nw_ref[...]
        hm = y * (1.0 + sc_ref[0]) + sh_ref[0]
        hb = hm.astype(BF16)
        h_s[...] = hb
        h_lo = (hm - hb.astype(F32)).astype(BF16)
        d_hi = jnp.dot(hb, wdt_ref[...], preferred_element_type=F32)
        d_lo = jnp.dot(h_lo, wdt_ref[:, :n_dt], preferred_element_type=F32)
        dt_ref[...] = d_hi[:, :n_dt] + d_hi[:, n_dt:] + d_lo
        half = RET_QK_DIM // 2
        for p in range(tn // sub):
            for r in range(tm // rows):
                rs = slice(r * rows, (r + 1) * rows)
                acc = jnp.dot(h_s[rs, :], w_ref[:, p * sub:(p + 1) * sub], preferred_element_type=F32)
                cos = cos_ref[rs, :]
                sin = sin_ref[rs, :]
                for cc in range(sub // RET_QK_DIM):
                    c = p * (sub // RET_QK_DIM) + cc
                    a = acc[:, cc * RET_QK_DIM: cc * RET_QK_DIM + half]
                    b = acc[:, cc * RET_QK_DIM + half: (cc + 1) * RET_QK_DIM]
                    scale = 1.0 if c < RET_HEADS else RET_QK_DIM ** -0.5
                    o_ref[rs, c * RET_QK_DIM: c * RET_QK_DIM + half] = ((a * cos - b * sin) * scale).astype(BF16)
                    o_ref[rs, c * RET_QK_DIM + half: (c + 1) * RET_QK_DIM] = (
                        (a * sin + b * cos) * scale).astype(BF16)

    is_conv = (j >= conv_j0) & (j < conv_j0 + conv_nj)

    @pl.when(is_conv)
    def _():
        cj = j - conv_j0
        pad = SUBLANES

        @pl.when(i % tiles_per_seq == 0)
        def _():
            carry[cj] = jnp.zeros(carry.shape[1:], F32)

        for p in range(tn // sub):
            for r in range(tm // rows):
                r0 = r * rows
                acc = jnp.dot(h_s[r0:r0 + rows, :], w_ref[:, p * sub:(p + 1) * sub], preferred_element_type=F32)
                for cc in range(sub // LANES):
                    c = p * (sub // LANES) + cc
                    cols = slice(c * LANES, (c + 1) * LANES)
                    if r == 0:
                        work[c, 0:pad, :] = carry[cj, c]
                    lo = pad + r0
                    work[c, lo:lo + rows, :] = acc[:, cc * LANES:(cc + 1) * LANES]
                    conv = cb_ref[:, cols] + cw_ref[SSM_CONV - 1:SSM_CONV, cols] * work[c, lo:lo + rows, :]
                    for k in range(SSM_CONV - 1):
                        shift = SSM_CONV - 1 - k
                        conv = conv + cw_ref[k:k + 1, cols] * work[c, lo - shift:lo - shift + rows, :]
                    if r0 + rows == tm:
                        carry[cj, c] = work[c, tm:tm + pad, :]
                    o_ref[r0:r0 + rows, cols] = _silu(conv).astype(BF16)

    def plain(act):
        for p in range(tn // sub):
            for r in range(tm // rows):
                acc = jnp.dot(h_s[r * rows:(r + 1) * rows, :], w_ref[:, p * sub:(p + 1) * sub],
                              preferred_element_type=F32)
                o_ref[r * rows:(r + 1) * rows, p * sub:(p + 1) * sub] = act(acc).astype(BF16)

    is_silu = (j >= silu_j[0]) & (j < silu_j[1])
    is_sigm = (j >= sigm_j[0]) & (j < sigm_j[1])
    pl.when(is_silu)(lambda: plain(_silu))
    pl.when(is_sigm)(lambda: plain(jax.nn.sigmoid))
    pl.when((j != 0) & jnp.logical_not(is_conv | is_silu | is_sigm))(lambda: plain(lambda v: v))


def _inproj_call(x2, row0, t, norm_w, mod3, cos, sin, w_main, w_dt, conv_w, conv_b, conv_off, seq, tm, tn):
    d = x2.shape[1]
    n = w_main.shape[1]
    conv_dim = conv_w.shape[1]
    tiles_per_seq = seq // tm
    off = row0 // tm
    assert tn == 2 * RET_HEADS * RET_QK_DIM, "rotary epilogue expects q and k in the first column tile"
    assert conv_off % tn == 0 and conv_dim % tn == 0
    conv_j0, conv_nj = conv_off // tn, conv_dim // tn
    sub = 512
    g_off = 2 * RET_HEADS * RET_QK_DIM + RET_HEADS * RET_V_DIM
    assert g_off % tn == 0 and (conv_off - g_off) % tn == 0 and (n - conv_off - conv_dim) % tn == 0
    silu_j = (g_off // tn, conv_off // tn)
    sigm_j = ((conv_off + conv_dim) // tn, n // tn)
    kern = functools.partial(_inproj_kernel, conv_j0=conv_j0, conv_nj=conv_nj, silu_j=silu_j, sigm_j=sigm_j,
                             tiles_per_seq=tiles_per_seq, tm=tm, tn=tn, sub=sub)
    conv_idx = lambda i, j: (0, jnp.clip(j - conv_j0, 0, conv_nj - 1))
    return pl.pallas_call(
        kern,
        out_shape=(jax.ShapeDtypeStruct((t, n), BF16), jax.ShapeDtypeStruct((t, LANES), F32)),
        grid=(t // tm, n // tn),
        in_specs=[
            pl.BlockSpec((tm, d), lambda i, j: (i + off, 0)),
            pl.BlockSpec((1, d), lambda i, j: (0, 0)),
            pl.BlockSpec((1, 1, d), lambda i, j: (((i + off) // tiles_per_seq) * N_MOD + 1, 0, 0)),
            pl.BlockSpec((1, 1, d), lambda i, j: (((i + off) // tiles_per_seq) * N_MOD + 0, 0, 0)),
            pl.BlockSpec((tm, LANES), lambda i, j: (i % tiles_per_seq, 0)),
            pl.BlockSpec((tm, LANES), lambda i, j: (i % tiles_per_seq, 0)),
            pl.BlockSpec((d, tn), lambda i, j: (0, j)),
            pl.BlockSpec((d, 2 * LANES), lambda i, j: (0, 0)),
            pl.BlockSpec((SSM_CONV, tn), conv_idx),
            pl.BlockSpec((1, tn), conv_idx),
        ],
        out_specs=(pl.BlockSpec((tm, tn), lambda i, j: (i, j)),
                   pl.BlockSpec((tm, LANES), lambda i, j: (i, 0))),
        scratch_shapes=[pltpu.VMEM((tm, d), BF16),
                        pltpu.VMEM((tn // LANES, tm + SUBLANES, LANES), F32),
                        pltpu.VMEM((conv_nj, tn // LANES, SUBLANES, LANES), F32)],
        compiler_params=_params(("arbitrary", "arbitrary")),
        name="inproj",
    )(x2, norm_w, mod3, mod3, cos, sin, w_main, w_dt, conv_w, conv_b.reshape(1, conv_dim))


def _retention_body(q_ref, k_ref, v_ref, g_ref, din_ref, dq_ref, dk_ref, w_ref, o_ref, state, decay_c):
    acc = None
    for h in range(RET_HEADS):
        qh = q_ref[:, h * RET_QK_DIM:(h + 1) * RET_QK_DIM]
        kh = k_ref[:, h * RET_QK_DIM:(h + 1) * RET_QK_DIM]
        vh = v_ref[:, h * RET_V_DIM:(h + 1) * RET_V_DIM]
        scores = _nt_dot(qh, kh) * din_ref[h]
        inner = jnp.dot(scores.astype(BF16), vh, preferred_element_type=F32)
        st = state[h]
        cross = jnp.dot(qh, st.astype(BF16), preferred_element_type=F32) * dq_ref[h]
        kd = (kh.astype(F32) * dk_ref[h]).astype(BF16)
        state[h] = st * decay_c[h] + _tn_dot(kd, vh)
        ret = inner + cross
        ret = ret * lax.rsqrt(jnp.mean(ret * ret, axis=-1, keepdims=True) + EPS)
        ret = ret * g_ref[:, h * RET_V_DIM:(h + 1) * RET_V_DIM].astype(F32)
        part = jnp.dot(ret.astype(BF16), w_ref[h * RET_V_DIM:(h + 1) * RET_V_DIM, :],
                       preferred_element_type=F32)
        acc = part if acc is None else acc + part
    o_ref[...] = acc.astype(o_ref.dtype)


def _retention_tables(chunk):
    lg = np.log(1.0 - 2.0 ** (-5.0 - np.arange(RET_HEADS, dtype=np.float64)))
    idx = np.arange(chunk, dtype=np.float64)
    rel = idx[:, None] - idx[None, :]
    causal = rel >= 0
    din = np.where(causal[None], np.exp(np.where(causal, rel, 0.0)[None] * lg[:, None, None]), 0.0)
    dq = np.exp((idx + 1.0)[None, :, None] * lg[:, None, None])
    dk = np.exp((chunk - 1.0 - idx)[None, :, None] * lg[:, None, None])
    dc = tuple(float(v) for v in np.exp(chunk * lg))
    return (jnp.asarray(din, F32), jnp.asarray(dq, F32), jnp.asarray(dk, F32), dc)


def _ssd_body(z_ref, xbc_ref, dt_ref, dtb_ref, alog_ref, dsk_ref, nw_ref,
              tril_ref, exp_ref, w_ref, o_ref, state, yn_s, chunk, d_inner):
    heads_per_group = d_inner // SSM_HEAD_DIM // SSM_GROUPS
    gw = heads_per_group * SSM_HEAD_DIM
    assert SSM_HEAD_DIM * 2 == LANES and gw == 2 * LANES

    dt =jax.nn.softplus(dt_ref[...] + dtb_ref[...])
    a = -jnp.exp(alog_ref[...])
    adt = dt * a
    acs = jnp.dot(tril_ref[...], adt, preferred_element_type=F32, precision=HIGHEST)
    acs_t = acs.T
    dt_x = jnp.dot(dt.astype(BF16), exp_ref[...], preferred_element_type=F32)
    li = lax.broadcasted_iota(jnp.int32, (chunk, chunk), 0)
    si = lax.broadcasted_iota(jnp.int32, (chunk, chunk), 1)
    causal = li >= si
    low_half = si < SSM_HEAD_DIM
    lane_g = lax.broadcasted_iota(jnp.int32, (chunk, gw), 1)

    b_off = d_inner
    c_off = d_inner + SSM_GROUPS * SSM_STATE
    for g in range(SSM_GROUPS):
        bm = xbc_ref[:, b_off + g * SSM_STATE: b_off + (g + 1) * SSM_STATE]
        cm = xbc_ref[:, c_off + g * SSM_STATE: c_off + (g + 1) * SSM_STATE]
        xs_g = xbc_ref[:, g * gw:(g + 1) * gw].astype(F32)
        xdt_g = xs_g * dt_x[:, g * gw:(g + 1) * gw]
        cb = _nt_dot(cm, bm)
        cols, ms, xm = [], [], []
        for jh in range(heads_per_group):
            h = g * heads_per_group + jh
            col = jnp.broadcast_to(acs[:, h:h + 1], (chunk, chunk))
            seg = jnp.exp(jnp.where(causal, col - acs_t[h:h + 1, :], -jnp.inf))
            cols.append(col)
            ms.append((cb * seg).astype(BF16))
            in_head = (lane_g >= jh * SSM_HEAD_DIM) & (lane_g < (jh + 1) * SSM_HEAD_DIM)
            xm.append(jnp.where(in_head, xdt_g, 0.0).astype(BF16))
        y_diag = jnp.dot(jnp.concatenate(ms, axis=-1), jnp.concatenate(xm, axis=0),
                         preferred_element_type=F32)
        a_x = jnp.concatenate([jnp.where(low_half, cols[0], cols[1]),
                               jnp.where(low_half, cols[2], cols[3])], axis=-1)
        e_acs_x = jnp.exp(a_x)
        a_last_x = a_x[chunk - 1:chunk, :]
        st = state[g]
        y_off = jnp.dot(cm, st.astype(BF16), preferred_element_type=F32) * e_acs_x
        xdec = (xdt_g * jnp.exp(a_last_x - a_x)).astype(BF16)
        state[g] = st * e_acs_x[chunk - 1:chunk, :] + _tn_dot(bm, xdec)
        y = y_diag + y_off + dsk_ref[:, g * gw:(g + 1) * gw] * xs_g
        yz = y * z_ref[:, g * gw:(g + 1) * gw].astype(F32)
        yn = yz * lax.rsqrt(jnp.mean(yz * yz, axis=-1, keepdims=True) + EPS) * nw_ref[:, g * gw:(g + 1) * gw]
        yn_s[:, g * gw:(g + 1) * gw] = yn.astype(BF16)
    o_ref[...] = jnp.dot(yn_s[...], w_ref[...], preferred_element_type=F32).astype(o_ref.dtype)


def _mixers_kernel(q_ref, k_ref, v_ref, g_ref, din_ref, dq_ref, dk_ref, wret_ref,
                   z_ref, xbc_ref, dt_ref, dtb_ref, alog_ref, dsk_ref, nw_ref, tril_ref, exp_ref, wssm_ref,
                   ya_ref, yb_ref, rstate, sstate, yn_s, *, decay_c, chunk, ssm_chunk, d_inner):
    c = pl.program_id(1)

    @pl.when(c == 0)
    def _():
        rstate[...] = jnp.zeros_like(rstate)
        sstate[...] = jnp.zeros_like(sstate)

    _retention_body(q_ref, k_ref, v_ref, g_ref, din_ref, dq_ref, dk_ref, wret_ref, ya_ref, rstate, decay_c)
    for sub in range(chunk // ssm_chunk):
        rows = pl.ds(sub * ssm_chunk, ssm_chunk)
        _ssd_body(z_ref.at[rows, :], xbc_ref.at[rows, :], dt_ref.at[rows, :], dtb_ref, alog_ref, dsk_ref,
                  nw_ref, tril_ref, exp_ref, wssm_ref, yb_ref.at[rows, :], sstate, yn_s, ssm_chunk, d_inner)


def _mixers_call(proj, dt_raw, w_ret, dt_bias, a_log, d_skip, ssm_norm, w_ssm, bsz, seq, chunk, ssm_chunk):
    t = proj.shape[0]
    d_inner, d = w_ssm.shape
    conv_dim = d_inner + 2 * SSM_GROUPS * SSM_STATE
    n_heads = d_inner // SSM_HEAD_DIM
    nc = seq // chunk
    gw = d_inner // SSM_GROUPS
    qk_w = RET_HEADS * RET_QK_DIM
    v_w = RET_HEADS * RET_V_DIM
    assert chunk % ssm_chunk == 0
    assert ssm_chunk == LANES, "the per-head decay tiles are built lane-for-lane against the chunk"
    din, dq, dk, dc = _retention_tables(chunk)
    pad_h = lambda v: jnp.pad(v.astype(F32), (0, LANES - n_heads)).reshape(1, LANES)
    tril = jnp.asarray(np.tril(np.ones((ssm_chunk, ssm_chunk), np.float32)))
    expand = np.zeros((LANES, d_inner), np.float32)
    for h in range(n_heads):
        expand[h, h * SSM_HEAD_DIM:(h + 1) * SSM_HEAD_DIM] = 1.0
    expand = jnp.asarray(expand, BF16)
    kern = functools.partial(_mixers_kernel, decay_c=dc, chunk=chunk, ssm_chunk=ssm_chunk, d_inner=d_inner)
    row = lambda b, c: b * nc + c
    z_blk = (2 * qk_w + 2 * v_w) // d_inner
    xbc_blk = (2 * qk_w + 2 * v_w + d_inner) // conv_dim
    full = lambda shape: pl.BlockSpec(shape, lambda b, c: (0,) * len(shape))
    out = jax.ShapeDtypeStruct((t, d), BF16)
    return pl.pallas_call(
        kern,
        out_shape=(out, out),
        grid=(bsz, nc),
        in_specs=[
            pl.BlockSpec((chunk, qk_w), lambda b, c: (row(b, c), 0)),
            pl.BlockSpec((chunk, qk_w), lambda b, c: (row(b, c), 1)),
            pl.BlockSpec((chunk, v_w), lambda b, c: (row(b, c), 1)),
            pl.BlockSpec((chunk, v_w), lambda b, c: (row(b, c), 2)),
            full((RET_HEADS, chunk, chunk)), full((RET_HEADS, chunk, 1)), full((RET_HEADS, chunk, 1)),
            full((v_w, d)),
            pl.BlockSpec((chunk, d_inner), lambda b, c: (row(b, c), z_blk)),
            pl.BlockSpec((chunk, conv_dim), lambda b, c: (row(b, c), xbc_blk)),
            pl.BlockSpec((chunk, LANES), lambda b, c: (row(b, c), 0)),
            full((1, LANES)), full((1, LANES)),
            full((1, d_inner)), full((1, d_inner)), full((ssm_chunk, ssm_chunk)), full((LANES, d_inner)),
            full((d_inner, d)),
        ],
        out_specs=(pl.BlockSpec((chunk, d), lambda b, c: (row(b, c), 0)),
                   pl.BlockSpec((chunk, d), lambda b, c: (row(b, c), 0))),
        scratch_shapes=[pltpu.VMEM((RET_HEADS, RET_QK_DIM, RET_V_DIM), F32),
                        pltpu.VMEM((SSM_GROUPS, SSM_STATE, gw), F32),
                        pltpu.VMEM((ssm_chunk, d_inner), BF16)],
        compiler_params=_params(("arbitrary", "arbitrary")),
        name="mixers",
    )(proj, proj, proj, proj, din, dq, dk, w_ret,
      proj, proj, dt_raw, pad_h(dt_bias), pad_h(a_log),
      jnp.repeat(d_skip.astype(F32), SSM_HEAD_DIM).reshape(1, d_inner), ssm_norm.reshape(1, d_inner),
      tril, expand, w_ssm)


def _merge_kernel(ya_ref, yb_ref, ga_ref, gb_ref, x_ref, gm_ref, scf_ref, shf_ref, nw_ref, wo_ref,
                  wr_ref, br_ref, tri_ref,
                  x1_ref, h2_ref, idx_ref, rank_ref, prow_ref, cnt_ref, cnt_s, *, tm):
    i = pl.program_id(0)

    @pl.when(i == 0)
    def _():
        cnt_s[...] = jnp.zeros_like(cnt_s)

    merged = (ga_ref[...].astype(F32) * ya_ref[...].astype(F32)
              + gb_ref[...].astype(F32) * yb_ref[...].astype(F32))
    mo = jnp.dot(merged.astype(BF16), wo_ref[...], preferred_element_type=F32)
    x1 = x_ref[...] + gm_ref[0] * mo
    x1_ref[...] = x1
    ms = jnp.mean(x1 * x1, axis=-1, keepdims=True)
    h2 = x1 * lax.rsqrt(ms + EPS) * nw_ref[...] * (1.0 + scf_ref[0]) + shf_ref[0]
    _store_slabs(h2_ref, h2, tm)

    h_hi = h2.astype(BF16)
    h_lo = (h2 - h_hi.astype(F32)).astype(BF16)
    lg2 = _nt_dot(wr_ref[...], h_hi)
    lg = lg2[:N_EXPERTS] + lg2[N_EXPERTS:] + _nt_dot(wr_ref[:N_EXPERTS, :], h_lo) + br_ref[...]
    sub = lax.broadcasted_iota(jnp.int32, lg.shape, 0)
    work = lg
    vals, idxs, sels = [], [], []
    for _ in range(TOP_K):
        m = jnp.max(work, axis=0, keepdims=True)
        ik = jnp.min(jnp.where(work == m, sub, N_EXPERTS), axis=0, keepdims=True)
        sel = sub == ik
        vals.append(m)
        idxs.append(ik)
        sels.append(sel)
        work = jnp.where(sel, -jnp.inf, work)
    exps = [jnp.exp(v - vals[0]) for v in vals]
    denom = exps[0]
    for e in exps[1:]:
        denom = denom + e
    probs = [e / denom for e in exps]

    base = cnt_s[:, 0:1]
    ranks = []
    for k in range(TOP_K):
        mk = jnp.where(sels[k], 1.0, 0.0)
        pre = jnp.dot(mk.astype(BF16), tri_ref[...], preferred_element_type=F32)
        ranks.append(jnp.sum(jnp.where(sels[k], pre + base, 0.0), axis=0, keepdims=True))
        base = base + jnp.sum(mk, axis=1, keepdims=True)
    cnt_s[...] = jnp.broadcast_to(base, cnt_s.shape)
    cnt_ref[...] = cnt_s[...].astype(jnp.int32)

    zi = jnp.zeros((SUBLANES - TOP_K, tm), jnp.int32)
    idx_ref[0] = jnp.concatenate(idxs + [zi], axis=0)
    rank_ref[0] = jnp.concatenate([r.astype(jnp.int32) for r in ranks] + [zi], axis=0)
    pt = jnp.concatenate(probs + [jnp.zeros((LANES - TOP_K, tm), F32)], axis=0)
    prow_ref[...] = pt.T


def _merge_call(ya, yb, proj, x2, row0, mod3, norm_w, w_out, w_router_t, b_router, seq, tm):
    t, d = ya.shape
    nt = t // tm
    tiles_per_seq = seq // tm
    off = row0 // tm
    ga_blk = proj.shape[1] // d - 2
    tri = jnp.asarray(np.triu(np.ones((tm, tm), np.float32), 1), BF16)
    kern = functools.partial(_merge_kernel, tm=tm)
    modspec = lambda m: pl.BlockSpec((1, 1, d), lambda i: (((i + off) // tiles_per_seq) * N_MOD + m, 0, 0))
    return pl.pallas_call(
        kern,
        out_shape=(jax.ShapeDtypeStruct((t, d), F32), jax.ShapeDtypeStruct((t * SLAB_ROWS, LANES), jnp.uint32),
                   jax.ShapeDtypeStruct((nt, SUBLANES, tm), jnp.int32),
                   jax.ShapeDtypeStruct((nt, SUBLANES, tm), jnp.int32),
                   jax.ShapeDtypeStruct((t, LANES), F32),
                   jax.ShapeDtypeStruct((N_EXPERTS, LANES), jnp.int32)),
        grid=(nt,),
        in_specs=[
            pl.BlockSpec((tm, d), lambda i: (i, 0)),
            pl.BlockSpec((tm, d), lambda i: (i, 0)),
            pl.BlockSpec((tm, d), lambda i: (i, ga_blk)),
            pl.BlockSpec((tm, d), lambda i: (i, ga_blk + 1)),
            pl.BlockSpec((tm, d), lambda i: (i + off, 0)),
            modspec(2), modspec(4), modspec(3),
            pl.BlockSpec((1, d), lambda i: (0, 0)),
            pl.BlockSpec((d, d), lambda i: (0, 0)),
            pl.BlockSpec((2 * N_EXPERTS, d), lambda i: (0, 0)),
            pl.BlockSpec((N_EXPERTS, 1), lambda i: (0, 0)),
            pl.BlockSpec((tm, tm), lambda i: (0, 0)),
        ],
        out_specs=(pl.BlockSpec((tm, d), lambda i: (i, 0)),
                   pl.BlockSpec((tm * SLAB_ROWS, LANES), lambda i: (i, 0)),
                   pl.BlockSpec((1, SUBLANES, tm), lambda i: (i, 0, 0)),
                   pl.BlockSpec((1, SUBLANES, tm), lambda i: (i, 0, 0)),
                   pl.BlockSpec((tm, LANES), lambda i: (i, 0)),
                   pl.BlockSpec((N_EXPERTS, LANES), lambda i: (0, 0))),
        scratch_shapes=[pltpu.VMEM((N_EXPERTS, LANES), F32)],
        compiler_params=_params(("arbitrary",)),
        name="merge",
    )(ya, yb, proj, proj, x2, mod3, mod3, mod3, norm_w, w_out, w_router_t,
      b_router.reshape(N_EXPERTS, 1), tri)


def _sc_mesh():
    return plsc.VectorSubcoreMesh(core_axis_name="c", subcore_axis_name="s")


def _sc_worker():
    return lax.axis_index("s") * SC_CORES + lax.axis_index("c")


def _sc_scatter_rows(rows, dest, n_out):
    t = rows.shape[0]
    n_k = dest.shape[0]
    g = SC_GROUP
    n_w = SC_CORES * SC_SUBCORES
    assert t % (n_w * g) == 0
    cpw = t // (n_w * g)
    dest_w = dest.reshape(n_k, n_w, cpw, g).transpose(1, 0, 2, 3)

    @functools.partial(
        pl.kernel, mesh=_sc_mesh(),
        out_type=jax.ShapeDtypeStruct((n_out,) + rows.shape[1:], rows.dtype),
        scratch_types=[pltpu.VMEM((n_k, cpw, g), jnp.int32),
                       pltpu.VMEM((g,) + rows.shape[1:], rows.dtype),
                       pltpu.SemaphoreType.DMA],
    )
    def scatter(rows_hbm, dest_hbm, out_hbm, idx_v, rows_v, sem):
        wid = _sc_worker()
        pltpu.sync_copy(dest_hbm.at[wid], idx_v)

        @pl.loop(0, cpw)
        def _(cc):
            r0 = pl.multiple_of((wid * cpw + cc) * g, g)
            pltpu.sync_copy(rows_hbm.at[pl.ds(r0, g)], rows_v)
            copies = [pltpu.async_copy(rows_v, out_hbm.at[idx_v.at[k, cc]], sem) for k in range(n_k)]
            for cp in copies:
                cp.wait()

    return scatter(rows, dest_w)


def _sc_gather_rows(table, idx):
    m = idx.shape[0]
    g = SC_GROUP
    n_w = SC_CORES * SC_SUBCORES
    assert m % (n_w * g) == 0
    per_w = m // n_w

    @functools.partial(
        pl.kernel, mesh=_sc_mesh(),
        out_type=jax.ShapeDtypeStruct((m,) + table.shape[1:], table.dtype),
        scratch_types=[pltpu.VMEM((per_w,), jnp.int32),
                       pltpu.VMEM((g,) + table.shape[1:], table.dtype),
                       pltpu.SemaphoreType.DMA],
    )
    def gather(table_hbm, idx_hbm, out_hbm, idx_v, rows_v, sem):
        base = _sc_worker() * per_w
        pltpu.sync_copy(idx_hbm.at[pl.ds(base, per_w)], idx_v)

        @pl.loop(0, per_w // g)
        def _(cc):
            off = pl.multiple_of(cc * g, g)
            pltpu.async_copy(table_hbm.at[idx_v.at[pl.ds(off, g)]], rows_v, sem).wait()
            pltpu.sync_copy(rows_v, out_hbm.at[pl.ds(base + off, g)])

    return gather(table, idx)


def _ffn_kernel(be_ref, br_ref, bv_ref, x_ref, wgu_ref, bgu_ref, wd_ref, bd_ref, o_ref, wgu_s, wd_s,
                *, bm, d_ff):
    i = pl.program_id(0)

    @pl.when((i == 0) | (be_ref[i] != be_ref[jnp.maximum(i - 1, 0)]))
    def _():
        wgu_s[...] = wgu_ref[0].astype(BF16)
        wd_s[...] = wd_ref[0].astype(BF16)

    @pl.when(bv_ref[i] == 1)
    def _():
        x = jnp.concatenate(_load_slabs(x_ref, bm), axis=-1).astype(BF16)
        gu = jnp.dot(x, wgu_s[...], preferred_element_type=F32) + bgu_ref[0]
        gate = jnp.minimum(gu[:, :d_ff], SWIGLU_LIMIT)
        up = jnp.clip(gu[:, d_ff:], -SWIGLU_LIMIT, SWIGLU_LIMIT)
        act = gate * jax.nn.sigmoid(SWIGLU_ALPHA * gate) * (up + 1.0)
        y = jnp.dot(act.astype(BF16), wd_s[...], preferred_element_type=F32) + bd_ref[0]
        _store_slabs(o_ref, y, bm)

    @pl.when(bv_ref[i] == 0)
    def _():
        o_ref[...] = jnp.zeros_like(o_ref)


def _ffn_call(blk_e, blk_row, blk_valid, xs, w_gu, b_gu, w_d, b_d, bm):
    n_e, d, f2 = w_gu.shape
    d_ff = f2 // 2
    nb = blk_e.shape[0]
    kern = functools.partial(_ffn_kernel, bm=bm, d_ff=d_ff)
    gs = pltpu.PrefetchScalarGridSpec(
        num_scalar_prefetch=3,
        grid=(nb,),
        in_specs=[pl.BlockSpec((bm * SLAB_ROWS, LANES), lambda i, be, br, bv: (br[i], 0)),
                  pl.BlockSpec((1, d, f2), lambda i, be, br, bv: (be[i], 0, 0)),
                  pl.BlockSpec((1, 1, f2), lambda i, be, br, bv: (be[i], 0, 0)),
                  pl.BlockSpec((1, d_ff, d), lambda i, be, br, bv: (be[i], 0, 0)),
                  pl.BlockSpec((1, 1, d), lambda i, be, br, bv: (be[i], 0, 0))],
        out_specs=pl.BlockSpec((bm * SLAB_ROWS, LANES), lambda i, be, br, bv: (i, 0)),
        scratch_shapes=[pltpu.VMEM((d, f2), BF16), pltpu.VMEM((d_ff, d), BF16)],
    )
    return pl.pallas_call(
        kern,
        out_shape=jax.ShapeDtypeStruct((nb * bm * SLAB_ROWS, LANES), jnp.uint32),
        grid_spec=gs,
        compiler_params=_params(("arbitrary",)),
        name="ffn",
    )(blk_e, blk_row, blk_valid, xs, w_gu, b_gu.reshape(n_e, 1, f2), w_d, b_d.reshape(n_e, 1, d))


def _combine_kernel(y0_ref, y1_ref, y2_ref, y3_ref, prow_ref, x1_ref, gf_ref, nw_ref, *rest, tm):
    o_ref = rest[-1]
    p = prow_ref[...]
    pieces = [_load_slabs(y_ref, tm) for y_ref in (y0_ref, y1_ref, y2_ref, y3_ref)]
    for s in range(len(pieces[0])):
        moe = None
        for k in range(TOP_K):
            piece = pieces[k][s] * p[:, k:k + 1]
            moe = piece if moe is None else moe + piece
        sl = slice(s * LANES, (s + 1) * LANES)
        o_ref[:, sl] = x1_ref[:, sl] + gf_ref[0][:, sl] * moe
    xo = o_ref[...]
    o_ref[...] = xo * lax.rsqrt(jnp.mean(xo * xo, axis=-1, keepdims=True) + EPS) * nw_ref[...]


def _combine_call(ytok, prow, x1, mod3, norm_final, seq, tm, row0, t_total, out_prev):
    t, d = x1.shape
    nt = t // tm
    tiles_per_seq = seq // tm
    off = row0 // tm
    kern = functools.partial(_combine_kernel, tm=tm)
    yspec = lambda k: pl.BlockSpec((tm * SLAB_ROWS, LANES), lambda i: (k * nt + i, 0))
    in_specs = [yspec(0), yspec(1), yspec(2), yspec(3),
                pl.BlockSpec((tm, LANES), lambda i: (i, 0)),
                pl.BlockSpec((tm, d), lambda i: (i, 0)),
                pl.BlockSpec((1, 1, d), lambda i: (((i + off) // tiles_per_seq) * N_MOD + 5, 0, 0)),
                pl.BlockSpec((1, d), lambda i: (0, 0))]
    args = [ytok, ytok, ytok, ytok, prow, x1, mod3, norm_final]
    aliases = {}
    if out_prev is not None:
        in_specs.append(pl.BlockSpec(memory_space=pl.ANY))
        aliases = {len(args): 0}
        args.append(out_prev)
    return pl.pallas_call(
        kern,
        out_shape=jax.ShapeDtypeStruct((t_total, d), F32),
        grid=(nt,),
        in_specs=in_specs,
        out_specs=pl.BlockSpec((tm, d), lambda i: (i + off, 0)),
        input_output_aliases=aliases,
        compiler_params=_params(("arbitrary",)),
        name="combine",
    )(*args)


def _plan(seq):
    def fit(pref):
        tm = min(pref, seq)
        assert seq % tm == 0
        return tm
    return dict(tm_in=fit(1024), tm_merge=fit(512), tm_moe=fit(512),
                ret_chunk=fit(RET_CHUNK), ssm_chunk=fit(SSM_CHUNK))


def _layer(x2, mod3, bsz, seq, norm_mix, norm_ffn, w_in, conv_w, conv_b, dt_bias, a_log, d_skip, ssm_norm,
           w_ret_out, w_ssm_out, w_out, w_router, b_router, w_gate_up, b_gate_up, w_down, b_down,
           norm_final):
    t, d = x2.shape
    plan = _plan(seq)
    qk_w = RET_HEADS * RET_QK_DIM
    v_w = RET_HEADS * RET_V_DIM
    d_inner = w_ssm_out.shape[0]
    conv_dim = conv_w.shape[1]
    n_heads = d_inner // SSM_HEAD_DIM
    dt_off = 2 * qk_w + 2 * v_w + d_inner + conv_dim

    w_main = jnp.concatenate([w_in[:, :dt_off], w_in[:, dt_off + n_heads:]], axis=1).astype(BF16)
    w_dt = jnp.pad(w_in[:, dt_off:dt_off + n_heads], ((0, 0), (0, LANES - n_heads)))
    w_dt_hi = w_dt.astype(BF16)
    w_dt = jnp.concatenate([w_dt_hi, (w_dt - w_dt_hi.astype(F32)).astype(BF16)], axis=1)
    half = RET_QK_DIM // 2
    inv_freq = ROPE_BASE ** (-jnp.arange(half, dtype=F32) / half)
    ang = jnp.arange(seq, dtype=F32)[:, None] * inv_freq[None, :]
    cos, sin = jnp.cos(ang), jnp.sin(ang)

    w_ret_b, w_ssm_b, w_out_b = w_ret_out.astype(BF16), w_ssm_out.astype(BF16), w_out.astype(BF16)
    w_r_hi = w_router.T.astype(BF16)
    w_router_t = jnp.concatenate([w_r_hi, (w_router.T - w_r_hi.astype(F32)).astype(BF16)], axis=0)
    bm = FFN_BLOCK
    slab = (SLAB_ROWS, LANES)

    def mixer(row0, tg, bg):
        proj, dt_raw = _inproj_call(x2, row0, tg, norm_mix.reshape(1, d), mod3, cos, sin, w_main, w_dt,
                                    conv_w, conv_b, 2 * qk_w + 2 * v_w + d_inner, seq, plan["tm_in"], 2 * qk_w)
        ya, yb = _mixers_call(proj, dt_raw, w_ret_b, dt_bias, a_log, d_skip, ssm_norm, w_ssm_b, bg, seq,
                              plan["ret_chunk"], plan["ssm_chunk"])
        x1, h2, idx, rank, prow, cnt = _merge_call(ya, yb, proj, x2, row0, mod3, norm_ffn.reshape(1, d),
                                                   w_out_b, w_router_t, b_router, seq, plan["tm_merge"])
        counts = cnt[:, 0]
        padded = ((counts + bm - 1) // bm) * bm
        pad_end = jnp.cumsum(padded)
        start_pad = pad_end - padded
        n_blocks = (tg * TOP_K) // bm + N_EXPERTS
        e_ids = jnp.arange(N_EXPERTS, dtype=jnp.int32)[:, None, None, None]
        dest = rank + jnp.sum(jnp.where(idx[None] == e_ids, start_pad[:, None, None, None], 0), axis=0)
        dest = dest[:, :TOP_K, :].transpose(1, 0, 2).reshape(TOP_K, tg).astype(jnp.int32)
        n_real = pad_end[-1] // bm
        blk_valid = (jnp.arange(n_blocks) < n_real).astype(jnp.int32)
        blk_row = jnp.minimum(jnp.arange(n_blocks), n_real - 1).astype(jnp.int32)
        blk_e = jnp.minimum(jnp.sum(pad_end[None, :] <= (blk_row * bm)[:, None], axis=1),
                            N_EXPERTS - 1).astype(jnp.int32)
        xs = _sc_scatter_rows(h2.reshape((tg,) + slab), dest, n_blocks * bm)
        return dict(x1=x1, prow=prow, dest=dest, blocks=(blk_e, blk_row, blk_valid), xs=xs, row0=row0)

    def experts(m):
        n_rows = m["xs"].shape[0]
        ys = _ffn_call(*m["blocks"], m["xs"].reshape(n_rows * SLAB_ROWS, LANES),
                       w_gate_up, b_gate_up, w_down, b_down, bm)
        return _sc_gather_rows(ys.reshape((n_rows,) + slab), m["dest"].reshape(-1))

    n_groups = N_GROUPS if bsz % N_GROUPS == 0 else 1
    bg = bsz // n_groups
    tg = bg * seq
    groups = [mixer(g * tg, tg, bg) for g in range(n_groups)]
    ytoks = [experts(m) for m in groups]
    out = None
    for y, m in zip(ytoks, groups):
        out = _combine_call(y.reshape(TOP_K * tg * SLAB_ROWS, LANES), m["prow"], m["x1"], mod3,
                            norm_final.reshape(1, d), seq, plan["tm_moe"], m["row0"], t, out)
    return out


def kernel(x, c, w_ada, b_ada, norm_mix, norm_ffn, w_in, conv_w, conv_b, dt_bias, a_log, d_skip, ssm_norm,
           w_ret_out, w_ssm_out, w_out, w_router, b_router, w_gate_up, b_gate_up, w_down, b_down, norm_final):
    bsz, seq, d = x.shape
    depth = w_ada.shape[0]
    assert depth == 1, "the final norm is fused into the single layer's last kernel"
    x2 = x.reshape(bsz * seq, d)
    l = 0
    mod = _mod_call(c, w_ada[l], b_ada[l])
    mod3 = mod.reshape(bsz * N_MOD, 1, d)
    out = _layer(x2, mod3, bsz, seq, norm_mix[l], norm_ffn[l], w_in[l], conv_w[l], conv_b[l], dt_bias[l],
                 a_log[l], d_skip[l], ssm_norm[l], w_ret_out[l], w_ssm_out[l], w_out[l], w_router[l],
                 b_router[l], w_gate_up[l], b_gate_up[l], w_down[l], b_down[l], norm_final)
    return out.reshape(bsz, seq, d)
```

```python
import functools
import math

import numpy as np
import jax
import jax.numpy as jnp
from jax import lax
from jax.experimental import pallas as pl
from jax.experimental.pallas import tpu as pltpu
from jax.experimental.pallas import tpu_sc as plsc

F32 = jnp.float32
BF16 = jnp.bfloat16
HIGHEST = lax.Precision.HIGHEST

EPS = 1e-6
N_MOD = 6
RET_HEADS = 4
RET_QK_DIM = 256
RET_V_DIM = 512
ROPE_BASE = 10000.0
SSM_HEAD_DIM = 64
SSM_GROUPS = 8
SSM_STATE = 128
SSM_CONV = 4
N_EXPERTS = 32
TOP_K = 4
SWIGLU_LIMIT = 7.0
SWIGLU_ALPHA = 1.702

LANES = 128
SUBLANES = 8
VMEM_LIMIT = 56 * 1024 * 1024

RET_CHUNK = 256
SSM_CHUNK = 128
FFN_BLOCK = 512
SC_CORES = 2
SC_SUBCORES = 16
SC_GROUP = 64
N_GROUPS = 2


def _params(sem, vmem=VMEM_LIMIT):
    return pltpu.CompilerParams(dimension_semantics=sem, vmem_limit_bytes=vmem)


def _nt_dot(a, b, **kw):
    return lax.dot_general(a, b, (((1,), (1,)), ((), ())), preferred_element_type=F32, **kw)


def _tn_dot(a, b, **kw):
    return lax.dot_general(a, b, (((0,), (0,)), ((), ())), preferred_element_type=F32, **kw)


def _silu(v):
    return v * jax.nn.sigmoid(v)


SLAB_ROWS = 4
HIGH_HALF = 0xFFFF0000


def _store_slabs(ref, vals, n):
    for s in range(SLAB_ROWS):
        lo = vals[:, s * LANES:(s + 1) * LANES].astype(BF16).astype(F32)
        hi = vals[:, (s + SLAB_ROWS) * LANES:(s + SLAB_ROWS + 1) * LANES].astype(BF16).astype(F32)
        word = (pltpu.bitcast(lo, jnp.uint32) >> 16) | (pltpu.bitcast(hi, jnp.uint32) & jnp.uint32(HIGH_HALF))
        ref[pl.ds(s, n, stride=SLAB_ROWS), :] = word


def _load_slabs(ref, n, base=0):
    lo, hi = [], []
    for s in range(SLAB_ROWS):
        word = ref[pl.ds(base + s, n, stride=SLAB_ROWS), :]
        lo.append(pltpu.bitcast(word << 16, F32))
        hi.append(pltpu.bitcast(word & jnp.uint32(HIGH_HALF), F32))
    return lo + hi


def _mod_kernel(c_ref, w_ref, b_ref, o_ref):
    cond = _silu(c_ref[...])
    o_ref[...] = jnp.dot(cond, w_ref[...], preferred_element_type=F32, precision=HIGHEST) + b_ref[...]


def _mod_call(c, w_ada, b_ada):
    bsz, d = c.shape
    n = w_ada.shape[1]
    return pl.pallas_call(
        _mod_kernel,
        out_shape=jax.ShapeDtypeStruct((bsz, n), F32),
        grid=(n // d,),
        in_specs=[pl.BlockSpec((bsz, d), lambda j: (0, 0)),
                  pl.BlockSpec((d, d), lambda j: (0, j)),
                  pl.BlockSpec((1, d), lambda j: (0, j))],
        out_specs=pl.BlockSpec((bsz, d), lambda j: (0, j)),
        compiler_params=_params(("arbitrary",)),
        name="mod",
    )(c, w_ada, b_ada.reshape(1, n))


def _inproj_kernel(x_ref, nw_ref, sc_ref, sh_ref, cos_ref, sin_ref, w_ref, wdt_ref, cw_ref, cb_ref,
                   o_ref, dt_ref, h_s, work, carry, *, conv_j0, conv_nj, silu_j, sigm_j, tiles_per_seq,
                   tm, tn, sub):
    i = pl.program_id(0)
    j = pl.program_id(1)
    n_dt = dt_ref.shape[1]
    rows = min(tm, 256)

    @pl.when(j == 0)
    def _():
        xf = x_ref[...]
        ms = jnp.mean(xf * xf, axis=-1, keepdims=True)
        y = xf * lax.rsqrt(ms + EPS) * nw_ref[...]
        hm = y * (1.0 + sc_ref[0]) + sh_ref[0]
        hb = hm.astype(BF16)
        h_s[...] = hb
        h_lo = (hm - hb.astype(F32)).astype(BF16)
        d_hi = jnp.dot(hb, wdt_ref[...], preferred_element_type=F32)
        d_lo = jnp.dot(h_lo, wdt_ref[:, :n_dt], preferred_element_type=F32)
        dt_ref[...] = d_hi[:, :n_dt] + d_hi[:, n_dt:] + d_lo
        half = RET_QK_DIM // 2
        for p in range(tn // sub):
            for r in range(tm // rows):
                rs = slice(r * rows, (r + 1) * rows)
                acc = jnp.dot(h_s[rs, :], w_ref[:, p * sub:(p + 1) * sub], preferred_element_type=F32)
                cos = cos_ref[rs, :]
                sin = sin_ref[rs, :]
                for cc in range(sub // RET_QK_DIM):
                    c = p * (sub // RET_QK_DIM) + cc
                    a = acc[:, cc * RET_QK_DIM: cc * RET_QK_DIM + half]
                    b = acc[:, cc * RET_QK_DIM + half: (cc + 1) * RET_QK_DIM]
                    scale = 1.0 if c < RET_HEADS else RET_QK_DIM ** -0.5
                    o_ref[rs, c * RET_QK_DIM: c * RET_QK_DIM + half] = ((a * cos - b * sin) * scale).astype(BF16)
                    o_ref[rs, c * RET_QK_DIM + half: (c + 1) * RET_QK_DIM] = (
                        (a * sin + b * cos) * scale).astype(BF16)

    is_conv = (j >= conv_j0) & (j < conv_j0 + conv_nj)

    @pl.when(is_conv)
    def _():
        cj = j - conv_j0
        pad = SUBLANES

        @pl.when(i % tiles_per_seq == 0)
        def _():
            carry[cj] = jnp.zeros(carry.shape[1:], F32)

        for p in range(tn // sub):
            for r in range(tm // rows):
                r0 = r * rows
                acc = jnp.dot(h_s[r0:r0 + rows, :], w_ref[:, p * sub:(p + 1) * sub], preferred_element_type=F32)
                for cc in range(sub // LANES):
                    c = p * (sub // LANES) + cc
                    cols = slice(c * LANES, (c + 1) * LANES)
                    if r == 0:
                        work[c, 0:pad, :] = carry[cj, c]
                    lo = pad + r0
                    work[c, lo:lo + rows, :] = acc[:, cc * LANES:(cc + 1) * LANES]
                    conv = cb_ref[:, cols] + cw_ref[SSM_CONV - 1:SSM_CONV, cols] * work[c, lo:lo + rows, :]
                    for k in range(SSM_CONV - 1):
                        shift = SSM_CONV - 1 - k
                        conv = conv + cw_ref[k:k + 1, cols] * work[c, lo - shift:lo - shift + rows, :]
                    if r0 + rows == tm:
                        carry[cj, c] = work[c, tm:tm + pad, :]
                    o_ref[r0:r0 + rows, cols] = _silu(conv).astype(BF16)

    def plain(act):
        for p in range(tn // sub):
            for r in range(tm // rows):
                acc = jnp.dot(h_s[r * rows:(r + 1) * rows, :], w_ref[:, p * sub:(p + 1) * sub],
                              preferred_element_type=F32)
                o_ref[r * rows:(r + 1) * rows, p * sub:(p + 1) * sub] = act(acc).astype(BF16)

    is_silu = (j >= silu_j[0]) & (j < silu_j[1])
    is_sigm = (j >= sigm_j[0]) & (j < sigm_j[1])
    pl.when(is_silu)(lambda: plain(_silu))
    pl.when(is_sigm)(lambda: plain(jax.nn.sigmoid))
    pl.when((j != 0) & jnp.logical_not(is_conv | is_silu | is_sigm))(lambda: plain(lambda v: v))


def _inproj_call(x2, row0, t, norm_w, mod3, cos, sin, w_main, w_dt, conv_w, conv_b, conv_off, seq, tm, tn):
    d = x2.shape[1]
    n = w_main.shape[1]
    conv_dim = conv_w.shape[1]
    tiles_per_seq = seq // tm
    off = row0 // tm
    assert tn == 2 * RET_HEADS * RET_QK_DIM, "rotary epilogue expects q and k in the first column tile"
    assert conv_off % tn == 0 and conv_dim % tn == 0
    conv_j0, conv_nj = conv_off // tn, conv_dim // tn
    sub = 512
    g_off = 2 * RET_HEADS * RET_QK_DIM + RET_HEADS * RET_V_DIM
    assert g_off % tn == 0 and (conv_off - g_off) % tn == 0 and (n - conv_off - conv_dim) % tn == 0
    silu_j = (g_off // tn, conv_off // tn)
    sigm_j = ((conv_off + conv_dim) // tn, n // tn)
    kern = functools.partial(_inproj_kernel, conv_j0=conv_j0, conv_nj=conv_nj, silu_j=silu_j, sigm_j=sigm_j,
                             tiles_per_seq=tiles_per_seq, tm=tm, tn=tn, sub=sub)
    conv_idx = lambda i, j: (0, jnp.clip(j - conv_j0, 0, conv_nj - 1))
    return pl.pallas_call(
        kern,
        out_shape=(jax.ShapeDtypeStruct((t, n), BF16), jax.ShapeDtypeStruct((t, LANES), F32)),
        grid=(t // tm, n // tn),
        in_specs=[
            pl.BlockSpec((tm, d), lambda i, j: (i + off, 0)),
            pl.BlockSpec((1, d), lambda i, j: (0, 0)),
            pl.BlockSpec((1, 1, d), lambda i, j: (((i + off) // tiles_per_seq) * N_MOD + 1, 0, 0)),
            pl.BlockSpec((1, 1, d), lambda i, j: (((i + off) // tiles_per_seq) * N_MOD + 0, 0, 0)),
            pl.BlockSpec((tm, LANES), lambda i, j: (i % tiles_per_seq, 0)),
            pl.BlockSpec((tm, LANES), lambda i, j: (i % tiles_per_seq, 0)),
            pl.BlockSpec((d, tn), lambda i, j: (0, j)),
            pl.BlockSpec((d, 2 * LANES), lambda i, j: (0, 0)),
            pl.BlockSpec((SSM_CONV, tn), conv_idx),
            pl.BlockSpec((1, tn), conv_idx),
        ],
        out_specs=(pl.BlockSpec((tm, tn), lambda i, j: (i, j)),
                   pl.BlockSpec((tm, LANES), lambda i, j: (i, 0))),
        scratch_shapes=[pltpu.VMEM((tm, d), BF16),
                        pltpu.VMEM((tn // LANES, tm + SUBLANES, LANES), F32),
                        pltpu.VMEM((conv_nj, tn // LANES, SUBLANES, LANES), F32)],
        compiler_params=_params(("arbitrary", "arbitrary")),
        name="inproj",
    )(x2, norm_w, mod3, mod3, cos, sin, w_main, w_dt, conv_w, conv_b.reshape(1, conv_dim))


def _retention_body(q_ref, k_ref, v_ref, g_ref, din_ref, dq_ref, dk_ref, w_ref, o_ref, state, decay_c):
    acc = None
    for h in range(RET_HEADS):
        qh = q_ref[:, h * RET_QK_DIM:(h + 1) * RET_QK_DIM]
        kh = k_ref[:, h * RET_QK_DIM:(h + 1) * RET_QK_DIM]
        vh = v_ref[:, h * RET_V_DIM:(h + 1) * RET_V_DIM]
        scores = _nt_dot(qh, kh) * din_ref[h]
        inner = jnp.dot(scores.astype(BF16), vh, preferred_element_type=F32)
        st = state[h]
        cross = jnp.dot(qh, st.astype(BF16), preferred_element_type=F32) * dq_ref[h]
        kd = (kh.astype(F32) * dk_ref[h]).astype(BF16)
        state[h] = st * decay_c[h] + _tn_dot(kd, vh)
        ret = inner + cross
        ret = ret * lax.rsqrt(jnp.mean(ret * ret, axis=-1, keepdims=True) + EPS)
        ret = ret * g_ref[:, h * RET_V_DIM:(h + 1) * RET_V_DIM].astype(F32)
        part = jnp.dot(ret.astype(BF16), w_ref[h * RET_V_DIM:(h + 1) * RET_V_DIM, :],
                       preferred_element_type=F32)
        acc = part if acc is None else acc + part
    o_ref[...] = acc.astype(o_ref.dtype)


def _retention_tables(chunk):
    lg = np.log(1.0 - 2.0 ** (-5.0 - np.arange(RET_HEADS, dtype=np.float64)))
    idx = np.arange(chunk, dtype=np.float64)
    rel = idx[:, None] - idx[None, :]
    causal = rel >= 0
    din = np.where(causal[None], np.exp(np.where(causal, rel, 0.0)[None] * lg[:, None, None]), 0.0)
    dq = np.exp((idx + 1.0)[None, :, None] * lg[:, None, None])
    dk = np.exp((chunk - 1.0 - idx)[None, :, None] * lg[:, None, None])
    dc = tuple(float(v) for v in np.exp(chunk * lg))
    return (jnp.asarray(din, F32), jnp.asarray(dq, F32), jnp.asarray(dk, F32), dc)


def _ssd_body(z_ref, xbc_ref, dt_ref, dtb_ref, alog_ref, dsk_ref, nw_ref,
              tril_ref, exp_ref, w_ref, o_ref, state, yn_s, chunk, d_inner):
    heads_per_group = d_inner // SSM_HEAD_DIM // SSM_GROUPS
    gw = heads_per_group * SSM_HEAD_DIM
    assert SSM_HEAD_DIM * 2 == LANES and gw == 2 * LANES

    dt =jax.nn.softplus(dt_ref[...] + dtb_ref[...])
    a = -jnp.exp(alog_ref[...])
    adt = dt * a
    acs = jnp.dot(tril_ref[...], adt, preferred_element_type=F32, precision=HIGHEST)
    acs_t = acs.T
    dt_x = jnp.dot(dt.astype(BF16), exp_ref[...], preferred_element_type=F32)
    li = lax.broadcasted_iota(jnp.int32, (chunk, chunk), 0)
    si = lax.broadcasted_iota(jnp.int32, (chunk, chunk), 1)
    causal = li >= si
    low_half = si < SSM_HEAD_DIM
    lane_g = lax.broadcasted_iota(jnp.int32, (chunk, gw), 1)

    b_off = d_inner
    c_off = d_inner + SSM_GROUPS * SSM_STATE
    for g in range(SSM_GROUPS):
        bm = xbc_ref[:, b_off + g * SSM_STATE: b_off + (g + 1) * SSM_STATE]
        cm = xbc_ref[:, c_off + g * SSM_STATE: c_off + (g + 1) * SSM_STATE]
        xs_g = xbc_ref[:, g * gw:(g + 1) * gw].astype(F32)
        xdt_g = xs_g * dt_x[:, g * gw:(g + 1) * gw]
        cb = _nt_dot(cm, bm)
        cols, ms, xm = [], [], []
        for jh in range(heads_per_group):
            h = g * heads_per_group + jh
            col = jnp.broadcast_to(acs[:, h:h + 1], (chunk, chunk))
            seg = jnp.exp(jnp.where(causal, col - acs_t[h:h + 1, :], -jnp.inf))
            cols.append(col)
            ms.append((cb * seg).astype(BF16))
            in_head = (lane_g >= jh * SSM_HEAD_DIM) & (lane_g < (jh + 1) * SSM_HEAD_DIM)
            xm.append(jnp.where(in_head, xdt_g, 0.0).astype(BF16))
        y_diag = jnp.dot(jnp.concatenate(ms, axis=-1), jnp.concatenate(xm, axis=0),
                         preferred_element_type=F32)
        a_x = jnp.concatenate([jnp.where(low_half, cols[0], cols[1]),
                               jnp.where(low_half, cols[2], cols[3])], axis=-1)
        e_acs_x = jnp.exp(a_x)
        a_last_x = a_x[chunk - 1:chunk, :]
        st = state[g]
        y_off = jnp.dot(cm, st.astype(BF16), preferred_element_type=F32) * e_acs_x
        xdec = (xdt_g * jnp.exp(a_last_x - a_x)).astype(BF16)
        state[g] = st * e_acs_x[chunk - 1:chunk, :] + _tn_dot(bm, xdec)
        y = y_diag + y_off + dsk_ref[:, g * gw:(g + 1) * gw] * xs_g
        yz = y * z_ref[:, g * gw:(g + 1) * gw].astype(F32)
        yn = yz * lax.rsqrt(jnp.mean(yz * yz, axis=-1, keepdims=True) + EPS) * nw_ref[:, g * gw:(g + 1) * gw]
        yn_s[:, g * gw:(g + 1) * gw] = yn.astype(BF16)
    o_ref[...] = jnp.dot(yn_s[...], w_ref[...], preferred_element_type=F32).astype(o_ref.dtype)


def _mixers_kernel(q_ref, k_ref, v_ref, g_ref, din_ref, dq_ref, dk_ref, wret_ref,
                   z_ref, xbc_ref, dt_ref, dtb_ref, alog_ref, dsk_ref, nw_ref, tril_ref, exp_ref, wssm_ref,
                   ya_ref, yb_ref, rstate, sstate, yn_s, *, decay_c, chunk, ssm_chunk, d_inner):
    c = pl.program_id(1)

    @pl.when(c == 0)
    def _():
        rstate[...] = jnp.zeros_like(rstate)
        sstate[...] = jnp.zeros_like(sstate)

    _retention_body(q_ref, k_ref, v_ref, g_ref, din_ref, dq_ref, dk_ref, wret_ref, ya_ref, rstate, decay_c)
    for sub in range(chunk // ssm_chunk):
        rows = pl.ds(sub * ssm_chunk, ssm_chunk)
        _ssd_body(z_ref.at[rows, :], xbc_ref.at[rows, :], dt_ref.at[rows, :], dtb_ref, alog_ref, dsk_ref,
                  nw_ref, tril_ref, exp_ref, wssm_ref, yb_ref.at[rows, :], sstate, yn_s, ssm_chunk, d_inner)


def _mixers_call(proj, dt_raw, w_ret, dt_bias, a_log, d_skip, ssm_norm, w_ssm, bsz, seq, chunk, ssm_chunk):
    t = proj.shape[0]
    d_inner, d = w_ssm.shape
    conv_dim = d_inner + 2 * SSM_GROUPS * SSM_STATE
    n_heads = d_inner // SSM_HEAD_DIM
    nc = seq // chunk
    gw = d_inner // SSM_GROUPS
    qk_w = RET_HEADS * RET_QK_DIM
    v_w = RET_HEADS * RET_V_DIM
    assert chunk % ssm_chunk == 0
    assert ssm_chunk == LANES, "the per-head decay tiles are built lane-for-lane against the chunk"
    din, dq, dk, dc = _retention_tables(chunk)
    pad_h = lambda v: jnp.pad(v.astype(F32), (0, LANES - n_heads)).reshape(1, LANES)
    tril = jnp.asarray(np.tril(np.ones((ssm_chunk, ssm_chunk), np.float32)))
    expand = np.zeros((LANES, d_inner), np.float32)
    for h in range(n_heads):
        expand[h, h * SSM_HEAD_DIM:(h + 1) * SSM_HEAD_DIM] = 1.0
    expand = jnp.asarray(expand, BF16)
    kern = functools.partial(_mixers_kernel, decay_c=dc, chunk=chunk, ssm_chunk=ssm_chunk, d_inner=d_inner)
    row = lambda b, c: b * nc + c
    z_blk = (2 * qk_w + 2 * v_w) // d_inner
    xbc_blk = (2 * qk_w + 2 * v_w + d_inner) // conv_dim
    full = lambda shape: pl.BlockSpec(shape, lambda b, c: (0,) * len(shape))
    out = jax.ShapeDtypeStruct((t, d), BF16)
    return pl.pallas_call(
        kern,
        out_shape=(out, out),
        grid=(bsz, nc),
        in_specs=[
            pl.BlockSpec((chunk, qk_w), lambda b, c: (row(b, c), 0)),
            pl.BlockSpec((chunk, qk_w), lambda b, c: (row(b, c), 1)),
            pl.BlockSpec((chunk, v_w), lambda b, c: (row(b, c), 1)),
            pl.BlockSpec((chunk, v_w), lambda b, c: (row(b, c), 2)),
            full((RET_HEADS, chunk, chunk)), full((RET_HEADS, chunk, 1)), full((RET_HEADS, chunk, 1)),
            full((v_w, d)),
            pl.BlockSpec((chunk, d_inner), lambda b, c: (row(b, c), z_blk)),
            pl.BlockSpec((chunk, conv_dim), lambda b, c: (row(b, c), xbc_blk)),
            pl.BlockSpec((chunk, LANES), lambda b, c: (row(b, c), 0)),
            full((1, LANES)), full((1, LANES)),
            full((1, d_inner)), full((1, d_inner)), full((ssm_chunk, ssm_chunk)), full((LANES, d_inner)),
            full((d_inner, d)),
        ],
        out_specs=(pl.BlockSpec((chunk, d), lambda b, c: (row(b, c), 0)),
                   pl.BlockSpec((chunk, d), lambda b, c: (row(b, c), 0))),
        scratch_shapes=[pltpu.VMEM((RET_HEADS, RET_QK_DIM, RET_V_DIM), F32),
                        pltpu.VMEM((SSM_GROUPS, SSM_STATE, gw), F32),
                        pltpu.VMEM((ssm_chunk, d_inner), BF16)],
        compiler_params=_params(("arbitrary", "arbitrary")),
        name="mixers",
    )(proj, proj, proj, proj, din, dq, dk, w_ret,
      proj, proj, dt_raw, pad_h(dt_bias), pad_h(a_log),
      jnp.repeat(d_skip.astype(F32), SSM_HEAD_DIM).reshape(1, d_inner), ssm_norm.reshape(1, d_inner),
      tril, expand, w_ssm)


def _merge_kernel(ya_ref, yb_ref, ga_ref, gb_ref, x_ref, gm_ref, scf_ref, shf_ref, nw_ref, wo_ref,
                  wr_ref, br_ref, tri_ref,
                  x1_ref, h2_ref, idx_ref, rank_ref, prow_ref, cnt_ref, cnt_s, *, tm):
    i = pl.program_id(0)

    @pl.when(i == 0)
    def _():
        cnt_s[...] = jnp.zeros_like(cnt_s)

    merged = (ga_ref[...].astype(F32) * ya_ref[...].astype(F32)
              + gb_ref[...].astype(F32) * yb_ref[...].astype(F32))
    mo = jnp.dot(merged.astype(BF16), wo_ref[...], preferred_element_type=F32)
    x1 = x_ref[...] + gm_ref[0] * mo
    x1_ref[...] = x1
    ms = jnp.mean(x1 * x1, axis=-1, keepdims=True)
    h2 = x1 * lax.rsqrt(ms + EPS) * nw_ref[...] * (1.0 + scf_ref[0]) + shf_ref[0]
    _store_slabs(h2_ref, h2, tm)

    h_hi = h2.astype(BF16)
    h_lo = (h2 - h_hi.astype(F32)).astype(BF16)
    lg2 = _nt_dot(wr_ref[...], h_hi)
    lg = lg2[:N_EXPERTS] + lg2[N_EXPERTS:] + _nt_dot(wr_ref[:N_EXPERTS, :], h_lo) + br_ref[...]
    sub = lax.broadcasted_iota(jnp.int32, lg.shape, 0)
    work = lg
    vals, idxs, sels = [], [], []
    for _ in range(TOP_K):
        m = jnp.max(work, axis=0, keepdims=True)
        ik = jnp.min(jnp.where(work == m, sub, N_EXPERTS), axis=0, keepdims=True)
        sel = sub == ik
        vals.append(m)
        idxs.append(ik)
        sels.append(sel)
        work = jnp.where(sel, -jnp.inf, work)
    exps = [jnp.exp(v - vals[0]) for v in vals]
    denom = exps[0]
    for e in exps[1:]:
        denom = denom + e
    probs = [e / denom for e in exps]

    base = cnt_s[:, 0:1]
    ranks = []
    for k in range(TOP_K):
        mk = jnp.where(sels[k], 1.0, 0.0)
        pre = jnp.dot(mk.astype(BF16), tri_ref[...], preferred_element_type=F32)
        ranks.append(jnp.sum(jnp.where(sels[k], pre + base, 0.0), axis=0, keepdims=True))
        base = base + jnp.sum(mk, axis=1, keepdims=True)
    cnt_s[...] = jnp.broadcast_to(base, cnt_s.shape)
    cnt_ref[...] = cnt_s[...].astype(jnp.int32)

    zi = jnp.zeros((SUBLANES - TOP_K, tm), jnp.int32)
    idx_ref[0] = jnp.concatenate(idxs + [zi], axis=0)
    rank_ref[0] = jnp.concatenate([r.astype(jnp.int32) for r in ranks] + [zi], axis=0)
    pt = jnp.concatenate(probs + [jnp.zeros((LANES - TOP_K, tm), F32)], axis=0)
    prow_ref[...] = pt.T


def _merge_call(ya, yb, proj, x2, row0, mod3, norm_w, w_out, w_router_t, b_router, seq, tm):
    t, d = ya.shape
    nt = t // tm
    tiles_per_seq = seq // tm
    off = row0 // tm
    ga_blk = proj.shape[1] // d - 2
    tri = jnp.asarray(np.triu(np.ones((tm, tm), np.float32), 1), BF16)
    kern = functools.partial(_merge_kernel, tm=tm)
    modspec = lambda m: pl.BlockSpec((1, 1, d), lambda i: (((i + off) // tiles_per_seq) * N_MOD + m, 0, 0))
    return pl.pallas_call(
        kern,
        out_shape=(jax.ShapeDtypeStruct((t, d), F32), jax.ShapeDtypeStruct((t * SLAB_ROWS, LANES), jnp.uint32),
                   jax.ShapeDtypeStruct((nt, SUBLANES, tm), jnp.int32),
                   jax.ShapeDtypeStruct((nt, SUBLANES, tm), jnp.int32),
                   jax.ShapeDtypeStruct((t, LANES), F32),
                   jax.ShapeDtypeStruct((N_EXPERTS, LANES), jnp.int32)),
        grid=(nt,),
        in_specs=[
            pl.BlockSpec((tm, d), lambda i: (i, 0)),
            pl.BlockSpec((tm, d), lambda i: (i, 0)),
            pl.BlockSpec((tm, d), lambda i: (i, ga_blk)),
            pl.BlockSpec((tm, d), lambda i: (i, ga_blk + 1)),
            pl.BlockSpec((tm, d), lambda i: (i + off, 0)),
            modspec(2), modspec(4), modspec(3),
            pl.BlockSpec((1, d), lambda i: (0, 0)),
            pl.BlockSpec((d, d), lambda i: (0, 0)),
            pl.BlockSpec((2 * N_EXPERTS, d), lambda i: (0, 0)),
            pl.BlockSpec((N_EXPERTS, 1), lambda i: (0, 0)),
            pl.BlockSpec((tm, tm), lambda i: (0, 0)),
        ],
        out_specs=(pl.BlockSpec((tm, d), lambda i: (i, 0)),
                   pl.BlockSpec((tm * SLAB_ROWS, LANES), lambda i: (i, 0)),
                   pl.BlockSpec((1, SUBLANES, tm), lambda i: (i, 0, 0)),
                   pl.BlockSpec((1, SUBLANES, tm), lambda i: (i, 0, 0)),
                   pl.BlockSpec((tm, LANES), lambda i: (i, 0)),
                   pl.BlockSpec((N_EXPERTS, LANES), lambda i: (0, 0))),
        scratch_shapes=[pltpu.VMEM((N_EXPERTS, LANES), F32)],
        compiler_params=_params(("arbitrary",)),
        name="merge",
    )(ya, yb, proj, proj, x2, mod3, mod3, mod3, norm_w, w_out, w_router_t,
      b_router.reshape(N_EXPERTS, 1), tri)


def _sc_mesh():
    return plsc.VectorSubcoreMesh(core_axis_name="c", subcore_axis_name="s")


def _sc_worker():
    return lax.axis_index("s") * SC_CORES + lax.axis_index("c")


def _sc_scatter_rows(rows, dest, n_out):
    t = rows.shape[0]
    n_k = dest.shape[0]
    g = SC_GROUP
    n_w = SC_CORES * SC_SUBCORES
    assert t % (n_w * g) == 0
    cpw = t // (n_w * g)
    dest_w = dest.reshape(n_k, n_w, cpw, g).transpose(1, 0, 2, 3)

    @functools.partial(
        pl.kernel, mesh=_sc_mesh(),
        out_type=jax.ShapeDtypeStruct((n_out,) + rows.shape[1:], rows.dtype),
        scratch_types=[pltpu.VMEM((n_k, cpw, g), jnp.int32),
                       pltpu.VMEM((g,) + rows.shape[1:], rows.dtype),
                       pltpu.SemaphoreType.DMA],
    )
    def scatter(rows_hbm, dest_hbm, out_hbm, idx_v, rows_v, sem):
        wid = _sc_worker()
        pltpu.sync_copy(dest_hbm.at[wid], idx_v)

        @pl.loop(0, cpw)
        def _(cc):
            r0 = pl.multiple_of((wid * cpw + cc) * g, g)
            pltpu.sync_copy(rows_hbm.at[pl.ds(r0, g)], rows_v)
            copies = [pltpu.async_copy(rows_v, out_hbm.at[idx_v.at[k, cc]], sem) for k in range(n_k)]
            for cp in copies:
                cp.wait()

    return scatter(rows, dest_w)


def _sc_gather_rows(table, idx):
    m = idx.shape[0]
    g = SC_GROUP
    n_w = SC_CORES * SC_SUBCORES
    assert m % (n_w * g) == 0
    per_w = m // n_w

    @functools.partial(
        pl.kernel, mesh=_sc_mesh(),
        out_type=jax.ShapeDtypeStruct((m,) + table.shape[1:], table.dtype),
        scratch_types=[pltpu.VMEM((per_w,), jnp.int32),
                       pltpu.VMEM((g,) + table.shape[1:], table.dtype),
                       pltpu.SemaphoreType.DMA],
    )
    def gather(table_hbm, idx_hbm, out_hbm, idx_v, rows_v, sem):
        base = _sc_worker() * per_w
        pltpu.sync_copy(idx_hbm.at[pl.ds(base, per_w)], idx_v)

        @pl.loop(0, per_w // g)
        def _(cc):
            off = pl.multiple_of(cc * g, g)
            pltpu.async_copy(table_hbm.at[idx_v.at[pl.ds(off, g)]], rows_v, sem).wait()
            pltpu.sync_copy(rows_v, out_hbm.at[pl.ds(base + off, g)])

    return gather(table, idx)


def _ffn_kernel(be_ref, br_ref, bv_ref, first_ref, slot_ref, next_ref,
                x_ref, wgu_hbm, bgu_ref, wd_hbm, bd_ref, o_ref, wgu_f, wd_f, wgu_s, wd_s, sem, *, bm, d_ff):
    i = pl.program_id(0)

    def fetch(e, slot):
        return (pltpu.make_async_copy(wgu_hbm.at[e], wgu_f.at[slot], sem.at[0, slot]),
                pltpu.make_async_copy(wd_hbm.at[e], wd_f.at[slot], sem.at[1, slot]))

    @pl.when(i == 0)
    def _():
        for cp in fetch(be_ref[0], slot_ref[0]):
            cp.start()

    @pl.when(first_ref[i] == 1)
    def _():
        slot = slot_ref[i]
        for cp in fetch(be_ref[i], slot):
            cp.wait()
        wgu_s[...] = wgu_f[slot].astype(BF16)
        wd_s[...] = wd_f[slot].astype(BF16)

        @pl.when(next_ref[i] >= 0)
        def _():
            for cp in fetch(next_ref[i], 1 - slot):
                cp.start()

    @pl.when(bv_ref[i] == 1)
    def _():
        x = jnp.concatenate(_load_slabs(x_ref, bm), axis=-1).astype(BF16)
        gu = jnp.dot(x, wgu_s[...], preferred_element_type=F32) + bgu_ref[0]
        gate = jnp.minimum(gu[:, :d_ff], SWIGLU_LIMIT)
        up = jnp.clip(gu[:, d_ff:], -SWIGLU_LIMIT, SWIGLU_LIMIT)
        act = gate * jax.nn.sigmoid(SWIGLU_ALPHA * gate) * (up + 1.0)
        y = jnp.dot(act.astype(BF16), wd_s[...], preferred_element_type=F32) + bd_ref[0]
        _store_slabs(o_ref, y, bm)

    @pl.when(bv_ref[i] == 0)
    def _():
        o_ref[...] = jnp.zeros_like(o_ref)


def _ffn_call(blk_e, blk_row, blk_valid, xs, w_gu, b_gu, w_d, b_d, bm):
    n_e, d, f2 = w_gu.shape
    d_ff = f2 // 2
    nb = blk_e.shape[0]
    first = jnp.concatenate([jnp.ones((1,), jnp.int32), (blk_e[1:] != blk_e[:-1]).astype(jnp.int32)])
    slot = (jnp.cumsum(first) - 1) % 2
    later = jnp.where(blk_e[None, :] > blk_e[:, None], blk_e[None, :], n_e)
    nxt = jnp.min(later, axis=1)
    nxt = jnp.where(nxt == n_e, -1, nxt)
    kern = functools.partial(_ffn_kernel, bm=bm, d_ff=d_ff)
    imap = lambda f: (lambda i, be, br, bv, fi, sl, nx: f(i, be, br))
    gs = pltpu.PrefetchScalarGridSpec(
        num_scalar_prefetch=6,
        grid=(nb,),
        in_specs=[pl.BlockSpec((bm * SLAB_ROWS, LANES), imap(lambda i, be, br: (br[i], 0))),
                  pl.BlockSpec(memory_space=pl.ANY),
                  pl.BlockSpec((1, 1, f2), imap(lambda i, be, br: (be[i], 0, 0))),
                  pl.BlockSpec(memory_space=pl.ANY),
                  pl.BlockSpec((1, 1, d), imap(lambda i, be, br: (be[i], 0, 0)))],
        out_specs=pl.BlockSpec((bm * SLAB_ROWS, LANES), imap(lambda i, be, br: (i, 0))),
        scratch_shapes=[pltpu.VMEM((2, d, f2), F32), pltpu.VMEM((2, d_ff, d), F32),
                        pltpu.VMEM((d, f2), BF16), pltpu.VMEM((d_ff, d), BF16),
                        pltpu.SemaphoreType.DMA((2, 2))],
    )
    return pl.pallas_call(
        kern,
        out_shape=jax.ShapeDtypeStruct((nb * bm * SLAB_ROWS, LANES), jnp.uint32),
        grid_spec=gs,
        compiler_params=_params(("arbitrary",)),
        name="ffn",
    )(blk_e, blk_row, blk_valid, first, slot.astype(jnp.int32), nxt.astype(jnp.int32),
      xs, w_gu, b_gu.reshape(n_e, 1, f2), w_d, b_d.reshape(n_e, 1, d))


def _combine_kernel(y0_ref, y1_ref, y2_ref, y3_ref, prow_ref, x1_ref, gf_ref, nw_ref, *rest, tm):
    o_ref = rest[-1]
    p = prow_ref[...]
    pieces = [_load_slabs(y_ref, tm) for y_ref in (y0_ref, y1_ref, y2_ref, y3_ref)]
    for s in range(len(pieces[0])):
        moe = None
        for k in range(TOP_K):
            piece = pieces[k][s] * p[:, k:k + 1]
            moe = piece if moe is None else moe + piece
        sl = slice(s * LANES, (s + 1) * LANES)
        o_ref[:, sl] = x1_ref[:, sl] + gf_ref[0][:, sl] * moe
    xo = o_ref[...]
    o_ref[...] = xo * lax.rsqrt(jnp.mean(xo * xo, axis=-1, keepdims=True) + EPS) * nw_ref[...]


def _combine_call(ytok, prow, x1, mod3, norm_final, seq, tm, row0, t_total, out_prev):
    t, d = x1.shape
    nt = t // tm
    tiles_per_seq = seq // tm
    off = row0 // tm
    kern = functools.partial(_combine_kernel, tm=tm)
    yspec = lambda k: pl.BlockSpec((tm * SLAB_ROWS, LANES), lambda i: (k * nt + i, 0))
    in_specs = [yspec(0), yspec(1), yspec(2), yspec(3),
                pl.BlockSpec((tm, LANES), lambda i: (i, 0)),
                pl.BlockSpec((tm, d), lambda i: (i, 0)),
                pl.BlockSpec((1, 1, d), lambda i: (((i + off) // tiles_per_seq) * N_MOD + 5, 0, 0)),
                pl.BlockSpec((1, d), lambda i: (0, 0))]
    args = [ytok, ytok, ytok, ytok, prow, x1, mod3, norm_final]
    aliases = {}
    if out_prev is not None:
        in_specs.append(pl.BlockSpec(memory_space=pl.ANY))
        aliases = {len(args): 0}
        args.append(out_prev)
    return pl.pallas_call(
        kern,
        out_shape=jax.ShapeDtypeStruct((t_total, d), F32),
        grid=(nt,),
        in_specs=in_specs,
        out_specs=pl.BlockSpec((tm, d), lambda i: (i + off, 0)),
        input_output_aliases=aliases,
        compiler_params=_params(("arbitrary",)),
        name="combine",
    )(*args)


def _plan(seq):
    def fit(pref):
        tm = min(pref, seq)
        assert seq % tm == 0
        return tm
    return dict(tm_in=fit(1024), tm_merge=fit(512), tm_moe=fit(512),
                ret_chunk=fit(RET_CHUNK), ssm_chunk=fit(SSM_CHUNK))


def _layer(x2, mod3, bsz, seq, norm_mix, norm_ffn, w_in, conv_w, conv_b, dt_bias, a_log, d_skip, ssm_norm,
           w_ret_out, w_ssm_out, w_out, w_router, b_router, w_gate_up, b_gate_up, w_down, b_down,
           norm_final):
    t, d = x2.shape
    plan = _plan(seq)
    qk_w = RET_HEADS * RET_QK_DIM
    v_w = RET_HEADS * RET_V_DIM
    d_inner = w_ssm_out.shape[0]
    conv_dim = conv_w.shape[1]
    n_heads = d_inner // SSM_HEAD_DIM
    dt_off = 2 * qk_w + 2 * v_w + d_inner + conv_dim

    w_main = jnp.concatenate([w_in[:, :dt_off], w_in[:, dt_off + n_heads:]], axis=1).astype(BF16)
    w_dt = jnp.pad(w_in[:, dt_off:dt_off + n_heads], ((0, 0), (0, LANES - n_heads)))
    w_dt_hi = w_dt.astype(BF16)
    w_dt = jnp.concatenate([w_dt_hi, (w_dt - w_dt_hi.astype(F32)).astype(BF16)], axis=1)
    half = RET_QK_DIM // 2
    inv_freq = ROPE_BASE ** (-jnp.arange(half, dtype=F32) / half)
    ang = jnp.arange(seq, dtype=F32)[:, None] * inv_freq[None, :]
    cos, sin = jnp.cos(ang), jnp.sin(ang)

    w_ret_b, w_ssm_b, w_out_b = w_ret_out.astype(BF16), w_ssm_out.astype(BF16), w_out.astype(BF16)
    w_r_hi = w_router.T.astype(BF16)
    w_router_t = jnp.concatenate([w_r_hi, (w_router.T - w_r_hi.astype(F32)).astype(BF16)], axis=0)
    bm = FFN_BLOCK
    slab = (SLAB_ROWS, LANES)

    def mixer(row0, tg, bg):
        proj, dt_raw = _inproj_call(x2, row0, tg, norm_mix.reshape(1, d), mod3, cos, sin, w_main, w_dt,
                                    conv_w, conv_b, 2 * qk_w + 2 * v_w + d_inner, seq, plan["tm_in"], 2 * qk_w)
        ya, yb = _mixers_call(proj, dt_raw, w_ret_b, dt_bias, a_log, d_skip, ssm_norm, w_ssm_b, bg, seq,
                              plan["ret_chunk"], plan["ssm_chunk"])
        x1, h2, idx, rank, prow, cnt = _merge_call(ya, yb, proj, x2, row0, mod3, norm_ffn.reshape(1, d),
                                                   w_out_b, w_router_t, b_router, seq, plan["tm_merge"])
        counts = cnt[:, 0]
        padded = ((counts + bm - 1) // bm) * bm
        pad_end = jnp.cumsum(padded)
        start_pad = pad_end - padded
        n_blocks = (tg * TOP_K) // bm + N_EXPERTS
        e_ids = jnp.arange(N_EXPERTS, dtype=jnp.int32)[:, None, None, None]
        dest = rank + jnp.sum(jnp.where(idx[None] == e_ids, start_pad[:, None, None, None], 0), axis=0)
        dest = dest[:, :TOP_K, :].transpose(1, 0, 2).reshape(TOP_K, tg).astype(jnp.int32)
        n_real = pad_end[-1] // bm
        blk_valid = (jnp.arange(n_blocks) < n_real).astype(jnp.int32)
        blk_row = jnp.minimum(jnp.arange(n_blocks), n_real - 1).astype(jnp.int32)
        blk_e = jnp.minimum(jnp.sum(pad_end[None, :] <= (blk_row * bm)[:, None], axis=1),
                            N_EXPERTS - 1).astype(jnp.int32)
        xs = _sc_scatter_rows(h2.reshape((tg,) + slab), dest, n_blocks * bm)
        return dict(x1=x1, prow=prow, dest=dest, blocks=(blk_e, blk_row, blk_valid), xs=xs, row0=row0)

    def experts(m):
        n_rows = m["xs"].shape[0]
        ys = _ffn_call(*m["blocks"], m["xs"].reshape(n_rows * SLAB_ROWS, LANES),
                       w_gate_up, b_gate_up, w_down, b_down, bm)
        return _sc_gather_rows(ys.reshape((n_rows,) + slab), m["dest"].reshape(-1))

    n_groups = N_GROUPS if bsz % N_GROUPS == 0 else 1
    bg = bsz // n_groups
    tg = bg * seq
    groups = [mixer(g * tg, tg, bg) for g in range(n_groups)]
    ytoks = [experts(m) for m in groups]
    out = None
    for y, m in zip(ytoks, groups):
        out = _combine_call(y.reshape(TOP_K * tg * SLAB_ROWS, LANES), m["prow"], m["x1"], mod3,
                            norm_final.reshape(1, d), seq, plan["tm_moe"], m["row0"], t, out)
    return out


def kernel(x, c, w_ada, b_ada, norm_mix, norm_ffn, w_in, conv_w, conv_b, dt_bias, a_log, d_skip, ssm_norm,
           w_ret_out, w_ssm_out, w_out, w_router, b_router, w_gate_up, b_gate_up, w_down, b_down, norm_final):
    bsz, seq, d = x.shape
    depth = w_ada.shape[0]
    assert depth == 1, "the final norm is fused into the single layer's last kernel"
    x2 = x.reshape(bsz * seq, d)
    l = 0
    mod = _mod_call(c, w_ada[l], b_ada[l])
    mod3 = mod.reshape(bsz * N_MOD, 1, d)
    out = _layer(x2, mod3, bsz, seq, norm_mix[l], norm_ffn[l], w_in[l], conv_w[l], conv_b[l], dt_bias[l],
                 a_log[l], d_skip[l], ssm_norm[l], w_ret_out[l], w_ssm_out[l], w_out[l], w_router[l],
                 b_router[l], w_gate_up[l], b_gate_up[l], w_down[l], b_down[l], norm_final)
    return out.reshape(bsz, seq, d)
```

```python
import functools
import math

import numpy as np
import jax
import jax.numpy as jnp
from jax import lax
from jax.experimental import pallas as pl
from jax.experimental.pallas import tpu as pltpu
from jax.experimental.pallas import tpu_sc as plsc

F32 = jnp.float32
BF16 = jnp.bfloat16
HIGHEST = lax.Precision.HIGHEST

EPS = 1e-6
N_MOD = 6
RET_HEADS = 4
RET_QK_DIM = 256
RET_V_DIM = 512
ROPE_BASE = 10000.0
SSM_HEAD_DIM = 64
SSM_GROUPS = 8
SSM_STATE = 128
SSM_CONV = 4
N_EXPERTS = 32
TOP_K = 4
SWIGLU_LIMIT = 7.0
SWIGLU_ALPHA = 1.702

LANES = 128
SUBLANES = 8
VMEM_LIMIT = 56 * 1024 * 1024

RET_CHUNK = 256
SSM_CHUNK = 128
FFN_BLOCK = 512
SC_CORES = 2
SC_SUBCORES = 16
SC_GROUP = 64
N_GROUPS = 2


def _params(sem, vmem=VMEM_LIMIT):
    return pltpu.CompilerParams(dimension_semantics=sem, vmem_limit_bytes=vmem)


def _nt_dot(a, b, **kw):
    return lax.dot_general(a, b, (((1,), (1,)), ((), ())), preferred_element_type=F32, **kw)


def _tn_dot(a, b, **kw):
    return lax.dot_general(a, b, (((0,), (0,)), ((), ())), preferred_element_type=F32, **kw)


def _silu(v):
    return v * jax.nn.sigmoid(v)


SLAB_ROWS = 4
HIGH_HALF = 0xFFFF0000


def _store_slabs(ref, vals, n):
    for s in range(SLAB_ROWS):
        lo = vals[:, s * LANES:(s + 1) * LANES].astype(BF16).astype(F32)
        hi = vals[:, (s + SLAB_ROWS) * LANES:(s + SLAB_ROWS + 1) * LANES].astype(BF16).astype(F32)
        word = (pltpu.bitcast(lo, jnp.uint32) >> 16) | (pltpu.bitcast(hi, jnp.uint32) & jnp.uint32(HIGH_HALF))
        ref[pl.ds(s, n, stride=SLAB_ROWS), :] = word


def _load_slabs(ref, n, base=0):
    lo, hi = [], []
    for s in range(SLAB_ROWS):
        word = ref[pl.ds(base + s, n, stride=SLAB_ROWS), :]
        lo.append(pltpu.bitcast(word << 16, F32))
        hi.append(pltpu.bitcast(word & jnp.uint32(HIGH_HALF), F32))
    return lo + hi


def _mod_kernel(c_ref, w_ref, b_ref, o_ref):
    cond = _silu(c_ref[...])
    o_ref[...] = jnp.dot(cond, w_ref[...], preferred_element_type=F32, precision=HIGHEST) + b_ref[...]


def _mod_call(c, w_ada, b_ada):
    bsz, d = c.shape
    n = w_ada.shape[1]
    return pl.pallas_call(
        _mod_kernel,
        out_shape=jax.ShapeDtypeStruct((bsz, n), F32),
        grid=(n // d,),
        in_specs=[pl.BlockSpec((bsz, d), lambda j: (0, 0)),
                  pl.BlockSpec((d, d), lambda j: (0, j)),
                  pl.BlockSpec((1, d), lambda j: (0, j))],
        out_specs=pl.BlockSpec((bsz, d), lambda j: (0, j)),
        compiler_params=_params(("arbitrary",)),
        name="mod",
    )(c, w_ada, b_ada.reshape(1, n))


def _inproj_kernel(x_ref, nw_ref, sc_ref, sh_ref, cos_ref, sin_ref, w_ref, wdt_ref, cw_ref, cb_ref,
                   o_ref, dt_ref, h_s, work, carry, *, conv_j0, conv_nj, silu_j, sigm_j, tiles_per_seq,
                   tm, tn, sub):
    i = pl.program_id(0)
    j = pl.program_id(1)
    n_dt = dt_ref.shape[1]
    rows = min(tm, 256)

    @pl.when(j == 0)
    def _():
        xf = x_ref[...]
        ms = jnp.mean(xf * xf, axis=-1, keepdims=True)
        y = xf * lax.rsqrt(ms + EPS) * nw_ref[...]
        hm = y * (1.0 + sc_ref[0]) + sh_ref[0]
        hb = hm.astype(BF16)
        h_s[...] = hb
        h_lo = (hm - hb.astype(F32)).astype(BF16)
        d_hi = jnp.dot(hb, wdt_ref[...], preferred_element_type=F32)
        d_lo = jnp.dot(h_lo, wdt_ref[:, :n_dt], preferred_element_type=F32)
        dt_ref[...] = d_hi[:, :n_dt] + d_hi[:, n_dt:] + d_lo
        half = RET_QK_DIM // 2
        for p in range(tn // sub):
            for r in range(tm // rows):
                rs = slice(r * rows, (r + 1) * rows)
                acc = jnp.dot(h_s[rs, :], w_ref[:, p * sub:(p + 1) * sub], preferred_element_type=F32)
                cos = cos_ref[rs, :]
                sin = sin_ref[rs, :]
                for cc in range(sub // RET_QK_DIM):
                    c = p * (sub // RET_QK_DIM) + cc
                    a = acc[:, cc * RET_QK_DIM: cc * RET_QK_DIM + half]
                    b = acc[:, cc * RET_QK_DIM + half: (cc + 1) * RET_QK_DIM]
                    scale = 1.0 if c < RET_HEADS else RET_QK_DIM ** -0.5
                    o_ref[rs, c * RET_QK_DIM: c * RET_QK_DIM + half] = ((a * cos - b * sin) * scale).astype(BF16)
                    o_ref[rs, c * RET_QK_DIM + half: (c + 1) * RET_QK_DIM] = (
                        (a * sin + b * cos) * scale).astype(BF16)

    is_conv = (j >= conv_j0) & (j < conv_j0 + conv_nj)

    @pl.when(is_conv)
    def _():
        cj = j - conv_j0
        pad = SUBLANES

        @pl.when(i % tiles_per_seq == 0)
        def _():
            carry[cj] = jnp.zeros(carry.shape[1:], F32)

        for p in range(tn // sub):
            for r in range(tm // rows):
                r0 = r * rows
                acc = jnp.dot(h_s[r0:r0 + rows, :], w_ref[:, p * sub:(p + 1) * sub], preferred_element_type=F32)
                for cc in range(sub // LANES):
                    c = p * (sub // LANES) + cc
                    cols = slice(c * LANES, (c + 1) * LANES)
                    if r == 0:
                        work[c, 0:pad, :] = carry[cj, c]
                    lo = pad + r0
                    work[c, lo:lo + rows, :] = acc[:, cc * LANES:(cc + 1) * LANES]
                    conv = cb_ref[:, cols] + cw_ref[SSM_CONV - 1:SSM_CONV, cols] * work[c, lo:lo + rows, :]
                    for k in range(SSM_CONV - 1):
                        shift = SSM_CONV - 1 - k
                        conv = conv + cw_ref[k:k + 1, cols] * work[c, lo - shift:lo - shift + rows, :]
                    if r0 + rows == tm:
                        carry[cj, c] = work[c, tm:tm + pad, :]
                    o_ref[r0:r0 + rows, cols] = _silu(conv).astype(BF16)

    def plain(act):
        for p in range(tn // sub):
            for r in range(tm // rows):
                acc = jnp.dot(h_s[r * rows:(r + 1) * rows, :], w_ref[:, p * sub:(p + 1) * sub],
                              preferred_element_type=F32)
                o_ref[r * rows:(r + 1) * rows, p * sub:(p + 1) * sub] = act(acc).astype(BF16)

    is_silu = (j >= silu_j[0]) & (j < silu_j[1])
    is_sigm = (j >= sigm_j[0]) & (j < sigm_j[1])
    pl.when(is_silu)(lambda: plain(_silu))
    pl.when(is_sigm)(lambda: plain(jax.nn.sigmoid))
    pl.when((j != 0) & jnp.logical_not(is_conv | is_silu | is_sigm))(lambda: plain(lambda v: v))


def _inproj_call(x2, row0, t, norm_w, mod3, cos, sin, w_main, w_dt, conv_w, conv_b, conv_off, seq, tm, tn):
    d = x2.shape[1]
    n = w_main.shape[1]
    conv_dim = conv_w.shape[1]
    tiles_per_seq = seq // tm
    off = row0 // tm
    assert tn == 2 * RET_HEADS * RET_QK_DIM, "rotary epilogue expects q and k in the first column tile"
    assert conv_off % tn == 0 and conv_dim % tn == 0
    conv_j0, conv_nj = conv_off // tn, conv_dim // tn
    sub = 512
    g_off = 2 * RET_HEADS * RET_QK_DIM + RET_HEADS * RET_V_DIM
    assert g_off % tn == 0 and (conv_off - g_off) % tn == 0 and (n - conv_off - conv_dim) % tn == 0
    silu_j = (g_off // tn, conv_off // tn)
    sigm_j = ((conv_off + conv_dim) // tn, n // tn)
    kern = functools.partial(_inproj_kernel, conv_j0=conv_j0, conv_nj=conv_nj, silu_j=silu_j, sigm_j=sigm_j,
                             tiles_per_seq=tiles_per_seq, tm=tm, tn=tn, sub=sub)
    conv_idx = lambda i, j: (0, jnp.clip(j - conv_j0, 0, conv_nj - 1))
    return pl.pallas_call(
        kern,
        out_shape=(jax.ShapeDtypeStruct((t, n), BF16), jax.ShapeDtypeStruct((t, LANES), F32)),
        grid=(t // tm, n // tn),
        in_specs=[
            pl.BlockSpec((tm, d), lambda i, j: (i + off, 0)),
            pl.BlockSpec((1, d), lambda i, j: (0, 0)),
            pl.BlockSpec((1, 1, d), lambda i, j: (((i + off) // tiles_per_seq) * N_MOD + 1, 0, 0)),
            pl.BlockSpec((1, 1, d), lambda i, j: (((i + off) // tiles_per_seq) * N_MOD + 0, 0, 0)),
            pl.BlockSpec((tm, LANES), lambda i, j: (i % tiles_per_seq, 0)),
            pl.BlockSpec((tm, LANES), lambda i, j: (i % tiles_per_seq, 0)),
            pl.BlockSpec((d, tn), lambda i, j: (0, j)),
            pl.BlockSpec((d, 2 * LANES), lambda i, j: (0, 0)),
            pl.BlockSpec((SSM_CONV, tn), conv_idx),
            pl.BlockSpec((1, tn), conv_idx),
        ],
        out_specs=(pl.BlockSpec((tm, tn), lambda i, j: (i, j)),
                   pl.BlockSpec((tm, LANES), lambda i, j: (i, 0))),
        scratch_shapes=[pltpu.VMEM((tm, d), BF16),
                        pltpu.VMEM((tn // LANES, tm + SUBLANES, LANES), F32),
                        pltpu.VMEM((conv_nj, tn // LANES, SUBLANES, LANES), F32)],
        compiler_params=_params(("arbitrary", "arbitrary")),
        name="inproj",
    )(x2, norm_w, mod3, mod3, cos, sin, w_main, w_dt, conv_w, conv_b.reshape(1, conv_dim))


def _retention_body(q_ref, k_ref, v_ref, g_ref, din_ref, dq_ref, dk_ref, w_ref, o_ref, state, decay_c):
    acc = None
    for h in range(RET_HEADS):
        qh = q_ref[:, h * RET_QK_DIM:(h + 1) * RET_QK_DIM]
        kh = k_ref[:, h * RET_QK_DIM:(h + 1) * RET_QK_DIM]
        vh = v_ref[:, h * RET_V_DIM:(h + 1) * RET_V_DIM]
        scores = _nt_dot(qh, kh) * din_ref[h]
        inner = jnp.dot(scores.astype(BF16), vh, preferred_element_type=F32)
        st = state[h]
        cross = jnp.dot(qh, st.astype(BF16), preferred_element_type=F32) * dq_ref[h]
        kd = (kh.astype(F32) * dk_ref[h]).astype(BF16)
        state[h] = st * decay_c[h] + _tn_dot(kd, vh)
        ret = inner + cross
        ret = ret * lax.rsqrt(jnp.mean(ret * ret, axis=-1, keepdims=True) + EPS)
        ret = ret * g_ref[:, h * RET_V_DIM:(h + 1) * RET_V_DIM].astype(F32)
        part = jnp.dot(ret.astype(BF16), w_ref[h * RET_V_DIM:(h + 1) * RET_V_DIM, :],
                       preferred_element_type=F32)
        acc = part if acc is None else acc + part
    o_ref[...] = acc.astype(o_ref.dtype)


def _retention_tables(chunk):
    lg = np.log(1.0 - 2.0 ** (-5.0 - np.arange(RET_HEADS, dtype=np.float64)))
    idx = np.arange(chunk, dtype=np.float64)
    rel = idx[:, None] - idx[None, :]
    causal = rel >= 0
    din = np.where(causal[None], np.exp(np.where(causal, rel, 0.0)[None] * lg[:, None, None]), 0.0)
    dq = np.exp((idx + 1.0)[None, :, None] * lg[:, None, None])
    dk = np.exp((chunk - 1.0 - idx)[None, :, None] * lg[:, None, None])
    dc = tuple(float(v) for v in np.exp(chunk * lg))
    return (jnp.asarray(din, F32), jnp.asarray(dq, F32), jnp.asarray(dk, F32), dc)


def _ssd_body(z_ref, xbc_ref, dt_ref, dtb_ref, alog_ref, dsk_ref, nw_ref,
              tril_ref, exp_ref, w_ref, o_ref, state, yn_s, chunk, d_inner):
    heads_per_group = d_inner // SSM_HEAD_DIM // SSM_GROUPS
    gw = heads_per_group * SSM_HEAD_DIM
    assert SSM_HEAD_DIM * 2 == LANES and gw == 2 * LANES

    dt =jax.nn.softplus(dt_ref[...] + dtb_ref[...])
    a = -jnp.exp(alog_ref[...])
    adt = dt * a
    acs = jnp.dot(tril_ref[...], adt, preferred_element_type=F32, precision=HIGHEST)
    acs_t = acs.T
    dt_x = jnp.dot(dt.astype(BF16), exp_ref[...], preferred_element_type=F32)
    li = lax.broadcasted_iota(jnp.int32, (chunk, chunk), 0)
    si = lax.broadcasted_iota(jnp.int32, (chunk, chunk), 1)
    causal = li >= si
    low_half = si < SSM_HEAD_DIM
    lane_g = lax.broadcasted_iota(jnp.int32, (chunk, gw), 1)

    b_off = d_inner
    c_off = d_inner + SSM_GROUPS * SSM_STATE
    for g in range(SSM_GROUPS):
        bm = xbc_ref[:, b_off + g * SSM_STATE: b_off + (g + 1) * SSM_STATE]
        cm = xbc_ref[:, c_off + g * SSM_STATE: c_off + (g + 1) * SSM_STATE]
        xs_g = xbc_ref[:, g * gw:(g + 1) * gw].astype(F32)
        xdt_g = xs_g * dt_x[:, g * gw:(g + 1) * gw]
        cb = _nt_dot(cm, bm)
        cols, ms, xm = [], [], []
        for jh in range(heads_per_group):
            h = g * heads_per_group + jh
            col = jnp.broadcast_to(acs[:, h:h + 1], (chunk, chunk))
            seg = jnp.exp(jnp.where(causal, col - acs_t[h:h + 1, :], -jnp.inf))
            cols.append(col)
            ms.append((cb * seg).astype(BF16))
            in_head = (lane_g >= jh * SSM_HEAD_DIM) & (lane_g < (jh + 1) * SSM_HEAD_DIM)
            xm.append(jnp.where(in_head, xdt_g, 0.0).astype(BF16))
        y_diag = jnp.dot(jnp.concatenate(ms, axis=-1), jnp.concatenate(xm, axis=0),
                         preferred_element_type=F32)
        a_x = jnp.concatenate([jnp.where(low_half, cols[0], cols[1]),
                               jnp.where(low_half, cols[2], cols[3])], axis=-1)
        e_acs_x = jnp.exp(a_x)
        a_last_x = a_x[chunk - 1:chunk, :]
        st = state[g]
        y_off = jnp.dot(cm, st.astype(BF16), preferred_element_type=F32) * e_acs_x
        xdec = (xdt_g * jnp.exp(a_last_x - a_x)).astype(BF16)
        state[g] = st * e_acs_x[chunk - 1:chunk, :] + _tn_dot(bm, xdec)
        y = y_diag + y_off + dsk_ref[:, g * gw:(g + 1) * gw] * xs_g
        yz = y * z_ref[:, g * gw:(g + 1) * gw].astype(F32)
        yn = yz * lax.rsqrt(jnp.mean(yz * yz, axis=-1, keepdims=True) + EPS) * nw_ref[:, g * gw:(g + 1) * gw]
        yn_s[:, g * gw:(g + 1) * gw] = yn.astype(BF16)
    o_ref[...] = jnp.dot(yn_s[...], w_ref[...], preferred_element_type=F32).astype(o_ref.dtype)


def _mixers_kernel(q_ref, k_ref, v_ref, g_ref, din_ref, dq_ref, dk_ref, wret_ref,
                   z_ref, xbc_ref, dt_ref, dtb_ref, alog_ref, dsk_ref, nw_ref, tril_ref, exp_ref, wssm_ref,
                   ga_ref, gb_ref, x_ref, gm_ref, scf_ref, shf_ref, nf_ref, wo_ref, wr_ref, br_ref, tri_ref,
                   x1_ref, h2_ref, idx_ref, rank_ref, prow_ref, cnt_ref,
                   rstate, sstate, yn_s, ya_s, yb_s, cnt_s, *, decay_c, chunk, ssm_chunk, d_inner):
    b = pl.program_id(0)
    c = pl.program_id(1)

    @pl.when(c == 0)
    def _():
        rstate[...] = jnp.zeros_like(rstate)
        sstate[...] = jnp.zeros_like(sstate)

    @pl.when((b == 0) & (c == 0))
    def _():
        cnt_s[...] = jnp.zeros_like(cnt_s)

    _retention_body(q_ref, k_ref, v_ref, g_ref, din_ref, dq_ref, dk_ref, wret_ref, ya_s, rstate, decay_c)
    for sub in range(chunk // ssm_chunk):
        rows = pl.ds(sub * ssm_chunk, ssm_chunk)
        _ssd_body(z_ref.at[rows, :], xbc_ref.at[rows, :], dt_ref.at[rows, :], dtb_ref, alog_ref, dsk_ref,
                  nw_ref, tril_ref, exp_ref, wssm_ref, yb_s.at[rows, :], sstate, yn_s, ssm_chunk, d_inner)
    _merge_body(ya_s, yb_s, ga_ref, gb_ref, x_ref, gm_ref, scf_ref, shf_ref, nf_ref, wo_ref, wr_ref, br_ref,
                tri_ref, x1_ref, h2_ref, idx_ref, rank_ref, prow_ref, cnt_ref, cnt_s, chunk)


def _mixers_call(proj, dt_raw, x2, row0, mod3, w_ret, dt_bias, a_log, d_skip, ssm_norm, w_ssm,
                 norm_ffn, w_out, w_router_t, b_router, bsz, seq, chunk, ssm_chunk):
    t = proj.shape[0]
    d_inner, d = w_ssm.shape
    conv_dim = d_inner + 2 * SSM_GROUPS * SSM_STATE
    n_heads = d_inner // SSM_HEAD_DIM
    nc = seq // chunk
    gw = d_inner // SSM_GROUPS
    qk_w = RET_HEADS * RET_QK_DIM
    v_w = RET_HEADS * RET_V_DIM
    assert chunk % ssm_chunk == 0
    assert ssm_chunk == LANES, "the per-head decay tiles are built lane-for-lane against the chunk"
    din, dq, dk, dc = _retention_tables(chunk)
    pad_h = lambda v: jnp.pad(v.astype(F32), (0, LANES - n_heads)).reshape(1, LANES)
    tril = jnp.asarray(np.tril(np.ones((ssm_chunk, ssm_chunk), np.float32)))
    expand = np.zeros((LANES, d_inner), np.float32)
    for h in range(n_heads):
        expand[h, h * SSM_HEAD_DIM:(h + 1) * SSM_HEAD_DIM] = 1.0
    expand = jnp.asarray(expand, BF16)
    kern = functools.partial(_mixers_kernel, decay_c=dc, chunk=chunk, ssm_chunk=ssm_chunk, d_inner=d_inner)
    row = lambda b, c: b * nc + c
    z_blk = (2 * qk_w + 2 * v_w) // d_inner
    xbc_blk = (2 * qk_w + 2 * v_w + d_inner) // conv_dim
    full = lambda shape: pl.BlockSpec(shape, lambda b, c: (0,) * len(shape))
    nt = t // chunk
    ga_blk = proj.shape[1] // d - 2
    seq0 = row0 // seq
    modspec = lambda m: pl.BlockSpec((1, 1, d), lambda b, c: ((b + seq0) * N_MOD + m, 0, 0))
    tri = jnp.asarray(np.triu(np.ones((chunk, chunk), np.float32), 1), BF16)
    return pl.pallas_call(
        kern,
        out_shape=(jax.ShapeDtypeStruct((t, d), F32), jax.ShapeDtypeStruct((t * SLAB_ROWS, LANES), jnp.uint32),
                   jax.ShapeDtypeStruct((nt, SUBLANES, chunk), jnp.int32),
                   jax.ShapeDtypeStruct((nt, SUBLANES, chunk), jnp.int32),
                   jax.ShapeDtypeStruct((t, LANES), F32),
                   jax.ShapeDtypeStruct((N_EXPERTS, LANES), jnp.int32)),
        grid=(bsz, nc),
        in_specs=[
            pl.BlockSpec((chunk, qk_w), lambda b, c: (row(b, c), 0)),
            pl.BlockSpec((chunk, qk_w), lambda b, c: (row(b, c), 1)),
            pl.BlockSpec((chunk, v_w), lambda b, c: (row(b, c), 1)),
            pl.BlockSpec((chunk, v_w), lambda b, c: (row(b, c), 2)),
            full((RET_HEADS, chunk, chunk)), full((RET_HEADS, chunk, 1)), full((RET_HEADS, chunk, 1)),
            full((v_w, d)),
            pl.BlockSpec((chunk, d_inner), lambda b, c: (row(b, c), z_blk)),
            pl.BlockSpec((chunk, conv_dim), lambda b, c: (row(b, c), xbc_blk)),
            pl.BlockSpec((chunk, LANES), lambda b, c: (row(b, c), 0)),
            full((1, LANES)), full((1, LANES)),
            full((1, d_inner)), full((1, d_inner)), full((ssm_chunk, ssm_chunk)), full((LANES, d_inner)),
            full((d_inner, d)),
            pl.BlockSpec((chunk, d), lambda b, c: (row(b, c), ga_blk)),
            pl.BlockSpec((chunk, d), lambda b, c: (row(b, c), ga_blk + 1)),
            pl.BlockSpec((chunk, d), lambda b, c: (row0 // chunk + row(b, c), 0)),
            modspec(2), modspec(4), modspec(3),
            full((1, d)), full((d, d)), full((2 * N_EXPERTS, d)), full((N_EXPERTS, 1)), full((chunk, chunk)),
        ],
        out_specs=(pl.BlockSpec((chunk, d), lambda b, c: (row(b, c), 0)),
                   pl.BlockSpec((chunk * SLAB_ROWS, LANES), lambda b, c: (row(b, c), 0)),
                   pl.BlockSpec((1, SUBLANES, chunk), lambda b, c: (row(b, c), 0, 0)),
                   pl.BlockSpec((1, SUBLANES, chunk), lambda b, c: (row(b, c), 0, 0)),
                   pl.BlockSpec((chunk, LANES), lambda b, c: (row(b, c), 0)),
                   full((N_EXPERTS, LANES))),
        scratch_shapes=[pltpu.VMEM((RET_HEADS, RET_QK_DIM, RET_V_DIM), F32),
                        pltpu.VMEM((SSM_GROUPS, SSM_STATE, gw), F32),
                        pltpu.VMEM((ssm_chunk, d_inner), BF16),
                        pltpu.VMEM((chunk, d), F32), pltpu.VMEM((chunk, d), F32),
                        pltpu.VMEM((N_EXPERTS, LANES), F32)],
        compiler_params=_params(("arbitrary", "arbitrary")),
        name="mixers",
    )(proj, proj, proj, proj, din, dq, dk, w_ret,
      proj, proj, dt_raw, pad_h(dt_bias), pad_h(a_log),
      jnp.repeat(d_skip.astype(F32), SSM_HEAD_DIM).reshape(1, d_inner), ssm_norm.reshape(1, d_inner),
      tril, expand, w_ssm,
      proj, proj, x2, mod3, mod3, mod3, norm_ffn, w_out, w_router_t, b_router.reshape(N_EXPERTS, 1), tri)


def _merge_body(ya_ref, yb_ref, ga_ref, gb_ref, x_ref, gm_ref, scf_ref, shf_ref, nw_ref, wo_ref,
                wr_ref, br_ref, tri_ref,
                x1_ref, h2_ref, idx_ref, rank_ref, prow_ref, cnt_ref, cnt_s, tm):
    merged = (ga_ref[...].astype(F32) * ya_ref[...].astype(F32)
              + gb_ref[...].astype(F32) * yb_ref[...].astype(F32))
    mo = jnp.dot(merged.astype(BF16), wo_ref[...], preferred_element_type=F32)
    x1 = x_ref[...] + gm_ref[0] * mo
    x1_ref[...] = x1
    ms = jnp.mean(x1 * x1, axis=-1, keepdims=True)
    h2 = x1 * lax.rsqrt(ms + EPS) * nw_ref[...] * (1.0 + scf_ref[0]) + shf_ref[0]
    _store_slabs(h2_ref, h2, tm)

    h_hi = h2.astype(BF16)
    h_lo = (h2 - h_hi.astype(F32)).astype(BF16)
    lg2 = _nt_dot(wr_ref[...], h_hi)
    lg = lg2[:N_EXPERTS] + lg2[N_EXPERTS:] + _nt_dot(wr_ref[:N_EXPERTS, :], h_lo) + br_ref[...]
    sub = lax.broadcasted_iota(jnp.int32, lg.shape, 0)
    work = lg
    vals, idxs, sels = [], [], []
    for _ in range(TOP_K):
        m = jnp.max(work, axis=0, keepdims=True)
        ik = jnp.min(jnp.where(work == m, sub, N_EXPERTS), axis=0, keepdims=True)
        sel = sub == ik
        vals.append(m)
        idxs.append(ik)
        sels.append(sel)
        work = jnp.where(sel, -jnp.inf, work)
    exps = [jnp.exp(v - vals[0]) for v in vals]
    denom = exps[0]
    for e in exps[1:]:
        denom = denom + e
    probs = [e / denom for e in exps]

    base = cnt_s[:, 0:1]
    ranks = []
    for k in range(TOP_K):
        mk = jnp.where(sels[k], 1.0, 0.0)
        pre = jnp.dot(mk.astype(BF16), tri_ref[...], preferred_element_type=F32)
        ranks.append(jnp.sum(jnp.where(sels[k], pre + base, 0.0), axis=0, keepdims=True))
        base = base + jnp.sum(mk, axis=1, keepdims=True)
    cnt_s[...] = jnp.broadcast_to(base, cnt_s.shape)
    cnt_ref[...] = cnt_s[...].astype(jnp.int32)

    zi = jnp.zeros((SUBLANES - TOP_K, tm), jnp.int32)
    idx_ref[0] = jnp.concatenate(idxs + [zi], axis=0)
    rank_ref[0] = jnp.concatenate([r.astype(jnp.int32) for r in ranks] + [zi], axis=0)
    pt = jnp.concatenate(probs + [jnp.zeros((LANES - TOP_K, tm), F32)], axis=0)
    prow_ref[...] = pt.T


def _sc_mesh():
    return plsc.VectorSubcoreMesh(core_axis_name="c", subcore_axis_name="s")


def _sc_worker():
    return lax.axis_index("s") * SC_CORES + lax.axis_index("c")


def _sc_scatter_rows(rows, dest, n_out):
    t = rows.shape[0]
    n_k = dest.shape[0]
    g = SC_GROUP
    n_w = SC_CORES * SC_SUBCORES
    assert t % (n_w * g) == 0
    cpw = t // (n_w * g)
    dest_w = dest.reshape(n_k, n_w, cpw, g).transpose(1, 0, 2, 3)

    @functools.partial(
        pl.kernel, mesh=_sc_mesh(),
        out_type=jax.ShapeDtypeStruct((n_out,) + rows.shape[1:], rows.dtype),
        scratch_types=[pltpu.VMEM((n_k, cpw, g), jnp.int32),
                       pltpu.VMEM((g,) + rows.shape[1:], rows.dtype),
                       pltpu.SemaphoreType.DMA],
    )
    def scatter(rows_hbm, dest_hbm, out_hbm, idx_v, rows_v, sem):
        wid = _sc_worker()
        pltpu.sync_copy(dest_hbm.at[wid], idx_v)

        @pl.loop(0, cpw)
        def _(cc):
            r0 = pl.multiple_of((wid * cpw + cc) * g, g)
            pltpu.sync_copy(rows_hbm.at[pl.ds(r0, g)], rows_v)
            copies = [pltpu.async_copy(rows_v, out_hbm.at[idx_v.at[k, cc]], sem) for k in range(n_k)]
            for cp in copies:
                cp.wait()

    return scatter(rows, dest_w)


def _sc_gather_rows(table, idx):
    m = idx.shape[0]
    g = SC_GROUP
    n_w = SC_CORES * SC_SUBCORES
    assert m % (n_w * g) == 0
    per_w = m // n_w

    @functools.partial(
        pl.kernel, mesh=_sc_mesh(),
        out_type=jax.ShapeDtypeStruct((m,) + table.shape[1:], table.dtype),
        scratch_types=[pltpu.VMEM((per_w,), jnp.int32),
                       pltpu.VMEM((g,) + table.shape[1:], table.dtype),
                       pltpu.SemaphoreType.DMA],
    )
    def gather(table_hbm, idx_hbm, out_hbm, idx_v, rows_v, sem):
        base = _sc_worker() * per_w
        pltpu.sync_copy(idx_hbm.at[pl.ds(base, per_w)], idx_v)

        @pl.loop(0, per_w // g)
        def _(cc):
            off = pl.multiple_of(cc * g, g)
            pltpu.async_copy(table_hbm.at[idx_v.at[pl.ds(off, g)]], rows_v, sem).wait()
            pltpu.sync_copy(rows_v, out_hbm.at[pl.ds(base + off, g)])

    return gather(table, idx)


def _ffn_kernel(be_ref, br_ref, bv_ref, first_ref, slot_ref, next_ref,
                x_ref, wgu_hbm, bgu_ref, wd_hbm, bd_ref, o_ref, wgu_f, wd_f, wgu_s, wd_s, sem, *, bm, d_ff):
    i = pl.program_id(0)

    def fetch(e, slot):
        return (pltpu.make_async_copy(wgu_hbm.at[e], wgu_f.at[slot], sem.at[0, slot]),
                pltpu.make_async_copy(wd_hbm.at[e], wd_f.at[slot], sem.at[1, slot]))

    @pl.when(i == 0)
    def _():
        for cp in fetch(be_ref[0], slot_ref[0]):
            cp.start()

    @pl.when(first_ref[i] == 1)
    def _():
        slot = slot_ref[i]
        for cp in fetch(be_ref[i], slot):
            cp.wait()
        wgu_s[...] = wgu_f[slot].astype(BF16)
        wd_s[...] = wd_f[slot].astype(BF16)

        @pl.when(next_ref[i] >= 0)
        def _():
            for cp in fetch(next_ref[i], 1 - slot):
                cp.start()

    @pl.when(bv_ref[i] == 1)
    def _():
        x = jnp.concatenate(_load_slabs(x_ref, bm), axis=-1).astype(BF16)
        gu = jnp.dot(x, wgu_s[...], preferred_element_type=F32) + bgu_ref[0]
        gate = jnp.minimum(gu[:, :d_ff], SWIGLU_LIMIT)
        up = jnp.clip(gu[:, d_ff:], -SWIGLU_LIMIT, SWIGLU_LIMIT)
        act = gate * jax.nn.sigmoid(SWIGLU_ALPHA * gate) * (up + 1.0)
        y = jnp.dot(act.astype(BF16), wd_s[...], preferred_element_type=F32) + bd_ref[0]
        _store_slabs(o_ref, y, bm)

    @pl.when(bv_ref[i] == 0)
    def _():
        o_ref[...] = jnp.zeros_like(o_ref)


def _ffn_call(blk_e, blk_row, blk_valid, xs, w_gu, b_gu, w_d, b_d, bm):
    n_e, d, f2 = w_gu.shape
    d_ff = f2 // 2
    nb = blk_e.shape[0]
    first = jnp.concatenate([jnp.ones((1,), jnp.int32), (blk_e[1:] != blk_e[:-1]).astype(jnp.int32)])
    slot = (jnp.cumsum(first) - 1) % 2
    later = jnp.where(blk_e[None, :] > blk_e[:, None], blk_e[None, :], n_e)
    nxt = jnp.min(later, axis=1)
    nxt = jnp.where(nxt == n_e, -1, nxt)
    kern = functools.partial(_ffn_kernel, bm=bm, d_ff=d_ff)
    imap = lambda f: (lambda i, be, br, bv, fi, sl, nx: f(i, be, br))
    gs = pltpu.PrefetchScalarGridSpec(
        num_scalar_prefetch=6,
        grid=(nb,),
        in_specs=[pl.BlockSpec((bm * SLAB_ROWS, LANES), imap(lambda i, be, br: (br[i], 0))),
                  pl.BlockSpec(memory_space=pl.ANY),
                  pl.BlockSpec((1, 1, f2), imap(lambda i, be, br: (be[i], 0, 0))),
                  pl.BlockSpec(memory_space=pl.ANY),
                  pl.BlockSpec((1, 1, d), imap(lambda i, be, br: (be[i], 0, 0)))],
        out_specs=pl.BlockSpec((bm * SLAB_ROWS, LANES), imap(lambda i, be, br: (i, 0))),
        scratch_shapes=[pltpu.VMEM((2, d, f2), F32), pltpu.VMEM((2, d_ff, d), F32),
                        pltpu.VMEM((d, f2), BF16), pltpu.VMEM((d_ff, d), BF16),
                        pltpu.SemaphoreType.DMA((2, 2))],
    )
    return pl.pallas_call(
        kern,
        out_shape=jax.ShapeDtypeStruct((nb * bm * SLAB_ROWS, LANES), jnp.uint32),
        grid_spec=gs,
        compiler_params=_params(("arbitrary",)),
        name="ffn",
    )(blk_e, blk_row, blk_valid, first, slot.astype(jnp.int32), nxt.astype(jnp.int32),
      xs, w_gu, b_gu.reshape(n_e, 1, f2), w_d, b_d.reshape(n_e, 1, d))


def _combine_kernel(y0_ref, y1_ref, y2_ref, y3_ref, prow_ref, x1_ref, gf_ref, nw_ref, *rest, tm):
    o_ref = rest[-1]
    p = prow_ref[...]
    pieces = [_load_slabs(y_ref, tm) for y_ref in (y0_ref, y1_ref, y2_ref, y3_ref)]
    for s in range(len(pieces[0])):
        moe = None
        for k in range(TOP_K):
            piece = pieces[k][s] * p[:, k:k + 1]
            moe = piece if moe is None else moe + piece
        sl = slice(s * LANES, (s + 1) * LANES)
        o_ref[:, sl] = x1_ref[:, sl] + gf_ref[0][:, sl] * moe
    xo = o_ref[...]
    o_ref[...] = xo * lax.rsqrt(jnp.mean(xo * xo, axis=-1, keepdims=True) + EPS) * nw_ref[...]


def _combine_call(ytok, prow, x1, mod3, norm_final, seq, tm, row0, t_total, out_prev):
    t, d = x1.shape
    nt = t // tm
    tiles_per_seq = seq // tm
    off = row0 // tm
    kern = functools.partial(_combine_kernel, tm=tm)
    yspec = lambda k: pl.BlockSpec((tm * SLAB_ROWS, LANES), lambda i: (k * nt + i, 0))
    in_specs = [yspec(0), yspec(1), yspec(2), yspec(3),
                pl.BlockSpec((tm, LANES), lambda i: (i, 0)),
                pl.BlockSpec((tm, d), lambda i: (i, 0)),
                pl.BlockSpec((1, 1, d), lambda i: (((i + off) // tiles_per_seq) * N_MOD + 5, 0, 0)),
                pl.BlockSpec((1, d), lambda i: (0, 0))]
    args = [ytok, ytok, ytok, ytok, prow, x1, mod3, norm_final]
    aliases = {}
    if out_prev is not None:
        in_specs.append(pl.BlockSpec(memory_space=pl.ANY))
        aliases = {len(args): 0}
        args.append(out_prev)
    return pl.pallas_call(
        kern,
        out_shape=jax.ShapeDtypeStruct((t_total, d), F32),
        grid=(nt,),
        in_specs=in_specs,
        out_specs=pl.BlockSpec((tm, d), lambda i: (i + off, 0)),
        input_output_aliases=aliases,
        compiler_params=_params(("arbitrary",)),
        name="combine",
    )(*args)


def _plan(seq):
    def fit(pref):
        tm = min(pref, seq)
        assert seq % tm == 0
        return tm
    return dict(tm_in=fit(1024), tm_merge=fit(512), tm_moe=fit(512),
                ret_chunk=fit(RET_CHUNK), ssm_chunk=fit(SSM_CHUNK))


def _layer(x2, mod3, bsz, seq, norm_mix, norm_ffn, w_in, conv_w, conv_b, dt_bias, a_log, d_skip, ssm_norm,
           w_ret_out, w_ssm_out, w_out, w_router, b_router, w_gate_up, b_gate_up, w_down, b_down,
           norm_final):
    t, d = x2.shape
    plan = _plan(seq)
    qk_w = RET_HEADS * RET_QK_DIM
    v_w = RET_HEADS * RET_V_DIM
    d_inner = w_ssm_out.shape[0]
    conv_dim = conv_w.shape[1]
    n_heads = d_inner // SSM_HEAD_DIM
    dt_off = 2 * qk_w + 2 * v_w + d_inner + conv_dim

    w_main = jnp.concatenate([w_in[:, :dt_off], w_in[:, dt_off + n_heads:]], axis=1).astype(BF16)
    w_dt = jnp.pad(w_in[:, dt_off:dt_off + n_heads], ((0, 0), (0, LANES - n_heads)))
    w_dt_hi = w_dt.astype(BF16)
    w_dt = jnp.concatenate([w_dt_hi, (w_dt - w_dt_hi.astype(F32)).astype(BF16)], axis=1)
    half = RET_QK_DIM // 2
    inv_freq = ROPE_BASE ** (-jnp.arange(half, dtype=F32) / half)
    ang = jnp.arange(seq, dtype=F32)[:, None] * inv_freq[None, :]
    cos, sin = jnp.cos(ang), jnp.sin(ang)

    w_ret_b, w_ssm_b, w_out_b = w_ret_out.astype(BF16), w_ssm_out.astype(BF16), w_out.astype(BF16)
    w_r_hi = w_router.T.astype(BF16)
    w_router_t = jnp.concatenate([w_r_hi, (w_router.T - w_r_hi.astype(F32)).astype(BF16)], axis=0)
    bm = FFN_BLOCK
    slab = (SLAB_ROWS, LANES)

    def mixer(row0, tg, bg):
        proj, dt_raw = _inproj_call(x2, row0, tg, norm_mix.reshape(1, d), mod3, cos, sin, w_main, w_dt,
                                    conv_w, conv_b, 2 * qk_w + 2 * v_w + d_inner, seq, plan["tm_in"], 2 * qk_w)
        x1, h2, idx, rank, prow, cnt = _mixers_call(
            proj, dt_raw, x2, row0, mod3, w_ret_b, dt_bias, a_log, d_skip, ssm_norm, w_ssm_b,
            norm_ffn.reshape(1, d), w_out_b, w_router_t, b_router, bg, seq, plan["ret_chunk"], plan["ssm_chunk"])
        counts = cnt[:, 0]
        padded = ((counts + bm - 1) // bm) * bm
        pad_end = jnp.cumsum(padded)
        start_pad = pad_end - padded
        n_blocks = (tg * TOP_K) // bm + N_EXPERTS
        e_ids = jnp.arange(N_EXPERTS, dtype=jnp.int32)[:, None, None, None]
        dest = rank + jnp.sum(jnp.where(idx[None] == e_ids, start_pad[:, None, None, None], 0), axis=0)
        dest = dest[:, :TOP_K, :].transpose(1, 0, 2).reshape(TOP_K, tg).astype(jnp.int32)
        n_real = pad_end[-1] // bm
        blk_valid = (jnp.arange(n_blocks) < n_real).astype(jnp.int32)
        blk_row = jnp.minimum(jnp.arange(n_blocks), n_real - 1).astype(jnp.int32)
        blk_e = jnp.minimum(jnp.sum(pad_end[None, :] <= (blk_row * bm)[:, None], axis=1),
                            N_EXPERTS - 1).astype(jnp.int32)
        xs = _sc_scatter_rows(h2.reshape((tg,) + slab), dest, n_blocks * bm)
        return dict(x1=x1, prow=prow, dest=dest, blocks=(blk_e, blk_row, blk_valid), xs=xs, row0=row0)

    def experts(m):
        n_rows = m["xs"].shape[0]
        ys = _ffn_call(*m["blocks"], m["xs"].reshape(n_rows * SLAB_ROWS, LANES),
                       w_gate_up, b_gate_up, w_down, b_down, bm)
        return _sc_gather_rows(ys.reshape((n_rows,) + slab), m["dest"].reshape(-1))

    n_groups = N_GROUPS if bsz % N_GROUPS == 0 else 1
    bg = bsz // n_groups
    tg = bg * seq
    groups = [mixer(g * tg, tg, bg) for g in range(n_groups)]
    ytoks = [experts(m) for m in groups]
    out = None
    for y, m in zip(ytoks, groups):
        out = _combine_call(y.reshape(TOP_K * tg * SLAB_ROWS, LANES), m["prow"], m["x1"], mod3,
                            norm_final.reshape(1, d), seq, plan["tm_moe"], m["row0"], t, out)
    return out


def kernel(x, c, w_ada, b_ada, norm_mix, norm_ffn, w_in, conv_w, conv_b, dt_bias, a_log, d_skip, ssm_norm,
           w_ret_out, w_ssm_out, w_out, w_router, b_router, w_gate_up, b_gate_up, w_down, b_down, norm_final):
    bsz, seq, d = x.shape
    depth = w_ada.shape[0]
    assert depth == 1, "the final norm is fused into the single layer's last kernel"
    x2 = x.reshape(bsz * seq, d)
    l = 0
    mod = _mod_call(c, w_ada[l], b_ada[l])
    mod3 = mod.reshape(bsz * N_MOD, 1, d)
    out = _layer(x2, mod3, bsz, seq, norm_mix[l], norm_ffn[l], w_in[l], conv_w[l], conv_b[l], dt_bias[l],
                 a_log[l], d_skip[l], ssm_norm[l], w_ret_out[l], w_ssm_out[l], w_out[l], w_router[l],
                 b_router[l], w_gate_up[l], b_gate_up[l], w_down[l], b_down[l], norm_final)
    return out.reshape(bsz, seq, d)
```

```python
import functools
import math

import numpy as np
import jax
import jax.numpy as jnp
from jax import lax
from jax.experimental import pallas as pl
from jax.experimental.pallas import tpu as pltpu
from jax.experimental.pallas import tpu_sc as plsc

F32 = jnp.float32
BF16 = jnp.bfloat16
HIGHEST = lax.Precision.HIGHEST

EPS = 1e-6
N_MOD = 6
RET_HEADS = 4
RET_QK_DIM = 256
RET_V_DIM = 512
ROPE_BASE = 10000.0
SSM_HEAD_DIM = 64
SSM_GROUPS = 8
SSM_STATE = 128
SSM_CONV = 4
N_EXPERTS = 32
TOP_K = 4
SWIGLU_LIMIT = 7.0
SWIGLU_ALPHA = 1.702

LANES = 128
SUBLANES = 8
VMEM_LIMIT = 56 * 1024 * 1024

RET_CHUNK = 256
SSM_CHUNK = 128
FFN_BLOCK = 512
SC_CORES = 2
SC_SUBCORES = 16
SC_GROUP = 64
N_GROUPS = 2


def _params(sem, vmem=VMEM_LIMIT):
    return pltpu.CompilerParams(dimension_semantics=sem, vmem_limit_bytes=vmem)


def _nt_dot(a, b, **kw):
    return lax.dot_general(a, b, (((1,), (1,)), ((), ())), preferred_element_type=F32, **kw)


def _tn_dot(a, b, **kw):
    return lax.dot_general(a, b, (((0,), (0,)), ((), ())), preferred_element_type=F32, **kw)


def _silu(v):
    return v * jax.nn.sigmoid(v)


SLAB_ROWS = 4
HIGH_HALF = 0xFFFF0000
LOG2_E = math.log2(math.e)


def _store_slabs(ref, vals, n):
    for s in range(SLAB_ROWS):
        lo = vals[:, s * LANES:(s + 1) * LANES].astype(BF16).astype(F32)
        hi = vals[:, (s + SLAB_ROWS) * LANES:(s + SLAB_ROWS + 1) * LANES].astype(BF16).astype(F32)
        word = (pltpu.bitcast(lo, jnp.uint32) >> 16) | (pltpu.bitcast(hi, jnp.uint32) & jnp.uint32(HIGH_HALF))
        ref[pl.ds(s, n, stride=SLAB_ROWS), :] = word


def _load_slabs(ref, n, base=0):
    lo, hi = [], []
    for s in range(SLAB_ROWS):
        word = ref[pl.ds(base + s, n, stride=SLAB_ROWS), :]
        lo.append(pltpu.bitcast(word << 16, F32))
        hi.append(pltpu.bitcast(word & jnp.uint32(HIGH_HALF), F32))
    return lo + hi


def _mod_kernel(c_ref, w_ref, b_ref, o_ref):
    cond = _silu(c_ref[...])
    o_ref[...] = jnp.dot(cond, w_ref[...], preferred_element_type=F32, precision=HIGHEST) + b_ref[...]


def _mod_call(c, w_ada, b_ada):
    bsz, d = c.shape
    n = w_ada.shape[1]
    return pl.pallas_call(
        _mod_kernel,
        out_shape=jax.ShapeDtypeStruct((bsz, n), F32),
        grid=(n // d,),
        in_specs=[pl.BlockSpec((bsz, d), lambda j: (0, 0)),
                  pl.BlockSpec((d, d), lambda j: (0, j)),
                  pl.BlockSpec((1, d), lambda j: (0, j))],
        out_specs=pl.BlockSpec((bsz, d), lambda j: (0, j)),
        compiler_params=_params(("arbitrary",)),
        name="mod",
    )(c, w_ada, b_ada.reshape(1, n))


def _inproj_kernel(x_ref, nw_ref, sc_ref, sh_ref, cos_ref, sin_ref, w_ref, wdt_ref, cw_ref, cb_ref,
                   o_ref, dt_ref, h_s, work, carry, *, conv_j0, conv_nj, silu_j, sigm_j, tiles_per_seq,
                   tm, tn, sub):
    i = pl.program_id(0)
    j = pl.program_id(1)
    n_dt = dt_ref.shape[1]
    rows = min(tm, 256)

    @pl.when(j == 0)
    def _():
        xf = x_ref[...]
        ms = jnp.mean(xf * xf, axis=-1, keepdims=True)
        y = xf * lax.rsqrt(ms + EPS) * nw_ref[...]
        hm = y * (1.0 + sc_ref[0]) + sh_ref[0]
        hb = hm.astype(BF16)
        h_s[...] = hb
        h_lo = (hm - hb.astype(F32)).astype(BF16)
        d_hi = jnp.dot(hb, wdt_ref[...], preferred_element_type=F32)
        d_lo = jnp.dot(h_lo, wdt_ref[:, :n_dt], preferred_element_type=F32)
        dt_ref[...] = d_hi[:, :n_dt] + d_hi[:, n_dt:] + d_lo
        half = RET_QK_DIM // 2
        for p in range(tn // sub):
            for r in range(tm // rows):
                rs = slice(r * rows, (r + 1) * rows)
                acc = jnp.dot(h_s[rs, :], w_ref[:, p * sub:(p + 1) * sub], preferred_element_type=F32)
                cos = cos_ref[rs, :]
                sin = sin_ref[rs, :]
                for cc in range(sub // RET_QK_DIM):
                    c = p * (sub // RET_QK_DIM) + cc
                    a = acc[:, cc * RET_QK_DIM: cc * RET_QK_DIM + half]
                    b = acc[:, cc * RET_QK_DIM + half: (cc + 1) * RET_QK_DIM]
                    scale = 1.0 if c < RET_HEADS else RET_QK_DIM ** -0.5
                    o_ref[rs, c * RET_QK_DIM: c * RET_QK_DIM + half] = ((a * cos - b * sin) * scale).astype(BF16)
                    o_ref[rs, c * RET_QK_DIM + half: (c + 1) * RET_QK_DIM] = (
                        (a * sin + b * cos) * scale).astype(BF16)

    is_conv = (j >= conv_j0) & (j < conv_j0 + conv_nj)

    @pl.when(is_conv)
    def _():
        cj = j - conv_j0
        pad = SUBLANES

        @pl.when(i % tiles_per_seq == 0)
        def _():
            carry[cj] = jnp.zeros(carry.shape[1:], F32)

        for p in range(tn // sub):
            for r in range(tm // rows):
                r0 = r * rows
                acc = jnp.dot(h_s[r0:r0 + rows, :], w_ref[:, p * sub:(p + 1) * sub], preferred_element_type=F32)
                for cc in range(sub // LANES):
                    c = p * (sub // LANES) + cc
                    cols = slice(c * LANES, (c + 1) * LANES)
                    if r == 0:
                        work[c, 0:pad, :] = carry[cj, c]
                    lo = pad + r0
                    work[c, lo:lo + rows, :] = acc[:, cc * LANES:(cc + 1) * LANES]
                    conv = cb_ref[:, cols] + cw_ref[SSM_CONV - 1:SSM_CONV, cols] * work[c, lo:lo + rows, :]
                    for k in range(SSM_CONV - 1):
                        shift = SSM_CONV - 1 - k
                        conv = conv + cw_ref[k:k + 1, cols] * work[c, lo - shift:lo - shift + rows, :]
                    if r0 + rows == tm:
                        carry[cj, c] = work[c, tm:tm + pad, :]
                    o_ref[r0:r0 + rows, cols] = _silu(conv).astype(BF16)

    def plain(act):
        for p in range(tn // sub):
            for r in range(tm // rows):
                acc = jnp.dot(h_s[r * rows:(r + 1) * rows, :], w_ref[:, p * sub:(p + 1) * sub],
                              preferred_element_type=F32)
                o_ref[r * rows:(r + 1) * rows, p * sub:(p + 1) * sub] = act(acc).astype(BF16)

    is_silu = (j >= silu_j[0]) & (j < silu_j[1])
    is_sigm = (j >= sigm_j[0]) & (j < sigm_j[1])
    pl.when(is_silu)(lambda: plain(_silu))
    pl.when(is_sigm)(lambda: plain(jax.nn.sigmoid))
    pl.when((j != 0) & jnp.logical_not(is_conv | is_silu | is_sigm))(lambda: plain(lambda v: v))


def _inproj_call(x2, row0, t, norm_w, mod3, cos, sin, w_main, w_dt, conv_w, conv_b, conv_off, seq, tm, tn):
    d = x2.shape[1]
    n = w_main.shape[1]
    conv_dim = conv_w.shape[1]
    tiles_per_seq = seq // tm
    off = row0 // tm
    assert tn == 2 * RET_HEADS * RET_QK_DIM, "rotary epilogue expects q and k in the first column tile"
    assert conv_off % tn == 0 and conv_dim % tn == 0
    conv_j0, conv_nj = conv_off // tn, conv_dim // tn
    sub = 512
    g_off = 2 * RET_HEADS * RET_QK_DIM + RET_HEADS * RET_V_DIM
    assert g_off % tn == 0 and (conv_off - g_off) % tn == 0 and (n - conv_off - conv_dim) % tn == 0
    silu_j = (g_off // tn, conv_off // tn)
    sigm_j = ((conv_off + conv_dim) // tn, n // tn)
    kern = functools.partial(_inproj_kernel, conv_j0=conv_j0, conv_nj=conv_nj, silu_j=silu_j, sigm_j=sigm_j,
                             tiles_per_seq=tiles_per_seq, tm=tm, tn=tn, sub=sub)
    conv_idx = lambda i, j: (0, jnp.clip(j - conv_j0, 0, conv_nj - 1))
    return pl.pallas_call(
        kern,
        out_shape=(jax.ShapeDtypeStruct((t, n), BF16), jax.ShapeDtypeStruct((t, LANES), F32)),
        grid=(t // tm, n // tn),
        in_specs=[
            pl.BlockSpec((tm, d), lambda i, j: (i + off, 0)),
            pl.BlockSpec((1, d), lambda i, j: (0, 0)),
            pl.BlockSpec((1, 1, d), lambda i, j: (((i + off) // tiles_per_seq) * N_MOD + 1, 0, 0)),
            pl.BlockSpec((1, 1, d), lambda i, j: (((i + off) // tiles_per_seq) * N_MOD + 0, 0, 0)),
            pl.BlockSpec((tm, LANES), lambda i, j: (i % tiles_per_seq, 0)),
            pl.BlockSpec((tm, LANES), lambda i, j: (i % tiles_per_seq, 0)),
            pl.BlockSpec((d, tn), lambda i, j: (0, j)),
            pl.BlockSpec((d, 2 * LANES), lambda i, j: (0, 0)),
            pl.BlockSpec((SSM_CONV, tn), conv_idx),
            pl.BlockSpec((1, tn), conv_idx),
        ],
        out_specs=(pl.BlockSpec((tm, tn), lambda i, j: (i, j)),
                   pl.BlockSpec((tm, LANES), lambda i, j: (i, 0))),
        scratch_shapes=[pltpu.VMEM((tm, d), BF16),
                        pltpu.VMEM((tn // LANES, tm + SUBLANES, LANES), F32),
                        pltpu.VMEM((conv_nj, tn // LANES, SUBLANES, LANES), F32)],
        compiler_params=_params(("arbitrary", "arbitrary")),
        name="inproj",
    )(x2, norm_w, mod3, mod3, cos, sin, w_main, w_dt, conv_w, conv_b.reshape(1, conv_dim))


def _retention_body(q_ref, k_ref, v_ref, g_ref, din_ref, dq_ref, dk_ref, w_ref, o_ref, state, decay_c):
    heads = range(RET_HEADS)
    q = [q_ref[:, h * RET_QK_DIM:(h + 1) * RET_QK_DIM] for h in heads]
    k = [k_ref[:, h * RET_QK_DIM:(h + 1) * RET_QK_DIM] for h in heads]
    v = [v_ref[:, h * RET_V_DIM:(h + 1) * RET_V_DIM] for h in heads]
    scores = [(_nt_dot(q[h], k[h]) * din_ref[h]).astype(BF16) for h in heads]
    st = [state[h] for h in heads]
    cross = [jnp.dot(q[h], st[h].astype(BF16), preferred_element_type=F32) * dq_ref[h] for h in heads]
    inner = [jnp.dot(scores[h], v[h], preferred_element_type=F32) for h in heads]
    for h in heads:
        kd = (k[h].astype(F32) * dk_ref[h]).astype(BF16)
        state[h] = st[h] * decay_c[h] + _tn_dot(kd, v[h])
    acc = None
    for h in heads:
        ret = inner[h] + cross[h]
        ret = ret * lax.rsqrt(jnp.mean(ret * ret, axis=-1, keepdims=True) + EPS)
        ret = ret * g_ref[:, h * RET_V_DIM:(h + 1) * RET_V_DIM].astype(F32)
        part = jnp.dot(ret.astype(BF16), w_ref[h * RET_V_DIM:(h + 1) * RET_V_DIM, :],
                       preferred_element_type=F32)
        acc = part if acc is None else acc + part
    o_ref[...] = acc.astype(o_ref.dtype)


def _retention_tables(chunk):
    lg = np.log(1.0 - 2.0 ** (-5.0 - np.arange(RET_HEADS, dtype=np.float64)))
    idx = np.arange(chunk, dtype=np.float64)
    rel = idx[:, None] - idx[None, :]
    causal = rel >= 0
    din = np.where(causal[None], np.exp(np.where(causal, rel, 0.0)[None] * lg[:, None, None]), 0.0)
    dq = np.exp((idx + 1.0)[None, :, None] * lg[:, None, None])
    dk = np.exp((chunk - 1.0 - idx)[None, :, None] * lg[:, None, None])
    dc = tuple(float(v) for v in np.exp(chunk * lg))
    return (jnp.asarray(din, F32), jnp.asarray(dq, F32), jnp.asarray(dk, F32), dc)


def _ssd_body(z_ref, xbc_ref, dt_ref, dtb_ref, alog_ref, dsk_ref, nw_ref,
              tril_ref, exp_ref, yn_s, state, chunk, d_inner):
    heads_per_group = d_inner // SSM_HEAD_DIM // SSM_GROUPS
    gw = heads_per_group * SSM_HEAD_DIM
    assert SSM_HEAD_DIM * 2 == LANES and gw == 2 * LANES

    dt =jax.nn.softplus(dt_ref[...] + dtb_ref[...])
    a = -jnp.exp(alog_ref[...])
    adt = dt * a
    acs = jnp.dot(tril_ref[...], adt, preferred_element_type=F32, precision=HIGHEST) * LOG2_E
    acs_t = acs.T
    dt_x = jnp.dot(dt.astype(BF16), exp_ref[...], preferred_element_type=F32)
    li = lax.broadcasted_iota(jnp.int32, (chunk, chunk), 0)
    si = lax.broadcasted_iota(jnp.int32, (chunk, chunk), 1)
    causal = li >= si
    low_half = si < SSM_HEAD_DIM
    lane_g = lax.broadcasted_iota(jnp.int32, (chunk, gw), 1)

    b_off = d_inner
    c_off = d_inner + SSM_GROUPS * SSM_STATE
    for g in range(SSM_GROUPS):
        bm = xbc_ref[:, b_off + g * SSM_STATE: b_off + (g + 1) * SSM_STATE]
        cm = xbc_ref[:, c_off + g * SSM_STATE: c_off + (g + 1) * SSM_STATE]
        xs_g = xbc_ref[:, g * gw:(g + 1) * gw].astype(F32)
        xdt_g = xs_g * dt_x[:, g * gw:(g + 1) * gw]
        xdt_b = xdt_g.astype(BF16)
        cb = _nt_dot(cm, bm)
        cols, ms, xm = [], [], []
        for jh in range(heads_per_group):
            h = g * heads_per_group + jh
            col = jnp.broadcast_to(acs[:, h:h + 1], (chunk, chunk))
            seg = jnp.exp2(jnp.where(causal, col - acs_t[h:h + 1, :], -jnp.inf))
            cols.append(col)
            ms.append((cb * seg).astype(BF16))
            in_head = (lane_g >= jh * SSM_HEAD_DIM) & (lane_g < (jh + 1) * SSM_HEAD_DIM)
            xm.append(jnp.where(in_head, xdt_b, jnp.zeros_like(xdt_b)))
        y_diag = jnp.dot(jnp.concatenate(ms, axis=-1), jnp.concatenate(xm, axis=0),
                         preferred_element_type=F32)
        a_x = jnp.concatenate([jnp.where(low_half, cols[0], cols[1]),
                               jnp.where(low_half, cols[2], cols[3])], axis=-1)
        e_acs_x = jnp.exp2(a_x)
        a_last_x = a_x[chunk - 1:chunk, :]
        st = state[g]
        y_off = jnp.dot(cm, st.astype(BF16), preferred_element_type=F32) * e_acs_x
        xdec = (xdt_g * jnp.exp2(a_last_x - a_x)).astype(BF16)
        state[g] = st * e_acs_x[chunk - 1:chunk, :] + _tn_dot(bm, xdec)
        y = y_diag + y_off + dsk_ref[:, g * gw:(g + 1) * gw] * xs_g
        yz = y * z_ref[:, g * gw:(g + 1) * gw].astype(F32)
        yn = yz * lax.rsqrt(jnp.mean(yz * yz, axis=-1, keepdims=True) + EPS) * nw_ref[:, g * gw:(g + 1) * gw]
        yn_s[:, g * gw:(g + 1) * gw] = yn.astype(BF16)


def _mixers_kernel(q_ref, k_ref, v_ref, g_ref, din_ref, dq_ref, dk_ref, wret_ref,
                   z_ref, xbc_ref, dt_ref, dtb_ref, alog_ref, dsk_ref, nw_ref, tril_ref, exp_ref, wssm_ref,
                   ga_ref, gb_ref, x_ref, gm_ref, scf_ref, shf_ref, nf_ref, wo_ref, wr_ref, br_ref, tri_ref,
                   x1_ref, h2_ref, idx_ref, rank_ref, prow_ref, cnt_ref,
                   rstate, sstate, yn_s, ya_s, yb_s, cnt_s, *, decay_c, chunk, ssm_chunk, d_inner):
    b = pl.program_id(0)
    c = pl.program_id(1)

    @pl.when(c == 0)
    def _():
        rstate[...] = jnp.zeros_like(rstate)
        sstate[...] = jnp.zeros_like(sstate)

    @pl.when((b == 0) & (c == 0))
    def _():
        cnt_s[...] = jnp.zeros_like(cnt_s)

    _retention_body(q_ref, k_ref, v_ref, g_ref, din_ref, dq_ref, dk_ref, wret_ref, ya_s, rstate, decay_c)
    for sub in range(chunk // ssm_chunk):
        rows = pl.ds(sub * ssm_chunk, ssm_chunk)
        _ssd_body(z_ref.at[rows, :], xbc_ref.at[rows, :], dt_ref.at[rows, :], dtb_ref, alog_ref, dsk_ref,
                  nw_ref, tril_ref, exp_ref, yn_s.at[rows, :], sstate, ssm_chunk, d_inner)
    yb_s[...] = jnp.dot(yn_s[...], wssm_ref[...], preferred_element_type=F32)
    _merge_body(ya_s, yb_s, ga_ref, gb_ref, x_ref, gm_ref, scf_ref, shf_ref, nf_ref, wo_ref, wr_ref, br_ref,
                tri_ref, x1_ref, h2_ref, idx_ref, rank_ref, prow_ref, cnt_ref, cnt_s, chunk)


def _mixers_call(proj, dt_raw, x2, row0, mod3, w_ret, dt_bias, a_log, d_skip, ssm_norm, w_ssm,
                 norm_ffn, w_out, w_router_t, b_router, bsz, seq, chunk, ssm_chunk):
    t = proj.shape[0]
    d_inner, d = w_ssm.shape
    conv_dim = d_inner + 2 * SSM_GROUPS * SSM_STATE
    n_heads = d_inner // SSM_HEAD_DIM
    nc = seq // chunk
    gw = d_inner // SSM_GROUPS
    qk_w = RET_HEADS * RET_QK_DIM
    v_w = RET_HEADS * RET_V_DIM
    assert chunk % ssm_chunk == 0
    assert ssm_chunk == LANES, "the per-head decay tiles are built lane-for-lane against the chunk"
    din, dq, dk, dc = _retention_tables(chunk)
    pad_h = lambda v: jnp.pad(v.astype(F32), (0, LANES - n_heads)).reshape(1, LANES)
    tril = jnp.asarray(np.tril(np.ones((ssm_chunk, ssm_chunk), np.float32)))
    expand = np.zeros((LANES, d_inner), np.float32)
    for h in range(n_heads):
        expand[h, h * SSM_HEAD_DIM:(h + 1) * SSM_HEAD_DIM] = 1.0
    expand = jnp.asarray(expand, BF16)
    kern = functools.partial(_mixers_kernel, decay_c=dc, chunk=chunk, ssm_chunk=ssm_chunk, d_inner=d_inner)
    row = lambda b, c: b * nc + c
    z_blk = (2 * qk_w + 2 * v_w) // d_inner
    xbc_blk = (2 * qk_w + 2 * v_w + d_inner) // conv_dim
    full = lambda shape: pl.BlockSpec(shape, lambda b, c: (0,) * len(shape))
    nt = t // chunk
    ga_blk = proj.shape[1] // d - 2
    seq0 = row0 // seq
    modspec = lambda m: pl.BlockSpec((1, 1, d), lambda b, c: ((b + seq0) * N_MOD + m, 0, 0))
    tri = jnp.asarray(np.triu(np.ones((chunk, chunk), np.float32), 1), BF16)
    return pl.pallas_call(
        kern,
        out_shape=(jax.ShapeDtypeStruct((t, d), F32), jax.ShapeDtypeStruct((t * SLAB_ROWS, LANES), jnp.uint32),
                   jax.ShapeDtypeStruct((nt, SUBLANES, chunk), jnp.int32),
                   jax.ShapeDtypeStruct((nt, SUBLANES, chunk), jnp.int32),
                   jax.ShapeDtypeStruct((t, LANES), F32),
                   jax.ShapeDtypeStruct((N_EXPERTS, LANES), jnp.int32)),
        grid=(bsz, nc),
        in_specs=[
            pl.BlockSpec((chunk, qk_w), lambda b, c: (row(b, c), 0)),
            pl.BlockSpec((chunk, qk_w), lambda b, c: (row(b, c), 1)),
            pl.BlockSpec((chunk, v_w), lambda b, c: (row(b, c), 1)),
            pl.BlockSpec((chunk, v_w), lambda b, c: (row(b, c), 2)),
            full((RET_HEADS, chunk, chunk)), full((RET_HEADS, chunk, 1)), full((RET_HEADS, chunk, 1)),
            full((v_w, d)),
            pl.BlockSpec((chunk, d_inner), lambda b, c: (row(b, c), z_blk)),
            pl.BlockSpec((chunk, conv_dim), lambda b, c: (row(b, c), xbc_blk)),
            pl.BlockSpec((chunk, LANES), lambda b, c: (row(b, c), 0)),
            full((1, LANES)), full((1, LANES)),
            full((1, d_inner)), full((1, d_inner)), full((ssm_chunk, ssm_chunk)), full((LANES, d_inner)),
            full((d_inner, d)),
            pl.BlockSpec((chunk, d), lambda b, c: (row(b, c), ga_blk)),
            pl.BlockSpec((chunk, d), lambda b, c: (row(b, c), ga_blk + 1)),
            pl.BlockSpec((chunk, d), lambda b, c: (row0 // chunk + row(b, c), 0)),
            modspec(2), modspec(4), modspec(3),
            full((1, d)), full((d, d)), full((2 * N_EXPERTS, d)), full((N_EXPERTS, 1)), full((chunk, chunk)),
        ],
        out_specs=(pl.BlockSpec((chunk, d), lambda b, c: (row(b, c), 0)),
                   pl.BlockSpec((chunk * SLAB_ROWS, LANES), lambda b, c: (row(b, c), 0)),
                   pl.BlockSpec((1, SUBLANES, chunk), lambda b, c: (row(b, c), 0, 0)),
                   pl.BlockSpec((1, SUBLANES, chunk), lambda b, c: (row(b, c), 0, 0)),
                   pl.BlockSpec((chunk, LANES), lambda b, c: (row(b, c), 0)),
                   full((N_EXPERTS, LANES))),
        scratch_shapes=[pltpu.VMEM((RET_HEADS, RET_QK_DIM, RET_V_DIM), F32),
                        pltpu.VMEM((SSM_GROUPS, SSM_STATE, gw), F32),
                        pltpu.VMEM((chunk, d_inner), BF16),
                        pltpu.VMEM((chunk, d), F32), pltpu.VMEM((chunk, d), F32),
                        pltpu.VMEM((N_EXPERTS, LANES), F32)],
        compiler_params=_params(("arbitrary", "arbitrary")),
        name="mixers",
    )(proj, proj, proj, proj, din, dq, dk, w_ret,
      proj, proj, dt_raw, pad_h(dt_bias), pad_h(a_log),
      jnp.repeat(d_skip.astype(F32), SSM_HEAD_DIM).reshape(1, d_inner), ssm_norm.reshape(1, d_inner),
      tril, expand, w_ssm,
      proj, proj, x2, mod3, mod3, mod3, norm_ffn, w_out, w_router_t, b_router.reshape(N_EXPERTS, 1), tri)


def _merge_body(ya_ref, yb_ref, ga_ref, gb_ref, x_ref, gm_ref, scf_ref, shf_ref, nw_ref, wo_ref,
                wr_ref, br_ref, tri_ref,
                x1_ref, h2_ref, idx_ref, rank_ref, prow_ref, cnt_ref, cnt_s, tm):
    merged = (ga_ref[...].astype(F32) * ya_ref[...].astype(F32)
              + gb_ref[...].astype(F32) * yb_ref[...].astype(F32))
    mo = jnp.dot(merged.astype(BF16), wo_ref[...], preferred_element_type=F32)
    x1 = x_ref[...] + gm_ref[0] * mo
    x1_ref[...] = x1
    ms = jnp.mean(x1 * x1, axis=-1, keepdims=True)
    h2 = x1 * lax.rsqrt(ms + EPS) * nw_ref[...] * (1.0 + scf_ref[0]) + shf_ref[0]
    _store_slabs(h2_ref, h2, tm)

    h_hi = h2.astype(BF16)
    h_lo = (h2 - h_hi.astype(F32)).astype(BF16)
    lg2 = _nt_dot(wr_ref[...], h_hi)
    lg = lg2[:N_EXPERTS] + lg2[N_EXPERTS:] + _nt_dot(wr_ref[:N_EXPERTS, :], h_lo) + br_ref[...]
    sub = lax.broadcasted_iota(jnp.int32, lg.shape, 0)
    work = lg
    vals, idxs, sels = [], [], []
    for _ in range(TOP_K):
        m = jnp.max(work, axis=0, keepdims=True)
        ik = jnp.min(jnp.where(work == m, sub, N_EXPERTS), axis=0, keepdims=True)
        sel = sub == ik
        vals.append(m)
        idxs.append(ik)
        sels.append(sel)
        work = jnp.where(sel, -jnp.inf, work)
    exps = [jnp.exp(v - vals[0]) for v in vals]
    denom = exps[0]
    for e in exps[1:]:
        denom = denom + e
    probs = [e / denom for e in exps]

    base = cnt_s[:, 0:1]
    ranks = []
    for k in range(TOP_K):
        mk = jnp.where(sels[k], 1.0, 0.0)
        pre = jnp.dot(mk.astype(BF16), tri_ref[...], preferred_element_type=F32)
        ranks.append(jnp.sum(jnp.where(sels[k], pre + base, 0.0), axis=0, keepdims=True))
        base = base + jnp.sum(mk, axis=1, keepdims=True)
    cnt_s[...] = jnp.broadcast_to(base, cnt_s.shape)
    cnt_ref[...] = cnt_s[...].astype(jnp.int32)

    zi = jnp.zeros((SUBLANES - TOP_K, tm), jnp.int32)
    idx_ref[0] = jnp.concatenate(idxs + [zi], axis=0)
    rank_ref[0] = jnp.concatenate([r.astype(jnp.int32) for r in ranks] + [zi], axis=0)
    pt = jnp.concatenate(probs + [jnp.zeros((LANES - TOP_K, tm), F32)], axis=0)
    prow_ref[...] = pt.T


def _sc_mesh():
    return plsc.VectorSubcoreMesh(core_axis_name="c", subcore_axis_name="s")


def _sc_worker():
    return lax.axis_index("s") * SC_CORES + lax.axis_index("c")


def _sc_scatter_rows(rows, dest, n_out):
    t = rows.shape[0]
    n_k = dest.shape[0]
    g = SC_GROUP
    n_w = SC_CORES * SC_SUBCORES
    assert t % (n_w * g) == 0
    cpw = t // (n_w * g)
    dest_w = dest.reshape(n_k, n_w, cpw, g).transpose(1, 0, 2, 3)

    @functools.partial(
        pl.kernel, mesh=_sc_mesh(),
        out_type=jax.ShapeDtypeStruct((n_out,) + rows.shape[1:], rows.dtype),
        scratch_types=[pltpu.VMEM((n_k, cpw, g), jnp.int32),
                       pltpu.VMEM((g,) + rows.shape[1:], rows.dtype),
                       pltpu.SemaphoreType.DMA],
    )
    def scatter(rows_hbm, dest_hbm, out_hbm, idx_v, rows_v, sem):
        wid = _sc_worker()
        pltpu.sync_copy(dest_hbm.at[wid], idx_v)

        @pl.loop(0, cpw)
        def _(cc):
            r0 = pl.multiple_of((wid * cpw + cc) * g, g)
            pltpu.sync_copy(rows_hbm.at[pl.ds(r0, g)], rows_v)
            copies = [pltpu.async_copy(rows_v, out_hbm.at[idx_v.at[k, cc]], sem) for k in range(n_k)]
            for cp in copies:
                cp.wait()

    return scatter(rows, dest_w)


def _sc_gather_rows(table, idx):
    m = idx.shape[0]
    g = SC_GROUP
    n_w = SC_CORES * SC_SUBCORES
    assert m % (n_w * g) == 0
    per_w = m // n_w

    @functools.partial(
        pl.kernel, mesh=_sc_mesh(),
        out_type=jax.ShapeDtypeStruct((m,) + table.shape[1:], table.dtype),
        scratch_types=[pltpu.VMEM((per_w,), jnp.int32),
                       pltpu.VMEM((g,) + table.shape[1:], table.dtype),
                       pltpu.SemaphoreType.DMA],
    )
    def gather(table_hbm, idx_hbm, out_hbm, idx_v, rows_v, sem):
        base = _sc_worker() * per_w
        pltpu.sync_copy(idx_hbm.at[pl.ds(base, per_w)], idx_v)

        @pl.loop(0, per_w // g)
        def _(cc):
            off = pl.multiple_of(cc * g, g)
            pltpu.async_copy(table_hbm.at[idx_v.at[pl.ds(off, g)]], rows_v, sem).wait()
            pltpu.sync_copy(rows_v, out_hbm.at[pl.ds(base + off, g)])

    return gather(table, idx)


def _ffn_kernel(be_ref, br_ref, bv_ref, first_ref, slot_ref, next_ref,
                x_ref, wgu_hbm, bgu_ref, wd_hbm, bd_ref, o_ref, wgu_f, wd_f, wgu_s, wd_s, sem, *, bm, d_ff):
    i = pl.program_id(0)

    def fetch(e, slot):
        return (pltpu.make_async_copy(wgu_hbm.at[e], wgu_f.at[slot], sem.at[0, slot]),
                pltpu.make_async_copy(wd_hbm.at[e], wd_f.at[slot], sem.at[1, slot]))

    @pl.when(i == 0)
    def _():
        for cp in fetch(be_ref[0], slot_ref[0]):
            cp.start()

    @pl.when(first_ref[i] == 1)
    def _():
        slot = slot_ref[i]
        for cp in fetch(be_ref[i], slot):
            cp.wait()
        wgu_s[...] = wgu_f[slot].astype(BF16)
        wd_s[...] = wd_f[slot].astype(BF16)

        @pl.when(next_ref[i] >= 0)
        def _():
            for cp in fetch(next_ref[i], 1 - slot):
                cp.start()

    @pl.when(bv_ref[i] == 1)
    def _():
        x = jnp.concatenate(_load_slabs(x_ref, bm), axis=-1).astype(BF16)
        gu = jnp.dot(x, wgu_s[...], preferred_element_type=F32) + bgu_ref[0]
        gate = jnp.minimum(gu[:, :d_ff], SWIGLU_LIMIT)
        up = jnp.clip(gu[:, d_ff:], -SWIGLU_LIMIT, SWIGLU_LIMIT)
        act = gate * jax.nn.sigmoid(SWIGLU_ALPHA * gate) * (up + 1.0)
        y = jnp.dot(act.astype(BF16), wd_s[...], preferred_element_type=F32) + bd_ref[0]
        _store_slabs(o_ref, y, bm)

    @pl.when(bv_ref[i] == 0)
    def _():
        o_ref[...] = jnp.zeros_like(o_ref)


def _ffn_call(blk_e, blk_row, blk_valid, xs, w_gu, b_gu, w_d, b_d, bm):
    n_e, d, f2 = w_gu.shape
    d_ff = f2 // 2
    nb = blk_e.shape[0]
    first = jnp.concatenate([jnp.ones((1,), jnp.int32), (blk_e[1:] != blk_e[:-1]).astype(jnp.int32)])
    slot = (jnp.cumsum(first) - 1) % 2
    later = jnp.where(blk_e[None, :] > blk_e[:, None], blk_e[None, :], n_e)
    nxt = jnp.min(later, axis=1)
    nxt = jnp.where(nxt == n_e, -1, nxt)
    kern = functools.partial(_ffn_kernel, bm=bm, d_ff=d_ff)
    imap = lambda f: (lambda i, be, br, bv, fi, sl, nx: f(i, be, br))
    gs = pltpu.PrefetchScalarGridSpec(
        num_scalar_prefetch=6,
        grid=(nb,),
        in_specs=[pl.BlockSpec((bm * SLAB_ROWS, LANES), imap(lambda i, be, br: (br[i], 0))),
                  pl.BlockSpec(memory_space=pl.ANY),
                  pl.BlockSpec((1, 1, f2), imap(lambda i, be, br: (be[i], 0, 0))),
                  pl.BlockSpec(memory_space=pl.ANY),
                  pl.BlockSpec((1, 1, d), imap(lambda i, be, br: (be[i], 0, 0)))],
        out_specs=pl.BlockSpec((bm * SLAB_ROWS, LANES), imap(lambda i, be, br: (i, 0))),
        scratch_shapes=[pltpu.VMEM((2, d, f2), F32), pltpu.VMEM((2, d_ff, d), F32),
                        pltpu.VMEM((d, f2), BF16), pltpu.VMEM((d_ff, d), BF16),
                        pltpu.SemaphoreType.DMA((2, 2))],
    )
    return pl.pallas_call(
        kern,
        out_shape=jax.ShapeDtypeStruct((nb * bm * SLAB_ROWS, LANES), jnp.uint32),
        grid_spec=gs,
        compiler_params=_params(("arbitrary",)),
        name="ffn",
    )(blk_e, blk_row, blk_valid, first, slot.astype(jnp.int32), nxt.astype(jnp.int32),
      xs, w_gu, b_gu.reshape(n_e, 1, f2), w_d, b_d.reshape(n_e, 1, d))


def _combine_kernel(y0_ref, y1_ref, y2_ref, y3_ref, prow_ref, x1_ref, gf_ref, nw_ref, *rest, tm):
    o_ref = rest[-1]
    p = prow_ref[...]
    pieces = [_load_slabs(y_ref, tm) for y_ref in (y0_ref, y1_ref, y2_ref, y3_ref)]
    for s in range(len(pieces[0])):
        moe = None
        for k in range(TOP_K):
            piece = pieces[k][s] * p[:, k:k + 1]
            moe = piece if moe is None else moe + piece
        sl = slice(s * LANES, (s + 1) * LANES)
        o_ref[:, sl] = x1_ref[:, sl] + gf_ref[0][:, sl] * moe
    xo = o_ref[...]
    o_ref[...] = xo * lax.rsqrt(jnp.mean(xo * xo, axis=-1, keepdims=True) + EPS) * nw_ref[...]


def _combine_call(ytok, prow, x1, mod3, norm_final, seq, tm, row0, t_total, out_prev):
    t, d = x1.shape
    nt = t // tm
    tiles_per_seq = seq // tm
    off = row0 // tm
    kern = functools.partial(_combine_kernel, tm=tm)
    yspec = lambda k: pl.BlockSpec((tm * SLAB_ROWS, LANES), lambda i: (k * nt + i, 0))
    in_specs = [yspec(0), yspec(1), yspec(2), yspec(3),
                pl.BlockSpec((tm, LANES), lambda i: (i, 0)),
                pl.BlockSpec((tm, d), lambda i: (i, 0)),
                pl.BlockSpec((1, 1, d), lambda i: (((i + off) // tiles_per_seq) * N_MOD + 5, 0, 0)),
                pl.BlockSpec((1, d), lambda i: (0, 0))]
    args = [ytok, ytok, ytok, ytok, prow, x1, mod3, norm_final]
    aliases = {}
    if out_prev is not None:
        in_specs.append(pl.BlockSpec(memory_space=pl.ANY))
        aliases = {len(args): 0}
        args.append(out_prev)
    return pl.pallas_call(
        kern,
        out_shape=jax.ShapeDtypeStruct((t_total, d), F32),
        grid=(nt,),
        in_specs=in_specs,
        out_specs=pl.BlockSpec((tm, d), lambda i: (i + off, 0)),
        input_output_aliases=aliases,
        compiler_params=_params(("arbitrary",)),
        name="combine",
    )(*args)


def _plan(seq):
    def fit(pref):
        tm = min(pref, seq)
        assert seq % tm == 0
        return tm
    return dict(tm_in=fit(1024), tm_merge=fit(512), tm_moe=fit(512),
                ret_chunk=fit(RET_CHUNK), ssm_chunk=fit(SSM_CHUNK))


def _layer(x2, mod3, bsz, seq, norm_mix, norm_ffn, w_in, conv_w, conv_b, dt_bias, a_log, d_skip, ssm_norm,
           w_ret_out, w_ssm_out, w_out, w_router, b_router, w_gate_up, b_gate_up, w_down, b_down,
           norm_final):
    t, d = x2.shape
    plan = _plan(seq)
    qk_w = RET_HEADS * RET_QK_DIM
    v_w = RET_HEADS * RET_V_DIM
    d_inner = w_ssm_out.shape[0]
    conv_dim = conv_w.shape[1]
    n_heads = d_inner // SSM_HEAD_DIM
    dt_off = 2 * qk_w + 2 * v_w + d_inner + conv_dim

    w_main = jnp.concatenate([w_in[:, :dt_off], w_in[:, dt_off + n_heads:]], axis=1).astype(BF16)
    w_dt = jnp.pad(w_in[:, dt_off:dt_off + n_heads], ((0, 0), (0, LANES - n_heads)))
    w_dt_hi = w_dt.astype(BF16)
    w_dt = jnp.concatenate([w_dt_hi, (w_dt - w_dt_hi.astype(F32)).astype(BF16)], axis=1)
    half = RET_QK_DIM // 2
    inv_freq = ROPE_BASE ** (-jnp.arange(half, dtype=F32) / half)
    ang = jnp.arange(seq, dtype=F32)[:, None] * inv_freq[None, :]
    cos, sin = jnp.cos(ang), jnp.sin(ang)

    w_ret_b, w_ssm_b, w_out_b = w_ret_out.astype(BF16), w_ssm_out.astype(BF16), w_out.astype(BF16)
    w_r_hi = w_router.T.astype(BF16)
    w_router_t = jnp.concatenate([w_r_hi, (w_router.T - w_r_hi.astype(F32)).astype(BF16)], axis=0)
    bm = FFN_BLOCK
    slab = (SLAB_ROWS, LANES)

    def mixer(row0, tg, bg):
        proj, dt_raw = _inproj_call(x2, row0, tg, norm_mix.reshape(1, d), mod3, cos, sin, w_main, w_dt,
                                    conv_w, conv_b, 2 * qk_w + 2 * v_w + d_inner, seq, plan["tm_in"], 2 * qk_w)
        x1, h2, idx, rank, prow, cnt = _mixers_call(
            proj, dt_raw, x2, row0, mod3, w_ret_b, dt_bias, a_log, d_skip, ssm_norm, w_ssm_b,
            norm_ffn.reshape(1, d), w_out_b, w_router_t, b_router, bg, seq, plan["ret_chunk"], plan["ssm_chunk"])
        counts = cnt[:, 0]
        padded = ((counts + bm - 1) // bm) * bm
        pad_end = jnp.cumsum(padded)
        start_pad = pad_end - padded
        n_blocks = (tg * TOP_K) // bm + N_EXPERTS
        e_ids = jnp.arange(N_EXPERTS, dtype=jnp.int32)[:, None, None, None]
        dest = rank + jnp.sum(jnp.where(idx[None] == e_ids, start_pad[:, None, None, None], 0), axis=0)
        dest = dest[:, :TOP_K, :].transpose(1, 0, 2).reshape(TOP_K, tg).astype(jnp.int32)
        n_real = pad_end[-1] // bm
        blk_valid = (jnp.arange(n_blocks) < n_real).astype(jnp.int32)
        blk_row = jnp.minimum(jnp.arange(n_blocks), n_real - 1).astype(jnp.int32)
        blk_e = jnp.minimum(jnp.sum(pad_end[None, :] <= (blk_row * bm)[:, None], axis=1),
                            N_EXPERTS - 1).astype(jnp.int32)
        xs = _sc_scatter_rows(h2.reshape((tg,) + slab), dest, n_blocks * bm)
        return dict(x1=x1, prow=prow, dest=dest, blocks=(blk_e, blk_row, blk_valid), xs=xs, row0=row0)

    def experts(m):
        n_rows = m["xs"].shape[0]
        ys = _ffn_call(*m["blocks"], m["xs"].reshape(n_rows * SLAB_ROWS, LANES),
                       w_gate_up, b_gate_up, w_down, b_down, bm)
        return _sc_gather_rows(ys.reshape((n_rows,) + slab), m["dest"].reshape(-1))

    n_groups = N_GROUPS if bsz % N_GROUPS == 0 else 1
    bg = bsz // n_groups
    tg = bg * seq
    groups = [mixer(g * tg, tg, bg) for g in range(n_groups)]
    ytoks = [experts(m) for m in groups]
    out = None
    for y, m in zip(ytoks, groups):
        out = _combine_call(y.reshape(TOP_K * tg * SLAB_ROWS, LANES), m["prow"], m["x1"], mod3,
                            norm_final.reshape(1, d), seq, plan["tm_moe"], m["row0"], t, out)
    return out


def kernel(x, c, w_ada, b_ada, norm_mix, norm_ffn, w_in, conv_w, conv_b, dt_bias, a_log, d_skip, ssm_norm,
           w_ret_out, w_ssm_out, w_out, w_router, b_router, w_gate_up, b_gate_up, w_down, b_down, norm_final):
    bsz, seq, d = x.shape
    depth = w_ada.shape[0]
    assert depth == 1, "the final norm is fused into the single layer's last kernel"
    x2 = x.reshape(bsz * seq, d)
    l = 0
    mod = _mod_call(c, w_ada[l], b_ada[l])
    mod3 = mod.reshape(bsz * N_MOD, 1, d)
    out = _layer(x2, mod3, bsz, seq, norm_mix[l], norm_ffn[l], w_in[l], conv_w[l], conv_b[l], dt_bias[l],
                 a_log[l], d_skip[l], ssm_norm[l], w_ret_out[l], w_ssm_out[l], w_out[l], w_router[l],
                 b_router[l], w_gate_up[l], b_gate_up[l], w_down[l], b_down[l], norm_final)
    return out.reshape(bsz, seq, d)
```

```python
import functools
import math

import numpy as np
import jax
import jax.numpy as jnp
from jax import lax
from jax.experimental import pallas as pl
from jax.experimental.pallas import tpu as pltpu
from jax.experimental.pallas import tpu_sc as plsc

F32 = jnp.float32
BF16 = jnp.bfloat16
HIGHEST = lax.Precision.HIGHEST

EPS = 1e-6
N_MOD = 6
RET_HEADS = 4
RET_QK_DIM = 256
RET_V_DIM = 512
ROPE_BASE = 10000.0
SSM_HEAD_DIM = 64
SSM_GROUPS = 8
SSM_STATE = 128
SSM_CONV = 4
N_EXPERTS = 32
TOP_K = 4
SWIGLU_LIMIT = 7.0
SWIGLU_ALPHA = 1.702

LANES = 128
SUBLANES = 8
VMEM_LIMIT = 56 * 1024 * 1024

RET_CHUNK = 256
SSM_CHUNK = 128
FFN_BLOCK = 512
SC_CORES = 2
SC_SUBCORES = 16
SC_GROUP = 64
N_GROUPS = 2


def _params(sem, vmem=VMEM_LIMIT):
    return pltpu.CompilerParams(dimension_semantics=sem, vmem_limit_bytes=vmem)


def _nt_dot(a, b, **kw):
    return lax.dot_general(a, b, (((1,), (1,)), ((), ())), preferred_element_type=F32, **kw)


def _tn_dot(a, b, **kw):
    return lax.dot_general(a, b, (((0,), (0,)), ((), ())), preferred_element_type=F32, **kw)


def _silu(v):
    return v * jax.nn.sigmoid(v)


SLAB_ROWS = 4
HIGH_HALF = 0xFFFF0000
LOG2_E = math.log2(math.e)


def _store_slabs(ref, vals, n):
    for s in range(SLAB_ROWS):
        lo = vals[:, s * LANES:(s + 1) * LANES].astype(BF16).astype(F32)
        hi = vals[:, (s + SLAB_ROWS) * LANES:(s + SLAB_ROWS + 1) * LANES].astype(BF16).astype(F32)
        word = (pltpu.bitcast(lo, jnp.uint32) >> 16) | (pltpu.bitcast(hi, jnp.uint32) & jnp.uint32(HIGH_HALF))
        ref[pl.ds(s, n, stride=SLAB_ROWS), :] = word


def _load_slabs(ref, n, base=0):
    lo, hi = [], []
    for s in range(SLAB_ROWS):
        word = ref[pl.ds(base + s, n, stride=SLAB_ROWS), :]
        lo.append(pltpu.bitcast(word << 16, F32))
        hi.append(pltpu.bitcast(word & jnp.uint32(HIGH_HALF), F32))
    return lo + hi


def _mod_kernel(c_ref, w_ref, b_ref, o_ref):
    cond = _silu(c_ref[...])
    o_ref[...] = jnp.dot(cond, w_ref[...], preferred_element_type=F32, precision=HIGHEST) + b_ref[...]


def _mod_call(c, w_ada, b_ada):
    bsz, d = c.shape
    n = w_ada.shape[1]
    return pl.pallas_call(
        _mod_kernel,
        out_shape=jax.ShapeDtypeStruct((bsz, n), F32),
        grid=(n // d,),
        in_specs=[pl.BlockSpec((bsz, d), lambda j: (0, 0)),
                  pl.BlockSpec((d, d), lambda j: (0, j)),
                  pl.BlockSpec((1, d), lambda j: (0, j))],
        out_specs=pl.BlockSpec((bsz, d), lambda j: (0, j)),
        compiler_params=_params(("arbitrary",)),
        name="mod",
    )(c, w_ada, b_ada.reshape(1, n))


def _inproj_kernel(x_ref, nw_ref, sc_ref, sh_ref, cos_ref, sin_ref, w_ref, wg_ref, wdt_ref, cw_ref, cb_ref,
                   o_ref, dt_ref, h_s, work, carry, *, conv_j0, conv_nj, silu_j, sigm_j, tiles_per_seq,
                   tm, tn, sub):
    i = pl.program_id(0)
    j = pl.program_id(1)
    n_dt = dt_ref.shape[1]
    rows = min(tm, 256)

    @pl.when(j == 0)
    def _():
        xf = x_ref[...]
        ms = jnp.mean(xf * xf, axis=-1, keepdims=True)
        y = xf * lax.rsqrt(ms + EPS) * nw_ref[...]
        hm = y * (1.0 + sc_ref[0]) + sh_ref[0]
        hb = hm.astype(BF16)
        h_s[...] = hb
        h_lo = (hm - hb.astype(F32)).astype(BF16)
        d_hi = jnp.dot(hb, wdt_ref[...], preferred_element_type=F32)
        d_lo = jnp.dot(h_lo, wdt_ref[:, :n_dt], preferred_element_type=F32)
        dt_ref[...] = d_hi[:, :n_dt] + d_hi[:, n_dt:] + d_lo
        half = RET_QK_DIM // 2
        for p in range(tn // sub):
            for r in range(tm // rows):
                rs = slice(r * rows, (r + 1) * rows)
                acc = jnp.dot(h_s[rs, :], w_ref[:, p * sub:(p + 1) * sub], preferred_element_type=F32)
                cos = cos_ref[rs, :]
                sin = sin_ref[rs, :]
                for cc in range(sub // RET_QK_DIM):
                    c = p * (sub // RET_QK_DIM) + cc
                    a = acc[:, cc * RET_QK_DIM: cc * RET_QK_DIM + half]
                    b = acc[:, cc * RET_QK_DIM + half: (cc + 1) * RET_QK_DIM]
                    scale = 1.0 if c < RET_HEADS else RET_QK_DIM ** -0.5
                    o_ref[rs, c * RET_QK_DIM: c * RET_QK_DIM + half] = ((a * cos - b * sin) * scale).astype(BF16)
                    o_ref[rs, c * RET_QK_DIM + half: (c + 1) * RET_QK_DIM] = (
                        (a * sin + b * cos) * scale).astype(BF16)

    is_conv = (j >= conv_j0) & (j < conv_j0 + conv_nj)

    @pl.when(is_conv)
    def _():
        cj = j - conv_j0
        pad = SUBLANES

        @pl.when(i % tiles_per_seq == 0)
        def _():
            carry[cj] = jnp.zeros(carry.shape[1:], F32)

        for p in range(tn // sub):
            for r in range(tm // rows):
                r0 = r * rows
                acc = jnp.dot(h_s[r0:r0 + rows, :], w_ref[:, p * sub:(p + 1) * sub], preferred_element_type=F32)
                for cc in range(sub // LANES):
                    c = p * (sub // LANES) + cc
                    cols = slice(c * LANES, (c + 1) * LANES)
                    if r == 0:
                        work[c, 0:pad, :] = carry[cj, c]
                    lo = pad + r0
                    work[c, lo:lo + rows, :] = acc[:, cc * LANES:(cc + 1) * LANES]
                    conv = cb_ref[:, cols] + cw_ref[SSM_CONV - 1:SSM_CONV, cols] * work[c, lo:lo + rows, :]
                    for k in range(SSM_CONV - 1):
                        shift = SSM_CONV - 1 - k
                        conv = conv + cw_ref[k:k + 1, cols] * work[c, lo - shift:lo - shift + rows, :]
                    if r0 + rows == tm:
                        carry[cj, c] = work[c, tm:tm + pad, :]
                    o_ref[r0:r0 + rows, cols] = _silu(conv).astype(BF16)

    def plain(act, weights):
        for p in range(tn // sub):
            for r in range(tm // rows):
                acc = jnp.dot(h_s[r * rows:(r + 1) * rows, :], weights[:, p * sub:(p + 1) * sub],
                              preferred_element_type=F32)
                o_ref[r * rows:(r + 1) * rows, p * sub:(p + 1) * sub] = act(acc).astype(BF16)

    is_silu = (j >= silu_j[0]) & (j < silu_j[1])
    is_sigm = (j >= sigm_j[0]) & (j < sigm_j[1])
    pl.when(is_silu)(lambda: plain(_silu, w_ref))
    pl.when(is_sigm)(lambda: plain(jax.nn.sigmoid, wg_ref))
    pl.when((j != 0) & jnp.logical_not(is_conv | is_silu | is_sigm))(lambda: plain(lambda v: v, w_ref))


def _inproj_call(x2, row0, t, norm_w, mod3, cos, sin, w_all, w_gate, w_dt, conv_w, conv_b, conv_off, seq, tm, tn):
    d = x2.shape[1]
    conv_dim = conv_w.shape[1]
    n_lead = conv_off + conv_dim
    n = n_lead + w_gate.shape[1]
    assert w_gate.shape[1] == tn and n_lead % tn == 0
    tiles_per_seq = seq // tm
    off = row0 // tm
    assert tn == 2 * RET_HEADS * RET_QK_DIM, "rotary epilogue expects q and k in the first column tile"
    assert conv_off % tn == 0 and conv_dim % tn == 0
    conv_j0, conv_nj = conv_off // tn, conv_dim // tn
    sub = 512
    g_off = 2 * RET_HEADS * RET_QK_DIM + RET_HEADS * RET_V_DIM
    assert g_off % tn == 0 and (conv_off - g_off) % tn == 0 and (n - conv_off - conv_dim) % tn == 0
    silu_j = (g_off // tn, conv_off // tn)
    sigm_j = ((conv_off + conv_dim) // tn, n // tn)
    kern = functools.partial(_inproj_kernel, conv_j0=conv_j0, conv_nj=conv_nj, silu_j=silu_j, sigm_j=sigm_j,
                             tiles_per_seq=tiles_per_seq, tm=tm, tn=tn, sub=sub)
    conv_idx = lambda i, j: (0, jnp.clip(j - conv_j0, 0, conv_nj - 1))
    return pl.pallas_call(
        kern,
        out_shape=(jax.ShapeDtypeStruct((t, n), BF16), jax.ShapeDtypeStruct((t, LANES), F32)),
        grid=(t // tm, n // tn),
        in_specs=[
            pl.BlockSpec((tm, d), lambda i, j: (i + off, 0)),
            pl.BlockSpec((1, d), lambda i, j: (0, 0)),
            pl.BlockSpec((1, 1, d), lambda i, j: (((i + off) // tiles_per_seq) * N_MOD + 1, 0, 0)),
            pl.BlockSpec((1, 1, d), lambda i, j: (((i + off) // tiles_per_seq) * N_MOD + 0, 0, 0)),
            pl.BlockSpec((tm, LANES), lambda i, j: (i % tiles_per_seq, 0)),
            pl.BlockSpec((tm, LANES), lambda i, j: (i % tiles_per_seq, 0)),
            pl.BlockSpec((d, tn), lambda i, j: (0, jnp.minimum(j, n_lead // tn - 1))),
            pl.BlockSpec((d, tn), lambda i, j: (0, 0)),
            pl.BlockSpec((d, 2 * LANES), lambda i, j: (0, 0)),
            pl.BlockSpec((SSM_CONV, tn), conv_idx),
            pl.BlockSpec((1, tn), conv_idx),
        ],
        out_specs=(pl.BlockSpec((tm, tn), lambda i, j: (i, j)),
                   pl.BlockSpec((tm, LANES), lambda i, j: (i, 0))),
        scratch_shapes=[pltpu.VMEM((tm, d), BF16),
                        pltpu.VMEM((tn // LANES, tm + SUBLANES, LANES), F32),
                        pltpu.VMEM((conv_nj, tn // LANES, SUBLANES, LANES), F32)],
        compiler_params=_params(("arbitrary", "arbitrary")),
        name="inproj",
    )(x2, norm_w, mod3, mod3, cos, sin, w_all, w_gate, w_dt, conv_w, conv_b.reshape(1, conv_dim))


def _retention_body(q_ref, k_ref, v_ref, g_ref, din_ref, dq_ref, dk_ref, w_ref, o_ref, state, decay_c):
    heads = range(RET_HEADS)
    q = [q_ref[:, h * RET_QK_DIM:(h + 1) * RET_QK_DIM] for h in heads]
    k = [k_ref[:, h * RET_QK_DIM:(h + 1) * RET_QK_DIM] for h in heads]
    v = [v_ref[:, h * RET_V_DIM:(h + 1) * RET_V_DIM] for h in heads]
    scores = [(_nt_dot(q[h], k[h]) * din_ref[h]).astype(BF16) for h in heads]
    st = [state[h] for h in heads]
    cross = [jnp.dot(q[h], st[h].astype(BF16), preferred_element_type=F32) * dq_ref[h] for h in heads]
    inner = [jnp.dot(scores[h], v[h], preferred_element_type=F32) for h in heads]
    for h in heads:
        kd = (k[h].astype(F32) * dk_ref[h]).astype(BF16)
        state[h] = st[h] * decay_c[h] + _tn_dot(kd, v[h])
    acc = None
    for h in heads:
        ret = inner[h] + cross[h]
        ret = ret * lax.rsqrt(jnp.mean(ret * ret, axis=-1, keepdims=True) + EPS)
        ret = ret * g_ref[:, h * RET_V_DIM:(h + 1) * RET_V_DIM].astype(F32)
        part = jnp.dot(ret.astype(BF16), w_ref[h * RET_V_DIM:(h + 1) * RET_V_DIM, :],
                       preferred_element_type=F32)
        acc = part if acc is None else acc + part
    o_ref[...] = acc.astype(o_ref.dtype)


def _retention_tables(chunk):
    lg = np.log(1.0 - 2.0 ** (-5.0 - np.arange(RET_HEADS, dtype=np.float64)))
    idx = np.arange(chunk, dtype=np.float64)
    rel = idx[:, None] - idx[None, :]
    causal = rel >= 0
    din = np.where(causal[None], np.exp(np.where(causal, rel, 0.0)[None] * lg[:, None, None]), 0.0)
    dq = np.exp((idx + 1.0)[None, :, None] * lg[:, None, None])
    dk = np.exp((chunk - 1.0 - idx)[None, :, None] * lg[:, None, None])
    dc = tuple(float(v) for v in np.exp(chunk * lg))
    return (jnp.asarray(din, F32), jnp.asarray(dq, F32), jnp.asarray(dk, F32), dc)


def _ssd_decays(dt_ref, dtb_ref, alog_ref, tril_ref, exp_ref, n_sub, chunk):
    dt = jax.nn.softplus(dt_ref[...] + dtb_ref[...])
    adt = dt * -jnp.exp(alog_ref[...])
    acs = [jnp.dot(tril_ref[...], adt[s * chunk:(s + 1) * chunk, :], preferred_element_type=F32,
                   precision=HIGHEST) * LOG2_E for s in range(n_sub)]
    dt_x = jnp.dot(dt.astype(BF16), exp_ref[...], preferred_element_type=F32)
    return acs, dt_x


def _ssd_body(z_ref, xbc_ref, acs, dt_x, dsk_ref, nw_ref, yn_s, state, chunk, d_inner):
    heads_per_group = d_inner // SSM_HEAD_DIM // SSM_GROUPS
    gw = heads_per_group * SSM_HEAD_DIM
    assert SSM_HEAD_DIM * 2 == LANES and gw == 2 * LANES

    acs_t = acs.T
    li = lax.broadcasted_iota(jnp.int32, (chunk, chunk), 0)
    si = lax.broadcasted_iota(jnp.int32, (chunk, chunk), 1)
    causal = li >= si
    low_half = si < SSM_HEAD_DIM
    lane_g = lax.broadcasted_iota(jnp.int32, (chunk, gw), 1)

    b_off = d_inner
    c_off = d_inner + SSM_GROUPS * SSM_STATE
    for g in range(SSM_GROUPS):
        bm = xbc_ref[:, b_off + g * SSM_STATE: b_off + (g + 1) * SSM_STATE]
        cm = xbc_ref[:, c_off + g * SSM_STATE: c_off + (g + 1) * SSM_STATE]
        xs_g = xbc_ref[:, g * gw:(g + 1) * gw].astype(F32)
        xdt_g = xs_g * dt_x[:, g * gw:(g + 1) * gw]
        xdt_b = xdt_g.astype(BF16)
        cb = _nt_dot(cm, bm)
        cols, ms, xm = [], [], []
        for jh in range(heads_per_group):
            h = g * heads_per_group + jh
            col = jnp.broadcast_to(acs[:, h:h + 1], (chunk, chunk))
            seg = jnp.exp2(jnp.where(causal, col - acs_t[h:h + 1, :], -jnp.inf))
            cols.append(col)
            ms.append((cb * seg).astype(BF16))
            in_head = (lane_g >= jh * SSM_HEAD_DIM) & (lane_g < (jh + 1) * SSM_HEAD_DIM)
            xm.append(jnp.where(in_head, xdt_b, jnp.zeros_like(xdt_b)))
        y_diag = jnp.dot(jnp.concatenate(ms, axis=-1), jnp.concatenate(xm, axis=0),
                         preferred_element_type=F32)
        a_x = jnp.concatenate([jnp.where(low_half, cols[0], cols[1]),
                               jnp.where(low_half, cols[2], cols[3])], axis=-1)
        e_acs_x = jnp.exp2(a_x)
        a_last_x = a_x[chunk - 1:chunk, :]
        st = state[g]
        y_off = jnp.dot(cm, st.astype(BF16), preferred_element_type=F32) * e_acs_x
        xdec = (xdt_g * jnp.exp2(a_last_x - a_x)).astype(BF16)
        state[g] = st * e_acs_x[chunk - 1:chunk, :] + _tn_dot(bm, xdec)
        y = y_diag + y_off + dsk_ref[:, g * gw:(g + 1) * gw] * xs_g
        yz = y * z_ref[:, g * gw:(g + 1) * gw].astype(F32)
        yn = yz * lax.rsqrt(jnp.mean(yz * yz, axis=-1, keepdims=True) + EPS) * nw_ref[:, g * gw:(g + 1) * gw]
        yn_s[:, g * gw:(g + 1) * gw] = yn.astype(BF16)


def _mixers_kernel(q_ref, k_ref, v_ref, g_ref, din_ref, dq_ref, dk_ref, wret_ref,
                   z_ref, xbc_ref, dt_ref, dtb_ref, alog_ref, dsk_ref, nw_ref, tril_ref, exp_ref, wssm_ref,
                   ga_ref, gb_ref, x_ref, gm_ref, scf_ref, shf_ref, nf_ref, wo_ref, wr_ref, br_ref, tri_ref,
                   x1_ref, h2_ref, idx_ref, rank_ref, prow_ref, cnt_ref,
                   rstate, sstate, yn_s, ya_s, yb_s, cnt_s, *, decay_c, chunk, ssm_chunk, d_inner):
    b = pl.program_id(0)
    c = pl.program_id(1)

    @pl.when(c == 0)
    def _():
        rstate[...] = jnp.zeros_like(rstate)
        sstate[...] = jnp.zeros_like(sstate)

    @pl.when((b == 0) & (c == 0))
    def _():
        cnt_s[...] = jnp.zeros_like(cnt_s)

    n_sub = chunk // ssm_chunk
    acs, dt_x = _ssd_decays(dt_ref, dtb_ref, alog_ref, tril_ref, exp_ref, n_sub, ssm_chunk)
    _retention_body(q_ref, k_ref, v_ref, g_ref, din_ref, dq_ref, dk_ref, wret_ref, ya_s, rstate, decay_c)
    for sub in range(n_sub):
        rows = pl.ds(sub * ssm_chunk, ssm_chunk)
        _ssd_body(z_ref.at[rows, :], xbc_ref.at[rows, :], acs[sub],
                  dt_x[sub * ssm_chunk:(sub + 1) * ssm_chunk, :], dsk_ref, nw_ref, yn_s.at[rows, :],
                  sstate, ssm_chunk, d_inner)
    yb_s[...] = jnp.dot(yn_s[...], wssm_ref[...], preferred_element_type=F32)
    _merge_body(ya_s, yb_s, ga_ref, gb_ref, x_ref, gm_ref, scf_ref, shf_ref, nf_ref, wo_ref, wr_ref, br_ref,
                tri_ref, x1_ref, h2_ref, idx_ref, rank_ref, prow_ref, cnt_ref, cnt_s, chunk)


def _mixers_call(proj, dt_raw, x2, row0, mod3, w_ret, dt_bias, a_log, d_skip, ssm_norm, w_ssm,
                 norm_ffn, w_out, w_router_t, b_router, bsz, seq, chunk, ssm_chunk):
    t = proj.shape[0]
    d_inner, d = w_ssm.shape
    conv_dim = d_inner + 2 * SSM_GROUPS * SSM_STATE
    n_heads = d_inner // SSM_HEAD_DIM
    nc = seq // chunk
    gw = d_inner // SSM_GROUPS
    qk_w = RET_HEADS * RET_QK_DIM
    v_w = RET_HEADS * RET_V_DIM
    assert chunk % ssm_chunk == 0
    assert ssm_chunk == LANES, "the per-head decay tiles are built lane-for-lane against the chunk"
    din, dq, dk, dc = _retention_tables(chunk)
    pad_h = lambda v: jnp.pad(v.astype(F32), (0, LANES - n_heads)).reshape(1, LANES)
    tril = jnp.asarray(np.tril(np.ones((ssm_chunk, ssm_chunk), np.float32)))
    expand = np.zeros((LANES, d_inner), np.float32)
    for h in range(n_heads):
        expand[h, h * SSM_HEAD_DIM:(h + 1) * SSM_HEAD_DIM] = 1.0
    expand = jnp.asarray(expand, BF16)
    kern = functools.partial(_mixers_kernel, decay_c=dc, chunk=chunk, ssm_chunk=ssm_chunk, d_inner=d_inner)
    row = lambda b, c: b * nc + c
    z_blk = (2 * qk_w + 2 * v_w) // d_inner
    xbc_blk = (2 * qk_w + 2 * v_w + d_inner) // conv_dim
    full = lambda shape: pl.BlockSpec(shape, lambda b, c: (0,) * len(shape))
    nt = t // chunk
    ga_blk = proj.shape[1] // d - 2
    seq0 = row0 // seq
    modspec = lambda m: pl.BlockSpec((1, 1, d), lambda b, c: ((b + seq0) * N_MOD + m, 0, 0))
    tri = jnp.asarray(np.triu(np.ones((chunk, chunk), np.float32), 1), BF16)
    return pl.pallas_call(
        kern,
        out_shape=(jax.ShapeDtypeStruct((t, d), F32), jax.ShapeDtypeStruct((t * SLAB_ROWS, LANES), jnp.uint32),
                   jax.ShapeDtypeStruct((nt, SUBLANES, chunk), jnp.int32),
                   jax.ShapeDtypeStruct((nt, SUBLANES, chunk), jnp.int32),
                   jax.ShapeDtypeStruct((t, LANES), F32),
                   jax.ShapeDtypeStruct((N_EXPERTS, LANES), jnp.int32)),
        grid=(bsz, nc),
        in_specs=[
            pl.BlockSpec((chunk, qk_w), lambda b, c: (row(b, c), 0)),
            pl.BlockSpec((chunk, qk_w), lambda b, c: (row(b, c), 1)),
            pl.BlockSpec((chunk, v_w), lambda b, c: (row(b, c), 1)),
            pl.BlockSpec((chunk, v_w), lambda b, c: (row(b, c), 2)),
            full((RET_HEADS, chunk, chunk)), full((RET_HEADS, chunk, 1)), full((RET_HEADS, chunk, 1)),
            full((v_w, d)),
            pl.BlockSpec((chunk, d_inner), lambda b, c: (row(b, c), z_blk)),
            pl.BlockSpec((chunk, conv_dim), lambda b, c: (row(b, c), xbc_blk)),
            pl.BlockSpec((chunk, LANES), lambda b, c: (row(b, c), 0)),
            full((1, LANES)), full((1, LANES)),
            full((1, d_inner)), full((1, d_inner)), full((ssm_chunk, ssm_chunk)), full((LANES, d_inner)),
            full((d_inner, d)),
            pl.BlockSpec((chunk, d), lambda b, c: (row(b, c), ga_blk)),
            pl.BlockSpec((chunk, d), lambda b, c: (row(b, c), ga_blk + 1)),
            pl.BlockSpec((chunk, d), lambda b, c: (row0 // chunk + row(b, c), 0)),
            modspec(2), modspec(4), modspec(3),
            full((1, d)), full((d, d)), full((2 * N_EXPERTS, d)), full((N_EXPERTS, 1)), full((chunk, chunk)),
        ],
        out_specs=(pl.BlockSpec((chunk, d), lambda b, c: (row(b, c), 0)),
                   pl.BlockSpec((chunk * SLAB_ROWS, LANES), lambda b, c: (row(b, c), 0)),
                   pl.BlockSpec((1, SUBLANES, chunk), lambda b, c: (row(b, c), 0, 0)),
                   pl.BlockSpec((1, SUBLANES, chunk), lambda b, c: (row(b, c), 0, 0)),
                   pl.BlockSpec((chunk, LANES), lambda b, c: (row(b, c), 0)),
                   full((N_EXPERTS, LANES))),
        scratch_shapes=[pltpu.VMEM((RET_HEADS, RET_QK_DIM, RET_V_DIM), F32),
                        pltpu.VMEM((SSM_GROUPS, SSM_STATE, gw), F32),
                        pltpu.VMEM((chunk, d_inner), BF16),
                        pltpu.VMEM((chunk, d), F32), pltpu.VMEM((chunk, d), F32),
                        pltpu.VMEM((N_EXPERTS, LANES), F32)],
        compiler_params=_params(("arbitrary", "arbitrary")),
        name="mixers",
    )(proj, proj, proj, proj, din, dq, dk, w_ret,
      proj, proj, dt_raw, pad_h(dt_bias), pad_h(a_log),
      jnp.repeat(d_skip.astype(F32), SSM_HEAD_DIM).reshape(1, d_inner), ssm_norm.reshape(1, d_inner),
      tril, expand, w_ssm,
      proj, proj, x2, mod3, mod3, mod3, norm_ffn, w_out, w_router_t, b_router.reshape(N_EXPERTS, 1), tri)


def _merge_body(ya_ref, yb_ref, ga_ref, gb_ref, x_ref, gm_ref, scf_ref, shf_ref, nw_ref, wo_ref,
                wr_ref, br_ref, tri_ref,
                x1_ref, h2_ref, idx_ref, rank_ref, prow_ref, cnt_ref, cnt_s, tm):
    merged = (ga_ref[...].astype(F32) * ya_ref[...].astype(F32)
              + gb_ref[...].astype(F32) * yb_ref[...].astype(F32))
    mo = jnp.dot(merged.astype(BF16), wo_ref[...], preferred_element_type=F32)
    x1 = x_ref[...] + gm_ref[0] * mo
    x1_ref[...] = x1
    ms = jnp.mean(x1 * x1, axis=-1, keepdims=True)
    h2 = x1 * lax.rsqrt(ms + EPS) * nw_ref[...] * (1.0 + scf_ref[0]) + shf_ref[0]
    _store_slabs(h2_ref, h2, tm)

    h_hi = h2.astype(BF16)
    h_lo = (h2 - h_hi.astype(F32)).astype(BF16)
    lg2 = _nt_dot(wr_ref[...], h_hi)
    lg = lg2[:N_EXPERTS] + lg2[N_EXPERTS:] + _nt_dot(wr_ref[:N_EXPERTS, :], h_lo) + br_ref[...]
    sub = lax.broadcasted_iota(jnp.int32, lg.shape, 0)
    work = lg
    vals, idxs, sels = [], [], []
    for _ in range(TOP_K):
        m = jnp.max(work, axis=0, keepdims=True)
        ik = jnp.min(jnp.where(work == m, sub, N_EXPERTS), axis=0, keepdims=True)
        sel = sub == ik
        vals.append(m)
        idxs.append(ik)
        sels.append(sel)
        work = jnp.where(sel, -jnp.inf, work)
    exps = [jnp.exp(v - vals[0]) for v in vals]
    denom = exps[0]
    for e in exps[1:]:
        denom = denom + e
    probs = [e / denom for e in exps]

    base = cnt_s[:, 0:1]
    ranks = []
    for k in range(TOP_K):
        mk = jnp.where(sels[k], 1.0, 0.0)
        pre = jnp.dot(mk.astype(BF16), tri_ref[...], preferred_element_type=F32)
        ranks.append(jnp.sum(jnp.where(sels[k], pre + base, 0.0), axis=0, keepdims=True))
        base = base + jnp.sum(mk, axis=1, keepdims=True)
    cnt_s[...] = jnp.broadcast_to(base, cnt_s.shape)
    cnt_ref[...] = cnt_s[...].astype(jnp.int32)

    zi = jnp.zeros((SUBLANES - TOP_K, tm), jnp.int32)
    idx_ref[0] = jnp.concatenate(idxs + [zi], axis=0)
    rank_ref[0] = jnp.concatenate([r.astype(jnp.int32) for r in ranks] + [zi], axis=0)
    pt = jnp.concatenate(probs + [jnp.zeros((LANES - TOP_K, tm), F32)], axis=0)
    prow_ref[...] = pt.T


def _sc_mesh():
    return plsc.VectorSubcoreMesh(core_axis_name="c", subcore_axis_name="s")


def _sc_worker():
    return lax.axis_index("s") * SC_CORES + lax.axis_index("c")


def _sc_scatter_rows(rows, dest, n_out):
    t = rows.shape[0]
    n_k = dest.shape[0]
    g = SC_GROUP
    n_w = SC_CORES * SC_SUBCORES
    assert t % (n_w * g) == 0
    cpw = t // (n_w * g)
    dest_w = dest.reshape(n_k, n_w, cpw, g).transpose(1, 0, 2, 3)

    @functools.partial(
        pl.kernel, mesh=_sc_mesh(),
        out_type=jax.ShapeDtypeStruct((n_out,) + rows.shape[1:], rows.dtype),
        scratch_types=[pltpu.VMEM((n_k, cpw, g), jnp.int32),
                       pltpu.VMEM((g,) + rows.shape[1:], rows.dtype),
                       pltpu.SemaphoreType.DMA],
    )
    def scatter(rows_hbm, dest_hbm, out_hbm, idx_v, rows_v, sem):
        wid = _sc_worker()
        pltpu.sync_copy(dest_hbm.at[wid], idx_v)

        @pl.loop(0, cpw)
        def _(cc):
            r0 = pl.multiple_of((wid * cpw + cc) * g, g)
            pltpu.sync_copy(rows_hbm.at[pl.ds(r0, g)], rows_v)
            copies = [pltpu.async_copy(rows_v, out_hbm.at[idx_v.at[k, cc]], sem) for k in range(n_k)]
            for cp in copies:
                cp.wait()

    return scatter(rows, dest_w)


def _sc_gather_rows(table, idx):
    m = idx.shape[0]
    g = SC_GROUP
    n_w = SC_CORES * SC_SUBCORES
    assert m % (n_w * g) == 0
    per_w = m // n_w

    @functools.partial(
        pl.kernel, mesh=_sc_mesh(),
        out_type=jax.ShapeDtypeStruct((m,) + table.shape[1:], table.dtype),
        scratch_types=[pltpu.VMEM((per_w,), jnp.int32),
                       pltpu.VMEM((g,) + table.shape[1:], table.dtype),
                       pltpu.SemaphoreType.DMA],
    )
    def gather(table_hbm, idx_hbm, out_hbm, idx_v, rows_v, sem):
        base = _sc_worker() * per_w
        pltpu.sync_copy(idx_hbm.at[pl.ds(base, per_w)], idx_v)

        @pl.loop(0, per_w // g)
        def _(cc):
            off = pl.multiple_of(cc * g, g)
            pltpu.async_copy(table_hbm.at[idx_v.at[pl.ds(off, g)]], rows_v, sem).wait()
            pltpu.sync_copy(rows_v, out_hbm.at[pl.ds(base + off, g)])

    return gather(table, idx)


def _ffn_kernel(be_ref, br_ref, bv_ref, first_ref, slot_ref, next_ref,
                x_ref, wgu_hbm, bgu_ref, wd_hbm, bd_ref, o_ref, wgu_f, wd_f, wgu_s, wd_s, sem, *, bm, d_ff):
    i = pl.program_id(0)

    def fetch(e, slot):
        return (pltpu.make_async_copy(wgu_hbm.at[e], wgu_f.at[slot], sem.at[0, slot]),
                pltpu.make_async_copy(wd_hbm.at[e], wd_f.at[slot], sem.at[1, slot]))

    @pl.when(i == 0)
    def _():
        for cp in fetch(be_ref[0], slot_ref[0]):
            cp.start()

    @pl.when(first_ref[i] == 1)
    def _():
        slot = slot_ref[i]
        for cp in fetch(be_ref[i], slot):
            cp.wait()
        wgu_s[...] = wgu_f[slot].astype(BF16)
        wd_s[...] = wd_f[slot].astype(BF16)

        @pl.when(next_ref[i] >= 0)
        def _():
            for cp in fetch(next_ref[i], 1 - slot):
                cp.start()

    @pl.when(bv_ref[i] == 1)
    def _():
        x = jnp.concatenate(_load_slabs(x_ref, bm), axis=-1).astype(BF16)
        gu = jnp.dot(x, wgu_s[...], preferred_element_type=F32) + bgu_ref[0]
        gate = jnp.minimum(gu[:, :d_ff], SWIGLU_LIMIT)
        up = jnp.clip(gu[:, d_ff:], -SWIGLU_LIMIT, SWIGLU_LIMIT)
        act = gate * jax.nn.sigmoid(SWIGLU_ALPHA * gate) * (up + 1.0)
        y = jnp.dot(act.astype(BF16), wd_s[...], preferred_element_type=F32) + bd_ref[0]
        _store_slabs(o_ref, y, bm)

    @pl.when(bv_ref[i] == 0)
    def _():
        o_ref[...] = jnp.zeros_like(o_ref)


def _ffn_call(blk_e, blk_row, blk_valid, xs, w_gu, b_gu, w_d, b_d, bm):
    n_e, d, f2 = w_gu.shape
    d_ff = f2 // 2
    nb = blk_e.shape[0]
    first = jnp.concatenate([jnp.ones((1,), jnp.int32), (blk_e[1:] != blk_e[:-1]).astype(jnp.int32)])
    slot = (jnp.cumsum(first) - 1) % 2
    later = jnp.where(blk_e[None, :] > blk_e[:, None], blk_e[None, :], n_e)
    nxt = jnp.min(later, axis=1)
    nxt = jnp.where(nxt == n_e, -1, nxt)
    kern = functools.partial(_ffn_kernel, bm=bm, d_ff=d_ff)
    imap = lambda f: (lambda i, be, br, bv, fi, sl, nx: f(i, be, br))
    gs = pltpu.PrefetchScalarGridSpec(
        num_scalar_prefetch=6,
        grid=(nb,),
        in_specs=[pl.BlockSpec((bm * SLAB_ROWS, LANES), imap(lambda i, be, br: (br[i], 0))),
                  pl.BlockSpec(memory_space=pl.ANY),
                  pl.BlockSpec((1, 1, f2), imap(lambda i, be, br: (be[i], 0, 0))),
                  pl.BlockSpec(memory_space=pl.ANY),
                  pl.BlockSpec((1, 1, d), imap(lambda i, be, br: (be[i], 0, 0)))],
        out_specs=pl.BlockSpec((bm * SLAB_ROWS, LANES), imap(lambda i, be, br: (i, 0))),
        scratch_shapes=[pltpu.VMEM((2, d, f2), F32), pltpu.VMEM((2, d_ff, d), F32),
                        pltpu.VMEM((d, f2), BF16), pltpu.VMEM((d_ff, d), BF16),
                        pltpu.SemaphoreType.DMA((2, 2))],
    )
    return pl.pallas_call(
        kern,
        out_shape=jax.ShapeDtypeStruct((nb * bm * SLAB_ROWS, LANES), jnp.uint32),
        grid_spec=gs,
        compiler_params=_params(("arbitrary",)),
        name="ffn",
    )(blk_e, blk_row, blk_valid, first, slot.astype(jnp.int32), nxt.astype(jnp.int32),
      xs, w_gu, b_gu.reshape(n_e, 1, f2), w_d, b_d.reshape(n_e, 1, d))


def _combine_kernel(y0_ref, y1_ref, y2_ref, y3_ref, prow_ref, x1_ref, gf_ref, nw_ref, *rest, tm):
    o_ref = rest[-1]
    p = prow_ref[...]
    pieces = [_load_slabs(y_ref, tm) for y_ref in (y0_ref, y1_ref, y2_ref, y3_ref)]
    for s in range(len(pieces[0])):
        moe = None
        for k in range(TOP_K):
            piece = pieces[k][s] * p[:, k:k + 1]
            moe = piece if moe is None else moe + piece
        sl = slice(s * LANES, (s + 1) * LANES)
        o_ref[:, sl] = x1_ref[:, sl] + gf_ref[0][:, sl] * moe
    xo = o_ref[...]
    o_ref[...] = xo * lax.rsqrt(jnp.mean(xo * xo, axis=-1, keepdims=True) + EPS) * nw_ref[...]


def _combine_call(ytok, prow, x1, mod3, norm_final, seq, tm, row0, t_total, out_prev):
    t, d = x1.shape
    nt = t // tm
    tiles_per_seq = seq // tm
    off = row0 // tm
    kern = functools.partial(_combine_kernel, tm=tm)
    yspec = lambda k: pl.BlockSpec((tm * SLAB_ROWS, LANES), lambda i: (k * nt + i, 0))
    in_specs = [yspec(0), yspec(1), yspec(2), yspec(3),
                pl.BlockSpec((tm, LANES), lambda i: (i, 0)),
                pl.BlockSpec((tm, d), lambda i: (i, 0)),
                pl.BlockSpec((1, 1, d), lambda i: (((i + off) // tiles_per_seq) * N_MOD + 5, 0, 0)),
                pl.BlockSpec((1, d), lambda i: (0, 0))]
    args = [ytok, ytok, ytok, ytok, prow, x1, mod3, norm_final]
    aliases = {}
    if out_prev is not None:
        in_specs.append(pl.BlockSpec(memory_space=pl.ANY))
        aliases = {len(args): 0}
        args.append(out_prev)
    return pl.pallas_call(
        kern,
        out_shape=jax.ShapeDtypeStruct((t_total, d), F32),
        grid=(nt,),
        in_specs=in_specs,
        out_specs=pl.BlockSpec((tm, d), lambda i: (i + off, 0)),
        input_output_aliases=aliases,
        compiler_params=_params(("arbitrary",)),
        name="combine",
    )(*args)


def _plan(seq):
    def fit(pref):
        tm = min(pref, seq)
        assert seq % tm == 0
        return tm
    return dict(tm_in=fit(1024), tm_merge=fit(512), tm_moe=fit(512),
                ret_chunk=fit(RET_CHUNK), ssm_chunk=fit(SSM_CHUNK))


def _layer(x2, mod3, bsz, seq, norm_mix, norm_ffn, w_in, conv_w, conv_b, dt_bias, a_log, d_skip, ssm_norm,
           w_ret_out, w_ssm_out, w_out, w_router, b_router, w_gate_up, b_gate_up, w_down, b_down,
           norm_final):
    t, d = x2.shape
    plan = _plan(seq)
    qk_w = RET_HEADS * RET_QK_DIM
    v_w = RET_HEADS * RET_V_DIM
    d_inner = w_ssm_out.shape[0]
    conv_dim = conv_w.shape[1]
    n_heads = d_inner // SSM_HEAD_DIM
    dt_off = 2 * qk_w + 2 * v_w + d_inner + conv_dim

    w_all = w_in.astype(BF16)
    w_gate = w_in[:, dt_off + n_heads:].astype(BF16)
    w_dt = jnp.pad(w_in[:, dt_off:dt_off + n_heads], ((0, 0), (0, LANES - n_heads)))
    w_dt_hi = w_dt.astype(BF16)
    w_dt = jnp.concatenate([w_dt_hi, (w_dt - w_dt_hi.astype(F32)).astype(BF16)], axis=1)
    half = RET_QK_DIM // 2
    inv_freq = ROPE_BASE ** (-jnp.arange(half, dtype=F32) / half)
    ang = jnp.arange(seq, dtype=F32)[:, None] * inv_freq[None, :]
    cos, sin = jnp.cos(ang), jnp.sin(ang)

    w_ret_b, w_ssm_b, w_out_b = w_ret_out.astype(BF16), w_ssm_out.astype(BF16), w_out.astype(BF16)
    w_r_hi = w_router.T.astype(BF16)
    w_router_t = jnp.concatenate([w_r_hi, (w_router.T - w_r_hi.astype(F32)).astype(BF16)], axis=0)
    bm = FFN_BLOCK
    slab = (SLAB_ROWS, LANES)

    def mixer(row0, tg, bg):
        proj, dt_raw = _inproj_call(x2, row0, tg, norm_mix.reshape(1, d), mod3, cos, sin, w_all, w_gate, w_dt,
                                    conv_w, conv_b, 2 * qk_w + 2 * v_w + d_inner, seq, plan["tm_in"], 2 * qk_w)
        x1, h2, idx, rank, prow, cnt = _mixers_call(
            proj, dt_raw, x2, row0, mod3, w_ret_b, dt_bias, a_log, d_skip, ssm_norm, w_ssm_b,
            norm_ffn.reshape(1, d), w_out_b, w_router_t, b_router, bg, seq, plan["ret_chunk"], plan["ssm_chunk"])
        counts = cnt[:, 0]
        padded = ((counts + bm - 1) // bm) * bm
        pad_end = jnp.cumsum(padded)
        start_pad = pad_end - padded
        n_blocks = (tg * TOP_K) // bm + N_EXPERTS
        e_ids = jnp.arange(N_EXPERTS, dtype=jnp.int32)[:, None, None, None]
        dest = rank + jnp.sum(jnp.where(idx[None] == e_ids, start_pad[:, None, None, None], 0), axis=0)
        dest = dest[:, :TOP_K, :].transpose(1, 0, 2).reshape(TOP_K, tg).astype(jnp.int32)
        n_real = pad_end[-1] // bm
        blk_valid = (jnp.arange(n_blocks) < n_real).astype(jnp.int32)
        blk_row = jnp.minimum(jnp.arange(n_blocks), n_real - 1).astype(jnp.int32)
        blk_e = jnp.minimum(jnp.sum(pad_end[None, :] <= (blk_row * bm)[:, None], axis=1),
                            N_EXPERTS - 1).astype(jnp.int32)
        xs = _sc_scatter_rows(h2.reshape((tg,) + slab), dest, n_blocks * bm)
        return dict(x1=x1, prow=prow, dest=dest, blocks=(blk_e, blk_row, blk_valid), xs=xs, row0=row0)

    def experts(m):
        n_rows = m["xs"].shape[0]
        ys = _ffn_call(*m["blocks"], m["xs"].reshape(n_rows * SLAB_ROWS, LANES),
                       w_gate_up, b_gate_up, w_down, b_down, bm)
        return _sc_gather_rows(ys.reshape((n_rows,) + slab), m["dest"].reshape(-1))

    n_groups = N_GROUPS if bsz % N_GROUPS == 0 else 1
    bg = bsz // n_groups
    tg = bg * seq
    groups = [mixer(g * tg, tg, bg) for g in range(n_groups)]
    ytoks = [experts(m) for m in groups]
    out = None
    for y, m in zip(ytoks, groups):
        out = _combine_call(y.reshape(TOP_K * tg * SLAB_ROWS, LANES), m["prow"], m["x1"], mod3,
                            norm_final.reshape(1, d), seq, plan["tm_moe"], m["row0"], t, out)
    return out


def kernel(x, c, w_ada, b_ada, norm_mix, norm_ffn, w_in, conv_w, conv_b, dt_bias, a_log, d_skip, ssm_norm,
           w_ret_out, w_ssm_out, w_out, w_router, b_router, w_gate_up, b_gate_up, w_down, b_down, norm_final):
    bsz, seq, d = x.shape
    depth = w_ada.shape[0]
    assert depth == 1, "the final norm is fused into the single layer's last kernel"
    x2 = x.reshape(bsz * seq, d)
    l = 0
    mod = _mod_call(c, w_ada[l], b_ada[l])
    mod3 = mod.reshape(bsz * N_MOD, 1, d)
    out = _layer(x2, mod3, bsz, seq, norm_mix[l], norm_ffn[l], w_in[l], conv_w[l], conv_b[l], dt_bias[l],
                 a_log[l], d_skip[l], ssm_norm[l], w_ret_out[l], w_ssm_out[l], w_out[l], w_router[l],
                 b_router[l], w_gate_up[l], b_gate_up[l], w_down[l], b_down[l], norm_final)
    return out.reshape(bsz, seq, d)
```

```python
import functools
import math

import numpy as np
import jax
import jax.numpy as jnp
from jax import lax
from jax.experimental import pallas as pl
from jax.experimental.pallas import tpu as pltpu
from jax.experimental.pallas import tpu_sc as plsc

F32 = jnp.float32
BF16 = jnp.bfloat16
HIGHEST = lax.Precision.HIGHEST

EPS = 1e-6
N_MOD = 6
RET_HEADS = 4
RET_QK_DIM = 256
RET_V_DIM = 512
ROPE_BASE = 10000.0
SSM_HEAD_DIM = 64
SSM_GROUPS = 8
SSM_STATE = 128
SSM_CONV = 4
N_EXPERTS = 32
TOP_K = 4
SWIGLU_LIMIT = 7.0
SWIGLU_ALPHA = 1.702

LANES = 128
SUBLANES = 8
VMEM_LIMIT = 56 * 1024 * 1024

RET_CHUNK = 256
SSM_CHUNK = 128
FFN_BLOCK = 432
SC_CORES = 2
SC_SUBCORES = 16
SC_GROUP = 64
N_GROUPS = 2


def _params(sem, vmem=VMEM_LIMIT):
    return pltpu.CompilerParams(dimension_semantics=sem, vmem_limit_bytes=vmem)


def _nt_dot(a, b, **kw):
    return lax.dot_general(a, b, (((1,), (1,)), ((), ())), preferred_element_type=F32, **kw)


def _tn_dot(a, b, **kw):
    return lax.dot_general(a, b, (((0,), (0,)), ((), ())), preferred_element_type=F32, **kw)


def _silu(v):
    return v * jax.nn.sigmoid(v)


SLAB_ROWS = 4
HIGH_HALF = 0xFFFF0000
LOG2_E = math.log2(math.e)


def _store_slabs(ref, vals, n):
    for s in range(SLAB_ROWS):
        lo = vals[:, s * LANES:(s + 1) * LANES].astype(BF16).astype(F32)
        hi = vals[:, (s + SLAB_ROWS) * LANES:(s + SLAB_ROWS + 1) * LANES].astype(BF16).astype(F32)
        word = (pltpu.bitcast(lo, jnp.uint32) >> 16) | (pltpu.bitcast(hi, jnp.uint32) & jnp.uint32(HIGH_HALF))
        ref[pl.ds(s, n, stride=SLAB_ROWS), :] = word


def _load_slabs(ref, n, base=0):
    lo, hi = [], []
    for s in range(SLAB_ROWS):
        word = ref[pl.ds(base + s, n, stride=SLAB_ROWS), :]
        lo.append(pltpu.bitcast(word << 16, F32))
        hi.append(pltpu.bitcast(word & jnp.uint32(HIGH_HALF), F32))
    return lo + hi


def _mod_kernel(c_ref, w_ref, b_ref, o_ref):
    cond = _silu(c_ref[...])
    o_ref[...] = jnp.dot(cond, w_ref[...], preferred_element_type=F32, precision=HIGHEST) + b_ref[...]


def _mod_call(c, w_ada, b_ada):
    bsz, d = c.shape
    n = w_ada.shape[1]
    return pl.pallas_call(
        _mod_kernel,
        out_shape=jax.ShapeDtypeStruct((bsz, n), F32),
        grid=(n // d,),
        in_specs=[pl.BlockSpec((bsz, d), lambda j: (0, 0)),
                  pl.BlockSpec((d, d), lambda j: (0, j)),
                  pl.BlockSpec((1, d), lambda j: (0, j))],
        out_specs=pl.BlockSpec((bsz, d), lambda j: (0, j)),
        compiler_params=_params(("arbitrary",)),
        name="mod",
    )(c, w_ada, b_ada.reshape(1, n))


def _inproj_kernel(x_ref, nw_ref, sc_ref, sh_ref, cos_ref, sin_ref, w_ref, wg_ref, wdt_ref, cw_ref, cb_ref,
                   o_ref, dt_ref, h_s, work, carry, *, conv_j0, conv_nj, silu_j, sigm_j, tiles_per_seq,
                   tm, tn, sub):
    i = pl.program_id(0)
    j = pl.program_id(1)
    n_dt = dt_ref.shape[1]
    rows = min(tm, 256)

    @pl.when(j == 0)
    def _():
        xf = x_ref[...]
        ms = jnp.mean(xf * xf, axis=-1, keepdims=True)
        y = xf * lax.rsqrt(ms + EPS) * nw_ref[...]
        hm = y * (1.0 + sc_ref[0]) + sh_ref[0]
        hb = hm.astype(BF16)
        h_s[...] = hb
        h_lo = (hm - hb.astype(F32)).astype(BF16)
        d_hi = jnp.dot(hb, wdt_ref[...], preferred_element_type=F32)
        d_lo = jnp.dot(h_lo, wdt_ref[:, :n_dt], preferred_element_type=F32)
        dt_ref[...] = d_hi[:, :n_dt] + d_hi[:, n_dt:] + d_lo
        half = RET_QK_DIM // 2
        for p in range(tn // sub):
            for r in range(tm // rows):
                rs = slice(r * rows, (r + 1) * rows)
                acc = jnp.dot(h_s[rs, :], w_ref[:, p * sub:(p + 1) * sub], preferred_element_type=F32)
                cos = cos_ref[rs, :]
                sin = sin_ref[rs, :]
                for cc in range(sub // RET_QK_DIM):
                    c = p * (sub // RET_QK_DIM) + cc
                    a = acc[:, cc * RET_QK_DIM: cc * RET_QK_DIM + half]
                    b = acc[:, cc * RET_QK_DIM + half: (cc + 1) * RET_QK_DIM]
                    scale = 1.0 if c < RET_HEADS else RET_QK_DIM ** -0.5
                    o_ref[rs, c * RET_QK_DIM: c * RET_QK_DIM + half] = ((a * cos - b * sin) * scale).astype(BF16)
                    o_ref[rs, c * RET_QK_DIM + half: (c + 1) * RET_QK_DIM] = (
                        (a * sin + b * cos) * scale).astype(BF16)

    is_conv = (j >= conv_j0) & (j < conv_j0 + conv_nj)

    @pl.when(is_conv)
    def _():
        cj = j - conv_j0
        pad = SUBLANES

        @pl.when(i % tiles_per_seq == 0)
        def _():
            carry[cj] = jnp.zeros(carry.shape[1:], F32)

        for p in range(tn // sub):
            for r in range(tm // rows):
                r0 = r * rows
                acc = jnp.dot(h_s[r0:r0 + rows, :], w_ref[:, p * sub:(p + 1) * sub], preferred_element_type=F32)
                for cc in range(sub // LANES):
                    c = p * (sub // LANES) + cc
                    cols = slice(c * LANES, (c + 1) * LANES)
                    if r == 0:
                        work[c, 0:pad, :] = carry[cj, c]
                    lo = pad + r0
                    work[c, lo:lo + rows, :] = acc[:, cc * LANES:(cc + 1) * LANES]
                    conv = cb_ref[:, cols] + cw_ref[SSM_CONV - 1:SSM_CONV, cols] * work[c, lo:lo + rows, :]
                    for k in range(SSM_CONV - 1):
                        shift = SSM_CONV - 1 - k
                        conv = conv + cw_ref[k:k + 1, cols] * work[c, lo - shift:lo - shift + rows, :]
                    if r0 + rows == tm:
                        carry[cj, c] = work[c, tm:tm + pad, :]
                    o_ref[r0:r0 + rows, cols] = _silu(conv).astype(BF16)

    def plain(act, weights):
        for p in range(tn // sub):
            for r in range(tm // rows):
                acc = jnp.dot(h_s[r * rows:(r + 1) * rows, :], weights[:, p * sub:(p + 1) * sub],
                              preferred_element_type=F32)
                o_ref[r * rows:(r + 1) * rows, p * sub:(p + 1) * sub] = act(acc).astype(BF16)

    is_silu = (j >= silu_j[0]) & (j < silu_j[1])
    is_sigm = (j >= sigm_j[0]) & (j < sigm_j[1])
    pl.when(is_silu)(lambda: plain(_silu, w_ref))
    pl.when(is_sigm)(lambda: plain(jax.nn.sigmoid, wg_ref))
    pl.when((j != 0) & jnp.logical_not(is_conv | is_silu | is_sigm))(lambda: plain(lambda v: v, w_ref))


def _inproj_call(x2, row0, t, norm_w, mod3, cos, sin, w_all, w_gate, w_dt, conv_w, conv_b, conv_off, seq, tm, tn):
    d = x2.shape[1]
    conv_dim = conv_w.shape[1]
    n_lead = conv_off + conv_dim
    n = n_lead + w_gate.shape[1]
    assert w_gate.shape[1] == tn and n_lead % tn == 0
    tiles_per_seq = seq // tm
    off = row0 // tm
    assert tn == 2 * RET_HEADS * RET_QK_DIM, "rotary epilogue expects q and k in the first column tile"
    assert conv_off % tn == 0 and conv_dim % tn == 0
    conv_j0, conv_nj = conv_off // tn, conv_dim // tn
    sub = 512
    g_off = 2 * RET_HEADS * RET_QK_DIM + RET_HEADS * RET_V_DIM
    assert g_off % tn == 0 and (conv_off - g_off) % tn == 0 and (n - conv_off - conv_dim) % tn == 0
    silu_j = (g_off // tn, conv_off // tn)
    sigm_j = ((conv_off + conv_dim) // tn, n // tn)
    kern = functools.partial(_inproj_kernel, conv_j0=conv_j0, conv_nj=conv_nj, silu_j=silu_j, sigm_j=sigm_j,
                             tiles_per_seq=tiles_per_seq, tm=tm, tn=tn, sub=sub)
    conv_idx = lambda i, j: (0, jnp.clip(j - conv_j0, 0, conv_nj - 1))
    return pl.pallas_call(
        kern,
        out_shape=(jax.ShapeDtypeStruct((t, n), BF16), jax.ShapeDtypeStruct((t, LANES), F32)),
        grid=(t // tm, n // tn),
        in_specs=[
            pl.BlockSpec((tm, d), lambda i, j: (i + off, 0)),
            pl.BlockSpec((1, d), lambda i, j: (0, 0)),
            pl.BlockSpec((1, 1, d), lambda i, j: (((i + off) // tiles_per_seq) * N_MOD + 1, 0, 0)),
            pl.BlockSpec((1, 1, d), lambda i, j: (((i + off) // tiles_per_seq) * N_MOD + 0, 0, 0)),
            pl.BlockSpec((tm, LANES), lambda i, j: (i % tiles_per_seq, 0)),
            pl.BlockSpec((tm, LANES), lambda i, j: (i % tiles_per_seq, 0)),
            pl.BlockSpec((d, tn), lambda i, j: (0, jnp.minimum(j, n_lead // tn - 1))),
            pl.BlockSpec((d, tn), lambda i, j: (0, 0)),
            pl.BlockSpec((d, 2 * LANES), lambda i, j: (0, 0)),
            pl.BlockSpec((SSM_CONV, tn), conv_idx),
            pl.BlockSpec((1, tn), conv_idx),
        ],
        out_specs=(pl.BlockSpec((tm, tn), lambda i, j: (i, j)),
                   pl.BlockSpec((tm, LANES), lambda i, j: (i, 0))),
        scratch_shapes=[pltpu.VMEM((tm, d), BF16),
                        pltpu.VMEM((tn // LANES, tm + SUBLANES, LANES), F32),
                        pltpu.VMEM((conv_nj, tn // LANES, SUBLANES, LANES), F32)],
        compiler_params=_params(("arbitrary", "arbitrary")),
        name="inproj",
    )(x2, norm_w, mod3, mod3, cos, sin, w_all, w_gate, w_dt, conv_w, conv_b.reshape(1, conv_dim))


def _retention_body(q_ref, k_ref, v_ref, g_ref, din_ref, dq_ref, dk_ref, w_ref, o_ref, state, decay_c):
    heads = range(RET_HEADS)
    q = [q_ref[:, h * RET_QK_DIM:(h + 1) * RET_QK_DIM] for h in heads]
    k = [k_ref[:, h * RET_QK_DIM:(h + 1) * RET_QK_DIM] for h in heads]
    v = [v_ref[:, h * RET_V_DIM:(h + 1) * RET_V_DIM] for h in heads]
    scores = [(_nt_dot(q[h], k[h]) * din_ref[h]).astype(BF16) for h in heads]
    st = [state[h] for h in heads]
    cross = [jnp.dot(q[h], st[h].astype(BF16), preferred_element_type=F32) * dq_ref[h] for h in heads]
    inner = [jnp.dot(scores[h], v[h], preferred_element_type=F32) for h in heads]
    for h in heads:
        kd = (k[h].astype(F32) * dk_ref[h]).astype(BF16)
        state[h] = st[h] * decay_c[h] + _tn_dot(kd, v[h])
    acc = None
    for h in heads:
        ret = inner[h] + cross[h]
        ret = ret * lax.rsqrt(jnp.mean(ret * ret, axis=-1, keepdims=True) + EPS)
        ret = ret * g_ref[:, h * RET_V_DIM:(h + 1) * RET_V_DIM].astype(F32)
        part = jnp.dot(ret.astype(BF16), w_ref[h * RET_V_DIM:(h + 1) * RET_V_DIM, :],
                       preferred_element_type=F32)
        acc = part if acc is None else acc + part
    o_ref[...] = acc.astype(o_ref.dtype)


def _retention_tables(chunk):
    lg = np.log(1.0 - 2.0 ** (-5.0 - np.arange(RET_HEADS, dtype=np.float64)))
    idx = np.arange(chunk, dtype=np.float64)
    rel = idx[:, None] - idx[None, :]
    causal = rel >= 0
    din = np.where(causal[None], np.exp(np.where(causal, rel, 0.0)[None] * lg[:, None, None]), 0.0)
    dq = np.exp((idx + 1.0)[None, :, None] * lg[:, None, None])
    dk = np.exp((chunk - 1.0 - idx)[None, :, None] * lg[:, None, None])
    dc = tuple(float(v) for v in np.exp(chunk * lg))
    return (jnp.asarray(din, F32), jnp.asarray(dq, F32), jnp.asarray(dk, F32), dc)


def _ssd_decays(dt_ref, dtb_ref, alog_ref, tril_ref, exp_ref, n_sub, chunk):
    dt = jax.nn.softplus(dt_ref[...] + dtb_ref[...])
    adt = dt * -jnp.exp(alog_ref[...])
    acs = [jnp.dot(tril_ref[...], adt[s * chunk:(s + 1) * chunk, :], preferred_element_type=F32,
                   precision=HIGHEST) * LOG2_E for s in range(n_sub)]
    dt_x = jnp.dot(dt.astype(BF16), exp_ref[...], preferred_element_type=F32)
    return acs, dt_x


def _ssd_body(z_ref, xbc_ref, acs, dt_x, dsk_ref, nw_ref, yn_s, state, chunk, d_inner):
    heads_per_group = d_inner // SSM_HEAD_DIM // SSM_GROUPS
    gw = heads_per_group * SSM_HEAD_DIM
    assert SSM_HEAD_DIM * 2 == LANES and gw == 2 * LANES

    acs_t = acs.T
    li = lax.broadcasted_iota(jnp.int32, (chunk, chunk), 0)
    si = lax.broadcasted_iota(jnp.int32, (chunk, chunk), 1)
    causal = li >= si
    low_half = si < SSM_HEAD_DIM
    lane_g = lax.broadcasted_iota(jnp.int32, (chunk, gw), 1)

    b_off = d_inner
    c_off = d_inner + SSM_GROUPS * SSM_STATE
    for g in range(SSM_GROUPS):
        bm = xbc_ref[:, b_off + g * SSM_STATE: b_off + (g + 1) * SSM_STATE]
        cm = xbc_ref[:, c_off + g * SSM_STATE: c_off + (g + 1) * SSM_STATE]
        xs_g = xbc_ref[:, g * gw:(g + 1) * gw].astype(F32)
        xdt_g = xs_g * dt_x[:, g * gw:(g + 1) * gw]
        xdt_b = xdt_g.astype(BF16)
        cb = _nt_dot(cm, bm)
        cols, ms, xm = [], [], []
        for jh in range(heads_per_group):
            h = g * heads_per_group + jh
            col = jnp.broadcast_to(acs[:, h:h + 1], (chunk, chunk))
            seg = jnp.exp2(jnp.where(causal, col - acs_t[h:h + 1, :], -jnp.inf))
            cols.append(col)
            ms.append((cb * seg).astype(BF16))
            in_head = (lane_g >= jh * SSM_HEAD_DIM) & (lane_g < (jh + 1) * SSM_HEAD_DIM)
            xm.append(jnp.where(in_head, xdt_b, jnp.zeros_like(xdt_b)))
        y_diag = jnp.dot(jnp.concatenate(ms, axis=-1), jnp.concatenate(xm, axis=0),
                         preferred_element_type=F32)
        a_x = jnp.concatenate([jnp.where(low_half, cols[0], cols[1]),
                               jnp.where(low_half, cols[2], cols[3])], axis=-1)
        e_acs_x = jnp.exp2(a_x)
        a_last_x = a_x[chunk - 1:chunk, :]
        st = state[g]
        y_off = jnp.dot(cm, st.astype(BF16), preferred_element_type=F32) * e_acs_x
        xdec = (xdt_g * jnp.exp2(a_last_x - a_x)).astype(BF16)
        state[g] = st * e_acs_x[chunk - 1:chunk, :] + _tn_dot(bm, xdec)
        y = y_diag + y_off + dsk_ref[:, g * gw:(g + 1) * gw] * xs_g
        yz = y * z_ref[:, g * gw:(g + 1) * gw].astype(F32)
        yn = yz * lax.rsqrt(jnp.mean(yz * yz, axis=-1, keepdims=True) + EPS) * nw_ref[:, g * gw:(g + 1) * gw]
        yn_s[:, g * gw:(g + 1) * gw] = yn.astype(BF16)


def _mixers_kernel(q_ref, k_ref, v_ref, g_ref, din_ref, dq_ref, dk_ref, wret_ref,
                   z_ref, xbc_ref, dt_ref, dtb_ref, alog_ref, dsk_ref, nw_ref, tril_ref, exp_ref, wssm_ref,
                   ga_ref, gb_ref, x_ref, gm_ref, scf_ref, shf_ref, nf_ref, wo_ref, wr_ref, br_ref, tri_ref,
                   x1_ref, h2_ref, idx_ref, rank_ref, prow_ref, cnt_ref,
                   rstate, sstate, yn_s, ya_s, yb_s, cnt_s, *, decay_c, chunk, ssm_chunk, d_inner):
    b = pl.program_id(0)
    c = pl.program_id(1)

    @pl.when(c == 0)
    def _():
        rstate[...] = jnp.zeros_like(rstate)
        sstate[...] = jnp.zeros_like(sstate)

    @pl.when((b == 0) & (c == 0))
    def _():
        cnt_s[...] = jnp.zeros_like(cnt_s)

    n_sub = chunk // ssm_chunk
    acs, dt_x = _ssd_decays(dt_ref, dtb_ref, alog_ref, tril_ref, exp_ref, n_sub, ssm_chunk)
    _retention_body(q_ref, k_ref, v_ref, g_ref, din_ref, dq_ref, dk_ref, wret_ref, ya_s, rstate, decay_c)
    for sub in range(n_sub):
        rows = pl.ds(sub * ssm_chunk, ssm_chunk)
        _ssd_body(z_ref.at[rows, :], xbc_ref.at[rows, :], acs[sub],
                  dt_x[sub * ssm_chunk:(sub + 1) * ssm_chunk, :], dsk_ref, nw_ref, yn_s.at[rows, :],
                  sstate, ssm_chunk, d_inner)
    yb_s[...] = jnp.dot(yn_s[...], wssm_ref[...], preferred_element_type=F32)
    _merge_body(ya_s, yb_s, ga_ref, gb_ref, x_ref, gm_ref, scf_ref, shf_ref, nf_ref, wo_ref, wr_ref, br_ref,
                tri_ref, x1_ref, h2_ref, idx_ref, rank_ref, prow_ref, cnt_ref, cnt_s, chunk)


def _mixers_call(proj, dt_raw, x2, row0, mod3, w_ret, dt_bias, a_log, d_skip, ssm_norm, w_ssm,
                 norm_ffn, w_out, w_router_t, b_router, bsz, seq, chunk, ssm_chunk):
    t = proj.shape[0]
    d_inner, d = w_ssm.shape
    conv_dim = d_inner + 2 * SSM_GROUPS * SSM_STATE
    n_heads = d_inner // SSM_HEAD_DIM
    nc = seq // chunk
    gw = d_inner // SSM_GROUPS
    qk_w = RET_HEADS * RET_QK_DIM
    v_w = RET_HEADS * RET_V_DIM
    assert chunk % ssm_chunk == 0
    assert ssm_chunk == LANES, "the per-head decay tiles are built lane-for-lane against the chunk"
    din, dq, dk, dc = _retention_tables(chunk)
    pad_h = lambda v: jnp.pad(v.astype(F32), (0, LANES - n_heads)).reshape(1, LANES)
    tril = jnp.asarray(np.tril(np.ones((ssm_chunk, ssm_chunk), np.float32)))
    expand = np.zeros((LANES, d_inner), np.float32)
    for h in range(n_heads):
        expand[h, h * SSM_HEAD_DIM:(h + 1) * SSM_HEAD_DIM] = 1.0
    expand = jnp.asarray(expand, BF16)
    kern = functools.partial(_mixers_kernel, decay_c=dc, chunk=chunk, ssm_chunk=ssm_chunk, d_inner=d_inner)
    row = lambda b, c: b * nc + c
    z_blk = (2 * qk_w + 2 * v_w) // d_inner
    xbc_blk = (2 * qk_w + 2 * v_w + d_inner) // conv_dim
    full = lambda shape: pl.BlockSpec(shape, lambda b, c: (0,) * len(shape))
    nt = t // chunk
    ga_blk = proj.shape[1] // d - 2
    seq0 = row0 // seq
    modspec = lambda m: pl.BlockSpec((1, 1, d), lambda b, c: ((b + seq0) * N_MOD + m, 0, 0))
    tri = jnp.asarray(np.triu(np.ones((chunk, chunk), np.float32), 1), BF16)
    return pl.pallas_call(
        kern,
        out_shape=(jax.ShapeDtypeStruct((t, d), F32), jax.ShapeDtypeStruct((t * SLAB_ROWS, LANES), jnp.uint32),
                   jax.ShapeDtypeStruct((nt, SUBLANES, chunk), jnp.int32),
                   jax.ShapeDtypeStruct((nt, SUBLANES, chunk), jnp.int32),
                   jax.ShapeDtypeStruct((t, LANES), F32),
                   jax.ShapeDtypeStruct((N_EXPERTS, LANES), jnp.int32)),
        grid=(bsz, nc),
        in_specs=[
            pl.BlockSpec((chunk, qk_w), lambda b, c: (row(b, c), 0)),
            pl.BlockSpec((chunk, qk_w), lambda b, c: (row(b, c), 1)),
            pl.BlockSpec((chunk, v_w), lambda b, c: (row(b, c), 1)),
            pl.BlockSpec((chunk, v_w), lambda b, c: (row(b, c), 2)),
            full((RET_HEADS, chunk, chunk)), full((RET_HEADS, chunk, 1)), full((RET_HEADS, chunk, 1)),
            full((v_w, d)),
            pl.BlockSpec((chunk, d_inner), lambda b, c: (row(b, c), z_blk)),
            pl.BlockSpec((chunk, conv_dim), lambda b, c: (row(b, c), xbc_blk)),
            pl.BlockSpec((chunk, LANES), lambda b, c: (row(b, c), 0)),
            full((1, LANES)), full((1, LANES)),
            full((1, d_inner)), full((1, d_inner)), full((ssm_chunk, ssm_chunk)), full((LANES, d_inner)),
            full((d_inner, d)),
            pl.BlockSpec((chunk, d), lambda b, c: (row(b, c), ga_blk)),
            pl.BlockSpec((chunk, d), lambda b, c: (row(b, c), ga_blk + 1)),
            pl.BlockSpec((chunk, d), lambda b, c: (row0 // chunk + row(b, c), 0)),
            modspec(2), modspec(4), modspec(3),
            full((1, d)), full((d, d)), full((2 * N_EXPERTS, d)), full((N_EXPERTS, 1)), full((chunk, chunk)),
        ],
        out_specs=(pl.BlockSpec((chunk, d), lambda b, c: (row(b, c), 0)),
                   pl.BlockSpec((chunk * SLAB_ROWS, LANES), lambda b, c: (row(b, c), 0)),
                   pl.BlockSpec((1, SUBLANES, chunk), lambda b, c: (row(b, c), 0, 0)),
                   pl.BlockSpec((1, SUBLANES, chunk), lambda b, c: (row(b, c), 0, 0)),
                   pl.BlockSpec((chunk, LANES), lambda b, c: (row(b, c), 0)),
                   full((N_EXPERTS, LANES))),
        scratch_shapes=[pltpu.VMEM((RET_HEADS, RET_QK_DIM, RET_V_DIM), F32),
                        pltpu.VMEM((SSM_GROUPS, SSM_STATE, gw), F32),
                        pltpu.VMEM((chunk, d_inner), BF16),
                        pltpu.VMEM((chunk, d), F32), pltpu.VMEM((chunk, d), F32),
                        pltpu.VMEM((N_EXPERTS, LANES), F32)],
        compiler_params=_params(("arbitrary", "arbitrary")),
        name="mixers",
    )(proj, proj, proj, proj, din, dq, dk, w_ret,
      proj, proj, dt_raw, pad_h(dt_bias), pad_h(a_log),
      jnp.repeat(d_skip.astype(F32), SSM_HEAD_DIM).reshape(1, d_inner), ssm_norm.reshape(1, d_inner),
      tril, expand, w_ssm,
      proj, proj, x2, mod3, mod3, mod3, norm_ffn, w_out, w_router_t, b_router.reshape(N_EXPERTS, 1), tri)


def _merge_body(ya_ref, yb_ref, ga_ref, gb_ref, x_ref, gm_ref, scf_ref, shf_ref, nw_ref, wo_ref,
                wr_ref, br_ref, tri_ref,
                x1_ref, h2_ref, idx_ref, rank_ref, prow_ref, cnt_ref, cnt_s, tm):
    merged = (ga_ref[...].astype(F32) * ya_ref[...].astype(F32)
              + gb_ref[...].astype(F32) * yb_ref[...].astype(F32))
    mo = jnp.dot(merged.astype(BF16), wo_ref[...], preferred_element_type=F32)
    x1 = x_ref[...] + gm_ref[0] * mo
    x1_ref[...] = x1
    ms = jnp.mean(x1 * x1, axis=-1, keepdims=True)
    h2 = x1 * lax.rsqrt(ms + EPS) * nw_ref[...] * (1.0 + scf_ref[0]) + shf_ref[0]
    _store_slabs(h2_ref, h2, tm)

    h_hi = h2.astype(BF16)
    h_lo = (h2 - h_hi.astype(F32)).astype(BF16)
    lg2 = _nt_dot(wr_ref[...], h_hi)
    lg = lg2[:N_EXPERTS] + lg2[N_EXPERTS:] + _nt_dot(wr_ref[:N_EXPERTS, :], h_lo) + br_ref[...]
    sub = lax.broadcasted_iota(jnp.int32, lg.shape, 0)
    work = lg
    vals, idxs, sels = [], [], []
    for _ in range(TOP_K):
        m = jnp.max(work, axis=0, keepdims=True)
        ik = jnp.min(jnp.where(work == m, sub, N_EXPERTS), axis=0, keepdims=True)
        sel = sub == ik
        vals.append(m)
        idxs.append(ik)
        sels.append(sel)
        work = jnp.where(sel, -jnp.inf, work)
    exps = [jnp.exp(v - vals[0]) for v in vals]
    denom = exps[0]
    for e in exps[1:]:
        denom = denom + e
    probs = [e / denom for e in exps]

    base = cnt_s[:, 0:1]
    ranks = []
    for k in range(TOP_K):
        mk = jnp.where(sels[k], 1.0, 0.0)
        pre = jnp.dot(mk.astype(BF16), tri_ref[...], preferred_element_type=F32)
        ranks.append(jnp.sum(jnp.where(sels[k], pre + base, 0.0), axis=0, keepdims=True))
        base = base + jnp.sum(mk, axis=1, keepdims=True)
    cnt_s[...] = jnp.broadcast_to(base, cnt_s.shape)
    cnt_ref[...] = cnt_s[...].astype(jnp.int32)

    zi = jnp.zeros((SUBLANES - TOP_K, tm), jnp.int32)
    idx_ref[0] = jnp.concatenate(idxs + [zi], axis=0)
    rank_ref[0] = jnp.concatenate([r.astype(jnp.int32) for r in ranks] + [zi], axis=0)
    pt = jnp.concatenate(probs + [jnp.zeros((LANES - TOP_K, tm), F32)], axis=0)
    prow_ref[...] = pt.T


def _sc_mesh():
    return plsc.VectorSubcoreMesh(core_axis_name="c", subcore_axis_name="s")


def _sc_worker():
    return lax.axis_index("s") * SC_CORES + lax.axis_index("c")


def _sc_scatter_rows(rows, dest, n_out):
    t = rows.shape[0]
    n_k = dest.shape[0]
    g = SC_GROUP
    n_w = SC_CORES * SC_SUBCORES
    assert t % (n_w * g) == 0
    cpw = t // (n_w * g)
    dest_w = dest.reshape(n_k, n_w, cpw, g).transpose(1, 0, 2, 3)

    @functools.partial(
        pl.kernel, mesh=_sc_mesh(),
        out_type=jax.ShapeDtypeStruct((n_out,) + rows.shape[1:], rows.dtype),
        scratch_types=[pltpu.VMEM((n_k, cpw, g), jnp.int32),
                       pltpu.VMEM((g,) + rows.shape[1:], rows.dtype),
                       pltpu.SemaphoreType.DMA],
    )
    def scatter(rows_hbm, dest_hbm, out_hbm, idx_v, rows_v, sem):
        wid = _sc_worker()
        pltpu.sync_copy(dest_hbm.at[wid], idx_v)

        @pl.loop(0, cpw)
        def _(cc):
            r0 = pl.multiple_of((wid * cpw + cc) * g, g)
            pltpu.sync_copy(rows_hbm.at[pl.ds(r0, g)], rows_v)
            copies = [pltpu.async_copy(rows_v, out_hbm.at[idx_v.at[k, cc]], sem) for k in range(n_k)]
            for cp in copies:
                cp.wait()

    return scatter(rows, dest_w)


def _sc_gather_rows(table, idx):
    m = idx.shape[0]
    g = SC_GROUP
    n_w = SC_CORES * SC_SUBCORES
    assert m % (n_w * g) == 0
    per_w = m // n_w

    @functools.partial(
        pl.kernel, mesh=_sc_mesh(),
        out_type=jax.ShapeDtypeStruct((m,) + table.shape[1:], table.dtype),
        scratch_types=[pltpu.VMEM((per_w,), jnp.int32),
                       pltpu.VMEM((g,) + table.shape[1:], table.dtype),
                       pltpu.SemaphoreType.DMA],
    )
    def gather(table_hbm, idx_hbm, out_hbm, idx_v, rows_v, sem):
        base = _sc_worker() * per_w
        pltpu.sync_copy(idx_hbm.at[pl.ds(base, per_w)], idx_v)

        @pl.loop(0, per_w // g)
        def _(cc):
            off = pl.multiple_of(cc * g, g)
            pltpu.async_copy(table_hbm.at[idx_v.at[pl.ds(off, g)]], rows_v, sem).wait()
            pltpu.sync_copy(rows_v, out_hbm.at[pl.ds(base + off, g)])

    return gather(table, idx)


def _ffn_kernel(be_ref, br_ref, bv_ref, first_ref, slot_ref, next_ref,
                x_ref, wgu_hbm, bgu_ref, wd_hbm, bd_ref, o_ref, wgu_f, wd_f, wgu_s, wd_s, sem, *, bm, d_ff):
    i = pl.program_id(0)

    def fetch(e, slot):
        return (pltpu.make_async_copy(wgu_hbm.at[e], wgu_f.at[slot], sem.at[0, slot]),
                pltpu.make_async_copy(wd_hbm.at[e], wd_f.at[slot], sem.at[1, slot]))

    @pl.when(i == 0)
    def _():
        for cp in fetch(be_ref[0], slot_ref[0]):
            cp.start()

    @pl.when(first_ref[i] == 1)
    def _():
        slot = slot_ref[i]
        for cp in fetch(be_ref[i], slot):
            cp.wait()
        wgu_s[...] = wgu_f[slot].astype(BF16)
        wd_s[...] = wd_f[slot].astype(BF16)

        @pl.when(next_ref[i] >= 0)
        def _():
            for cp in fetch(next_ref[i], 1 - slot):
                cp.start()

    @pl.when(bv_ref[i] == 1)
    def _():
        x = jnp.concatenate(_load_slabs(x_ref, bm), axis=-1).astype(BF16)
        gu = jnp.dot(x, wgu_s[...], preferred_element_type=F32) + bgu_ref[0]
        gate = jnp.minimum(gu[:, :d_ff], SWIGLU_LIMIT)
        up = jnp.clip(gu[:, d_ff:], -SWIGLU_LIMIT, SWIGLU_LIMIT)
        act = gate * jax.nn.sigmoid(SWIGLU_ALPHA * gate) * (up + 1.0)
        y = jnp.dot(act.astype(BF16), wd_s[...], preferred_element_type=F32) + bd_ref[0]
        _store_slabs(o_ref, y, bm)

    @pl.when(bv_ref[i] == 0)
    def _():
        o_ref[...] = jnp.zeros_like(o_ref)


def _ffn_call(blk_e, blk_row, blk_valid, xs, w_gu, b_gu, w_d, b_d, bm):
    n_e, d, f2 = w_gu.shape
    d_ff = f2 // 2
    nb = blk_e.shape[0]
    first = jnp.concatenate([jnp.ones((1,), jnp.int32), (blk_e[1:] != blk_e[:-1]).astype(jnp.int32)])
    slot = (jnp.cumsum(first) - 1) % 2
    later = jnp.where(blk_e[None, :] > blk_e[:, None], blk_e[None, :], n_e)
    nxt = jnp.min(later, axis=1)
    nxt = jnp.where(nxt == n_e, -1, nxt)
    kern = functools.partial(_ffn_kernel, bm=bm, d_ff=d_ff)
    imap = lambda f: (lambda i, be, br, bv, fi, sl, nx: f(i, be, br))
    gs = pltpu.PrefetchScalarGridSpec(
        num_scalar_prefetch=6,
        grid=(nb,),
        in_specs=[pl.BlockSpec((bm * SLAB_ROWS, LANES), imap(lambda i, be, br: (br[i], 0))),
                  pl.BlockSpec(memory_space=pl.ANY),
                  pl.BlockSpec((1, 1, f2), imap(lambda i, be, br: (be[i], 0, 0))),
                  pl.BlockSpec(memory_space=pl.ANY),
                  pl.BlockSpec((1, 1, d), imap(lambda i, be, br: (be[i], 0, 0)))],
        out_specs=pl.BlockSpec((bm * SLAB_ROWS, LANES), imap(lambda i, be, br: (i, 0))),
        scratch_shapes=[pltpu.VMEM((2, d, f2), F32), pltpu.VMEM((2, d_ff, d), F32),
                        pltpu.VMEM((d, f2), BF16), pltpu.VMEM((d_ff, d), BF16),
                        pltpu.SemaphoreType.DMA((2, 2))],
    )
    return pl.pallas_call(
        kern,
        out_shape=jax.ShapeDtypeStruct((nb * bm * SLAB_ROWS, LANES), jnp.uint32),
        grid_spec=gs,
        compiler_params=_params(("arbitrary",)),
        name="ffn",
    )(blk_e, blk_row, blk_valid, first, slot.astype(jnp.int32), nxt.astype(jnp.int32),
      xs, w_gu, b_gu.reshape(n_e, 1, f2), w_d, b_d.reshape(n_e, 1, d))


def _combine_kernel(y0_ref, y1_ref, y2_ref, y3_ref, prow_ref, x1_ref, gf_ref, nw_ref, *rest, tm):
    o_ref = rest[-1]
    p = prow_ref[...]
    pieces = [_load_slabs(y_ref, tm) for y_ref in (y0_ref, y1_ref, y2_ref, y3_ref)]
    for s in range(len(pieces[0])):
        moe = None
        for k in range(TOP_K):
            piece = pieces[k][s] * p[:, k:k + 1]
            moe = piece if moe is None else moe + piece
        sl = slice(s * LANES, (s + 1) * LANES)
        o_ref[:, sl] = x1_ref[:, sl] + gf_ref[0][:, sl] * moe
    xo = o_ref[...]
    o_ref[...] = xo * lax.rsqrt(jnp.mean(xo * xo, axis=-1, keepdims=True) + EPS) * nw_ref[...]


def _combine_call(ytok, prow, x1, mod3, norm_final, seq, tm, row0, t_total, out_prev):
    t, d = x1.shape
    nt = t // tm
    tiles_per_seq = seq // tm
    off = row0 // tm
    kern = functools.partial(_combine_kernel, tm=tm)
    yspec = lambda k: pl.BlockSpec((tm * SLAB_ROWS, LANES), lambda i: (k * nt + i, 0))
    in_specs = [yspec(0), yspec(1), yspec(2), yspec(3),
                pl.BlockSpec((tm, LANES), lambda i: (i, 0)),
                pl.BlockSpec((tm, d), lambda i: (i, 0)),
                pl.BlockSpec((1, 1, d), lambda i: (((i + off) // tiles_per_seq) * N_MOD + 5, 0, 0)),
                pl.BlockSpec((1, d), lambda i: (0, 0))]
    args = [ytok, ytok, ytok, ytok, prow, x1, mod3, norm_final]
    aliases = {}
    if out_prev is not None:
        in_specs.append(pl.BlockSpec(memory_space=pl.ANY))
        aliases = {len(args): 0}
        args.append(out_prev)
    return pl.pallas_call(
        kern,
        out_shape=jax.ShapeDtypeStruct((t_total, d), F32),
        grid=(nt,),
        in_specs=in_specs,
        out_specs=pl.BlockSpec((tm, d), lambda i: (i + off, 0)),
        input_output_aliases=aliases,
        compiler_params=_params(("arbitrary",)),
        name="combine",
    )(*args)


def _plan(seq):
    def fit(pref):
        tm = min(pref, seq)
        assert seq % tm == 0
        return tm
    return dict(tm_in=fit(1024), tm_merge=fit(512), tm_moe=fit(512),
                ret_chunk=fit(RET_CHUNK), ssm_chunk=fit(SSM_CHUNK))


def _layer(x2, mod3, bsz, seq, norm_mix, norm_ffn, w_in, conv_w, conv_b, dt_bias, a_log, d_skip, ssm_norm,
           w_ret_out, w_ssm_out, w_out, w_router, b_router, w_gate_up, b_gate_up, w_down, b_down,
           norm_final):
    t, d = x2.shape
    plan = _plan(seq)
    qk_w = RET_HEADS * RET_QK_DIM
    v_w = RET_HEADS * RET_V_DIM
    d_inner = w_ssm_out.shape[0]
    conv_dim = conv_w.shape[1]
    n_heads = d_inner // SSM_HEAD_DIM
    dt_off = 2 * qk_w + 2 * v_w + d_inner + conv_dim

    w_all = w_in.astype(BF16)
    w_gate = w_in[:, dt_off + n_heads:].astype(BF16)
    w_dt = jnp.pad(w_in[:, dt_off:dt_off + n_heads], ((0, 0), (0, LANES - n_heads)))
    w_dt_hi = w_dt.astype(BF16)
    w_dt = jnp.concatenate([w_dt_hi, (w_dt - w_dt_hi.astype(F32)).astype(BF16)], axis=1)
    half = RET_QK_DIM // 2
    inv_freq = ROPE_BASE ** (-jnp.arange(half, dtype=F32) / half)
    ang = jnp.arange(seq, dtype=F32)[:, None] * inv_freq[None, :]
    cos, sin = jnp.cos(ang), jnp.sin(ang)

    w_ret_b, w_ssm_b, w_out_b = w_ret_out.astype(BF16), w_ssm_out.astype(BF16), w_out.astype(BF16)
    w_r_hi = w_router.T.astype(BF16)
    w_router_t = jnp.concatenate([w_r_hi, (w_router.T - w_r_hi.astype(F32)).astype(BF16)], axis=0)
    bm = FFN_BLOCK
    slab = (SLAB_ROWS, LANES)

    def mixer(row0, tg, bg):
        proj, dt_raw = _inproj_call(x2, row0, tg, norm_mix.reshape(1, d), mod3, cos, sin, w_all, w_gate, w_dt,
                                    conv_w, conv_b, 2 * qk_w + 2 * v_w + d_inner, seq, plan["tm_in"], 2 * qk_w)
        x1, h2, idx, rank, prow, cnt = _mixers_call(
            proj, dt_raw, x2, row0, mod3, w_ret_b, dt_bias, a_log, d_skip, ssm_norm, w_ssm_b,
            norm_ffn.reshape(1, d), w_out_b, w_router_t, b_router, bg, seq, plan["ret_chunk"], plan["ssm_chunk"])
        counts = cnt[:, 0]
        padded = ((counts + bm - 1) // bm) * bm
        pad_end = jnp.cumsum(padded)
        start_pad = pad_end - padded
        n_blocks = -(-(tg * TOP_K) // bm) + N_EXPERTS
        e_ids = jnp.arange(N_EXPERTS, dtype=jnp.int32)[:, None, None, None]
        dest = rank + jnp.sum(jnp.where(idx[None] == e_ids, start_pad[:, None, None, None], 0), axis=0)
        dest = dest[:, :TOP_K, :].transpose(1, 0, 2).reshape(TOP_K, tg).astype(jnp.int32)
        n_real = pad_end[-1] // bm
        blk_valid = (jnp.arange(n_blocks) < n_real).astype(jnp.int32)
        blk_row = jnp.minimum(jnp.arange(n_blocks), n_real - 1).astype(jnp.int32)
        blk_e = jnp.minimum(jnp.sum(pad_end[None, :] <= (blk_row * bm)[:, None], axis=1),
                            N_EXPERTS - 1).astype(jnp.int32)
        xs = _sc_scatter_rows(h2.reshape((tg,) + slab), dest, n_blocks * bm)
        return dict(x1=x1, prow=prow, dest=dest, blocks=(blk_e, blk_row, blk_valid), xs=xs, row0=row0)

    def experts(m):
        n_rows = m["xs"].shape[0]
        ys = _ffn_call(*m["blocks"], m["xs"].reshape(n_rows * SLAB_ROWS, LANES),
                       w_gate_up, b_gate_up, w_down, b_down, bm)
        return _sc_gather_rows(ys.reshape((n_rows,) + slab), m["dest"].reshape(-1))

    n_groups = N_GROUPS if bsz % N_GROUPS == 0 else 1
    bg = bsz // n_groups
    tg = bg * seq
    groups = [mixer(g * tg, tg, bg) for g in range(n_groups)]
    ytoks = [experts(m) for m in groups]
    out = None
    for y, m in zip(ytoks, groups):
        out = _combine_call(y.reshape(TOP_K * tg * SLAB_ROWS, LANES), m["prow"], m["x1"], mod3,
                            norm_final.reshape(1, d), seq, plan["tm_moe"], m["row0"], t, out)
    return out


def kernel(x, c, w_ada, b_ada, norm_mix, norm_ffn, w_in, conv_w, conv_b, dt_bias, a_log, d_skip, ssm_norm,
           w_ret_out, w_ssm_out, w_out, w_router, b_router, w_gate_up, b_gate_up, w_down, b_down, norm_final):
    bsz, seq, d = x.shape
    depth = w_ada.shape[0]
    assert depth == 1, "the final norm is fused into the single layer's last kernel"
    x2 = x.reshape(bsz * seq, d)
    l = 0
    mod = _mod_call(c, w_ada[l], b_ada[l])
    mod3 = mod.reshape(bsz * N_MOD, 1, d)
    out = _layer(x2, mod3, bsz, seq, norm_mix[l], norm_ffn[l], w_in[l], conv_w[l], conv_b[l], dt_bias[l],
                 a_log[l], d_skip[l], ssm_norm[l], w_ret_out[l], w_ssm_out[l], w_out[l], w_router[l],
                 b_router[l], w_gate_up[l], b_gate_up[l], w_down[l], b_down[l], norm_final)
    return out.reshape(bsz, seq, d)
```

```python
import functools
import math

import numpy as np
import jax
import jax.numpy as jnp
from jax import lax
from jax.experimental import pallas as pl
from jax.experimental.pallas import tpu as pltpu
from jax.experimental.pallas import tpu_sc as plsc

F32 = jnp.float32
BF16 = jnp.bfloat16
HIGHEST = lax.Precision.HIGHEST

EPS = 1e-6
N_MOD = 6
RET_HEADS = 4
RET_QK_DIM = 256
RET_V_DIM = 512
ROPE_BASE = 10000.0
SSM_HEAD_DIM = 64
SSM_GROUPS = 8
SSM_STATE = 128
SSM_CONV = 4
N_EXPERTS = 32
TOP_K = 4
SWIGLU_LIMIT = 7.0
SWIGLU_ALPHA = 1.702

LANES = 128
SUBLANES = 8
VMEM_LIMIT = 56 * 1024 * 1024

RET_CHUNK = 256
SSM_CHUNK = 128
FFN_BLOCK = 512
SC_CORES = 2
SC_SUBCORES = 16
SC_GROUP = 64
N_GROUPS = 2


def _params(sem, vmem=VMEM_LIMIT):
    return pltpu.CompilerParams(dimension_semantics=sem, vmem_limit_bytes=vmem)


def _nt_dot(a, b, **kw):
    return lax.dot_general(a, b, (((1,), (1,)), ((), ())), preferred_element_type=F32, **kw)


def _tn_dot(a, b, **kw):
    return lax.dot_general(a, b, (((0,), (0,)), ((), ())), preferred_element_type=F32, **kw)


def _silu(v):
    return v * jax.nn.sigmoid(v)


SLAB_ROWS = 4
HIGH_HALF = 0xFFFF0000
LOG2_E = math.log2(math.e)


def _store_slabs(ref, vals, n):
    for s in range(SLAB_ROWS):
        lo = vals[:, s * LANES:(s + 1) * LANES].astype(BF16).astype(F32)
        hi = vals[:, (s + SLAB_ROWS) * LANES:(s + SLAB_ROWS + 1) * LANES].astype(BF16).astype(F32)
        word = (pltpu.bitcast(lo, jnp.uint32) >> 16) | (pltpu.bitcast(hi, jnp.uint32) & jnp.uint32(HIGH_HALF))
        ref[pl.ds(s, n, stride=SLAB_ROWS), :] = word


def _load_slabs(ref, n, base=0):
    lo, hi = [], []
    for s in range(SLAB_ROWS):
        word = ref[pl.ds(base + s, n, stride=SLAB_ROWS), :]
        lo.append(pltpu.bitcast(word << 16, F32))
        hi.append(pltpu.bitcast(word & jnp.uint32(HIGH_HALF), F32))
    return lo + hi


def _mod_kernel(c_ref, w_ref, b_ref, o_ref):
    cond = _silu(c_ref[...])
    o_ref[...] = jnp.dot(cond, w_ref[...], preferred_element_type=F32, precision=HIGHEST) + b_ref[...]


def _mod_call(c, w_ada, b_ada):
    bsz, d = c.shape
    n = w_ada.shape[1]
    return pl.pallas_call(
        _mod_kernel,
        out_shape=jax.ShapeDtypeStruct((bsz, n), F32),
        grid=(n // d,),
        in_specs=[pl.BlockSpec((bsz, d), lambda j: (0, 0)),
                  pl.BlockSpec((d, d), lambda j: (0, j)),
                  pl.BlockSpec((1, d), lambda j: (0, j))],
        out_specs=pl.BlockSpec((bsz, d), lambda j: (0, j)),
        compiler_params=_params(("arbitrary",)),
        name="mod",
    )(c, w_ada, b_ada.reshape(1, n))


def _inproj_kernel(x_ref, nw_ref, sc_ref, sh_ref, cos_ref, sin_ref, w_ref, wg_ref, wdt_ref, cw_ref, cb_ref,
                   o_ref, dt_ref, h_s, work, carry, *, conv_j0, conv_nj, silu_j, sigm_j, tiles_per_seq,
                   tm, tn, sub):
    i = pl.program_id(0)
    j = pl.program_id(1)
    n_dt = dt_ref.shape[1]
    rows = min(tm, 256)

    @pl.when(j == 0)
    def _():
        xf = x_ref[...]
        ms = jnp.mean(xf * xf, axis=-1, keepdims=True)
        y = xf * lax.rsqrt(ms + EPS) * nw_ref[...]
        hm = y * (1.0 + sc_ref[0]) + sh_ref[0]
        hb = hm.astype(BF16)
        h_s[...] = hb
        h_lo = (hm - hb.astype(F32)).astype(BF16)
        d_hi = jnp.dot(hb, wdt_ref[...], preferred_element_type=F32)
        d_lo = jnp.dot(h_lo, wdt_ref[:, :n_dt], preferred_element_type=F32)
        dt_ref[...] = d_hi[:, :n_dt] + d_hi[:, n_dt:] + d_lo
        half = RET_QK_DIM // 2
        for p in range(tn // sub):
            for r in range(tm // rows):
                rs = slice(r * rows, (r + 1) * rows)
                acc = jnp.dot(h_s[rs, :], w_ref[:, p * sub:(p + 1) * sub], preferred_element_type=F32)
                cos = cos_ref[rs, :]
                sin = sin_ref[rs, :]
                for cc in range(sub // RET_QK_DIM):
                    c = p * (sub // RET_QK_DIM) + cc
                    a = acc[:, cc * RET_QK_DIM: cc * RET_QK_DIM + half]
                    b = acc[:, cc * RET_QK_DIM + half: (cc + 1) * RET_QK_DIM]
                    scale = 1.0 if c < RET_HEADS else RET_QK_DIM ** -0.5
                    o_ref[rs, c * RET_QK_DIM: c * RET_QK_DIM + half] = ((a * cos - b * sin) * scale).astype(BF16)
                    o_ref[rs, c * RET_QK_DIM + half: (c + 1) * RET_QK_DIM] = (
                        (a * sin + b * cos) * scale).astype(BF16)

    is_conv = (j >= conv_j0) & (j < conv_j0 + conv_nj)

    @pl.when(is_conv)
    def _():
        cj = j - conv_j0
        pad = SUBLANES

        @pl.when(i % tiles_per_seq == 0)
        def _():
            carry[cj] = jnp.zeros(carry.shape[1:], F32)

        for p in range(tn // sub):
            for r in range(tm // rows):
                r0 = r * rows
                acc = jnp.dot(h_s[r0:r0 + rows, :], w_ref[:, p * sub:(p + 1) * sub], preferred_element_type=F32)
                for cc in range(sub // LANES):
                    c = p * (sub // LANES) + cc
                    cols = slice(c * LANES, (c + 1) * LANES)
                    if r == 0:
                        work[c, 0:pad, :] = carry[cj, c]
                    lo = pad + r0
                    work[c, lo:lo + rows, :] = acc[:, cc * LANES:(cc + 1) * LANES]
                    conv = cb_ref[:, cols] + cw_ref[SSM_CONV - 1:SSM_CONV, cols] * work[c, lo:lo + rows, :]
                    for k in range(SSM_CONV - 1):
                        shift = SSM_CONV - 1 - k
                        conv = conv + cw_ref[k:k + 1, cols] * work[c, lo - shift:lo - shift + rows, :]
                    if r0 + rows == tm:
                        carry[cj, c] = work[c, tm:tm + pad, :]
                    o_ref[r0:r0 + rows, cols] = _silu(conv).astype(BF16)

    def plain(act, weights):
        for p in range(tn // sub):
            for r in range(tm // rows):
                acc = jnp.dot(h_s[r * rows:(r + 1) * rows, :], weights[:, p * sub:(p + 1) * sub],
                              preferred_element_type=F32)
                o_ref[r * rows:(r + 1) * rows, p * sub:(p + 1) * sub] = act(acc).astype(BF16)

    is_silu = (j >= silu_j[0]) & (j < silu_j[1])
    is_sigm = (j >= sigm_j[0]) & (j < sigm_j[1])
    pl.when(is_silu)(lambda: plain(_silu, w_ref))
    pl.when(is_sigm)(lambda: plain(jax.nn.sigmoid, wg_ref))
    pl.when((j != 0) & jnp.logical_not(is_conv | is_silu | is_sigm))(lambda: plain(lambda v: v, w_ref))


def _inproj_call(x2, row0, t, norm_w, mod3, cos, sin, w_all, w_gate, w_dt, conv_w, conv_b, conv_off, seq, tm, tn):
    d = x2.shape[1]
    conv_dim = conv_w.shape[1]
    n_lead = conv_off + conv_dim
    n = n_lead + w_gate.shape[1]
    assert w_gate.shape[1] == tn and n_lead % tn == 0
    tiles_per_seq = seq // tm
    off = row0 // tm
    assert tn == 2 * RET_HEADS * RET_QK_DIM, "rotary epilogue expects q and k in the first column tile"
    assert conv_off % tn == 0 and conv_dim % tn == 0
    conv_j0, conv_nj = conv_off // tn, conv_dim // tn
    sub = 512
    g_off = 2 * RET_HEADS * RET_QK_DIM + RET_HEADS * RET_V_DIM
    assert g_off % tn == 0 and (conv_off - g_off) % tn == 0 and (n - conv_off - conv_dim) % tn == 0
    silu_j = (g_off // tn, conv_off // tn)
    sigm_j = ((conv_off + conv_dim) // tn, n // tn)
    kern = functools.partial(_inproj_kernel, conv_j0=conv_j0, conv_nj=conv_nj, silu_j=silu_j, sigm_j=sigm_j,
                             tiles_per_seq=tiles_per_seq, tm=tm, tn=tn, sub=sub)
    conv_idx = lambda i, j: (0, jnp.clip(j - conv_j0, 0, conv_nj - 1))
    return pl.pallas_call(
        kern,
        out_shape=(jax.ShapeDtypeStruct((t, n), BF16), jax.ShapeDtypeStruct((t, LANES), F32)),
        grid=(t // tm, n // tn),
        in_specs=[
            pl.BlockSpec((tm, d), lambda i, j: (i + off, 0)),
            pl.BlockSpec((1, d), lambda i, j: (0, 0)),
            pl.BlockSpec((1, 1, d), lambda i, j: (((i + off) // tiles_per_seq) * N_MOD + 1, 0, 0)),
            pl.BlockSpec((1, 1, d), lambda i, j: (((i + off) // tiles_per_seq) * N_MOD + 0, 0, 0)),
            pl.BlockSpec((tm, LANES), lambda i, j: (i % tiles_per_seq, 0)),
            pl.BlockSpec((tm, LANES), lambda i, j: (i % tiles_per_seq, 0)),
            pl.BlockSpec((d, tn), lambda i, j: (0, jnp.minimum(j, n_lead // tn - 1))),
            pl.BlockSpec((d, tn), lambda i, j: (0, 0)),
            pl.BlockSpec((d, 2 * LANES), lambda i, j: (0, 0)),
            pl.BlockSpec((SSM_CONV, tn), conv_idx),
            pl.BlockSpec((1, tn), conv_idx),
        ],
        out_specs=(pl.BlockSpec((tm, tn), lambda i, j: (i, j)),
                   pl.BlockSpec((tm, LANES), lambda i, j: (i, 0))),
        scratch_shapes=[pltpu.VMEM((tm, d), BF16),
                        pltpu.VMEM((tn // LANES, tm + SUBLANES, LANES), F32),
                        pltpu.VMEM((conv_nj, tn // LANES, SUBLANES, LANES), F32)],
        compiler_params=_params(("arbitrary", "arbitrary")),
        name="inproj",
    )(x2, norm_w, mod3, mod3, cos, sin, w_all, w_gate, w_dt, conv_w, conv_b.reshape(1, conv_dim))


def _retention_body(q_ref, k_ref, v_ref, g_ref, din_ref, dq_ref, dk_ref, w_ref, o_ref, state, decay_c):
    heads = range(RET_HEADS)
    q = [q_ref[:, h * RET_QK_DIM:(h + 1) * RET_QK_DIM] for h in heads]
    k = [k_ref[:, h * RET_QK_DIM:(h + 1) * RET_QK_DIM] for h in heads]
    v = [v_ref[:, h * RET_V_DIM:(h + 1) * RET_V_DIM] for h in heads]
    scores = [(_nt_dot(q[h], k[h]) * din_ref[h]).astype(BF16) for h in heads]
    st = [state[h] for h in heads]
    cross = [jnp.dot(q[h], st[h].astype(BF16), preferred_element_type=F32) * dq_ref[h] for h in heads]
    inner = [jnp.dot(scores[h], v[h], preferred_element_type=F32) for h in heads]
    for h in heads:
        kd = (k[h].astype(F32) * dk_ref[h]).astype(BF16)
        state[h] = st[h] * decay_c[h] + _tn_dot(kd, v[h])
    acc = None
    for h in heads:
        ret = inner[h] + cross[h]
        ret = ret * lax.rsqrt(jnp.mean(ret * ret, axis=-1, keepdims=True) + EPS)
        ret = ret * g_ref[:, h * RET_V_DIM:(h + 1) * RET_V_DIM].astype(F32)
        part = jnp.dot(ret.astype(BF16), w_ref[h * RET_V_DIM:(h + 1) * RET_V_DIM, :],
                       preferred_element_type=F32)
        acc = part if acc is None else acc + part
    o_ref[...] = acc.astype(o_ref.dtype)


def _retention_tables(chunk):
    lg = np.log(1.0 - 2.0 ** (-5.0 - np.arange(RET_HEADS, dtype=np.float64)))
    idx = np.arange(chunk, dtype=np.float64)
    rel = idx[:, None] - idx[None, :]
    causal = rel >= 0
    din = np.where(causal[None], np.exp(np.where(causal, rel, 0.0)[None] * lg[:, None, None]), 0.0)
    dq = np.exp((idx + 1.0)[None, :, None] * lg[:, None, None])
    dk = np.exp((chunk - 1.0 - idx)[None, :, None] * lg[:, None, None])
    dc = tuple(float(v) for v in np.exp(chunk * lg))
    return (jnp.asarray(din, F32), jnp.asarray(dq, F32), jnp.asarray(dk, F32), dc)


def _ssd_decays(dt_ref, dtb_ref, alog_ref, tril_ref, exp_ref, n_sub, chunk):
    dt = jax.nn.softplus(dt_ref[...] + dtb_ref[...])
    adt = dt * (-LOG2_E * jnp.exp(alog_ref[...]))
    p1 = adt.astype(BF16)
    r1 = adt - p1.astype(F32)
    p2 = r1.astype(BF16)
    p3 = (r1 - p2.astype(F32)).astype(BF16)
    pieces = jnp.concatenate([p1, p2, p3], axis=-1)
    acs = []
    for s in range(n_sub):
        c3 = jnp.dot(tril_ref[...], pieces[s * chunk:(s + 1) * chunk, :], preferred_element_type=F32)
        acs.append(c3[:, :LANES] + c3[:, LANES:2 * LANES] + c3[:, 2 * LANES:])
    dt_x = jnp.dot(dt.astype(BF16), exp_ref[...], preferred_element_type=F32)
    return acs, dt_x


def _ssd_body(z_ref, xbc_ref, acs, dt_x, dsk_ref, nw_ref, yn_s, state, chunk, d_inner):
    heads_per_group = d_inner // SSM_HEAD_DIM // SSM_GROUPS
    gw = heads_per_group * SSM_HEAD_DIM
    assert SSM_HEAD_DIM * 2 == LANES and gw == 2 * LANES

    acs_t = acs.T
    li = lax.broadcasted_iota(jnp.int32, (chunk, chunk), 0)
    si = lax.broadcasted_iota(jnp.int32, (chunk, chunk), 1)
    causal = li >= si
    low_half = si < SSM_HEAD_DIM
    lane_g = lax.broadcasted_iota(jnp.int32, (chunk, gw), 1)

    b_off = d_inner
    c_off = d_inner + SSM_GROUPS * SSM_STATE
    for g in range(SSM_GROUPS):
        bm = xbc_ref[:, b_off + g * SSM_STATE: b_off + (g + 1) * SSM_STATE]
        cm = xbc_ref[:, c_off + g * SSM_STATE: c_off + (g + 1) * SSM_STATE]
        xs_g = xbc_ref[:, g * gw:(g + 1) * gw].astype(F32)
        xdt_g = xs_g * dt_x[:, g * gw:(g + 1) * gw]
        xdt_b = xdt_g.astype(BF16)
        cb = _nt_dot(cm, bm)
        cols, ms, xm = [], [], []
        for jh in range(heads_per_group):
            h = g * heads_per_group + jh
            col = jnp.broadcast_to(acs[:, h:h + 1], (chunk, chunk))
            seg = jnp.exp2(jnp.where(causal, col - acs_t[h:h + 1, :], -jnp.inf))
            cols.append(col)
            ms.append((cb * seg).astype(BF16))
            in_head = (lane_g >= jh * SSM_HEAD_DIM) & (lane_g < (jh + 1) * SSM_HEAD_DIM)
            xm.append(jnp.where(in_head, xdt_b, jnp.zeros_like(xdt_b)))
        y_diag = jnp.dot(jnp.concatenate(ms, axis=-1), jnp.concatenate(xm, axis=0),
                         preferred_element_type=F32)
        a_x = jnp.concatenate([jnp.where(low_half, cols[0], cols[1]),
                               jnp.where(low_half, cols[2], cols[3])], axis=-1)
        e_acs_x = jnp.exp2(a_x)
        a_last_x = a_x[chunk - 1:chunk, :]
        st = state[g]
        y_off = jnp.dot(cm, st.astype(BF16), preferred_element_type=F32) * e_acs_x
        xdec = (xdt_g * jnp.exp2(a_last_x - a_x)).astype(BF16)
        state[g] = st * e_acs_x[chunk - 1:chunk, :] + _tn_dot(bm, xdec)
        y = y_diag + y_off + dsk_ref[:, g * gw:(g + 1) * gw] * xs_g
        yz = y * z_ref[:, g * gw:(g + 1) * gw].astype(F32)
        yn = yz * lax.rsqrt(jnp.mean(yz * yz, axis=-1, keepdims=True) + EPS) * nw_ref[:, g * gw:(g + 1) * gw]
        yn_s[:, g * gw:(g + 1) * gw] = yn.astype(BF16)


def _mixers_kernel(q_ref, k_ref, v_ref, g_ref, din_ref, dq_ref, dk_ref, wret_ref,
                   z_ref, xbc_ref, dt_ref, dtb_ref, alog_ref, dsk_ref, nw_ref, tril_ref, exp_ref, wssm_ref,
                   ga_ref, gb_ref, x_ref, gm_ref, scf_ref, shf_ref, nf_ref, wo_ref, wr_ref, br_ref, tri_ref,
                   x1_ref, h2_ref, idx_ref, rank_ref, prow_ref, cnt_ref,
                   rstate, sstate, yn_s, ya_s, yb_s, cnt_s, *, decay_c, chunk, ssm_chunk, d_inner):
    b = pl.program_id(0)
    c = pl.program_id(1)

    @pl.when(c == 0)
    def _():
        rstate[...] = jnp.zeros_like(rstate)
        sstate[...] = jnp.zeros_like(sstate)

    @pl.when((b == 0) & (c == 0))
    def _():
        cnt_s[...] = jnp.zeros_like(cnt_s)

    n_sub = chunk // ssm_chunk
    acs, dt_x = _ssd_decays(dt_ref, dtb_ref, alog_ref, tril_ref, exp_ref, n_sub, ssm_chunk)
    _retention_body(q_ref, k_ref, v_ref, g_ref, din_ref, dq_ref, dk_ref, wret_ref, ya_s, rstate, decay_c)
    for sub in range(n_sub):
        rows = pl.ds(sub * ssm_chunk, ssm_chunk)
        _ssd_body(z_ref.at[rows, :], xbc_ref.at[rows, :], acs[sub],
                  dt_x[sub * ssm_chunk:(sub + 1) * ssm_chunk, :], dsk_ref, nw_ref, yn_s.at[rows, :],
                  sstate, ssm_chunk, d_inner)
    yb_s[...] = jnp.dot(yn_s[...], wssm_ref[...], preferred_element_type=F32)
    _merge_body(ya_s, yb_s, ga_ref, gb_ref, x_ref, gm_ref, scf_ref, shf_ref, nf_ref, wo_ref, wr_ref, br_ref,
                tri_ref, x1_ref, h2_ref, idx_ref, rank_ref, prow_ref, cnt_ref, cnt_s, chunk)


def _mixers_call(proj, dt_raw, x2, row0, mod3, w_ret, dt_bias, a_log, d_skip, ssm_norm, w_ssm,
                 norm_ffn, w_out, w_router_t, b_router, bsz, seq, chunk, ssm_chunk):
    t = proj.shape[0]
    d_inner, d = w_ssm.shape
    conv_dim = d_inner + 2 * SSM_GROUPS * SSM_STATE
    n_heads = d_inner // SSM_HEAD_DIM
    nc = seq // chunk
    gw = d_inner // SSM_GROUPS
    qk_w = RET_HEADS * RET_QK_DIM
    v_w = RET_HEADS * RET_V_DIM
    assert chunk % ssm_chunk == 0
    assert ssm_chunk == LANES, "the per-head decay tiles are built lane-for-lane against the chunk"
    din, dq, dk, dc = _retention_tables(chunk)
    pad_h = lambda v: jnp.pad(v.astype(F32), (0, LANES - n_heads)).reshape(1, LANES)
    tril = jnp.asarray(np.tril(np.ones((ssm_chunk, ssm_chunk), np.float32)), BF16)
    expand = np.zeros((LANES, d_inner), np.float32)
    for h in range(n_heads):
        expand[h, h * SSM_HEAD_DIM:(h + 1) * SSM_HEAD_DIM] = 1.0
    expand = jnp.asarray(expand, BF16)
    kern = functools.partial(_mixers_kernel, decay_c=dc, chunk=chunk, ssm_chunk=ssm_chunk, d_inner=d_inner)
    row = lambda b, c: b * nc + c
    z_blk = (2 * qk_w + 2 * v_w) // d_inner
    xbc_blk = (2 * qk_w + 2 * v_w + d_inner) // conv_dim
    full = lambda shape: pl.BlockSpec(shape, lambda b, c: (0,) * len(shape))
    nt = t // chunk
    ga_blk = proj.shape[1] // d - 2
    seq0 = row0 // seq
    modspec = lambda m: pl.BlockSpec((1, 1, d), lambda b, c: ((b + seq0) * N_MOD + m, 0, 0))
    tri = jnp.asarray(np.triu(np.ones((chunk, chunk), np.float32), 1), BF16)
    return pl.pallas_call(
        kern,
        out_shape=(jax.ShapeDtypeStruct((t, d), F32), jax.ShapeDtypeStruct((t * SLAB_ROWS, LANES), jnp.uint32),
                   jax.ShapeDtypeStruct((nt, SUBLANES, chunk), jnp.int32),
                   jax.ShapeDtypeStruct((nt, SUBLANES, chunk), jnp.int32),
                   jax.ShapeDtypeStruct((t, LANES), F32),
                   jax.ShapeDtypeStruct((N_EXPERTS, LANES), jnp.int32)),
        grid=(bsz, nc),
        in_specs=[
            pl.BlockSpec((chunk, qk_w), lambda b, c: (row(b, c), 0)),
            pl.BlockSpec((chunk, qk_w), lambda b, c: (row(b, c), 1)),
            pl.BlockSpec((chunk, v_w), lambda b, c: (row(b, c), 1)),
            pl.BlockSpec((chunk, v_w), lambda b, c: (row(b, c), 2)),
            full((RET_HEADS, chunk, chunk)), full((RET_HEADS, chunk, 1)), full((RET_HEADS, chunk, 1)),
            full((v_w, d)),
            pl.BlockSpec((chunk, d_inner), lambda b, c: (row(b, c), z_blk)),
            pl.BlockSpec((chunk, conv_dim), lambda b, c: (row(b, c), xbc_blk)),
            pl.BlockSpec((chunk, LANES), lambda b, c: (row(b, c), 0)),
            full((1, LANES)), full((1, LANES)),
            full((1, d_inner)), full((1, d_inner)), full((ssm_chunk, ssm_chunk)), full((LANES, d_inner)),
            full((d_inner, d)),
            pl.BlockSpec((chunk, d), lambda b, c: (row(b, c), ga_blk)),
            pl.BlockSpec((chunk, d), lambda b, c: (row(b, c), ga_blk + 1)),
            pl.BlockSpec((chunk, d), lambda b, c: (row0 // chunk + row(b, c), 0)),
            modspec(2), modspec(4), modspec(3),
            full((1, d)), full((d, d)), full((2 * N_EXPERTS, d)), full((N_EXPERTS, 1)), full((chunk, chunk)),
        ],
        out_specs=(pl.BlockSpec((chunk, d), lambda b, c: (row(b, c), 0)),
                   pl.BlockSpec((chunk * SLAB_ROWS, LANES), lambda b, c: (row(b, c), 0)),
                   pl.BlockSpec((1, SUBLANES, chunk), lambda b, c: (row(b, c), 0, 0)),
                   pl.BlockSpec((1, SUBLANES, chunk), lambda b, c: (row(b, c), 0, 0)),
                   pl.BlockSpec((chunk, LANES), lambda b, c: (row(b, c), 0)),
                   full((N_EXPERTS, LANES))),
        scratch_shapes=[pltpu.VMEM((RET_HEADS, RET_QK_DIM, RET_V_DIM), F32),
                        pltpu.VMEM((SSM_GROUPS, SSM_STATE, gw), F32),
                        pltpu.VMEM((chunk, d_inner), BF16),
                        pltpu.VMEM((chunk, d), F32), pltpu.VMEM((chunk, d), F32),
                        pltpu.VMEM((N_EXPERTS, LANES), F32)],
        compiler_params=_params(("arbitrary", "arbitrary")),
        name="mixers",
    )(proj, proj, proj, proj, din, dq, dk, w_ret,
      proj, proj, dt_raw, pad_h(dt_bias), pad_h(a_log),
      jnp.repeat(d_skip.astype(F32), SSM_HEAD_DIM).reshape(1, d_inner), ssm_norm.reshape(1, d_inner),
      tril, expand, w_ssm,
      proj, proj, x2, mod3, mod3, mod3, norm_ffn, w_out, w_router_t, b_router.reshape(N_EXPERTS, 1), tri)


def _merge_body(ya_ref, yb_ref, ga_ref, gb_ref, x_ref, gm_ref, scf_ref, shf_ref, nw_ref, wo_ref,
                wr_ref, br_ref, tri_ref,
                x1_ref, h2_ref, idx_ref, rank_ref, prow_ref, cnt_ref, cnt_s, tm):
    merged = (ga_ref[...].astype(F32) * ya_ref[...].astype(F32)
              + gb_ref[...].astype(F32) * yb_ref[...].astype(F32))
    mo = jnp.dot(merged.astype(BF16), wo_ref[...], preferred_element_type=F32)
    x1 = x_ref[...] + gm_ref[0] * mo
    x1_ref[...] = x1
    ms = jnp.mean(x1 * x1, axis=-1, keepdims=True)
    h2 = x1 * lax.rsqrt(ms + EPS) * nw_ref[...] * (1.0 + scf_ref[0]) + shf_ref[0]
    _store_slabs(h2_ref, h2, tm)

    h_hi = h2.astype(BF16)
    h_lo = (h2 - h_hi.astype(F32)).astype(BF16)
    lg2 = _nt_dot(wr_ref[...], h_hi)
    lg = lg2[:N_EXPERTS] + lg2[N_EXPERTS:] + _nt_dot(wr_ref[:N_EXPERTS, :], h_lo) + br_ref[...]
    sub = lax.broadcasted_iota(jnp.int32, lg.shape, 0)
    work = lg
    vals, idxs, sels = [], [], []
    for _ in range(TOP_K):
        m = jnp.max(work, axis=0, keepdims=True)
        ik = jnp.min(jnp.where(work == m, sub, N_EXPERTS), axis=0, keepdims=True)
        sel = sub == ik
        vals.append(m)
        idxs.append(ik)
        sels.append(sel)
        work = jnp.where(sel, -jnp.inf, work)
    exps = [jnp.exp(v - vals[0]) for v in vals]
    denom = exps[0]
    for e in exps[1:]:
        denom = denom + e
    probs = [e / denom for e in exps]

    base = cnt_s[:, 0:1]
    ranks = []
    for k in range(TOP_K):
        mk = jnp.where(sels[k], 1.0, 0.0)
        pre = jnp.dot(mk.astype(BF16), tri_ref[...], preferred_element_type=F32)
        ranks.append(jnp.sum(jnp.where(sels[k], pre + base, 0.0), axis=0, keepdims=True))
        base = base + jnp.sum(mk, axis=1, keepdims=True)
    cnt_s[...] = jnp.broadcast_to(base, cnt_s.shape)
    cnt_ref[...] = cnt_s[...].astype(jnp.int32)

    zi = jnp.zeros((SUBLANES - TOP_K, tm), jnp.int32)
    idx_ref[0] = jnp.concatenate(idxs + [zi], axis=0)
    rank_ref[0] = jnp.concatenate([r.astype(jnp.int32) for r in ranks] + [zi], axis=0)
    pt = jnp.concatenate(probs + [jnp.zeros((LANES - TOP_K, tm), F32)], axis=0)
    prow_ref[...] = pt.T


def _sc_mesh():
    return plsc.VectorSubcoreMesh(core_axis_name="c", subcore_axis_name="s")


def _sc_worker():
    return lax.axis_index("s") * SC_CORES + lax.axis_index("c")


def _sc_scatter_rows(rows, dest, n_out):
    t = rows.shape[0]
    n_k = dest.shape[0]
    g = SC_GROUP
    n_w = SC_CORES * SC_SUBCORES
    assert t % (n_w * g) == 0
    cpw = t // (n_w * g)
    dest_w = dest.reshape(n_k, n_w, cpw, g).transpose(1, 0, 2, 3)

    @functools.partial(
        pl.kernel, mesh=_sc_mesh(),
        out_type=jax.ShapeDtypeStruct((n_out,) + rows.shape[1:], rows.dtype),
        scratch_types=[pltpu.VMEM((n_k, cpw, g), jnp.int32),
                       pltpu.VMEM((g,) + rows.shape[1:], rows.dtype),
                       pltpu.SemaphoreType.DMA],
    )
    def scatter(rows_hbm, dest_hbm, out_hbm, idx_v, rows_v, sem):
        wid = _sc_worker()
        pltpu.sync_copy(dest_hbm.at[wid], idx_v)

        @pl.loop(0, cpw)
        def _(cc):
            r0 = pl.multiple_of((wid * cpw + cc) * g, g)
            pltpu.sync_copy(rows_hbm.at[pl.ds(r0, g)], rows_v)
            copies = [pltpu.async_copy(rows_v, out_hbm.at[idx_v.at[k, cc]], sem) for k in range(n_k)]
            for cp in copies:
                cp.wait()

    return scatter(rows, dest_w)


def _sc_gather_rows(table, idx):
    m = idx.shape[0]
    g = SC_GROUP
    n_w = SC_CORES * SC_SUBCORES
    assert m % (n_w * g) == 0
    per_w = m // n_w

    @functools.partial(
        pl.kernel, mesh=_sc_mesh(),
        out_type=jax.ShapeDtypeStruct((m,) + table.shape[1:], table.dtype),
        scratch_types=[pltpu.VMEM((per_w,), jnp.int32),
                       pltpu.VMEM((g,) + table.shape[1:], table.dtype),
                       pltpu.SemaphoreType.DMA],
    )
    def gather(table_hbm, idx_hbm, out_hbm, idx_v, rows_v, sem):
        base = _sc_worker() * per_w
        pltpu.sync_copy(idx_hbm.at[pl.ds(base, per_w)], idx_v)

        @pl.loop(0, per_w // g)
        def _(cc):
            off = pl.multiple_of(cc * g, g)
            pltpu.async_copy(table_hbm.at[idx_v.at[pl.ds(off, g)]], rows_v, sem).wait()
            pltpu.sync_copy(rows_v, out_hbm.at[pl.ds(base + off, g)])

    return gather(table, idx)


def _ffn_kernel(be_ref, br_ref, bv_ref, first_ref, slot_ref, next_ref,
                x_ref, wgu_hbm, bgu_ref, wd_hbm, bd_ref, o_ref, wgu_f, wd_f, wgu_s, wd_s, sem, *, bm, d_ff):
    i = pl.program_id(0)

    def fetch(e, slot):
        return (pltpu.make_async_copy(wgu_hbm.at[e], wgu_f.at[slot], sem.at[0, slot]),
                pltpu.make_async_copy(wd_hbm.at[e], wd_f.at[slot], sem.at[1, slot]))

    @pl.when(i == 0)
    def _():
        for cp in fetch(be_ref[0], slot_ref[0]):
            cp.start()

    @pl.when(first_ref[i] == 1)
    def _():
        slot = slot_ref[i]
        for cp in fetch(be_ref[i], slot):
            cp.wait()
        wgu_s[...] = wgu_f[slot].astype(BF16)
        wd_s[...] = wd_f[slot].astype(BF16)

        @pl.when(next_ref[i] >= 0)
        def _():
            for cp in fetch(next_ref[i], 1 - slot):
                cp.start()

    @pl.when(bv_ref[i] == 1)
    def _():
        x = jnp.concatenate(_load_slabs(x_ref, bm), axis=-1).astype(BF16)
        gu = jnp.dot(x, wgu_s[...], preferred_element_type=F32) + bgu_ref[0]
        gate = jnp.minimum(gu[:, :d_ff], SWIGLU_LIMIT)
        up = jnp.clip(gu[:, d_ff:], -SWIGLU_LIMIT, SWIGLU_LIMIT)
        act = gate * jax.nn.sigmoid(SWIGLU_ALPHA * gate) * (up + 1.0)
        y = jnp.dot(act.astype(BF16), wd_s[...], preferred_element_type=F32) + bd_ref[0]
        _store_slabs(o_ref, y, bm)

    @pl.when(bv_ref[i] == 0)
    def _():
        o_ref[...] = jnp.zeros_like(o_ref)


def _ffn_call(blk_e, blk_row, blk_valid, xs, w_gu, b_gu, w_d, b_d, bm):
    n_e, d, f2 = w_gu.shape
    d_ff = f2 // 2
    nb = blk_e.shape[0]
    first = jnp.concatenate([jnp.ones((1,), jnp.int32), (blk_e[1:] != blk_e[:-1]).astype(jnp.int32)])
    slot = (jnp.cumsum(first) - 1) % 2
    later = jnp.where(blk_e[None, :] > blk_e[:, None], blk_e[None, :], n_e)
    nxt = jnp.min(later, axis=1)
    nxt = jnp.where(nxt == n_e, -1, nxt)
    kern = functools.partial(_ffn_kernel, bm=bm, d_ff=d_ff)
    imap = lambda f: (lambda i, be, br, bv, fi, sl, nx: f(i, be, br))
    gs = pltpu.PrefetchScalarGridSpec(
        num_scalar_prefetch=6,
        grid=(nb,),
        in_specs=[pl.BlockSpec((bm * SLAB_ROWS, LANES), imap(lambda i, be, br: (br[i], 0))),
                  pl.BlockSpec(memory_space=pl.ANY),
                  pl.BlockSpec((1, 1, f2), imap(lambda i, be, br: (be[i], 0, 0))),
                  pl.BlockSpec(memory_space=pl.ANY),
                  pl.BlockSpec((1, 1, d), imap(lambda i, be, br: (be[i], 0, 0)))],
        out_specs=pl.BlockSpec((bm * SLAB_ROWS, LANES), imap(lambda i, be, br: (i, 0))),
        scratch_shapes=[pltpu.VMEM((2, d, f2), F32), pltpu.VMEM((2, d_ff, d), F32),
                        pltpu.VMEM((d, f2), BF16), pltpu.VMEM((d_ff, d), BF16),
                        pltpu.SemaphoreType.DMA((2, 2))],
    )
    return pl.pallas_call(
        kern,
        out_shape=jax.ShapeDtypeStruct((nb * bm * SLAB_ROWS, LANES), jnp.uint32),
        grid_spec=gs,
        compiler_params=_params(("arbitrary",)),
        name="ffn",
    )(blk_e, blk_row, blk_valid, first, slot.astype(jnp.int32), nxt.astype(jnp.int32),
      xs, w_gu, b_gu.reshape(n_e, 1, f2), w_d, b_d.reshape(n_e, 1, d))


def _combine_kernel(y0_ref, y1_ref, y2_ref, y3_ref, prow_ref, x1_ref, gf_ref, nw_ref, *rest, tm):
    o_ref = rest[-1]
    p = prow_ref[...]
    pieces = [_load_slabs(y_ref, tm) for y_ref in (y0_ref, y1_ref, y2_ref, y3_ref)]
    for s in range(len(pieces[0])):
        moe = None
        for k in range(TOP_K):
            piece = pieces[k][s] * p[:, k:k + 1]
            moe = piece if moe is None else moe + piece
        sl = slice(s * LANES, (s + 1) * LANES)
        o_ref[:, sl] = x1_ref[:, sl] + gf_ref[0][:, sl] * moe
    xo = o_ref[...]
    o_ref[...] = xo * lax.rsqrt(jnp.mean(xo * xo, axis=-1, keepdims=True) + EPS) * nw_ref[...]


def _combine_call(ytok, prow, x1, mod3, norm_final, seq, tm, row0, t_total, out_prev):
    t, d = x1.shape
    nt = t // tm
    tiles_per_seq = seq // tm
    off = row0 // tm
    kern = functools.partial(_combine_kernel, tm=tm)
    yspec = lambda k: pl.BlockSpec((tm * SLAB_ROWS, LANES), lambda i: (k * nt + i, 0))
    in_specs = [yspec(0), yspec(1), yspec(2), yspec(3),
                pl.BlockSpec((tm, LANES), lambda i: (i, 0)),
                pl.BlockSpec((tm, d), lambda i: (i, 0)),
                pl.BlockSpec((1, 1, d), lambda i: (((i + off) // tiles_per_seq) * N_MOD + 5, 0, 0)),
                pl.BlockSpec((1, d), lambda i: (0, 0))]
    args = [ytok, ytok, ytok, ytok, prow, x1, mod3, norm_final]
    aliases = {}
    if out_prev is not None:
        in_specs.append(pl.BlockSpec(memory_space=pl.ANY))
        aliases = {len(args): 0}
        args.append(out_prev)
    return pl.pallas_call(
        kern,
        out_shape=jax.ShapeDtypeStruct((t_total, d), F32),
        grid=(nt,),
        in_specs=in_specs,
        out_specs=pl.BlockSpec((tm, d), lambda i: (i + off, 0)),
        input_output_aliases=aliases,
        compiler_params=_params(("arbitrary",)),
        name="combine",
    )(*args)


def _plan(seq):
    def fit(pref):
        tm = min(pref, seq)
        assert seq % tm == 0
        return tm
    return dict(tm_in=fit(1024), tm_moe=fit(512), ret_chunk=fit(RET_CHUNK), ssm_chunk=fit(SSM_CHUNK))


def _layer(x2, mod3, bsz, seq, norm_mix, norm_ffn, w_in, conv_w, conv_b, dt_bias, a_log, d_skip, ssm_norm,
           w_ret_out, w_ssm_out, w_out, w_router, b_router, w_gate_up, b_gate_up, w_down, b_down,
           norm_final):
    t, d = x2.shape
    plan = _plan(seq)
    qk_w = RET_HEADS * RET_QK_DIM
    v_w = RET_HEADS * RET_V_DIM
    d_inner = w_ssm_out.shape[0]
    conv_dim = conv_w.shape[1]
    n_heads = d_inner // SSM_HEAD_DIM
    dt_off = 2 * qk_w + 2 * v_w + d_inner + conv_dim

    w_all = w_in.astype(BF16)
    w_gate = w_in[:, dt_off + n_heads:].astype(BF16)
    w_dt = jnp.pad(w_in[:, dt_off:dt_off + n_heads], ((0, 0), (0, LANES - n_heads)))
    w_dt_hi = w_dt.astype(BF16)
    w_dt = jnp.concatenate([w_dt_hi, (w_dt - w_dt_hi.astype(F32)).astype(BF16)], axis=1)
    half = RET_QK_DIM // 2
    inv_freq = ROPE_BASE ** (-jnp.arange(half, dtype=F32) / half)
    ang = jnp.arange(seq, dtype=F32)[:, None] * inv_freq[None, :]
    cos, sin = jnp.cos(ang), jnp.sin(ang)

    w_ret_b, w_ssm_b, w_out_b = w_ret_out.astype(BF16), w_ssm_out.astype(BF16), w_out.astype(BF16)
    w_r_hi = w_router.T.astype(BF16)
    w_router_t = jnp.concatenate([w_r_hi, (w_router.T - w_r_hi.astype(F32)).astype(BF16)], axis=0)
    bm = FFN_BLOCK
    slab = (SLAB_ROWS, LANES)

    def mixer(row0, tg, bg):
        proj, dt_raw = _inproj_call(x2, row0, tg, norm_mix.reshape(1, d), mod3, cos, sin, w_all, w_gate, w_dt,
                                    conv_w, conv_b, 2 * qk_w + 2 * v_w + d_inner, seq, plan["tm_in"], 2 * qk_w)
        x1, h2, idx, rank, prow, cnt = _mixers_call(
            proj, dt_raw, x2, row0, mod3, w_ret_b, dt_bias, a_log, d_skip, ssm_norm, w_ssm_b,
            norm_ffn.reshape(1, d), w_out_b, w_router_t, b_router, bg, seq, plan["ret_chunk"], plan["ssm_chunk"])
        counts = cnt[:, 0]
        padded = ((counts + bm - 1) // bm) * bm
        pad_end = jnp.cumsum(padded)
        start_pad = pad_end - padded
        n_blocks = -(-(tg * TOP_K) // bm) + N_EXPERTS
        e_ids = jnp.arange(N_EXPERTS, dtype=jnp.int32)[:, None, None, None]
        dest = rank + jnp.sum(jnp.where(idx[None] == e_ids, start_pad[:, None, None, None], 0), axis=0)
        dest = dest[:, :TOP_K, :].transpose(1, 0, 2).reshape(TOP_K, tg).astype(jnp.int32)
        n_real = pad_end[-1] // bm
        blk_valid = (jnp.arange(n_blocks) < n_real).astype(jnp.int32)
        blk_row = jnp.minimum(jnp.arange(n_blocks), n_real - 1).astype(jnp.int32)
        blk_e = jnp.minimum(jnp.sum(pad_end[None, :] <= (blk_row * bm)[:, None], axis=1),
                            N_EXPERTS - 1).astype(jnp.int32)
        xs = _sc_scatter_rows(h2.reshape((tg,) + slab), dest, n_blocks * bm)
        return dict(x1=x1, prow=prow, dest=dest, blocks=(blk_e, blk_row, blk_valid), xs=xs, row0=row0)

    def experts(m):
        n_rows = m["xs"].shape[0]
        ys = _ffn_call(*m["blocks"], m["xs"].reshape(n_rows * SLAB_ROWS, LANES),
                       w_gate_up, b_gate_up, w_down, b_down, bm)
        return _sc_gather_rows(ys.reshape((n_rows,) + slab), m["dest"].reshape(-1))

    n_groups = N_GROUPS if bsz % N_GROUPS == 0 else 1
    bg = bsz // n_groups
    tg = bg * seq
    groups = [mixer(g * tg, tg, bg) for g in range(n_groups)]
    ytoks = [experts(m) for m in groups]
    out = None
    for y, m in zip(ytoks, groups):
        out = _combine_call(y.reshape(TOP_K * tg * SLAB_ROWS, LANES), m["prow"], m["x1"], mod3,
                            norm_final.reshape(1, d), seq, plan["tm_moe"], m["row0"], t, out)
    return out


def kernel(x, c, w_ada, b_ada, norm_mix, norm_ffn, w_in, conv_w, conv_b, dt_bias, a_log, d_skip, ssm_norm,
           w_ret_out, w_ssm_out, w_out, w_router, b_router, w_gate_up, b_gate_up, w_down, b_down, norm_final):
    bsz, seq, d = x.shape
    depth = w_ada.shape[0]
    assert depth == 1, "the final norm is fused into the single layer's last kernel"
    x2 = x.reshape(bsz * seq, d)
    l = 0
    mod = _mod_call(c, w_ada[l], b_ada[l])
    mod3 = mod.reshape(bsz * N_MOD, 1, d)
    out = _layer(x2, mod3, bsz, seq, norm_mix[l], norm_ffn[l], w_in[l], conv_w[l], conv_b[l], dt_bias[l],
                 a_log[l], d_skip[l], ssm_norm[l], w_ret_out[l], w_ssm_out[l], w_out[l], w_router[l],
                 b_router[l], w_gate_up[l], b_gate_up[l], w_down[l], b_down[l], norm_final)
    return out.reshape(bsz, seq, d)
```

```python
import functools
import math

import numpy as np
import jax
import jax.numpy as jnp
from jax import lax
from jax.experimental import pallas as pl
from jax.experimental.pallas import tpu as pltpu
from jax.experimental.pallas import tpu_sc as plsc

F32 = jnp.float32
BF16 = jnp.bfloat16
HIGHEST = lax.Precision.HIGHEST

EPS = 1e-6
N_MOD = 6
RET_HEADS = 4
RET_QK_DIM = 256
RET_V_DIM = 512
ROPE_BASE = 10000.0
SSM_HEAD_DIM = 64
SSM_GROUPS = 8
SSM_STATE = 128
SSM_CONV = 4
N_EXPERTS = 32
TOP_K = 4
SWIGLU_LIMIT = 7.0
SWIGLU_ALPHA = 1.702

LANES = 128
SUBLANES = 8
VMEM_LIMIT = 56 * 1024 * 1024

RET_CHUNK = 256
SSM_CHUNK = 128
FFN_BLOCK = 512
SC_CORES = 2
SC_SUBCORES = 16
SC_GROUP = 64
N_GROUPS = 2


def _params(sem, vmem=VMEM_LIMIT):
    return pltpu.CompilerParams(dimension_semantics=sem, vmem_limit_bytes=vmem)


def _nt_dot(a, b, **kw):
    return lax.dot_general(a, b, (((1,), (1,)), ((), ())), preferred_element_type=F32, **kw)


def _tn_dot(a, b, **kw):
    return lax.dot_general(a, b, (((0,), (0,)), ((), ())), preferred_element_type=F32, **kw)


def _silu(v):
    return v * jax.nn.sigmoid(v)


SLAB_ROWS = 4
HIGH_HALF = 0xFFFF0000
LOG2_E = math.log2(math.e)


def _store_slabs(ref, vals, n):
    for s in range(SLAB_ROWS):
        lo = vals[:, s * LANES:(s + 1) * LANES].astype(BF16).astype(F32)
        hi = vals[:, (s + SLAB_ROWS) * LANES:(s + SLAB_ROWS + 1) * LANES].astype(BF16).astype(F32)
        word = (pltpu.bitcast(lo, jnp.uint32) >> 16) | (pltpu.bitcast(hi, jnp.uint32) & jnp.uint32(HIGH_HALF))
        ref[pl.ds(s, n, stride=SLAB_ROWS), :] = word


def _load_slabs(ref, n, base=0, keep=None):
    lo, hi = [], []
    for s in range(SLAB_ROWS):
        word = ref[pl.ds(base + s, n, stride=SLAB_ROWS), :]
        if keep is not None:
            word = jnp.where(keep, word, jnp.zeros_like(word))
        lo.append(pltpu.bitcast(word << 16, F32))
        hi.append(pltpu.bitcast(word & jnp.uint32(HIGH_HALF), F32))
    return lo + hi


def _mod_kernel(c_ref, w_ref, b_ref, o_ref):
    cond = _silu(c_ref[...])
    o_ref[...] = jnp.dot(cond, w_ref[...], preferred_element_type=F32, precision=HIGHEST) + b_ref[...]


def _mod_call(c, w_ada, b_ada):
    bsz, d = c.shape
    n = w_ada.shape[1]
    return pl.pallas_call(
        _mod_kernel,
        out_shape=jax.ShapeDtypeStruct((bsz, n), F32),
        grid=(n // d,),
        in_specs=[pl.BlockSpec((bsz, d), lambda j: (0, 0)),
                  pl.BlockSpec((d, d), lambda j: (0, j)),
                  pl.BlockSpec((1, d), lambda j: (0, j))],
        out_specs=pl.BlockSpec((bsz, d), lambda j: (0, j)),
        compiler_params=_params(("arbitrary",)),
        name="mod",
    )(c, w_ada, b_ada.reshape(1, n))


def _inproj_kernel(x_ref, nw_ref, sc_ref, sh_ref, cos_ref, sin_ref, w_ref, wg_ref, wdt_ref, cw_ref, cb_ref,
                   o_ref, dt_ref, h_s, work, carry, *, conv_j0, conv_nj, silu_j, sigm_j, tiles_per_seq,
                   tm, tn, sub):
    i = pl.program_id(0)
    j = pl.program_id(1)
    n_dt = dt_ref.shape[1]
    rows = min(tm, 256)

    @pl.when(j == 0)
    def _():
        xf = x_ref[...]
        ms = jnp.mean(xf * xf, axis=-1, keepdims=True)
        y = xf * lax.rsqrt(ms + EPS) * nw_ref[...]
        hm = y * (1.0 + sc_ref[0]) + sh_ref[0]
        hb = hm.astype(BF16)
        h_s[...] = hb
        h_lo = (hm - hb.astype(F32)).astype(BF16)
        d_hi = jnp.dot(hb, wdt_ref[...], preferred_element_type=F32)
        d_lo = jnp.dot(h_lo, wdt_ref[:, :n_dt], preferred_element_type=F32)
        dt_ref[...] = d_hi[:, :n_dt] + d_hi[:, n_dt:] + d_lo
        half = RET_QK_DIM // 2
        for p in range(tn // sub):
            for r in range(tm // rows):
                rs = slice(r * rows, (r + 1) * rows)
                acc = jnp.dot(h_s[rs, :], w_ref[:, p * sub:(p + 1) * sub], preferred_element_type=F32)
                cos = cos_ref[rs, :]
                sin = sin_ref[rs, :]
                for cc in range(sub // RET_QK_DIM):
                    c = p * (sub // RET_QK_DIM) + cc
                    a = acc[:, cc * RET_QK_DIM: cc * RET_QK_DIM + half]
                    b = acc[:, cc * RET_QK_DIM + half: (cc + 1) * RET_QK_DIM]
                    scale = 1.0 if c < RET_HEADS else RET_QK_DIM ** -0.5
                    o_ref[rs, c * RET_QK_DIM: c * RET_QK_DIM + half] = ((a * cos - b * sin) * scale).astype(BF16)
                    o_ref[rs, c * RET_QK_DIM + half: (c + 1) * RET_QK_DIM] = (
                        (a * sin + b * cos) * scale).astype(BF16)

    is_conv = (j >= conv_j0) & (j < conv_j0 + conv_nj)

    @pl.when(is_conv)
    def _():
        cj = j - conv_j0
        pad = SUBLANES

        @pl.when(i % tiles_per_seq == 0)
        def _():
            carry[cj] = jnp.zeros(carry.shape[1:], F32)

        for p in range(tn // sub):
            for r in range(tm // rows):
                r0 = r * rows
                acc = jnp.dot(h_s[r0:r0 + rows, :], w_ref[:, p * sub:(p + 1) * sub], preferred_element_type=F32)
                for cc in range(sub // LANES):
                    c = p * (sub // LANES) + cc
                    cols = slice(c * LANES, (c + 1) * LANES)
                    if r == 0:
                        work[c, 0:pad, :] = carry[cj, c]
                    lo = pad + r0
                    work[c, lo:lo + rows, :] = acc[:, cc * LANES:(cc + 1) * LANES]
                    conv = cb_ref[:, cols] + cw_ref[SSM_CONV - 1:SSM_CONV, cols] * work[c, lo:lo + rows, :]
                    for k in range(SSM_CONV - 1):
                        shift = SSM_CONV - 1 - k
                        conv = conv + cw_ref[k:k + 1, cols] * work[c, lo - shift:lo - shift + rows, :]
                    if r0 + rows == tm:
                        carry[cj, c] = work[c, tm:tm + pad, :]
                    o_ref[r0:r0 + rows, cols] = _silu(conv).astype(BF16)

    def plain(act, weights):
        for p in range(tn // sub):
            for r in range(tm // rows):
                acc = jnp.dot(h_s[r * rows:(r + 1) * rows, :], weights[:, p * sub:(p + 1) * sub],
                              preferred_element_type=F32)
                o_ref[r * rows:(r + 1) * rows, p * sub:(p + 1) * sub] = act(acc).astype(BF16)

    is_silu = (j >= silu_j[0]) & (j < silu_j[1])
    is_sigm = (j >= sigm_j[0]) & (j < sigm_j[1])
    pl.when(is_silu)(lambda: plain(_silu, w_ref))
    pl.when(is_sigm)(lambda: plain(jax.nn.sigmoid, wg_ref))
    pl.when((j != 0) & jnp.logical_not(is_conv | is_silu | is_sigm))(lambda: plain(lambda v: v, w_ref))


def _inproj_call(x2, row0, t, norm_w, mod3, cos, sin, w_all, w_gate, w_dt, conv_w, conv_b, conv_off, seq, tm, tn):
    d = x2.shape[1]
    conv_dim = conv_w.shape[1]
    n_lead = conv_off + conv_dim
    n = n_lead + w_gate.shape[1]
    assert w_gate.shape[1] == tn and n_lead % tn == 0
    tiles_per_seq = seq // tm
    off = row0 // tm
    assert tn == 2 * RET_HEADS * RET_QK_DIM, "rotary epilogue expects q and k in the first column tile"
    assert conv_off % tn == 0 and conv_dim % tn == 0
    conv_j0, conv_nj = conv_off // tn, conv_dim // tn
    sub = 512
    g_off = 2 * RET_HEADS * RET_QK_DIM + RET_HEADS * RET_V_DIM
    assert g_off % tn == 0 and (conv_off - g_off) % tn == 0 and (n - conv_off - conv_dim) % tn == 0
    silu_j = (g_off // tn, conv_off // tn)
    sigm_j = ((conv_off + conv_dim) // tn, n // tn)
    kern = functools.partial(_inproj_kernel, conv_j0=conv_j0, conv_nj=conv_nj, silu_j=silu_j, sigm_j=sigm_j,
                             tiles_per_seq=tiles_per_seq, tm=tm, tn=tn, sub=sub)
    conv_idx = lambda i, j: (0, jnp.clip(j - conv_j0, 0, conv_nj - 1))
    return pl.pallas_call(
        kern,
        out_shape=(jax.ShapeDtypeStruct((t, n), BF16), jax.ShapeDtypeStruct((t, LANES), F32)),
        grid=(t // tm, n // tn),
        in_specs=[
            pl.BlockSpec((tm, d), lambda i, j: (i + off, 0)),
            pl.BlockSpec((1, d), lambda i, j: (0, 0)),
            pl.BlockSpec((1, 1, d), lambda i, j: (((i + off) // tiles_per_seq) * N_MOD + 1, 0, 0)),
            pl.BlockSpec((1, 1, d), lambda i, j: (((i + off) // tiles_per_seq) * N_MOD + 0, 0, 0)),
            pl.BlockSpec((tm, LANES), lambda i, j: (i % tiles_per_seq, 0)),
            pl.BlockSpec((tm, LANES), lambda i, j: (i % tiles_per_seq, 0)),
            pl.BlockSpec((d, tn), lambda i, j: (0, jnp.minimum(j, n_lead // tn - 1))),
            pl.BlockSpec((d, tn), lambda i, j: (0, 0)),
            pl.BlockSpec((d, 2 * LANES), lambda i, j: (0, 0)),
            pl.BlockSpec((SSM_CONV, tn), conv_idx),
            pl.BlockSpec((1, tn), conv_idx),
        ],
        out_specs=(pl.BlockSpec((tm, tn), lambda i, j: (i, j)),
                   pl.BlockSpec((tm, LANES), lambda i, j: (i, 0))),
        scratch_shapes=[pltpu.VMEM((tm, d), BF16),
                        pltpu.VMEM((tn // LANES, tm + SUBLANES, LANES), F32),
                        pltpu.VMEM((conv_nj, tn // LANES, SUBLANES, LANES), F32)],
        compiler_params=_params(("arbitrary", "arbitrary")),
        name="inproj",
    )(x2, norm_w, mod3, mod3, cos, sin, w_all, w_gate, w_dt, conv_w, conv_b.reshape(1, conv_dim))


def _retention_body(q_ref, k_ref, v_ref, g_ref, din_ref, dq_ref, dk_ref, w_ref, o_ref, state, decay_c):
    heads = range(RET_HEADS)
    q = [q_ref[:, h * RET_QK_DIM:(h + 1) * RET_QK_DIM] for h in heads]
    k = [k_ref[:, h * RET_QK_DIM:(h + 1) * RET_QK_DIM] for h in heads]
    v = [v_ref[:, h * RET_V_DIM:(h + 1) * RET_V_DIM] for h in heads]
    scores = [(_nt_dot(q[h], k[h]) * din_ref[h]).astype(BF16) for h in heads]
    st = [state[h] for h in heads]
    cross = [jnp.dot(q[h], st[h].astype(BF16), preferred_element_type=F32) * dq_ref[h] for h in heads]
    inner = [jnp.dot(scores[h], v[h], preferred_element_type=F32) for h in heads]
    for h in heads:
        kd = (k[h].astype(F32) * dk_ref[h]).astype(BF16)
        state[h] = st[h] * decay_c[h] + _tn_dot(kd, v[h])
    acc = None
    for h in heads:
        ret = inner[h] + cross[h]
        ret = ret * lax.rsqrt(jnp.mean(ret * ret, axis=-1, keepdims=True) + EPS)
        ret = ret * g_ref[:, h * RET_V_DIM:(h + 1) * RET_V_DIM].astype(F32)
        part = jnp.dot(ret.astype(BF16), w_ref[h * RET_V_DIM:(h + 1) * RET_V_DIM, :],
                       preferred_element_type=F32)
        acc = part if acc is None else acc + part
    o_ref[...] = acc.astype(o_ref.dtype)


def _retention_tables(chunk):
    lg = np.log(1.0 - 2.0 ** (-5.0 - np.arange(RET_HEADS, dtype=np.float64)))
    idx = np.arange(chunk, dtype=np.float64)
    rel = idx[:, None] - idx[None, :]
    causal = rel >= 0
    din = np.where(causal[None], np.exp(np.where(causal, rel, 0.0)[None] * lg[:, None, None]), 0.0)
    dq = np.exp((idx + 1.0)[None, :, None] * lg[:, None, None])
    dk = np.exp((chunk - 1.0 - idx)[None, :, None] * lg[:, None, None])
    dc = tuple(float(v) for v in np.exp(chunk * lg))
    return (jnp.asarray(din, F32), jnp.asarray(dq, F32), jnp.asarray(dk, F32), dc)


def _ssd_decays(dt_ref, dtb_ref, alog_ref, tril_ref, exp_ref, n_sub, chunk):
    dt = jax.nn.softplus(dt_ref[...] + dtb_ref[...])
    adt = dt * (-LOG2_E * jnp.exp(alog_ref[...]))
    p1 = adt.astype(BF16)
    r1 = adt - p1.astype(F32)
    p2 = r1.astype(BF16)
    p3 = (r1 - p2.astype(F32)).astype(BF16)
    pieces = jnp.concatenate([p1, p2, p3], axis=-1)
    acs = []
    for s in range(n_sub):
        c3 = jnp.dot(tril_ref[...], pieces[s * chunk:(s + 1) * chunk, :], preferred_element_type=F32)
        acs.append(c3[:, :LANES] + c3[:, LANES:2 * LANES] + c3[:, 2 * LANES:])
    dt_x = jnp.dot(dt.astype(BF16), exp_ref[...], preferred_element_type=F32)
    return acs, dt_x


def _ssd_body(z_ref, xbc_ref, acs, dt_x, dsk_ref, nw_ref, yn_s, state, chunk, d_inner):
    heads_per_group = d_inner // SSM_HEAD_DIM // SSM_GROUPS
    gw = heads_per_group * SSM_HEAD_DIM
    assert SSM_HEAD_DIM * 2 == LANES and gw == 2 * LANES

    acs_t = acs.T
    li = lax.broadcasted_iota(jnp.int32, (chunk, chunk), 0)
    si = lax.broadcasted_iota(jnp.int32, (chunk, chunk), 1)
    causal = li >= si
    low_half = si < SSM_HEAD_DIM
    lane_g = lax.broadcasted_iota(jnp.int32, (chunk, gw), 1)

    b_off = d_inner
    c_off = d_inner + SSM_GROUPS * SSM_STATE
    for g in range(SSM_GROUPS):
        bm = xbc_ref[:, b_off + g * SSM_STATE: b_off + (g + 1) * SSM_STATE]
        cm = xbc_ref[:, c_off + g * SSM_STATE: c_off + (g + 1) * SSM_STATE]
        xs_g = xbc_ref[:, g * gw:(g + 1) * gw].astype(F32)
        xdt_g = xs_g * dt_x[:, g * gw:(g + 1) * gw]
        xdt_b = xdt_g.astype(BF16)
        cb = _nt_dot(cm, bm)
        cols, ms, xm = [], [], []
        for jh in range(heads_per_group):
            h = g * heads_per_group + jh
            col = jnp.broadcast_to(acs[:, h:h + 1], (chunk, chunk))
            seg = jnp.exp2(jnp.where(causal, col - acs_t[h:h + 1, :], -jnp.inf))
            cols.append(col)
            ms.append((cb * seg).astype(BF16))
            in_head = (lane_g >= jh * SSM_HEAD_DIM) & (lane_g < (jh + 1) * SSM_HEAD_DIM)
            xm.append(jnp.where(in_head, xdt_b, jnp.zeros_like(xdt_b)))
        y_diag = jnp.dot(jnp.concatenate(ms, axis=-1), jnp.concatenate(xm, axis=0),
                         preferred_element_type=F32)
        a_x = jnp.concatenate([jnp.where(low_half, cols[0], cols[1]),
                               jnp.where(low_half, cols[2], cols[3])], axis=-1)
        e_acs_x = jnp.exp2(a_x)
        a_last_x = a_x[chunk - 1:chunk, :]
        st = state[g]
        y_off = jnp.dot(cm, st.astype(BF16), preferred_element_type=F32) * e_acs_x
        xdec = (xdt_g * jnp.exp2(a_last_x - a_x)).astype(BF16)
        state[g] = st * e_acs_x[chunk - 1:chunk, :] + _tn_dot(bm, xdec)
        y = y_diag + y_off + dsk_ref[:, g * gw:(g + 1) * gw] * xs_g
        yz = y * z_ref[:, g * gw:(g + 1) * gw].astype(F32)
        yn = yz * lax.rsqrt(jnp.mean(yz * yz, axis=-1, keepdims=True) + EPS) * nw_ref[:, g * gw:(g + 1) * gw]
        yn_s[:, g * gw:(g + 1) * gw] = yn.astype(BF16)


def _mixers_kernel(q_ref, k_ref, v_ref, g_ref, din_ref, dq_ref, dk_ref, wret_ref,
                   z_ref, xbc_ref, dt_ref, dtb_ref, alog_ref, dsk_ref, nw_ref, tril_ref, exp_ref, wssm_ref,
                   ga_ref, gb_ref, x_ref, gm_ref, scf_ref, shf_ref, nf_ref, wo_ref, wr_ref, br_ref, tri_ref,
                   x1_ref, h2_ref, idx_ref, rank_ref, prow_ref, cnt_ref,
                   rstate, sstate, yn_s, ya_s, yb_s, cnt_s, *, decay_c, chunk, ssm_chunk, d_inner):
    b = pl.program_id(0)
    c = pl.program_id(1)

    @pl.when(c == 0)
    def _():
        rstate[...] = jnp.zeros_like(rstate)
        sstate[...] = jnp.zeros_like(sstate)

    @pl.when((b == 0) & (c == 0))
    def _():
        cnt_s[...] = jnp.zeros_like(cnt_s)

    n_sub = chunk // ssm_chunk
    acs, dt_x = _ssd_decays(dt_ref, dtb_ref, alog_ref, tril_ref, exp_ref, n_sub, ssm_chunk)
    _retention_body(q_ref, k_ref, v_ref, g_ref, din_ref, dq_ref, dk_ref, wret_ref, ya_s, rstate, decay_c)
    for sub in range(n_sub):
        rows = pl.ds(sub * ssm_chunk, ssm_chunk)
        _ssd_body(z_ref.at[rows, :], xbc_ref.at[rows, :], acs[sub],
                  dt_x[sub * ssm_chunk:(sub + 1) * ssm_chunk, :], dsk_ref, nw_ref, yn_s.at[rows, :],
                  sstate, ssm_chunk, d_inner)
    yb_s[...] = jnp.dot(yn_s[...], wssm_ref[...], preferred_element_type=F32)
    _merge_body(ya_s, yb_s, ga_ref, gb_ref, x_ref, gm_ref, scf_ref, shf_ref, nf_ref, wo_ref, wr_ref, br_ref,
                tri_ref, x1_ref, h2_ref, idx_ref, rank_ref, prow_ref, cnt_ref, cnt_s, chunk)


def _mixers_call(proj, dt_raw, x2, row0, mod3, w_ret, dt_bias, a_log, d_skip, ssm_norm, w_ssm,
                 norm_ffn, w_out, w_router_t, b_router, bsz, seq, chunk, ssm_chunk):
    t = proj.shape[0]
    d_inner, d = w_ssm.shape
    conv_dim = d_inner + 2 * SSM_GROUPS * SSM_STATE
    n_heads = d_inner // SSM_HEAD_DIM
    nc = seq // chunk
    gw = d_inner // SSM_GROUPS
    qk_w = RET_HEADS * RET_QK_DIM
    v_w = RET_HEADS * RET_V_DIM
    assert chunk % ssm_chunk == 0
    assert ssm_chunk == LANES, "the per-head decay tiles are built lane-for-lane against the chunk"
    din, dq, dk, dc = _retention_tables(chunk)
    pad_h = lambda v: jnp.pad(v.astype(F32), (0, LANES - n_heads)).reshape(1, LANES)
    tril = jnp.asarray(np.tril(np.ones((ssm_chunk, ssm_chunk), np.float32)), BF16)
    expand = np.zeros((LANES, d_inner), np.float32)
    for h in range(n_heads):
        expand[h, h * SSM_HEAD_DIM:(h + 1) * SSM_HEAD_DIM] = 1.0
    expand = jnp.asarray(expand, BF16)
    kern = functools.partial(_mixers_kernel, decay_c=dc, chunk=chunk, ssm_chunk=ssm_chunk, d_inner=d_inner)
    row = lambda b, c: b * nc + c
    z_blk = (2 * qk_w + 2 * v_w) // d_inner
    xbc_blk = (2 * qk_w + 2 * v_w + d_inner) // conv_dim
    full = lambda shape: pl.BlockSpec(shape, lambda b, c: (0,) * len(shape))
    nt = t // chunk
    ga_blk = proj.shape[1] // d - 2
    seq0 = row0 // seq
    modspec = lambda m: pl.BlockSpec((1, 1, d), lambda b, c: ((b + seq0) * N_MOD + m, 0, 0))
    tri = jnp.asarray(np.triu(np.ones((chunk, chunk), np.float32), 1), BF16)
    return pl.pallas_call(
        kern,
        out_shape=(jax.ShapeDtypeStruct((t, d), F32), jax.ShapeDtypeStruct((t * SLAB_ROWS, LANES), jnp.uint32),
                   jax.ShapeDtypeStruct((nt, SUBLANES, chunk), jnp.int32),
                   jax.ShapeDtypeStruct((nt, SUBLANES, chunk), jnp.int32),
                   jax.ShapeDtypeStruct((t, LANES), F32),
                   jax.ShapeDtypeStruct((N_EXPERTS, LANES), jnp.int32)),
        grid=(bsz, nc),
        in_specs=[
            pl.BlockSpec((chunk, qk_w), lambda b, c: (row(b, c), 0)),
            pl.BlockSpec((chunk, qk_w), lambda b, c: (row(b, c), 1)),
            pl.BlockSpec((chunk, v_w), lambda b, c: (row(b, c), 1)),
            pl.BlockSpec((chunk, v_w), lambda b, c: (row(b, c), 2)),
            full((RET_HEADS, chunk, chunk)), full((RET_HEADS, chunk, 1)), full((RET_HEADS, chunk, 1)),
            full((v_w, d)),
            pl.BlockSpec((chunk, d_inner), lambda b, c: (row(b, c), z_blk)),
            pl.BlockSpec((chunk, conv_dim), lambda b, c: (row(b, c), xbc_blk)),
            pl.BlockSpec((chunk, LANES), lambda b, c: (row(b, c), 0)),
            full((1, LANES)), full((1, LANES)),
            full((1, d_inner)), full((1, d_inner)), full((ssm_chunk, ssm_chunk)), full((LANES, d_inner)),
            full((d_inner, d)),
            pl.BlockSpec((chunk, d), lambda b, c: (row(b, c), ga_blk)),
            pl.BlockSpec((chunk, d), lambda b, c: (row(b, c), ga_blk + 1)),
            pl.BlockSpec((chunk, d), lambda b, c: (row0 // chunk + row(b, c), 0)),
            modspec(2), modspec(4), modspec(3),
            full((1, d)), full((d, d)), full((2 * N_EXPERTS, d)), full((N_EXPERTS, 1)), full((chunk, chunk)),
        ],
        out_specs=(pl.BlockSpec((chunk, d), lambda b, c: (row(b, c), 0)),
                   pl.BlockSpec((chunk * SLAB_ROWS, LANES), lambda b, c: (row(b, c), 0)),
                   pl.BlockSpec((1, SUBLANES, chunk), lambda b, c: (row(b, c), 0, 0)),
                   pl.BlockSpec((1, SUBLANES, chunk), lambda b, c: (row(b, c), 0, 0)),
                   pl.BlockSpec((chunk, LANES), lambda b, c: (row(b, c), 0)),
                   full((N_EXPERTS, LANES))),
        scratch_shapes=[pltpu.VMEM((RET_HEADS, RET_QK_DIM, RET_V_DIM), F32),
                        pltpu.VMEM((SSM_GROUPS, SSM_STATE, gw), F32),
                        pltpu.VMEM((chunk, d_inner), BF16),
                        pltpu.VMEM((chunk, d), F32), pltpu.VMEM((chunk, d), F32),
                        pltpu.VMEM((N_EXPERTS, LANES), F32)],
        compiler_params=_params(("arbitrary", "arbitrary")),
        name="mixers",
    )(proj, proj, proj, proj, din, dq, dk, w_ret,
      proj, proj, dt_raw, pad_h(dt_bias), pad_h(a_log),
      jnp.repeat(d_skip.astype(F32), SSM_HEAD_DIM).reshape(1, d_inner), ssm_norm.reshape(1, d_inner),
      tril, expand, w_ssm,
      proj, proj, x2, mod3, mod3, mod3, norm_ffn, w_out, w_router_t, b_router.reshape(N_EXPERTS, 1), tri)


def _merge_body(ya_ref, yb_ref, ga_ref, gb_ref, x_ref, gm_ref, scf_ref, shf_ref, nw_ref, wo_ref,
                wr_ref, br_ref, tri_ref,
                x1_ref, h2_ref, idx_ref, rank_ref, prow_ref, cnt_ref, cnt_s, tm):
    merged = (ga_ref[...].astype(F32) * ya_ref[...].astype(F32)
              + gb_ref[...].astype(F32) * yb_ref[...].astype(F32))
    mo = jnp.dot(merged.astype(BF16), wo_ref[...], preferred_element_type=F32)
    x1 = x_ref[...] + gm_ref[0] * mo
    x1_ref[...] = x1
    ms = jnp.mean(x1 * x1, axis=-1, keepdims=True)
    h2 = x1 * lax.rsqrt(ms + EPS) * nw_ref[...] * (1.0 + scf_ref[0]) + shf_ref[0]
    _store_slabs(h2_ref, h2, tm)

    h_hi = h2.astype(BF16)
    h_lo = (h2 - h_hi.astype(F32)).astype(BF16)
    lg2 = _nt_dot(wr_ref[...], h_hi)
    lg = lg2[:N_EXPERTS] + lg2[N_EXPERTS:] + _nt_dot(wr_ref[:N_EXPERTS, :], h_lo) + br_ref[...]
    sub = lax.broadcasted_iota(jnp.int32, lg.shape, 0)
    work = lg
    vals, idxs, sels = [], [], []
    for _ in range(TOP_K):
        m = jnp.max(work, axis=0, keepdims=True)
        ik = jnp.min(jnp.where(work == m, sub, N_EXPERTS), axis=0, keepdims=True)
        sel = sub == ik
        vals.append(m)
        idxs.append(ik)
        sels.append(sel)
        work = jnp.where(sel, -jnp.inf, work)
    exps = [jnp.exp(v - vals[0]) for v in vals]
    denom = exps[0]
    for e in exps[1:]:
        denom = denom + e
    probs = [e / denom for e in exps]

    base = cnt_s[:, 0:1]
    ranks = []
    for k in range(TOP_K):
        mk = jnp.where(sels[k], 1.0, 0.0)
        pre = jnp.dot(mk.astype(BF16), tri_ref[...], preferred_element_type=F32)
        ranks.append(jnp.sum(jnp.where(sels[k], pre + base, 0.0), axis=0, keepdims=True))
        base = base + jnp.sum(mk, axis=1, keepdims=True)
    cnt_s[...] = jnp.broadcast_to(base, cnt_s.shape)
    cnt_ref[...] = cnt_s[...].astype(jnp.int32)

    zi = jnp.zeros((SUBLANES - TOP_K, tm), jnp.int32)
    idx_ref[0] = jnp.concatenate(idxs + [zi], axis=0)
    rank_ref[0] = jnp.concatenate([r.astype(jnp.int32) for r in ranks] + [zi], axis=0)
    pt = jnp.concatenate(probs + [jnp.zeros((LANES - TOP_K, tm), F32)], axis=0)
    prow_ref[...] = pt.T


def _sc_mesh():
    return plsc.VectorSubcoreMesh(core_axis_name="c", subcore_axis_name="s")


def _sc_worker():
    return lax.axis_index("s") * SC_CORES + lax.axis_index("c")


def _sc_scatter_rows(rows, dest, n_out):
    t = rows.shape[0]
    n_k = dest.shape[0]
    g = SC_GROUP
    n_w = SC_CORES * SC_SUBCORES
    assert t % (n_w * g) == 0
    cpw = t // (n_w * g)
    dest_w = dest.reshape(n_k, n_w, cpw, g).transpose(1, 0, 2, 3)

    @functools.partial(
        pl.kernel, mesh=_sc_mesh(),
        out_type=jax.ShapeDtypeStruct((n_out,) + rows.shape[1:], rows.dtype),
        scratch_types=[pltpu.VMEM((n_k, cpw, g), jnp.int32),
                       pltpu.VMEM((g,) + rows.shape[1:], rows.dtype),
                       pltpu.SemaphoreType.DMA],
    )
    def scatter(rows_hbm, dest_hbm, out_hbm, idx_v, rows_v, sem):
        wid = _sc_worker()
        pltpu.sync_copy(dest_hbm.at[wid], idx_v)

        @pl.loop(0, cpw)
        def _(cc):
            r0 = pl.multiple_of((wid * cpw + cc) * g, g)
            pltpu.sync_copy(rows_hbm.at[pl.ds(r0, g)], rows_v)
            copies = [pltpu.async_copy(rows_v, out_hbm.at[idx_v.at[k, cc]], sem) for k in range(n_k)]
            for cp in copies:
                cp.wait()

    return scatter(rows, dest_w)


def _sc_gather_rows(table, idx):
    m = idx.shape[0]
    g = SC_GROUP
    n_w = SC_CORES * SC_SUBCORES
    assert m % (n_w * g) == 0
    per_w = m // n_w

    @functools.partial(
        pl.kernel, mesh=_sc_mesh(),
        out_type=jax.ShapeDtypeStruct((m,) + table.shape[1:], table.dtype),
        scratch_types=[pltpu.VMEM((per_w,), jnp.int32),
                       pltpu.VMEM((g,) + table.shape[1:], table.dtype),
                       pltpu.SemaphoreType.DMA],
    )
    def gather(table_hbm, idx_hbm, out_hbm, idx_v, rows_v, sem):
        base = _sc_worker() * per_w
        pltpu.sync_copy(idx_hbm.at[pl.ds(base, per_w)], idx_v)

        @pl.loop(0, per_w // g)
        def _(cc):
            off = pl.multiple_of(cc * g, g)
            pltpu.async_copy(table_hbm.at[idx_v.at[pl.ds(off, g)]], rows_v, sem).wait()
            pltpu.sync_copy(rows_v, out_hbm.at[pl.ds(base + off, g)])

    return gather(table, idx)


def _ffn_kernel(be_ref, br_ref, bv_ref, first_ref, slot_ref, next_ref, rows_ref,
                x_ref, wgu_hbm, bgu_ref, wd_hbm, bd_ref, o_ref, wgu_f, wd_f, wgu_s, wd_s, sem, *, bm, d_ff):
    i = pl.program_id(0)

    def fetch(e, slot):
        return (pltpu.make_async_copy(wgu_hbm.at[e], wgu_f.at[slot], sem.at[0, slot]),
                pltpu.make_async_copy(wd_hbm.at[e], wd_f.at[slot], sem.at[1, slot]))

    @pl.when(i == 0)
    def _():
        for cp in fetch(be_ref[0], slot_ref[0]):
            cp.start()

    @pl.when(first_ref[i] == 1)
    def _():
        slot = slot_ref[i]
        for cp in fetch(be_ref[i], slot):
            cp.wait()
        wgu_s[...] = wgu_f[slot].astype(BF16)
        wd_s[...] = wd_f[slot].astype(BF16)

        @pl.when(next_ref[i] >= 0)
        def _():
            for cp in fetch(next_ref[i], 1 - slot):
                cp.start()

    @pl.when(bv_ref[i] == 1)
    def _():
        keep = lax.broadcasted_iota(jnp.int32, (bm, LANES), 0) < rows_ref[i]
        x = jnp.concatenate(_load_slabs(x_ref, bm, keep=keep), axis=-1).astype(BF16)
        gu = jnp.dot(x, wgu_s[...], preferred_element_type=F32) + bgu_ref[0]
        gate = jnp.minimum(gu[:, :d_ff], SWIGLU_LIMIT)
        up = jnp.clip(gu[:, d_ff:], -SWIGLU_LIMIT, SWIGLU_LIMIT)
        act = gate * jax.nn.sigmoid(SWIGLU_ALPHA * gate) * (up + 1.0)
        y = jnp.dot(act.astype(BF16), wd_s[...], preferred_element_type=F32) + bd_ref[0]
        _store_slabs(o_ref, y, bm)

    @pl.when(bv_ref[i] == 0)
    def _():
        o_ref[...] = jnp.zeros_like(o_ref)


def _ffn_call(blk_e, blk_row, blk_valid, blk_rows, xs, w_gu, b_gu, w_d, b_d, bm):
    n_e, d, f2 = w_gu.shape
    d_ff = f2 // 2
    nb = blk_e.shape[0]
    first = jnp.concatenate([jnp.ones((1,), jnp.int32), (blk_e[1:] != blk_e[:-1]).astype(jnp.int32)])
    slot = (jnp.cumsum(first) - 1) % 2
    later = jnp.where(blk_e[None, :] > blk_e[:, None], blk_e[None, :], n_e)
    nxt = jnp.min(later, axis=1)
    nxt = jnp.where(nxt == n_e, -1, nxt)
    kern = functools.partial(_ffn_kernel, bm=bm, d_ff=d_ff)
    imap = lambda f: (lambda i, be, br, bv, fi, sl, nx, rw: f(i, be, br))
    gs = pltpu.PrefetchScalarGridSpec(
        num_scalar_prefetch=7,
        grid=(nb,),
        in_specs=[pl.BlockSpec((bm * SLAB_ROWS, LANES), imap(lambda i, be, br: (br[i], 0))),
                  pl.BlockSpec(memory_space=pl.ANY),
                  pl.BlockSpec((1, 1, f2), imap(lambda i, be, br: (be[i], 0, 0))),
                  pl.BlockSpec(memory_space=pl.ANY),
                  pl.BlockSpec((1, 1, d), imap(lambda i, be, br: (be[i], 0, 0)))],
        out_specs=pl.BlockSpec((bm * SLAB_ROWS, LANES), imap(lambda i, be, br: (i, 0))),
        scratch_shapes=[pltpu.VMEM((2, d, f2), F32), pltpu.VMEM((2, d_ff, d), F32),
                        pltpu.VMEM((d, f2), BF16), pltpu.VMEM((d_ff, d), BF16),
                        pltpu.SemaphoreType.DMA((2, 2))],
    )
    return pl.pallas_call(
        kern,
        out_shape=jax.ShapeDtypeStruct((nb * bm * SLAB_ROWS, LANES), jnp.uint32),
        grid_spec=gs,
        compiler_params=_params(("arbitrary",)),
        name="ffn",
    )(blk_e, blk_row, blk_valid, first, slot.astype(jnp.int32), nxt.astype(jnp.int32), blk_rows,
      xs, w_gu, b_gu.reshape(n_e, 1, f2), w_d, b_d.reshape(n_e, 1, d))


def _combine_kernel(y0_ref, y1_ref, y2_ref, y3_ref, prow_ref, x1_ref, gf_ref, nw_ref, *rest, tm):
    o_ref = rest[-1]
    p = prow_ref[...]
    pieces = [_load_slabs(y_ref, tm) for y_ref in (y0_ref, y1_ref, y2_ref, y3_ref)]
    for s in range(len(pieces[0])):
        moe = None
        for k in range(TOP_K):
            piece = pieces[k][s] * p[:, k:k + 1]
            moe = piece if moe is None else moe + piece
        sl = slice(s * LANES, (s + 1) * LANES)
        o_ref[:, sl] = x1_ref[:, sl] + gf_ref[0][:, sl] * moe
    xo = o_ref[...]
    o_ref[...] = xo * lax.rsqrt(jnp.mean(xo * xo, axis=-1, keepdims=True) + EPS) * nw_ref[...]


def _combine_call(ytok, prow, x1, mod3, norm_final, seq, tm, row0, t_total, out_prev):
    t, d = x1.shape
    nt = t // tm
    tiles_per_seq = seq // tm
    off = row0 // tm
    kern = functools.partial(_combine_kernel, tm=tm)
    yspec = lambda k: pl.BlockSpec((tm * SLAB_ROWS, LANES), lambda i: (k * nt + i, 0))
    in_specs = [yspec(0), yspec(1), yspec(2), yspec(3),
                pl.BlockSpec((tm, LANES), lambda i: (i, 0)),
                pl.BlockSpec((tm, d), lambda i: (i, 0)),
                pl.BlockSpec((1, 1, d), lambda i: (((i + off) // tiles_per_seq) * N_MOD + 5, 0, 0)),
                pl.BlockSpec((1, d), lambda i: (0, 0))]
    args = [ytok, ytok, ytok, ytok, prow, x1, mod3, norm_final]
    aliases = {}
    if out_prev is not None:
        in_specs.append(pl.BlockSpec(memory_space=pl.ANY))
        aliases = {len(args): 0}
        args.append(out_prev)
    return pl.pallas_call(
        kern,
        out_shape=jax.ShapeDtypeStruct((t_total, d), F32),
        grid=(nt,),
        in_specs=in_specs,
        out_specs=pl.BlockSpec((tm, d), lambda i: (i + off, 0)),
        input_output_aliases=aliases,
        compiler_params=_params(("arbitrary",)),
        name="combine",
    )(*args)


def _plan(seq):
    def fit(pref):
        tm = min(pref, seq)
        assert seq % tm == 0
        return tm
    return dict(tm_in=fit(1024), tm_moe=fit(512), ret_chunk=fit(RET_CHUNK), ssm_chunk=fit(SSM_CHUNK))


def _layer(x2, mod3, bsz, seq, norm_mix, norm_ffn, w_in, conv_w, conv_b, dt_bias, a_log, d_skip, ssm_norm,
           w_ret_out, w_ssm_out, w_out, w_router, b_router, w_gate_up, b_gate_up, w_down, b_down,
           norm_final):
    t, d = x2.shape
    plan = _plan(seq)
    qk_w = RET_HEADS * RET_QK_DIM
    v_w = RET_HEADS * RET_V_DIM
    d_inner = w_ssm_out.shape[0]
    conv_dim = conv_w.shape[1]
    n_heads = d_inner // SSM_HEAD_DIM
    dt_off = 2 * qk_w + 2 * v_w + d_inner + conv_dim

    w_all = w_in.astype(BF16)
    w_gate = w_in[:, dt_off + n_heads:].astype(BF16)
    w_dt = jnp.pad(w_in[:, dt_off:dt_off + n_heads], ((0, 0), (0, LANES - n_heads)))
    w_dt_hi = w_dt.astype(BF16)
    w_dt = jnp.concatenate([w_dt_hi, (w_dt - w_dt_hi.astype(F32)).astype(BF16)], axis=1)
    half = RET_QK_DIM // 2
    inv_freq = ROPE_BASE ** (-jnp.arange(half, dtype=F32) / half)
    ang = jnp.arange(seq, dtype=F32)[:, None] * inv_freq[None, :]
    cos, sin = jnp.cos(ang), jnp.sin(ang)

    w_ret_b, w_ssm_b, w_out_b = w_ret_out.astype(BF16), w_ssm_out.astype(BF16), w_out.astype(BF16)
    w_r_hi = w_router.T.astype(BF16)
    w_router_t = jnp.concatenate([w_r_hi, (w_router.T - w_r_hi.astype(F32)).astype(BF16)], axis=0)
    bm = FFN_BLOCK
    slab = (SLAB_ROWS, LANES)

    def mixer(row0, tg, bg):
        proj, dt_raw = _inproj_call(x2, row0, tg, norm_mix.reshape(1, d), mod3, cos, sin, w_all, w_gate, w_dt,
                                    conv_w, conv_b, 2 * qk_w + 2 * v_w + d_inner, seq, plan["tm_in"], 2 * qk_w)
        x1, h2, idx, rank, prow, cnt = _mixers_call(
            proj, dt_raw, x2, row0, mod3, w_ret_b, dt_bias, a_log, d_skip, ssm_norm, w_ssm_b,
            norm_ffn.reshape(1, d), w_out_b, w_router_t, b_router, bg, seq, plan["ret_chunk"], plan["ssm_chunk"])
        counts = cnt[:, 0]
        padded = ((counts + bm - 1) // bm) * bm
        pad_end = jnp.cumsum(padded)
        start_pad = pad_end - padded
        n_blocks = -(-(tg * TOP_K) // bm) + N_EXPERTS
        e_ids = jnp.arange(N_EXPERTS, dtype=jnp.int32)[:, None, None, None]
        dest = rank + jnp.sum(jnp.where(idx[None] == e_ids, start_pad[:, None, None, None], 0), axis=0)
        dest = dest[:, :TOP_K, :].transpose(1, 0, 2).reshape(TOP_K, tg).astype(jnp.int32)
        n_real = pad_end[-1] // bm
        blk_valid = (jnp.arange(n_blocks) < n_real).astype(jnp.int32)
        blk_row = jnp.minimum(jnp.arange(n_blocks), n_real - 1).astype(jnp.int32)
        blk_e = jnp.minimum(jnp.sum(pad_end[None, :] <= (blk_row * bm)[:, None], axis=1),
                            N_EXPERTS - 1).astype(jnp.int32)
        sel_e = blk_e[:, None] == jnp.arange(N_EXPERTS, dtype=jnp.int32)[None, :]
        row_end = jnp.sum(jnp.where(sel_e, (start_pad + counts)[None, :], 0), axis=1)
        blk_rows = jnp.clip(row_end - blk_row * bm, 0, bm).astype(jnp.int32)
        xs = _sc_scatter_rows(h2.reshape((tg,) + slab), dest, n_blocks * bm)
        return dict(x1=x1, prow=prow, dest=dest, blocks=(blk_e, blk_row, blk_valid, blk_rows), xs=xs,
                    row0=row0)

    def experts(m):
        n_rows = m["xs"].shape[0]
        ys = _ffn_call(*m["blocks"], m["xs"].reshape(n_rows * SLAB_ROWS, LANES),
                       w_gate_up, b_gate_up, w_down, b_down, bm)
        return _sc_gather_rows(ys.reshape((n_rows,) + slab), m["dest"].reshape(-1))

    n_groups = N_GROUPS if bsz % N_GROUPS == 0 else 1
    bg = bsz // n_groups
    tg = bg * seq
    groups = [mixer(g * tg, tg, bg) for g in range(n_groups)]
    ytoks = [experts(m) for m in groups]
    out = None
    for y, m in zip(ytoks, groups):
        out = _combine_call(y.reshape(TOP_K * tg * SLAB_ROWS, LANES), m["prow"], m["x1"], mod3,
                            norm_final.reshape(1, d), seq, plan["tm_moe"], m["row0"], t, out)
    return out


def kernel(x, c, w_ada, b_ada, norm_mix, norm_ffn, w_in, conv_w, conv_b, dt_bias, a_log, d_skip, ssm_norm,
           w_ret_out, w_ssm_out, w_out, w_router, b_router, w_gate_up, b_gate_up, w_down, b_down, norm_final):
    bsz, seq, d = x.shape
    depth = w_ada.shape[0]
    assert depth == 1, "the final norm is fused into the single layer's last kernel"
    x2 = x.reshape(bsz * seq, d)
    l = 0
    mod = _mod_call(c, w_ada[l], b_ada[l])
    mod3 = mod.reshape(bsz * N_MOD, 1, d)
    out = _layer(x2, mod3, bsz, seq, norm_mix[l], norm_ffn[l], w_in[l], conv_w[l], conv_b[l], dt_bias[l],
                 a_log[l], d_skip[l], ssm_norm[l], w_ret_out[l], w_ssm_out[l], w_out[l], w_router[l],
                 b_router[l], w_gate_up[l], b_gate_up[l], w_down[l], b_down[l], norm_final)
    return out.reshape(bsz, seq, d)
```

```python
import functools
import math

import numpy as np
import jax
import jax.numpy as jnp
from jax import lax
from jax.experimental import pallas as pl
from jax.experimental.pallas import tpu as pltpu
from jax.experimental.pallas import tpu_sc as plsc

F32 = jnp.float32
BF16 = jnp.bfloat16
HIGHEST = lax.Precision.HIGHEST

EPS = 1e-6
N_MOD = 6
RET_HEADS = 4
RET_QK_DIM = 256
RET_V_DIM = 512
ROPE_BASE = 10000.0
SSM_HEAD_DIM = 64
SSM_GROUPS = 8
SSM_STATE = 128
SSM_CONV = 4
N_EXPERTS = 32
TOP_K = 4
SWIGLU_LIMIT = 7.0
SWIGLU_ALPHA = 1.702

LANES = 128
SUBLANES = 8
VMEM_LIMIT = 56 * 1024 * 1024

RET_CHUNK = 256
SSM_CHUNK = 128
FFN_BLOCK = 512
SC_CORES = 2
SC_SUBCORES = 16
SC_GROUP = 64
N_GROUPS = 2


def _params(sem, vmem=VMEM_LIMIT):
    return pltpu.CompilerParams(dimension_semantics=sem, vmem_limit_bytes=vmem)


def _nt_dot(a, b, **kw):
    return lax.dot_general(a, b, (((1,), (1,)), ((), ())), preferred_element_type=F32, **kw)


def _tn_dot(a, b, **kw):
    return lax.dot_general(a, b, (((0,), (0,)), ((), ())), preferred_element_type=F32, **kw)


def _silu(v):
    return v * jax.nn.sigmoid(v)


SLAB_ROWS = 4
HIGH_HALF = 0xFFFF0000
LOG2_E = math.log2(math.e)


def _store_slabs(ref, vals, n):
    for s in range(SLAB_ROWS):
        lo = vals[:, s * LANES:(s + 1) * LANES].astype(BF16).astype(F32)
        hi = vals[:, (s + SLAB_ROWS) * LANES:(s + SLAB_ROWS + 1) * LANES].astype(BF16).astype(F32)
        word = (pltpu.bitcast(lo, jnp.uint32) >> 16) | (pltpu.bitcast(hi, jnp.uint32) & jnp.uint32(HIGH_HALF))
        ref[pl.ds(s, n, stride=SLAB_ROWS), :] = word


def _load_slabs(ref, n, base=0, keep=None):
    lo, hi = [], []
    for s in range(SLAB_ROWS):
        word = ref[pl.ds(base + s, n, stride=SLAB_ROWS), :]
        if keep is not None:
            word = jnp.where(keep, word, jnp.zeros_like(word))
        lo.append(pltpu.bitcast(word << 16, F32))
        hi.append(pltpu.bitcast(word & jnp.uint32(HIGH_HALF), F32))
    return lo + hi


def _mod_kernel(c_ref, w_ref, b_ref, o_ref):
    cond = _silu(c_ref[...])
    o_ref[...] = jnp.dot(cond, w_ref[...], preferred_element_type=F32, precision=HIGHEST) + b_ref[...]


def _mod_call(c, w_ada, b_ada):
    bsz, d = c.shape
    n = w_ada.shape[1]
    return pl.pallas_call(
        _mod_kernel,
        out_shape=jax.ShapeDtypeStruct((bsz, n), F32),
        grid=(n // d,),
        in_specs=[pl.BlockSpec((bsz, d), lambda j: (0, 0)),
                  pl.BlockSpec((d, d), lambda j: (0, j)),
                  pl.BlockSpec((1, d), lambda j: (0, j))],
        out_specs=pl.BlockSpec((bsz, d), lambda j: (0, j)),
        compiler_params=_params(("arbitrary",)),
        name="mod",
    )(c, w_ada, b_ada.reshape(1, n))


def _inproj_kernel(x_ref, nw_ref, sc_ref, sh_ref, cos_ref, sin_ref, w_ref, wg_ref, wdt_ref, cw_ref, cb_ref,
                   o_ref, dt_ref, h_s, work, carry, *, conv_j0, conv_nj, silu_j, sigm_j, tiles_per_seq,
                   tm, tn, sub):
    i = pl.program_id(0)
    j = pl.program_id(1)
    n_dt = dt_ref.shape[1]
    rows = min(tm, 256)

    @pl.when(j == 0)
    def _():
        xf = x_ref[...]
        ms = jnp.mean(xf * xf, axis=-1, keepdims=True)
        y = xf * lax.rsqrt(ms + EPS) * nw_ref[...]
        hm = y * (1.0 + sc_ref[0]) + sh_ref[0]
        hb = hm.astype(BF16)
        h_s[...] = hb
        h_lo = (hm - hb.astype(F32)).astype(BF16)
        d_hi = jnp.dot(hb, wdt_ref[...], preferred_element_type=F32)
        d_lo = jnp.dot(h_lo, wdt_ref[:, :n_dt], preferred_element_type=F32)
        dt_ref[...] = d_hi[:, :n_dt] + d_hi[:, n_dt:] + d_lo
        half = RET_QK_DIM // 2
        for p in range(tn // sub):
            for r in range(tm // rows):
                rs = slice(r * rows, (r + 1) * rows)
                acc = jnp.dot(h_s[rs, :], w_ref[:, p * sub:(p + 1) * sub], preferred_element_type=F32)
                cos = cos_ref[rs, :]
                sin = sin_ref[rs, :]
                for cc in range(sub // RET_QK_DIM):
                    c = p * (sub // RET_QK_DIM) + cc
                    a = acc[:, cc * RET_QK_DIM: cc * RET_QK_DIM + half]
                    b = acc[:, cc * RET_QK_DIM + half: (cc + 1) * RET_QK_DIM]
                    scale = 1.0 if c < RET_HEADS else RET_QK_DIM ** -0.5
                    o_ref[rs, c * RET_QK_DIM: c * RET_QK_DIM + half] = ((a * cos - b * sin) * scale).astype(BF16)
                    o_ref[rs, c * RET_QK_DIM + half: (c + 1) * RET_QK_DIM] = (
                        (a * sin + b * cos) * scale).astype(BF16)

    is_conv = (j >= conv_j0) & (j < conv_j0 + conv_nj)

    @pl.when(is_conv)
    def _():
        cj = j - conv_j0
        pad = SUBLANES

        @pl.when(i % tiles_per_seq == 0)
        def _():
            carry[cj] = jnp.zeros(carry.shape[1:], F32)

        for p in range(tn // sub):
            for r in range(tm // rows):
                r0 = r * rows
                acc = jnp.dot(h_s[r0:r0 + rows, :], w_ref[:, p * sub:(p + 1) * sub], preferred_element_type=F32)
                for cc in range(sub // LANES):
                    c = p * (sub // LANES) + cc
                    cols = slice(c * LANES, (c + 1) * LANES)
                    if r == 0:
                        work[c, 0:pad, :] = carry[cj, c]
                    lo = pad + r0
                    work[c, lo:lo + rows, :] = acc[:, cc * LANES:(cc + 1) * LANES]
                    conv = cb_ref[:, cols] + cw_ref[SSM_CONV - 1:SSM_CONV, cols] * work[c, lo:lo + rows, :]
                    for k in range(SSM_CONV - 1):
                        shift = SSM_CONV - 1 - k
                        conv = conv + cw_ref[k:k + 1, cols] * work[c, lo - shift:lo - shift + rows, :]
                    if r0 + rows == tm:
                        carry[cj, c] = work[c, tm:tm + pad, :]
                    o_ref[r0:r0 + rows, cols] = _silu(conv).astype(BF16)

    def plain(act, weights):
        for p in range(tn // sub):
            for r in range(tm // rows):
                acc = jnp.dot(h_s[r * rows:(r + 1) * rows, :], weights[:, p * sub:(p + 1) * sub],
                              preferred_element_type=F32)
                o_ref[r * rows:(r + 1) * rows, p * sub:(p + 1) * sub] = act(acc).astype(BF16)

    is_silu = (j >= silu_j[0]) & (j < silu_j[1])
    is_sigm = (j >= sigm_j[0]) & (j < sigm_j[1])
    pl.when(is_silu)(lambda: plain(_silu, w_ref))
    pl.when(is_sigm)(lambda: plain(jax.nn.sigmoid, wg_ref))
    pl.when((j != 0) & jnp.logical_not(is_conv | is_silu | is_sigm))(lambda: plain(lambda v: v, w_ref))


def _inproj_call(x2, row0, t, norm_w, mod3, cos, sin, w_all, w_gate, w_dt, conv_w, conv_b, conv_off, seq, tm, tn):
    d = x2.shape[1]
    conv_dim = conv_w.shape[1]
    n_lead = conv_off + conv_dim
    n = n_lead + w_gate.shape[1]
    assert w_gate.shape[1] == tn and n_lead % tn == 0
    tiles_per_seq = seq // tm
    off = row0 // tm
    assert tn == 2 * RET_HEADS * RET_QK_DIM, "rotary epilogue expects q and k in the first column tile"
    assert conv_off % tn == 0 and conv_dim % tn == 0
    conv_j0, conv_nj = conv_off // tn, conv_dim // tn
    sub = 512
    g_off = 2 * RET_HEADS * RET_QK_DIM + RET_HEADS * RET_V_DIM
    assert g_off % tn == 0 and (conv_off - g_off) % tn == 0 and (n - conv_off - conv_dim) % tn == 0
    silu_j = (g_off // tn, conv_off // tn)
    sigm_j = ((conv_off + conv_dim) // tn, n // tn)
    kern = functools.partial(_inproj_kernel, conv_j0=conv_j0, conv_nj=conv_nj, silu_j=silu_j, sigm_j=sigm_j,
                             tiles_per_seq=tiles_per_seq, tm=tm, tn=tn, sub=sub)
    conv_idx = lambda i, j: (0, jnp.clip(j - conv_j0, 0, conv_nj - 1))
    return pl.pallas_call(
        kern,
        out_shape=(jax.ShapeDtypeStruct((t, n), BF16), jax.ShapeDtypeStruct((t, LANES), F32)),
        grid=(t // tm, n // tn),
        in_specs=[
            pl.BlockSpec((tm, d), lambda i, j: (i + off, 0)),
            pl.BlockSpec((1, d), lambda i, j: (0, 0)),
            pl.BlockSpec((1, 1, d), lambda i, j: (((i + off) // tiles_per_seq) * N_MOD + 1, 0, 0)),
            pl.BlockSpec((1, 1, d), lambda i, j: (((i + off) // tiles_per_seq) * N_MOD + 0, 0, 0)),
            pl.BlockSpec((tm, LANES), lambda i, j: (i % tiles_per_seq, 0)),
            pl.BlockSpec((tm, LANES), lambda i, j: (i % tiles_per_seq, 0)),
            pl.BlockSpec((d, tn), lambda i, j: (0, jnp.minimum(j, n_lead // tn - 1))),
            pl.BlockSpec((d, tn), lambda i, j: (0, 0)),
            pl.BlockSpec((d, 2 * LANES), lambda i, j: (0, 0)),
            pl.BlockSpec((SSM_CONV, tn), conv_idx),
            pl.BlockSpec((1, tn), conv_idx),
        ],
        out_specs=(pl.BlockSpec((tm, tn), lambda i, j: (i, j)),
                   pl.BlockSpec((tm, LANES), lambda i, j: (i, 0))),
        scratch_shapes=[pltpu.VMEM((tm, d), BF16),
                        pltpu.VMEM((tn // LANES, tm + SUBLANES, LANES), F32),
                        pltpu.VMEM((conv_nj, tn // LANES, SUBLANES, LANES), F32)],
        compiler_params=_params(("arbitrary", "arbitrary")),
        name="inproj",
    )(x2, norm_w, mod3, mod3, cos, sin, w_all, w_gate, w_dt, conv_w, conv_b.reshape(1, conv_dim))


def _retention_body(q_ref, k_ref, v_ref, g_ref, din_ref, dq_ref, dk_ref, w_ref, o_ref, state, decay_c):
    heads = range(RET_HEADS)
    q = [q_ref[:, h * RET_QK_DIM:(h + 1) * RET_QK_DIM] for h in heads]
    k = [k_ref[:, h * RET_QK_DIM:(h + 1) * RET_QK_DIM] for h in heads]
    v = [v_ref[:, h * RET_V_DIM:(h + 1) * RET_V_DIM] for h in heads]
    scores = [(_nt_dot(q[h], k[h]) * din_ref[h]).astype(BF16) for h in heads]
    st = [state[h] for h in heads]
    cross = [jnp.dot(q[h], st[h].astype(BF16), preferred_element_type=F32) * dq_ref[h] for h in heads]
    inner = [jnp.dot(scores[h], v[h], preferred_element_type=F32) for h in heads]
    for h in heads:
        kd = (k[h].astype(F32) * dk_ref[h]).astype(BF16)
        state[h] = st[h] * decay_c[h] + _tn_dot(kd, v[h])
    acc = None
    for h in heads:
        ret = inner[h] + cross[h]
        ret = ret * lax.rsqrt(jnp.mean(ret * ret, axis=-1, keepdims=True) + EPS)
        ret = ret * g_ref[:, h * RET_V_DIM:(h + 1) * RET_V_DIM].astype(F32)
        part = jnp.dot(ret.astype(BF16), w_ref[h * RET_V_DIM:(h + 1) * RET_V_DIM, :],
                       preferred_element_type=F32)
        acc = part if acc is None else acc + part
    o_ref[...] = acc.astype(o_ref.dtype)


def _retention_tables(chunk):
    lg = np.log(1.0 - 2.0 ** (-5.0 - np.arange(RET_HEADS, dtype=np.float64)))
    idx = np.arange(chunk, dtype=np.float64)
    rel = idx[:, None] - idx[None, :]
    causal = rel >= 0
    din = np.where(causal[None], np.exp(np.where(causal, rel, 0.0)[None] * lg[:, None, None]), 0.0)
    dq = np.exp((idx + 1.0)[None, :, None] * lg[:, None, None])
    dk = np.exp((chunk - 1.0 - idx)[None, :, None] * lg[:, None, None])
    dc = tuple(float(v) for v in np.exp(chunk * lg))
    return (jnp.asarray(din, F32), jnp.asarray(dq, F32), jnp.asarray(dk, F32), dc)


def _ssd_decays(dt_ref, dtb_ref, alog_ref, tril_ref, exp_ref, n_sub, chunk):
    dt = jax.nn.softplus(dt_ref[...] + dtb_ref[...])
    adt = dt * (-LOG2_E * jnp.exp(alog_ref[...]))
    p1 = adt.astype(BF16)
    r1 = adt - p1.astype(F32)
    p2 = r1.astype(BF16)
    p3 = (r1 - p2.astype(F32)).astype(BF16)
    pieces = jnp.concatenate([p1, p2, p3], axis=-1)
    acs = []
    for s in range(n_sub):
        c3 = jnp.dot(tril_ref[...], pieces[s * chunk:(s + 1) * chunk, :], preferred_element_type=F32)
        acs.append(c3[:, :LANES] + c3[:, LANES:2 * LANES] + c3[:, 2 * LANES:])
    dt_x = jnp.dot(dt.astype(BF16), exp_ref[...], preferred_element_type=F32)
    return acs, dt_x


def _ssd_body(z_ref, xbc_ref, acs, dt_x, dsk_ref, nw_ref, yn_s, state, chunk, d_inner):
    heads_per_group = d_inner // SSM_HEAD_DIM // SSM_GROUPS
    gw = heads_per_group * SSM_HEAD_DIM
    assert SSM_HEAD_DIM * 2 == LANES and gw == 2 * LANES

    acs_t = acs.T
    li = lax.broadcasted_iota(jnp.int32, (chunk, chunk), 0)
    si = lax.broadcasted_iota(jnp.int32, (chunk, chunk), 1)
    causal = li >= si
    low_half = si < SSM_HEAD_DIM
    lane_g = lax.broadcasted_iota(jnp.int32, (chunk, gw), 1)

    b_off = d_inner
    c_off = d_inner + SSM_GROUPS * SSM_STATE
    for g in range(SSM_GROUPS):
        bm = xbc_ref[:, b_off + g * SSM_STATE: b_off + (g + 1) * SSM_STATE]
        cm = xbc_ref[:, c_off + g * SSM_STATE: c_off + (g + 1) * SSM_STATE]
        xs_g = xbc_ref[:, g * gw:(g + 1) * gw].astype(F32)
        xdt_g = xs_g * dt_x[:, g * gw:(g + 1) * gw]
        xdt_b = xdt_g.astype(BF16)
        cb = _nt_dot(cm, bm)
        cols, ms, xm = [], [], []
        for jh in range(heads_per_group):
            h = g * heads_per_group + jh
            col = jnp.broadcast_to(acs[:, h:h + 1], (chunk, chunk))
            seg = jnp.exp2(jnp.where(causal, col - acs_t[h:h + 1, :], -jnp.inf))
            cols.append(col)
            ms.append((cb * seg).astype(BF16))
            in_head = (lane_g >= jh * SSM_HEAD_DIM) & (lane_g < (jh + 1) * SSM_HEAD_DIM)
            xm.append(jnp.where(in_head, xdt_b, jnp.zeros_like(xdt_b)))
        y_diag = jnp.dot(jnp.concatenate(ms, axis=-1), jnp.concatenate(xm, axis=0),
                         preferred_element_type=F32)
        a_x = jnp.concatenate([jnp.where(low_half, cols[0], cols[1]),
                               jnp.where(low_half, cols[2], cols[3])], axis=-1)
        e_acs_x = jnp.exp2(a_x)
        a_last_x = a_x[chunk - 1:chunk, :]
        st = state[g]
        y_off = jnp.dot(cm, st.astype(BF16), preferred_element_type=F32) * e_acs_x
        xdec = (xdt_g * jnp.exp2(a_last_x - a_x)).astype(BF16)
        state[g] = st * e_acs_x[chunk - 1:chunk, :] + _tn_dot(bm, xdec)
        y = y_diag + y_off + dsk_ref[:, g * gw:(g + 1) * gw] * xs_g
        yz = y * z_ref[:, g * gw:(g + 1) * gw].astype(F32)
        yn = yz * lax.rsqrt(jnp.mean(yz * yz, axis=-1, keepdims=True) + EPS) * nw_ref[:, g * gw:(g + 1) * gw]
        yn_s[:, g * gw:(g + 1) * gw] = yn.astype(BF16)


def _mixers_kernel(q_ref, k_ref, v_ref, g_ref, din_ref, dq_ref, dk_ref, wret_ref,
                   z_ref, xbc_ref, dt_ref, dtb_ref, alog_ref, dsk_ref, nw_ref, tril_ref, exp_ref, wssm_ref,
                   ga_ref, gb_ref, x_ref, gm_ref, scf_ref, shf_ref, nf_ref, wo_ref, wr_ref, br_ref, tri_ref,
                   x1_ref, h2_ref, idx_ref, rank_ref, prow_ref, cnt_ref,
                   rstate, sstate, yn_s, ya_s, yb_s, cnt_s, *, decay_c, chunk, ssm_chunk, d_inner):
    b = pl.program_id(0)
    c = pl.program_id(1)

    @pl.when(c == 0)
    def _():
        rstate[...] = jnp.zeros_like(rstate)
        sstate[...] = jnp.zeros_like(sstate)

    @pl.when((b == 0) & (c == 0))
    def _():
        cnt_s[...] = jnp.zeros_like(cnt_s)

    n_sub = chunk // ssm_chunk
    acs, dt_x = _ssd_decays(dt_ref, dtb_ref, alog_ref, tril_ref, exp_ref, n_sub, ssm_chunk)
    _retention_body(q_ref, k_ref, v_ref, g_ref, din_ref, dq_ref, dk_ref, wret_ref, ya_s, rstate, decay_c)
    for sub in range(n_sub):
        rows = pl.ds(sub * ssm_chunk, ssm_chunk)
        _ssd_body(z_ref.at[rows, :], xbc_ref.at[rows, :], acs[sub],
                  dt_x[sub * ssm_chunk:(sub + 1) * ssm_chunk, :], dsk_ref, nw_ref, yn_s.at[rows, :],
                  sstate, ssm_chunk, d_inner)
    yb_s[...] = jnp.dot(yn_s[...], wssm_ref[...], preferred_element_type=F32)
    _merge_body(ya_s, yb_s, ga_ref, gb_ref, x_ref, gm_ref, scf_ref, shf_ref, nf_ref, wo_ref, wr_ref, br_ref,
                tri_ref, x1_ref, h2_ref, idx_ref, rank_ref, prow_ref, cnt_ref, cnt_s, chunk)


def _mixers_call(proj, dt_raw, x2, row0, mod3, w_ret, dt_bias, a_log, d_skip, ssm_norm, w_ssm,
                 norm_ffn, w_out, w_router_t, b_router, bsz, seq, chunk, ssm_chunk):
    t = proj.shape[0]
    d_inner, d = w_ssm.shape
    conv_dim = d_inner + 2 * SSM_GROUPS * SSM_STATE
    n_heads = d_inner // SSM_HEAD_DIM
    nc = seq // chunk
    gw = d_inner // SSM_GROUPS
    qk_w = RET_HEADS * RET_QK_DIM
    v_w = RET_HEADS * RET_V_DIM
    assert chunk % ssm_chunk == 0
    assert ssm_chunk == LANES, "the per-head decay tiles are built lane-for-lane against the chunk"
    din, dq, dk, dc = _retention_tables(chunk)
    pad_h = lambda v: jnp.pad(v.astype(F32), (0, LANES - n_heads)).reshape(1, LANES)
    tril = jnp.asarray(np.tril(np.ones((ssm_chunk, ssm_chunk), np.float32)), BF16)
    expand = np.zeros((LANES, d_inner), np.float32)
    for h in range(n_heads):
        expand[h, h * SSM_HEAD_DIM:(h + 1) * SSM_HEAD_DIM] = 1.0
    expand = jnp.asarray(expand, BF16)
    kern = functools.partial(_mixers_kernel, decay_c=dc, chunk=chunk, ssm_chunk=ssm_chunk, d_inner=d_inner)
    row = lambda b, c: b * nc + c
    z_blk = (2 * qk_w + 2 * v_w) // d_inner
    xbc_blk = (2 * qk_w + 2 * v_w + d_inner) // conv_dim
    full = lambda shape: pl.BlockSpec(shape, lambda b, c: (0,) * len(shape))
    nt = t // chunk
    ga_blk = proj.shape[1] // d - 2
    seq0 = row0 // seq
    modspec = lambda m: pl.BlockSpec((1, 1, d), lambda b, c: ((b + seq0) * N_MOD + m, 0, 0))
    tri = jnp.asarray(np.triu(np.ones((chunk, chunk), np.float32), 1), BF16)
    return pl.pallas_call(
        kern,
        out_shape=(jax.ShapeDtypeStruct((t, d), F32), jax.ShapeDtypeStruct((t * SLAB_ROWS, LANES), jnp.uint32),
                   jax.ShapeDtypeStruct((nt, SUBLANES, chunk), jnp.int32),
                   jax.ShapeDtypeStruct((nt, SUBLANES, chunk), jnp.int32),
                   jax.ShapeDtypeStruct((t, LANES), F32),
                   jax.ShapeDtypeStruct((N_EXPERTS, LANES), jnp.int32)),
        grid=(bsz, nc),
        in_specs=[
            pl.BlockSpec((chunk, qk_w), lambda b, c: (row(b, c), 0)),
            pl.BlockSpec((chunk, qk_w), lambda b, c: (row(b, c), 1)),
            pl.BlockSpec((chunk, v_w), lambda b, c: (row(b, c), 1)),
            pl.BlockSpec((chunk, v_w), lambda b, c: (row(b, c), 2)),
            full((RET_HEADS, chunk, chunk)), full((RET_HEADS, chunk, 1)), full((RET_HEADS, chunk, 1)),
            full((v_w, d)),
            pl.BlockSpec((chunk, d_inner), lambda b, c: (row(b, c), z_blk)),
            pl.BlockSpec((chunk, conv_dim), lambda b, c: (row(b, c), xbc_blk)),
            pl.BlockSpec((chunk, LANES), lambda b, c: (row(b, c), 0)),
            full((1, LANES)), full((1, LANES)),
            full((1, d_inner)), full((1, d_inner)), full((ssm_chunk, ssm_chunk)), full((LANES, d_inner)),
            full((d_inner, d)),
            pl.BlockSpec((chunk, d), lambda b, c: (row(b, c), ga_blk)),
            pl.BlockSpec((chunk, d), lambda b, c: (row(b, c), ga_blk + 1)),
            pl.BlockSpec((chunk, d), lambda b, c: (row0 // chunk + row(b, c), 0)),
            modspec(2), modspec(4), modspec(3),
            full((1, d)), full((d, d)), full((2 * N_EXPERTS, d)), full((N_EXPERTS, 1)), full((chunk, chunk)),
        ],
        out_specs=(pl.BlockSpec((chunk, d), lambda b, c: (row(b, c), 0)),
                   pl.BlockSpec((chunk * SLAB_ROWS, LANES), lambda b, c: (row(b, c), 0)),
                   pl.BlockSpec((1, SUBLANES, chunk), lambda b, c: (row(b, c), 0, 0)),
                   pl.BlockSpec((1, SUBLANES, chunk), lambda b, c: (row(b, c), 0, 0)),
                   pl.BlockSpec((chunk, LANES), lambda b, c: (row(b, c), 0)),
                   full((N_EXPERTS, LANES))),
        scratch_shapes=[pltpu.VMEM((RET_HEADS, RET_QK_DIM, RET_V_DIM), F32),
                        pltpu.VMEM((SSM_GROUPS, SSM_STATE, gw), F32),
                        pltpu.VMEM((chunk, d_inner), BF16),
                        pltpu.VMEM((chunk, d), F32), pltpu.VMEM((chunk, d), F32),
                        pltpu.VMEM((N_EXPERTS, LANES), F32)],
        compiler_params=_params(("arbitrary", "arbitrary")),
        name="mixers",
    )(proj, proj, proj, proj, din, dq, dk, w_ret,
      proj, proj, dt_raw, pad_h(dt_bias), pad_h(a_log),
      jnp.repeat(d_skip.astype(F32), SSM_HEAD_DIM).reshape(1, d_inner), ssm_norm.reshape(1, d_inner),
      tril, expand, w_ssm,
      proj, proj, x2, mod3, mod3, mod3, norm_ffn, w_out, w_router_t, b_router.reshape(N_EXPERTS, 1), tri)


def _merge_body(ya_ref, yb_ref, ga_ref, gb_ref, x_ref, gm_ref, scf_ref, shf_ref, nw_ref, wo_ref,
                wr_ref, br_ref, tri_ref,
                x1_ref, h2_ref, idx_ref, rank_ref, prow_ref, cnt_ref, cnt_s, tm):
    merged = (ga_ref[...].astype(F32) * ya_ref[...].astype(F32)
              + gb_ref[...].astype(F32) * yb_ref[...].astype(F32))
    mo = jnp.dot(merged.astype(BF16), wo_ref[...], preferred_element_type=F32)
    x1 = x_ref[...] + gm_ref[0] * mo
    x1_ref[...] = x1
    ms = jnp.mean(x1 * x1, axis=-1, keepdims=True)
    h2 = x1 * lax.rsqrt(ms + EPS) * nw_ref[...] * (1.0 + scf_ref[0]) + shf_ref[0]
    _store_slabs(h2_ref, h2, tm)

    h_hi = h2.astype(BF16)
    h_lo = (h2 - h_hi.astype(F32)).astype(BF16)
    lg2 = _nt_dot(wr_ref[...], h_hi)
    lg = lg2[:N_EXPERTS] + lg2[N_EXPERTS:] + _nt_dot(wr_ref[:N_EXPERTS, :], h_lo) + br_ref[...]
    sub = lax.broadcasted_iota(jnp.int32, lg.shape, 0)
    work = lg
    vals, idxs, sels = [], [], []
    for _ in range(TOP_K):
        m = jnp.max(work, axis=0, keepdims=True)
        ik = jnp.min(jnp.where(work == m, sub, N_EXPERTS), axis=0, keepdims=True)
        sel = sub == ik
        vals.append(m)
        idxs.append(ik)
        sels.append(sel)
        work = jnp.where(sel, -jnp.inf, work)
    exps = [jnp.exp(v - vals[0]) for v in vals]
    denom = exps[0]
    for e in exps[1:]:
        denom = denom + e
    probs = [e / denom for e in exps]

    base = cnt_s[:, 0:1]
    ranks = []
    for k in range(TOP_K):
        mk = jnp.where(sels[k], 1.0, 0.0)
        pre = jnp.dot(mk.astype(BF16), tri_ref[...], preferred_element_type=F32)
        ranks.append(jnp.sum(jnp.where(sels[k], pre + base, 0.0), axis=0, keepdims=True))
        base = base + jnp.sum(mk, axis=1, keepdims=True)
    cnt_s[...] = jnp.broadcast_to(base, cnt_s.shape)
    cnt_ref[...] = cnt_s[...].astype(jnp.int32)

    zi = jnp.zeros((SUBLANES - TOP_K, tm), jnp.int32)
    idx_ref[0] = jnp.concatenate(idxs + [zi], axis=0)
    rank_ref[0] = jnp.concatenate([r.astype(jnp.int32) for r in ranks] + [zi], axis=0)
    pt = jnp.concatenate(probs + [jnp.zeros((LANES - TOP_K, tm), F32)], axis=0)
    prow_ref[...] = pt.T


def _sc_mesh():
    return plsc.VectorSubcoreMesh(core_axis_name="c", subcore_axis_name="s")


def _sc_worker():
    return lax.axis_index("s") * SC_CORES + lax.axis_index("c")


def _sc_scatter_rows(rows, dest, n_out):
    t = rows.shape[0]
    n_k = dest.shape[0]
    g = SC_GROUP
    n_w = SC_CORES * SC_SUBCORES
    assert t % (n_w * g) == 0
    cpw = t // (n_w * g)
    dest_w = dest.reshape(n_k, n_w, cpw, g).transpose(1, 0, 2, 3)

    @functools.partial(
        pl.kernel, mesh=_sc_mesh(),
        out_type=jax.ShapeDtypeStruct((n_out,) + rows.shape[1:], rows.dtype),
        scratch_types=[pltpu.VMEM((n_k, cpw, g), jnp.int32),
                       pltpu.VMEM((g,) + rows.shape[1:], rows.dtype),
                       pltpu.SemaphoreType.DMA],
    )
    def scatter(rows_hbm, dest_hbm, out_hbm, idx_v, rows_v, sem):
        wid = _sc_worker()
        pltpu.sync_copy(dest_hbm.at[wid], idx_v)

        @pl.loop(0, cpw)
        def _(cc):
            r0 = pl.multiple_of((wid * cpw + cc) * g, g)
            pltpu.sync_copy(rows_hbm.at[pl.ds(r0, g)], rows_v)
            copies = [pltpu.async_copy(rows_v, out_hbm.at[idx_v.at[k, cc]], sem) for k in range(n_k)]
            for cp in copies:
                cp.wait()

    return scatter(rows, dest_w)


def _sc_gather_rows(table, idx):
    m = idx.shape[0]
    g = SC_GROUP
    n_w = SC_CORES * SC_SUBCORES
    assert m % (n_w * g) == 0
    per_w = m // n_w

    @functools.partial(
        pl.kernel, mesh=_sc_mesh(),
        out_type=jax.ShapeDtypeStruct((m,) + table.shape[1:], table.dtype),
        scratch_types=[pltpu.VMEM((per_w,), jnp.int32),
                       pltpu.VMEM((g,) + table.shape[1:], table.dtype),
                       pltpu.SemaphoreType.DMA],
    )
    def gather(table_hbm, idx_hbm, out_hbm, idx_v, rows_v, sem):
        base = _sc_worker() * per_w
        pltpu.sync_copy(idx_hbm.at[pl.ds(base, per_w)], idx_v)

        @pl.loop(0, per_w // g)
        def _(cc):
            off = pl.multiple_of(cc * g, g)
            pltpu.async_copy(table_hbm.at[idx_v.at[pl.ds(off, g)]], rows_v, sem).wait()
            pltpu.sync_copy(rows_v, out_hbm.at[pl.ds(base + off, g)])

    return gather(table, idx)


def _ffn_kernel(be_ref, br_ref, bv_ref, first_ref, slot_ref, next_ref, rows_ref,
                x_ref, wgu_hbm, bgu_ref, wd_hbm, bd_ref, o_ref, wgu_f, wd_f, wgu_s, wd_s, sem, *, bm, d_ff):
    i = pl.program_id(0)

    def fetch(e, slot):
        return (pltpu.make_async_copy(wgu_hbm.at[e], wgu_f.at[slot], sem.at[0, slot]),
                pltpu.make_async_copy(wd_hbm.at[e], wd_f.at[slot], sem.at[1, slot]))

    @pl.when(i == 0)
    def _():
        for cp in fetch(be_ref[0], slot_ref[0]):
            cp.start()

    @pl.when(first_ref[i] == 1)
    def _():
        slot = slot_ref[i]
        for cp in fetch(be_ref[i], slot):
            cp.wait()
        wgu_s[...] = wgu_f[slot].astype(BF16)
        wd_s[...] = wd_f[slot].astype(BF16)

        @pl.when(next_ref[i] >= 0)
        def _():
            for cp in fetch(next_ref[i], 1 - slot):
                cp.start()

    def expert_mlp(keep):
        x = jnp.concatenate(_load_slabs(x_ref, bm, keep=keep), axis=-1).astype(BF16)
        gu = jnp.dot(x, wgu_s[...], preferred_element_type=F32) + bgu_ref[0]
        gate = jnp.minimum(gu[:, :d_ff], SWIGLU_LIMIT)
        up = jnp.clip(gu[:, d_ff:], -SWIGLU_LIMIT, SWIGLU_LIMIT)
        act = gate * jax.nn.sigmoid(SWIGLU_ALPHA * gate) * (up + 1.0)
        y = jnp.dot(act.astype(BF16), wd_s[...], preferred_element_type=F32) + bd_ref[0]
        _store_slabs(o_ref, y, bm)

    @pl.when((bv_ref[i] == 1) & (rows_ref[i] == bm))
    def _():
        expert_mlp(None)

    @pl.when((bv_ref[i] == 1) & (rows_ref[i] < bm))
    def _():
        expert_mlp(lax.broadcasted_iota(jnp.int32, (bm, LANES), 0) < rows_ref[i])

    @pl.when(bv_ref[i] == 0)
    def _():
        o_ref[...] = jnp.zeros_like(o_ref)


def _ffn_call(blk_e, blk_row, blk_valid, blk_rows, xs, w_gu, b_gu, w_d, b_d, bm):
    n_e, d, f2 = w_gu.shape
    d_ff = f2 // 2
    nb = blk_e.shape[0]
    first = jnp.concatenate([jnp.ones((1,), jnp.int32), (blk_e[1:] != blk_e[:-1]).astype(jnp.int32)])
    slot = (jnp.cumsum(first) - 1) % 2
    later = jnp.where(blk_e[None, :] > blk_e[:, None], blk_e[None, :], n_e)
    nxt = jnp.min(later, axis=1)
    nxt = jnp.where(nxt == n_e, -1, nxt)
    kern = functools.partial(_ffn_kernel, bm=bm, d_ff=d_ff)
    imap = lambda f: (lambda i, be, br, bv, fi, sl, nx, rw: f(i, be, br))
    gs = pltpu.PrefetchScalarGridSpec(
        num_scalar_prefetch=7,
        grid=(nb,),
        in_specs=[pl.BlockSpec((bm * SLAB_ROWS, LANES), imap(lambda i, be, br: (br[i], 0))),
                  pl.BlockSpec(memory_space=pl.ANY),
                  pl.BlockSpec((1, 1, f2), imap(lambda i, be, br: (be[i], 0, 0))),
                  pl.BlockSpec(memory_space=pl.ANY),
                  pl.BlockSpec((1, 1, d), imap(lambda i, be, br: (be[i], 0, 0)))],
        out_specs=pl.BlockSpec((bm * SLAB_ROWS, LANES), imap(lambda i, be, br: (i, 0))),
        scratch_shapes=[pltpu.VMEM((2, d, f2), F32), pltpu.VMEM((2, d_ff, d), F32),
                        pltpu.VMEM((d, f2), BF16), pltpu.VMEM((d_ff, d), BF16),
                        pltpu.SemaphoreType.DMA((2, 2))],
    )
    return pl.pallas_call(
        kern,
        out_shape=jax.ShapeDtypeStruct((nb * bm * SLAB_ROWS, LANES), jnp.uint32),
        grid_spec=gs,
        compiler_params=_params(("arbitrary",)),
        name="ffn",
    )(blk_e, blk_row, blk_valid, first, slot.astype(jnp.int32), nxt.astype(jnp.int32), blk_rows,
      xs, w_gu, b_gu.reshape(n_e, 1, f2), w_d, b_d.reshape(n_e, 1, d))


def _combine_kernel(y0_ref, y1_ref, y2_ref, y3_ref, prow_ref, x1_ref, gf_ref, nw_ref, *rest, tm):
    o_ref = rest[-1]
    p = prow_ref[...]
    pieces = [_load_slabs(y_ref, tm) for y_ref in (y0_ref, y1_ref, y2_ref, y3_ref)]
    for s in range(len(pieces[0])):
        moe = None
        for k in range(TOP_K):
            piece = pieces[k][s] * p[:, k:k + 1]
            moe = piece if moe is None else moe + piece
        sl = slice(s * LANES, (s + 1) * LANES)
        o_ref[:, sl] = x1_ref[:, sl] + gf_ref[0][:, sl] * moe
    xo = o_ref[...]
    o_ref[...] = xo * lax.rsqrt(jnp.mean(xo * xo, axis=-1, keepdims=True) + EPS) * nw_ref[...]


def _combine_call(ytok, prow, x1, mod3, norm_final, seq, tm, row0, t_total, out_prev):
    t, d = x1.shape
    nt = t // tm
    tiles_per_seq = seq // tm
    off = row0 // tm
    kern = functools.partial(_combine_kernel, tm=tm)
    yspec = lambda k: pl.BlockSpec((tm * SLAB_ROWS, LANES), lambda i: (k * nt + i, 0))
    in_specs = [yspec(0), yspec(1), yspec(2), yspec(3),
                pl.BlockSpec((tm, LANES), lambda i: (i, 0)),
                pl.BlockSpec((tm, d), lambda i: (i, 0)),
                pl.BlockSpec((1, 1, d), lambda i: (((i + off) // tiles_per_seq) * N_MOD + 5, 0, 0)),
                pl.BlockSpec((1, d), lambda i: (0, 0))]
    args = [ytok, ytok, ytok, ytok, prow, x1, mod3, norm_final]
    aliases = {}
    if out_prev is not None:
        in_specs.append(pl.BlockSpec(memory_space=pl.ANY))
        aliases = {len(args): 0}
        args.append(out_prev)
    return pl.pallas_call(
        kern,
        out_shape=jax.ShapeDtypeStruct((t_total, d), F32),
        grid=(nt,),
        in_specs=in_specs,
        out_specs=pl.BlockSpec((tm, d), lambda i: (i + off, 0)),
        input_output_aliases=aliases,
        compiler_params=_params(("arbitrary",)),
        name="combine",
    )(*args)


def _plan(seq):
    def fit(pref):
        tm = min(pref, seq)
        assert seq % tm == 0
        return tm
    return dict(tm_in=fit(1024), tm_moe=fit(512), ret_chunk=fit(RET_CHUNK), ssm_chunk=fit(SSM_CHUNK))


def _layer(x2, mod3, bsz, seq, norm_mix, norm_ffn, w_in, conv_w, conv_b, dt_bias, a_log, d_skip, ssm_norm,
           w_ret_out, w_ssm_out, w_out, w_router, b_router, w_gate_up, b_gate_up, w_down, b_down,
           norm_final):
    t, d = x2.shape
    plan = _plan(seq)
    qk_w = RET_HEADS * RET_QK_DIM
    v_w = RET_HEADS * RET_V_DIM
    d_inner = w_ssm_out.shape[0]
    conv_dim = conv_w.shape[1]
    n_heads = d_inner // SSM_HEAD_DIM
    dt_off = 2 * qk_w + 2 * v_w + d_inner + conv_dim

    w_all = w_in.astype(BF16)
    w_gate = w_in[:, dt_off + n_heads:].astype(BF16)
    w_dt = jnp.pad(w_in[:, dt_off:dt_off + n_heads], ((0, 0), (0, LANES - n_heads)))
    w_dt_hi = w_dt.astype(BF16)
    w_dt = jnp.concatenate([w_dt_hi, (w_dt - w_dt_hi.astype(F32)).astype(BF16)], axis=1)
    half = RET_QK_DIM // 2
    inv_freq = ROPE_BASE ** (-jnp.arange(half, dtype=F32) / half)
    ang = jnp.arange(seq, dtype=F32)[:, None] * inv_freq[None, :]
    cos, sin = jnp.cos(ang), jnp.sin(ang)

    w_ret_b, w_ssm_b, w_out_b = w_ret_out.astype(BF16), w_ssm_out.astype(BF16), w_out.astype(BF16)
    w_r_hi = w_router.T.astype(BF16)
    w_router_t = jnp.concatenate([w_r_hi, (w_router.T - w_r_hi.astype(F32)).astype(BF16)], axis=0)
    bm = FFN_BLOCK
    slab = (SLAB_ROWS, LANES)

    def mixer(row0, tg, bg):
        proj, dt_raw = _inproj_call(x2, row0, tg, norm_mix.reshape(1, d), mod3, cos, sin, w_all, w_gate, w_dt,
                                    conv_w, conv_b, 2 * qk_w + 2 * v_w + d_inner, seq, plan["tm_in"], 2 * qk_w)
        x1, h2, idx, rank, prow, cnt = _mixers_call(
            proj, dt_raw, x2, row0, mod3, w_ret_b, dt_bias, a_log, d_skip, ssm_norm, w_ssm_b,
            norm_ffn.reshape(1, d), w_out_b, w_router_t, b_router, bg, seq, plan["ret_chunk"], plan["ssm_chunk"])
        counts = cnt[:, 0]
        padded = ((counts + bm - 1) // bm) * bm
        pad_end = jnp.cumsum(padded)
        start_pad = pad_end - padded
        n_blocks = -(-(tg * TOP_K) // bm) + N_EXPERTS
        e_ids = jnp.arange(N_EXPERTS, dtype=jnp.int32)[:, None, None, None]
        dest = rank + jnp.sum(jnp.where(idx[None] == e_ids, start_pad[:, None, None, None], 0), axis=0)
        dest = dest[:, :TOP_K, :].transpose(1, 0, 2).reshape(TOP_K, tg).astype(jnp.int32)
        n_real = pad_end[-1] // bm
        blk_valid = (jnp.arange(n_blocks) < n_real).astype(jnp.int32)
        blk_row = jnp.minimum(jnp.arange(n_blocks), n_real - 1).astype(jnp.int32)
        blk_e = jnp.minimum(jnp.sum(pad_end[None, :] <= (blk_row * bm)[:, None], axis=1),
                            N_EXPERTS - 1).astype(jnp.int32)
        sel_e = blk_e[:, None] == jnp.arange(N_EXPERTS, dtype=jnp.int32)[None, :]
        row_end = jnp.sum(jnp.where(sel_e, (start_pad + counts)[None, :], 0), axis=1)
        blk_rows = jnp.clip(row_end - blk_row * bm, 0, bm).astype(jnp.int32)
        xs = _sc_scatter_rows(h2.reshape((tg,) + slab), dest, n_blocks * bm)
        return dict(x1=x1, prow=prow, dest=dest, blocks=(blk_e, blk_row, blk_valid, blk_rows), xs=xs,
                    row0=row0)

    def experts(m):
        n_rows = m["xs"].shape[0]
        ys = _ffn_call(*m["blocks"], m["xs"].reshape(n_rows * SLAB_ROWS, LANES),
                       w_gate_up, b_gate_up, w_down, b_down, bm)
        return _sc_gather_rows(ys.reshape((n_rows,) + slab), m["dest"].reshape(-1))

    n_groups = N_GROUPS if bsz % N_GROUPS == 0 else 1
    bg = bsz // n_groups
    tg = bg * seq
    groups = [mixer(g * tg, tg, bg) for g in range(n_groups)]
    ytoks = [experts(m) for m in groups]
    out = None
    for y, m in zip(ytoks, groups):
        out = _combine_call(y.reshape(TOP_K * tg * SLAB_ROWS, LANES), m["prow"], m["x1"], mod3,
                            norm_final.reshape(1, d), seq, plan["tm_moe"], m["row0"], t, out)
    return out


def kernel(x, c, w_ada, b_ada, norm_mix, norm_ffn, w_in, conv_w, conv_b, dt_bias, a_log, d_skip, ssm_norm,
           w_ret_out, w_ssm_out, w_out, w_router, b_router, w_gate_up, b_gate_up, w_down, b_down, norm_final):
    bsz, seq, d = x.shape
    depth = w_ada.shape[0]
    assert depth == 1, "the final norm is fused into the single layer's last kernel"
    x2 = x.reshape(bsz * seq, d)
    l = 0
    mod = _mod_call(c, w_ada[l], b_ada[l])
    mod3 = mod.reshape(bsz * N_MOD, 1, d)
    out = _layer(x2, mod3, bsz, seq, norm_mix[l], norm_ffn[l], w_in[l], conv_w[l], conv_b[l], dt_bias[l],
                 a_log[l], d_skip[l], ssm_norm[l], w_ret_out[l], w_ssm_out[l], w_out[l], w_router[l],
                 b_router[l], w_gate_up[l], b_gate_up[l], w_down[l], b_down[l], norm_final)
    return out.reshape(bsz, seq, d)
```

```python
import functools
import math

import numpy as np
import jax
import jax.numpy as jnp
from jax import lax
from jax.experimental import pallas as pl
from jax.experimental.pallas import tpu as pltpu
from jax.experimental.pallas import tpu_sc as plsc

F32 = jnp.float32
BF16 = jnp.bfloat16
HIGHEST = lax.Precision.HIGHEST

EPS = 1e-6
N_MOD = 6
RET_HEADS = 4
RET_QK_DIM = 256
RET_V_DIM = 512
ROPE_BASE = 10000.0
SSM_HEAD_DIM = 64
SSM_GROUPS = 8
SSM_STATE = 128
SSM_CONV = 4
N_EXPERTS = 32
TOP_K = 4
SWIGLU_LIMIT = 7.0
SWIGLU_ALPHA = 1.702

LANES = 128
SUBLANES = 8
VMEM_LIMIT = 56 * 1024 * 1024

RET_CHUNK = 256
SSM_CHUNK = 128
FFN_BLOCK = 512
SC_CORES = 2
SC_SUBCORES = 16
SC_GROUP = 64
N_GROUPS = 2


def _params(sem, vmem=VMEM_LIMIT):
    return pltpu.CompilerParams(dimension_semantics=sem, vmem_limit_bytes=vmem)


def _nt_dot(a, b, **kw):
    return lax.dot_general(a, b, (((1,), (1,)), ((), ())), preferred_element_type=F32, **kw)


def _tn_dot(a, b, **kw):
    return lax.dot_general(a, b, (((0,), (0,)), ((), ())), preferred_element_type=F32, **kw)


def _silu(v):
    return v * jax.nn.sigmoid(v)


SLAB_ROWS = 4
HIGH_HALF = 0xFFFF0000
LOG2_E = math.log2(math.e)


def _store_slabs(ref, vals, n):
    for s in range(SLAB_ROWS):
        lo = vals[:, s * LANES:(s + 1) * LANES].astype(BF16).astype(F32)
        hi = vals[:, (s + SLAB_ROWS) * LANES:(s + SLAB_ROWS + 1) * LANES].astype(BF16).astype(F32)
        word = (pltpu.bitcast(lo, jnp.uint32) >> 16) | (pltpu.bitcast(hi, jnp.uint32) & jnp.uint32(HIGH_HALF))
        ref[pl.ds(s, n, stride=SLAB_ROWS), :] = word


def _load_slabs(ref, n, base=0, keep=None):
    lo, hi = [], []
    for s in range(SLAB_ROWS):
        word = ref[pl.ds(base + s, n, stride=SLAB_ROWS), :]
        if keep is not None:
            word = jnp.where(keep, word, jnp.zeros_like(word))
        lo.append(pltpu.bitcast(word << 16, F32))
        hi.append(pltpu.bitcast(word & jnp.uint32(HIGH_HALF), F32))
    return lo + hi


def _mod_kernel(c_ref, w_ref, b_ref, o_ref):
    cond = _silu(c_ref[...])
    o_ref[...] = jnp.dot(cond, w_ref[...], preferred_element_type=F32, precision=HIGHEST) + b_ref[...]


def _mod_call(c, w_ada, b_ada):
    bsz, d = c.shape
    n = w_ada.shape[1]
    return pl.pallas_call(
        _mod_kernel,
        out_shape=jax.ShapeDtypeStruct((bsz, n), F32),
        grid=(n // d,),
        in_specs=[pl.BlockSpec((bsz, d), lambda j: (0, 0)),
                  pl.BlockSpec((d, d), lambda j: (0, j)),
                  pl.BlockSpec((1, d), lambda j: (0, j))],
        out_specs=pl.BlockSpec((bsz, d), lambda j: (0, j)),
        compiler_params=_params(("arbitrary",)),
        name="mod",
    )(c, w_ada, b_ada.reshape(1, n))


def _inproj_kernel(x_ref, nw_ref, sc_ref, sh_ref, cos_ref, sin_ref, w_ref, wg_ref, wdt_ref, cw_ref, cb_ref,
                   o_ref, dt_ref, h_s, work, carry, *, conv_j0, conv_nj, silu_j, sigm_j, tiles_per_seq,
                   tm, tn, sub):
    i = pl.program_id(0)
    j = pl.program_id(1)
    n_dt = dt_ref.shape[1]
    rows = min(tm, 256)

    @pl.when(j == 0)
    def _():
        xf = x_ref[...]
        ms = jnp.mean(xf * xf, axis=-1, keepdims=True)
        y = xf * lax.rsqrt(ms + EPS) * nw_ref[...]
        hm = y * (1.0 + sc_ref[0]) + sh_ref[0]
        hb = hm.astype(BF16)
        h_s[...] = hb
        h_lo = (hm - hb.astype(F32)).astype(BF16)
        d_hi = jnp.dot(hb, wdt_ref[...], preferred_element_type=F32)
        d_lo = jnp.dot(h_lo, wdt_ref[:, :n_dt], preferred_element_type=F32)
        dt_ref[...] = d_hi[:, :n_dt] + d_hi[:, n_dt:] + d_lo
        half = RET_QK_DIM // 2
        for p in range(tn // sub):
            for r in range(tm // rows):
                rs = slice(r * rows, (r + 1) * rows)
                acc = jnp.dot(h_s[rs, :], w_ref[:, p * sub:(p + 1) * sub], preferred_element_type=F32)
                cos = cos_ref[rs, :]
                sin = sin_ref[rs, :]
                for cc in range(sub // RET_QK_DIM):
                    c = p * (sub // RET_QK_DIM) + cc
                    a = acc[:, cc * RET_QK_DIM: cc * RET_QK_DIM + half]
                    b = acc[:, cc * RET_QK_DIM + half: (cc + 1) * RET_QK_DIM]
                    scale = 1.0 if c < RET_HEADS else RET_QK_DIM ** -0.5
                    o_ref[rs, c * RET_QK_DIM: c * RET_QK_DIM + half] = ((a * cos - b * sin) * scale).astype(BF16)
                    o_ref[rs, c * RET_QK_DIM + half: (c + 1) * RET_QK_DIM] = (
                        (a * sin + b * cos) * scale).astype(BF16)

    is_conv = (j >= conv_j0) & (j < conv_j0 + conv_nj)

    @pl.when(is_conv)
    def _():
        cj = j - conv_j0
        pad = SUBLANES

        @pl.when(i % tiles_per_seq == 0)
        def _():
            carry[cj] = jnp.zeros(carry.shape[1:], F32)

        for p in range(tn // sub):
            for r in range(tm // rows):
                r0 = r * rows
                acc = jnp.dot(h_s[r0:r0 + rows, :], w_ref[:, p * sub:(p + 1) * sub], preferred_element_type=F32)
                for cc in range(sub // LANES):
                    c = p * (sub // LANES) + cc
                    cols = slice(c * LANES, (c + 1) * LANES)
                    if r == 0:
                        work[c, 0:pad, :] = carry[cj, c]
                    lo = pad + r0
                    work[c, lo:lo + rows, :] = acc[:, cc * LANES:(cc + 1) * LANES]
                    conv = cb_ref[:, cols] + cw_ref[SSM_CONV - 1:SSM_CONV, cols] * work[c, lo:lo + rows, :]
                    for k in range(SSM_CONV - 1):
                        shift = SSM_CONV - 1 - k
                        conv = conv + cw_ref[k:k + 1, cols] * work[c, lo - shift:lo - shift + rows, :]
                    if r0 + rows == tm:
                        carry[cj, c] = work[c, tm:tm + pad, :]
                    o_ref[r0:r0 + rows, cols] = _silu(conv).astype(BF16)

    def plain(act, weights):
        for p in range(tn // sub):
            for r in range(tm // rows):
                acc = jnp.dot(h_s[r * rows:(r + 1) * rows, :], weights[:, p * sub:(p + 1) * sub],
                              preferred_element_type=F32)
                o_ref[r * rows:(r + 1) * rows, p * sub:(p + 1) * sub] = act(acc).astype(BF16)

    is_silu = (j >= silu_j[0]) & (j < silu_j[1])
    is_sigm = (j >= sigm_j[0]) & (j < sigm_j[1])
    pl.when(is_silu)(lambda: plain(_silu, w_ref))
    pl.when(is_sigm)(lambda: plain(jax.nn.sigmoid, wg_ref))
    pl.when((j != 0) & jnp.logical_not(is_conv | is_silu | is_sigm))(lambda: plain(lambda v: v, w_ref))


def _inproj_call(x2, row0, t, norm_w, mod3, cos, sin, w_all, w_gate, w_dt, conv_w, conv_b, conv_off, seq, tm, tn):
    d = x2.shape[1]
    conv_dim = conv_w.shape[1]
    n_lead = conv_off + conv_dim
    n = n_lead + w_gate.shape[1]
    assert w_gate.shape[1] == tn and n_lead % tn == 0
    tiles_per_seq = seq // tm
    off = row0 // tm
    assert tn == 2 * RET_HEADS * RET_QK_DIM, "rotary epilogue expects q and k in the first column tile"
    assert conv_off % tn == 0 and conv_dim % tn == 0
    conv_j0, conv_nj = conv_off // tn, conv_dim // tn
    sub = 512
    g_off = 2 * RET_HEADS * RET_QK_DIM + RET_HEADS * RET_V_DIM
    assert g_off % tn == 0 and (conv_off - g_off) % tn == 0 and (n - conv_off - conv_dim) % tn == 0
    silu_j = (g_off // tn, conv_off // tn)
    sigm_j = ((conv_off + conv_dim) // tn, n // tn)
    kern = functools.partial(_inproj_kernel, conv_j0=conv_j0, conv_nj=conv_nj, silu_j=silu_j, sigm_j=sigm_j,
                             tiles_per_seq=tiles_per_seq, tm=tm, tn=tn, sub=sub)
    conv_idx = lambda i, j: (0, jnp.clip(j - conv_j0, 0, conv_nj - 1))
    return pl.pallas_call(
        kern,
        out_shape=(jax.ShapeDtypeStruct((t, n), BF16), jax.ShapeDtypeStruct((t, LANES), F32)),
        grid=(t // tm, n // tn),
        in_specs=[
            pl.BlockSpec((tm, d), lambda i, j: (i + off, 0)),
            pl.BlockSpec((1, d), lambda i, j: (0, 0)),
            pl.BlockSpec((1, 1, d), lambda i, j: (((i + off) // tiles_per_seq) * N_MOD + 1, 0, 0)),
            pl.BlockSpec((1, 1, d), lambda i, j: (((i + off) // tiles_per_seq) * N_MOD + 0, 0, 0)),
            pl.BlockSpec((tm, LANES), lambda i, j: (i % tiles_per_seq, 0)),
            pl.BlockSpec((tm, LANES), lambda i, j: (i % tiles_per_seq, 0)),
            pl.BlockSpec((d, tn), lambda i, j: (0, jnp.minimum(j, n_lead // tn - 1))),
            pl.BlockSpec((d, tn), lambda i, j: (0, 0)),
            pl.BlockSpec((d, 2 * LANES), lambda i, j: (0, 0)),
            pl.BlockSpec((SSM_CONV, tn), conv_idx),
            pl.BlockSpec((1, tn), conv_idx),
        ],
        out_specs=(pl.BlockSpec((tm, tn), lambda i, j: (i, j)),
                   pl.BlockSpec((tm, LANES), lambda i, j: (i, 0))),
        scratch_shapes=[pltpu.VMEM((tm, d), BF16),
                        pltpu.VMEM((tn // LANES, tm + SUBLANES, LANES), F32),
                        pltpu.VMEM((conv_nj, tn // LANES, SUBLANES, LANES), F32)],
        compiler_params=_params(("arbitrary", "arbitrary")),
        name="inproj",
    )(x2, norm_w, mod3, mod3, cos, sin, w_all, w_gate, w_dt, conv_w, conv_b.reshape(1, conv_dim))


def _retention_body(q_ref, k_ref, v_ref, g_ref, din_ref, dq_ref, dk_ref, w_ref, o_ref, state, decay_c):
    heads = range(RET_HEADS)
    q = [q_ref[:, h * RET_QK_DIM:(h + 1) * RET_QK_DIM] for h in heads]
    k = [k_ref[:, h * RET_QK_DIM:(h + 1) * RET_QK_DIM] for h in heads]
    v = [v_ref[:, h * RET_V_DIM:(h + 1) * RET_V_DIM] for h in heads]
    scores = [(_nt_dot(q[h], k[h]) * din_ref[h]).astype(BF16) for h in heads]
    st = [state[h] for h in heads]
    cross = [jnp.dot(q[h], st[h].astype(BF16), preferred_element_type=F32) * dq_ref[h] for h in heads]
    inner = [jnp.dot(scores[h], v[h], preferred_element_type=F32) for h in heads]
    for h in heads:
        kd = (k[h].astype(F32) * dk_ref[h]).astype(BF16)
        state[h] = st[h] * decay_c[h] + _tn_dot(kd, v[h])
    gated = []
    for h in heads:
        ret = inner[h] + cross[h]
        ret = ret * lax.rsqrt(jnp.mean(ret * ret, axis=-1, keepdims=True) + EPS)
        ret = ret * g_ref[:, h * RET_V_DIM:(h + 1) * RET_V_DIM].astype(F32)
        gated.append(ret.astype(BF16))
    o_ref[...] = jnp.dot(jnp.concatenate(gated, axis=-1), w_ref[...],
                         preferred_element_type=F32).astype(o_ref.dtype)


def _retention_tables(chunk):
    lg = np.log(1.0 - 2.0 ** (-5.0 - np.arange(RET_HEADS, dtype=np.float64)))
    idx = np.arange(chunk, dtype=np.float64)
    rel = idx[:, None] - idx[None, :]
    causal = rel >= 0
    din = np.where(causal[None], np.exp(np.where(causal, rel, 0.0)[None] * lg[:, None, None]), 0.0)
    dq = np.exp((idx + 1.0)[None, :, None] * lg[:, None, None])
    dk = np.exp((chunk - 1.0 - idx)[None, :, None] * lg[:, None, None])
    dc = tuple(float(v) for v in np.exp(chunk * lg))
    return (jnp.asarray(din, F32), jnp.asarray(dq, F32), jnp.asarray(dk, F32), dc)


def _ssd_decays(dt_ref, dtb_ref, alog_ref, tril_ref, exp_ref, n_sub, chunk):
    dt = jax.nn.softplus(dt_ref[...] + dtb_ref[...])
    adt = dt * (-LOG2_E * jnp.exp(alog_ref[...]))
    p1 = adt.astype(BF16)
    r1 = adt - p1.astype(F32)
    p2 = r1.astype(BF16)
    p3 = (r1 - p2.astype(F32)).astype(BF16)
    pieces = jnp.concatenate([p1, p2, p3], axis=-1)
    acs = []
    for s in range(n_sub):
        c3 = jnp.dot(tril_ref[...], pieces[s * chunk:(s + 1) * chunk, :], preferred_element_type=F32)
        acs.append(c3[:, :LANES] + c3[:, LANES:2 * LANES] + c3[:, 2 * LANES:])
    dt_x = jnp.dot(dt.astype(BF16), exp_ref[...], preferred_element_type=F32)
    return acs, dt_x


def _ssd_body(z_ref, xbc_ref, acs, dt_x, dsk_ref, nw_ref, yn_s, state, chunk, d_inner):
    heads_per_group = d_inner // SSM_HEAD_DIM // SSM_GROUPS
    gw = heads_per_group * SSM_HEAD_DIM
    assert SSM_HEAD_DIM * 2 == LANES and gw == 2 * LANES

    acs_t = acs.T
    li = lax.broadcasted_iota(jnp.int32, (chunk, chunk), 0)
    si = lax.broadcasted_iota(jnp.int32, (chunk, chunk), 1)
    causal = li >= si
    low_half = si < SSM_HEAD_DIM
    lane_g = lax.broadcasted_iota(jnp.int32, (chunk, gw), 1)

    b_off = d_inner
    c_off = d_inner + SSM_GROUPS * SSM_STATE
    for g in range(SSM_GROUPS):
        bm = xbc_ref[:, b_off + g * SSM_STATE: b_off + (g + 1) * SSM_STATE]
        cm = xbc_ref[:, c_off + g * SSM_STATE: c_off + (g + 1) * SSM_STATE]
        xs_g = xbc_ref[:, g * gw:(g + 1) * gw].astype(F32)
        xdt_g = xs_g * dt_x[:, g * gw:(g + 1) * gw]
        xdt_b = xdt_g.astype(BF16)
        cb = _nt_dot(cm, bm)
        cols, ms, xm = [], [], []
        for jh in range(heads_per_group):
            h = g * heads_per_group + jh
            col = jnp.broadcast_to(acs[:, h:h + 1], (chunk, chunk))
            seg = jnp.exp2(jnp.where(causal, col - acs_t[h:h + 1, :], -jnp.inf))
            cols.append(col)
            ms.append((cb * seg).astype(BF16))
            in_head = (lane_g >= jh * SSM_HEAD_DIM) & (lane_g < (jh + 1) * SSM_HEAD_DIM)
            xm.append(jnp.where(in_head, xdt_b, jnp.zeros_like(xdt_b)))
        y_diag = jnp.dot(jnp.concatenate(ms, axis=-1), jnp.concatenate(xm, axis=0),
                         preferred_element_type=F32)
        a_x = jnp.concatenate([jnp.where(low_half, cols[0], cols[1]),
                               jnp.where(low_half, cols[2], cols[3])], axis=-1)
        e_acs_x = jnp.exp2(a_x)
        a_last_x = a_x[chunk - 1:chunk, :]
        st = state[g]
        y_off = jnp.dot(cm, st.astype(BF16), preferred_element_type=F32) * e_acs_x
        xdec = (xdt_g * jnp.exp2(a_last_x - a_x)).astype(BF16)
        state[g] = st * e_acs_x[chunk - 1:chunk, :] + _tn_dot(bm, xdec)
        y = y_diag + y_off + dsk_ref[:, g * gw:(g + 1) * gw] * xs_g
        yz = y * z_ref[:, g * gw:(g + 1) * gw].astype(F32)
        yn = yz * lax.rsqrt(jnp.mean(yz * yz, axis=-1, keepdims=True) + EPS) * nw_ref[:, g * gw:(g + 1) * gw]
        yn_s[:, g * gw:(g + 1) * gw] = yn.astype(BF16)


def _mixers_kernel(q_ref, k_ref, v_ref, g_ref, din_ref, dq_ref, dk_ref, wret_ref,
                   z_ref, xbc_ref, dt_ref, dtb_ref, alog_ref, dsk_ref, nw_ref, tril_ref, exp_ref, wssm_ref,
                   ga_ref, gb_ref, x_ref, gm_ref, scf_ref, shf_ref, nf_ref, wo_ref, wr_ref, br_ref, tri_ref,
                   x1_ref, h2_ref, idx_ref, rank_ref, prow_ref, cnt_ref,
                   rstate, sstate, yn_s, ya_s, yb_s, cnt_s, *, decay_c, chunk, ssm_chunk, d_inner):
    b = pl.program_id(0)
    c = pl.program_id(1)

    @pl.when(c == 0)
    def _():
        rstate[...] = jnp.zeros_like(rstate)
        sstate[...] = jnp.zeros_like(sstate)

    @pl.when((b == 0) & (c == 0))
    def _():
        cnt_s[...] = jnp.zeros_like(cnt_s)

    n_sub = chunk // ssm_chunk
    acs, dt_x = _ssd_decays(dt_ref, dtb_ref, alog_ref, tril_ref, exp_ref, n_sub, ssm_chunk)
    _retention_body(q_ref, k_ref, v_ref, g_ref, din_ref, dq_ref, dk_ref, wret_ref, ya_s, rstate, decay_c)
    for sub in range(n_sub):
        rows = pl.ds(sub * ssm_chunk, ssm_chunk)
        _ssd_body(z_ref.at[rows, :], xbc_ref.at[rows, :], acs[sub],
                  dt_x[sub * ssm_chunk:(sub + 1) * ssm_chunk, :], dsk_ref, nw_ref, yn_s.at[rows, :],
                  sstate, ssm_chunk, d_inner)
    yb_s[...] = jnp.dot(yn_s[...], wssm_ref[...], preferred_element_type=F32)
    _merge_body(ya_s, yb_s, ga_ref, gb_ref, x_ref, gm_ref, scf_ref, shf_ref, nf_ref, wo_ref, wr_ref, br_ref,
                tri_ref, x1_ref, h2_ref, idx_ref, rank_ref, prow_ref, cnt_ref, cnt_s, chunk)


def _mixers_call(proj, dt_raw, x2, row0, mod3, w_ret, dt_bias, a_log, d_skip, ssm_norm, w_ssm,
                 norm_ffn, w_out, w_router_t, b_router, bsz, seq, chunk, ssm_chunk):
    t = proj.shape[0]
    d_inner, d = w_ssm.shape
    conv_dim = d_inner + 2 * SSM_GROUPS * SSM_STATE
    n_heads = d_inner // SSM_HEAD_DIM
    nc = seq // chunk
    gw = d_inner // SSM_GROUPS
    qk_w = RET_HEADS * RET_QK_DIM
    v_w = RET_HEADS * RET_V_DIM
    assert chunk % ssm_chunk == 0
    assert ssm_chunk == LANES, "the per-head decay tiles are built lane-for-lane against the chunk"
    din, dq, dk, dc = _retention_tables(chunk)
    pad_h = lambda v: jnp.pad(v.astype(F32), (0, LANES - n_heads)).reshape(1, LANES)
    tril = jnp.asarray(np.tril(np.ones((ssm_chunk, ssm_chunk), np.float32)), BF16)
    expand = np.zeros((LANES, d_inner), np.float32)
    for h in range(n_heads):
        expand[h, h * SSM_HEAD_DIM:(h + 1) * SSM_HEAD_DIM] = 1.0
    expand = jnp.asarray(expand, BF16)
    kern = functools.partial(_mixers_kernel, decay_c=dc, chunk=chunk, ssm_chunk=ssm_chunk, d_inner=d_inner)
    row = lambda b, c: b * nc + c
    z_blk = (2 * qk_w + 2 * v_w) // d_inner
    xbc_blk = (2 * qk_w + 2 * v_w + d_inner) // conv_dim
    full = lambda shape: pl.BlockSpec(shape, lambda b, c: (0,) * len(shape))
    nt = t // chunk
    ga_blk = proj.shape[1] // d - 2
    seq0 = row0 // seq
    modspec = lambda m: pl.BlockSpec((1, 1, d), lambda b, c: ((b + seq0) * N_MOD + m, 0, 0))
    tri = jnp.asarray(np.triu(np.ones((chunk, chunk), np.float32), 1), BF16)
    return pl.pallas_call(
        kern,
        out_shape=(jax.ShapeDtypeStruct((t, d), F32), jax.ShapeDtypeStruct((t * SLAB_ROWS, LANES), jnp.uint32),
                   jax.ShapeDtypeStruct((nt, SUBLANES, chunk), jnp.int32),
                   jax.ShapeDtypeStruct((nt, SUBLANES, chunk), jnp.int32),
                   jax.ShapeDtypeStruct((t, LANES), F32),
                   jax.ShapeDtypeStruct((N_EXPERTS, LANES), jnp.int32)),
        grid=(bsz, nc),
        in_specs=[
            pl.BlockSpec((chunk, qk_w), lambda b, c: (row(b, c), 0)),
            pl.BlockSpec((chunk, qk_w), lambda b, c: (row(b, c), 1)),
            pl.BlockSpec((chunk, v_w), lambda b, c: (row(b, c), 1)),
            pl.BlockSpec((chunk, v_w), lambda b, c: (row(b, c), 2)),
            full((RET_HEADS, chunk, chunk)), full((RET_HEADS, chunk, 1)), full((RET_HEADS, chunk, 1)),
            full((v_w, d)),
            pl.BlockSpec((chunk, d_inner), lambda b, c: (row(b, c), z_blk)),
            pl.BlockSpec((chunk, conv_dim), lambda b, c: (row(b, c), xbc_blk)),
            pl.BlockSpec((chunk, LANES), lambda b, c: (row(b, c), 0)),
            full((1, LANES)), full((1, LANES)),
            full((1, d_inner)), full((1, d_inner)), full((ssm_chunk, ssm_chunk)), full((LANES, d_inner)),
            full((d_inner, d)),
            pl.BlockSpec((chunk, d), lambda b, c: (row(b, c), ga_blk)),
            pl.BlockSpec((chunk, d), lambda b, c: (row(b, c), ga_blk + 1)),
            pl.BlockSpec((chunk, d), lambda b, c: (row0 // chunk + row(b, c), 0)),
            modspec(2), modspec(4), modspec(3),
            full((1, d)), full((d, d)), full((2 * N_EXPERTS, d)), full((N_EXPERTS, 1)), full((chunk, chunk)),
        ],
        out_specs=(pl.BlockSpec((chunk, d), lambda b, c: (row(b, c), 0)),
                   pl.BlockSpec((chunk * SLAB_ROWS, LANES), lambda b, c: (row(b, c), 0)),
                   pl.BlockSpec((1, SUBLANES, chunk), lambda b, c: (row(b, c), 0, 0)),
                   pl.BlockSpec((1, SUBLANES, chunk), lambda b, c: (row(b, c), 0, 0)),
                   pl.BlockSpec((chunk, LANES), lambda b, c: (row(b, c), 0)),
                   full((N_EXPERTS, LANES))),
        scratch_shapes=[pltpu.VMEM((RET_HEADS, RET_QK_DIM, RET_V_DIM), F32),
                        pltpu.VMEM((SSM_GROUPS, SSM_STATE, gw), F32),
                        pltpu.VMEM((chunk, d_inner), BF16),
                        pltpu.VMEM((chunk, d), F32), pltpu.VMEM((chunk, d), F32),
                        pltpu.VMEM((N_EXPERTS, LANES), F32)],
        compiler_params=_params(("arbitrary", "arbitrary")),
        name="mixers",
    )(proj, proj, proj, proj, din, dq, dk, w_ret,
      proj, proj, dt_raw, pad_h(dt_bias), pad_h(a_log),
      jnp.repeat(d_skip.astype(F32), SSM_HEAD_DIM).reshape(1, d_inner), ssm_norm.reshape(1, d_inner),
      tril, expand, w_ssm,
      proj, proj, x2, mod3, mod3, mod3, norm_ffn, w_out, w_router_t, b_router.reshape(N_EXPERTS, 1), tri)


def _merge_body(ya_ref, yb_ref, ga_ref, gb_ref, x_ref, gm_ref, scf_ref, shf_ref, nw_ref, wo_ref,
                wr_ref, br_ref, tri_ref,
                x1_ref, h2_ref, idx_ref, rank_ref, prow_ref, cnt_ref, cnt_s, tm):
    merged = (ga_ref[...].astype(F32) * ya_ref[...].astype(F32)
              + gb_ref[...].astype(F32) * yb_ref[...].astype(F32))
    mo = jnp.dot(merged.astype(BF16), wo_ref[...], preferred_element_type=F32)
    x1 = x_ref[...] + gm_ref[0] * mo
    x1_ref[...] = x1
    ms = jnp.mean(x1 * x1, axis=-1, keepdims=True)
    h2 = x1 * lax.rsqrt(ms + EPS) * nw_ref[...] * (1.0 + scf_ref[0]) + shf_ref[0]
    _store_slabs(h2_ref, h2, tm)

    h_hi = h2.astype(BF16)
    h_lo = (h2 - h_hi.astype(F32)).astype(BF16)
    lg2 = _nt_dot(wr_ref[...], h_hi)
    lg = lg2[:N_EXPERTS] + lg2[N_EXPERTS:] + _nt_dot(wr_ref[:N_EXPERTS, :], h_lo) + br_ref[...]
    sub = lax.broadcasted_iota(jnp.int32, lg.shape, 0)
    work = lg
    vals, idxs, sels = [], [], []
    for _ in range(TOP_K):
        m = jnp.max(work, axis=0, keepdims=True)
        ik = jnp.min(jnp.where(work == m, sub, N_EXPERTS), axis=0, keepdims=True)
        sel = sub == ik
        vals.append(m)
        idxs.append(ik)
        sels.append(sel)
        work = jnp.where(sel, -jnp.inf, work)
    exps = [jnp.exp(v - vals[0]) for v in vals]
    denom = exps[0]
    for e in exps[1:]:
        denom = denom + e
    probs = [e / denom for e in exps]

    base = cnt_s[:, 0:1]
    ranks = []
    for k in range(TOP_K):
        mk = jnp.where(sels[k], 1.0, 0.0)
        pre = jnp.dot(mk.astype(BF16), tri_ref[...], preferred_element_type=F32)
        ranks.append(jnp.sum(jnp.where(sels[k], pre + base, 0.0), axis=0, keepdims=True))
        base = base + jnp.sum(mk, axis=1, keepdims=True)
    cnt_s[...] = jnp.broadcast_to(base, cnt_s.shape)
    cnt_ref[...] = cnt_s[...].astype(jnp.int32)

    zi = jnp.zeros((SUBLANES - TOP_K, tm), jnp.int32)
    idx_ref[0] = jnp.concatenate(idxs + [zi], axis=0)
    rank_ref[0] = jnp.concatenate([r.astype(jnp.int32) for r in ranks] + [zi], axis=0)
    pt = jnp.concatenate(probs + [jnp.zeros((LANES - TOP_K, tm), F32)], axis=0)
    prow_ref[...] = pt.T


def _sc_mesh():
    return plsc.VectorSubcoreMesh(core_axis_name="c", subcore_axis_name="s")


def _sc_worker():
    return lax.axis_index("s") * SC_CORES + lax.axis_index("c")


def _sc_scatter_rows(rows, dest, n_out):
    t = rows.shape[0]
    n_k = dest.shape[0]
    g = SC_GROUP
    n_w = SC_CORES * SC_SUBCORES
    assert t % (n_w * g) == 0
    cpw = t // (n_w * g)
    dest_w = dest.reshape(n_k, n_w, cpw, g).transpose(1, 0, 2, 3)

    @functools.partial(
        pl.kernel, mesh=_sc_mesh(),
        out_type=jax.ShapeDtypeStruct((n_out,) + rows.shape[1:], rows.dtype),
        scratch_types=[pltpu.VMEM((n_k, cpw, g), jnp.int32),
                       pltpu.VMEM((g,) + rows.shape[1:], rows.dtype),
                       pltpu.SemaphoreType.DMA],
    )
    def scatter(rows_hbm, dest_hbm, out_hbm, idx_v, rows_v, sem):
        wid = _sc_worker()
        pltpu.sync_copy(dest_hbm.at[wid], idx_v)

        @pl.loop(0, cpw)
        def _(cc):
            r0 = pl.multiple_of((wid * cpw + cc) * g, g)
            pltpu.sync_copy(rows_hbm.at[pl.ds(r0, g)], rows_v)
            copies = [pltpu.async_copy(rows_v, out_hbm.at[idx_v.at[k, cc]], sem) for k in range(n_k)]
            for cp in copies:
                cp.wait()

    return scatter(rows, dest_w)


def _sc_gather_rows(table, idx):
    m = idx.shape[0]
    g = SC_GROUP
    n_w = SC_CORES * SC_SUBCORES
    assert m % (n_w * g) == 0
    per_w = m // n_w

    @functools.partial(
        pl.kernel, mesh=_sc_mesh(),
        out_type=jax.ShapeDtypeStruct((m,) + table.shape[1:], table.dtype),
        scratch_types=[pltpu.VMEM((per_w,), jnp.int32),
                       pltpu.VMEM((g,) + table.shape[1:], table.dtype),
                       pltpu.SemaphoreType.DMA],
    )
    def gather(table_hbm, idx_hbm, out_hbm, idx_v, rows_v, sem):
        base = _sc_worker() * per_w
        pltpu.sync_copy(idx_hbm.at[pl.ds(base, per_w)], idx_v)

        @pl.loop(0, per_w // g)
        def _(cc):
            off = pl.multiple_of(cc * g, g)
            pltpu.async_copy(table_hbm.at[idx_v.at[pl.ds(off, g)]], rows_v, sem).wait()
            pltpu.sync_copy(rows_v, out_hbm.at[pl.ds(base + off, g)])

    return gather(table, idx)


def _ffn_kernel(be_ref, br_ref, bv_ref, first_ref, slot_ref, next_ref, rows_ref,
                x_ref, wgu_hbm, bgu_ref, wd_hbm, bd_ref, o_ref, wgu_f, wd_f, wgu_s, wd_s, sem, *, bm, d_ff):
    i = pl.program_id(0)

    def fetch(e, slot):
        return (pltpu.make_async_copy(wgu_hbm.at[e], wgu_f.at[slot], sem.at[0, slot]),
                pltpu.make_async_copy(wd_hbm.at[e], wd_f.at[slot], sem.at[1, slot]))

    @pl.when(i == 0)
    def _():
        for cp in fetch(be_ref[0], slot_ref[0]):
            cp.start()

    @pl.when(first_ref[i] == 1)
    def _():
        slot = slot_ref[i]
        for cp in fetch(be_ref[i], slot):
            cp.wait()
        wgu_s[...] = wgu_f[slot].astype(BF16)
        wd_s[...] = wd_f[slot].astype(BF16)

        @pl.when(next_ref[i] >= 0)
        def _():
            for cp in fetch(next_ref[i], 1 - slot):
                cp.start()

    def expert_mlp(keep):
        x = jnp.concatenate(_load_slabs(x_ref, bm, keep=keep), axis=-1).astype(BF16)
        gu = jnp.dot(x, wgu_s[...], preferred_element_type=F32) + bgu_ref[0]
        gate = jnp.minimum(gu[:, :d_ff], SWIGLU_LIMIT)
        up = jnp.clip(gu[:, d_ff:], -SWIGLU_LIMIT, SWIGLU_LIMIT)
        act = gate * jax.nn.sigmoid(SWIGLU_ALPHA * gate) * (up + 1.0)
        y = jnp.dot(act.astype(BF16), wd_s[...], preferred_element_type=F32) + bd_ref[0]
        _store_slabs(o_ref, y, bm)

    @pl.when((bv_ref[i] == 1) & (rows_ref[i] == bm))
    def _():
        expert_mlp(None)

    @pl.when((bv_ref[i] == 1) & (rows_ref[i] < bm))
    def _():
        expert_mlp(lax.broadcasted_iota(jnp.int32, (bm, LANES), 0) < rows_ref[i])

    @pl.when(bv_ref[i] == 0)
    def _():
        o_ref[...] = jnp.zeros_like(o_ref)


def _ffn_call(blk_e, blk_row, blk_valid, blk_rows, xs, w_gu, b_gu, w_d, b_d, bm):
    n_e, d, f2 = w_gu.shape
    d_ff = f2 // 2
    nb = blk_e.shape[0]
    first = jnp.concatenate([jnp.ones((1,), jnp.int32), (blk_e[1:] != blk_e[:-1]).astype(jnp.int32)])
    slot = (jnp.cumsum(first) - 1) % 2
    later = jnp.where(blk_e[None, :] > blk_e[:, None], blk_e[None, :], n_e)
    nxt = jnp.min(later, axis=1)
    nxt = jnp.where(nxt == n_e, -1, nxt)
    kern = functools.partial(_ffn_kernel, bm=bm, d_ff=d_ff)
    imap = lambda f: (lambda i, be, br, bv, fi, sl, nx, rw: f(i, be, br))
    gs = pltpu.PrefetchScalarGridSpec(
        num_scalar_prefetch=7,
        grid=(nb,),
        in_specs=[pl.BlockSpec((bm * SLAB_ROWS, LANES), imap(lambda i, be, br: (br[i], 0))),
                  pl.BlockSpec(memory_space=pl.ANY),
                  pl.BlockSpec((1, 1, f2), imap(lambda i, be, br: (be[i], 0, 0))),
                  pl.BlockSpec(memory_space=pl.ANY),
                  pl.BlockSpec((1, 1, d), imap(lambda i, be, br: (be[i], 0, 0)))],
        out_specs=pl.BlockSpec((bm * SLAB_ROWS, LANES), imap(lambda i, be, br: (i, 0))),
        scratch_shapes=[pltpu.VMEM((2, d, f2), F32), pltpu.VMEM((2, d_ff, d), F32),
                        pltpu.VMEM((d, f2), BF16), pltpu.VMEM((d_ff, d), BF16),
                        pltpu.SemaphoreType.DMA((2, 2))],
    )
    return pl.pallas_call(
        kern,
        out_shape=jax.ShapeDtypeStruct((nb * bm * SLAB_ROWS, LANES), jnp.uint32),
        grid_spec=gs,
        compiler_params=_params(("arbitrary",)),
        name="ffn",
    )(blk_e, blk_row, blk_valid, first, slot.astype(jnp.int32), nxt.astype(jnp.int32), blk_rows,
      xs, w_gu, b_gu.reshape(n_e, 1, f2), w_d, b_d.reshape(n_e, 1, d))


def _combine_kernel(y0_ref, y1_ref, y2_ref, y3_ref, prow_ref, x1_ref, gf_ref, nw_ref, *rest, tm):
    o_ref = rest[-1]
    p = prow_ref[...]
    pieces = [_load_slabs(y_ref, tm) for y_ref in (y0_ref, y1_ref, y2_ref, y3_ref)]
    for s in range(len(pieces[0])):
        moe = None
        for k in range(TOP_K):
            piece = pieces[k][s] * p[:, k:k + 1]
            moe = piece if moe is None else moe + piece
        sl = slice(s * LANES, (s + 1) * LANES)
        o_ref[:, sl] = x1_ref[:, sl] + gf_ref[0][:, sl] * moe
    xo = o_ref[...]
    o_ref[...] = xo * lax.rsqrt(jnp.mean(xo * xo, axis=-1, keepdims=True) + EPS) * nw_ref[...]


def _combine_call(ytok, prow, x1, mod3, norm_final, seq, tm, row0, t_total, out_prev):
    t, d = x1.shape
    nt = t // tm
    tiles_per_seq = seq // tm
    off = row0 // tm
    kern = functools.partial(_combine_kernel, tm=tm)
    yspec = lambda k: pl.BlockSpec((tm * SLAB_ROWS, LANES), lambda i: (k * nt + i, 0))
    in_specs = [yspec(0), yspec(1), yspec(2), yspec(3),
                pl.BlockSpec((tm, LANES), lambda i: (i, 0)),
                pl.BlockSpec((tm, d), lambda i: (i, 0)),
                pl.BlockSpec((1, 1, d), lambda i: (((i + off) // tiles_per_seq) * N_MOD + 5, 0, 0)),
                pl.BlockSpec((1, d), lambda i: (0, 0))]
    args = [ytok, ytok, ytok, ytok, prow, x1, mod3, norm_final]
    aliases = {}
    if out_prev is not None:
        in_specs.append(pl.BlockSpec(memory_space=pl.ANY))
        aliases = {len(args): 0}
        args.append(out_prev)
    return pl.pallas_call(
        kern,
        out_shape=jax.ShapeDtypeStruct((t_total, d), F32),
        grid=(nt,),
        in_specs=in_specs,
        out_specs=pl.BlockSpec((tm, d), lambda i: (i + off, 0)),
        input_output_aliases=aliases,
        compiler_params=_params(("arbitrary",)),
        name="combine",
    )(*args)


def _plan(seq):
    def fit(pref):
        tm = min(pref, seq)
        assert seq % tm == 0
        return tm
    return dict(tm_in=fit(1024), tm_moe=fit(1024), ret_chunk=fit(RET_CHUNK), ssm_chunk=fit(SSM_CHUNK))


def _layer(x2, mod3, bsz, seq, norm_mix, norm_ffn, w_in, conv_w, conv_b, dt_bias, a_log, d_skip, ssm_norm,
           w_ret_out, w_ssm_out, w_out, w_router, b_router, w_gate_up, b_gate_up, w_down, b_down,
           norm_final):
    t, d = x2.shape
    plan = _plan(seq)
    qk_w = RET_HEADS * RET_QK_DIM
    v_w = RET_HEADS * RET_V_DIM
    d_inner = w_ssm_out.shape[0]
    conv_dim = conv_w.shape[1]
    n_heads = d_inner // SSM_HEAD_DIM
    dt_off = 2 * qk_w + 2 * v_w + d_inner + conv_dim

    w_all = w_in.astype(BF16)
    w_gate = w_in[:, dt_off + n_heads:].astype(BF16)
    w_dt = jnp.pad(w_in[:, dt_off:dt_off + n_heads], ((0, 0), (0, LANES - n_heads)))
    w_dt_hi = w_dt.astype(BF16)
    w_dt = jnp.concatenate([w_dt_hi, (w_dt - w_dt_hi.astype(F32)).astype(BF16)], axis=1)
    half = RET_QK_DIM // 2
    inv_freq = ROPE_BASE ** (-jnp.arange(half, dtype=F32) / half)
    ang = jnp.arange(seq, dtype=F32)[:, None] * inv_freq[None, :]
    cos, sin = jnp.cos(ang), jnp.sin(ang)

    w_ret_b, w_ssm_b, w_out_b = w_ret_out.astype(BF16), w_ssm_out.astype(BF16), w_out.astype(BF16)
    w_r_hi = w_router.T.astype(BF16)
    w_router_t = jnp.concatenate([w_r_hi, (w_router.T - w_r_hi.astype(F32)).astype(BF16)], axis=0)
    bm = FFN_BLOCK
    slab = (SLAB_ROWS, LANES)

    def mixer(row0, tg, bg):
        proj, dt_raw = _inproj_call(x2, row0, tg, norm_mix.reshape(1, d), mod3, cos, sin, w_all, w_gate, w_dt,
                                    conv_w, conv_b, 2 * qk_w + 2 * v_w + d_inner, seq, plan["tm_in"], 2 * qk_w)
        x1, h2, idx, rank, prow, cnt = _mixers_call(
            proj, dt_raw, x2, row0, mod3, w_ret_b, dt_bias, a_log, d_skip, ssm_norm, w_ssm_b,
            norm_ffn.reshape(1, d), w_out_b, w_router_t, b_router, bg, seq, plan["ret_chunk"], plan["ssm_chunk"])
        counts = cnt[:, 0]
        padded = ((counts + bm - 1) // bm) * bm
        pad_end = jnp.cumsum(padded)
        start_pad = pad_end - padded
        n_blocks = -(-(tg * TOP_K) // bm) + N_EXPERTS
        e_ids = jnp.arange(N_EXPERTS, dtype=jnp.int32)[:, None, None, None]
        dest = rank + jnp.sum(jnp.where(idx[None] == e_ids, start_pad[:, None, None, None], 0), axis=0)
        dest = dest[:, :TOP_K, :].transpose(1, 0, 2).reshape(TOP_K, tg).astype(jnp.int32)
        n_real = pad_end[-1] // bm
        blk_valid = (jnp.arange(n_blocks) < n_real).astype(jnp.int32)
        blk_row = jnp.minimum(jnp.arange(n_blocks), n_real - 1).astype(jnp.int32)
        blk_e = jnp.minimum(jnp.sum(pad_end[None, :] <= (blk_row * bm)[:, None], axis=1),
                            N_EXPERTS - 1).astype(jnp.int32)
        sel_e = blk_e[:, None] == jnp.arange(N_EXPERTS, dtype=jnp.int32)[None, :]
        row_end = jnp.sum(jnp.where(sel_e, (start_pad + counts)[None, :], 0), axis=1)
        blk_rows = jnp.clip(row_end - blk_row * bm, 0, bm).astype(jnp.int32)
        xs = _sc_scatter_rows(h2.reshape((tg,) + slab), dest, n_blocks * bm)
        return dict(x1=x1, prow=prow, dest=dest, blocks=(blk_e, blk_row, blk_valid, blk_rows), xs=xs,
                    row0=row0)

    def experts(m):
        n_rows = m["xs"].shape[0]
        ys = _ffn_call(*m["blocks"], m["xs"].reshape(n_rows * SLAB_ROWS, LANES),
                       w_gate_up, b_gate_up, w_down, b_down, bm)
        return _sc_gather_rows(ys.reshape((n_rows,) + slab), m["dest"].reshape(-1))

    n_groups = N_GROUPS if bsz % N_GROUPS == 0 else 1
    bg = bsz // n_groups
    tg = bg * seq
    groups = [mixer(g * tg, tg, bg) for g in range(n_groups)]
    ytoks = [experts(m) for m in groups]
    out = None
    for y, m in zip(ytoks, groups):
        out = _combine_call(y.reshape(TOP_K * tg * SLAB_ROWS, LANES), m["prow"], m["x1"], mod3,
                            norm_final.reshape(1, d), seq, plan["tm_moe"], m["row0"], t, out)
    return out


def kernel(x, c, w_ada, b_ada, norm_mix, norm_ffn, w_in, conv_w, conv_b, dt_bias, a_log, d_skip, ssm_norm,
           w_ret_out, w_ssm_out, w_out, w_router, b_router, w_gate_up, b_gate_up, w_down, b_down, norm_final):
    bsz, seq, d = x.shape
    depth = w_ada.shape[0]
    assert depth == 1, "the final norm is fused into the single layer's last kernel"
    x2 = x.reshape(bsz * seq, d)
    l = 0
    mod = _mod_call(c, w_ada[l], b_ada[l])
    mod3 = mod.reshape(bsz * N_MOD, 1, d)
    out = _layer(x2, mod3, bsz, seq, norm_mix[l], norm_ffn[l], w_in[l], conv_w[l], conv_b[l], dt_bias[l],
                 a_log[l], d_skip[l], ssm_norm[l], w_ret_out[l], w_ssm_out[l], w_out[l], w_router[l],
                 b_router[l], w_gate_up[l], b_gate_up[l], w_down[l], b_down[l], norm_final)
    return out.reshape(bsz, seq, d)
```

```python
import functools
import math

import numpy as np
import jax
import jax.numpy as jnp
from jax import lax
from jax.experimental import pallas as pl
from jax.experimental.pallas import tpu as pltpu
from jax.experimental.pallas import tpu_sc as plsc

F32 = jnp.float32
BF16 = jnp.bfloat16
HIGHEST = lax.Precision.HIGHEST

EPS = 1e-6
N_MOD = 6
RET_HEADS = 4
RET_QK_DIM = 256
RET_V_DIM = 512
ROPE_BASE = 10000.0
SSM_HEAD_DIM = 64
SSM_GROUPS = 8
SSM_STATE = 128
SSM_CONV = 4
N_EXPERTS = 32
TOP_K = 4
SWIGLU_LIMIT = 7.0
SWIGLU_ALPHA = 1.702

LANES = 128
SUBLANES = 8
VMEM_LIMIT = 56 * 1024 * 1024

RET_CHUNK = 256
SSM_CHUNK = 128
FFN_BLOCK = 512
SC_CORES = 2
SC_SUBCORES = 16
SC_GROUP = 64
N_GROUPS = 2


def _params(sem, vmem=VMEM_LIMIT):
    return pltpu.CompilerParams(dimension_semantics=sem, vmem_limit_bytes=vmem)


def _nt_dot(a, b, **kw):
    return lax.dot_general(a, b, (((1,), (1,)), ((), ())), preferred_element_type=F32, **kw)


def _tn_dot(a, b, **kw):
    return lax.dot_general(a, b, (((0,), (0,)), ((), ())), preferred_element_type=F32, **kw)


def _silu(v):
    return v * jax.nn.sigmoid(v)


SLAB_ROWS = 4
HIGH_HALF = 0xFFFF0000
LOG2_E = math.log2(math.e)


def _store_slabs(ref, vals, n):
    for s in range(SLAB_ROWS):
        lo = vals[:, s * LANES:(s + 1) * LANES].astype(BF16).astype(F32)
        hi = vals[:, (s + SLAB_ROWS) * LANES:(s + SLAB_ROWS + 1) * LANES].astype(BF16).astype(F32)
        word = (pltpu.bitcast(lo, jnp.uint32) >> 16) | (pltpu.bitcast(hi, jnp.uint32) & jnp.uint32(HIGH_HALF))
        ref[pl.ds(s, n, stride=SLAB_ROWS), :] = word


def _load_slabs(ref, n, base=0, keep=None):
    lo, hi = [], []
    for s in range(SLAB_ROWS):
        word = ref[pl.ds(base + s, n, stride=SLAB_ROWS), :]
        if keep is not None:
            word = jnp.where(keep, word, jnp.zeros_like(word))
        lo.append(pltpu.bitcast(word << 16, F32))
        hi.append(pltpu.bitcast(word & jnp.uint32(HIGH_HALF), F32))
    return lo + hi


def _mod_kernel(c_ref, w_ref, b_ref, o_ref):
    cond = _silu(c_ref[...])
    o_ref[...] = jnp.dot(cond, w_ref[...], preferred_element_type=F32, precision=HIGHEST) + b_ref[...]


def _mod_call(c, w_ada, b_ada):
    bsz, d = c.shape
    n = w_ada.shape[1]
    return pl.pallas_call(
        _mod_kernel,
        out_shape=jax.ShapeDtypeStruct((bsz, n), F32),
        grid=(n // d,),
        in_specs=[pl.BlockSpec((bsz, d), lambda j: (0, 0)),
                  pl.BlockSpec((d, d), lambda j: (0, j)),
                  pl.BlockSpec((1, d), lambda j: (0, j))],
        out_specs=pl.BlockSpec((bsz, d), lambda j: (0, j)),
        compiler_params=_params(("arbitrary",)),
        name="mod",
    )(c, w_ada, b_ada.reshape(1, n))


def _inproj_kernel(x_ref, nw_ref, sc_ref, sh_ref, cos_ref, sin_ref, w_ref, wg_ref, wdt_ref, cw_ref, cb_ref,
                   o_ref, dt_ref, h_s, work, carry, *, conv_j0, conv_nj, silu_j, sigm_j, tiles_per_seq,
                   tm, tn, sub):
    i = pl.program_id(0)
    j = pl.program_id(1)
    n_dt = dt_ref.shape[1]
    rows = min(tm, 256)

    @pl.when(j == 0)
    def _():
        xf = x_ref[...]
        ms = jnp.mean(xf * xf, axis=-1, keepdims=True)
        y = xf * lax.rsqrt(ms + EPS) * nw_ref[...]
        hm = y * (1.0 + sc_ref[0]) + sh_ref[0]
        hb = hm.astype(BF16)
        h_s[...] = hb
        h_lo = (hm - hb.astype(F32)).astype(BF16)
        d_hi = jnp.dot(hb, wdt_ref[...], preferred_element_type=F32)
        d_lo = jnp.dot(h_lo, wdt_ref[:, :n_dt], preferred_element_type=F32)
        dt_ref[...] = d_hi[:, :n_dt] + d_hi[:, n_dt:] + d_lo
        half = RET_QK_DIM // 2
        for p in range(tn // sub):
            for r in range(tm // rows):
                rs = slice(r * rows, (r + 1) * rows)
                acc = jnp.dot(h_s[rs, :], w_ref[:, p * sub:(p + 1) * sub], preferred_element_type=F32)
                cos = cos_ref[rs, :]
                sin = sin_ref[rs, :]
                for cc in range(sub // RET_QK_DIM):
                    c = p * (sub // RET_QK_DIM) + cc
                    a = acc[:, cc * RET_QK_DIM: cc * RET_QK_DIM + half]
                    b = acc[:, cc * RET_QK_DIM + half: (cc + 1) * RET_QK_DIM]
                    scale = 1.0 if c < RET_HEADS else RET_QK_DIM ** -0.5
                    o_ref[rs, c * RET_QK_DIM: c * RET_QK_DIM + half] = ((a * cos - b * sin) * scale).astype(BF16)
                    o_ref[rs, c * RET_QK_DIM + half: (c + 1) * RET_QK_DIM] = (
                        (a * sin + b * cos) * scale).astype(BF16)

    is_conv = (j >= conv_j0) & (j < conv_j0 + conv_nj)

    @pl.when(is_conv)
    def _():
        cj = j - conv_j0
        pad = SUBLANES

        @pl.when(i % tiles_per_seq == 0)
        def _():
            carry[cj] = jnp.zeros(carry.shape[1:], F32)

        for p in range(tn // sub):
            for r in range(tm // rows):
                r0 = r * rows
                acc = jnp.dot(h_s[r0:r0 + rows, :], w_ref[:, p * sub:(p + 1) * sub], preferred_element_type=F32)
                for cc in range(sub // LANES):
                    c = p * (sub // LANES) + cc
                    cols = slice(c * LANES, (c + 1) * LANES)
                    if r == 0:
                        work[c, 0:pad, :] = carry[cj, c]
                    lo = pad + r0
                    work[c, lo:lo + rows, :] = acc[:, cc * LANES:(cc + 1) * LANES]
                    conv = cb_ref[:, cols] + cw_ref[SSM_CONV - 1:SSM_CONV, cols] * work[c, lo:lo + rows, :]
                    for k in range(SSM_CONV - 1):
                        shift = SSM_CONV - 1 - k
                        conv = conv + cw_ref[k:k + 1, cols] * work[c, lo - shift:lo - shift + rows, :]
                    if r0 + rows == tm:
                        carry[cj, c] = work[c, tm:tm + pad, :]
                    o_ref[r0:r0 + rows, cols] = _silu(conv).astype(BF16)

    def plain(act, weights):
        for p in range(tn // sub):
            for r in range(tm // rows):
                acc = jnp.dot(h_s[r * rows:(r + 1) * rows, :], weights[:, p * sub:(p + 1) * sub],
                              preferred_element_type=F32)
                o_ref[r * rows:(r + 1) * rows, p * sub:(p + 1) * sub] = act(acc).astype(BF16)

    is_silu = (j >= silu_j[0]) & (j < silu_j[1])
    is_sigm = (j >= sigm_j[0]) & (j < sigm_j[1])
    pl.when(is_silu)(lambda: plain(_silu, w_ref))
    pl.when(is_sigm)(lambda: plain(jax.nn.sigmoid, wg_ref))
    pl.when((j != 0) & jnp.logical_not(is_conv | is_silu | is_sigm))(lambda: plain(lambda v: v, w_ref))


def _inproj_call(x2, row0, t, norm_w, mod3, cos, sin, w_all, w_gate, w_dt, conv_w, conv_b, conv_off, seq, tm, tn):
    d = x2.shape[1]
    conv_dim = conv_w.shape[1]
    n_lead = conv_off + conv_dim
    n = n_lead + w_gate.shape[1]
    assert w_gate.shape[1] == tn and n_lead % tn == 0
    tiles_per_seq = seq // tm
    off = row0 // tm
    assert tn == 2 * RET_HEADS * RET_QK_DIM, "rotary epilogue expects q and k in the first column tile"
    assert conv_off % tn == 0 and conv_dim % tn == 0
    conv_j0, conv_nj = conv_off // tn, conv_dim // tn
    sub = 512
    g_off = 2 * RET_HEADS * RET_QK_DIM + RET_HEADS * RET_V_DIM
    assert g_off % tn == 0 and (conv_off - g_off) % tn == 0 and (n - conv_off - conv_dim) % tn == 0
    silu_j = (g_off // tn, conv_off // tn)
    sigm_j = ((conv_off + conv_dim) // tn, n // tn)
    kern = functools.partial(_inproj_kernel, conv_j0=conv_j0, conv_nj=conv_nj, silu_j=silu_j, sigm_j=sigm_j,
                             tiles_per_seq=tiles_per_seq, tm=tm, tn=tn, sub=sub)
    conv_idx = lambda i, j: (0, jnp.clip(j - conv_j0, 0, conv_nj - 1))
    return pl.pallas_call(
        kern,
        out_shape=(jax.ShapeDtypeStruct((t, n), BF16), jax.ShapeDtypeStruct((t, LANES), F32)),
        grid=(t // tm, n // tn),
        in_specs=[
            pl.BlockSpec((tm, d), lambda i, j: (i + off, 0)),
            pl.BlockSpec((1, d), lambda i, j: (0, 0)),
            pl.BlockSpec((1, 1, d), lambda i, j: (((i + off) // tiles_per_seq) * N_MOD + 1, 0, 0)),
            pl.BlockSpec((1, 1, d), lambda i, j: (((i + off) // tiles_per_seq) * N_MOD + 0, 0, 0)),
            pl.BlockSpec((tm, LANES), lambda i, j: (i % tiles_per_seq, 0)),
            pl.BlockSpec((tm, LANES), lambda i, j: (i % tiles_per_seq, 0)),
            pl.BlockSpec((d, tn), lambda i, j: (0, jnp.minimum(j, n_lead // tn - 1))),
            pl.BlockSpec((d, tn), lambda i, j: (0, 0)),
            pl.BlockSpec((d, 2 * LANES), lambda i, j: (0, 0)),
            pl.BlockSpec((SSM_CONV, tn), conv_idx),
            pl.BlockSpec((1, tn), conv_idx),
        ],
        out_specs=(pl.BlockSpec((tm, tn), lambda i, j: (i, j)),
                   pl.BlockSpec((tm, LANES), lambda i, j: (i, 0))),
        scratch_shapes=[pltpu.VMEM((tm, d), BF16),
                        pltpu.VMEM((tn // LANES, tm + SUBLANES, LANES), F32),
                        pltpu.VMEM((conv_nj, tn // LANES, SUBLANES, LANES), F32)],
        compiler_params=_params(("arbitrary", "arbitrary")),
        name="inproj",
    )(x2, norm_w, mod3, mod3, cos, sin, w_all, w_gate, w_dt, conv_w, conv_b.reshape(1, conv_dim))


def _retention_body(q_ref, k_ref, v_ref, g_ref, din_ref, dq_ref, dk_ref, w_ref, o_ref, state, decay_c):
    heads = range(RET_HEADS)
    q = [q_ref[:, h * RET_QK_DIM:(h + 1) * RET_QK_DIM] for h in heads]
    k = [k_ref[:, h * RET_QK_DIM:(h + 1) * RET_QK_DIM] for h in heads]
    v = [v_ref[:, h * RET_V_DIM:(h + 1) * RET_V_DIM] for h in heads]
    gated = []
    for h in heads:
        scores = (_nt_dot(q[h], k[h]) * din_ref[h]).astype(BF16)
        st = state[h]
        cross = jnp.dot(q[h], st.astype(BF16), preferred_element_type=F32) * dq_ref[h]
        inner = jnp.dot(scores, v[h], preferred_element_type=F32)
        kd = (k[h].astype(F32) * dk_ref[h]).astype(BF16)
        state[h] = st * decay_c[h] + _tn_dot(kd, v[h])
        ret = inner + cross
        ret = ret * lax.rsqrt(jnp.mean(ret * ret, axis=-1, keepdims=True) + EPS)
        ret = ret * g_ref[:, h * RET_V_DIM:(h + 1) * RET_V_DIM].astype(F32)
        gated.append(ret.astype(BF16))
    o_ref[...] = jnp.dot(jnp.concatenate(gated, axis=-1), w_ref[...],
                         preferred_element_type=F32).astype(o_ref.dtype)


def _retention_tables(chunk):
    lg = np.log(1.0 - 2.0 ** (-5.0 - np.arange(RET_HEADS, dtype=np.float64)))
    idx = np.arange(chunk, dtype=np.float64)
    rel = idx[:, None] - idx[None, :]
    causal = rel >= 0
    din = np.where(causal[None], np.exp(np.where(causal, rel, 0.0)[None] * lg[:, None, None]), 0.0)
    dq = np.exp((idx + 1.0)[None, :, None] * lg[:, None, None])
    dk = np.exp((chunk - 1.0 - idx)[None, :, None] * lg[:, None, None])
    dc = tuple(float(v) for v in np.exp(chunk * lg))
    return (jnp.asarray(din, F32), jnp.asarray(dq, F32), jnp.asarray(dk, F32), dc)


def _ssd_decays(dt_ref, dtb_ref, alog_ref, tril_ref, exp_ref, n_sub, chunk):
    dt = jax.nn.softplus(dt_ref[...] + dtb_ref[...])
    adt = dt * (-LOG2_E * jnp.exp(alog_ref[...]))
    p1 = adt.astype(BF16)
    r1 = adt - p1.astype(F32)
    p2 = r1.astype(BF16)
    p3 = (r1 - p2.astype(F32)).astype(BF16)
    pieces = jnp.concatenate([p1, p2, p3], axis=-1)
    acs = []
    for s in range(n_sub):
        c3 = jnp.dot(tril_ref[...], pieces[s * chunk:(s + 1) * chunk, :], preferred_element_type=F32)
        acs.append(c3[:, :LANES] + c3[:, LANES:2 * LANES] + c3[:, 2 * LANES:])
    dt_x = jnp.dot(dt.astype(BF16), exp_ref[...], preferred_element_type=F32)
    return acs, dt_x


def _ssd_body(z_ref, xbc_ref, acs, dt_x, dsk_ref, nw_ref, yn_s, state, chunk, d_inner):
    heads_per_group = d_inner // SSM_HEAD_DIM // SSM_GROUPS
    gw = heads_per_group * SSM_HEAD_DIM
    assert SSM_HEAD_DIM * 2 == LANES and gw == 2 * LANES

    acs_t = acs.T
    li = lax.broadcasted_iota(jnp.int32, (chunk, chunk), 0)
    si = lax.broadcasted_iota(jnp.int32, (chunk, chunk), 1)
    causal = li >= si
    low_half = si < SSM_HEAD_DIM
    lane_g = lax.broadcasted_iota(jnp.int32, (chunk, gw), 1)

    b_off = d_inner
    c_off = d_inner + SSM_GROUPS * SSM_STATE
    for g in range(SSM_GROUPS):
        bm = xbc_ref[:, b_off + g * SSM_STATE: b_off + (g + 1) * SSM_STATE]
        cm = xbc_ref[:, c_off + g * SSM_STATE: c_off + (g + 1) * SSM_STATE]
        xs_g = xbc_ref[:, g * gw:(g + 1) * gw].astype(F32)
        xdt_g = xs_g * dt_x[:, g * gw:(g + 1) * gw]
        xdt_b = xdt_g.astype(BF16)
        cb = _nt_dot(cm, bm)
        cols, ms, xm = [], [], []
        for jh in range(heads_per_group):
            h = g * heads_per_group + jh
            col = jnp.broadcast_to(acs[:, h:h + 1], (chunk, chunk))
            seg = jnp.exp2(jnp.where(causal, col - acs_t[h:h + 1, :], -jnp.inf))
            cols.append(col)
            ms.append((cb * seg).astype(BF16))
            in_head = (lane_g >= jh * SSM_HEAD_DIM) & (lane_g < (jh + 1) * SSM_HEAD_DIM)
            xm.append(jnp.where(in_head, xdt_b, jnp.zeros_like(xdt_b)))
        y_diag = jnp.dot(jnp.concatenate(ms, axis=-1), jnp.concatenate(xm, axis=0),
                         preferred_element_type=F32)
        a_x = jnp.concatenate([jnp.where(low_half, cols[0], cols[1]),
                               jnp.where(low_half, cols[2], cols[3])], axis=-1)
        e_acs_x = jnp.exp2(a_x)
        a_last_x = a_x[chunk - 1:chunk, :]
        st = state[g]
        y_off = jnp.dot(cm, st.astype(BF16), preferred_element_type=F32) * e_acs_x
        xdec = (xdt_g * jnp.exp2(a_last_x - a_x)).astype(BF16)
        state[g] = st * e_acs_x[chunk - 1:chunk, :] + _tn_dot(bm, xdec)
        y = y_diag + y_off + dsk_ref[:, g * gw:(g + 1) * gw] * xs_g
        yz = y * z_ref[:, g * gw:(g + 1) * gw].astype(F32)
        yn = yz * lax.rsqrt(jnp.mean(yz * yz, axis=-1, keepdims=True) + EPS) * nw_ref[:, g * gw:(g + 1) * gw]
        yn_s[:, g * gw:(g + 1) * gw] = yn.astype(BF16)


def _mixers_kernel(q_ref, k_ref, v_ref, g_ref, din_ref, dq_ref, dk_ref, wret_ref,
                   z_ref, xbc_ref, dt_ref, dtb_ref, alog_ref, dsk_ref, nw_ref, tril_ref, exp_ref, wssm_ref,
                   ga_ref, gb_ref, x_ref, gm_ref, scf_ref, shf_ref, nf_ref, wo_ref, wr_ref, br_ref, tri_ref,
                   x1_ref, h2_ref, idx_ref, rank_ref, prow_ref, cnt_ref,
                   rstate, sstate, yn_s, ya_s, yb_s, cnt_s, *, decay_c, chunk, ssm_chunk, d_inner):
    b = pl.program_id(0)
    c = pl.program_id(1)

    @pl.when(c == 0)
    def _():
        rstate[...] = jnp.zeros_like(rstate)
        sstate[...] = jnp.zeros_like(sstate)

    @pl.when((b == 0) & (c == 0))
    def _():
        cnt_s[...] = jnp.zeros_like(cnt_s)

    n_sub = chunk // ssm_chunk
    acs, dt_x = _ssd_decays(dt_ref, dtb_ref, alog_ref, tril_ref, exp_ref, n_sub, ssm_chunk)
    _retention_body(q_ref, k_ref, v_ref, g_ref, din_ref, dq_ref, dk_ref, wret_ref, ya_s, rstate, decay_c)
    for sub in range(n_sub):
        rows = pl.ds(sub * ssm_chunk, ssm_chunk)
        _ssd_body(z_ref.at[rows, :], xbc_ref.at[rows, :], acs[sub],
                  dt_x[sub * ssm_chunk:(sub + 1) * ssm_chunk, :], dsk_ref, nw_ref, yn_s.at[rows, :],
                  sstate, ssm_chunk, d_inner)
    yb_s[...] = jnp.dot(yn_s[...], wssm_ref[...], preferred_element_type=F32)
    _merge_body(ya_s, yb_s, ga_ref, gb_ref, x_ref, gm_ref, scf_ref, shf_ref, nf_ref, wo_ref, wr_ref, br_ref,
                tri_ref, x1_ref, h2_ref, idx_ref, rank_ref, prow_ref, cnt_ref, cnt_s, chunk)


def _mixers_call(proj, dt_raw, x2, row0, mod3, w_ret, dt_bias, a_log, d_skip, ssm_norm, w_ssm,
                 norm_ffn, w_out, w_router_t, b_router, bsz, seq, chunk, ssm_chunk):
    t = proj.shape[0]
    d_inner, d = w_ssm.shape
    conv_dim = d_inner + 2 * SSM_GROUPS * SSM_STATE
    n_heads = d_inner // SSM_HEAD_DIM
    nc = seq // chunk
    gw = d_inner // SSM_GROUPS
    qk_w = RET_HEADS * RET_QK_DIM
    v_w = RET_HEADS * RET_V_DIM
    assert chunk % ssm_chunk == 0
    assert ssm_chunk == LANES, "the per-head decay tiles are built lane-for-lane against the chunk"
    din, dq, dk, dc = _retention_tables(chunk)
    pad_h = lambda v: jnp.pad(v.astype(F32), (0, LANES - n_heads)).reshape(1, LANES)
    tril = jnp.asarray(np.tril(np.ones((ssm_chunk, ssm_chunk), np.float32)), BF16)
    expand = np.zeros((LANES, d_inner), np.float32)
    for h in range(n_heads):
        expand[h, h * SSM_HEAD_DIM:(h + 1) * SSM_HEAD_DIM] = 1.0
    expand = jnp.asarray(expand, BF16)
    kern = functools.partial(_mixers_kernel, decay_c=dc, chunk=chunk, ssm_chunk=ssm_chunk, d_inner=d_inner)
    row = lambda b, c: b * nc + c
    z_blk = (2 * qk_w + 2 * v_w) // d_inner
    xbc_blk = (2 * qk_w + 2 * v_w + d_inner) // conv_dim
    full = lambda shape: pl.BlockSpec(shape, lambda b, c: (0,) * len(shape))
    nt = t // chunk
    ga_blk = proj.shape[1] // d - 2
    seq0 = row0 // seq
    modspec = lambda m: pl.BlockSpec((1, 1, d), lambda b, c: ((b + seq0) * N_MOD + m, 0, 0))
    tri = jnp.asarray(np.triu(np.ones((chunk, chunk), np.float32), 1), BF16)
    return pl.pallas_call(
        kern,
        out_shape=(jax.ShapeDtypeStruct((t, d), F32), jax.ShapeDtypeStruct((t * SLAB_ROWS, LANES), jnp.uint32),
                   jax.ShapeDtypeStruct((nt, SUBLANES, chunk), jnp.int32),
                   jax.ShapeDtypeStruct((nt, SUBLANES, chunk), jnp.int32),
                   jax.ShapeDtypeStruct((t, LANES), F32),
                   jax.ShapeDtypeStruct((N_EXPERTS, LANES), jnp.int32)),
        grid=(bsz, nc),
        in_specs=[
            pl.BlockSpec((chunk, qk_w), lambda b, c: (row(b, c), 0)),
            pl.BlockSpec((chunk, qk_w), lambda b, c: (row(b, c), 1)),
            pl.BlockSpec((chunk, v_w), lambda b, c: (row(b, c), 1)),
            pl.BlockSpec((chunk, v_w), lambda b, c: (row(b, c), 2)),
            full((RET_HEADS, chunk, chunk)), full((RET_HEADS, chunk, 1)), full((RET_HEADS, chunk, 1)),
            full((v_w, d)),
            pl.BlockSpec((chunk, d_inner), lambda b, c: (row(b, c), z_blk)),
            pl.BlockSpec((chunk, conv_dim), lambda b, c: (row(b, c), xbc_blk)),
            pl.BlockSpec((chunk, LANES), lambda b, c: (row(b, c), 0)),
            full((1, LANES)), full((1, LANES)),
            full((1, d_inner)), full((1, d_inner)), full((ssm_chunk, ssm_chunk)), full((LANES, d_inner)),
            full((d_inner, d)),
            pl.BlockSpec((chunk, d), lambda b, c: (row(b, c), ga_blk)),
            pl.BlockSpec((chunk, d), lambda b, c: (row(b, c), ga_blk + 1)),
            pl.BlockSpec((chunk, d), lambda b, c: (row0 // chunk + row(b, c), 0)),
            modspec(2), modspec(4), modspec(3),
            full((1, d)), full((d, d)), full((2 * N_EXPERTS, d)), full((N_EXPERTS, 1)), full((chunk, chunk)),
        ],
        out_specs=(pl.BlockSpec((chunk, d), lambda b, c: (row(b, c), 0)),
                   pl.BlockSpec((chunk * SLAB_ROWS, LANES), lambda b, c: (row(b, c), 0)),
                   pl.BlockSpec((1, SUBLANES, chunk), lambda b, c: (row(b, c), 0, 0)),
                   pl.BlockSpec((1, SUBLANES, chunk), lambda b, c: (row(b, c), 0, 0)),
                   pl.BlockSpec((chunk, LANES), lambda b, c: (row(b, c), 0)),
                   full((N_EXPERTS, LANES))),
        scratch_shapes=[pltpu.VMEM((RET_HEADS, RET_QK_DIM, RET_V_DIM), F32),
                        pltpu.VMEM((SSM_GROUPS, SSM_STATE, gw), F32),
                        pltpu.VMEM((chunk, d_inner), BF16),
                        pltpu.VMEM((chunk, d), F32), pltpu.VMEM((chunk, d), F32),
                        pltpu.VMEM((N_EXPERTS, LANES), F32)],
        compiler_params=_params(("arbitrary", "arbitrary")),
        name="mixers",
    )(proj, proj, proj, proj, din, dq, dk, w_ret,
      proj, proj, dt_raw, pad_h(dt_bias), pad_h(a_log),
      jnp.repeat(d_skip.astype(F32), SSM_HEAD_DIM).reshape(1, d_inner), ssm_norm.reshape(1, d_inner),
      tril, expand, w_ssm,
      proj, proj, x2, mod3, mod3, mod3, norm_ffn, w_out, w_router_t, b_router.reshape(N_EXPERTS, 1), tri)


def _merge_body(ya_ref, yb_ref, ga_ref, gb_ref, x_ref, gm_ref, scf_ref, shf_ref, nw_ref, wo_ref,
                wr_ref, br_ref, tri_ref,
                x1_ref, h2_ref, idx_ref, rank_ref, prow_ref, cnt_ref, cnt_s, tm):
    merged = (ga_ref[...].astype(F32) * ya_ref[...].astype(F32)
              + gb_ref[...].astype(F32) * yb_ref[...].astype(F32))
    mo = jnp.dot(merged.astype(BF16), wo_ref[...], preferred_element_type=F32)
    x1 = x_ref[...] + gm_ref[0] * mo
    x1_ref[...] = x1
    ms = jnp.mean(x1 * x1, axis=-1, keepdims=True)
    h2 = x1 * lax.rsqrt(ms + EPS) * nw_ref[...] * (1.0 + scf_ref[0]) + shf_ref[0]
    _store_slabs(h2_ref, h2, tm)

    h_hi = h2.astype(BF16)
    h_lo = (h2 - h_hi.astype(F32)).astype(BF16)
    lg2 = _nt_dot(wr_ref[...], h_hi)
    lg = lg2[:N_EXPERTS] + lg2[N_EXPERTS:] + _nt_dot(wr_ref[:N_EXPERTS, :], h_lo) + br_ref[...]
    sub = lax.broadcasted_iota(jnp.int32, lg.shape, 0)
    work = lg
    vals, idxs, sels = [], [], []
    for _ in range(TOP_K):
        m = jnp.max(work, axis=0, keepdims=True)
        ik = jnp.min(jnp.where(work == m, sub, N_EXPERTS), axis=0, keepdims=True)
        sel = sub == ik
        vals.append(m)
        idxs.append(ik)
        sels.append(sel)
        work = jnp.where(sel, -jnp.inf, work)
    exps = [jnp.exp(v - vals[0]) for v in vals]
    denom = exps[0]
    for e in exps[1:]:
        denom = denom + e
    probs = [e / denom for e in exps]

    base = cnt_s[:, 0:1]
    ranks = []
    for k in range(TOP_K):
        mk = jnp.where(sels[k], 1.0, 0.0)
        pre = jnp.dot(mk.astype(BF16), tri_ref[...], preferred_element_type=F32)
        ranks.append(jnp.sum(jnp.where(sels[k], pre + base, 0.0), axis=0, keepdims=True))
        base = base + jnp.sum(mk, axis=1, keepdims=True)
    cnt_s[...] = jnp.broadcast_to(base, cnt_s.shape)
    cnt_ref[...] = cnt_s[...].astype(jnp.int32)

    zi = jnp.zeros((SUBLANES - TOP_K, tm), jnp.int32)
    idx_ref[0] = jnp.concatenate(idxs + [zi], axis=0)
    rank_ref[0] = jnp.concatenate([r.astype(jnp.int32) for r in ranks] + [zi], axis=0)
    pt = jnp.concatenate(probs + [jnp.zeros((LANES - TOP_K, tm), F32)], axis=0)
    prow_ref[...] = pt.T


def _sc_mesh():
    return plsc.VectorSubcoreMesh(core_axis_name="c", subcore_axis_name="s")


def _sc_worker():
    return lax.axis_index("s") * SC_CORES + lax.axis_index("c")


def _sc_scatter_rows(rows, dest, n_out):
    t = rows.shape[0]
    n_k = dest.shape[0]
    g = SC_GROUP
    n_w = SC_CORES * SC_SUBCORES
    assert t % (n_w * g) == 0
    cpw = t // (n_w * g)
    dest_w = dest.reshape(n_k, n_w, cpw, g).transpose(1, 0, 2, 3)

    @functools.partial(
        pl.kernel, mesh=_sc_mesh(),
        out_type=jax.ShapeDtypeStruct((n_out,) + rows.shape[1:], rows.dtype),
        scratch_types=[pltpu.VMEM((n_k, cpw, g), jnp.int32),
                       pltpu.VMEM((g,) + rows.shape[1:], rows.dtype),
                       pltpu.SemaphoreType.DMA],
    )
    def scatter(rows_hbm, dest_hbm, out_hbm, idx_v, rows_v, sem):
        wid = _sc_worker()
        pltpu.sync_copy(dest_hbm.at[wid], idx_v)

        @pl.loop(0, cpw)
        def _(cc):
            r0 = pl.multiple_of((wid * cpw + cc) * g, g)
            pltpu.sync_copy(rows_hbm.at[pl.ds(r0, g)], rows_v)
            copies = [pltpu.async_copy(rows_v, out_hbm.at[idx_v.at[k, cc]], sem) for k in range(n_k)]
            for cp in copies:
                cp.wait()

    return scatter(rows, dest_w)


def _sc_gather_rows(table, idx):
    m = idx.shape[0]
    g = SC_GROUP
    n_w = SC_CORES * SC_SUBCORES
    assert m % (n_w * g) == 0
    per_w = m // n_w

    @functools.partial(
        pl.kernel, mesh=_sc_mesh(),
        out_type=jax.ShapeDtypeStruct((m,) + table.shape[1:], table.dtype),
        scratch_types=[pltpu.VMEM((per_w,), jnp.int32),
                       pltpu.VMEM((g,) + table.shape[1:], table.dtype),
                       pltpu.SemaphoreType.DMA],
    )
    def gather(table_hbm, idx_hbm, out_hbm, idx_v, rows_v, sem):
        base = _sc_worker() * per_w
        pltpu.sync_copy(idx_hbm.at[pl.ds(base, per_w)], idx_v)

        @pl.loop(0, per_w // g)
        def _(cc):
            off = pl.multiple_of(cc * g, g)
            pltpu.async_copy(table_hbm.at[idx_v.at[pl.ds(off, g)]], rows_v, sem).wait()
            pltpu.sync_copy(rows_v, out_hbm.at[pl.ds(base + off, g)])

    return gather(table, idx)


def _ffn_kernel(be_ref, br_ref, bv_ref, first_ref, slot_ref, next_ref, rows_ref,
                x_ref, wgu_hbm, bgu_ref, wd_hbm, bd_ref, o_ref, wgu_f, wd_f, wgu_s, wd_s, sem, *, bm, d_ff):
    i = pl.program_id(0)

    def fetch(e, slot):
        return (pltpu.make_async_copy(wgu_hbm.at[e], wgu_f.at[slot], sem.at[0, slot]),
                pltpu.make_async_copy(wd_hbm.at[e], wd_f.at[slot], sem.at[1, slot]))

    @pl.when(i == 0)
    def _():
        for cp in fetch(be_ref[0], slot_ref[0]):
            cp.start()

    @pl.when(first_ref[i] == 1)
    def _():
        slot = slot_ref[i]
        for cp in fetch(be_ref[i], slot):
            cp.wait()
        wgu_s[...] = wgu_f[slot].astype(BF16)
        wd_s[...] = wd_f[slot].astype(BF16)

        @pl.when(next_ref[i] >= 0)
        def _():
            for cp in fetch(next_ref[i], 1 - slot):
                cp.start()

    def expert_mlp(keep):
        x = jnp.concatenate(_load_slabs(x_ref, bm, keep=keep), axis=-1).astype(BF16)
        gu = jnp.dot(x, wgu_s[...], preferred_element_type=F32) + bgu_ref[0]
        gate = jnp.minimum(gu[:, :d_ff], SWIGLU_LIMIT)
        up = jnp.clip(gu[:, d_ff:], -SWIGLU_LIMIT, SWIGLU_LIMIT)
        act = gate * jax.nn.sigmoid(SWIGLU_ALPHA * gate) * (up + 1.0)
        y = jnp.dot(act.astype(BF16), wd_s[...], preferred_element_type=F32) + bd_ref[0]
        _store_slabs(o_ref, y, bm)

    @pl.when((bv_ref[i] == 1) & (rows_ref[i] == bm))
    def _():
        expert_mlp(None)

    @pl.when((bv_ref[i] == 1) & (rows_ref[i] < bm))
    def _():
        expert_mlp(lax.broadcasted_iota(jnp.int32, (bm, LANES), 0) < rows_ref[i])

    @pl.when(bv_ref[i] == 0)
    def _():
        o_ref[...] = jnp.zeros_like(o_ref)


def _ffn_call(blk_e, blk_row, blk_valid, blk_rows, xs, w_gu, b_gu, w_d, b_d, bm):
    n_e, d, f2 = w_gu.shape
    d_ff = f2 // 2
    nb = blk_e.shape[0]
    first = jnp.concatenate([jnp.ones((1,), jnp.int32), (blk_e[1:] != blk_e[:-1]).astype(jnp.int32)])
    slot = (jnp.cumsum(first) - 1) % 2
    later = jnp.where(blk_e[None, :] > blk_e[:, None], blk_e[None, :], n_e)
    nxt = jnp.min(later, axis=1)
    nxt = jnp.where(nxt == n_e, -1, nxt)
    kern = functools.partial(_ffn_kernel, bm=bm, d_ff=d_ff)
    imap = lambda f: (lambda i, be, br, bv, fi, sl, nx, rw: f(i, be, br))
    gs = pltpu.PrefetchScalarGridSpec(
        num_scalar_prefetch=7,
        grid=(nb,),
        in_specs=[pl.BlockSpec((bm * SLAB_ROWS, LANES), imap(lambda i, be, br: (br[i], 0))),
                  pl.BlockSpec(memory_space=pl.ANY),
                  pl.BlockSpec((1, 1, f2), imap(lambda i, be, br: (be[i], 0, 0))),
                  pl.BlockSpec(memory_space=pl.ANY),
                  pl.BlockSpec((1, 1, d), imap(lambda i, be, br: (be[i], 0, 0)))],
        out_specs=pl.BlockSpec((bm * SLAB_ROWS, LANES), imap(lambda i, be, br: (i, 0))),
        scratch_shapes=[pltpu.VMEM((2, d, f2), F32), pltpu.VMEM((2, d_ff, d), F32),
                        pltpu.VMEM((d, f2), BF16), pltpu.VMEM((d_ff, d), BF16),
                        pltpu.SemaphoreType.DMA((2, 2))],
    )
    return pl.pallas_call(
        kern,
        out_shape=jax.ShapeDtypeStruct((nb * bm * SLAB_ROWS, LANES), jnp.uint32),
        grid_spec=gs,
        compiler_params=_params(("arbitrary",)),
        name="ffn",
    )(blk_e, blk_row, blk_valid, first, slot.astype(jnp.int32), nxt.astype(jnp.int32), blk_rows,
      xs, w_gu, b_gu.reshape(n_e, 1, f2), w_d, b_d.reshape(n_e, 1, d))


def _combine_kernel(y0_ref, y1_ref, y2_ref, y3_ref, prow_ref, x1_ref, gf_ref, nw_ref, *rest, tm):
    o_ref = rest[-1]
    p = prow_ref[...]
    pieces = [_load_slabs(y_ref, tm) for y_ref in (y0_ref, y1_ref, y2_ref, y3_ref)]
    for s in range(len(pieces[0])):
        moe = None
        for k in range(TOP_K):
            piece = pieces[k][s] * p[:, k:k + 1]
            moe = piece if moe is None else moe + piece
        sl = slice(s * LANES, (s + 1) * LANES)
        o_ref[:, sl] = x1_ref[:, sl] + gf_ref[0][:, sl] * moe
    xo = o_ref[...]
    o_ref[...] = xo * lax.rsqrt(jnp.mean(xo * xo, axis=-1, keepdims=True) + EPS) * nw_ref[...]


def _combine_call(ytok, prow, x1, mod3, norm_final, seq, tm, row0, t_total, out_prev):
    t, d = x1.shape
    nt = t // tm
    tiles_per_seq = seq // tm
    off = row0 // tm
    kern = functools.partial(_combine_kernel, tm=tm)
    yspec = lambda k: pl.BlockSpec((tm * SLAB_ROWS, LANES), lambda i: (k * nt + i, 0))
    in_specs = [yspec(0), yspec(1), yspec(2), yspec(3),
                pl.BlockSpec((tm, LANES), lambda i: (i, 0)),
                pl.BlockSpec((tm, d), lambda i: (i, 0)),
                pl.BlockSpec((1, 1, d), lambda i: (((i + off) // tiles_per_seq) * N_MOD + 5, 0, 0)),
                pl.BlockSpec((1, d), lambda i: (0, 0))]
    args = [ytok, ytok, ytok, ytok, prow, x1, mod3, norm_final]
    aliases = {}
    if out_prev is not None:
        in_specs.append(pl.BlockSpec(memory_space=pl.ANY))
        aliases = {len(args): 0}
        args.append(out_prev)
    return pl.pallas_call(
        kern,
        out_shape=jax.ShapeDtypeStruct((t_total, d), F32),
        grid=(nt,),
        in_specs=in_specs,
        out_specs=pl.BlockSpec((tm, d), lambda i: (i + off, 0)),
        input_output_aliases=aliases,
        compiler_params=_params(("arbitrary",)),
        name="combine",
    )(*args)


def _plan(seq):
    def fit(pref):
        tm = min(pref, seq)
        assert seq % tm == 0
        return tm
    return dict(tm_in=fit(1024), tm_moe=fit(1024), ret_chunk=fit(RET_CHUNK), ssm_chunk=fit(SSM_CHUNK))


def _layer(x2, mod3, bsz, seq, norm_mix, norm_ffn, w_in, conv_w, conv_b, dt_bias, a_log, d_skip, ssm_norm,
           w_ret_out, w_ssm_out, w_out, w_router, b_router, w_gate_up, b_gate_up, w_down, b_down,
           norm_final):
    t, d = x2.shape
    plan = _plan(seq)
    qk_w = RET_HEADS * RET_QK_DIM
    v_w = RET_HEADS * RET_V_DIM
    d_inner = w_ssm_out.shape[0]
    conv_dim = conv_w.shape[1]
    n_heads = d_inner // SSM_HEAD_DIM
    dt_off = 2 * qk_w + 2 * v_w + d_inner + conv_dim

    w_all = w_in.astype(BF16)
    w_gate = w_in[:, dt_off + n_heads:].astype(BF16)
    w_dt = jnp.pad(w_in[:, dt_off:dt_off + n_heads], ((0, 0), (0, LANES - n_heads)))
    w_dt_hi = w_dt.astype(BF16)
    w_dt = jnp.concatenate([w_dt_hi, (w_dt - w_dt_hi.astype(F32)).astype(BF16)], axis=1)
    half = RET_QK_DIM // 2
    inv_freq = ROPE_BASE ** (-jnp.arange(half, dtype=F32) / half)
    ang = jnp.arange(seq, dtype=F32)[:, None] * inv_freq[None, :]
    cos, sin = jnp.cos(ang), jnp.sin(ang)

    w_ret_b, w_ssm_b, w_out_b = w_ret_out.astype(BF16), w_ssm_out.astype(BF16), w_out.astype(BF16)
    w_r_hi = w_router.T.astype(BF16)
    w_router_t = jnp.concatenate([w_r_hi, (w_router.T - w_r_hi.astype(F32)).astype(BF16)], axis=0)
    bm = FFN_BLOCK
    slab = (SLAB_ROWS, LANES)

    def mixer(row0, tg, bg):
        proj, dt_raw = _inproj_call(x2, row0, tg, norm_mix.reshape(1, d), mod3, cos, sin, w_all, w_gate, w_dt,
                                    conv_w, conv_b, 2 * qk_w + 2 * v_w + d_inner, seq, plan["tm_in"], 2 * qk_w)
        x1, h2, idx, rank, prow, cnt = _mixers_call(
            proj, dt_raw, x2, row0, mod3, w_ret_b, dt_bias, a_log, d_skip, ssm_norm, w_ssm_b,
            norm_ffn.reshape(1, d), w_out_b, w_router_t, b_router, bg, seq, plan["ret_chunk"], plan["ssm_chunk"])
        counts = cnt[:, 0]
        padded = ((counts + bm - 1) // bm) * bm
        pad_end = jnp.cumsum(padded)
        start_pad = pad_end - padded
        n_blocks = -(-(tg * TOP_K) // bm) + N_EXPERTS
        e_ids = jnp.arange(N_EXPERTS, dtype=jnp.int32)[:, None, None, None]
        dest = rank + jnp.sum(jnp.where(idx[None] == e_ids, start_pad[:, None, None, None], 0), axis=0)
        dest = dest[:, :TOP_K, :].transpose(1, 0, 2).reshape(TOP_K, tg).astype(jnp.int32)
        n_real = pad_end[-1] // bm
        blk_valid = (jnp.arange(n_blocks) < n_real).astype(jnp.int32)
        blk_row = jnp.minimum(jnp.arange(n_blocks), n_real - 1).astype(jnp.int32)
        blk_e = jnp.minimum(jnp.sum(pad_end[None, :] <= (blk_row * bm)[:, None], axis=1),
                            N_EXPERTS - 1).astype(jnp.int32)
        sel_e = blk_e[:, None] == jnp.arange(N_EXPERTS, dtype=jnp.int32)[None, :]
        row_end = jnp.sum(jnp.where(sel_e, (start_pad + counts)[None, :], 0), axis=1)
        blk_rows = jnp.clip(row_end - blk_row * bm, 0, bm).astype(jnp.int32)
        xs = _sc_scatter_rows(h2.reshape((tg,) + slab), dest, n_blocks * bm)
        return dict(x1=x1, prow=prow, dest=dest, blocks=(blk_e, blk_row, blk_valid, blk_rows), xs=xs,
                    row0=row0)

    def experts(m):
        n_rows = m["xs"].shape[0]
        ys = _ffn_call(*m["blocks"], m["xs"].reshape(n_rows * SLAB_ROWS, LANES),
                       w_gate_up, b_gate_up, w_down, b_down, bm)
        return _sc_gather_rows(ys.reshape((n_rows,) + slab), m["dest"].reshape(-1))

    n_groups = N_GROUPS if bsz % N_GROUPS == 0 else 1
    bg = bsz // n_groups
    tg = bg * seq
    groups = [mixer(g * tg, tg, bg) for g in range(n_groups)]
    ytoks = [experts(m) for m in groups]
    out = None
    for y, m in zip(ytoks, groups):
        out = _combine_call(y.reshape(TOP_K * tg * SLAB_ROWS, LANES), m["prow"], m["x1"], mod3,
                            norm_final.reshape(1, d), seq, plan["tm_moe"], m["row0"], t, out)
    return out


def kernel(x, c, w_ada, b_ada, norm_mix, norm_ffn, w_in, conv_w, conv_b, dt_bias, a_log, d_skip, ssm_norm,
           w_ret_out, w_ssm_out, w_out, w_router, b_router, w_gate_up, b_gate_up, w_down, b_down, norm_final):
    bsz, seq, d = x.shape
    depth = w_ada.shape[0]
    assert depth == 1, "the final norm is fused into the single layer's last kernel"
    x2 = x.reshape(bsz * seq, d)
    l = 0
    mod = _mod_call(c, w_ada[l], b_ada[l])
    mod3 = mod.reshape(bsz * N_MOD, 1, d)
    out = _layer(x2, mod3, bsz, seq, norm_mix[l], norm_ffn[l], w_in[l], conv_w[l], conv_b[l], dt_bias[l],
                 a_log[l], d_skip[l], ssm_norm[l], w_ret_out[l], w_ssm_out[l], w_out[l], w_router[l],
                 b_router[l], w_gate_up[l], b_gate_up[l], w_down[l], b_down[l], norm_final)
    return out.reshape(bsz, seq, d)
```

```python
import functools
import math

import numpy as np
import jax
import jax.numpy as jnp
from jax import lax
from jax.experimental import pallas as pl
from jax.experimental.pallas import tpu as pltpu
from jax.experimental.pallas import tpu_sc as plsc

F32 = jnp.float32
BF16 = jnp.bfloat16
HIGHEST = lax.Precision.HIGHEST

EPS = 1e-6
N_MOD = 6
RET_HEADS = 4
RET_QK_DIM = 256
RET_V_DIM = 512
ROPE_BASE = 10000.0
SSM_HEAD_DIM = 64
SSM_GROUPS = 8
SSM_STATE = 128
SSM_CONV = 4
N_EXPERTS = 32
TOP_K = 4
SWIGLU_LIMIT = 7.0
SWIGLU_ALPHA = 1.702

LANES = 128
SUBLANES = 8
VMEM_LIMIT = 56 * 1024 * 1024

RET_CHUNK = 256
SSM_CHUNK = 128
FFN_BLOCK = 512
FFN_TAIL = 128
SC_CORES = 2
SC_SUBCORES = 16
SC_GROUP = 64
N_GROUPS = 2


def _params(sem, vmem=VMEM_LIMIT):
    return pltpu.CompilerParams(dimension_semantics=sem, vmem_limit_bytes=vmem)


def _nt_dot(a, b, **kw):
    return lax.dot_general(a, b, (((1,), (1,)), ((), ())), preferred_element_type=F32, **kw)


def _tn_dot(a, b, **kw):
    return lax.dot_general(a, b, (((0,), (0,)), ((), ())), preferred_element_type=F32, **kw)


def _silu(v):
    return v * jax.nn.sigmoid(v)


SLAB_ROWS = 4
HIGH_HALF = 0xFFFF0000
LOG2_E = math.log2(math.e)


def _store_slabs(ref, vals, n):
    for s in range(SLAB_ROWS):
        lo = vals[:, s * LANES:(s + 1) * LANES].astype(BF16).astype(F32)
        hi = vals[:, (s + SLAB_ROWS) * LANES:(s + SLAB_ROWS + 1) * LANES].astype(BF16).astype(F32)
        word = (pltpu.bitcast(lo, jnp.uint32) >> 16) | (pltpu.bitcast(hi, jnp.uint32) & jnp.uint32(HIGH_HALF))
        ref[pl.ds(s, n, stride=SLAB_ROWS), :] = word


def _load_slabs(ref, n, base=0, keep=None):
    lo, hi = [], []
    for s in range(SLAB_ROWS):
        word = ref[pl.ds(base + s, n, stride=SLAB_ROWS), :]
        if keep is not None:
            word = jnp.where(keep, word, jnp.zeros_like(word))
        lo.append(pltpu.bitcast(word << 16, F32))
        hi.append(pltpu.bitcast(word & jnp.uint32(HIGH_HALF), F32))
    return lo + hi


def _mod_kernel(c_ref, w_ref, b_ref, o_ref):
    cond = _silu(c_ref[...])
    o_ref[...] = jnp.dot(cond, w_ref[...], preferred_element_type=F32, precision=HIGHEST) + b_ref[...]


def _mod_call(c, w_ada, b_ada):
    bsz, d = c.shape
    n = w_ada.shape[1]
    return pl.pallas_call(
        _mod_kernel,
        out_shape=jax.ShapeDtypeStruct((bsz, n), F32),
        grid=(n // d,),
        in_specs=[pl.BlockSpec((bsz, d), lambda j: (0, 0)),
                  pl.BlockSpec((d, d), lambda j: (0, j)),
                  pl.BlockSpec((1, d), lambda j: (0, j))],
        out_specs=pl.BlockSpec((bsz, d), lambda j: (0, j)),
        compiler_params=_params(("arbitrary",)),
        name="mod",
    )(c, w_ada, b_ada.reshape(1, n))


def _inproj_kernel(x_ref, nw_ref, sc_ref, sh_ref, cos_ref, sin_ref, w_ref, wg_ref, wdt_ref, cw_ref, cb_ref,
                   o_ref, dt_ref, h_s, work, carry, *, conv_j0, conv_nj, silu_j, sigm_j, tiles_per_seq,
                   tm, tn, sub):
    i = pl.program_id(0)
    j = pl.program_id(1)
    n_dt = dt_ref.shape[1]
    rows = min(tm, 256)

    @pl.when(j == 0)
    def _():
        xf = x_ref[...]
        ms = jnp.mean(xf * xf, axis=-1, keepdims=True)
        y = xf * lax.rsqrt(ms + EPS) * nw_ref[...]
        hm = y * (1.0 + sc_ref[0]) + sh_ref[0]
        hb = hm.astype(BF16)
        h_s[...] = hb
        h_lo = (hm - hb.astype(F32)).astype(BF16)
        d_hi = jnp.dot(hb, wdt_ref[...], preferred_element_type=F32)
        d_lo = jnp.dot(h_lo, wdt_ref[:, :n_dt], preferred_element_type=F32)
        dt_ref[...] = d_hi[:, :n_dt] + d_hi[:, n_dt:] + d_lo
        half = RET_QK_DIM // 2
        for p in range(tn // sub):
            for r in range(tm // rows):
                rs = slice(r * rows, (r + 1) * rows)
                acc = jnp.dot(h_s[rs, :], w_ref[:, p * sub:(p + 1) * sub], preferred_element_type=F32)
                cos = cos_ref[rs, :]
                sin = sin_ref[rs, :]
                for cc in range(sub // RET_QK_DIM):
                    c = p * (sub // RET_QK_DIM) + cc
                    a = acc[:, cc * RET_QK_DIM: cc * RET_QK_DIM + half]
                    b = acc[:, cc * RET_QK_DIM + half: (cc + 1) * RET_QK_DIM]
                    scale = 1.0 if c < RET_HEADS else RET_QK_DIM ** -0.5
                    o_ref[rs, c * RET_QK_DIM: c * RET_QK_DIM + half] = ((a * cos - b * sin) * scale).astype(BF16)
                    o_ref[rs, c * RET_QK_DIM + half: (c + 1) * RET_QK_DIM] = (
                        (a * sin + b * cos) * scale).astype(BF16)

    is_conv = (j >= conv_j0) & (j < conv_j0 + conv_nj)

    @pl.when(is_conv)
    def _():
        cj = j - conv_j0
        pad = SUBLANES

        @pl.when(i % tiles_per_seq == 0)
        def _():
            carry[cj] = jnp.zeros(carry.shape[1:], F32)

        for p in range(tn // sub):
            for r in range(tm // rows):
                r0 = r * rows
                acc = jnp.dot(h_s[r0:r0 + rows, :], w_ref[:, p * sub:(p + 1) * sub], preferred_element_type=F32)
                for cc in range(sub // LANES):
                    c = p * (sub // LANES) + cc
                    cols = slice(c * LANES, (c + 1) * LANES)
                    if r == 0:
                        work[c, 0:pad, :] = carry[cj, c]
                    lo = pad + r0
                    work[c, lo:lo + rows, :] = acc[:, cc * LANES:(cc + 1) * LANES]
                    conv = cb_ref[:, cols] + cw_ref[SSM_CONV - 1:SSM_CONV, cols] * work[c, lo:lo + rows, :]
                    for k in range(SSM_CONV - 1):
                        shift = SSM_CONV - 1 - k
                        conv = conv + cw_ref[k:k + 1, cols] * work[c, lo - shift:lo - shift + rows, :]
                    if r0 + rows == tm:
                        carry[cj, c] = work[c, tm:tm + pad, :]
                    o_ref[r0:r0 + rows, cols] = _silu(conv).astype(BF16)

    def plain(act, weights):
        for p in range(tn // sub):
            for r in range(tm // rows):
                acc = jnp.dot(h_s[r * rows:(r + 1) * rows, :], weights[:, p * sub:(p + 1) * sub],
                              preferred_element_type=F32)
                o_ref[r * rows:(r + 1) * rows, p * sub:(p + 1) * sub] = act(acc).astype(BF16)

    is_silu = (j >= silu_j[0]) & (j < silu_j[1])
    is_sigm = (j >= sigm_j[0]) & (j < sigm_j[1])
    pl.when(is_silu)(lambda: plain(_silu, w_ref))
    pl.when(is_sigm)(lambda: plain(jax.nn.sigmoid, wg_ref))
    pl.when((j != 0) & jnp.logical_not(is_conv | is_silu | is_sigm))(lambda: plain(lambda v: v, w_ref))


def _inproj_call(x2, row0, t, norm_w, mod3, cos, sin, w_all, w_gate, w_dt, conv_w, conv_b, conv_off, seq, tm, tn):
    d = x2.shape[1]
    conv_dim = conv_w.shape[1]
    n_lead = conv_off + conv_dim
    n = n_lead + w_gate.shape[1]
    assert w_gate.shape[1] == tn and n_lead % tn == 0
    tiles_per_seq = seq // tm
    off = row0 // tm
    assert tn == 2 * RET_HEADS * RET_QK_DIM, "rotary epilogue expects q and k in the first column tile"
    assert conv_off % tn == 0 and conv_dim % tn == 0
    conv_j0, conv_nj = conv_off // tn, conv_dim // tn
    sub = 512
    g_off = 2 * RET_HEADS * RET_QK_DIM + RET_HEADS * RET_V_DIM
    assert g_off % tn == 0 and (conv_off - g_off) % tn == 0 and (n - conv_off - conv_dim) % tn == 0
    silu_j = (g_off // tn, conv_off // tn)
    sigm_j = ((conv_off + conv_dim) // tn, n // tn)
    kern = functools.partial(_inproj_kernel, conv_j0=conv_j0, conv_nj=conv_nj, silu_j=silu_j, sigm_j=sigm_j,
                             tiles_per_seq=tiles_per_seq, tm=tm, tn=tn, sub=sub)
    conv_idx = lambda i, j: (0, jnp.clip(j - conv_j0, 0, conv_nj - 1))
    return pl.pallas_call(
        kern,
        out_shape=(jax.ShapeDtypeStruct((t, n), BF16), jax.ShapeDtypeStruct((t, LANES), F32)),
        grid=(t // tm, n // tn),
        in_specs=[
            pl.BlockSpec((tm, d), lambda i, j: (i + off, 0)),
            pl.BlockSpec((1, d), lambda i, j: (0, 0)),
            pl.BlockSpec((1, 1, d), lambda i, j: (((i + off) // tiles_per_seq) * N_MOD + 1, 0, 0)),
            pl.BlockSpec((1, 1, d), lambda i, j: (((i + off) // tiles_per_seq) * N_MOD + 0, 0, 0)),
            pl.BlockSpec((tm, LANES), lambda i, j: (i % tiles_per_seq, 0)),
            pl.BlockSpec((tm, LANES), lambda i, j: (i % tiles_per_seq, 0)),
            pl.BlockSpec((d, tn), lambda i, j: (0, jnp.minimum(j, n_lead // tn - 1))),
            pl.BlockSpec((d, tn), lambda i, j: (0, 0)),
            pl.BlockSpec((d, 2 * LANES), lambda i, j: (0, 0)),
            pl.BlockSpec((SSM_CONV, tn), conv_idx),
            pl.BlockSpec((1, tn), conv_idx),
        ],
        out_specs=(pl.BlockSpec((tm, tn), lambda i, j: (i, j)),
                   pl.BlockSpec((tm, LANES), lambda i, j: (i, 0))),
        scratch_shapes=[pltpu.VMEM((tm, d), BF16),
                        pltpu.VMEM((tn // LANES, tm + SUBLANES, LANES), F32),
                        pltpu.VMEM((conv_nj, tn // LANES, SUBLANES, LANES), F32)],
        compiler_params=_params(("arbitrary", "arbitrary")),
        name="inproj",
    )(x2, norm_w, mod3, mod3, cos, sin, w_all, w_gate, w_dt, conv_w, conv_b.reshape(1, conv_dim))


def _retention_body(q_ref, k_ref, v_ref, g_ref, din_ref, dq_ref, dk_ref, w_ref, o_ref, state, decay_c):
    heads = range(RET_HEADS)
    q = [q_ref[:, h * RET_QK_DIM:(h + 1) * RET_QK_DIM] for h in heads]
    k = [k_ref[:, h * RET_QK_DIM:(h + 1) * RET_QK_DIM] for h in heads]
    v = [v_ref[:, h * RET_V_DIM:(h + 1) * RET_V_DIM] for h in heads]
    gated = []
    for h in heads:
        scores = (_nt_dot(q[h], k[h]) * din_ref[h]).astype(BF16)
        st = state[h]
        cross = jnp.dot(q[h], st.astype(BF16), preferred_element_type=F32) * dq_ref[h]
        inner = jnp.dot(scores, v[h], preferred_element_type=F32)
        kd = (k[h].astype(F32) * dk_ref[h]).astype(BF16)
        state[h] = st * decay_c[h] + _tn_dot(kd, v[h])
        ret = inner + cross
        ret = ret * lax.rsqrt(jnp.mean(ret * ret, axis=-1, keepdims=True) + EPS)
        ret = ret * g_ref[:, h * RET_V_DIM:(h + 1) * RET_V_DIM].astype(F32)
        gated.append(ret.astype(BF16))
    o_ref[...] = jnp.dot(jnp.concatenate(gated, axis=-1), w_ref[...],
                         preferred_element_type=F32).astype(o_ref.dtype)


def _retention_tables(chunk):
    lg = np.log(1.0 - 2.0 ** (-5.0 - np.arange(RET_HEADS, dtype=np.float64)))
    idx = np.arange(chunk, dtype=np.float64)
    rel = idx[:, None] - idx[None, :]
    causal = rel >= 0
    din = np.where(causal[None], np.exp(np.where(causal, rel, 0.0)[None] * lg[:, None, None]), 0.0)
    dq = np.exp((idx + 1.0)[None, :, None] * lg[:, None, None])
    dk = np.exp((chunk - 1.0 - idx)[None, :, None] * lg[:, None, None])
    dc = tuple(float(v) for v in np.exp(chunk * lg))
    return (jnp.asarray(din, F32), jnp.asarray(dq, F32), jnp.asarray(dk, F32), dc)


def _ssd_decays(dt_ref, dtb_ref, alog_ref, tril_ref, exp_ref, n_sub, chunk):
    dt = jax.nn.softplus(dt_ref[...] + dtb_ref[...])
    adt = dt * (-LOG2_E * jnp.exp(alog_ref[...]))
    p1 = adt.astype(BF16)
    r1 = adt - p1.astype(F32)
    p2 = r1.astype(BF16)
    p3 = (r1 - p2.astype(F32)).astype(BF16)
    pieces = jnp.concatenate([p1, p2, p3], axis=-1)
    acs = []
    for s in range(n_sub):
        c3 = jnp.dot(tril_ref[...], pieces[s * chunk:(s + 1) * chunk, :], preferred_element_type=F32)
        acs.append(c3[:, :LANES] + c3[:, LANES:2 * LANES] + c3[:, 2 * LANES:])
    dt_x = jnp.dot(dt.astype(BF16), exp_ref[...], preferred_element_type=F32)
    return acs, dt_x


def _ssd_body(z_ref, xbc_ref, acs, dt_x, dsk_ref, nw_ref, yn_s, state, chunk, d_inner):
    heads_per_group = d_inner // SSM_HEAD_DIM // SSM_GROUPS
    gw = heads_per_group * SSM_HEAD_DIM
    assert SSM_HEAD_DIM * 2 == LANES and gw == 2 * LANES

    acs_t = acs.T
    li = lax.broadcasted_iota(jnp.int32, (chunk, chunk), 0)
    si = lax.broadcasted_iota(jnp.int32, (chunk, chunk), 1)
    causal = li >= si
    low_half = si < SSM_HEAD_DIM
    lane_g = lax.broadcasted_iota(jnp.int32, (chunk, gw), 1)

    b_off = d_inner
    c_off = d_inner + SSM_GROUPS * SSM_STATE
    for g in range(SSM_GROUPS):
        bm = xbc_ref[:, b_off + g * SSM_STATE: b_off + (g + 1) * SSM_STATE]
        cm = xbc_ref[:, c_off + g * SSM_STATE: c_off + (g + 1) * SSM_STATE]
        xs_g = xbc_ref[:, g * gw:(g + 1) * gw].astype(F32)
        xdt_g = xs_g * dt_x[:, g * gw:(g + 1) * gw]
        xdt_b = xdt_g.astype(BF16)
        cb = _nt_dot(cm, bm)
        cols, ms, xm = [], [], []
        for jh in range(heads_per_group):
            h = g * heads_per_group + jh
            col = jnp.broadcast_to(acs[:, h:h + 1], (chunk, chunk))
            seg = jnp.exp2(jnp.where(causal, col - acs_t[h:h + 1, :], -jnp.inf))
            cols.append(col)
            ms.append((cb * seg).astype(BF16))
            in_head = (lane_g >= jh * SSM_HEAD_DIM) & (lane_g < (jh + 1) * SSM_HEAD_DIM)
            xm.append(jnp.where(in_head, xdt_b, jnp.zeros_like(xdt_b)))
        y_diag = jnp.dot(jnp.concatenate(ms, axis=-1), jnp.concatenate(xm, axis=0),
                         preferred_element_type=F32)
        a_x = jnp.concatenate([jnp.where(low_half, cols[0], cols[1]),
                               jnp.where(low_half, cols[2], cols[3])], axis=-1)
        e_acs_x = jnp.exp2(a_x)
        a_last_x = a_x[chunk - 1:chunk, :]
        st = state[g]
        y_off = jnp.dot(cm, st.astype(BF16), preferred_element_type=F32) * e_acs_x
        xdec = (xdt_g * jnp.exp2(a_last_x - a_x)).astype(BF16)
        state[g] = st * e_acs_x[chunk - 1:chunk, :] + _tn_dot(bm, xdec)
        y = y_diag + y_off + dsk_ref[:, g * gw:(g + 1) * gw] * xs_g
        yz = y * z_ref[:, g * gw:(g + 1) * gw].astype(F32)
        yn = yz * lax.rsqrt(jnp.mean(yz * yz, axis=-1, keepdims=True) + EPS) * nw_ref[:, g * gw:(g + 1) * gw]
        yn_s[:, g * gw:(g + 1) * gw] = yn.astype(BF16)


def _mixers_kernel(q_ref, k_ref, v_ref, g_ref, din_ref, dq_ref, dk_ref, wret_ref,
                   z_ref, xbc_ref, dt_ref, dtb_ref, alog_ref, dsk_ref, nw_ref, tril_ref, exp_ref, wssm_ref,
                   ga_ref, gb_ref, x_ref, gm_ref, scf_ref, shf_ref, nf_ref, wo_ref, wr_ref, br_ref, tri_ref,
                   x1_ref, h2_ref, idx_ref, rank_ref, prow_ref, cnt_ref,
                   rstate, sstate, yn_s, ya_s, yb_s, cnt_s, *, decay_c, chunk, ssm_chunk, d_inner):
    b = pl.program_id(0)
    c = pl.program_id(1)

    @pl.when(c == 0)
    def _():
        rstate[...] = jnp.zeros_like(rstate)
        sstate[...] = jnp.zeros_like(sstate)

    @pl.when((b == 0) & (c == 0))
    def _():
        cnt_s[...] = jnp.zeros_like(cnt_s)

    n_sub = chunk // ssm_chunk
    acs, dt_x = _ssd_decays(dt_ref, dtb_ref, alog_ref, tril_ref, exp_ref, n_sub, ssm_chunk)
    _retention_body(q_ref, k_ref, v_ref, g_ref, din_ref, dq_ref, dk_ref, wret_ref, ya_s, rstate, decay_c)
    for sub in range(n_sub):
        rows = pl.ds(sub * ssm_chunk, ssm_chunk)
        _ssd_body(z_ref.at[rows, :], xbc_ref.at[rows, :], acs[sub],
                  dt_x[sub * ssm_chunk:(sub + 1) * ssm_chunk, :], dsk_ref, nw_ref, yn_s.at[rows, :],
                  sstate, ssm_chunk, d_inner)
    yb_s[...] = jnp.dot(yn_s[...], wssm_ref[...], preferred_element_type=F32)
    _merge_body(ya_s, yb_s, ga_ref, gb_ref, x_ref, gm_ref, scf_ref, shf_ref, nf_ref, wo_ref, wr_ref, br_ref,
                tri_ref, x1_ref, h2_ref, idx_ref, rank_ref, prow_ref, cnt_ref, cnt_s, chunk)


def _mixers_call(proj, dt_raw, x2, row0, mod3, w_ret, dt_bias, a_log, d_skip, ssm_norm, w_ssm,
                 norm_ffn, w_out, w_router_t, b_router, bsz, seq, chunk, ssm_chunk):
    t = proj.shape[0]
    d_inner, d = w_ssm.shape
    conv_dim = d_inner + 2 * SSM_GROUPS * SSM_STATE
    n_heads = d_inner // SSM_HEAD_DIM
    nc = seq // chunk
    gw = d_inner // SSM_GROUPS
    qk_w = RET_HEADS * RET_QK_DIM
    v_w = RET_HEADS * RET_V_DIM
    assert chunk % ssm_chunk == 0
    assert ssm_chunk == LANES, "the per-head decay tiles are built lane-for-lane against the chunk"
    din, dq, dk, dc = _retention_tables(chunk)
    pad_h = lambda v: jnp.pad(v.astype(F32), (0, LANES - n_heads)).reshape(1, LANES)
    tril = jnp.asarray(np.tril(np.ones((ssm_chunk, ssm_chunk), np.float32)), BF16)
    expand = np.zeros((LANES, d_inner), np.float32)
    for h in range(n_heads):
        expand[h, h * SSM_HEAD_DIM:(h + 1) * SSM_HEAD_DIM] = 1.0
    expand = jnp.asarray(expand, BF16)
    kern = functools.partial(_mixers_kernel, decay_c=dc, chunk=chunk, ssm_chunk=ssm_chunk, d_inner=d_inner)
    row = lambda b, c: b * nc + c
    z_blk = (2 * qk_w + 2 * v_w) // d_inner
    xbc_blk = (2 * qk_w + 2 * v_w + d_inner) // conv_dim
    full = lambda shape: pl.BlockSpec(shape, lambda b, c: (0,) * len(shape))
    nt = t // chunk
    ga_blk = proj.shape[1] // d - 2
    seq0 = row0 // seq
    modspec = lambda m: pl.BlockSpec((1, 1, d), lambda b, c: ((b + seq0) * N_MOD + m, 0, 0))
    tri = jnp.asarray(np.triu(np.ones((chunk, chunk), np.float32), 1), BF16)
    return pl.pallas_call(
        kern,
        out_shape=(jax.ShapeDtypeStruct((t, d), F32), jax.ShapeDtypeStruct((t * SLAB_ROWS, LANES), jnp.uint32),
                   jax.ShapeDtypeStruct((nt, SUBLANES, chunk), jnp.int32),
                   jax.ShapeDtypeStruct((nt, SUBLANES, chunk), jnp.int32),
                   jax.ShapeDtypeStruct((t, LANES), F32),
                   jax.ShapeDtypeStruct((N_EXPERTS, LANES), jnp.int32)),
        grid=(bsz, nc),
        in_specs=[
            pl.BlockSpec((chunk, qk_w), lambda b, c: (row(b, c), 0)),
            pl.BlockSpec((chunk, qk_w), lambda b, c: (row(b, c), 1)),
            pl.BlockSpec((chunk, v_w), lambda b, c: (row(b, c), 1)),
            pl.BlockSpec((chunk, v_w), lambda b, c: (row(b, c), 2)),
            full((RET_HEADS, chunk, chunk)), full((RET_HEADS, chunk, 1)), full((RET_HEADS, chunk, 1)),
            full((v_w, d)),
            pl.BlockSpec((chunk, d_inner), lambda b, c: (row(b, c), z_blk)),
            pl.BlockSpec((chunk, conv_dim), lambda b, c: (row(b, c), xbc_blk)),
            pl.BlockSpec((chunk, LANES), lambda b, c: (row(b, c), 0)),
            full((1, LANES)), full((1, LANES)),
            full((1, d_inner)), full((1, d_inner)), full((ssm_chunk, ssm_chunk)), full((LANES, d_inner)),
            full((d_inner, d)),
            pl.BlockSpec((chunk, d), lambda b, c: (row(b, c), ga_blk)),
            pl.BlockSpec((chunk, d), lambda b, c: (row(b, c), ga_blk + 1)),
            pl.BlockSpec((chunk, d), lambda b, c: (row0 // chunk + row(b, c), 0)),
            modspec(2), modspec(4), modspec(3),
            full((1, d)), full((d, d)), full((2 * N_EXPERTS, d)), full((N_EXPERTS, 1)), full((chunk, chunk)),
        ],
        out_specs=(pl.BlockSpec((chunk, d), lambda b, c: (row(b, c), 0)),
                   pl.BlockSpec((chunk * SLAB_ROWS, LANES), lambda b, c: (row(b, c), 0)),
                   pl.BlockSpec((1, SUBLANES, chunk), lambda b, c: (row(b, c), 0, 0)),
                   pl.BlockSpec((1, SUBLANES, chunk), lambda b, c: (row(b, c), 0, 0)),
                   pl.BlockSpec((chunk, LANES), lambda b, c: (row(b, c), 0)),
                   full((N_EXPERTS, LANES))),
        scratch_shapes=[pltpu.VMEM((RET_HEADS, RET_QK_DIM, RET_V_DIM), F32),
                        pltpu.VMEM((SSM_GROUPS, SSM_STATE, gw), F32),
                        pltpu.VMEM((chunk, d_inner), BF16),
                        pltpu.VMEM((chunk, d), F32), pltpu.VMEM((chunk, d), F32),
                        pltpu.VMEM((N_EXPERTS, LANES), F32)],
        compiler_params=_params(("arbitrary", "arbitrary")),
        name="mixers",
    )(proj, proj, proj, proj, din, dq, dk, w_ret,
      proj, proj, dt_raw, pad_h(dt_bias), pad_h(a_log),
      jnp.repeat(d_skip.astype(F32), SSM_HEAD_DIM).reshape(1, d_inner), ssm_norm.reshape(1, d_inner),
      tril, expand, w_ssm,
      proj, proj, x2, mod3, mod3, mod3, norm_ffn, w_out, w_router_t, b_router.reshape(N_EXPERTS, 1), tri)


def _merge_body(ya_ref, yb_ref, ga_ref, gb_ref, x_ref, gm_ref, scf_ref, shf_ref, nw_ref, wo_ref,
                wr_ref, br_ref, tri_ref,
                x1_ref, h2_ref, idx_ref, rank_ref, prow_ref, cnt_ref, cnt_s, tm):
    merged = (ga_ref[...].astype(F32) * ya_ref[...].astype(F32)
              + gb_ref[...].astype(F32) * yb_ref[...].astype(F32))
    mo = jnp.dot(merged.astype(BF16), wo_ref[...], preferred_element_type=F32)
    x1 = x_ref[...] + gm_ref[0] * mo
    x1_ref[...] = x1
    ms = jnp.mean(x1 * x1, axis=-1, keepdims=True)
    h2 = x1 * lax.rsqrt(ms + EPS) * nw_ref[...] * (1.0 + scf_ref[0]) + shf_ref[0]
    _store_slabs(h2_ref, h2, tm)

    h_hi = h2.astype(BF16)
    h_lo = (h2 - h_hi.astype(F32)).astype(BF16)
    lg2 = _nt_dot(wr_ref[...], h_hi)
    lg = lg2[:N_EXPERTS] + lg2[N_EXPERTS:] + _nt_dot(wr_ref[:N_EXPERTS, :], h_lo) + br_ref[...]
    sub = lax.broadcasted_iota(jnp.int32, lg.shape, 0)
    work = lg
    vals, idxs, sels = [], [], []
    for _ in range(TOP_K):
        m = jnp.max(work, axis=0, keepdims=True)
        ik = jnp.min(jnp.where(work == m, sub, N_EXPERTS), axis=0, keepdims=True)
        sel = sub == ik
        vals.append(m)
        idxs.append(ik)
        sels.append(sel)
        work = jnp.where(sel, -jnp.inf, work)
    exps = [jnp.exp(v - vals[0]) for v in vals]
    denom = exps[0]
    for e in exps[1:]:
        denom = denom + e
    probs = [e / denom for e in exps]

    base = cnt_s[:, 0:1]
    ranks = []
    for k in range(TOP_K):
        mk = jnp.where(sels[k], 1.0, 0.0)
        pre = jnp.dot(mk.astype(BF16), tri_ref[...], preferred_element_type=F32)
        ranks.append(jnp.sum(jnp.where(sels[k], pre + base, 0.0), axis=0, keepdims=True))
        base = base + jnp.sum(mk, axis=1, keepdims=True)
    cnt_s[...] = jnp.broadcast_to(base, cnt_s.shape)
    cnt_ref[...] = cnt_s[...].astype(jnp.int32)

    zi = jnp.zeros((SUBLANES - TOP_K, tm), jnp.int32)
    idx_ref[0] = jnp.concatenate(idxs + [zi], axis=0)
    rank_ref[0] = jnp.concatenate([r.astype(jnp.int32) for r in ranks] + [zi], axis=0)
    pt = jnp.concatenate(probs + [jnp.zeros((LANES - TOP_K, tm), F32)], axis=0)
    prow_ref[...] = pt.T


def _sc_mesh():
    return plsc.VectorSubcoreMesh(core_axis_name="c", subcore_axis_name="s")


def _sc_worker():
    return lax.axis_index("s") * SC_CORES + lax.axis_index("c")


def _sc_scatter_rows(rows, dest, n_out):
    t = rows.shape[0]
    n_k = dest.shape[0]
    g = SC_GROUP
    n_w = SC_CORES * SC_SUBCORES
    assert t % (n_w * g) == 0
    cpw = t // (n_w * g)
    dest_w = dest.reshape(n_k, n_w, cpw, g).transpose(1, 0, 2, 3)

    @functools.partial(
        pl.kernel, mesh=_sc_mesh(),
        out_type=jax.ShapeDtypeStruct((n_out,) + rows.shape[1:], rows.dtype),
        scratch_types=[pltpu.VMEM((n_k, cpw, g), jnp.int32),
                       pltpu.VMEM((g,) + rows.shape[1:], rows.dtype),
                       pltpu.SemaphoreType.DMA],
    )
    def scatter(rows_hbm, dest_hbm, out_hbm, idx_v, rows_v, sem):
        wid = _sc_worker()
        pltpu.sync_copy(dest_hbm.at[wid], idx_v)

        @pl.loop(0, cpw)
        def _(cc):
            r0 = pl.multiple_of((wid * cpw + cc) * g, g)
            pltpu.sync_copy(rows_hbm.at[pl.ds(r0, g)], rows_v)
            copies = [pltpu.async_copy(rows_v, out_hbm.at[idx_v.at[k, cc]], sem) for k in range(n_k)]
            for cp in copies:
                cp.wait()

    return scatter(rows, dest_w)


def _sc_gather_rows(table, idx):
    m = idx.shape[0]
    g = SC_GROUP
    n_w = SC_CORES * SC_SUBCORES
    assert m % (n_w * g) == 0
    per_w = m // n_w

    @functools.partial(
        pl.kernel, mesh=_sc_mesh(),
        out_type=jax.ShapeDtypeStruct((m,) + table.shape[1:], table.dtype),
        scratch_types=[pltpu.VMEM((per_w,), jnp.int32),
                       pltpu.VMEM((g,) + table.shape[1:], table.dtype),
                       pltpu.SemaphoreType.DMA],
    )
    def gather(table_hbm, idx_hbm, out_hbm, idx_v, rows_v, sem):
        base = _sc_worker() * per_w
        pltpu.sync_copy(idx_hbm.at[pl.ds(base, per_w)], idx_v)

        @pl.loop(0, per_w // g)
        def _(cc):
            off = pl.multiple_of(cc * g, g)
            pltpu.async_copy(table_hbm.at[idx_v.at[pl.ds(off, g)]], rows_v, sem).wait()
            pltpu.sync_copy(rows_v, out_hbm.at[pl.ds(base + off, g)])

    return gather(table, idx)


def _ffn_kernel(be_ref, br_ref, bv_ref, first_ref, slot_ref, next_ref, rows_ref,
                x_ref, wgu_hbm, bgu_ref, wd_hbm, bd_ref, o_ref, wgu_f, wd_f, wgu_s, wd_s, sem, *, bm, d_ff):
    i = pl.program_id(0)

    def fetch(e, slot):
        return (pltpu.make_async_copy(wgu_hbm.at[e], wgu_f.at[slot], sem.at[0, slot]),
                pltpu.make_async_copy(wd_hbm.at[e], wd_f.at[slot], sem.at[1, slot]))

    @pl.when(i == 0)
    def _():
        for cp in fetch(be_ref[0], slot_ref[0]):
            cp.start()

    @pl.when(first_ref[i] == 1)
    def _():
        slot = slot_ref[i]
        for cp in fetch(be_ref[i], slot):
            cp.wait()
        wgu_s[...] = wgu_f[slot].astype(BF16)
        wd_s[...] = wd_f[slot].astype(BF16)

        @pl.when(next_ref[i] >= 0)
        def _():
            for cp in fetch(next_ref[i], 1 - slot):
                cp.start()

    def expert_mlp(n, keep):
        x = jnp.concatenate(_load_slabs(x_ref, n, keep=keep), axis=-1).astype(BF16)
        gu = jnp.dot(x, wgu_s[...], preferred_element_type=F32) + bgu_ref[0]
        gate = jnp.minimum(gu[:, :d_ff], SWIGLU_LIMIT)
        up = jnp.clip(gu[:, d_ff:], -SWIGLU_LIMIT, SWIGLU_LIMIT)
        act = gate * jax.nn.sigmoid(SWIGLU_ALPHA * gate) * (up + 1.0)
        y = jnp.dot(act.astype(BF16), wd_s[...], preferred_element_type=F32) + bd_ref[0]
        _store_slabs(o_ref, y, n)
        if n < bm:
            o_ref[pl.ds(n * SLAB_ROWS, (bm - n) * SLAB_ROWS), :] = jnp.zeros(
                ((bm - n) * SLAB_ROWS, LANES), jnp.uint32)

    @pl.when((bv_ref[i] == 1) & (rows_ref[i] == bm))
    def _():
        expert_mlp(bm, None)

    for q in range(1, bm // FFN_TAIL + 1):
        n = q * FFN_TAIL
        lo = (q - 1) * FFN_TAIL if q > 1 else -1

        @pl.when((bv_ref[i] == 1) & (rows_ref[i] < bm) & (rows_ref[i] > lo) & (rows_ref[i] <= n))
        def _(n=n):
            expert_mlp(n, lax.broadcasted_iota(jnp.int32, (n, LANES), 0) < rows_ref[i])

    @pl.when(bv_ref[i] == 0)
    def _():
        o_ref[...] = jnp.zeros_like(o_ref)


def _ffn_call(blk_e, blk_row, blk_valid, blk_rows, xs, w_gu, b_gu, w_d, b_d, bm):
    n_e, d, f2 = w_gu.shape
    d_ff = f2 // 2
    nb = blk_e.shape[0]
    first = jnp.concatenate([jnp.ones((1,), jnp.int32), (blk_e[1:] != blk_e[:-1]).astype(jnp.int32)])
    slot = (jnp.cumsum(first) - 1) % 2
    later = jnp.where(blk_e[None, :] > blk_e[:, None], blk_e[None, :], n_e)
    nxt = jnp.min(later, axis=1)
    nxt = jnp.where(nxt == n_e, -1, nxt)
    kern = functools.partial(_ffn_kernel, bm=bm, d_ff=d_ff)
    imap = lambda f: (lambda i, be, br, bv, fi, sl, nx, rw: f(i, be, br))
    gs = pltpu.PrefetchScalarGridSpec(
        num_scalar_prefetch=7,
        grid=(nb,),
        in_specs=[pl.BlockSpec((bm * SLAB_ROWS, LANES), imap(lambda i, be, br: (br[i], 0))),
                  pl.BlockSpec(memory_space=pl.ANY),
                  pl.BlockSpec((1, 1, f2), imap(lambda i, be, br: (be[i], 0, 0))),
                  pl.BlockSpec(memory_space=pl.ANY),
                  pl.BlockSpec((1, 1, d), imap(lambda i, be, br: (be[i], 0, 0)))],
        out_specs=pl.BlockSpec((bm * SLAB_ROWS, LANES), imap(lambda i, be, br: (i, 0))),
        scratch_shapes=[pltpu.VMEM((2, d, f2), F32), pltpu.VMEM((2, d_ff, d), F32),
                        pltpu.VMEM((d, f2), BF16), pltpu.VMEM((d_ff, d), BF16),
                        pltpu.SemaphoreType.DMA((2, 2))],
    )
    return pl.pallas_call(
        kern,
        out_shape=jax.ShapeDtypeStruct((nb * bm * SLAB_ROWS, LANES), jnp.uint32),
        grid_spec=gs,
        compiler_params=_params(("arbitrary",)),
        name="ffn",
    )(blk_e, blk_row, blk_valid, first, slot.astype(jnp.int32), nxt.astype(jnp.int32), blk_rows,
      xs, w_gu, b_gu.reshape(n_e, 1, f2), w_d, b_d.reshape(n_e, 1, d))


def _combine_kernel(y0_ref, y1_ref, y2_ref, y3_ref, prow_ref, x1_ref, gf_ref, nw_ref, *rest, tm):
    o_ref = rest[-1]
    p = prow_ref[...]
    pieces = [_load_slabs(y_ref, tm) for y_ref in (y0_ref, y1_ref, y2_ref, y3_ref)]
    for s in range(len(pieces[0])):
        moe = None
        for k in range(TOP_K):
            piece = pieces[k][s] * p[:, k:k + 1]
            moe = piece if moe is None else moe + piece
        sl = slice(s * LANES, (s + 1) * LANES)
        o_ref[:, sl] = x1_ref[:, sl] + gf_ref[0][:, sl] * moe
    xo = o_ref[...]
    o_ref[...] = xo * lax.rsqrt(jnp.mean(xo * xo, axis=-1, keepdims=True) + EPS) * nw_ref[...]


def _combine_call(ytok, prow, x1, mod3, norm_final, seq, tm, row0, t_total, out_prev):
    t, d = x1.shape
    nt = t // tm
    tiles_per_seq = seq // tm
    off = row0 // tm
    kern = functools.partial(_combine_kernel, tm=tm)
    yspec = lambda k: pl.BlockSpec((tm * SLAB_ROWS, LANES), lambda i: (k * nt + i, 0))
    in_specs = [yspec(0), yspec(1), yspec(2), yspec(3),
                pl.BlockSpec((tm, LANES), lambda i: (i, 0)),
                pl.BlockSpec((tm, d), lambda i: (i, 0)),
                pl.BlockSpec((1, 1, d), lambda i: (((i + off) // tiles_per_seq) * N_MOD + 5, 0, 0)),
                pl.BlockSpec((1, d), lambda i: (0, 0))]
    args = [ytok, ytok, ytok, ytok, prow, x1, mod3, norm_final]
    aliases = {}
    if out_prev is not None:
        in_specs.append(pl.BlockSpec(memory_space=pl.ANY))
        aliases = {len(args): 0}
        args.append(out_prev)
    return pl.pallas_call(
        kern,
        out_shape=jax.ShapeDtypeStruct((t_total, d), F32),
        grid=(nt,),
        in_specs=in_specs,
        out_specs=pl.BlockSpec((tm, d), lambda i: (i + off, 0)),
        input_output_aliases=aliases,
        compiler_params=_params(("arbitrary",)),
        name="combine",
    )(*args)


def _plan(seq):
    def fit(pref):
        tm = min(pref, seq)
        assert seq % tm == 0
        return tm
    return dict(tm_in=fit(1024), tm_moe=fit(1024), ret_chunk=fit(RET_CHUNK), ssm_chunk=fit(SSM_CHUNK))


def _layer(x2, mod3, bsz, seq, norm_mix, norm_ffn, w_in, conv_w, conv_b, dt_bias, a_log, d_skip, ssm_norm,
           w_ret_out, w_ssm_out, w_out, w_router, b_router, w_gate_up, b_gate_up, w_down, b_down,
           norm_final):
    t, d = x2.shape
    plan = _plan(seq)
    qk_w = RET_HEADS * RET_QK_DIM
    v_w = RET_HEADS * RET_V_DIM
    d_inner = w_ssm_out.shape[0]
    conv_dim = conv_w.shape[1]
    n_heads = d_inner // SSM_HEAD_DIM
    dt_off = 2 * qk_w + 2 * v_w + d_inner + conv_dim

    w_all = w_in.astype(BF16)
    w_gate = w_in[:, dt_off + n_heads:].astype(BF16)
    w_dt = jnp.pad(w_in[:, dt_off:dt_off + n_heads], ((0, 0), (0, LANES - n_heads)))
    w_dt_hi = w_dt.astype(BF16)
    w_dt = jnp.concatenate([w_dt_hi, (w_dt - w_dt_hi.astype(F32)).astype(BF16)], axis=1)
    half = RET_QK_DIM // 2
    inv_freq = ROPE_BASE ** (-jnp.arange(half, dtype=F32) / half)
    ang = jnp.arange(seq, dtype=F32)[:, None] * inv_freq[None, :]
    cos, sin = jnp.cos(ang), jnp.sin(ang)

    w_ret_b, w_ssm_b, w_out_b = w_ret_out.astype(BF16), w_ssm_out.astype(BF16), w_out.astype(BF16)
    w_r_hi = w_router.T.astype(BF16)
    w_router_t = jnp.concatenate([w_r_hi, (w_router.T - w_r_hi.astype(F32)).astype(BF16)], axis=0)
    bm = FFN_BLOCK
    slab = (SLAB_ROWS, LANES)

    def mixer(row0, tg, bg):
        proj, dt_raw = _inproj_call(x2, row0, tg, norm_mix.reshape(1, d), mod3, cos, sin, w_all, w_gate, w_dt,
                                    conv_w, conv_b, 2 * qk_w + 2 * v_w + d_inner, seq, plan["tm_in"], 2 * qk_w)
        x1, h2, idx, rank, prow, cnt = _mixers_call(
            proj, dt_raw, x2, row0, mod3, w_ret_b, dt_bias, a_log, d_skip, ssm_norm, w_ssm_b,
            norm_ffn.reshape(1, d), w_out_b, w_router_t, b_router, bg, seq, plan["ret_chunk"], plan["ssm_chunk"])
        counts = cnt[:, 0]
        padded = ((counts + bm - 1) // bm) * bm
        pad_end = jnp.cumsum(padded)
        start_pad = pad_end - padded
        n_blocks = -(-(tg * TOP_K) // bm) + N_EXPERTS
        e_ids = jnp.arange(N_EXPERTS, dtype=jnp.int32)[:, None, None, None]
        dest = rank + jnp.sum(jnp.where(idx[None] == e_ids, start_pad[:, None, None, None], 0), axis=0)
        dest = dest[:, :TOP_K, :].transpose(1, 0, 2).reshape(TOP_K, tg).astype(jnp.int32)
        n_real = pad_end[-1] // bm
        blk_valid = (jnp.arange(n_blocks) < n_real).astype(jnp.int32)
        blk_row = jnp.minimum(jnp.arange(n_blocks), n_real - 1).astype(jnp.int32)
        blk_e = jnp.minimum(jnp.sum(pad_end[None, :] <= (blk_row * bm)[:, None], axis=1),
                            N_EXPERTS - 1).astype(jnp.int32)
        sel_e = blk_e[:, None] == jnp.arange(N_EXPERTS, dtype=jnp.int32)[None, :]
        row_end = jnp.sum(jnp.where(sel_e, (start_pad + counts)[None, :], 0), axis=1)
        blk_rows = jnp.clip(row_end - blk_row * bm, 0, bm).astype(jnp.int32)
        xs = _sc_scatter_rows(h2.reshape((tg,) + slab), dest, n_blocks * bm)
        return dict(x1=x1, prow=prow, dest=dest, blocks=(blk_e, blk_row, blk_valid, blk_rows), xs=xs,
                    row0=row0)

    def experts(m):
        n_rows = m["xs"].shape[0]
        ys = _ffn_call(*m["blocks"], m["xs"].reshape(n_rows * SLAB_ROWS, LANES),
                       w_gate_up, b_gate_up, w_down, b_down, bm)
        return _sc_gather_rows(ys.reshape((n_rows,) + slab), m["dest"].reshape(-1))

    n_groups = N_GROUPS if bsz % N_GROUPS == 0 else 1
    bg = bsz // n_groups
    tg = bg * seq
    groups = [mixer(g * tg, tg, bg) for g in range(n_groups)]
    ytoks = [experts(m) for m in groups]
    out = None
    for y, m in zip(ytoks, groups):
        out = _combine_call(y.reshape(TOP_K * tg * SLAB_ROWS, LANES), m["prow"], m["x1"], mod3,
                            norm_final.reshape(1, d), seq, plan["tm_moe"], m["row0"], t, out)
    return out


def kernel(x, c, w_ada, b_ada, norm_mix, norm_ffn, w_in, conv_w, conv_b, dt_bias, a_log, d_skip, ssm_norm,
           w_ret_out, w_ssm_out, w_out, w_router, b_router, w_gate_up, b_gate_up, w_down, b_down, norm_final):
    bsz, seq, d = x.shape
    depth = w_ada.shape[0]
    assert depth == 1, "the final norm is fused into the single layer's last kernel"
    x2 = x.reshape(bsz * seq, d)
    l = 0
    mod = _mod_call(c, w_ada[l], b_ada[l])
    mod3 = mod.reshape(bsz * N_MOD, 1, d)
    out = _layer(x2, mod3, bsz, seq, norm_mix[l], norm_ffn[l], w_in[l], conv_w[l], conv_b[l], dt_bias[l],
                 a_log[l], d_skip[l], ssm_norm[l], w_ret_out[l], w_ssm_out[l], w_out[l], w_router[l],
                 b_router[l], w_gate_up[l], b_gate_up[l], w_down[l], b_down[l], norm_final)
    return out.reshape(bsz, seq, d)
```

```python
import functools
import math

import numpy as np
import jax
import jax.numpy as jnp
from jax import lax
from jax.experimental import pallas as pl
from jax.experimental.pallas import tpu as pltpu
from jax.experimental.pallas import tpu_sc as plsc

F32 = jnp.float32
BF16 = jnp.bfloat16
HIGHEST = lax.Precision.HIGHEST

EPS = 1e-6
N_MOD = 6
RET_HEADS = 4
RET_QK_DIM = 256
RET_V_DIM = 512
ROPE_BASE = 10000.0
SSM_HEAD_DIM = 64
SSM_GROUPS = 8
SSM_STATE = 128
SSM_CONV = 4
N_EXPERTS = 32
TOP_K = 4
SWIGLU_LIMIT = 7.0
SWIGLU_ALPHA = 1.702

LANES = 128
SUBLANES = 8
VMEM_LIMIT = 56 * 1024 * 1024

RET_CHUNK = 256
SSM_CHUNK = 128
FFN_BLOCK = 512
FFN_TAIL = 128
SC_CORES = 2
SC_SUBCORES = 16
SC_GROUP = 64
N_GROUPS = 2


def _params(sem, vmem=VMEM_LIMIT):
    return pltpu.CompilerParams(dimension_semantics=sem, vmem_limit_bytes=vmem)


def _nt_dot(a, b, **kw):
    return lax.dot_general(a, b, (((1,), (1,)), ((), ())), preferred_element_type=F32, **kw)


def _tn_dot(a, b, **kw):
    return lax.dot_general(a, b, (((0,), (0,)), ((), ())), preferred_element_type=F32, **kw)


def _silu(v):
    return v * jax.nn.sigmoid(v)


SLAB_ROWS = 4
HIGH_HALF = 0xFFFF0000
LOG2_E = math.log2(math.e)


def _store_slabs(ref, vals, n):
    for s in range(SLAB_ROWS):
        lo = vals[:, s * LANES:(s + 1) * LANES].astype(BF16).astype(F32)
        hi = vals[:, (s + SLAB_ROWS) * LANES:(s + SLAB_ROWS + 1) * LANES].astype(BF16).astype(F32)
        word = (pltpu.bitcast(lo, jnp.uint32) >> 16) | (pltpu.bitcast(hi, jnp.uint32) & jnp.uint32(HIGH_HALF))
        ref[pl.ds(s, n, stride=SLAB_ROWS), :] = word


def _load_slabs(ref, n, base=0, keep=None):
    lo, hi = [], []
    for s in range(SLAB_ROWS):
        word = ref[pl.ds(base + s, n, stride=SLAB_ROWS), :]
        if keep is not None:
            word = jnp.where(keep, word, jnp.zeros_like(word))
        lo.append(pltpu.bitcast(word << 16, F32))
        hi.append(pltpu.bitcast(word & jnp.uint32(HIGH_HALF), F32))
    return lo + hi


def _mod_kernel(c_ref, w_ref, b_ref, o_ref):
    cond = _silu(c_ref[...])
    o_ref[...] = jnp.dot(cond, w_ref[...], preferred_element_type=F32, precision=HIGHEST) + b_ref[...]


def _mod_call(c, w_ada, b_ada):
    bsz, d = c.shape
    n = w_ada.shape[1]
    return pl.pallas_call(
        _mod_kernel,
        out_shape=jax.ShapeDtypeStruct((bsz, n), F32),
        grid=(n // d,),
        in_specs=[pl.BlockSpec((bsz, d), lambda j: (0, 0)),
                  pl.BlockSpec((d, d), lambda j: (0, j)),
                  pl.BlockSpec((1, d), lambda j: (0, j))],
        out_specs=pl.BlockSpec((bsz, d), lambda j: (0, j)),
        compiler_params=_params(("arbitrary",)),
        name="mod",
    )(c, w_ada, b_ada.reshape(1, n))


def _inproj_kernel(x_ref, nw_ref, sc_ref, sh_ref, cos_ref, sin_ref, w_ref, wg_ref, wdt_ref, cw_ref, cb_ref,
                   o_ref, dt_ref, h_s, work, carry, *, conv_j0, conv_nj, silu_j, sigm_j, tiles_per_seq,
                   tm, tn, sub):
    i = pl.program_id(0)
    j = pl.program_id(1)
    n_dt = dt_ref.shape[1]
    rows = min(tm, 256)

    @pl.when(j == 0)
    def _():
        xf = x_ref[...]
        ms = jnp.mean(xf * xf, axis=-1, keepdims=True)
        y = xf * lax.rsqrt(ms + EPS) * nw_ref[...]
        hm = y * (1.0 + sc_ref[0]) + sh_ref[0]
        hb = hm.astype(BF16)
        h_s[...] = hb
        h_lo = (hm - hb.astype(F32)).astype(BF16)
        d_hi = jnp.dot(hb, wdt_ref[...], preferred_element_type=F32)
        d_lo = jnp.dot(h_lo, wdt_ref[:, :n_dt], preferred_element_type=F32)
        dt_ref[...] = d_hi[:, :n_dt] + d_hi[:, n_dt:] + d_lo
        half = RET_QK_DIM // 2
        for p in range(tn // sub):
            for r in range(tm // rows):
                rs = slice(r * rows, (r + 1) * rows)
                acc = jnp.dot(h_s[rs, :], w_ref[:, p * sub:(p + 1) * sub], preferred_element_type=F32)
                cos = cos_ref[rs, :]
                sin = sin_ref[rs, :]
                for cc in range(sub // RET_QK_DIM):
                    c = p * (sub // RET_QK_DIM) + cc
                    a = acc[:, cc * RET_QK_DIM: cc * RET_QK_DIM + half]
                    b = acc[:, cc * RET_QK_DIM + half: (cc + 1) * RET_QK_DIM]
                    scale = 1.0 if c < RET_HEADS else RET_QK_DIM ** -0.5
                    o_ref[rs, c * RET_QK_DIM: c * RET_QK_DIM + half] = ((a * cos - b * sin) * scale).astype(BF16)
                    o_ref[rs, c * RET_QK_DIM + half: (c + 1) * RET_QK_DIM] = (
                        (a * sin + b * cos) * scale).astype(BF16)

    is_conv = (j >= conv_j0) & (j < conv_j0 + conv_nj)

    @pl.when(is_conv)
    def _():
        cj = j - conv_j0
        pad = SUBLANES

        @pl.when(i % tiles_per_seq == 0)
        def _():
            carry[cj] = jnp.zeros(carry.shape[1:], F32)

        for p in range(tn // sub):
            for r in range(tm // rows):
                r0 = r * rows
                acc = jnp.dot(h_s[r0:r0 + rows, :], w_ref[:, p * sub:(p + 1) * sub], preferred_element_type=F32)
                for cc in range(sub // LANES):
                    c = p * (sub // LANES) + cc
                    cols = slice(c * LANES, (c + 1) * LANES)
                    if r == 0:
                        work[c, 0:pad, :] = carry[cj, c]
                    lo = pad + r0
                    work[c, lo:lo + rows, :] = acc[:, cc * LANES:(cc + 1) * LANES]
                    conv = cb_ref[:, cols] + cw_ref[SSM_CONV - 1:SSM_CONV, cols] * work[c, lo:lo + rows, :]
                    for k in range(SSM_CONV - 1):
                        shift = SSM_CONV - 1 - k
                        conv = conv + cw_ref[k:k + 1, cols] * work[c, lo - shift:lo - shift + rows, :]
                    if r0 + rows == tm:
                        carry[cj, c] = work[c, tm:tm + pad, :]
                    o_ref[r0:r0 + rows, cols] = _silu(conv).astype(BF16)

    def plain(act, weights):
        for p in range(tn // sub):
            for r in range(tm // rows):
                acc = jnp.dot(h_s[r * rows:(r + 1) * rows, :], weights[:, p * sub:(p + 1) * sub],
                              preferred_element_type=F32)
                o_ref[r * rows:(r + 1) * rows, p * sub:(p + 1) * sub] = act(acc).astype(BF16)

    is_silu = (j >= silu_j[0]) & (j < silu_j[1])
    is_sigm = (j >= sigm_j[0]) & (j < sigm_j[1])
    pl.when(is_silu)(lambda: plain(_silu, w_ref))
    pl.when(is_sigm)(lambda: plain(jax.nn.sigmoid, wg_ref))
    pl.when((j != 0) & jnp.logical_not(is_conv | is_silu | is_sigm))(lambda: plain(lambda v: v, w_ref))


def _inproj_call(x2, row0, t, norm_w, mod3, cos, sin, w_all, w_gate, w_dt, conv_w, conv_b, conv_off, seq, tm, tn):
    d = x2.shape[1]
    conv_dim = conv_w.shape[1]
    n_lead = conv_off + conv_dim
    n = n_lead + w_gate.shape[1]
    assert w_gate.shape[1] == tn and n_lead % tn == 0
    tiles_per_seq = seq // tm
    off = row0 // tm
    assert tn == 2 * RET_HEADS * RET_QK_DIM, "rotary epilogue expects q and k in the first column tile"
    assert conv_off % tn == 0 and conv_dim % tn == 0
    conv_j0, conv_nj = conv_off // tn, conv_dim // tn
    sub = 512
    g_off = 2 * RET_HEADS * RET_QK_DIM + RET_HEADS * RET_V_DIM
    assert g_off % tn == 0 and (conv_off - g_off) % tn == 0 and (n - conv_off - conv_dim) % tn == 0
    silu_j = (g_off // tn, conv_off // tn)
    sigm_j = ((conv_off + conv_dim) // tn, n // tn)
    kern = functools.partial(_inproj_kernel, conv_j0=conv_j0, conv_nj=conv_nj, silu_j=silu_j, sigm_j=sigm_j,
                             tiles_per_seq=tiles_per_seq, tm=tm, tn=tn, sub=sub)
    conv_idx = lambda i, j: (0, jnp.clip(j - conv_j0, 0, conv_nj - 1))
    return pl.pallas_call(
        kern,
        out_shape=(jax.ShapeDtypeStruct((t, n), BF16), jax.ShapeDtypeStruct((t, LANES), F32)),
        grid=(t // tm, n // tn),
        in_specs=[
            pl.BlockSpec((tm, d), lambda i, j: (i + off, 0)),
            pl.BlockSpec((1, d), lambda i, j: (0, 0)),
            pl.BlockSpec((1, 1, d), lambda i, j: (((i + off) // tiles_per_seq) * N_MOD + 1, 0, 0)),
            pl.BlockSpec((1, 1, d), lambda i, j: (((i + off) // tiles_per_seq) * N_MOD + 0, 0, 0)),
            pl.BlockSpec((tm, LANES), lambda i, j: (i % tiles_per_seq, 0)),
            pl.BlockSpec((tm, LANES), lambda i, j: (i % tiles_per_seq, 0)),
            pl.BlockSpec((d, tn), lambda i, j: (0, jnp.minimum(j, n_lead // tn - 1))),
            pl.BlockSpec((d, tn), lambda i, j: (0, 0)),
            pl.BlockSpec((d, 2 * LANES), lambda i, j: (0, 0)),
            pl.BlockSpec((SSM_CONV, tn), conv_idx),
            pl.BlockSpec((1, tn), conv_idx),
        ],
        out_specs=(pl.BlockSpec((tm, tn), lambda i, j: (i, j)),
                   pl.BlockSpec((tm, LANES), lambda i, j: (i, 0))),
        scratch_shapes=[pltpu.VMEM((tm, d), BF16),
                        pltpu.VMEM((tn // LANES, tm + SUBLANES, LANES), F32),
                        pltpu.VMEM((conv_nj, tn // LANES, SUBLANES, LANES), F32)],
        compiler_params=_params(("arbitrary", "arbitrary")),
        name="inproj",
    )(x2, norm_w, mod3, mod3, cos, sin, w_all, w_gate, w_dt, conv_w, conv_b.reshape(1, conv_dim))


def _retention_body(q_ref, k_ref, v_ref, g_ref, din_ref, dq_ref, dk_ref, w_ref, o_ref, state, decay_c):
    heads = range(RET_HEADS)
    q = [q_ref[:, h * RET_QK_DIM:(h + 1) * RET_QK_DIM] for h in heads]
    k = [k_ref[:, h * RET_QK_DIM:(h + 1) * RET_QK_DIM] for h in heads]
    v = [v_ref[:, h * RET_V_DIM:(h + 1) * RET_V_DIM] for h in heads]
    gated = []
    for h in heads:
        scores = (_nt_dot(q[h], k[h]) * din_ref[h]).astype(BF16)
        st = state[h]
        cross = jnp.dot(q[h], st.astype(BF16), preferred_element_type=F32) * dq_ref[h]
        inner = jnp.dot(scores, v[h], preferred_element_type=F32)
        kd = (k[h].astype(F32) * dk_ref[h]).astype(BF16)
        state[h] = st * decay_c[h] + _tn_dot(kd, v[h])
        ret = inner + cross
        ret = ret * lax.rsqrt(jnp.mean(ret * ret, axis=-1, keepdims=True) + EPS)
        ret = ret * g_ref[:, h * RET_V_DIM:(h + 1) * RET_V_DIM].astype(F32)
        gated.append(ret.astype(BF16))
    o_ref[...] = jnp.dot(jnp.concatenate(gated, axis=-1), w_ref[...],
                         preferred_element_type=F32).astype(o_ref.dtype)


def _retention_tables(chunk):
    lg = np.log(1.0 - 2.0 ** (-5.0 - np.arange(RET_HEADS, dtype=np.float64)))
    idx = np.arange(chunk, dtype=np.float64)
    rel = idx[:, None] - idx[None, :]
    causal = rel >= 0
    din = np.where(causal[None], np.exp(np.where(causal, rel, 0.0)[None] * lg[:, None, None]), 0.0)
    dq = np.exp((idx + 1.0)[None, :, None] * lg[:, None, None])
    dk = np.exp((chunk - 1.0 - idx)[None, :, None] * lg[:, None, None])
    dc = tuple(float(v) for v in np.exp(chunk * lg))
    return (jnp.asarray(din, F32), jnp.asarray(dq, F32), jnp.asarray(dk, F32), dc)


def _ssd_decays(dt_ref, dtb_ref, alog_ref, tril_ref, exp_ref, n_sub, chunk):
    dt = jax.nn.softplus(dt_ref[...] + dtb_ref[...])
    adt = dt * (-LOG2_E * jnp.exp(alog_ref[...]))
    p1 = adt.astype(BF16)
    r1 = adt - p1.astype(F32)
    p2 = r1.astype(BF16)
    p3 = (r1 - p2.astype(F32)).astype(BF16)
    pieces = jnp.concatenate([p1, p2, p3], axis=-1)
    acs = []
    for s in range(n_sub):
        c3 = jnp.dot(tril_ref[...], pieces[s * chunk:(s + 1) * chunk, :], preferred_element_type=F32)
        acs.append(c3[:, :LANES] + c3[:, LANES:2 * LANES] + c3[:, 2 * LANES:])
    dt_x = jnp.dot(dt.astype(BF16), exp_ref[...], preferred_element_type=F32)
    return acs, dt_x


def _ssd_body(z_ref, xbc_ref, acs, dt_x, dsk_ref, nw_ref, yn_s, state, chunk, d_inner):
    heads_per_group = d_inner // SSM_HEAD_DIM // SSM_GROUPS
    gw = heads_per_group * SSM_HEAD_DIM
    assert SSM_HEAD_DIM * 2 == LANES and gw == 2 * LANES

    acs_t = acs.T
    li = lax.broadcasted_iota(jnp.int32, (chunk, chunk), 0)
    si = lax.broadcasted_iota(jnp.int32, (chunk, chunk), 1)
    causal = li >= si
    low_half = si < SSM_HEAD_DIM
    lane_g = lax.broadcasted_iota(jnp.int32, (chunk, gw), 1)

    b_off = d_inner
    c_off = d_inner + SSM_GROUPS * SSM_STATE
    for g in range(SSM_GROUPS):
        bm = xbc_ref[:, b_off + g * SSM_STATE: b_off + (g + 1) * SSM_STATE]
        cm = xbc_ref[:, c_off + g * SSM_STATE: c_off + (g + 1) * SSM_STATE]
        xs_g = xbc_ref[:, g * gw:(g + 1) * gw].astype(F32)
        xdt_g = xs_g * dt_x[:, g * gw:(g + 1) * gw]
        xdt_b = xdt_g.astype(BF16)
        cb = _nt_dot(cm, bm)
        cols, ms, xm = [], [], []
        for jh in range(heads_per_group):
            h = g * heads_per_group + jh
            col = jnp.broadcast_to(acs[:, h:h + 1], (chunk, chunk))
            seg = jnp.exp2(jnp.where(causal, col - acs_t[h:h + 1, :], -jnp.inf))
            cols.append(col)
            ms.append((cb * seg).astype(BF16))
            in_head = (lane_g >= jh * SSM_HEAD_DIM) & (lane_g < (jh + 1) * SSM_HEAD_DIM)
            xm.append(jnp.where(in_head, xdt_b, jnp.zeros_like(xdt_b)))
        y_diag = jnp.dot(jnp.concatenate(ms, axis=-1), jnp.concatenate(xm, axis=0),
                         preferred_element_type=F32)
        a_x = jnp.concatenate([jnp.where(low_half, cols[0], cols[1]),
                               jnp.where(low_half, cols[2], cols[3])], axis=-1)
        e_acs_x = jnp.exp2(a_x)
        a_last_x = a_x[chunk - 1:chunk, :]
        st = state[g]
        y_off = jnp.dot(cm, st.astype(BF16), preferred_element_type=F32) * e_acs_x
        xdec = (xdt_g * jnp.exp2(a_last_x - a_x)).astype(BF16)
        state[g] = st * e_acs_x[chunk - 1:chunk, :] + _tn_dot(bm, xdec)
        y = y_diag + y_off + dsk_ref[:, g * gw:(g + 1) * gw] * xs_g
        yz = y * z_ref[:, g * gw:(g + 1) * gw].astype(F32)
        yn = yz * lax.rsqrt(jnp.mean(yz * yz, axis=-1, keepdims=True) + EPS) * nw_ref[:, g * gw:(g + 1) * gw]
        yn_s[:, g * gw:(g + 1) * gw] = yn.astype(BF16)


def _mixers_kernel(q_ref, k_ref, v_ref, g_ref, din_ref, dq_ref, dk_ref, wret_ref,
                   z_ref, xbc_ref, dt_ref, dtb_ref, alog_ref, dsk_ref, nw_ref, tril_ref, exp_ref, wssm_ref,
                   ga_ref, gb_ref, x_ref, gm_ref, scf_ref, shf_ref, nf_ref, wo_ref, wr_ref, br_ref, tri_ref,
                   x1_ref, h2_ref, idx_ref, rank_ref, prow_ref, cnt_ref,
                   rstate, sstate, yn_s, ya_s, yb_s, cnt_s, *, decay_c, chunk, ssm_chunk, d_inner):
    b = pl.program_id(0)
    c = pl.program_id(1)

    @pl.when(c == 0)
    def _():
        rstate[...] = jnp.zeros_like(rstate)
        sstate[...] = jnp.zeros_like(sstate)

    @pl.when((b == 0) & (c == 0))
    def _():
        cnt_s[...] = jnp.zeros_like(cnt_s)

    n_sub = chunk // ssm_chunk
    acs, dt_x = _ssd_decays(dt_ref, dtb_ref, alog_ref, tril_ref, exp_ref, n_sub, ssm_chunk)
    _retention_body(q_ref, k_ref, v_ref, g_ref, din_ref, dq_ref, dk_ref, wret_ref, ya_s, rstate, decay_c)
    for sub in range(n_sub):
        rows = pl.ds(sub * ssm_chunk, ssm_chunk)
        _ssd_body(z_ref.at[rows, :], xbc_ref.at[rows, :], acs[sub],
                  dt_x[sub * ssm_chunk:(sub + 1) * ssm_chunk, :], dsk_ref, nw_ref, yn_s.at[rows, :],
                  sstate, ssm_chunk, d_inner)
    yb_s[...] = jnp.dot(yn_s[...], wssm_ref[...], preferred_element_type=F32)
    _merge_body(ya_s, yb_s, ga_ref, gb_ref, x_ref, gm_ref, scf_ref, shf_ref, nf_ref, wo_ref, wr_ref, br_ref,
                tri_ref, x1_ref, h2_ref, idx_ref, rank_ref, prow_ref, cnt_ref, cnt_s, chunk)


def _mixers_call(proj, dt_raw, x2, row0, mod3, w_ret, dt_bias, a_log, d_skip, ssm_norm, w_ssm,
                 norm_ffn, w_out, w_router_t, b_router, bsz, seq, chunk, ssm_chunk):
    t = proj.shape[0]
    d_inner, d = w_ssm.shape
    conv_dim = d_inner + 2 * SSM_GROUPS * SSM_STATE
    n_heads = d_inner // SSM_HEAD_DIM
    nc = seq // chunk
    gw = d_inner // SSM_GROUPS
    qk_w = RET_HEADS * RET_QK_DIM
    v_w = RET_HEADS * RET_V_DIM
    assert chunk % ssm_chunk == 0
    assert ssm_chunk == LANES, "the per-head decay tiles are built lane-for-lane against the chunk"
    din, dq, dk, dc = _retention_tables(chunk)
    pad_h = lambda v: jnp.pad(v.astype(F32), (0, LANES - n_heads)).reshape(1, LANES)
    tril = jnp.asarray(np.tril(np.ones((ssm_chunk, ssm_chunk), np.float32)), BF16)
    expand = np.zeros((LANES, d_inner), np.float32)
    for h in range(n_heads):
        expand[h, h * SSM_HEAD_DIM:(h + 1) * SSM_HEAD_DIM] = 1.0
    expand = jnp.asarray(expand, BF16)
    kern = functools.partial(_mixers_kernel, decay_c=dc, chunk=chunk, ssm_chunk=ssm_chunk, d_inner=d_inner)
    row = lambda b, c: b * nc + c
    z_blk = (2 * qk_w + 2 * v_w) // d_inner
    xbc_blk = (2 * qk_w + 2 * v_w + d_inner) // conv_dim
    full = lambda shape: pl.BlockSpec(shape, lambda b, c: (0,) * len(shape))
    nt = t // chunk
    ga_blk = proj.shape[1] // d - 2
    seq0 = row0 // seq
    modspec = lambda m: pl.BlockSpec((1, 1, d), lambda b, c: ((b + seq0) * N_MOD + m, 0, 0))
    tri = jnp.asarray(np.triu(np.ones((chunk, chunk), np.float32), 1), BF16)
    return pl.pallas_call(
        kern,
        out_shape=(jax.ShapeDtypeStruct((t, d), F32), jax.ShapeDtypeStruct((t * SLAB_ROWS, LANES), jnp.uint32),
                   jax.ShapeDtypeStruct((nt, SUBLANES, chunk), jnp.int32),
                   jax.ShapeDtypeStruct((nt, SUBLANES, chunk), jnp.int32),
                   jax.ShapeDtypeStruct((t, LANES), F32),
                   jax.ShapeDtypeStruct((N_EXPERTS, LANES), jnp.int32)),
        grid=(bsz, nc),
        in_specs=[
            pl.BlockSpec((chunk, qk_w), lambda b, c: (row(b, c), 0)),
            pl.BlockSpec((chunk, qk_w), lambda b, c: (row(b, c), 1)),
            pl.BlockSpec((chunk, v_w), lambda b, c: (row(b, c), 1)),
            pl.BlockSpec((chunk, v_w), lambda b, c: (row(b, c), 2)),
            full((RET_HEADS, chunk, chunk)), full((RET_HEADS, chunk, 1)), full((RET_HEADS, chunk, 1)),
            full((v_w, d)),
            pl.BlockSpec((chunk, d_inner), lambda b, c: (row(b, c), z_blk)),
            pl.BlockSpec((chunk, conv_dim), lambda b, c: (row(b, c), xbc_blk)),
            pl.BlockSpec((chunk, LANES), lambda b, c: (row(b, c), 0)),
            full((1, LANES)), full((1, LANES)),
            full((1, d_inner)), full((1, d_inner)), full((ssm_chunk, ssm_chunk)), full((LANES, d_inner)),
            full((d_inner, d)),
            pl.BlockSpec((chunk, d), lambda b, c: (row(b, c), ga_blk)),
            pl.BlockSpec((chunk, d), lambda b, c: (row(b, c), ga_blk + 1)),
            pl.BlockSpec((chunk, d), lambda b, c: (row0 // chunk + row(b, c), 0)),
            modspec(2), modspec(4), modspec(3),
            full((1, d)), full((d, d)), full((2 * N_EXPERTS, d)), full((N_EXPERTS, 1)), full((chunk, chunk)),
        ],
        out_specs=(pl.BlockSpec((chunk, d), lambda b, c: (row(b, c), 0)),
                   pl.BlockSpec((chunk * SLAB_ROWS, LANES), lambda b, c: (row(b, c), 0)),
                   pl.BlockSpec((1, SUBLANES, chunk), lambda b, c: (row(b, c), 0, 0)),
                   pl.BlockSpec((1, SUBLANES, chunk), lambda b, c: (row(b, c), 0, 0)),
                   pl.BlockSpec((chunk, LANES), lambda b, c: (row(b, c), 0)),
                   full((N_EXPERTS, LANES))),
        scratch_shapes=[pltpu.VMEM((RET_HEADS, RET_QK_DIM, RET_V_DIM), F32),
                        pltpu.VMEM((SSM_GROUPS, SSM_STATE, gw), F32),
                        pltpu.VMEM((chunk, d_inner), BF16),
                        pltpu.VMEM((chunk, d), F32), pltpu.VMEM((chunk, d), F32),
                        pltpu.VMEM((N_EXPERTS, LANES), F32)],
        compiler_params=_params(("arbitrary", "arbitrary")),
        name="mixers",
    )(proj, proj, proj, proj, din, dq, dk, w_ret,
      proj, proj, dt_raw, pad_h(dt_bias), pad_h(a_log),
      jnp.repeat(d_skip.astype(F32), SSM_HEAD_DIM).reshape(1, d_inner), ssm_norm.reshape(1, d_inner),
      tril, expand, w_ssm,
      proj, proj, x2, mod3, mod3, mod3, norm_ffn, w_out, w_router_t, b_router.reshape(N_EXPERTS, 1), tri)


def _merge_body(ya_ref, yb_ref, ga_ref, gb_ref, x_ref, gm_ref, scf_ref, shf_ref, nw_ref, wo_ref,
                wr_ref, br_ref, tri_ref,
                x1_ref, h2_ref, idx_ref, rank_ref, prow_ref, cnt_ref, cnt_s, tm):
    merged = (ga_ref[...].astype(F32) * ya_ref[...].astype(F32)
              + gb_ref[...].astype(F32) * yb_ref[...].astype(F32))
    mo = jnp.dot(merged.astype(BF16), wo_ref[...], preferred_element_type=F32)
    x1 = x_ref[...] + gm_ref[0] * mo
    x1_ref[...] = x1
    ms = jnp.mean(x1 * x1, axis=-1, keepdims=True)
    h2 = x1 * lax.rsqrt(ms + EPS) * nw_ref[...] * (1.0 + scf_ref[0]) + shf_ref[0]
    _store_slabs(h2_ref, h2, tm)

    h_hi = h2.astype(BF16)
    h_lo = (h2 - h_hi.astype(F32)).astype(BF16)
    lg2 = _nt_dot(wr_ref[...], h_hi)
    lg = lg2[:N_EXPERTS] + lg2[N_EXPERTS:] + _nt_dot(wr_ref[:N_EXPERTS, :], h_lo) + br_ref[...]
    sub = lax.broadcasted_iota(jnp.int32, lg.shape, 0)
    work = lg
    vals, idxs, sels = [], [], []
    for _ in range(TOP_K):
        m = jnp.max(work, axis=0, keepdims=True)
        ik = jnp.min(jnp.where(work == m, sub, N_EXPERTS), axis=0, keepdims=True)
        sel = sub == ik
        vals.append(m)
        idxs.append(ik)
        sels.append(sel)
        work = jnp.where(sel, -jnp.inf, work)
    exps = [jnp.exp(v - vals[0]) for v in vals]
    denom = exps[0]
    for e in exps[1:]:
        denom = denom + e
    probs = [e / denom for e in exps]

    base = cnt_s[:, 0:1]
    ranks = []
    for k in range(TOP_K):
        mk = jnp.where(sels[k], 1.0, 0.0)
        pre = jnp.dot(mk.astype(BF16), tri_ref[...], preferred_element_type=F32)
        ranks.append(jnp.sum(jnp.where(sels[k], pre + base, 0.0), axis=0, keepdims=True))
        base = base + jnp.sum(mk, axis=1, keepdims=True)
    cnt_s[...] = jnp.broadcast_to(base, cnt_s.shape)
    cnt_ref[...] = cnt_s[...].astype(jnp.int32)

    zi = jnp.zeros((SUBLANES - TOP_K, tm), jnp.int32)
    idx_ref[0] = jnp.concatenate(idxs + [zi], axis=0)
    rank_ref[0] = jnp.concatenate([r.astype(jnp.int32) for r in ranks] + [zi], axis=0)
    pt = jnp.concatenate(probs + [jnp.zeros((LANES - TOP_K, tm), F32)], axis=0)
    prow_ref[...] = pt.T


def _sc_mesh():
    return plsc.VectorSubcoreMesh(core_axis_name="c", subcore_axis_name="s")


def _sc_worker():
    return lax.axis_index("s") * SC_CORES + lax.axis_index("c")


def _sc_scatter_rows(rows, dest, n_out):
    t = rows.shape[0]
    n_k = dest.shape[0]
    g = SC_GROUP
    n_w = SC_CORES * SC_SUBCORES
    assert t % (n_w * g) == 0
    cpw = t // (n_w * g)
    dest_w = dest.reshape(n_k, n_w, cpw, g).transpose(1, 0, 2, 3)

    @functools.partial(
        pl.kernel, mesh=_sc_mesh(),
        out_type=jax.ShapeDtypeStruct((n_out,) + rows.shape[1:], rows.dtype),
        scratch_types=[pltpu.VMEM((n_k, cpw, g), jnp.int32),
                       pltpu.VMEM((g,) + rows.shape[1:], rows.dtype),
                       pltpu.SemaphoreType.DMA],
    )
    def scatter(rows_hbm, dest_hbm, out_hbm, idx_v, rows_v, sem):
        wid = _sc_worker()
        pltpu.sync_copy(dest_hbm.at[wid], idx_v)

        @pl.loop(0, cpw)
        def _(cc):
            r0 = pl.multiple_of((wid * cpw + cc) * g, g)
            pltpu.sync_copy(rows_hbm.at[pl.ds(r0, g)], rows_v)
            copies = [pltpu.async_copy(rows_v, out_hbm.at[idx_v.at[k, cc]], sem) for k in range(n_k)]
            for cp in copies:
                cp.wait()

    return scatter(rows, dest_w)


def _sc_gather_rows(table, idx):
    m = idx.shape[0]
    g = SC_GROUP
    n_w = SC_CORES * SC_SUBCORES
    assert m % (n_w * g) == 0
    per_w = m // n_w

    @functools.partial(
        pl.kernel, mesh=_sc_mesh(),
        out_type=jax.ShapeDtypeStruct((m,) + table.shape[1:], table.dtype),
        scratch_types=[pltpu.VMEM((per_w,), jnp.int32),
                       pltpu.VMEM((g,) + table.shape[1:], table.dtype),
                       pltpu.SemaphoreType.DMA],
    )
    def gather(table_hbm, idx_hbm, out_hbm, idx_v, rows_v, sem):
        base = _sc_worker() * per_w
        pltpu.sync_copy(idx_hbm.at[pl.ds(base, per_w)], idx_v)

        @pl.loop(0, per_w // g)
        def _(cc):
            off = pl.multiple_of(cc * g, g)
            pltpu.async_copy(table_hbm.at[idx_v.at[pl.ds(off, g)]], rows_v, sem).wait()
            pltpu.sync_copy(rows_v, out_hbm.at[pl.ds(base + off, g)])

    return gather(table, idx)


def _ffn_kernel(be_ref, br_ref, bv_ref, first_ref, slot_ref, next_ref, rows_ref,
                x_ref, wgu_hbm, bgu_ref, wd_hbm, bd_ref, o_ref, wgu_f, wd_f, wgu_s, wd_s, sem, *, bm, d_ff):
    i = pl.program_id(0)

    def fetch(e, slot):
        return (pltpu.make_async_copy(wgu_hbm.at[e], wgu_f.at[slot], sem.at[0, slot]),
                pltpu.make_async_copy(wd_hbm.at[e], wd_f.at[slot], sem.at[1, slot]))

    @pl.when(i == 0)
    def _():
        for cp in fetch(be_ref[0], slot_ref[0]):
            cp.start()

    @pl.when(first_ref[i] == 1)
    def _():
        slot = slot_ref[i]
        for cp in fetch(be_ref[i], slot):
            cp.wait()
        wgu_s[...] = wgu_f[slot].astype(BF16)
        wd_s[...] = wd_f[slot].astype(BF16)

        @pl.when(next_ref[i] >= 0)
        def _():
            for cp in fetch(next_ref[i], 1 - slot):
                cp.start()

    def expert_mlp(n, keep):
        x = jnp.concatenate(_load_slabs(x_ref, n, keep=keep), axis=-1).astype(BF16)
        gu = jnp.dot(x, wgu_s[...], preferred_element_type=F32) + bgu_ref[0]
        gate = jnp.minimum(gu[:, :d_ff], SWIGLU_LIMIT)
        up = jnp.clip(gu[:, d_ff:], -SWIGLU_LIMIT, SWIGLU_LIMIT)
        act = gate * jax.nn.sigmoid(SWIGLU_ALPHA * gate) * (up + 1.0)
        y = jnp.dot(act.astype(BF16), wd_s[...], preferred_element_type=F32) + bd_ref[0]
        _store_slabs(o_ref, y, n)
        if n < bm:
            o_ref[pl.ds(n * SLAB_ROWS, (bm - n) * SLAB_ROWS), :] = jnp.zeros(
                ((bm - n) * SLAB_ROWS, LANES), jnp.uint32)

    @pl.when((bv_ref[i] == 1) & (rows_ref[i] == bm))
    def _():
        expert_mlp(bm, None)

    for q in range(1, bm // FFN_TAIL + 1):
        n = q * FFN_TAIL
        lo = (q - 1) * FFN_TAIL if q > 1 else -1

        @pl.when((bv_ref[i] == 1) & (rows_ref[i] < bm) & (rows_ref[i] > lo) & (rows_ref[i] <= n))
        def _(n=n):
            expert_mlp(n, lax.broadcasted_iota(jnp.int32, (n, LANES), 0) < rows_ref[i])


def _ffn_call(blk_e, blk_row, blk_valid, blk_rows, xs, w_gu, b_gu, w_d, b_d, bm):
    n_e, d, f2 = w_gu.shape
    d_ff = f2 // 2
    nb = blk_e.shape[0]
    first = jnp.concatenate([jnp.ones((1,), jnp.int32), (blk_e[1:] != blk_e[:-1]).astype(jnp.int32)])
    slot = (jnp.cumsum(first) - 1) % 2
    later = jnp.where(blk_e[None, :] > blk_e[:, None], blk_e[None, :], n_e)
    nxt = jnp.min(later, axis=1)
    nxt = jnp.where(nxt == n_e, -1, nxt)
    kern = functools.partial(_ffn_kernel, bm=bm, d_ff=d_ff)
    imap = lambda f: (lambda i, be, br, bv, fi, sl, nx, rw: f(i, be, br))
    gs = pltpu.PrefetchScalarGridSpec(
        num_scalar_prefetch=7,
        grid=(nb,),
        in_specs=[pl.BlockSpec((bm * SLAB_ROWS, LANES), imap(lambda i, be, br: (br[i], 0))),
                  pl.BlockSpec(memory_space=pl.ANY),
                  pl.BlockSpec((1, 1, f2), imap(lambda i, be, br: (be[i], 0, 0))),
                  pl.BlockSpec(memory_space=pl.ANY),
                  pl.BlockSpec((1, 1, d), imap(lambda i, be, br: (be[i], 0, 0)))],
        out_specs=pl.BlockSpec((bm * SLAB_ROWS, LANES), imap(lambda i, be, br: (br[i], 0))),
        scratch_shapes=[pltpu.VMEM((2, d, f2), F32), pltpu.VMEM((2, d_ff, d), F32),
                        pltpu.VMEM((d, f2), BF16), pltpu.VMEM((d_ff, d), BF16),
                        pltpu.SemaphoreType.DMA((2, 2))],
    )
    return pl.pallas_call(
        kern,
        out_shape=jax.ShapeDtypeStruct((nb * bm * SLAB_ROWS, LANES), jnp.uint32),
        grid_spec=gs,
        compiler_params=_params(("arbitrary",)),
        name="ffn",
    )(blk_e, blk_row, blk_valid, first, slot.astype(jnp.int32), nxt.astype(jnp.int32), blk_rows,
      xs, w_gu, b_gu.reshape(n_e, 1, f2), w_d, b_d.reshape(n_e, 1, d))


def _combine_kernel(y0_ref, y1_ref, y2_ref, y3_ref, prow_ref, x1_ref, gf_ref, nw_ref, *rest, tm):
    o_ref = rest[-1]
    p = prow_ref[...]
    pieces = [_load_slabs(y_ref, tm) for y_ref in (y0_ref, y1_ref, y2_ref, y3_ref)]
    for s in range(len(pieces[0])):
        moe = None
        for k in range(TOP_K):
            piece = pieces[k][s] * p[:, k:k + 1]
            moe = piece if moe is None else moe + piece
        sl = slice(s * LANES, (s + 1) * LANES)
        o_ref[:, sl] = x1_ref[:, sl] + gf_ref[0][:, sl] * moe
    xo = o_ref[...]
    o_ref[...] = xo * lax.rsqrt(jnp.mean(xo * xo, axis=-1, keepdims=True) + EPS) * nw_ref[...]


def _combine_call(ytok, prow, x1, mod3, norm_final, seq, tm, row0, t_total, out_prev):
    t, d = x1.shape
    nt = t // tm
    tiles_per_seq = seq // tm
    off = row0 // tm
    kern = functools.partial(_combine_kernel, tm=tm)
    yspec = lambda k: pl.BlockSpec((tm * SLAB_ROWS, LANES), lambda i: (k * nt + i, 0))
    in_specs = [yspec(0), yspec(1), yspec(2), yspec(3),
                pl.BlockSpec((tm, LANES), lambda i: (i, 0)),
                pl.BlockSpec((tm, d), lambda i: (i, 0)),
                pl.BlockSpec((1, 1, d), lambda i: (((i + off) // tiles_per_seq) * N_MOD + 5, 0, 0)),
                pl.BlockSpec((1, d), lambda i: (0, 0))]
    args = [ytok, ytok, ytok, ytok, prow, x1, mod3, norm_final]
    aliases = {}
    if out_prev is not None:
        in_specs.append(pl.BlockSpec(memory_space=pl.ANY))
        aliases = {len(args): 0}
        args.append(out_prev)
    return pl.pallas_call(
        kern,
        out_shape=jax.ShapeDtypeStruct((t_total, d), F32),
        grid=(nt,),
        in_specs=in_specs,
        out_specs=pl.BlockSpec((tm, d), lambda i: (i + off, 0)),
        input_output_aliases=aliases,
        compiler_params=_params(("arbitrary",)),
        name="combine",
    )(*args)


def _plan(seq):
    def fit(pref):
        tm = min(pref, seq)
        assert seq % tm == 0
        return tm
    return dict(tm_in=fit(1024), tm_moe=fit(1024), ret_chunk=fit(RET_CHUNK), ssm_chunk=fit(SSM_CHUNK))


def _layer(x2, mod3, bsz, seq, norm_mix, norm_ffn, w_in, conv_w, conv_b, dt_bias, a_log, d_skip, ssm_norm,
           w_ret_out, w_ssm_out, w_out, w_router, b_router, w_gate_up, b_gate_up, w_down, b_down,
           norm_final):
    t, d = x2.shape
    plan = _plan(seq)
    qk_w = RET_HEADS * RET_QK_DIM
    v_w = RET_HEADS * RET_V_DIM
    d_inner = w_ssm_out.shape[0]
    conv_dim = conv_w.shape[1]
    n_heads = d_inner // SSM_HEAD_DIM
    dt_off = 2 * qk_w + 2 * v_w + d_inner + conv_dim

    w_all = w_in.astype(BF16)
    w_gate = w_in[:, dt_off + n_heads:].astype(BF16)
    w_dt = jnp.pad(w_in[:, dt_off:dt_off + n_heads], ((0, 0), (0, LANES - n_heads)))
    w_dt_hi = w_dt.astype(BF16)
    w_dt = jnp.concatenate([w_dt_hi, (w_dt - w_dt_hi.astype(F32)).astype(BF16)], axis=1)
    half = RET_QK_DIM // 2
    inv_freq = ROPE_BASE ** (-jnp.arange(half, dtype=F32) / half)
    ang = jnp.arange(seq, dtype=F32)[:, None] * inv_freq[None, :]
    cos, sin = jnp.cos(ang), jnp.sin(ang)

    w_ret_b, w_ssm_b, w_out_b = w_ret_out.astype(BF16), w_ssm_out.astype(BF16), w_out.astype(BF16)
    w_r_hi = w_router.T.astype(BF16)
    w_router_t = jnp.concatenate([w_r_hi, (w_router.T - w_r_hi.astype(F32)).astype(BF16)], axis=0)
    bm = FFN_BLOCK
    slab = (SLAB_ROWS, LANES)

    def mixer(row0, tg, bg):
        proj, dt_raw = _inproj_call(x2, row0, tg, norm_mix.reshape(1, d), mod3, cos, sin, w_all, w_gate, w_dt,
                                    conv_w, conv_b, 2 * qk_w + 2 * v_w + d_inner, seq, plan["tm_in"], 2 * qk_w)
        x1, h2, idx, rank, prow, cnt = _mixers_call(
            proj, dt_raw, x2, row0, mod3, w_ret_b, dt_bias, a_log, d_skip, ssm_norm, w_ssm_b,
            norm_ffn.reshape(1, d), w_out_b, w_router_t, b_router, bg, seq, plan["ret_chunk"], plan["ssm_chunk"])
        counts = cnt[:, 0]
        padded = ((counts + bm - 1) // bm) * bm
        pad_end = jnp.cumsum(padded)
        start_pad = pad_end - padded
        n_blocks = -(-(tg * TOP_K) // bm) + N_EXPERTS
        e_ids = jnp.arange(N_EXPERTS, dtype=jnp.int32)[:, None, None, None]
        dest = rank + jnp.sum(jnp.where(idx[None] == e_ids, start_pad[:, None, None, None], 0), axis=0)
        dest = dest[:, :TOP_K, :].transpose(1, 0, 2).reshape(TOP_K, tg).astype(jnp.int32)
        n_real = pad_end[-1] // bm
        blk_valid = (jnp.arange(n_blocks) < n_real).astype(jnp.int32)
        blk_row = jnp.minimum(jnp.arange(n_blocks), n_real - 1).astype(jnp.int32)
        blk_e = jnp.minimum(jnp.sum(pad_end[None, :] <= (blk_row * bm)[:, None], axis=1),
                            N_EXPERTS - 1).astype(jnp.int32)
        sel_e = blk_e[:, None] == jnp.arange(N_EXPERTS, dtype=jnp.int32)[None, :]
        row_end = jnp.sum(jnp.where(sel_e, (start_pad + counts)[None, :], 0), axis=1)
        blk_rows = jnp.clip(row_end - blk_row * bm, 0, bm).astype(jnp.int32)
        xs = _sc_scatter_rows(h2.reshape((tg,) + slab), dest, n_blocks * bm)
        return dict(x1=x1, prow=prow, dest=dest, blocks=(blk_e, blk_row, blk_valid, blk_rows), xs=xs,
                    row0=row0)

    def experts(m):
        n_rows = m["xs"].shape[0]
        ys = _ffn_call(*m["blocks"], m["xs"].reshape(n_rows * SLAB_ROWS, LANES),
                       w_gate_up, b_gate_up, w_down, b_down, bm)
        return _sc_gather_rows(ys.reshape((n_rows,) + slab), m["dest"].reshape(-1))

    n_groups = N_GROUPS if bsz % N_GROUPS == 0 else 1
    bg = bsz // n_groups
    tg = bg * seq
    groups = [mixer(g * tg, tg, bg) for g in range(n_groups)]
    ytoks = [experts(m) for m in groups]
    out = None
    for y, m in zip(ytoks, groups):
        out = _combine_call(y.reshape(TOP_K * tg * SLAB_ROWS, LANES), m["prow"], m["x1"], mod3,
                            norm_final.reshape(1, d), seq, plan["tm_moe"], m["row0"], t, out)
    return out


def kernel(x, c, w_ada, b_ada, norm_mix, norm_ffn, w_in, conv_w, conv_b, dt_bias, a_log, d_skip, ssm_norm,
           w_ret_out, w_ssm_out, w_out, w_router, b_router, w_gate_up, b_gate_up, w_down, b_down, norm_final):
    bsz, seq, d = x.shape
    depth = w_ada.shape[0]
    assert depth == 1, "the final norm is fused into the single layer's last kernel"
    x2 = x.reshape(bsz * seq, d)
    l = 0
    mod = _mod_call(c, w_ada[l], b_ada[l])
    mod3 = mod.reshape(bsz * N_MOD, 1, d)
    out = _layer(x2, mod3, bsz, seq, norm_mix[l], norm_ffn[l], w_in[l], conv_w[l], conv_b[l], dt_bias[l],
                 a_log[l], d_skip[l], ssm_norm[l], w_ret_out[l], w_ssm_out[l], w_out[l], w_router[l],
                 b_router[l], w_gate_up[l], b_gate_up[l], w_down[l], b_down[l], norm_final)
    return out.reshape(bsz, seq, d)
```

```python
import functools
import math

import numpy as np
import jax
import jax.numpy as jnp
from jax import lax
from jax.experimental import pallas as pl
from jax.experimental.pallas import tpu as pltpu
from jax.experimental.pallas import tpu_sc as plsc

F32 = jnp.float32
BF16 = jnp.bfloat16
HIGHEST = lax.Precision.HIGHEST

EPS = 1e-6
N_MOD = 6
RET_HEADS = 4
RET_QK_DIM = 256
RET_V_DIM = 512
ROPE_BASE = 10000.0
SSM_HEAD_DIM = 64
SSM_GROUPS = 8
SSM_STATE = 128
SSM_CONV = 4
N_EXPERTS = 32
TOP_K = 4
SWIGLU_LIMIT = 7.0
SWIGLU_ALPHA = 1.702

LANES = 128
SUBLANES = 8
VMEM_LIMIT = 56 * 1024 * 1024

RET_CHUNK = 256
SSM_CHUNK = 128
FFN_BLOCK = 1024
FFN_TAIL = 128
SC_CORES = 2
SC_SUBCORES = 16
SC_GROUP = 64
N_GROUPS = 2


def _params(sem, vmem=VMEM_LIMIT):
    return pltpu.CompilerParams(dimension_semantics=sem, vmem_limit_bytes=vmem)


def _nt_dot(a, b, **kw):
    return lax.dot_general(a, b, (((1,), (1,)), ((), ())), preferred_element_type=F32, **kw)


def _tn_dot(a, b, **kw):
    return lax.dot_general(a, b, (((0,), (0,)), ((), ())), preferred_element_type=F32, **kw)


def _silu(v):
    return v * jax.nn.sigmoid(v)


SLAB_ROWS = 4
HIGH_HALF = 0xFFFF0000
LOG2_E = math.log2(math.e)


def _store_slabs(ref, vals, n):
    for s in range(SLAB_ROWS):
        lo = vals[:, s * LANES:(s + 1) * LANES].astype(BF16).astype(F32)
        hi = vals[:, (s + SLAB_ROWS) * LANES:(s + SLAB_ROWS + 1) * LANES].astype(BF16).astype(F32)
        word = (pltpu.bitcast(lo, jnp.uint32) >> 16) | (pltpu.bitcast(hi, jnp.uint32) & jnp.uint32(HIGH_HALF))
        ref[pl.ds(s, n, stride=SLAB_ROWS), :] = word


def _load_slabs(ref, n, base=0, keep=None):
    lo, hi = [], []
    for s in range(SLAB_ROWS):
        word = ref[pl.ds(base + s, n, stride=SLAB_ROWS), :]
        if keep is not None:
            word = jnp.where(keep, word, jnp.zeros_like(word))
        lo.append(pltpu.bitcast(word << 16, F32))
        hi.append(pltpu.bitcast(word & jnp.uint32(HIGH_HALF), F32))
    return lo + hi


def _mod_kernel(c_ref, w_ref, b_ref, o_ref):
    cond = _silu(c_ref[...])
    o_ref[...] = jnp.dot(cond, w_ref[...], preferred_element_type=F32, precision=HIGHEST) + b_ref[...]


def _mod_call(c, w_ada, b_ada):
    bsz, d = c.shape
    n = w_ada.shape[1]
    return pl.pallas_call(
        _mod_kernel,
        out_shape=jax.ShapeDtypeStruct((bsz, n), F32),
        grid=(n // d,),
        in_specs=[pl.BlockSpec((bsz, d), lambda j: (0, 0)),
                  pl.BlockSpec((d, d), lambda j: (0, j)),
                  pl.BlockSpec((1, d), lambda j: (0, j))],
        out_specs=pl.BlockSpec((bsz, d), lambda j: (0, j)),
        compiler_params=_params(("arbitrary",)),
        name="mod",
    )(c, w_ada, b_ada.reshape(1, n))


def _inproj_kernel(x_ref, nw_ref, sc_ref, sh_ref, cos_ref, sin_ref, w_ref, wg_ref, wdt_ref, cw_ref, cb_ref,
                   o_ref, dt_ref, h_s, work, carry, *, conv_j0, conv_nj, silu_j, sigm_j, tiles_per_seq,
                   tm, tn, sub):
    i = pl.program_id(0)
    j = pl.program_id(1)
    n_dt = dt_ref.shape[1]
    rows = min(tm, 256)

    @pl.when(j == 0)
    def _():
        xf = x_ref[...]
        ms = jnp.mean(xf * xf, axis=-1, keepdims=True)
        y = xf * lax.rsqrt(ms + EPS) * nw_ref[...]
        hm = y * (1.0 + sc_ref[0]) + sh_ref[0]
        hb = hm.astype(BF16)
        h_s[...] = hb
        h_lo = (hm - hb.astype(F32)).astype(BF16)
        d_hi = jnp.dot(hb, wdt_ref[...], preferred_element_type=F32)
        d_lo = jnp.dot(h_lo, wdt_ref[:, :n_dt], preferred_element_type=F32)
        dt_ref[...] = d_hi[:, :n_dt] + d_hi[:, n_dt:] + d_lo
        half = RET_QK_DIM // 2
        for p in range(tn // sub):
            for r in range(tm // rows):
                rs = slice(r * rows, (r + 1) * rows)
                acc = jnp.dot(h_s[rs, :], w_ref[:, p * sub:(p + 1) * sub], preferred_element_type=F32)
                cos = cos_ref[rs, :]
                sin = sin_ref[rs, :]
                for cc in range(sub // RET_QK_DIM):
                    c = p * (sub // RET_QK_DIM) + cc
                    a = acc[:, cc * RET_QK_DIM: cc * RET_QK_DIM + half]
                    b = acc[:, cc * RET_QK_DIM + half: (cc + 1) * RET_QK_DIM]
                    scale = 1.0 if c < RET_HEADS else RET_QK_DIM ** -0.5
                    o_ref[rs, c * RET_QK_DIM: c * RET_QK_DIM + half] = ((a * cos - b * sin) * scale).astype(BF16)
                    o_ref[rs, c * RET_QK_DIM + half: (c + 1) * RET_QK_DIM] = (
                        (a * sin + b * cos) * scale).astype(BF16)

    is_conv = (j >= conv_j0) & (j < conv_j0 + conv_nj)

    @pl.when(is_conv)
    def _():
        cj = j - conv_j0
        pad = SUBLANES

        @pl.when(i % tiles_per_seq == 0)
        def _():
            carry[cj] = jnp.zeros(carry.shape[1:], F32)

        for p in range(tn // sub):
            for r in range(tm // rows):
                r0 = r * rows
                acc = jnp.dot(h_s[r0:r0 + rows, :], w_ref[:, p * sub:(p + 1) * sub], preferred_element_type=F32)
                for cc in range(sub // LANES):
                    c = p * (sub // LANES) + cc
                    cols = slice(c * LANES, (c + 1) * LANES)
                    if r == 0:
                        work[c, 0:pad, :] = carry[cj, c]
                    lo = pad + r0
                    work[c, lo:lo + rows, :] = acc[:, cc * LANES:(cc + 1) * LANES]
                    conv = cb_ref[:, cols] + cw_ref[SSM_CONV - 1:SSM_CONV, cols] * work[c, lo:lo + rows, :]
                    for k in range(SSM_CONV - 1):
                        shift = SSM_CONV - 1 - k
                        conv = conv + cw_ref[k:k + 1, cols] * work[c, lo - shift:lo - shift + rows, :]
                    if r0 + rows == tm:
                        carry[cj, c] = work[c, tm:tm + pad, :]
                    o_ref[r0:r0 + rows, cols] = _silu(conv).astype(BF16)

    def plain(act, weights):
        for p in range(tn // sub):
            for r in range(tm // rows):
                acc = jnp.dot(h_s[r * rows:(r + 1) * rows, :], weights[:, p * sub:(p + 1) * sub],
                              preferred_element_type=F32)
                o_ref[r * rows:(r + 1) * rows, p * sub:(p + 1) * sub] = act(acc).astype(BF16)

    is_silu = (j >= silu_j[0]) & (j < silu_j[1])
    is_sigm = (j >= sigm_j[0]) & (j < sigm_j[1])
    pl.when(is_silu)(lambda: plain(_silu, w_ref))
    pl.when(is_sigm)(lambda: plain(jax.nn.sigmoid, wg_ref))
    pl.when((j != 0) & jnp.logical_not(is_conv | is_silu | is_sigm))(lambda: plain(lambda v: v, w_ref))


def _inproj_call(x2, row0, t, norm_w, mod3, cos, sin, w_all, w_gate, w_dt, conv_w, conv_b, conv_off, seq, tm, tn):
    d = x2.shape[1]
    conv_dim = conv_w.shape[1]
    n_lead = conv_off + conv_dim
    n = n_lead + w_gate.shape[1]
    assert w_gate.shape[1] == tn and n_lead % tn == 0
    tiles_per_seq = seq // tm
    off = row0 // tm
    assert tn == 2 * RET_HEADS * RET_QK_DIM, "rotary epilogue expects q and k in the first column tile"
    assert conv_off % tn == 0 and conv_dim % tn == 0
    conv_j0, conv_nj = conv_off // tn, conv_dim // tn
    sub = 512
    g_off = 2 * RET_HEADS * RET_QK_DIM + RET_HEADS * RET_V_DIM
    assert g_off % tn == 0 and (conv_off - g_off) % tn == 0 and (n - conv_off - conv_dim) % tn == 0
    silu_j = (g_off // tn, conv_off // tn)
    sigm_j = ((conv_off + conv_dim) // tn, n // tn)
    kern = functools.partial(_inproj_kernel, conv_j0=conv_j0, conv_nj=conv_nj, silu_j=silu_j, sigm_j=sigm_j,
                             tiles_per_seq=tiles_per_seq, tm=tm, tn=tn, sub=sub)
    conv_idx = lambda i, j: (0, jnp.clip(j - conv_j0, 0, conv_nj - 1))
    return pl.pallas_call(
        kern,
        out_shape=(jax.ShapeDtypeStruct((t, n), BF16), jax.ShapeDtypeStruct((t, LANES), F32)),
        grid=(t // tm, n // tn),
        in_specs=[
            pl.BlockSpec((tm, d), lambda i, j: (i + off, 0)),
            pl.BlockSpec((1, d), lambda i, j: (0, 0)),
            pl.BlockSpec((1, 1, d), lambda i, j: (((i + off) // tiles_per_seq) * N_MOD + 1, 0, 0)),
            pl.BlockSpec((1, 1, d), lambda i, j: (((i + off) // tiles_per_seq) * N_MOD + 0, 0, 0)),
            pl.BlockSpec((tm, LANES), lambda i, j: (i % tiles_per_seq, 0)),
            pl.BlockSpec((tm, LANES), lambda i, j: (i % tiles_per_seq, 0)),
            pl.BlockSpec((d, tn), lambda i, j: (0, jnp.minimum(j, n_lead // tn - 1))),
            pl.BlockSpec((d, tn), lambda i, j: (0, 0)),
            pl.BlockSpec((d, 2 * LANES), lambda i, j: (0, 0)),
            pl.BlockSpec((SSM_CONV, tn), conv_idx),
            pl.BlockSpec((1, tn), conv_idx),
        ],
        out_specs=(pl.BlockSpec((tm, tn), lambda i, j: (i, j)),
                   pl.BlockSpec((tm, LANES), lambda i, j: (i, 0))),
        scratch_shapes=[pltpu.VMEM((tm, d), BF16),
                        pltpu.VMEM((tn // LANES, tm + SUBLANES, LANES), F32),
                        pltpu.VMEM((conv_nj, tn // LANES, SUBLANES, LANES), F32)],
        compiler_params=_params(("arbitrary", "arbitrary")),
        name="inproj",
    )(x2, norm_w, mod3, mod3, cos, sin, w_all, w_gate, w_dt, conv_w, conv_b.reshape(1, conv_dim))


def _retention_body(q_ref, k_ref, v_ref, g_ref, din_ref, dq_ref, dk_ref, w_ref, o_ref, state, decay_c):
    heads = range(RET_HEADS)
    q = [q_ref[:, h * RET_QK_DIM:(h + 1) * RET_QK_DIM] for h in heads]
    k = [k_ref[:, h * RET_QK_DIM:(h + 1) * RET_QK_DIM] for h in heads]
    v = [v_ref[:, h * RET_V_DIM:(h + 1) * RET_V_DIM] for h in heads]
    gated = []
    for h in heads:
        scores = (_nt_dot(q[h], k[h]) * din_ref[h]).astype(BF16)
        st = state[h]
        cross = jnp.dot(q[h], st.astype(BF16), preferred_element_type=F32) * dq_ref[h]
        inner = jnp.dot(scores, v[h], preferred_element_type=F32)
        kd = (k[h].astype(F32) * dk_ref[h]).astype(BF16)
        state[h] = st * decay_c[h] + _tn_dot(kd, v[h])
        ret = inner + cross
        ret = ret * lax.rsqrt(jnp.mean(ret * ret, axis=-1, keepdims=True) + EPS)
        ret = ret * g_ref[:, h * RET_V_DIM:(h + 1) * RET_V_DIM].astype(F32)
        gated.append(ret.astype(BF16))
    o_ref[...] = jnp.dot(jnp.concatenate(gated, axis=-1), w_ref[...],
                         preferred_element_type=F32).astype(o_ref.dtype)


def _retention_tables(chunk):
    lg = np.log(1.0 - 2.0 ** (-5.0 - np.arange(RET_HEADS, dtype=np.float64)))
    idx = np.arange(chunk, dtype=np.float64)
    rel = idx[:, None] - idx[None, :]
    causal = rel >= 0
    din = np.where(causal[None], np.exp(np.where(causal, rel, 0.0)[None] * lg[:, None, None]), 0.0)
    dq = np.exp((idx + 1.0)[None, :, None] * lg[:, None, None])
    dk = np.exp((chunk - 1.0 - idx)[None, :, None] * lg[:, None, None])
    dc = tuple(float(v) for v in np.exp(chunk * lg))
    return (jnp.asarray(din, F32), jnp.asarray(dq, F32), jnp.asarray(dk, F32), dc)


def _ssd_decays(dt_ref, dtb_ref, alog_ref, tril_ref, exp_ref, n_sub, chunk):
    dt = jax.nn.softplus(dt_ref[...] + dtb_ref[...])
    adt = dt * (-LOG2_E * jnp.exp(alog_ref[...]))
    p1 = adt.astype(BF16)
    r1 = adt - p1.astype(F32)
    p2 = r1.astype(BF16)
    p3 = (r1 - p2.astype(F32)).astype(BF16)
    pieces = jnp.concatenate([p1, p2, p3], axis=-1)
    acs = []
    for s in range(n_sub):
        c3 = jnp.dot(tril_ref[...], pieces[s * chunk:(s + 1) * chunk, :], preferred_element_type=F32)
        acs.append(c3[:, :LANES] + c3[:, LANES:2 * LANES] + c3[:, 2 * LANES:])
    dt_x = jnp.dot(dt.astype(BF16), exp_ref[...], preferred_element_type=F32)
    return acs, dt_x


def _ssd_body(z_ref, xbc_ref, acs, dt_x, dsk_ref, nw_ref, yn_s, state, chunk, d_inner):
    heads_per_group = d_inner // SSM_HEAD_DIM // SSM_GROUPS
    gw = heads_per_group * SSM_HEAD_DIM
    assert SSM_HEAD_DIM * 2 == LANES and gw == 2 * LANES

    acs_t = acs.T
    li = lax.broadcasted_iota(jnp.int32, (chunk, chunk), 0)
    si = lax.broadcasted_iota(jnp.int32, (chunk, chunk), 1)
    causal = li >= si
    low_half = si < SSM_HEAD_DIM
    lane_g = lax.broadcasted_iota(jnp.int32, (chunk, gw), 1)

    b_off = d_inner
    c_off = d_inner + SSM_GROUPS * SSM_STATE
    for g in range(SSM_GROUPS):
        bm = xbc_ref[:, b_off + g * SSM_STATE: b_off + (g + 1) * SSM_STATE]
        cm = xbc_ref[:, c_off + g * SSM_STATE: c_off + (g + 1) * SSM_STATE]
        xs_g = xbc_ref[:, g * gw:(g + 1) * gw].astype(F32)
        xdt_g = xs_g * dt_x[:, g * gw:(g + 1) * gw]
        xdt_b = xdt_g.astype(BF16)
        cb = _nt_dot(cm, bm)
        cols, ms, xm = [], [], []
        for jh in range(heads_per_group):
            h = g * heads_per_group + jh
            col = jnp.broadcast_to(acs[:, h:h + 1], (chunk, chunk))
            seg = jnp.exp2(jnp.where(causal, col - acs_t[h:h + 1, :], -jnp.inf))
            cols.append(col)
            ms.append((cb * seg).astype(BF16))
            in_head = (lane_g >= jh * SSM_HEAD_DIM) & (lane_g < (jh + 1) * SSM_HEAD_DIM)
            xm.append(jnp.where(in_head, xdt_b, jnp.zeros_like(xdt_b)))
        y_diag = jnp.dot(jnp.concatenate(ms, axis=-1), jnp.concatenate(xm, axis=0),
                         preferred_element_type=F32)
        a_x = jnp.concatenate([jnp.where(low_half, cols[0], cols[1]),
                               jnp.where(low_half, cols[2], cols[3])], axis=-1)
        e_acs_x = jnp.exp2(a_x)
        a_last_x = a_x[chunk - 1:chunk, :]
        st = state[g]
        y_off = jnp.dot(cm, st.astype(BF16), preferred_element_type=F32) * e_acs_x
        xdec = (xdt_g * jnp.exp2(a_last_x - a_x)).astype(BF16)
        state[g] = st * e_acs_x[chunk - 1:chunk, :] + _tn_dot(bm, xdec)
        y = y_diag + y_off + dsk_ref[:, g * gw:(g + 1) * gw] * xs_g
        yz = y * z_ref[:, g * gw:(g + 1) * gw].astype(F32)
        yn = yz * lax.rsqrt(jnp.mean(yz * yz, axis=-1, keepdims=True) + EPS) * nw_ref[:, g * gw:(g + 1) * gw]
        yn_s[:, g * gw:(g + 1) * gw] = yn.astype(BF16)


def _mixers_kernel(q_ref, k_ref, v_ref, g_ref, din_ref, dq_ref, dk_ref, wret_ref,
                   z_ref, xbc_ref, dt_ref, dtb_ref, alog_ref, dsk_ref, nw_ref, tril_ref, exp_ref, wssm_ref,
                   ga_ref, gb_ref, x_ref, gm_ref, scf_ref, shf_ref, nf_ref, wo_ref, wr_ref, br_ref, tri_ref,
                   x1_ref, h2_ref, idx_ref, rank_ref, prow_ref, cnt_ref,
                   rstate, sstate, yn_s, ya_s, yb_s, cnt_s, *, decay_c, chunk, ssm_chunk, d_inner):
    b = pl.program_id(0)
    c = pl.program_id(1)

    @pl.when(c == 0)
    def _():
        rstate[...] = jnp.zeros_like(rstate)
        sstate[...] = jnp.zeros_like(sstate)

    @pl.when((b == 0) & (c == 0))
    def _():
        cnt_s[...] = jnp.zeros_like(cnt_s)

    n_sub = chunk // ssm_chunk
    acs, dt_x = _ssd_decays(dt_ref, dtb_ref, alog_ref, tril_ref, exp_ref, n_sub, ssm_chunk)
    _retention_body(q_ref, k_ref, v_ref, g_ref, din_ref, dq_ref, dk_ref, wret_ref, ya_s, rstate, decay_c)
    for sub in range(n_sub):
        rows = pl.ds(sub * ssm_chunk, ssm_chunk)
        _ssd_body(z_ref.at[rows, :], xbc_ref.at[rows, :], acs[sub],
                  dt_x[sub * ssm_chunk:(sub + 1) * ssm_chunk, :], dsk_ref, nw_ref, yn_s.at[rows, :],
                  sstate, ssm_chunk, d_inner)
    yb_s[...] = jnp.dot(yn_s[...], wssm_ref[...], preferred_element_type=F32)
    _merge_body(ya_s, yb_s, ga_ref, gb_ref, x_ref, gm_ref, scf_ref, shf_ref, nf_ref, wo_ref, wr_ref, br_ref,
                tri_ref, x1_ref, h2_ref, idx_ref, rank_ref, prow_ref, cnt_ref, cnt_s, chunk)


def _mixers_call(proj, dt_raw, x2, row0, mod3, w_ret, dt_bias, a_log, d_skip, ssm_norm, w_ssm,
                 norm_ffn, w_out, w_router_t, b_router, bsz, seq, chunk, ssm_chunk):
    t = proj.shape[0]
    d_inner, d = w_ssm.shape
    conv_dim = d_inner + 2 * SSM_GROUPS * SSM_STATE
    n_heads = d_inner // SSM_HEAD_DIM
    nc = seq // chunk
    gw = d_inner // SSM_GROUPS
    qk_w = RET_HEADS * RET_QK_DIM
    v_w = RET_HEADS * RET_V_DIM
    assert chunk % ssm_chunk == 0
    assert ssm_chunk == LANES, "the per-head decay tiles are built lane-for-lane against the chunk"
    din, dq, dk, dc = _retention_tables(chunk)
    pad_h = lambda v: jnp.pad(v.astype(F32), (0, LANES - n_heads)).reshape(1, LANES)
    tril = jnp.asarray(np.tril(np.ones((ssm_chunk, ssm_chunk), np.float32)), BF16)
    expand = np.zeros((LANES, d_inner), np.float32)
    for h in range(n_heads):
        expand[h, h * SSM_HEAD_DIM:(h + 1) * SSM_HEAD_DIM] = 1.0
    expand = jnp.asarray(expand, BF16)
    kern = functools.partial(_mixers_kernel, decay_c=dc, chunk=chunk, ssm_chunk=ssm_chunk, d_inner=d_inner)
    row = lambda b, c: b * nc + c
    z_blk = (2 * qk_w + 2 * v_w) // d_inner
    xbc_blk = (2 * qk_w + 2 * v_w + d_inner) // conv_dim
    full = lambda shape: pl.BlockSpec(shape, lambda b, c: (0,) * len(shape))
    nt = t // chunk
    ga_blk = proj.shape[1] // d - 2
    seq0 = row0 // seq
    modspec = lambda m: pl.BlockSpec((1, 1, d), lambda b, c: ((b + seq0) * N_MOD + m, 0, 0))
    tri = jnp.asarray(np.triu(np.ones((chunk, chunk), np.float32), 1), BF16)
    return pl.pallas_call(
        kern,
        out_shape=(jax.ShapeDtypeStruct((t, d), F32), jax.ShapeDtypeStruct((t * SLAB_ROWS, LANES), jnp.uint32),
                   jax.ShapeDtypeStruct((nt, SUBLANES, chunk), jnp.int32),
                   jax.ShapeDtypeStruct((nt, SUBLANES, chunk), jnp.int32),
                   jax.ShapeDtypeStruct((t, LANES), F32),
                   jax.ShapeDtypeStruct((N_EXPERTS, LANES), jnp.int32)),
        grid=(bsz, nc),
        in_specs=[
            pl.BlockSpec((chunk, qk_w), lambda b, c: (row(b, c), 0)),
            pl.BlockSpec((chunk, qk_w), lambda b, c: (row(b, c), 1)),
            pl.BlockSpec((chunk, v_w), lambda b, c: (row(b, c), 1)),
            pl.BlockSpec((chunk, v_w), lambda b, c: (row(b, c), 2)),
            full((RET_HEADS, chunk, chunk)), full((RET_HEADS, chunk, 1)), full((RET_HEADS, chunk, 1)),
            full((v_w, d)),
            pl.BlockSpec((chunk, d_inner), lambda b, c: (row(b, c), z_blk)),
            pl.BlockSpec((chunk, conv_dim), lambda b, c: (row(b, c), xbc_blk)),
            pl.BlockSpec((chunk, LANES), lambda b, c: (row(b, c), 0)),
            full((1, LANES)), full((1, LANES)),
            full((1, d_inner)), full((1, d_inner)), full((ssm_chunk, ssm_chunk)), full((LANES, d_inner)),
            full((d_inner, d)),
            pl.BlockSpec((chunk, d), lambda b, c: (row(b, c), ga_blk)),
            pl.BlockSpec((chunk, d), lambda b, c: (row(b, c), ga_blk + 1)),
            pl.BlockSpec((chunk, d), lambda b, c: (row0 // chunk + row(b, c), 0)),
            modspec(2), modspec(4), modspec(3),
            full((1, d)), full((d, d)), full((2 * N_EXPERTS, d)), full((N_EXPERTS, 1)), full((chunk, chunk)),
        ],
        out_specs=(pl.BlockSpec((chunk, d), lambda b, c: (row(b, c), 0)),
                   pl.BlockSpec((chunk * SLAB_ROWS, LANES), lambda b, c: (row(b, c), 0)),
                   pl.BlockSpec((1, SUBLANES, chunk), lambda b, c: (row(b, c), 0, 0)),
                   pl.BlockSpec((1, SUBLANES, chunk), lambda b, c: (row(b, c), 0, 0)),
                   pl.BlockSpec((chunk, LANES), lambda b, c: (row(b, c), 0)),
                   full((N_EXPERTS, LANES))),
        scratch_shapes=[pltpu.VMEM((RET_HEADS, RET_QK_DIM, RET_V_DIM), F32),
                        pltpu.VMEM((SSM_GROUPS, SSM_STATE, gw), F32),
                        pltpu.VMEM((chunk, d_inner), BF16),
                        pltpu.VMEM((chunk, d), F32), pltpu.VMEM((chunk, d), F32),
                        pltpu.VMEM((N_EXPERTS, LANES), F32)],
        compiler_params=_params(("arbitrary", "arbitrary")),
        name="mixers",
    )(proj, proj, proj, proj, din, dq, dk, w_ret,
      proj, proj, dt_raw, pad_h(dt_bias), pad_h(a_log),
      jnp.repeat(d_skip.astype(F32), SSM_HEAD_DIM).reshape(1, d_inner), ssm_norm.reshape(1, d_inner),
      tril, expand, w_ssm,
      proj, proj, x2, mod3, mod3, mod3, norm_ffn, w_out, w_router_t, b_router.reshape(N_EXPERTS, 1), tri)


def _merge_body(ya_ref, yb_ref, ga_ref, gb_ref, x_ref, gm_ref, scf_ref, shf_ref, nw_ref, wo_ref,
                wr_ref, br_ref, tri_ref,
                x1_ref, h2_ref, idx_ref, rank_ref, prow_ref, cnt_ref, cnt_s, tm):
    merged = (ga_ref[...].astype(F32) * ya_ref[...].astype(F32)
              + gb_ref[...].astype(F32) * yb_ref[...].astype(F32))
    mo = jnp.dot(merged.astype(BF16), wo_ref[...], preferred_element_type=F32)
    x1 = x_ref[...] + gm_ref[0] * mo
    x1_ref[...] = x1
    ms = jnp.mean(x1 * x1, axis=-1, keepdims=True)
    h2 = x1 * lax.rsqrt(ms + EPS) * nw_ref[...] * (1.0 + scf_ref[0]) + shf_ref[0]
    _store_slabs(h2_ref, h2, tm)

    h_hi = h2.astype(BF16)
    h_lo = (h2 - h_hi.astype(F32)).astype(BF16)
    lg2 = _nt_dot(wr_ref[...], h_hi)
    lg = lg2[:N_EXPERTS] + lg2[N_EXPERTS:] + _nt_dot(wr_ref[:N_EXPERTS, :], h_lo) + br_ref[...]
    sub = lax.broadcasted_iota(jnp.int32, lg.shape, 0)
    work = lg
    vals, idxs, sels = [], [], []
    for _ in range(TOP_K):
        m = jnp.max(work, axis=0, keepdims=True)
        ik = jnp.min(jnp.where(work == m, sub, N_EXPERTS), axis=0, keepdims=True)
        sel = sub == ik
        vals.append(m)
        idxs.append(ik)
        sels.append(sel)
        work = jnp.where(sel, -jnp.inf, work)
    exps = [jnp.exp(v - vals[0]) for v in vals]
    denom = exps[0]
    for e in exps[1:]:
        denom = denom + e
    probs = [e / denom for e in exps]

    base = cnt_s[:, 0:1]
    ranks = []
    for k in range(TOP_K):
        mk = jnp.where(sels[k], 1.0, 0.0)
        pre = jnp.dot(mk.astype(BF16), tri_ref[...], preferred_element_type=F32)
        ranks.append(jnp.sum(jnp.where(sels[k], pre + base, 0.0), axis=0, keepdims=True))
        base = base + jnp.sum(mk, axis=1, keepdims=True)
    cnt_s[...] = jnp.broadcast_to(base, cnt_s.shape)
    cnt_ref[...] = cnt_s[...].astype(jnp.int32)

    zi = jnp.zeros((SUBLANES - TOP_K, tm), jnp.int32)
    idx_ref[0] = jnp.concatenate(idxs + [zi], axis=0)
    rank_ref[0] = jnp.concatenate([r.astype(jnp.int32) for r in ranks] + [zi], axis=0)
    pt = jnp.concatenate(probs + [jnp.zeros((LANES - TOP_K, tm), F32)], axis=0)
    prow_ref[...] = pt.T


def _sc_mesh():
    return plsc.VectorSubcoreMesh(core_axis_name="c", subcore_axis_name="s")


def _sc_worker():
    return lax.axis_index("s") * SC_CORES + lax.axis_index("c")


def _sc_scatter_rows(rows, dest, n_out):
    t = rows.shape[0]
    n_k = dest.shape[0]
    g = SC_GROUP
    n_w = SC_CORES * SC_SUBCORES
    assert t % (n_w * g) == 0
    cpw = t // (n_w * g)
    dest_w = dest.reshape(n_k, n_w, cpw, g).transpose(1, 0, 2, 3)

    @functools.partial(
        pl.kernel, mesh=_sc_mesh(),
        out_type=jax.ShapeDtypeStruct((n_out,) + rows.shape[1:], rows.dtype),
        scratch_types=[pltpu.VMEM((n_k, cpw, g), jnp.int32),
                       pltpu.VMEM((g,) + rows.shape[1:], rows.dtype),
                       pltpu.SemaphoreType.DMA],
    )
    def scatter(rows_hbm, dest_hbm, out_hbm, idx_v, rows_v, sem):
        wid = _sc_worker()
        pltpu.sync_copy(dest_hbm.at[wid], idx_v)

        @pl.loop(0, cpw)
        def _(cc):
            r0 = pl.multiple_of((wid * cpw + cc) * g, g)
            pltpu.sync_copy(rows_hbm.at[pl.ds(r0, g)], rows_v)
            copies = [pltpu.async_copy(rows_v, out_hbm.at[idx_v.at[k, cc]], sem) for k in range(n_k)]
            for cp in copies:
                cp.wait()

    return scatter(rows, dest_w)


def _sc_gather_rows(table, idx):
    m = idx.shape[0]
    g = SC_GROUP
    n_w = SC_CORES * SC_SUBCORES
    assert m % (n_w * g) == 0
    per_w = m // n_w

    @functools.partial(
        pl.kernel, mesh=_sc_mesh(),
        out_type=jax.ShapeDtypeStruct((m,) + table.shape[1:], table.dtype),
        scratch_types=[pltpu.VMEM((per_w,), jnp.int32),
                       pltpu.VMEM((g,) + table.shape[1:], table.dtype),
                       pltpu.SemaphoreType.DMA],
    )
    def gather(table_hbm, idx_hbm, out_hbm, idx_v, rows_v, sem):
        base = _sc_worker() * per_w
        pltpu.sync_copy(idx_hbm.at[pl.ds(base, per_w)], idx_v)

        @pl.loop(0, per_w // g)
        def _(cc):
            off = pl.multiple_of(cc * g, g)
            pltpu.async_copy(table_hbm.at[idx_v.at[pl.ds(off, g)]], rows_v, sem).wait()
            pltpu.sync_copy(rows_v, out_hbm.at[pl.ds(base + off, g)])

    return gather(table, idx)


def _ffn_kernel(be_ref, br_ref, bv_ref, first_ref, slot_ref, next_ref, rows_ref,
                x_ref, wgu_hbm, bgu_ref, wd_hbm, bd_ref, o_ref, wgu_f, wd_f, wgu_s, wd_s, sem, *, bm, d_ff):
    i = pl.program_id(0)

    def fetch(e, slot):
        return (pltpu.make_async_copy(wgu_hbm.at[e], wgu_f.at[slot], sem.at[0, slot]),
                pltpu.make_async_copy(wd_hbm.at[e], wd_f.at[slot], sem.at[1, slot]))

    @pl.when(i == 0)
    def _():
        for cp in fetch(be_ref[0], slot_ref[0]):
            cp.start()

    @pl.when(first_ref[i] == 1)
    def _():
        slot = slot_ref[i]
        for cp in fetch(be_ref[i], slot):
            cp.wait()
        wgu_s[...] = wgu_f[slot].astype(BF16)
        wd_s[...] = wd_f[slot].astype(BF16)

        @pl.when(next_ref[i] >= 0)
        def _():
            for cp in fetch(next_ref[i], 1 - slot):
                cp.start()

    def expert_mlp(n, keep):
        x = jnp.concatenate(_load_slabs(x_ref, n, keep=keep), axis=-1).astype(BF16)
        gu = jnp.dot(x, wgu_s[...], preferred_element_type=F32) + bgu_ref[0]
        gate = jnp.minimum(gu[:, :d_ff], SWIGLU_LIMIT)
        up = jnp.clip(gu[:, d_ff:], -SWIGLU_LIMIT, SWIGLU_LIMIT)
        act = gate * jax.nn.sigmoid(SWIGLU_ALPHA * gate) * (up + 1.0)
        y = jnp.dot(act.astype(BF16), wd_s[...], preferred_element_type=F32) + bd_ref[0]
        _store_slabs(o_ref, y, n)
        if n < bm:
            o_ref[pl.ds(n * SLAB_ROWS, (bm - n) * SLAB_ROWS), :] = jnp.zeros(
                ((bm - n) * SLAB_ROWS, LANES), jnp.uint32)

    @pl.when((bv_ref[i] == 1) & (rows_ref[i] == bm))
    def _():
        expert_mlp(bm, None)

    for q in range(1, bm // FFN_TAIL + 1):
        n = q * FFN_TAIL
        lo = (q - 1) * FFN_TAIL if q > 1 else -1

        @pl.when((bv_ref[i] == 1) & (rows_ref[i] < bm) & (rows_ref[i] > lo) & (rows_ref[i] <= n))
        def _(n=n):
            expert_mlp(n, lax.broadcasted_iota(jnp.int32, (n, LANES), 0) < rows_ref[i])

    @pl.when(bv_ref[i] == 0)
    def _():
        o_ref[...] = jnp.zeros_like(o_ref)


def _ffn_call(blk_e, blk_row, blk_valid, blk_rows, xs, w_gu, b_gu, w_d, b_d, bm):
    n_e, d, f2 = w_gu.shape
    d_ff = f2 // 2
    nb = blk_e.shape[0]
    first = jnp.concatenate([jnp.ones((1,), jnp.int32), (blk_e[1:] != blk_e[:-1]).astype(jnp.int32)])
    slot = (jnp.cumsum(first) - 1) % 2
    later = jnp.where(blk_e[None, :] > blk_e[:, None], blk_e[None, :], n_e)
    nxt = jnp.min(later, axis=1)
    nxt = jnp.where(nxt == n_e, -1, nxt)
    kern = functools.partial(_ffn_kernel, bm=bm, d_ff=d_ff)
    imap = lambda f: (lambda i, be, br, bv, fi, sl, nx, rw: f(i, be, br))
    gs = pltpu.PrefetchScalarGridSpec(
        num_scalar_prefetch=7,
        grid=(nb,),
        in_specs=[pl.BlockSpec((bm * SLAB_ROWS, LANES), imap(lambda i, be, br: (br[i], 0))),
                  pl.BlockSpec(memory_space=pl.ANY),
                  pl.BlockSpec((1, 1, f2), imap(lambda i, be, br: (be[i], 0, 0))),
                  pl.BlockSpec(memory_space=pl.ANY),
                  pl.BlockSpec((1, 1, d), imap(lambda i, be, br: (be[i], 0, 0)))],
        out_specs=pl.BlockSpec((bm * SLAB_ROWS, LANES), imap(lambda i, be, br: (i, 0))),
        scratch_shapes=[pltpu.VMEM((2, d, f2), F32), pltpu.VMEM((2, d_ff, d), F32),
                        pltpu.VMEM((d, f2), BF16), pltpu.VMEM((d_ff, d), BF16),
                        pltpu.SemaphoreType.DMA((2, 2))],
    )
    return pl.pallas_call(
        kern,
        out_shape=jax.ShapeDtypeStruct((nb * bm * SLAB_ROWS, LANES), jnp.uint32),
        grid_spec=gs,
        compiler_params=_params(("arbitrary",)),
        name="ffn",
    )(blk_e, blk_row, blk_valid, first, slot.astype(jnp.int32), nxt.astype(jnp.int32), blk_rows,
      xs, w_gu, b_gu.reshape(n_e, 1, f2), w_d, b_d.reshape(n_e, 1, d))


def _combine_kernel(y0_ref, y1_ref, y2_ref, y3_ref, prow_ref, x1_ref, gf_ref, nw_ref, *rest, tm):
    o_ref = rest[-1]
    p = prow_ref[...]
    pieces = [_load_slabs(y_ref, tm) for y_ref in (y0_ref, y1_ref, y2_ref, y3_ref)]
    for s in range(len(pieces[0])):
        moe = None
        for k in range(TOP_K):
            piece = pieces[k][s] * p[:, k:k + 1]
            moe = piece if moe is None else moe + piece
        sl = slice(s * LANES, (s + 1) * LANES)
        o_ref[:, sl] = x1_ref[:, sl] + gf_ref[0][:, sl] * moe
    xo = o_ref[...]
    o_ref[...] = xo * lax.rsqrt(jnp.mean(xo * xo, axis=-1, keepdims=True) + EPS) * nw_ref[...]


def _combine_call(ytok, prow, x1, mod3, norm_final, seq, tm, row0, t_total, out_prev):
    t, d = x1.shape
    nt = t // tm
    tiles_per_seq = seq // tm
    off = row0 // tm
    kern = functools.partial(_combine_kernel, tm=tm)
    yspec = lambda k: pl.BlockSpec((tm * SLAB_ROWS, LANES), lambda i: (k * nt + i, 0))
    in_specs = [yspec(0), yspec(1), yspec(2), yspec(3),
                pl.BlockSpec((tm, LANES), lambda i: (i, 0)),
                pl.BlockSpec((tm, d), lambda i: (i, 0)),
                pl.BlockSpec((1, 1, d), lambda i: (((i + off) // tiles_per_seq) * N_MOD + 5, 0, 0)),
                pl.BlockSpec((1, d), lambda i: (0, 0))]
    args = [ytok, ytok, ytok, ytok, prow, x1, mod3, norm_final]
    aliases = {}
    if out_prev is not None:
        in_specs.append(pl.BlockSpec(memory_space=pl.ANY))
        aliases = {len(args): 0}
        args.append(out_prev)
    return pl.pallas_call(
        kern,
        out_shape=jax.ShapeDtypeStruct((t_total, d), F32),
        grid=(nt,),
        in_specs=in_specs,
        out_specs=pl.BlockSpec((tm, d), lambda i: (i + off, 0)),
        input_output_aliases=aliases,
        compiler_params=_params(("arbitrary",)),
        name="combine",
    )(*args)


def _plan(seq):
    def fit(pref):
        tm = min(pref, seq)
        assert seq % tm == 0
        return tm
    return dict(tm_in=fit(1024), tm_moe=fit(1024), ret_chunk=fit(RET_CHUNK), ssm_chunk=fit(SSM_CHUNK))


def _layer(x2, mod3, bsz, seq, norm_mix, norm_ffn, w_in, conv_w, conv_b, dt_bias, a_log, d_skip, ssm_norm,
           w_ret_out, w_ssm_out, w_out, w_router, b_router, w_gate_up, b_gate_up, w_down, b_down,
           norm_final):
    t, d = x2.shape
    plan = _plan(seq)
    qk_w = RET_HEADS * RET_QK_DIM
    v_w = RET_HEADS * RET_V_DIM
    d_inner = w_ssm_out.shape[0]
    conv_dim = conv_w.shape[1]
    n_heads = d_inner // SSM_HEAD_DIM
    dt_off = 2 * qk_w + 2 * v_w + d_inner + conv_dim

    w_all = w_in.astype(BF16)
    w_gate = w_in[:, dt_off + n_heads:].astype(BF16)
    w_dt = jnp.pad(w_in[:, dt_off:dt_off + n_heads], ((0, 0), (0, LANES - n_heads)))
    w_dt_hi = w_dt.astype(BF16)
    w_dt = jnp.concatenate([w_dt_hi, (w_dt - w_dt_hi.astype(F32)).astype(BF16)], axis=1)
    half = RET_QK_DIM // 2
    inv_freq = ROPE_BASE ** (-jnp.arange(half, dtype=F32) / half)
    ang = jnp.arange(seq, dtype=F32)[:, None] * inv_freq[None, :]
    cos, sin = jnp.cos(ang), jnp.sin(ang)

    w_ret_b, w_ssm_b, w_out_b = w_ret_out.astype(BF16), w_ssm_out.astype(BF16), w_out.astype(BF16)
    w_r_hi = w_router.T.astype(BF16)
    w_router_t = jnp.concatenate([w_r_hi, (w_router.T - w_r_hi.astype(F32)).astype(BF16)], axis=0)
    bm = FFN_BLOCK
    slab = (SLAB_ROWS, LANES)

    def mixer(row0, tg, bg):
        proj, dt_raw = _inproj_call(x2, row0, tg, norm_mix.reshape(1, d), mod3, cos, sin, w_all, w_gate, w_dt,
                                    conv_w, conv_b, 2 * qk_w + 2 * v_w + d_inner, seq, plan["tm_in"], 2 * qk_w)
        x1, h2, idx, rank, prow, cnt = _mixers_call(
            proj, dt_raw, x2, row0, mod3, w_ret_b, dt_bias, a_log, d_skip, ssm_norm, w_ssm_b,
            norm_ffn.reshape(1, d), w_out_b, w_router_t, b_router, bg, seq, plan["ret_chunk"], plan["ssm_chunk"])
        counts = cnt[:, 0]
        padded = ((counts + bm - 1) // bm) * bm
        pad_end = jnp.cumsum(padded)
        start_pad = pad_end - padded
        n_blocks = -(-(tg * TOP_K) // bm) + N_EXPERTS
        e_ids = jnp.arange(N_EXPERTS, dtype=jnp.int32)[:, None, None, None]
        dest = rank + jnp.sum(jnp.where(idx[None] == e_ids, start_pad[:, None, None, None], 0), axis=0)
        dest = dest[:, :TOP_K, :].transpose(1, 0, 2).reshape(TOP_K, tg).astype(jnp.int32)
        n_real = pad_end[-1] // bm
        blk_valid = (jnp.arange(n_blocks) < n_real).astype(jnp.int32)
        blk_row = jnp.minimum(jnp.arange(n_blocks), n_real - 1).astype(jnp.int32)
        blk_e = jnp.minimum(jnp.sum(pad_end[None, :] <= (blk_row * bm)[:, None], axis=1),
                            N_EXPERTS - 1).astype(jnp.int32)
        sel_e = blk_e[:, None] == jnp.arange(N_EXPERTS, dtype=jnp.int32)[None, :]
        row_end = jnp.sum(jnp.where(sel_e, (start_pad + counts)[None, :], 0), axis=1)
        blk_rows = jnp.clip(row_end - blk_row * bm, 0, bm).astype(jnp.int32)
        xs = _sc_scatter_rows(h2.reshape((tg,) + slab), dest, n_blocks * bm)
        return dict(x1=x1, prow=prow, dest=dest, blocks=(blk_e, blk_row, blk_valid, blk_rows), xs=xs,
                    row0=row0)

    def experts(m):
        n_rows = m["xs"].shape[0]
        ys = _ffn_call(*m["blocks"], m["xs"].reshape(n_rows * SLAB_ROWS, LANES),
                       w_gate_up, b_gate_up, w_down, b_down, bm)
        return _sc_gather_rows(ys.reshape((n_rows,) + slab), m["dest"].reshape(-1))

    n_groups = N_GROUPS if bsz % N_GROUPS == 0 else 1
    bg = bsz // n_groups
    tg = bg * seq
    groups = [mixer(g * tg, tg, bg) for g in range(n_groups)]
    ytoks = [experts(m) for m in groups]
    out = None
    for y, m in zip(ytoks, groups):
        out = _combine_call(y.reshape(TOP_K * tg * SLAB_ROWS, LANES), m["prow"], m["x1"], mod3,
                            norm_final.reshape(1, d), seq, plan["tm_moe"], m["row0"], t, out)
    return out


def kernel(x, c, w_ada, b_ada, norm_mix, norm_ffn, w_in, conv_w, conv_b, dt_bias, a_log, d_skip, ssm_norm,
           w_ret_out, w_ssm_out, w_out, w_router, b_router, w_gate_up, b_gate_up, w_down, b_down, norm_final):
    bsz, seq, d = x.shape
    depth = w_ada.shape[0]
    assert depth == 1, "the final norm is fused into the single layer's last kernel"
    x2 = x.reshape(bsz * seq, d)
    l = 0
    mod = _mod_call(c, w_ada[l], b_ada[l])
    mod3 = mod.reshape(bsz * N_MOD, 1, d)
    out = _layer(x2, mod3, bsz, seq, norm_mix[l], norm_ffn[l], w_in[l], conv_w[l], conv_b[l], dt_bias[l],
                 a_log[l], d_skip[l], ssm_norm[l], w_ret_out[l], w_ssm_out[l], w_out[l], w_router[l],
                 b_router[l], w_gate_up[l], b_gate_up[l], w_down[l], b_down[l], norm_final)
    return out.reshape(bsz, seq, d)
```
